```python
import jax, jax.numpy as jnp
from jax import lax
import numpy as np

D_MODEL = 1024
BATCH = 8
SEQ = 8192
DEPTH = 4

D_MIX = D_MODEL
MLA_V = 128
MLA_NOPE = 128
MLA_ROPE = 64
MLA_WIDTH = D_MIX // 2
MLA_HEADS = MLA_WIDTH // MLA_V
MLA_Q_RANK = 384
MLA_KV_RANK = 256
GDN_DK = 128
GDN_DV = 128
GDN_WIDTH = D_MIX - MLA_WIDTH
GDN_HEADS = GDN_WIDTH // GDN_DV
GDN_CONV = 4
GDN_CHUNK = 64
Q_BLOCK = 128
ROPE_THETA = 10000.0
NORM_EPS = 1e-6
GDN_QKV = 2 * GDN_HEADS * GDN_DK + GDN_HEADS * GDN_DV

IN_SIZES = (MLA_Q_RANK, MLA_KV_RANK, MLA_ROPE, MLA_WIDTH, GDN_QKV, GDN_HEADS, GDN_HEADS, GDN_WIDTH)
IN_COLS = MLA_Q_RANK + MLA_KV_RANK + MLA_ROPE + MLA_WIDTH + GDN_QKV + GDN_HEADS + GDN_HEADS + GDN_WIDTH
IN_SPLIT_POINTS = tuple(int(s) for s in np.cumsum(IN_SIZES)[:-1])

kernel_name = "hybrid_mla_gdn_parallel_heads"


def rmsnorm(x, w):
    xf = x.astype(jnp.float32)
    y = xf * lax.rsqrt(jnp.mean(xf * xf, axis=-1, keepdims=True) + NORM_EPS)
    return (y * w.astype(jnp.float32)).astype(x.dtype)


def l2norm(x):
    return x * lax.rsqrt(jnp.sum(x * x, axis=-1, keepdims=True) + NORM_EPS)


def apply_rope(x, pos):
    r = x.shape[-1]
    half = r // 2
    inv_freq = jnp.power(ROPE_THETA, -jnp.arange(half, dtype=jnp.float32) * 2.0 / r)
    ang = pos.astype(jnp.float32)[:, :, None, None] * inv_freq
    cos, sin = jnp.cos(ang), jnp.sin(ang)
    xf = x.astype(jnp.float32)
    x1, x2 = xf[..., :half], xf[..., half:]
    return jnp.concatenate([x1 * cos - x2 * sin, x2 * cos + x1 * sin], axis=-1).astype(x.dtype)


def causal_attention(q, k, v, scale):
    b, s, h, dq = q.shape
    dv = v.shape[-1]
    nb = s // Q_BLOCK
    qb = q.reshape(b, nb, Q_BLOCK, h, dq).transpose(1, 0, 2, 3, 4)
    key_pos = jnp.arange(s)

    def one_block(args):
        i, q_blk = args
        sc = jnp.einsum('bqhd,bkhd->bhqk', q_blk, k, preferred_element_type=jnp.float32) * scale
        q_pos = i * Q_BLOCK + jnp.arange(Q_BLOCK)
        sc = jnp.where(key_pos[None, :] <= q_pos[:, None], sc, -1e30)
        p = jax.nn.softmax(sc, axis=-1)
        return jnp.einsum('bhqk,bkhd->bqhd', p.astype(v.dtype), v)

    o = lax.map(one_block, (jnp.arange(nb), qb))
    return o.transpose(1, 0, 2, 3, 4).reshape(b, s, h, dv)


def mla_branch(q_lat, kv_lat, k_pe_raw, z, pos, q_norm_w, q_up, kv_norm_w, kv_up):
    b, s, _ = q_lat.shape
    q = (rmsnorm(q_lat, q_norm_w) @ q_up).reshape(b, s, MLA_HEADS, MLA_NOPE + MLA_ROPE)
    q = jnp.concatenate([q[..., :MLA_NOPE], apply_rope(q[..., MLA_NOPE:], pos)], axis=-1)
    kv = (rmsnorm(kv_lat, kv_norm_w) @ kv_up).reshape(b, s, MLA_HEADS, MLA_NOPE + MLA_V)
    k_nope, v = kv[..., :MLA_NOPE], kv[..., MLA_NOPE:]
    k_pe = apply_rope(k_pe_raw[:, :, None, :], pos)
    k = jnp.concatenate([k_nope, jnp.broadcast_to(k_pe, (b, s, MLA_HEADS, MLA_ROPE))], axis=-1)
    o = causal_attention(q, k, v, (MLA_NOPE + MLA_ROPE) ** -0.5)
    return o.reshape(b, s, MLA_WIDTH) * jax.nn.silu(z)


def causal_conv(x, w):
    kw = w.shape[0]
    return lax.conv_general_dilated(x, w[:, None, :], window_strides=(1,), padding=[(kw - 1, 0)],
                                    dimension_numbers=('NWC', 'WIO', 'NWC'),
                                    feature_group_count=x.shape[-1])


def chunk_gated_delta_rule(q, k, v, g, beta):
    b, s, h, dk = q.shape
    dv = v.shape[-1]
    c = GDN_CHUNK
    n = s // c

    def chunks(t):
        t = t.reshape((b, n, c, h) + t.shape[3:])
        return jnp.moveaxis(t, 3, 1)

    q, k, v, g, beta = chunks(q), chunks(k), chunks(v), chunks(g), chunks(beta)
    g = jnp.cumsum(g, axis=-1)
    incl = jnp.tril(jnp.ones((c, c), dtype=bool))
    strict = jnp.tril(jnp.ones((c, c), dtype=bool), -1)
    decay = jnp.exp(jnp.where(incl, g[..., :, None] - g[..., None, :], -jnp.inf))
    kb = k * beta[..., None]
    kk = jnp.einsum('bhnid,bhnjd->bhnij', kb, k)
    a_mat = jnp.eye(c, dtype=jnp.float32) + jnp.where(strict, kk * decay, 0.0)
    rhs = jnp.concatenate([v * beta[..., None], kb * jnp.exp(g)[..., None]], axis=-1)
    sol = lax.linalg.triangular_solve(a_mat, rhs, left_side=True, lower=True, unit_diagonal=True)
    u, w = sol[..., :dv], sol[..., dv:]
    qk = jnp.einsum('bhnid,bhnjd->bhnij', q, k) * decay
    q_dec = q * jnp.exp(g)[..., None]
    k_dec = k * jnp.exp(g[..., -1:] - g)[..., None]
    g_last = jnp.exp(g[..., -1])

    def step(state, inp):
        qd, kd, u_c, w_c, qk_c, gl = inp
        v_new = u_c - jnp.einsum('bhcd,bhde->bhce', w_c, state)
        o = jnp.einsum('bhcd,bhde->bhce', qd, state) + jnp.einsum('bhij,bhje->bhie', qk_c, v_new)
        state = state * gl[..., None, None] + jnp.einsum('bhcd,bhce->bhde', kd, v_new)
        return state, o

    xs = tuple(jnp.moveaxis(t, 2, 0) for t in (q_dec, k_dec, u, w, qk, g_last))
    _, o = lax.scan(step, jnp.zeros((b, h, dk, dv), jnp.float32), xs)
    return o.transpose(1, 0, 3, 2, 4).reshape(b, s, h, dv)


def gdn_branch(qkv, a, bt, z, conv_w, a_log, dt_bias, o_norm_w):
    bsz, s, _ = qkv.shape
    qkv = jax.nn.silu(causal_conv(qkv, conv_w)).astype(jnp.float32)
    nqk = GDN_HEADS * GDN_DK
    q = l2norm(qkv[..., :nqk].reshape(bsz, s, GDN_HEADS, GDN_DK)) * (GDN_DK ** -0.5)
    k = l2norm(qkv[..., nqk:2 * nqk].reshape(bsz, s, GDN_HEADS, GDN_DK))
    v = qkv[..., 2 * nqk:].reshape(bsz, s, GDN_HEADS, GDN_DV)
    beta = jax.nn.sigmoid(bt.astype(jnp.float32))
    g = -jnp.exp(a_log.astype(jnp.float32)) * jax.nn.softplus(a.astype(jnp.float32) + dt_bias.astype(jnp.float32))
    o = chunk_gated_delta_rule(q, k, v, g, beta)
    o = rmsnorm(o, o_norm_w).astype(z.dtype).reshape(bsz, s, GDN_WIDTH)
    return o * jax.nn.silu(z)


def _fwd_setup_inputs(seed: int = 0) -> dict:
    key = jax.random.key(seed)
    ks = jax.random.split(key, 20)
    f32 = jnp.float32
    nrm = lambda k, shape, sc: jax.random.normal(k, shape, f32) * sc
    dt = jnp.exp(jax.random.uniform(ks[14], (DEPTH, GDN_HEADS), f32, np.log(1e-3), np.log(1e-1)))
    offsets = jax.random.randint(ks[2], (BATCH, 1), 0, 4096, dtype=jnp.int32)
    return {
        "x": nrm(ks[0], (BATCH, SEQ, D_MODEL), 1.0),
        "c": nrm(ks[1], (BATCH, D_MODEL), 1.0),
        "positions": offsets + jnp.arange(SEQ, dtype=jnp.int32)[None, :],
        "w_mod": nrm(ks[3], (DEPTH, D_MODEL, 3 * D_MODEL), 0.5 * D_MODEL ** -0.5),
        "b_mod": nrm(ks[4], (DEPTH, 3 * D_MODEL), 0.02),
        "pre_norm_w": 1.0 + nrm(ks[5], (DEPTH, D_MODEL), 0.05),
        "post_norm_w": 1.0 + nrm(ks[6], (DEPTH, D_MODEL), 0.05),
        "w_in": nrm(ks[7], (DEPTH, D_MODEL, IN_COLS), D_MODEL ** -0.5),
        "mla_q_norm_w": 1.0 + nrm(ks[8], (DEPTH, MLA_Q_RANK), 0.05),
        "mla_q_up": nrm(ks[9], (DEPTH, MLA_Q_RANK, MLA_HEADS * (MLA_NOPE + MLA_ROPE)), MLA_Q_RANK ** -0.5),
        "mla_kv_norm_w": 1.0 + nrm(ks[10], (DEPTH, MLA_KV_RANK), 0.05),
        "mla_kv_up": nrm(ks[11], (DEPTH, MLA_KV_RANK, MLA_HEADS * (MLA_NOPE + MLA_V)), MLA_KV_RANK ** -0.5),
        "gdn_conv_w": nrm(ks[12], (DEPTH, GDN_CONV, GDN_QKV), 0.5),
        "gdn_a_log": jnp.log(jax.random.uniform(ks[13], (DEPTH, GDN_HEADS), f32, 1.0, 16.0)),
        "gdn_dt_bias": dt + jnp.log(-jnp.expm1(-dt)),
        "gdn_o_norm_w": 1.0 + nrm(ks[15], (DEPTH, GDN_DV), 0.05),
        "w_out": nrm(ks[16], (DEPTH, D_MIX, D_MODEL), D_MIX ** -0.5),
    }


def _fwd_reference(x, c, positions, w_mod, b_mod, pre_norm_w, post_norm_w, w_in, mla_q_norm_w, mla_q_up,
              mla_kv_norm_w, mla_kv_up, gdn_conv_w, gdn_a_log, gdn_dt_bias, gdn_o_norm_w, w_out):
    c_act = jax.nn.silu(c)
    for l in range(DEPTH):
        mod = c_act @ w_mod[l] + b_mod[l]
        shift, scale, gate = jnp.split(mod, 3, axis=-1)
        h = rmsnorm(x, pre_norm_w[l]) * (1.0 + scale[:, None, :]) + shift[:, None, :]
        proj = h @ w_in[l]
        q_lat, kv_lat, k_pe, z_mla, qkv, a, bt, z_gdn = jnp.split(proj, IN_SPLIT_POINTS, axis=-1)
        y_mla = mla_branch(q_lat, kv_lat, k_pe, z_mla, positions, mla_q_norm_w[l], mla_q_up[l],
                           mla_kv_norm_w[l], mla_kv_up[l])
        y_gdn = gdn_branch(qkv, a, bt, z_gdn, gdn_conv_w[l], gdn_a_log[l], gdn_dt_bias[l],
                           gdn_o_norm_w[l])
        y = jnp.concatenate([y_mla, y_gdn], axis=-1) @ w_out[l]
        x = x + gate[:, None, :] * rmsnorm(y, post_norm_w[l])
    return x


import jax as _jax
import jax.numpy as _jnp

TWIN_FORMAT = 'train_step'
FWD_PARAMS = ['x', 'c', 'positions', 'w_mod', 'b_mod', 'pre_norm_w', 'post_norm_w', 'w_in', 'mla_q_norm_w', 'mla_q_up', 'mla_kv_norm_w', 'mla_kv_up', 'gdn_conv_w', 'gdn_a_log', 'gdn_dt_bias', 'gdn_o_norm_w', 'w_out']
TWIN_WEIGHTS = ['w_mod', 'b_mod', 'pre_norm_w', 'post_norm_w', 'w_in', 'mla_q_norm_w', 'mla_q_up', 'mla_kv_norm_w', 'mla_kv_up', 'gdn_conv_w', 'gdn_a_log', 'gdn_dt_bias', 'gdn_o_norm_w', 'w_out']
TWIN_DIFF_INPUT = 'x'
TWIN_INPUTS = ['x', 'c', 'positions', 'w_mod', 'b_mod', 'pre_norm_w', 'post_norm_w', 'w_in', 'mla_q_norm_w', 'mla_q_up', 'mla_kv_norm_w', 'mla_kv_up', 'gdn_conv_w', 'gdn_a_log', 'gdn_dt_bias', 'gdn_o_norm_w', 'w_out', 'loss_target', 'm_w_mod', 'm_b_mod', 'm_pre_norm_w', 'm_post_norm_w', 'm_w_in', 'm_mla_q_norm_w', 'm_mla_q_up', 'm_mla_kv_norm_w', 'm_mla_kv_up', 'm_gdn_conv_w', 'm_gdn_a_log', 'm_gdn_dt_bias', 'm_gdn_o_norm_w', 'm_w_out', 'v_w_mod', 'v_b_mod', 'v_pre_norm_w', 'v_post_norm_w', 'v_w_in', 'v_mla_q_norm_w', 'v_mla_q_up', 'v_mla_kv_norm_w', 'v_mla_kv_up', 'v_gdn_conv_w', 'v_gdn_a_log', 'v_gdn_dt_bias', 'v_gdn_o_norm_w', 'v_w_out']
TWIN_OUTPUTS = ['loss', 'grad_x', 'grad_w_mod', 'grad_b_mod', 'grad_pre_norm_w', 'grad_post_norm_w', 'grad_w_in', 'grad_mla_q_norm_w', 'grad_mla_q_up', 'grad_mla_kv_norm_w', 'grad_mla_kv_up', 'grad_gdn_conv_w', 'grad_gdn_a_log', 'grad_gdn_dt_bias', 'grad_gdn_o_norm_w', 'grad_w_out', 'delta_w_mod', 'delta_b_mod', 'delta_pre_norm_w', 'delta_post_norm_w', 'delta_w_in', 'delta_mla_q_norm_w', 'delta_mla_q_up', 'delta_mla_kv_norm_w', 'delta_mla_kv_up', 'delta_gdn_conv_w', 'delta_gdn_a_log', 'delta_gdn_dt_bias', 'delta_gdn_o_norm_w', 'delta_w_out', 'new_m_w_mod', 'new_m_b_mod', 'new_m_pre_norm_w', 'new_m_post_norm_w', 'new_m_w_in', 'new_m_mla_q_norm_w', 'new_m_mla_q_up', 'new_m_mla_kv_norm_w', 'new_m_mla_kv_up', 'new_m_gdn_conv_w', 'new_m_gdn_a_log', 'new_m_gdn_dt_bias', 'new_m_gdn_o_norm_w', 'new_m_w_out', 'new_v_w_mod', 'new_v_b_mod', 'new_v_pre_norm_w', 'new_v_post_norm_w', 'new_v_w_in', 'new_v_mla_q_norm_w', 'new_v_mla_q_up', 'new_v_mla_kv_norm_w', 'new_v_mla_kv_up', 'new_v_gdn_conv_w', 'new_v_gdn_a_log', 'new_v_gdn_dt_bias', 'new_v_gdn_o_norm_w', 'new_v_w_out']
TWIN_LEAF_KINDS = {'loss': 'loss', 'grad_x': 'grad_x', 'grad_w_mod': 'grad_w', 'grad_b_mod': 'grad_w', 'grad_pre_norm_w': 'grad_w', 'grad_post_norm_w': 'grad_w', 'grad_w_in': 'grad_w', 'grad_mla_q_norm_w': 'grad_w', 'grad_mla_q_up': 'grad_w', 'grad_mla_kv_norm_w': 'grad_w', 'grad_mla_kv_up': 'grad_w', 'grad_gdn_conv_w': 'grad_w', 'grad_gdn_a_log': 'grad_w', 'grad_gdn_dt_bias': 'grad_w', 'grad_gdn_o_norm_w': 'grad_w', 'grad_w_out': 'grad_w', 'delta_w_mod': 'delta_w', 'delta_b_mod': 'delta_w', 'delta_pre_norm_w': 'delta_w', 'delta_post_norm_w': 'delta_w', 'delta_w_in': 'delta_w', 'delta_mla_q_norm_w': 'delta_w', 'delta_mla_q_up': 'delta_w', 'delta_mla_kv_norm_w': 'delta_w', 'delta_mla_kv_up': 'delta_w', 'delta_gdn_conv_w': 'delta_w', 'delta_gdn_a_log': 'delta_w', 'delta_gdn_dt_bias': 'delta_w', 'delta_gdn_o_norm_w': 'delta_w', 'delta_w_out': 'delta_w', 'new_m_w_mod': 'new_m', 'new_m_b_mod': 'new_m', 'new_m_pre_norm_w': 'new_m', 'new_m_post_norm_w': 'new_m', 'new_m_w_in': 'new_m', 'new_m_mla_q_norm_w': 'new_m', 'new_m_mla_q_up': 'new_m', 'new_m_mla_kv_norm_w': 'new_m', 'new_m_mla_kv_up': 'new_m', 'new_m_gdn_conv_w': 'new_m', 'new_m_gdn_a_log': 'new_m', 'new_m_gdn_dt_bias': 'new_m', 'new_m_gdn_o_norm_w': 'new_m', 'new_m_w_out': 'new_m', 'new_v_w_mod': 'new_v', 'new_v_b_mod': 'new_v', 'new_v_pre_norm_w': 'new_v', 'new_v_post_norm_w': 'new_v', 'new_v_w_in': 'new_v', 'new_v_mla_q_norm_w': 'new_v', 'new_v_mla_q_up': 'new_v', 'new_v_mla_kv_norm_w': 'new_v', 'new_v_mla_kv_up': 'new_v', 'new_v_gdn_conv_w': 'new_v', 'new_v_gdn_a_log': 'new_v', 'new_v_gdn_dt_bias': 'new_v', 'new_v_gdn_o_norm_w': 'new_v', 'new_v_w_out': 'new_v'}


def _forward(args):
    return _fwd_reference(*[args[k] for k in FWD_PARAMS])


def _output_shape():
    def fwd():
        inp = _fwd_setup_inputs(0)
        return _fwd_reference(*[inp[k] for k in FWD_PARAMS])
    out = _jax.eval_shape(fwd)
    return out.shape, out.dtype

N_MICROBATCH = 1
ADAM_LR = 0.001
ADAM_B1 = 0.9
ADAM_B2 = 0.999
ADAM_EPS = 1e-08
ADAM_WD = 0.01
ADAM_STEP = 10
PER_EXAMPLE_BATCH_AXIS = {'x': 0, 'c': 0, 'positions': 0, 'loss_target': 0}
SHARED_INPUTS = []
_WEIGHT_DTYPES = {'w_mod': _jnp.float32, 'b_mod': _jnp.float32, 'pre_norm_w': _jnp.float32, 'post_norm_w': _jnp.float32, 'w_in': _jnp.float32, 'mla_q_norm_w': _jnp.float32, 'mla_q_up': _jnp.float32, 'mla_kv_norm_w': _jnp.float32, 'mla_kv_up': _jnp.float32, 'gdn_conv_w': _jnp.float32, 'gdn_a_log': _jnp.float32, 'gdn_dt_bias': _jnp.float32, 'gdn_o_norm_w': _jnp.float32, 'w_out': _jnp.float32}
MOMENT_SCALE = {'w_mod': 2.540783e+00, 'b_mod': 5.541470e+00, 'pre_norm_w': 2.495615e-01, 'post_norm_w': 6.773324e+00, 'w_in': 2.040414e-01, 'mla_q_norm_w': 3.806031e-02, 'mla_q_up': 2.690052e-02, 'mla_kv_norm_w': 2.678697e-01, 'mla_kv_up': 1.428142e-01, 'gdn_conv_w': 2.459621e-01, 'gdn_a_log': 8.370424e-01, 'gdn_dt_bias': 7.793934e-01, 'gdn_o_norm_w': 1.036796e+00, 'w_out': 4.327759e-01}


def _to_microbatches(a, axis):
    t = _jnp.moveaxis(a, axis, 0)
    t = t.reshape((N_MICROBATCH, t.shape[0] // N_MICROBATCH) + t.shape[1:])
    return _jnp.moveaxis(t, 1, axis + 1)


def setup_inputs(seed: int = 0) -> dict:
    inp = _fwd_setup_inputs(seed)
    key = _jax.random.fold_in(_jax.random.key(seed), 7919)
    shape, _ = _output_shape()
    out = dict(inp)
    out["loss_target"] = _jax.random.normal(_jax.random.fold_in(key, 0), shape, _jnp.float32)
    for i, name in enumerate(TWIN_WEIGHTS):
        w = inp[name].astype(_jnp.float32)
        if MOMENT_SCALE is None:
            s = _jnp.sqrt(_jnp.mean(_jnp.square(w)) + 1e-30)
        else:
            s = MOMENT_SCALE[name]
        km, kv = _jax.random.split(_jax.random.fold_in(key, i + 1))
        out[name] = w
        out["m_" + name] = s * _jax.random.normal(km, w.shape, _jnp.float32)
        out["v_" + name] = (s * s) * _jax.random.uniform(kv, w.shape, _jnp.float32, 0.5, 1.5)
    if N_MICROBATCH > 1:
        for name, axis in PER_EXAMPLE_BATCH_AXIS.items():
            out[name] = _to_microbatches(out[name], axis)
    return {'x': out['x'], 'c': out['c'], 'positions': out['positions'], 'w_mod': out['w_mod'], 'b_mod': out['b_mod'], 'pre_norm_w': out['pre_norm_w'], 'post_norm_w': out['post_norm_w'], 'w_in': out['w_in'], 'mla_q_norm_w': out['mla_q_norm_w'], 'mla_q_up': out['mla_q_up'], 'mla_kv_norm_w': out['mla_kv_norm_w'], 'mla_kv_up': out['mla_kv_up'], 'gdn_conv_w': out['gdn_conv_w'], 'gdn_a_log': out['gdn_a_log'], 'gdn_dt_bias': out['gdn_dt_bias'], 'gdn_o_norm_w': out['gdn_o_norm_w'], 'w_out': out['w_out'], 'loss_target': out['loss_target'], 'm_w_mod': out['m_w_mod'], 'm_b_mod': out['m_b_mod'], 'm_pre_norm_w': out['m_pre_norm_w'], 'm_post_norm_w': out['m_post_norm_w'], 'm_w_in': out['m_w_in'], 'm_mla_q_norm_w': out['m_mla_q_norm_w'], 'm_mla_q_up': out['m_mla_q_up'], 'm_mla_kv_norm_w': out['m_mla_kv_norm_w'], 'm_mla_kv_up': out['m_mla_kv_up'], 'm_gdn_conv_w': out['m_gdn_conv_w'], 'm_gdn_a_log': out['m_gdn_a_log'], 'm_gdn_dt_bias': out['m_gdn_dt_bias'], 'm_gdn_o_norm_w': out['m_gdn_o_norm_w'], 'm_w_out': out['m_w_out'], 'v_w_mod': out['v_w_mod'], 'v_b_mod': out['v_b_mod'], 'v_pre_norm_w': out['v_pre_norm_w'], 'v_post_norm_w': out['v_post_norm_w'], 'v_w_in': out['v_w_in'], 'v_mla_q_norm_w': out['v_mla_q_norm_w'], 'v_mla_q_up': out['v_mla_q_up'], 'v_mla_kv_norm_w': out['v_mla_kv_norm_w'], 'v_mla_kv_up': out['v_mla_kv_up'], 'v_gdn_conv_w': out['v_gdn_conv_w'], 'v_gdn_a_log': out['v_gdn_a_log'], 'v_gdn_dt_bias': out['v_gdn_dt_bias'], 'v_gdn_o_norm_w': out['v_gdn_o_norm_w'], 'v_w_out': out['v_w_out']}


def _loss(weights, diff, rest, loss_target):
    with _jax.named_scope("forward"):
        args = {**rest, TWIN_DIFF_INPUT: diff, **{k: w.astype(_WEIGHT_DTYPES[k]) for k, w in weights.items()}}
        y = _forward(args)
    with _jax.named_scope("loss_head"):
        err = _jnp.square(y.astype(_jnp.float32) - loss_target)
        return 0.5 * _jnp.sum(_jnp.mean(err, axis=-1)) if err.ndim else 0.5 * err


def _adamw(w, g, m, v):
    m = ADAM_B1 * m + (1.0 - ADAM_B1) * g
    v = ADAM_B2 * v + (1.0 - ADAM_B2) * _jnp.square(g)
    m_hat = m / (1.0 - ADAM_B1 ** ADAM_STEP)
    v_hat = v / (1.0 - ADAM_B2 ** ADAM_STEP)
    delta = -ADAM_LR * (m_hat / (_jnp.sqrt(v_hat) + ADAM_EPS) + ADAM_WD * w)
    return delta, m, v


def reference(x, c, positions, w_mod, b_mod, pre_norm_w, post_norm_w, w_in, mla_q_norm_w, mla_q_up, mla_kv_norm_w, mla_kv_up, gdn_conv_w, gdn_a_log, gdn_dt_bias, gdn_o_norm_w, w_out, loss_target, m_w_mod, m_b_mod, m_pre_norm_w, m_post_norm_w, m_w_in, m_mla_q_norm_w, m_mla_q_up, m_mla_kv_norm_w, m_mla_kv_up, m_gdn_conv_w, m_gdn_a_log, m_gdn_dt_bias, m_gdn_o_norm_w, m_w_out, v_w_mod, v_b_mod, v_pre_norm_w, v_post_norm_w, v_w_in, v_mla_q_norm_w, v_mla_q_up, v_mla_kv_norm_w, v_mla_kv_up, v_gdn_conv_w, v_gdn_a_log, v_gdn_dt_bias, v_gdn_o_norm_w, v_w_out):
    given = dict(x=x, c=c, positions=positions, w_mod=w_mod, b_mod=b_mod, pre_norm_w=pre_norm_w, post_norm_w=post_norm_w, w_in=w_in, mla_q_norm_w=mla_q_norm_w, mla_q_up=mla_q_up, mla_kv_norm_w=mla_kv_norm_w, mla_kv_up=mla_kv_up, gdn_conv_w=gdn_conv_w, gdn_a_log=gdn_a_log, gdn_dt_bias=gdn_dt_bias, gdn_o_norm_w=gdn_o_norm_w, w_out=w_out, loss_target=loss_target, m_w_mod=m_w_mod, m_b_mod=m_b_mod, m_pre_norm_w=m_pre_norm_w, m_post_norm_w=m_post_norm_w, m_w_in=m_w_in, m_mla_q_norm_w=m_mla_q_norm_w, m_mla_q_up=m_mla_q_up, m_mla_kv_norm_w=m_mla_kv_norm_w, m_mla_kv_up=m_mla_kv_up, m_gdn_conv_w=m_gdn_conv_w, m_gdn_a_log=m_gdn_a_log, m_gdn_dt_bias=m_gdn_dt_bias, m_gdn_o_norm_w=m_gdn_o_norm_w, m_w_out=m_w_out, v_w_mod=v_w_mod, v_b_mod=v_b_mod, v_pre_norm_w=v_pre_norm_w, v_post_norm_w=v_post_norm_w, v_w_in=v_w_in, v_mla_q_norm_w=v_mla_q_norm_w, v_mla_q_up=v_mla_q_up, v_mla_kv_norm_w=v_mla_kv_norm_w, v_mla_kv_up=v_mla_kv_up, v_gdn_conv_w=v_gdn_conv_w, v_gdn_a_log=v_gdn_a_log, v_gdn_dt_bias=v_gdn_dt_bias, v_gdn_o_norm_w=v_gdn_o_norm_w, v_w_out=v_w_out)
    weights = {n: given[n] for n in TWIN_WEIGHTS}
    shared = {n: given[n] for n in SHARED_INPUTS}
    per_example = {n: given[n] for n in ['x', 'c', 'positions']}
    grad_fn = _jax.value_and_grad(_loss, argnums=(0, 1))

    def one_microbatch(ex, loss_target):
        ex = dict(ex)
        diff = ex.pop(TWIN_DIFF_INPUT)
        return grad_fn(weights, diff, {**shared, **ex}, loss_target)

    if N_MICROBATCH == 1:
        loss, (grad_w, grad_x) = one_microbatch(per_example, given["loss_target"])
    else:
        def body(carry, xs):
            loss_sum, grad_sum = carry
            l_k, (gw_k, gx_k) = one_microbatch(xs[0], xs[1])
            with _jax.named_scope("update"):
                return (loss_sum + l_k, _jax.tree.map(_jnp.add, grad_sum, gw_k)), gx_k

        init = (_jnp.zeros((), _jnp.float32), _jax.tree.map(_jnp.zeros_like, weights))
        (loss, grad_w), grad_x = _jax.lax.scan(body, init, (per_example, given["loss_target"]))
    with _jax.named_scope("update"):
        delta_w, new_m, new_v = {}, {}, {}
        for n in TWIN_WEIGHTS:
            delta_w[n], new_m[n], new_v[n] = _adamw(weights[n], grad_w[n], given["m_" + n], given["v_" + n])
    return (loss, grad_x, *[grad_w[n] for n in TWIN_WEIGHTS], *[delta_w[n] for n in TWIN_WEIGHTS],
            *[new_m[n] for n in TWIN_WEIGHTS], *[new_v[n] for n in TWIN_WEIGHTS])
```

```python
import functools
import math

import numpy as np
import jax
import jax.numpy as jnp
from jax import lax
from jax.experimental import pallas as pl
from jax.experimental.pallas import tpu as pltpu

F32 = jnp.float32
BF16 = jnp.bfloat16
MESH = pl.DeviceIdType.MESH
AXES = ("x", "y", "c")

D_MODEL = 1024
DEPTH = 4
MLA_HEADS = 4
MLA_NOPE = 128
MLA_ROPE = 64
MLA_V = 128
MLA_Q_RANK = 384
MLA_KV_RANK = 256
MLA_WIDTH = 512
GDN_HEADS = 4
GDN_DK = 128
GDN_WIDTH = 512
GDN_QKV = 1536
GDN_CONV = 4
IN_COLS = 3272
ROPE_THETA = 10000.0
NORM_EPS = 1e-6
ADAM_LR, ADAM_B1, ADAM_B2, ADAM_EPS, ADAM_WD, ADAM_STEP = 0.001, 0.9, 0.999, 1e-08, 0.01, 10

LANES = 128
N_CHIPS = 4
GDN_CHUNK = 128
VMEM_LIMIT = 56 * 1024 * 1024

EXT_COLS = 3456
E_QLAT, E_KVLAT, E_KR, E_ZMLA, E_QKV, E_AB, E_ZGDN = 0, 384, 640, 768, 1280, 2816, 2944


def _cparams(sem=None):
    if sem is None:
        return pltpu.CompilerParams(vmem_limit_bytes=VMEM_LIMIT)
    return pltpu.CompilerParams(dimension_semantics=sem, vmem_limit_bytes=VMEM_LIMIT)


def _pick(dim, target):
    if dim <= target:
        return dim
    best = None
    for t in range(LANES, target + 1, LANES):
        if dim % t == 0:
            best = t
    assert best is not None, (dim, target)
    return best


_DN = {"nn": (((1,), (0,)), ((), ())), "nt": (((1,), (1,)), ((), ())), "tn": (((0,), (0,)), ((), ()))}


def _dot_raw(a, b, mode, exact):
    if exact:
        return lax.dot_general(a, b, _DN[mode], precision=lax.Precision.HIGHEST, preferred_element_type=F32)
    return lax.dot_general(a.astype(BF16), b.astype(BF16), _DN[mode], preferred_element_type=F32)


@functools.partial(jax.custom_vjp, nondiff_argnums=(2, 3))
def bdot(a, b, mode="nn", exact=False):
    return _dot_raw(a, b, mode, exact)


def _bdot_fwd(a, b, mode, exact):
    return _dot_raw(a, b, mode, exact), (a, b)


def _bdot_bwd(mode, exact, res, g):
    a, b = res
    if mode == "nn":
        return bdot(g, b, "nt", exact), bdot(a, g, "tn", exact)
    if mode == "nt":
        return bdot(g, b, "nn", exact), bdot(g, a, "tn", exact)
    return bdot(b, g, "nt", exact), bdot(a, g, "nn", exact)


bdot.defvjp(_bdot_fwd, _bdot_bwd)


@jax.custom_vjp
def roll_half(x):
    return pltpu.roll(x, 64, 1)


roll_half.defvjp(lambda x: (pltpu.roll(x, 64, 1), None), lambda _, g: (pltpu.roll(g, 64, 1),))


def _sigmoid(x):
    return 1.0 / (1.0 + jnp.exp(-x))


def _silu(x):
    return x * _sigmoid(x)


def _softplus(x):
    return jnp.maximum(x, 0.0) + jnp.log(1.0 + jnp.exp(-jnp.abs(x)))


def _rms(x, w):
    return x * lax.rsqrt(jnp.mean(x * x, axis=-1, keepdims=True) + NORM_EPS) * w


def _rw_fwd(fn, name, rows, params, out_dims, out_dtypes, acc_dims, tile):
    s = rows[0].shape[0]
    t = min(tile, s)
    n = s // t
    nr, npar, no, na = len(rows), len(params), len(out_dims), len(acc_dims)

    def body(*refs):
        r, p = refs[:nr], refs[nr:nr + npar]
        o, a = refs[nr + npar:nr + npar + no], refs[nr + npar + no:]
        outs = fn(*[x[...] for x in r], *[x[...] for x in p])
        for ref, val in zip(o, outs[:no]):
            ref[...] = val.astype(ref.dtype)
        if na:
            @pl.when(pl.program_id(0) == 0)
            def _():
                for ref in a:
                    ref[...] = jnp.zeros_like(ref)
            for ref, val in zip(a, outs[no:]):
                ref[...] += val

    in_specs = [pl.BlockSpec((t, x.shape[1]), lambda i: (i, 0)) for x in rows]
    in_specs += [pl.BlockSpec(x.shape, lambda i: (0, 0)) for x in params]
    out_specs = [pl.BlockSpec((t, d), lambda i: (i, 0)) for d in out_dims]
    out_specs += [pl.BlockSpec((1, d), lambda i: (0, 0)) for d in acc_dims]
    out_shape = [jax.ShapeDtypeStruct((s, d), dt) for d, dt in zip(out_dims, out_dtypes)]
    out_shape += [jax.ShapeDtypeStruct((1, d), F32) for d in acc_dims]
    res = pl.pallas_call(body, name=name, grid=(n,), in_specs=in_specs, out_specs=out_specs, out_shape=out_shape,
                         compiler_params=_cparams(("arbitrary",)))(*rows, *params)
    return tuple(res)


def _rw_bwd(fn, name, rows, params, row_cts, acc_cts, n_diff, tile):
    s = rows[0].shape[0]
    t = min(tile, s)
    n = s // t
    nr, npar, no, na = len(rows), len(params), len(row_cts), len(acc_cts)

    def body(*refs):
        r, p = refs[:nr], refs[nr:nr + npar]
        g, ga = refs[nr + npar:nr + npar + no], refs[nr + npar + no:nr + npar + no + na]
        dr, dp = refs[nr + npar + no + na:nr + npar + no + na + n_diff], refs[nr + npar + no + na + n_diff:]
        _, vjp = jax.vjp(fn, *[x[...] for x in r], *[x[...] for x in p])
        cts = vjp(tuple([x[...] for x in g] + [x[...] for x in ga]))
        for ref, val in zip(dr, cts[:n_diff]):
            ref[...] = val
        if npar:
            @pl.when(pl.program_id(0) == 0)
            def _():
                for ref in dp:
                    ref[...] = jnp.zeros_like(ref)
            for ref, val in zip(dp, cts[nr:]):
                ref[...] += val

    in_specs = [pl.BlockSpec((t, x.shape[1]), lambda i: (i, 0)) for x in rows]
    in_specs += [pl.BlockSpec(x.shape, lambda i: (0, 0)) for x in params]
    in_specs += [pl.BlockSpec((t, x.shape[1]), lambda i: (i, 0)) for x in row_cts]
    in_specs += [pl.BlockSpec(x.shape, lambda i: (0, 0)) for x in acc_cts]
    out_specs = [pl.BlockSpec((t, x.shape[1]), lambda i: (i, 0)) for x in rows[:n_diff]]
    out_specs += [pl.BlockSpec(x.shape, lambda i: (0, 0)) for x in params]
    out_shape = [jax.ShapeDtypeStruct(x.shape, F32) for x in rows[:n_diff]]
    out_shape += [jax.ShapeDtypeStruct(x.shape, F32) for x in params]
    res = pl.pallas_call(body, name=name, grid=(n,), in_specs=in_specs, out_specs=out_specs, out_shape=out_shape,
                         compiler_params=_cparams(("arbitrary",)))(*rows, *params, *row_cts, *acc_cts)
    return tuple(res[:n_diff]), tuple(res[n_diff:])


def make_rowwise(fn, name, out_dims, acc_dims=(), n_nondiff=0, tile=256):
    out_dtypes = (F32,) * len(out_dims)

    @jax.custom_vjp
    def op(rows, params):
        return _rw_fwd(fn, name, rows, params, out_dims, out_dtypes, acc_dims, tile)

    def fwd(rows, params):
        return op(rows, params), (rows, params)

    def bwd(res, cts):
        rows, params = res
        n_diff = len(rows) - n_nondiff
        d_rows, d_params = _rw_bwd(fn, name + "_bwd", rows, params, cts[:len(out_dims)], cts[len(out_dims):],
                                   n_diff, tile)
        d_rows = d_rows + tuple(jnp.zeros_like(x) for x in rows[n_diff:])
        return d_rows, d_params

    op.defvjp(fwd, bwd)
    return op


def _mm(a, b, mode, name):
    if mode == "nn":
        (m, k), (_, n) = a.shape, b.shape
    elif mode == "nt":
        (m, k), (n, _) = a.shape, b.shape
    else:
        (k, m), (_, n) = a.shape, b.shape
    tm = _pick(m, 512)
    tn = _pick(n, 1152)
    tk = _pick(k, 1152) if mode != "tn" else _pick(k, 512)
    nk = k // tk

    def body(a_ref, b_ref, o_ref, acc_ref):
        kk = pl.program_id(2)

        @pl.when(kk == 0)
        def _():
            acc_ref[...] = jnp.zeros_like(acc_ref)

        acc_ref[...] += _dot_raw(a_ref[...], b_ref[...], mode, False)

        @pl.when(kk == nk - 1)
        def _():
            o_ref[...] = acc_ref[...]

    if mode == "nn":
        a_spec = pl.BlockSpec((tm, tk), lambda i, j, kk: (i, kk))
        b_spec = pl.BlockSpec((tk, tn), lambda i, j, kk: (kk, j))
    elif mode == "nt":
        a_spec = pl.BlockSpec((tm, tk), lambda i, j, kk: (i, kk))
        b_spec = pl.BlockSpec((tn, tk), lambda i, j, kk: (j, kk))
    else:
        a_spec = pl.BlockSpec((tk, tm), lambda i, j, kk: (kk, i))
        b_spec = pl.BlockSpec((tk, tn), lambda i, j, kk: (kk, j))
    return pl.pallas_call(
        body, name=name, grid=(m // tm, n // tn, nk), in_specs=[a_spec, b_spec],
        out_specs=pl.BlockSpec((tm, tn), lambda i, j, kk: (i, j)),
        out_shape=jax.ShapeDtypeStruct((m, n), F32), scratch_shapes=[pltpu.VMEM((tm, tn), F32)],
        compiler_params=_cparams(("parallel", "parallel", "arbitrary")))(a, b)


def make_linear(name):
    @jax.custom_vjp
    def op(a, w):
        return _mm(a, w.astype(BF16), "nn", name)

    def fwd(a, w):
        w16 = w.astype(BF16)
        return _mm(a, w16, "nn", name), (a, w16)

    def bwd(res, g):
        a, w16 = res
        return _mm(g, w16, "nt", name + "_dx"), _mm(a, g, "tn", name + "_dw")

    op.defvjp(fwd, bwd)
    return op


def _mla_prep_fn(qraw, kvraw, kr, t1, t2):
    kr_rot = kr * t1 + roll_half(kr) * t2
    qs, ks = [], []
    for h in range(MLA_HEADS):
        q_r = qraw[:, h * 256 + 128:(h + 1) * 256]
        qs += [qraw[:, h * 256:h * 256 + 128], q_r * t1 + roll_half(q_r) * t2]
        ks += [kvraw[:, h * 128:(h + 1) * 128], kr_rot]
    return jnp.concatenate(qs, axis=1), jnp.concatenate(ks, axis=1), kvraw[:, 512:]


def _flash_tile(s):
    return 512 if s >= 2048 else 128


def _flash_fwd(qf, kf, vf):
    s = qf.shape[0]
    t = _flash_tile(s)
    nb = s // t
    scale = (MLA_NOPE + MLA_ROPE) ** -0.5

    def body(q_ref, k_ref, v_ref, o_ref, lse_ref, m_sc, l_sc, acc_sc):
        i, j = pl.program_id(1), pl.program_id(2)

        @pl.when(j == 0)
        def _():
            m_sc[...] = jnp.full_like(m_sc, -1e30)
            l_sc[...] = jnp.zeros_like(l_sc)
            acc_sc[...] = jnp.zeros_like(acc_sc)

        @pl.when(j <= i)
        def _():
            sc = _dot_raw(q_ref[...], k_ref[...], "nt", False) * scale
            row = i * t + lax.broadcasted_iota(jnp.int32, (t, t), 0)
            col = j * t + lax.broadcasted_iota(jnp.int32, (t, t), 1)
            sc = jnp.where(col <= row, sc, -1e30)
            m_new = jnp.maximum(m_sc[...], jnp.max(sc, axis=1, keepdims=True))
            alpha = jnp.exp(m_sc[...] - m_new)
            p = jnp.exp(sc - m_new)
            l_sc[...] = alpha * l_sc[...] + jnp.sum(p, axis=1, keepdims=True)
            acc_sc[...] = alpha * acc_sc[...] + _dot_raw(p, v_ref[...], "nn", False)
            m_sc[...] = m_new

        @pl.when(j == i)
        def _():
            o_ref[...] = acc_sc[...] / l_sc[...]
            lse_ref[0] = m_sc[...] + jnp.log(l_sc[...])

    return pl.pallas_call(
        body, name="flash_fwd", grid=(MLA_HEADS, nb, nb),
        in_specs=[pl.BlockSpec((t, 256), lambda h, i, j: (i, h)),
                  pl.BlockSpec((t, 256), lambda h, i, j: (jnp.minimum(j, i), h)),
                  pl.BlockSpec((t, 128), lambda h, i, j: (jnp.minimum(j, i), h))],
        out_specs=[pl.BlockSpec((t, 128), lambda h, i, j: (i, h)),
                   pl.BlockSpec((1, t, 1), lambda h, i, j: (h, i, 0))],
        out_shape=[jax.ShapeDtypeStruct((s, MLA_WIDTH), F32), jax.ShapeDtypeStruct((MLA_HEADS, s, 1), F32)],
        scratch_shapes=[pltpu.VMEM((t, 1), F32), pltpu.VMEM((t, 1), F32), pltpu.VMEM((t, 128), F32)],
        compiler_params=_cparams(("parallel", "parallel", "arbitrary")))(qf, kf, vf)


def _flash_probs(q, k, lse, i, j, t, scale):
    sc = _dot_raw(q, k, "nt", False) * scale
    row = i * t + lax.broadcasted_iota(jnp.int32, (t, t), 0)
    col = j * t + lax.broadcasted_iota(jnp.int32, (t, t), 1)
    return jnp.where(col <= row, jnp.exp(sc - lse), 0.0)


def _flash_bwd(qf, kf, vf, o, lse, do):
    s = qf.shape[0]
    t = _flash_tile(s)
    nb = s // t
    scale = (MLA_NOPE + MLA_ROPE) ** -0.5

    def ds_of(q_ref, k_ref, v_ref, o_ref, lse_ref, do_ref, i, j):
        p = _flash_probs(q_ref[...], k_ref[...], lse_ref[0], i, j, t, scale)
        do_t = do_ref[...]
        dp = _dot_raw(do_t, v_ref[...], "nt", False)
        delta = jnp.sum(do_t * o_ref[...], axis=1, keepdims=True)
        return p, p * (dp - delta), do_t

    def dkv_body(q_ref, k_ref, v_ref, o_ref, lse_ref, do_ref, dk_ref, dv_ref, dk_sc, dv_sc):
        j, i = pl.program_id(1), pl.program_id(2)

        @pl.when(i == 0)
        def _():
            dk_sc[...] = jnp.zeros_like(dk_sc)
            dv_sc[...] = jnp.zeros_like(dv_sc)

        @pl.when(i >= j)
        def _():
            p, ds, do_t = ds_of(q_ref, k_ref, v_ref, o_ref, lse_ref, do_ref, i, j)
            dv_sc[...] += _dot_raw(p, do_t, "tn", False)
            dk_sc[...] += _dot_raw(ds, q_ref[...], "tn", False) * scale

        @pl.when(i == nb - 1)
        def _():
            dk_ref[...] = dk_sc[...]
            dv_ref[...] = dv_sc[...]

    qmap = lambda h, j, i: (jnp.maximum(i, j), h)
    dk, dv = pl.pallas_call(
        dkv_body, name="flash_dkv", grid=(MLA_HEADS, nb, nb),
        in_specs=[pl.BlockSpec((t, 256), qmap), pl.BlockSpec((t, 256), lambda h, j, i: (j, h)),
                  pl.BlockSpec((t, 128), lambda h, j, i: (j, h)), pl.BlockSpec((t, 128), qmap),
                  pl.BlockSpec((1, t, 1), lambda h, j, i: (h, jnp.maximum(i, j), 0)), pl.BlockSpec((t, 128), qmap)],
        out_specs=[pl.BlockSpec((t, 256), lambda h, j, i: (j, h)), pl.BlockSpec((t, 128), lambda h, j, i: (j, h))],
        out_shape=[jax.ShapeDtypeStruct((s, 1024), F32), jax.ShapeDtypeStruct((s, MLA_WIDTH), F32)],
        scratch_shapes=[pltpu.VMEM((t, 256), F32), pltpu.VMEM((t, 128), F32)],
        compiler_params=_cparams(("parallel", "parallel", "arbitrary")))(qf, kf, vf, o, lse, do)

    def dq_body(q_ref, k_ref, v_ref, o_ref, lse_ref, do_ref, dq_ref, dq_sc):
        i, j = pl.program_id(1), pl.program_id(2)

        @pl.when(j == 0)
        def _():
            dq_sc[...] = jnp.zeros_like(dq_sc)

        @pl.when(j <= i)
        def _():
            _, ds, _ = ds_of(q_ref, k_ref, v_ref, o_ref, lse_ref, do_ref, i, j)
            dq_sc[...] += _dot_raw(ds, k_ref[...], "nn", False) * scale

        @pl.when(j == i)
        def _():
            dq_ref[...] = dq_sc[...]

    kmap = lambda h, i, j: (jnp.minimum(j, i), h)
    dq = pl.pallas_call(
        dq_body, name="flash_dq", grid=(MLA_HEADS, nb, nb),
        in_specs=[pl.BlockSpec((t, 256), lambda h, i, j: (i, h)), pl.BlockSpec((t, 256), kmap),
                  pl.BlockSpec((t, 128), kmap), pl.BlockSpec((t, 128), lambda h, i, j: (i, h)),
                  pl.BlockSpec((1, t, 1), lambda h, i, j: (h, i, 0)), pl.BlockSpec((t, 128), lambda h, i, j: (i, h))],
        out_specs=pl.BlockSpec((t, 256), lambda h, i, j: (i, h)),
        out_shape=jax.ShapeDtypeStruct((s, 1024), F32), scratch_shapes=[pltpu.VMEM((t, 256), F32)],
        compiler_params=_cparams(("parallel", "parallel", "arbitrary")))(qf, kf, vf, o, lse, do)
    return dq, dk, dv


@jax.custom_vjp
def mla_attention(qraw, kvraw, kr, t1, t2):
    qf, kf, vf = _rw_fwd(_mla_prep_fn, "mla_prep", (qraw, kvraw, kr, t1, t2), (), (1024, 1024, 512),
                         (BF16, BF16, BF16), (), 256)
    return _flash_fwd(qf, kf, vf)[0]


def _mla_attention_fwd(qraw, kvraw, kr, t1, t2):
    qf, kf, vf = _rw_fwd(_mla_prep_fn, "mla_prep", (qraw, kvraw, kr, t1, t2), (), (1024, 1024, 512),
                         (BF16, BF16, BF16), (), 256)
    o, lse = _flash_fwd(qf, kf, vf)
    return o, (qraw, kvraw, kr, t1, t2, qf, kf, vf, o, lse)


def _mla_attention_bwd(res, do):
    qraw, kvraw, kr, t1, t2, qf, kf, vf, o, lse = res
    dq, dk, dv = _flash_bwd(qf, kf, vf, o, lse, do)
    (dqraw, dkvraw, dkr), _ = _rw_bwd(_mla_prep_fn, "mla_prep_bwd", (qraw, kvraw, kr, t1, t2), (), (dq, dk, dv), (),
                                      3, 256)
    return dqraw, dkvraw, dkr, jnp.zeros_like(t1), jnp.zeros_like(t2)


mla_attention.defvjp(_mla_attention_fwd, _mla_attention_bwd)


def _lane_pick(x, lane):
    ids = lax.broadcasted_iota(jnp.int32, x.shape, 1)
    col = jnp.sum(jnp.where(ids == lane, x, 0.0), axis=1, keepdims=True)
    return jnp.broadcast_to(col, x.shape)


def _gdn_prep_fn(x0, x1, x2, x3, ab, w0, w1, w2, w3, a_log, dt_bias):
    qkv = _silu(x0 * w0 + x1 * w1 + x2 * w2 + x3 * w3)
    g_all = -jnp.exp(a_log) * _softplus(ab + dt_bias)
    beta_all = _sigmoid(ab)
    qs, ks, gs, bs = [], [], [], []
    for h in range(GDN_HEADS):
        q = qkv[:, h * 128:(h + 1) * 128]
        k = qkv[:, 512 + h * 128:512 + (h + 1) * 128]
        qs.append(q * lax.rsqrt(jnp.sum(q * q, axis=-1, keepdims=True) + NORM_EPS) * (GDN_DK ** -0.5))
        ks.append(k * lax.rsqrt(jnp.sum(k * k, axis=-1, keepdims=True) + NORM_EPS))
        gs.append(_lane_pick(g_all, h))
        bs.append(_lane_pick(beta_all, GDN_HEADS + h))
    cat = lambda xs: jnp.concatenate(xs, axis=1)
    return cat(qs), cat(ks), qkv[:, 1024:], cat(gs), cat(bs)


@jax.custom_vjp
def _unit_lower_inverse(lm):
    c = lm.shape[0]
    row = lax.broadcasted_iota(jnp.int32, (c, c), 0)
    col = lax.broadcasted_iota(jnp.int32, (c, c), 1)
    t = (row == col).astype(F32) - jnp.where((row >> 1) == (col >> 1), lm, 0.0)
    for level in range(1, int(math.log2(c))):
        below = ((row >> (level + 1)) == (col >> (level + 1))) & ((row >> level) != (col >> level))
        t = t - _dot_raw(_dot_raw(t, jnp.where(below, lm, 0.0), "nn", True), t, "nn", True)
    return t


def _uli_fwd(lm):
    t = _unit_lower_inverse(lm)
    return t, t


def _uli_bwd(t, g):
    return (-bdot(bdot(t, g, "tn"), t, "nt"),)


_unit_lower_inverse.defvjp(_uli_fwd, _uli_bwd)


def _gdn_chunk_fn(q, k, v, gb, bb, s0):
    c = q.shape[0]
    row = lax.broadcasted_iota(jnp.int32, (c, c), 0)
    col = lax.broadcasted_iota(jnp.int32, (c, c), 1)
    incl, strict = row >= col, row > col
    gc = bdot(incl.astype(F32), gb, "nn", True)
    decay = jnp.exp(jnp.where(incl, gc - gc.T, -1e30))
    g_last = jnp.sum(gb, axis=0, keepdims=True)
    eg = jnp.exp(gc)
    kb = k * bb
    t = _unit_lower_inverse(jnp.where(strict, bdot(kb, k, "nt") * decay, 0.0))
    u = bdot(t, v * bb)
    w = bdot(t, kb * eg)
    qk = bdot(q, k, "nt") * decay
    v_new = u - bdot(w, s0)
    o = bdot(q * eg, s0) + bdot(qk, v_new)
    s1 = s0 * jnp.exp(g_last) + bdot(k * jnp.exp(g_last - gc), v_new, "tn")
    return o, s1


def _gdn_fwd(q, k, v, gb, bb):
    s = q.shape[0]
    c = min(GDN_CHUNK, s)
    nc = s // c

    def body(q_ref, k_ref, v_ref, g_ref, b_ref, o_ref, st_ref, s_sc):
        @pl.when(pl.program_id(1) == 0)
        def _():
            s_sc[...] = jnp.zeros_like(s_sc)

        st_ref[0, 0] = s_sc[...]
        o, s1 = _gdn_chunk_fn(q_ref[...], k_ref[...], v_ref[...], g_ref[...], b_ref[...], s_sc[...])
        o_ref[...] = o
        s_sc[...] = s1

    blk = pl.BlockSpec((c, 128), lambda h, n: (n, h))
    return pl.pallas_call(
        body, name="gdn_fwd", grid=(GDN_HEADS, nc), in_specs=[blk] * 5,
        out_specs=[blk, pl.BlockSpec((1, 1, 128, 128), lambda h, n: (h, n, 0, 0))],
        out_shape=[jax.ShapeDtypeStruct((s, GDN_WIDTH), F32), jax.ShapeDtypeStruct((GDN_HEADS, nc, 128, 128), F32)],
        scratch_shapes=[pltpu.VMEM((128, 128), F32)],
        compiler_params=_cparams(("parallel", "arbitrary")))(q, k, v, gb, bb)


def _gdn_bwd(q, k, v, gb, bb, states, do):
    s = q.shape[0]
    c = min(GDN_CHUNK, s)
    nc = s // c

    def body(q_ref, k_ref, v_ref, g_ref, b_ref, st_ref, do_ref, dq_ref, dk_ref, dv_ref, dg_ref, db_ref, ds_sc):
        @pl.when(pl.program_id(1) == 0)
        def _():
            ds_sc[...] = jnp.zeros_like(ds_sc)

        _, vjp = jax.vjp(_gdn_chunk_fn, q_ref[...], k_ref[...], v_ref[...], g_ref[...], b_ref[...], st_ref[0, 0])
        dq, dk, dv, dg, db, ds0 = vjp((do_ref[...], ds_sc[...]))
        dq_ref[...] = dq
        dk_ref[...] = dk
        dv_ref[...] = dv
        dg_ref[...] = dg
        db_ref[...] = db
        ds_sc[...] = ds0

    blk = pl.BlockSpec((c, 128), lambda h, n: (nc - 1 - n, h))
    return pl.pallas_call(
        body, name="gdn_bwd", grid=(GDN_HEADS, nc),
        in_specs=[blk] * 5 + [pl.BlockSpec((1, 1, 128, 128), lambda h, n: (h, nc - 1 - n, 0, 0)), blk],
        out_specs=[blk] * 5, out_shape=[jax.ShapeDtypeStruct((s, GDN_WIDTH), F32)] * 5,
        scratch_shapes=[pltpu.VMEM((128, 128), F32)],
        compiler_params=_cparams(("parallel", "arbitrary")))(q, k, v, gb, bb, states, do)


@jax.custom_vjp
def gdn_core(q, k, v, gb, bb):
    return _gdn_fwd(q, k, v, gb, bb)[0]


def _gdn_core_fwd(q, k, v, gb, bb):
    o, states = _gdn_fwd(q, k, v, gb, bb)
    return o, (q, k, v, gb, bb, states)


def _gdn_core_bwd(res, do):
    return tuple(_gdn_bwd(*res, do))


gdn_core.defvjp(_gdn_core_fwd, _gdn_core_bwd)


def _cact_fn(c):
    return (_silu(c),)


def _prenorm_fn(x, w, scale_raw, scale_b, shift_raw, shift_b):
    return (_rms(x, w) * (1.0 + scale_raw + scale_b) + shift_raw + shift_b,)


def _rmsnorm_fn(x, w):
    return (_rms(x, w),)


def _gate_fn(o, z):
    return (o * _silu(z),)


def _gdn_out_fn(o, z, w):
    parts = [_rms(o[:, h * 128:(h + 1) * 128], w) for h in range(GDN_HEADS)]
    return (jnp.concatenate(parts, axis=1) * _silu(z),)


def _post_fn(x, y, w, gate_raw, gate_b):
    return (x + (gate_raw + gate_b) * _rms(y, w),)


def _loss_fn(y, tgt):
    err = y - tgt
    part = jnp.sum(0.5 * jnp.mean(err * err, axis=-1, keepdims=True), axis=0, keepdims=True)
    return (jnp.broadcast_to(part, (1, LANES)),)


_OPS = dict(
    cact=make_rowwise(_cact_fn, "c_act", (D_MODEL,), tile=16),
    prenorm=make_rowwise(_prenorm_fn, "prenorm", (D_MODEL,)),
    qnorm=make_rowwise(_rmsnorm_fn, "q_norm", (MLA_Q_RANK,)),
    kvnorm=make_rowwise(_rmsnorm_fn, "kv_norm", (MLA_KV_RANK,)),
    gate=make_rowwise(_gate_fn, "mla_gate", (MLA_WIDTH,)),
    gdn_prep=make_rowwise(_gdn_prep_fn, "gdn_prep", (512, 512, 512, 512, 512), tile=128),
    gdn_out=make_rowwise(_gdn_out_fn, "gdn_out", (GDN_WIDTH,)),
    post=make_rowwise(_post_fn, "postnorm", (D_MODEL,)),
    loss=make_rowwise(_loss_fn, "loss", (), acc_dims=(LANES,), n_nondiff=1),
    lin_mod=make_linear("lin_mod"), lin_in=make_linear("lin_in"), lin_q=make_linear("lin_q"),
    lin_kv=make_linear("lin_kv"), lin_out=make_linear("lin_out"),
)


def _swap_halves(w):
    half = w.shape[-1] // 2
    return jnp.concatenate([w[..., half:], w[..., :half]], axis=-1)


def _w_in_ext(w):
    k_pe = w[:, 640:704]
    return jnp.concatenate([w[:, :640], k_pe, _swap_halves(k_pe), w[:, 704:2752], w[:, 2752:2760],
                            jnp.zeros((w.shape[0], LANES - 8), w.dtype), w[:, 2760:]], axis=1)


def _q_up_ext(w):
    parts = []
    for h in range(MLA_HEADS):
        rope = w[:, h * 192 + 128:(h + 1) * 192]
        parts += [w[:, h * 192:h * 192 + 128], rope, _swap_halves(rope)]
    return jnp.concatenate(parts, axis=1)


def _kv_up_perm(w):
    ks = [w[:, h * 256:h * 256 + 128] for h in range(MLA_HEADS)]
    vs = [w[:, h * 256 + 128:(h + 1) * 256] for h in range(MLA_HEADS)]
    return jnp.concatenate(ks + vs, axis=1)


def _delay(x, d):
    if d == 0:
        return x
    return jnp.concatenate([jnp.zeros((d, x.shape[1]), x.dtype), x[:-d]], axis=0)


def _pad_lanes(v):
    return jnp.pad(v, (0, LANES - v.shape[0]))[None, :]


def _local_loss(weights, x, c, positions, target):
    s = x.shape[0]
    half = MLA_ROPE // 2
    inv_freq = jnp.power(ROPE_THETA, -jnp.arange(half, dtype=F32) * 2.0 / MLA_ROPE)
    ang = positions.astype(F32)[:, None] * inv_freq
    cos, sin, zero = jnp.cos(ang), jnp.sin(ang), jnp.zeros((s, 2 * half), F32)
    t1 = jnp.concatenate([cos, cos, zero], axis=1)
    t2 = jnp.concatenate([-sin, sin, zero], axis=1)

    (c_act,) = _OPS["cact"]((jnp.pad(c, ((0, 15), (0, 0))),), ())
    for l in range(DEPTH):
        mod = _OPS["lin_mod"](c_act, weights["w_mod"][l])[0:1]
        b = weights["b_mod"][l][None, :]
        shift_raw, scale_raw, gate_raw = mod[:, :1024], mod[:, 1024:2048], mod[:, 2048:]
        shift_b, scale_b, gate_b = b[:, :1024], b[:, 1024:2048], b[:, 2048:]
        (h,) = _OPS["prenorm"]((x,), (weights["pre_norm_w"][l][None], scale_raw, scale_b, shift_raw, shift_b))
        proj = _OPS["lin_in"](h, _w_in_ext(weights["w_in"][l]))
        (qn,) = _OPS["qnorm"]((proj[:, E_QLAT:E_KVLAT],), (weights["mla_q_norm_w"][l][None],))
        qraw = _OPS["lin_q"](qn, _q_up_ext(weights["mla_q_up"][l]))
        (kvn,) = _OPS["kvnorm"]((proj[:, E_KVLAT:E_KR],), (weights["mla_kv_norm_w"][l][None],))
        kvraw = _OPS["lin_kv"](kvn, _kv_up_perm(weights["mla_kv_up"][l]))
        o_mla = mla_attention(qraw, kvraw, proj[:, E_KR:E_ZMLA], t1, t2)
        (y_mla,) = _OPS["gate"]((o_mla, proj[:, E_ZMLA:E_QKV]), ())
        qkv = proj[:, E_QKV:E_AB]
        cw = weights["gdn_conv_w"][l]
        taps = tuple(_delay(qkv, GDN_CONV - 1 - j) for j in range(GDN_CONV))
        params = tuple(cw[j][None] for j in range(GDN_CONV))
        params += (_pad_lanes(weights["gdn_a_log"][l]), _pad_lanes(weights["gdn_dt_bias"][l]))
        qg, kg, vg, gb, bb = _OPS["gdn_prep"](taps + (proj[:, E_AB:E_ZGDN],), params)
        o_gdn = gdn_core(qg, kg, vg, gb, bb)
        (y_gdn,) = _OPS["gdn_out"]((o_gdn, proj[:, E_ZGDN:]), (weights["gdn_o_norm_w"][l][None],))
        y = _OPS["lin_out"](jnp.concatenate([y_mla, y_gdn], axis=1), weights["w_out"][l])
        (x,) = _OPS["post"]((x, y), (weights["post_norm_w"][l][None], gate_raw, gate_b))
    (acc,) = _OPS["loss"]((x, target), ())
    return acc[0, 0]


def _chip_index():
    return 2 * lax.axis_index("x") + lax.axis_index("y")


def _other_chips(x, y):
    return [(1 - x, y), (x, 1 - y), (1 - x, 1 - y)]


def _any_spec():
    return pl.BlockSpec(memory_space=pl.ANY)


def ag_weights(pack):
    r = pack.shape[0]
    rh = r // 2

    def body(in_ref, out_ref, send_sems, recv_sems, local_sem):
        x, y, c = lax.axis_index("x"), lax.axis_index("y"), lax.axis_index("c")
        sibling = (x, y, 1 - c)
        chips = _other_chips(x, y)

        def half(ref, chip_xy, hc):
            return ref.at[2 * chip_xy[0] + chip_xy[1], pl.ds(hc * rh, rh), :]

        def copy(k, src, chip_xy, hc, to):
            return pltpu.make_async_remote_copy(src_ref=src, dst_ref=half(out_ref, chip_xy, hc),
                                                send_sem=send_sems.at[k], recv_sem=recv_sems.at[k],
                                                device_id=to, device_id_type=MESH)

        mine = pltpu.make_async_copy(in_ref, out_ref.at[2 * x + y], local_sem)
        mine.start()
        first = [copy(k, in_ref.at[pl.ds(c * rh, rh), :], (x, y), c, (*chip, c)) for k, chip in enumerate(chips)]
        for cp in first:
            cp.start()
        passed = [copy(3 + k, half(out_ref, chip, c), chip, c, sibling) for k, chip in enumerate(chips)]
        for k, chip in enumerate(chips):
            copy(k, half(out_ref, chip, c), chip, c, (x, y, c)).wait_recv()
            passed[k].start()
        for k, chip in enumerate(chips):
            copy(3 + k, half(out_ref, chip, 1 - c), chip, 1 - c, (x, y, c)).wait_recv()
        for cp in first + passed:
            cp.wait_send()
        mine.wait()

    return pl.pallas_call(
        body, name="ag_weights", in_specs=[_any_spec()], out_specs=_any_spec(),
        out_shape=jax.ShapeDtypeStruct((N_CHIPS, r, LANES), pack.dtype),
        scratch_shapes=[pltpu.SemaphoreType.DMA((6,)), pltpu.SemaphoreType.DMA((6,)), pltpu.SemaphoreType.DMA],
        compiler_params=pltpu.CompilerParams(has_side_effects=True))(pack)


def rs_pair(gpack):
    r = gpack.shape[1]
    rh = r // 2

    def body(in_ref, out_ref, send_sem, recv_sem):
        x, y, c = lax.axis_index("x"), lax.axis_index("y"), lax.axis_index("c")
        cp = pltpu.make_async_remote_copy(src_ref=in_ref.at[:, pl.ds((1 - c) * rh, rh), :], dst_ref=out_ref,
                                          send_sem=send_sem, recv_sem=recv_sem, device_id=(x, y, 1 - c),
                                          device_id_type=MESH)
        cp.start()
        cp.wait()

    return pl.pallas_call(
        body, name="rs_pair", in_specs=[_any_spec()], out_specs=_any_spec(),
        out_shape=jax.ShapeDtypeStruct((N_CHIPS, rh, LANES), F32),
        scratch_shapes=[pltpu.SemaphoreType.DMA, pltpu.SemaphoreType.DMA],
        compiler_params=pltpu.CompilerParams(has_side_effects=True))(gpack)


def pair_add(gpack, from_sibling):
    rh = from_sibling.shape[1]
    t = _row_tile(rh, 1024)
    nt = rh // t
    c_arr = lax.axis_index("c").astype(jnp.int32).reshape(1)

    def body(c_ref, a_ref, b_ref, o_ref):
        o_ref[...] = a_ref[...] + b_ref[...]

    return pl.pallas_call(
        body, name="pair_add",
        grid_spec=pltpu.PrefetchScalarGridSpec(
            num_scalar_prefetch=1, grid=(N_CHIPS, nt),
            in_specs=[pl.BlockSpec((1, t, LANES), lambda j, i, c_ref: (j, c_ref[0] * nt + i, 0)),
                      pl.BlockSpec((1, t, LANES), lambda j, i, c_ref: (j, i, 0))],
            out_specs=pl.BlockSpec((1, t, LANES), lambda j, i, c_ref: (j, i, 0))),
        out_shape=jax.ShapeDtypeStruct((N_CHIPS, rh, LANES), F32),
        compiler_params=_cparams(("parallel", "parallel")))(c_arr, gpack, from_sibling)


def rs_cross(pairs):
    rh = pairs.shape[1]

    def body(in_ref, out_ref, send_sems, recv_sems):
        x, y, c = lax.axis_index("x"), lax.axis_index("y"), lax.axis_index("c")
        copies = []
        for k, chip in enumerate(_other_chips(x, y)):
            copies.append(pltpu.make_async_remote_copy(
                src_ref=in_ref.at[2 * chip[0] + chip[1]], dst_ref=out_ref.at[k], send_sem=send_sems.at[k],
                recv_sem=recv_sems.at[k], device_id=(*chip, c), device_id_type=MESH))
        for cp in copies:
            cp.start()
        for cp in copies:
            cp.wait()

    return pl.pallas_call(
        body, name="rs_cross", in_specs=[_any_spec()], out_specs=_any_spec(),
        out_shape=jax.ShapeDtypeStruct((3, rh, LANES), F32),
        scratch_shapes=[pltpu.SemaphoreType.DMA((3,)), pltpu.SemaphoreType.DMA((3,))],
        compiler_params=pltpu.CompilerParams(has_side_effects=True))(pairs)


def chip_add(pairs, received):
    rh = pairs.shape[1]
    t = _row_tile(rh, 1024)
    j_arr = _chip_index().astype(jnp.int32).reshape(1)

    def body(j_ref, a_ref, r0_ref, r1_ref, r2_ref, o_ref):
        o_ref[...] = (a_ref[0] + r0_ref[0]) + (r1_ref[0] + r2_ref[0])

    return pl.pallas_call(
        body, name="chip_add",
        grid_spec=pltpu.PrefetchScalarGridSpec(
            num_scalar_prefetch=1, grid=(rh // t,),
            in_specs=[pl.BlockSpec((1, t, LANES), lambda i, j_ref: (j_ref[0], i, 0)),
                      pl.BlockSpec((1, t, LANES), lambda i, j_ref: (0, i, 0)),
                      pl.BlockSpec((1, t, LANES), lambda i, j_ref: (1, i, 0)),
                      pl.BlockSpec((1, t, LANES), lambda i, j_ref: (2, i, 0))],
            out_specs=pl.BlockSpec((t, LANES), lambda i, j_ref: (i, 0))),
        out_shape=jax.ShapeDtypeStruct((rh, LANES), F32),
        compiler_params=_cparams(("parallel",)))(j_arr, pairs, received, received, received)


def rs_share(reduced):
    rh = reduced.shape[0]

    def body(in_ref, out_ref, send_sem, recv_sem, local_sem):
        x, y, c = lax.axis_index("x"), lax.axis_index("y"), lax.axis_index("c")
        mine = pltpu.make_async_copy(in_ref, out_ref.at[pl.ds(c * rh, rh), :], local_sem)
        mine.start()
        cp = pltpu.make_async_remote_copy(src_ref=in_ref, dst_ref=out_ref.at[pl.ds(c * rh, rh), :], send_sem=send_sem,
                                          recv_sem=recv_sem, device_id=(x, y, 1 - c), device_id_type=MESH)
        cp.start()
        recv = pltpu.make_async_remote_copy(src_ref=in_ref, dst_ref=out_ref.at[pl.ds((1 - c) * rh, rh), :],
                                            send_sem=send_sem, recv_sem=recv_sem, device_id=(x, y, 1 - c),
                                            device_id_type=MESH)
        recv.wait_recv()
        cp.wait_send()
        mine.wait()

    return pl.pallas_call(
        body, name="rs_share", in_specs=[_any_spec()], out_specs=_any_spec(),
        out_shape=jax.ShapeDtypeStruct((2 * rh, LANES), F32),
        scratch_shapes=[pltpu.SemaphoreType.DMA, pltpu.SemaphoreType.DMA, pltpu.SemaphoreType.DMA],
        compiler_params=pltpu.CompilerParams(has_side_effects=True))(reduced)


def _row_tile(rows, target):
    best = 8
    for t in range(8, min(rows, target) + 1, 8):
        if rows % t == 0:
            best = t
    return best


def adamw(w, g, m, v):
    shape = w.shape
    cols = shape[-1]
    rows = int(np.prod(shape[:-1]))
    flat = lambda a: a.reshape(rows, cols)
    t = rows if rows <= 256 else _row_tile(rows, 256)

    def body(w_ref, g_ref, m_ref, v_ref, d_ref, mo_ref, vo_ref):
        gv = g_ref[...]
        m_new = ADAM_B1 * m_ref[...] + (1.0 - ADAM_B1) * gv
        v_new = ADAM_B2 * v_ref[...] + (1.0 - ADAM_B2) * (gv * gv)
        m_hat = m_new / (1.0 - ADAM_B1 ** ADAM_STEP)
        v_hat = v_new / (1.0 - ADAM_B2 ** ADAM_STEP)
        d_ref[...] = -ADAM_LR * (m_hat / (jnp.sqrt(v_hat) + ADAM_EPS) + ADAM_WD * w_ref[...])
        mo_ref[...] = m_new
        vo_ref[...] = v_new

    spec = pl.BlockSpec((t, cols), lambda i: (i, 0))
    outs = pl.pallas_call(
        body, name="adamw", grid=(rows // t,), in_specs=[spec] * 4, out_specs=[spec] * 3,
        out_shape=[jax.ShapeDtypeStruct((rows, cols), F32)] * 3,
        compiler_params=_cparams(("parallel",)))(flat(w), flat(g), flat(m), flat(v))
    return tuple(o.reshape(shape) for o in outs)


SHARDED = (("w_mod", 2), ("w_in", 2), ("mla_q_up", 2), ("mla_kv_up", 2), ("gdn_conv_w", 2), ("w_out", 1))
REPLICATED = ("b_mod", "pre_norm_w", "post_norm_w", "mla_q_norm_w", "mla_kv_norm_w", "gdn_a_log", "gdn_dt_bias",
              "gdn_o_norm_w")
WEIGHT_ORDER = ("w_mod", "b_mod", "pre_norm_w", "post_norm_w", "w_in", "mla_q_norm_w", "mla_q_up", "mla_kv_norm_w",
                "mla_kv_up", "gdn_conv_w", "gdn_a_log", "gdn_dt_bias", "gdn_o_norm_w", "w_out")
EXACT_F32 = ("gdn_conv_w",)


def _pad_to(flat, multiple):
    return jnp.pad(flat, (0, (-flat.shape[0]) % multiple))


def _gather_weights(shards):
    pieces, sizes = [], []
    for name, _ in SHARDED:
        a = shards[name]
        a = lax.bitcast_convert_type(a, BF16) if name in EXACT_F32 else a.astype(BF16)
        pieces.append(a.reshape(-1))
        sizes.append(pieces[-1].shape[0])
    flat = _pad_to(jnp.concatenate(pieces), 32 * LANES)
    gathered = ag_weights(flat.reshape(-1, LANES)).reshape(N_CHIPS, -1)
    full, off = {}, 0
    for (name, axis), size in zip(SHARDED, sizes):
        blk = gathered[:, off:off + size]
        off += size
        shp = shards[name].shape
        if name in EXACT_F32:
            blk = lax.bitcast_convert_type(blk.reshape((N_CHIPS,) + shp + (2,)), F32)
        else:
            blk = blk.reshape((N_CHIPS,) + shp).astype(F32)
        blk = jnp.moveaxis(blk, 0, axis)
        full[name] = blk.reshape(shp[:axis] + (N_CHIPS * shp[axis],) + shp[axis + 1:])
    return full


def _pack_grads(grads):
    pieces = []
    for name, axis in SHARDED:
        g = grads[name]
        shp = g.shape
        g = g.reshape(shp[:axis] + (N_CHIPS, shp[axis] // N_CHIPS) + shp[axis + 1:])
        pieces.append(jnp.moveaxis(g, axis, 0).reshape(N_CHIPS, -1))
    small = jnp.concatenate([grads[name].reshape(-1) for name in REPLICATED])
    pieces.append(jnp.broadcast_to(small[None, :], (N_CHIPS, small.shape[0])))
    flat = jnp.concatenate(pieces, axis=1)
    flat = jnp.pad(flat, ((0, 0), (0, (-flat.shape[1]) % (16 * LANES))))
    return flat.reshape(N_CHIPS, -1, LANES)


def _unpack_grads(reduced, shard_shapes, rep_shapes):
    flat = reduced.reshape(-1)
    out, off = {}, 0
    for name, _ in SHARDED:
        size = int(np.prod(shard_shapes[name]))
        out[name] = flat[off:off + size].reshape(shard_shapes[name])
        off += size
    for name in REPLICATED:
        size = int(np.prod(rep_shapes[name]))
        out[name] = flat[off:off + size].reshape(rep_shapes[name])
        off += size
    return out


def reduce_grads(gpack):
    pairs = pair_add(gpack, rs_pair(gpack))
    return rs_share(chip_add(pairs, rs_cross(pairs)))


def kernel(x, c, positions, w_mod, b_mod, pre_norm_w, post_norm_w, w_in, mla_q_norm_w, mla_q_up, mla_kv_norm_w, mla_kv_up, gdn_conv_w, gdn_a_log, gdn_dt_bias, gdn_o_norm_w, w_out, loss_target, m_w_mod, m_b_mod, m_pre_norm_w, m_post_norm_w, m_w_in, m_mla_q_norm_w, m_mla_q_up, m_mla_kv_norm_w, m_mla_kv_up, m_gdn_conv_w, m_gdn_a_log, m_gdn_dt_bias, m_gdn_o_norm_w, m_w_out, v_w_mod, v_b_mod, v_pre_norm_w, v_post_norm_w, v_w_in, v_mla_q_norm_w, v_mla_q_up, v_mla_kv_norm_w, v_mla_kv_up, v_gdn_conv_w, v_gdn_a_log, v_gdn_dt_bias, v_gdn_o_norm_w, v_w_out):
    given = dict(w_mod=w_mod, b_mod=b_mod, pre_norm_w=pre_norm_w, post_norm_w=post_norm_w, w_in=w_in,
                 mla_q_norm_w=mla_q_norm_w, mla_q_up=mla_q_up, mla_kv_norm_w=mla_kv_norm_w, mla_kv_up=mla_kv_up,
                 gdn_conv_w=gdn_conv_w, gdn_a_log=gdn_a_log, gdn_dt_bias=gdn_dt_bias, gdn_o_norm_w=gdn_o_norm_w,
                 w_out=w_out)
    moments_m = dict(w_mod=m_w_mod, b_mod=m_b_mod, pre_norm_w=m_pre_norm_w, post_norm_w=m_post_norm_w, w_in=m_w_in,
                     mla_q_norm_w=m_mla_q_norm_w, mla_q_up=m_mla_q_up, mla_kv_norm_w=m_mla_kv_norm_w,
                     mla_kv_up=m_mla_kv_up, gdn_conv_w=m_gdn_conv_w, gdn_a_log=m_gdn_a_log,
                     gdn_dt_bias=m_gdn_dt_bias, gdn_o_norm_w=m_gdn_o_norm_w, w_out=m_w_out)
    moments_v = dict(w_mod=v_w_mod, b_mod=v_b_mod, pre_norm_w=v_pre_norm_w, post_norm_w=v_post_norm_w, w_in=v_w_in,
                     mla_q_norm_w=v_mla_q_norm_w, mla_q_up=v_mla_q_up, mla_kv_norm_w=v_mla_kv_norm_w,
                     mla_kv_up=v_mla_kv_up, gdn_conv_w=v_gdn_conv_w, gdn_a_log=v_gdn_a_log,
                     gdn_dt_bias=v_gdn_dt_bias, gdn_o_norm_w=v_gdn_o_norm_w, w_out=v_w_out)

    full = _gather_weights({name: given[name] for name, _ in SHARDED})
    for name in REPLICATED:
        full[name] = given[name]
    loss_local, (grads, grad_x) = jax.value_and_grad(_local_loss, argnums=(0, 1))(
        full, x[0], c, positions[0], loss_target[0])
    loss = lax.psum(loss_local, AXES)

    reduced = reduce_grads(_pack_grads(grads))
    grad_w = _unpack_grads(reduced, {name: given[name].shape for name, _ in SHARDED},
                           {name: given[name].shape for name in REPLICATED})
    delta, new_m, new_v = {}, {}, {}
    for name in WEIGHT_ORDER:
        delta[name], new_m[name], new_v[name] = adamw(given[name], grad_w[name], moments_m[name], moments_v[name])
    return (loss, grad_x[None], *[grad_w[n] for n in WEIGHT_ORDER], *[delta[n] for n in WEIGHT_ORDER],
            *[new_m[n] for n in WEIGHT_ORDER], *[new_v[n] for n in WEIGHT_ORDER])
```

```python
import functools
import math

import numpy as np
import jax
import jax.numpy as jnp
from jax import lax
from jax.experimental import pallas as pl
from jax.experimental.pallas import tpu as pltpu

F32 = jnp.float32
BF16 = jnp.bfloat16
MESH = pl.DeviceIdType.MESH
AXES = ("x", "y", "c")

D_MODEL = 1024
DEPTH = 4
MLA_HEADS = 4
MLA_NOPE = 128
MLA_ROPE = 64
MLA_V = 128
MLA_Q_RANK = 384
MLA_KV_RANK = 256
MLA_WIDTH = 512
GDN_HEADS = 4
GDN_DK = 128
GDN_WIDTH = 512
GDN_QKV = 1536
GDN_CONV = 4
IN_COLS = 3272
ROPE_THETA = 10000.0
NORM_EPS = 1e-6
ADAM_LR, ADAM_B1, ADAM_B2, ADAM_EPS, ADAM_WD, ADAM_STEP = 0.001, 0.9, 0.999, 1e-08, 0.01, 10

LANES = 128
N_CHIPS = 4
GDN_CHUNK = 128
VMEM_LIMIT = 56 * 1024 * 1024

EXT_COLS = 3456
E_QLAT, E_KVLAT, E_KR, E_ZMLA, E_QKV, E_AB, E_ZGDN = 0, 384, 640, 768, 1280, 2816, 2944


def _cparams(sem=None):
    if sem is None:
        return pltpu.CompilerParams(vmem_limit_bytes=VMEM_LIMIT)
    return pltpu.CompilerParams(dimension_semantics=sem, vmem_limit_bytes=VMEM_LIMIT)


def _pick(dim, target):
    if dim <= target:
        return dim
    best = None
    for t in range(LANES, target + 1, LANES):
        if dim % t == 0:
            best = t
    assert best is not None, (dim, target)
    return best


_DN = {"nn": (((1,), (0,)), ((), ())), "nt": (((1,), (1,)), ((), ())), "tn": (((0,), (0,)), ((), ()))}


def _dot_raw(a, b, mode, exact):
    if exact:
        return lax.dot_general(a, b, _DN[mode], precision=lax.Precision.HIGHEST, preferred_element_type=F32)
    return lax.dot_general(a.astype(BF16), b.astype(BF16), _DN[mode], preferred_element_type=F32)


def _dot_split(a, b):
    a_hi, b_hi = a.astype(BF16), b.astype(BF16)
    a_lo, b_lo = (a - a_hi.astype(F32)).astype(BF16), (b - b_hi.astype(F32)).astype(BF16)
    dot = lambda u, w: lax.dot_general(u, w, _DN["nn"], preferred_element_type=F32)
    return dot(a_hi, b_hi) + (dot(a_hi, b_lo) + dot(a_lo, b_hi))


@functools.partial(jax.custom_vjp, nondiff_argnums=(2, 3))
def bdot(a, b, mode="nn", exact=False):
    return _dot_raw(a, b, mode, exact)


def _bdot_fwd(a, b, mode, exact):
    return _dot_raw(a, b, mode, exact), (a, b)


def _bdot_bwd(mode, exact, res, g):
    a, b = res
    if mode == "nn":
        return bdot(g, b, "nt", exact), bdot(a, g, "tn", exact)
    if mode == "nt":
        return bdot(g, b, "nn", exact), bdot(g, a, "tn", exact)
    return bdot(b, g, "nt", exact), bdot(a, g, "nn", exact)


bdot.defvjp(_bdot_fwd, _bdot_bwd)


@jax.custom_vjp
def roll_half(x):
    return pltpu.roll(x, 64, 1)


roll_half.defvjp(lambda x: (pltpu.roll(x, 64, 1), None), lambda _, g: (pltpu.roll(g, 64, 1),))


def _sigmoid(x):
    return 1.0 / (1.0 + jnp.exp(-x))


def _silu(x):
    return x * _sigmoid(x)


def _softplus(x):
    return jnp.maximum(x, 0.0) + jnp.log(1.0 + jnp.exp(-jnp.abs(x)))


def _rms(x, w):
    return x * lax.rsqrt(jnp.mean(x * x, axis=-1, keepdims=True) + NORM_EPS) * w


def _rw_fwd(fn, name, rows, params, out_dims, out_dtypes, acc_dims, tile):
    s = rows[0].shape[0]
    t = min(tile, s)
    n = s // t
    nr, npar, no, na = len(rows), len(params), len(out_dims), len(acc_dims)

    def body(*refs):
        r, p = refs[:nr], refs[nr:nr + npar]
        o, a = refs[nr + npar:nr + npar + no], refs[nr + npar + no:]
        outs = fn(*[x[...] for x in r], *[x[...] for x in p])
        for ref, val in zip(o, outs[:no]):
            ref[...] = val.astype(ref.dtype)
        if na:
            @pl.when(pl.program_id(0) == 0)
            def _():
                for ref in a:
                    ref[...] = jnp.zeros_like(ref)
            for ref, val in zip(a, outs[no:]):
                ref[...] += val

    in_specs = [pl.BlockSpec((t, x.shape[1]), lambda i: (i, 0)) for x in rows]
    in_specs += [pl.BlockSpec(x.shape, lambda i: (0, 0)) for x in params]
    out_specs = [pl.BlockSpec((t, d), lambda i: (i, 0)) for d in out_dims]
    out_specs += [pl.BlockSpec((1, d), lambda i: (0, 0)) for d in acc_dims]
    out_shape = [jax.ShapeDtypeStruct((s, d), dt) for d, dt in zip(out_dims, out_dtypes)]
    out_shape += [jax.ShapeDtypeStruct((1, d), F32) for d in acc_dims]
    res = pl.pallas_call(body, name=name, grid=(n,), in_specs=in_specs, out_specs=out_specs, out_shape=out_shape,
                         compiler_params=_cparams(("arbitrary",)))(*rows, *params)
    return tuple(res)


def _rw_bwd(fn, name, rows, params, row_cts, acc_cts, n_diff, tile):
    s = rows[0].shape[0]
    t = min(tile, s)
    n = s // t
    nr, npar, no, na = len(rows), len(params), len(row_cts), len(acc_cts)

    def body(*refs):
        r, p = refs[:nr], refs[nr:nr + npar]
        g, ga = refs[nr + npar:nr + npar + no], refs[nr + npar + no:nr + npar + no + na]
        dr, dp = refs[nr + npar + no + na:nr + npar + no + na + n_diff], refs[nr + npar + no + na + n_diff:]
        _, vjp = jax.vjp(fn, *[x[...] for x in r], *[x[...] for x in p])
        cts = vjp(tuple([x[...] for x in g] + [x[...] for x in ga]))
        for ref, val in zip(dr, cts[:n_diff]):
            ref[...] = val
        if npar:
            @pl.when(pl.program_id(0) == 0)
            def _():
                for ref in dp:
                    ref[...] = jnp.zeros_like(ref)
            for ref, val in zip(dp, cts[nr:]):
                ref[...] += val

    in_specs = [pl.BlockSpec((t, x.shape[1]), lambda i: (i, 0)) for x in rows]
    in_specs += [pl.BlockSpec(x.shape, lambda i: (0, 0)) for x in params]
    in_specs += [pl.BlockSpec((t, x.shape[1]), lambda i: (i, 0)) for x in row_cts]
    in_specs += [pl.BlockSpec(x.shape, lambda i: (0, 0)) for x in acc_cts]
    out_specs = [pl.BlockSpec((t, x.shape[1]), lambda i: (i, 0)) for x in rows[:n_diff]]
    out_specs += [pl.BlockSpec(x.shape, lambda i: (0, 0)) for x in params]
    out_shape = [jax.ShapeDtypeStruct(x.shape, F32) for x in rows[:n_diff]]
    out_shape += [jax.ShapeDtypeStruct(x.shape, F32) for x in params]
    res = pl.pallas_call(body, name=name, grid=(n,), in_specs=in_specs, out_specs=out_specs, out_shape=out_shape,
                         compiler_params=_cparams(("arbitrary",)))(*rows, *params, *row_cts, *acc_cts)
    return tuple(res[:n_diff]), tuple(res[n_diff:])


def make_rowwise(fn, name, out_dims, acc_dims=(), n_nondiff=0, tile=256):
    out_dtypes = (F32,) * len(out_dims)

    @jax.custom_vjp
    def op(rows, params):
        return _rw_fwd(fn, name, rows, params, out_dims, out_dtypes, acc_dims, tile)

    def fwd(rows, params):
        return op(rows, params), (rows, params)

    def bwd(res, cts):
        rows, params = res
        n_diff = len(rows) - n_nondiff
        d_rows, d_params = _rw_bwd(fn, name + "_bwd", rows, params, cts[:len(out_dims)], cts[len(out_dims):],
                                   n_diff, tile)
        d_rows = d_rows + tuple(jnp.zeros_like(x) for x in rows[n_diff:])
        return d_rows, d_params

    op.defvjp(fwd, bwd)
    return op


def _mm(a, b, mode, name):
    if mode == "nn":
        (m, k), (_, n) = a.shape, b.shape
    elif mode == "nt":
        (m, k), (n, _) = a.shape, b.shape
    else:
        (k, m), (_, n) = a.shape, b.shape
    tm = _pick(m, 512)
    tn = _pick(n, 1152)
    tk = _pick(k, 1152) if mode != "tn" else _pick(k, 512)
    nk = k // tk

    def body(a_ref, b_ref, o_ref, acc_ref):
        kk = pl.program_id(2)

        @pl.when(kk == 0)
        def _():
            acc_ref[...] = jnp.zeros_like(acc_ref)

        acc_ref[...] += _dot_raw(a_ref[...], b_ref[...], mode, False)

        @pl.when(kk == nk - 1)
        def _():
            o_ref[...] = acc_ref[...]

    if mode == "nn":
        a_spec = pl.BlockSpec((tm, tk), lambda i, j, kk: (i, kk))
        b_spec = pl.BlockSpec((tk, tn), lambda i, j, kk: (kk, j))
    elif mode == "nt":
        a_spec = pl.BlockSpec((tm, tk), lambda i, j, kk: (i, kk))
        b_spec = pl.BlockSpec((tn, tk), lambda i, j, kk: (j, kk))
    else:
        a_spec = pl.BlockSpec((tk, tm), lambda i, j, kk: (kk, i))
        b_spec = pl.BlockSpec((tk, tn), lambda i, j, kk: (kk, j))
    return pl.pallas_call(
        body, name=name, grid=(m // tm, n // tn, nk), in_specs=[a_spec, b_spec],
        out_specs=pl.BlockSpec((tm, tn), lambda i, j, kk: (i, j)),
        out_shape=jax.ShapeDtypeStruct((m, n), F32), scratch_shapes=[pltpu.VMEM((tm, tn), F32)],
        compiler_params=_cparams(("parallel", "parallel", "arbitrary")))(a, b)


def make_linear(name):
    @jax.custom_vjp
    def op(a, w):
        return _mm(a, w.astype(BF16), "nn", name)

    def fwd(a, w):
        w16 = w.astype(BF16)
        return _mm(a, w16, "nn", name), (a, w16)

    def bwd(res, g):
        a, w16 = res
        return _mm(g, w16, "nt", name + "_dx"), _mm(a, g, "tn", name + "_dw")

    op.defvjp(fwd, bwd)
    return op


def _mla_prep_fn(qraw, kvraw, kr, t1, t2):
    kr_rot = kr * t1 + roll_half(kr) * t2
    qs, ks = [], []
    for h in range(MLA_HEADS):
        q_r = qraw[:, h * 256 + 128:(h + 1) * 256]
        qs += [qraw[:, h * 256:h * 256 + 128], q_r * t1 + roll_half(q_r) * t2]
        ks += [kvraw[:, h * 128:(h + 1) * 128], kr_rot]
    return jnp.concatenate(qs, axis=1), jnp.concatenate(ks, axis=1), kvraw[:, 512:]


def _flash_tile(s):
    return 512 if s >= 2048 else 128


FLASH_SCALE = (MLA_NOPE + MLA_ROPE) ** -0.5
STAT_ROWS = 8


def _as_rows(col_b):
    ones = jnp.full((STAT_ROWS, LANES), 1.0 / LANES, F32)
    return _dot_raw(ones, col_b, "nt", True)


def _flash_fwd(qf, kf, vf):
    s = qf.shape[0]
    t = _flash_tile(s)
    nb = s // t

    def body(q_ref, k_ref, v_ref, o_ref, lse_ref, m_sc, l_sc, acc_sc):
        i = pl.program_id(1)
        m_sc[...] = jnp.full_like(m_sc, -1e30)
        l_sc[...] = jnp.zeros_like(l_sc)
        acc_sc[...] = jnp.zeros_like(acc_sc)
        q = q_ref[...]

        def step(j, on_diagonal):
            rows = pl.ds(pl.multiple_of(j * t, t), t)
            sc = _dot_raw(q, k_ref[rows, :], "nt", False) * FLASH_SCALE
            if on_diagonal:
                keep = lax.broadcasted_iota(jnp.int32, (t, t), 1) <= lax.broadcasted_iota(jnp.int32, (t, t), 0)
                sc = jnp.where(keep, sc, -1e30)
            m_old = m_sc[...]
            m_new = jnp.maximum(m_old, jnp.max(sc, axis=1, keepdims=True))
            alpha = jnp.exp(m_old - m_new)
            p = jnp.exp(sc - m_new)
            l_sc[...] = alpha * l_sc[...] + jnp.sum(p, axis=1, keepdims=True)
            acc_sc[...] = alpha * acc_sc[...] + _dot_raw(p, v_ref[rows, :], "nn", False)
            m_sc[...] = m_new

        def two_steps(p, carry):
            step(2 * p, False)
            step(2 * p + 1, False)
            return carry

        lax.fori_loop(0, i // 2, two_steps, 0)

        @pl.when(i % 2 == 1)
        def _():
            step(i - 1, False)

        step(i, True)
        o_ref[...] = acc_sc[...] / l_sc[...]
        lse_ref[0] = _as_rows(jnp.broadcast_to(m_sc[...] + jnp.log(l_sc[...]), (t, LANES)))

    return pl.pallas_call(
        body, name="flash_fwd", grid=(MLA_HEADS, nb),
        in_specs=[pl.BlockSpec((t, 256), lambda h, i: (i, h)), pl.BlockSpec((s, 256), lambda h, i: (0, h)),
                  pl.BlockSpec((s, 128), lambda h, i: (0, h))],
        out_specs=[pl.BlockSpec((t, 128), lambda h, i: (i, h)),
                   pl.BlockSpec((1, STAT_ROWS, t), lambda h, i: (h, 0, i))],
        out_shape=[jax.ShapeDtypeStruct((s, MLA_WIDTH), F32),
                   jax.ShapeDtypeStruct((MLA_HEADS, STAT_ROWS, s), F32)],
        scratch_shapes=[pltpu.VMEM((t, 1), F32), pltpu.VMEM((t, 1), F32), pltpu.VMEM((t, 128), F32)],
        compiler_params=_cparams(("parallel", "arbitrary")))(qf, kf, vf)


def _flash_bwd_prep(o, do):
    s = o.shape[0]
    t = _flash_tile(s)

    def body(o_ref, do_ref, dl_ref, do16_ref):
        do_t = do_ref[...]
        delta = jnp.sum(do_t * o_ref[...], axis=1, keepdims=True)
        dl_ref[0] = _as_rows(jnp.broadcast_to(delta, (t, LANES)))
        do16_ref[...] = do_t.astype(BF16)

    blk = pl.BlockSpec((t, 128), lambda h, i: (i, h))
    return pl.pallas_call(
        body, name="flash_bwd_prep", grid=(MLA_HEADS, s // t), in_specs=[blk, blk],
        out_specs=[pl.BlockSpec((1, STAT_ROWS, t), lambda h, i: (h, 0, i)), blk],
        out_shape=[jax.ShapeDtypeStruct((MLA_HEADS, STAT_ROWS, s), F32), jax.ShapeDtypeStruct((s, MLA_WIDTH), BF16)],
        compiler_params=_cparams(("parallel", "parallel")))(o, do)


def _flash_bwd(qf, kf, vf, lse, delta, do16):
    s = qf.shape[0]
    t = _flash_tile(s)
    nb = s // t

    def body(q_ref, k_ref, v_ref, lse_ref, dl_ref, do_ref, dq_ref, dk_ref, dv_ref):
        j = pl.program_id(1)

        @pl.when(j == 0)
        def _():
            dq_ref[...] = jnp.zeros_like(dq_ref)

        dk_ref[...] = jnp.zeros_like(dk_ref)
        dv_ref[...] = jnp.zeros_like(dv_ref)
        k = k_ref[...]
        v = v_ref[...]

        def step(i, on_diagonal):
            rows = pl.ds(pl.multiple_of(i * t, t), t)
            q = q_ref[rows, :]
            do_t = do_ref[rows, :]
            st = _dot_raw(k, q, "nt", False) * FLASH_SCALE - lse_ref[0, 0:1, rows]
            if on_diagonal:
                keep = lax.broadcasted_iota(jnp.int32, (t, t), 0) <= lax.broadcasted_iota(jnp.int32, (t, t), 1)
                st = jnp.where(keep, st, -1e30)
            pt = jnp.exp(st)
            dst = pt * (_dot_raw(v, do_t, "nt", False) - dl_ref[0, 0:1, rows])
            dv_ref[...] += _dot_raw(pt, do_t, "nn", False)
            dk_ref[...] += _dot_raw(dst, q, "nn", False) * FLASH_SCALE
            dq_ref[rows, :] += _dot_raw(dst, k, "tn", False) * FLASH_SCALE

        def two_steps(p, carry):
            step(j + 1 + 2 * p, False)
            step(j + 2 + 2 * p, False)
            return carry

        step(j, True)
        below = nb - 1 - j
        lax.fori_loop(0, below // 2, two_steps, 0)

        @pl.when(below % 2 == 1)
        def _():
            step(nb - 1, False)

    stat = pl.BlockSpec((1, STAT_ROWS, s), lambda h, j: (h, 0, 0))
    return pl.pallas_call(
        body, name="flash_bwd", grid=(MLA_HEADS, nb),
        in_specs=[pl.BlockSpec((s, 256), lambda h, j: (0, h)), pl.BlockSpec((t, 256), lambda h, j: (j, h)),
                  pl.BlockSpec((t, 128), lambda h, j: (j, h)), stat, stat, pl.BlockSpec((s, 128), lambda h, j: (0, h))],
        out_specs=[pl.BlockSpec((s, 256), lambda h, j: (0, h)), pl.BlockSpec((t, 256), lambda h, j: (j, h)),
                   pl.BlockSpec((t, 128), lambda h, j: (j, h))],
        out_shape=[jax.ShapeDtypeStruct((s, 1024), F32), jax.ShapeDtypeStruct((s, 1024), F32),
                   jax.ShapeDtypeStruct((s, MLA_WIDTH), F32)],
        compiler_params=_cparams(("parallel", "arbitrary")))(qf, kf, vf, lse, delta, do16)


def _mla_prep(qraw, kvraw, kr, t1, t2):
    return _rw_fwd(_mla_prep_fn, "mla_prep", (qraw, kvraw, kr, t1, t2), (), (1024, 1024, 512), (BF16, BF16, BF16), (),
                   256)


@jax.custom_vjp
def mla_attention(qraw, kvraw, kr, t1, t2):
    return _flash_fwd(*_mla_prep(qraw, kvraw, kr, t1, t2))[0]


def _mla_attention_fwd(qraw, kvraw, kr, t1, t2):
    qf, kf, vf = _mla_prep(qraw, kvraw, kr, t1, t2)
    o, lse = _flash_fwd(qf, kf, vf)
    return o, (qraw, kvraw, kr, t1, t2, qf, kf, vf, o, lse)


def _mla_attention_bwd(res, do):
    qraw, kvraw, kr, t1, t2, qf, kf, vf, o, lse = res
    delta, do16 = _flash_bwd_prep(o, do)
    dq, dk, dv = _flash_bwd(qf, kf, vf, lse, delta, do16)
    (dqraw, dkvraw, dkr), _ = _rw_bwd(_mla_prep_fn, "mla_prep_bwd", (qraw, kvraw, kr, t1, t2), (), (dq, dk, dv), (),
                                      3, 256)
    return dqraw, dkvraw, dkr, jnp.zeros_like(t1), jnp.zeros_like(t2)


mla_attention.defvjp(_mla_attention_fwd, _mla_attention_bwd)


def _lane_pick(x, lane):
    ids = lax.broadcasted_iota(jnp.int32, x.shape, 1)
    col = jnp.sum(jnp.where(ids == lane, x, 0.0), axis=1, keepdims=True)
    return jnp.broadcast_to(col, x.shape)


def _gdn_prep_fn(x0, x1, x2, x3, ab, w0, w1, w2, w3, a_log, dt_bias):
    qkv = _silu(x0 * w0 + x1 * w1 + x2 * w2 + x3 * w3)
    g_all = -jnp.exp(a_log) * _softplus(ab + dt_bias)
    beta_all = _sigmoid(ab)
    qs, ks, gs, bs = [], [], [], []
    for h in range(GDN_HEADS):
        q = qkv[:, h * 128:(h + 1) * 128]
        k = qkv[:, 512 + h * 128:512 + (h + 1) * 128]
        qs.append(q * lax.rsqrt(jnp.sum(q * q, axis=-1, keepdims=True) + NORM_EPS) * (GDN_DK ** -0.5))
        ks.append(k * lax.rsqrt(jnp.sum(k * k, axis=-1, keepdims=True) + NORM_EPS))
        gs.append(_lane_pick(g_all, h))
        bs.append(_lane_pick(beta_all, GDN_HEADS + h))
    cat = lambda xs: jnp.concatenate(xs, axis=1)
    return cat(qs), cat(ks), qkv[:, 1024:], cat(gs), cat(bs)


@jax.custom_vjp
def _unit_lower_inverse(lms):
    c = lms[0].shape[0]
    row = lax.broadcasted_iota(jnp.int32, (c, c), 0)
    col = lax.broadcasted_iota(jnp.int32, (c, c), 1)
    ts = [(row == col).astype(F32) - jnp.where((row >> 1) == (col >> 1), lm, 0.0) for lm in lms]
    for level in range(1, int(math.log2(c))):
        below = ((row >> (level + 1)) == (col >> (level + 1))) & ((row >> level) != (col >> level))
        mids = [_dot_split(t, jnp.where(below, lm, 0.0)) for t, lm in zip(ts, lms)]
        ts = [t - _dot_split(mid, t) for t, mid in zip(ts, mids)]
    return tuple(ts)


def _uli_fwd(lms):
    ts = _unit_lower_inverse(lms)
    return ts, ts


def _uli_bwd(ts, gs):
    mids = [bdot(t, g, "tn") for t, g in zip(ts, gs)]
    return (tuple(-bdot(mid, t, "nt") for t, mid in zip(ts, mids)),)


_unit_lower_inverse.defvjp(_uli_fwd, _uli_bwd)


def _gdn_chunk_fn(qs, ks, vs, gbs, bbs, s0s):
    heads = range(len(qs))
    c = qs[0].shape[0]
    row = lax.broadcasted_iota(jnp.int32, (c, c), 0)
    col = lax.broadcasted_iota(jnp.int32, (c, c), 1)
    incl, strict = row >= col, row > col
    tri = incl.astype(F32)
    gc = [bdot(tri, gbs[h], "nn", True) for h in heads]
    decay = [jnp.exp(jnp.where(incl, gc[h] - gc[h].T, -1e30)) for h in heads]
    g_last = [jnp.sum(gbs[h], axis=0, keepdims=True) for h in heads]
    eg = [jnp.exp(gc[h]) for h in heads]
    kb = [ks[h] * bbs[h] for h in heads]
    ts = _unit_lower_inverse(tuple(jnp.where(strict, bdot(kb[h], ks[h], "nt") * decay[h], 0.0) for h in heads))
    u = [bdot(ts[h], vs[h] * bbs[h]) for h in heads]
    w = [bdot(ts[h], kb[h] * eg[h]) for h in heads]
    qk = [bdot(qs[h], ks[h], "nt") * decay[h] for h in heads]
    v_new = [u[h] - bdot(w[h], s0s[h]) for h in heads]
    o = [bdot(qs[h] * eg[h], s0s[h]) + bdot(qk[h], v_new[h]) for h in heads]
    s1 = [s0s[h] * jnp.exp(g_last[h]) + bdot(ks[h] * jnp.exp(g_last[h] - gc[h]), v_new[h], "tn") for h in heads]
    return tuple(o), tuple(s1)


def _head_tiles(ref):
    return tuple(ref[:, h * 128:(h + 1) * 128] for h in range(GDN_HEADS))


def _gdn_fwd(q, k, v, gb, bb):
    s = q.shape[0]
    c = min(GDN_CHUNK, s)
    nc = s // c

    def body(q_ref, k_ref, v_ref, g_ref, b_ref, o_ref, st_ref, s_sc):
        @pl.when(pl.program_id(0) == 0)
        def _():
            s_sc[...] = jnp.zeros_like(s_sc)

        s0s = tuple(s_sc[h] for h in range(GDN_HEADS))
        for h in range(GDN_HEADS):
            st_ref[h, 0] = s0s[h]
        os, s1s = _gdn_chunk_fn(*[_head_tiles(ref) for ref in (q_ref, k_ref, v_ref, g_ref, b_ref)], s0s)
        for h in range(GDN_HEADS):
            o_ref[:, h * 128:(h + 1) * 128] = os[h]
            s_sc[h] = s1s[h]

    blk = pl.BlockSpec((c, GDN_WIDTH), lambda n: (n, 0))
    return pl.pallas_call(
        body, name="gdn_fwd", grid=(nc,), in_specs=[blk] * 5,
        out_specs=[blk, pl.BlockSpec((GDN_HEADS, 1, 128, 128), lambda n: (0, n, 0, 0))],
        out_shape=[jax.ShapeDtypeStruct((s, GDN_WIDTH), F32), jax.ShapeDtypeStruct((GDN_HEADS, nc, 128, 128), F32)],
        scratch_shapes=[pltpu.VMEM((GDN_HEADS, 128, 128), F32)],
        compiler_params=_cparams(("arbitrary",)))(q, k, v, gb, bb)


def _gdn_bwd(q, k, v, gb, bb, states, do):
    s = q.shape[0]
    c = min(GDN_CHUNK, s)
    nc = s // c

    def body(q_ref, k_ref, v_ref, g_ref, b_ref, st_ref, do_ref, dq_ref, dk_ref, dv_ref, dg_ref, db_ref, ds_sc):
        @pl.when(pl.program_id(0) == 0)
        def _():
            ds_sc[...] = jnp.zeros_like(ds_sc)

        s0s = tuple(st_ref[h, 0] for h in range(GDN_HEADS))
        _, vjp = jax.vjp(_gdn_chunk_fn, *[_head_tiles(ref) for ref in (q_ref, k_ref, v_ref, g_ref, b_ref)], s0s)
        *d_tiles, ds0s = vjp((_head_tiles(do_ref), tuple(ds_sc[h] for h in range(GDN_HEADS))))
        for h in range(GDN_HEADS):
            for ref, d in zip((dq_ref, dk_ref, dv_ref, dg_ref, db_ref), d_tiles):
                ref[:, h * 128:(h + 1) * 128] = d[h]
            ds_sc[h] = ds0s[h]

    blk = pl.BlockSpec((c, GDN_WIDTH), lambda n: (nc - 1 - n, 0))
    return pl.pallas_call(
        body, name="gdn_bwd", grid=(nc,),
        in_specs=[blk] * 5 + [pl.BlockSpec((GDN_HEADS, 1, 128, 128), lambda n: (0, nc - 1 - n, 0, 0)), blk],
        out_specs=[blk] * 5, out_shape=[jax.ShapeDtypeStruct((s, GDN_WIDTH), F32)] * 5,
        scratch_shapes=[pltpu.VMEM((GDN_HEADS, 128, 128), F32)],
        compiler_params=_cparams(("arbitrary",)))(q, k, v, gb, bb, states, do)


@jax.custom_vjp
def gdn_core(q, k, v, gb, bb):
    return _gdn_fwd(q, k, v, gb, bb)[0]


def _gdn_core_fwd(q, k, v, gb, bb):
    o, states = _gdn_fwd(q, k, v, gb, bb)
    return o, (q, k, v, gb, bb, states)


def _gdn_core_bwd(res, do):
    return tuple(_gdn_bwd(*res, do))


gdn_core.defvjp(_gdn_core_fwd, _gdn_core_bwd)


def _cact_fn(c):
    return (_silu(c),)


def _prenorm_fn(x, w, scale_raw, scale_b, shift_raw, shift_b):
    return (_rms(x, w) * (1.0 + scale_raw + scale_b) + shift_raw + shift_b,)


def _rmsnorm_fn(x, w):
    return (_rms(x, w),)


def _gate_fn(o, z):
    return (o * _silu(z),)


def _gdn_out_fn(o, z, w):
    parts = [_rms(o[:, h * 128:(h + 1) * 128], w) for h in range(GDN_HEADS)]
    return (jnp.concatenate(parts, axis=1) * _silu(z),)


def _post_fn(x, y, w, gate_raw, gate_b):
    return (x + (gate_raw + gate_b) * _rms(y, w),)


def _loss_fn(y, tgt):
    err = y - tgt
    part = jnp.sum(0.5 * jnp.mean(err * err, axis=-1, keepdims=True), axis=0, keepdims=True)
    return (jnp.broadcast_to(part, (1, LANES)),)


_OPS = dict(
    cact=make_rowwise(_cact_fn, "c_act", (D_MODEL,), tile=16),
    prenorm=make_rowwise(_prenorm_fn, "prenorm", (D_MODEL,)),
    qnorm=make_rowwise(_rmsnorm_fn, "q_norm", (MLA_Q_RANK,)),
    kvnorm=make_rowwise(_rmsnorm_fn, "kv_norm", (MLA_KV_RANK,)),
    gate=make_rowwise(_gate_fn, "mla_gate", (MLA_WIDTH,)),
    gdn_prep=make_rowwise(_gdn_prep_fn, "gdn_prep", (512, 512, 512, 512, 512), tile=128),
    gdn_out=make_rowwise(_gdn_out_fn, "gdn_out", (GDN_WIDTH,)),
    post=make_rowwise(_post_fn, "postnorm", (D_MODEL,)),
    loss=make_rowwise(_loss_fn, "loss", (), acc_dims=(LANES,), n_nondiff=1),
    lin_mod=make_linear("lin_mod"), lin_in=make_linear("lin_in"), lin_q=make_linear("lin_q"),
    lin_kv=make_linear("lin_kv"), lin_out=make_linear("lin_out"),
)


def _swap_halves(w):
    half = w.shape[-1] // 2
    return jnp.concatenate([w[..., half:], w[..., :half]], axis=-1)


def _w_in_ext(w):
    k_pe = w[:, 640:704]
    return jnp.concatenate([w[:, :640], k_pe, _swap_halves(k_pe), w[:, 704:2752], w[:, 2752:2760],
                            jnp.zeros((w.shape[0], LANES - 8), w.dtype), w[:, 2760:]], axis=1)


def _q_up_ext(w):
    parts = []
    for h in range(MLA_HEADS):
        rope = w[:, h * 192 + 128:(h + 1) * 192]
        parts += [w[:, h * 192:h * 192 + 128], rope, _swap_halves(rope)]
    return jnp.concatenate(parts, axis=1)


def _kv_up_perm(w):
    ks = [w[:, h * 256:h * 256 + 128] for h in range(MLA_HEADS)]
    vs = [w[:, h * 256 + 128:(h + 1) * 256] for h in range(MLA_HEADS)]
    return jnp.concatenate(ks + vs, axis=1)


def _delay(x, d):
    if d == 0:
        return x
    return jnp.concatenate([jnp.zeros((d, x.shape[1]), x.dtype), x[:-d]], axis=0)


def _pad_lanes(v):
    return jnp.pad(v, (0, LANES - v.shape[0]))[None, :]


def _local_loss(weights, x, c, positions, target):
    s = x.shape[0]
    half = MLA_ROPE // 2
    inv_freq = jnp.power(ROPE_THETA, -jnp.arange(half, dtype=F32) * 2.0 / MLA_ROPE)
    ang = positions.astype(F32)[:, None] * inv_freq
    cos, sin, zero = jnp.cos(ang), jnp.sin(ang), jnp.zeros((s, 2 * half), F32)
    t1 = jnp.concatenate([cos, cos, zero], axis=1)
    t2 = jnp.concatenate([-sin, sin, zero], axis=1)

    (c_act,) = _OPS["cact"]((jnp.pad(c, ((0, 15), (0, 0))),), ())
    for l in range(DEPTH):
        mod = _OPS["lin_mod"](c_act, weights["w_mod"][l])[0:1]
        b = weights["b_mod"][l][None, :]
        shift_raw, scale_raw, gate_raw = mod[:, :1024], mod[:, 1024:2048], mod[:, 2048:]
        shift_b, scale_b, gate_b = b[:, :1024], b[:, 1024:2048], b[:, 2048:]
        (h,) = _OPS["prenorm"]((x,), (weights["pre_norm_w"][l][None], scale_raw, scale_b, shift_raw, shift_b))
        proj = _OPS["lin_in"](h, _w_in_ext(weights["w_in"][l]))
        (qn,) = _OPS["qnorm"]((proj[:, E_QLAT:E_KVLAT],), (weights["mla_q_norm_w"][l][None],))
        qraw = _OPS["lin_q"](qn, _q_up_ext(weights["mla_q_up"][l]))
        (kvn,) = _OPS["kvnorm"]((proj[:, E_KVLAT:E_KR],), (weights["mla_kv_norm_w"][l][None],))
        kvraw = _OPS["lin_kv"](kvn, _kv_up_perm(weights["mla_kv_up"][l]))
        o_mla = mla_attention(qraw, kvraw, proj[:, E_KR:E_ZMLA], t1, t2)
        (y_mla,) = _OPS["gate"]((o_mla, proj[:, E_ZMLA:E_QKV]), ())
        qkv = proj[:, E_QKV:E_AB]
        cw = weights["gdn_conv_w"][l]
        taps = tuple(_delay(qkv, GDN_CONV - 1 - j) for j in range(GDN_CONV))
        params = tuple(cw[j][None] for j in range(GDN_CONV))
        params += (_pad_lanes(weights["gdn_a_log"][l]), _pad_lanes(weights["gdn_dt_bias"][l]))
        qg, kg, vg, gb, bb = _OPS["gdn_prep"](taps + (proj[:, E_AB:E_ZGDN],), params)
        o_gdn = gdn_core(qg, kg, vg, gb, bb)
        (y_gdn,) = _OPS["gdn_out"]((o_gdn, proj[:, E_ZGDN:]), (weights["gdn_o_norm_w"][l][None],))
        y = _OPS["lin_out"](jnp.concatenate([y_mla, y_gdn], axis=1), weights["w_out"][l])
        (x,) = _OPS["post"]((x, y), (weights["post_norm_w"][l][None], gate_raw, gate_b))
    (acc,) = _OPS["loss"]((x, target), ())
    return acc[0, 0]


def _chip_index():
    return 2 * lax.axis_index("x") + lax.axis_index("y")


def _other_chips(x, y):
    return [(1 - x, y), (x, 1 - y), (1 - x, 1 - y)]


def _any_spec():
    return pl.BlockSpec(memory_space=pl.ANY)


def _half(ref, hc):
    n = ref.shape[0] // 2
    return ref.at[pl.ds(hc * n, n)]


def ag_weights(shards):
    n = len(shards)

    def body(*refs):
        ins, outs = refs[:n], refs[n:2 * n]
        send_sems, recv_sems, local_sems = refs[2 * n:]
        x, y, c = lax.axis_index("x"), lax.axis_index("y"), lax.axis_index("c")
        sibling = (x, y, 1 - c)
        chips = _other_chips(x, y)

        def copy(t, k, src, chip_xy, hc, to):
            return pltpu.make_async_remote_copy(
                src_ref=src, dst_ref=_half(outs[t].at[2 * chip_xy[0] + chip_xy[1]], hc),
                send_sem=send_sems.at[6 * t + k], recv_sem=recv_sems.at[6 * t + k], device_id=to, device_id_type=MESH)

        mine = [pltpu.make_async_copy(ins[t], outs[t].at[2 * x + y], local_sems.at[t]) for t in range(n)]
        first = [copy(t, k, _half(ins[t], c), (x, y), c, (*chip, c)) for k, chip in enumerate(chips) for t in range(n)]
        for cp in mine + first:
            cp.start()
        passed = []
        for k, chip in enumerate(chips):
            for t in range(n):
                landed = _half(outs[t].at[2 * chip[0] + chip[1]], c)
                copy(t, k, landed, chip, c, (x, y, c)).wait_recv()
                passed.append(copy(t, 3 + k, landed, chip, c, sibling))
                passed[-1].start()
        for k, chip in enumerate(chips):
            for t in range(n):
                copy(t, 3 + k, _half(ins[t], c), chip, 1 - c, (x, y, c)).wait_recv()
        for cp in first + passed:
            cp.wait_send()
        for cp in mine:
            cp.wait()

    return pl.pallas_call(
        body, name="ag_weights", in_specs=[_any_spec()] * n, out_specs=[_any_spec()] * n,
        out_shape=[jax.ShapeDtypeStruct((N_CHIPS,) + a.shape, a.dtype) for a in shards],
        scratch_shapes=[pltpu.SemaphoreType.DMA((6 * n,)), pltpu.SemaphoreType.DMA((6 * n,)),
                        pltpu.SemaphoreType.DMA((n,))],
        compiler_params=pltpu.CompilerParams(has_side_effects=True))(*shards)


def rs_pair(gs):
    n = len(gs)

    def body(*refs):
        ins, outs = refs[:n], refs[n:2 * n]
        send_sems, recv_sems = refs[2 * n:]
        x, y, c = lax.axis_index("x"), lax.axis_index("y"), lax.axis_index("c")
        lh = [g.shape[1] // 2 for g in gs]
        copies = [pltpu.make_async_remote_copy(
            src_ref=ins[t].at[:, pl.ds((1 - c) * lh[t], lh[t])], dst_ref=outs[t], send_sem=send_sems.at[t],
            recv_sem=recv_sems.at[t], device_id=(x, y, 1 - c), device_id_type=MESH) for t in range(n)]
        for cp in copies:
            cp.start()
        for cp in copies:
            cp.wait()

    return pl.pallas_call(
        body, name="rs_pair", in_specs=[_any_spec()] * n, out_specs=[_any_spec()] * n,
        out_shape=[jax.ShapeDtypeStruct((N_CHIPS, g.shape[1] // 2) + g.shape[2:], F32) for g in gs],
        scratch_shapes=[pltpu.SemaphoreType.DMA((n,)), pltpu.SemaphoreType.DMA((n,))],
        compiler_params=pltpu.CompilerParams(has_side_effects=True))(*gs)


def rs_cross(pairs):
    n = len(pairs)

    def body(*refs):
        ins, outs = refs[:n], refs[n:2 * n]
        send_sems, recv_sems = refs[2 * n:]
        x, y, c = lax.axis_index("x"), lax.axis_index("y"), lax.axis_index("c")
        copies = []
        for k, chip in enumerate(_other_chips(x, y)):
            for t in range(n):
                copies.append(pltpu.make_async_remote_copy(
                    src_ref=ins[t].at[2 * chip[0] + chip[1]], dst_ref=outs[t].at[k], send_sem=send_sems.at[3 * t + k],
                    recv_sem=recv_sems.at[3 * t + k], device_id=(*chip, c), device_id_type=MESH))
        for cp in copies:
            cp.start()
        for cp in copies:
            cp.wait()

    return pl.pallas_call(
        body, name="rs_cross", in_specs=[_any_spec()] * n, out_specs=[_any_spec()] * n,
        out_shape=[jax.ShapeDtypeStruct((3,) + p.shape[1:], F32) for p in pairs],
        scratch_shapes=[pltpu.SemaphoreType.DMA((3 * n,)), pltpu.SemaphoreType.DMA((3 * n,))],
        compiler_params=pltpu.CompilerParams(has_side_effects=True))(*pairs)


def rs_share(halves):
    n = len(halves)

    def body(*refs):
        ins, outs = refs[:n], refs[n:2 * n]
        send_sems, recv_sems, local_sems = refs[2 * n:]
        x, y, c = lax.axis_index("x"), lax.axis_index("y"), lax.axis_index("c")
        mine = [pltpu.make_async_copy(ins[t], _half(outs[t], c), local_sems.at[t]) for t in range(n)]
        sends = [pltpu.make_async_remote_copy(
            src_ref=ins[t], dst_ref=_half(outs[t], c), send_sem=send_sems.at[t], recv_sem=recv_sems.at[t],
            device_id=(x, y, 1 - c), device_id_type=MESH) for t in range(n)]
        for cp in mine + sends:
            cp.start()
        for t in range(n):
            pltpu.make_async_remote_copy(
                src_ref=ins[t], dst_ref=_half(outs[t], 1 - c), send_sem=send_sems.at[t], recv_sem=recv_sems.at[t],
                device_id=(x, y, 1 - c), device_id_type=MESH).wait_recv()
        for cp in sends:
            cp.wait_send()
        for cp in mine:
            cp.wait()

    return pl.pallas_call(
        body, name="rs_share", in_specs=[_any_spec()] * n, out_specs=[_any_spec()] * n,
        out_shape=[jax.ShapeDtypeStruct((2 * h.shape[0],) + h.shape[1:], F32) for h in halves],
        scratch_shapes=[pltpu.SemaphoreType.DMA((n,)), pltpu.SemaphoreType.DMA((n,)), pltpu.SemaphoreType.DMA((n,))],
        compiler_params=pltpu.CompilerParams(has_side_effects=True))(*halves)


def _row_tile(rows, target):
    best = None
    for t in range(8, min(rows, target) + 1, 8):
        if rows % t == 0:
            best = t
    return rows if best is None else best


TILE_BYTES = 2 * 1024 * 1024


def _flat_rows(shape):
    rows = int(np.prod(shape[1:-1]))
    cols_padded = -(-shape[-1] // LANES) * LANES
    return rows, _row_tile(rows, max(8, TILE_BYTES // (4 * cols_padded)))


def pair_add(g, from_sibling):
    cols = g.shape[-1]
    rph, t = _flat_rows(from_sibling.shape)
    nt = rph // t
    c_arr = lax.axis_index("c").astype(jnp.int32).reshape(1)

    def body(c_ref, a_ref, b_ref, o_ref):
        o_ref[...] = a_ref[...] + b_ref[...]

    out = pl.pallas_call(
        body, name="pair_add",
        grid_spec=pltpu.PrefetchScalarGridSpec(
            num_scalar_prefetch=1, grid=(N_CHIPS, nt),
            in_specs=[pl.BlockSpec((t, cols), lambda j, i, c_ref: (j * 2 * nt + c_ref[0] * nt + i, 0)),
                      pl.BlockSpec((t, cols), lambda j, i, c_ref: (j * nt + i, 0))],
            out_specs=pl.BlockSpec((t, cols), lambda j, i, c_ref: (j * nt + i, 0))),
        out_shape=jax.ShapeDtypeStruct((N_CHIPS * rph, cols), F32),
        compiler_params=_cparams(("parallel", "parallel")))(c_arr, g.reshape(-1, cols), from_sibling.reshape(-1, cols))
    return out.reshape(from_sibling.shape)


def chip_add(pairs, received):
    cols = pairs.shape[-1]
    rph, t = _flat_rows(pairs.shape)
    nt = rph // t
    j_arr = _chip_index().astype(jnp.int32).reshape(1)
    r2 = received.reshape(-1, cols)

    def body(j_ref, a_ref, r0_ref, r1_ref, r2_ref, o_ref):
        o_ref[...] = (a_ref[...] + r0_ref[...]) + (r1_ref[...] + r2_ref[...])

    out = pl.pallas_call(
        body, name="chip_add",
        grid_spec=pltpu.PrefetchScalarGridSpec(
            num_scalar_prefetch=1, grid=(nt,),
            in_specs=[pl.BlockSpec((t, cols), lambda i, j_ref: (j_ref[0] * nt + i, 0)),
                      pl.BlockSpec((t, cols), lambda i, j_ref: (i, 0)),
                      pl.BlockSpec((t, cols), lambda i, j_ref: (nt + i, 0)),
                      pl.BlockSpec((t, cols), lambda i, j_ref: (2 * nt + i, 0))],
            out_specs=pl.BlockSpec((t, cols), lambda i, j_ref: (i, 0))),
        out_shape=jax.ShapeDtypeStruct((rph, cols), F32),
        compiler_params=_cparams(("parallel",)))(j_arr, pairs.reshape(-1, cols), r2, r2, r2)
    return out.reshape(pairs.shape[1:])


def reduce_grads(gs):
    pairs = [pair_add(g, r) for g, r in zip(gs, rs_pair(gs))]
    return rs_share([chip_add(p, r) for p, r in zip(pairs, rs_cross(pairs))])


def adamw(w, g, m, v):
    shape = w.shape
    cols = shape[-1]
    rows = int(np.prod(shape[:-1]))
    flat = lambda a: a.reshape(rows, cols)
    t = _row_tile(rows, 256)

    def body(w_ref, g_ref, m_ref, v_ref, d_ref, mo_ref, vo_ref):
        gv = g_ref[...]
        m_new = ADAM_B1 * m_ref[...] + (1.0 - ADAM_B1) * gv
        v_new = ADAM_B2 * v_ref[...] + (1.0 - ADAM_B2) * (gv * gv)
        m_hat = m_new / (1.0 - ADAM_B1 ** ADAM_STEP)
        v_hat = v_new / (1.0 - ADAM_B2 ** ADAM_STEP)
        d_ref[...] = -ADAM_LR * (m_hat / (jnp.sqrt(v_hat) + ADAM_EPS) + ADAM_WD * w_ref[...])
        mo_ref[...] = m_new
        vo_ref[...] = v_new

    spec = pl.BlockSpec((t, cols), lambda i: (i, 0))
    outs = pl.pallas_call(
        body, name="adamw", grid=(rows // t,), in_specs=[spec] * 4, out_specs=[spec] * 3,
        out_shape=[jax.ShapeDtypeStruct((rows, cols), F32)] * 3,
        compiler_params=_cparams(("parallel",)))(flat(w), flat(g), flat(m), flat(v))
    return tuple(o.reshape(shape) for o in outs)


SHARDED = (("w_mod", 2), ("w_in", 2), ("mla_q_up", 2), ("mla_kv_up", 2), ("gdn_conv_w", 2), ("w_out", 1))
REPLICATED = ("b_mod", "pre_norm_w", "post_norm_w", "mla_q_norm_w", "mla_kv_norm_w", "gdn_a_log", "gdn_dt_bias",
              "gdn_o_norm_w")
WEIGHT_ORDER = ("w_mod", "b_mod", "pre_norm_w", "post_norm_w", "w_in", "mla_q_norm_w", "mla_q_up", "mla_kv_norm_w",
                "mla_kv_up", "gdn_conv_w", "gdn_a_log", "gdn_dt_bias", "gdn_o_norm_w", "w_out")
EXACT_F32 = ("gdn_conv_w",)
SMALL_ROWS = 48


def _gather_weights(shards):
    names = [name for name, _ in SHARDED]
    gathered = ag_weights([shards[n] if n in EXACT_F32 else shards[n].astype(BF16) for n in names])
    full = {}
    for (name, axis), blk in zip(SHARDED, gathered):
        shp = shards[name].shape
        blk = jnp.moveaxis(blk.astype(F32), 0, axis)
        full[name] = blk.reshape(shp[:axis] + (N_CHIPS * shp[axis],) + shp[axis + 1:])
    return full


def _split_grads(grads):
    pieces = []
    for name, axis in SHARDED:
        g = grads[name]
        shp = g.shape
        g = g.reshape(shp[:axis] + (N_CHIPS, shp[axis] // N_CHIPS) + shp[axis + 1:])
        pieces.append(jnp.moveaxis(g, axis, 0))
    small = jnp.concatenate([grads[name] for name in REPLICATED], axis=1)
    small = jnp.pad(small, ((0, 0), (0, SMALL_ROWS * LANES - small.shape[1]))).reshape(DEPTH, SMALL_ROWS, LANES)
    pieces.append(jnp.broadcast_to(small[None], (N_CHIPS,) + small.shape))
    return pieces


def _unsplit_small(small, rep_shapes):
    flat = small.reshape(DEPTH, SMALL_ROWS * LANES)
    out, off = {}, 0
    for name in REPLICATED:
        size = rep_shapes[name][1]
        out[name] = flat[:, off:off + size]
        off += size
    return out


def kernel(x, c, positions, w_mod, b_mod, pre_norm_w, post_norm_w, w_in, mla_q_norm_w, mla_q_up, mla_kv_norm_w, mla_kv_up, gdn_conv_w, gdn_a_log, gdn_dt_bias, gdn_o_norm_w, w_out, loss_target, m_w_mod, m_b_mod, m_pre_norm_w, m_post_norm_w, m_w_in, m_mla_q_norm_w, m_mla_q_up, m_mla_kv_norm_w, m_mla_kv_up, m_gdn_conv_w, m_gdn_a_log, m_gdn_dt_bias, m_gdn_o_norm_w, m_w_out, v_w_mod, v_b_mod, v_pre_norm_w, v_post_norm_w, v_w_in, v_mla_q_norm_w, v_mla_q_up, v_mla_kv_norm_w, v_mla_kv_up, v_gdn_conv_w, v_gdn_a_log, v_gdn_dt_bias, v_gdn_o_norm_w, v_w_out):
    given = dict(w_mod=w_mod, b_mod=b_mod, pre_norm_w=pre_norm_w, post_norm_w=post_norm_w, w_in=w_in,
                 mla_q_norm_w=mla_q_norm_w, mla_q_up=mla_q_up, mla_kv_norm_w=mla_kv_norm_w, mla_kv_up=mla_kv_up,
                 gdn_conv_w=gdn_conv_w, gdn_a_log=gdn_a_log, gdn_dt_bias=gdn_dt_bias, gdn_o_norm_w=gdn_o_norm_w,
                 w_out=w_out)
    moments_m = dict(w_mod=m_w_mod, b_mod=m_b_mod, pre_norm_w=m_pre_norm_w, post_norm_w=m_post_norm_w, w_in=m_w_in,
                     mla_q_norm_w=m_mla_q_norm_w, mla_q_up=m_mla_q_up, mla_kv_norm_w=m_mla_kv_norm_w,
                     mla_kv_up=m_mla_kv_up, gdn_conv_w=m_gdn_conv_w, gdn_a_log=m_gdn_a_log,
                     gdn_dt_bias=m_gdn_dt_bias, gdn_o_norm_w=m_gdn_o_norm_w, w_out=m_w_out)
    moments_v = dict(w_mod=v_w_mod, b_mod=v_b_mod, pre_norm_w=v_pre_norm_w, post_norm_w=v_post_norm_w, w_in=v_w_in,
                     mla_q_norm_w=v_mla_q_norm_w, mla_q_up=v_mla_q_up, mla_kv_norm_w=v_mla_kv_norm_w,
                     mla_kv_up=v_mla_kv_up, gdn_conv_w=v_gdn_conv_w, gdn_a_log=v_gdn_a_log,
                     gdn_dt_bias=v_gdn_dt_bias, gdn_o_norm_w=v_gdn_o_norm_w, w_out=v_w_out)

    full = _gather_weights({name: given[name] for name, _ in SHARDED})
    for name in REPLICATED:
        full[name] = given[name]
    loss_local, (grads, grad_x) = jax.value_and_grad(_local_loss, argnums=(0, 1))(
        full, x[0], c, positions[0], loss_target[0])
    loss = lax.psum(loss_local, AXES)

    reduced = reduce_grads(_split_grads(grads))
    grad_w = {name: g for (name, _), g in zip(SHARDED, reduced)}
    grad_w.update(_unsplit_small(reduced[-1], {name: given[name].shape for name in REPLICATED}))
    delta, new_m, new_v = {}, {}, {}
    for name in WEIGHT_ORDER:
        delta[name], new_m[name], new_v[name] = adamw(given[name], grad_w[name], moments_m[name], moments_v[name])
    return (loss, grad_x[None], *[grad_w[n] for n in WEIGHT_ORDER], *[delta[n] for n in WEIGHT_ORDER],
            *[new_m[n] for n in WEIGHT_ORDER], *[new_v[n] for n in WEIGHT_ORDER])
```

```python
import functools
import math

import numpy as np
import jax
import jax.numpy as jnp
from jax import lax
from jax.experimental import pallas as pl
from jax.experimental.pallas import tpu as pltpu

F32 = jnp.float32
BF16 = jnp.bfloat16
MESH = pl.DeviceIdType.MESH
AXES = ("x", "y", "c")

D_MODEL = 1024
DEPTH = 4
MLA_HEADS = 4
MLA_NOPE = 128
MLA_ROPE = 64
MLA_V = 128
MLA_Q_RANK = 384
MLA_KV_RANK = 256
MLA_WIDTH = 512
GDN_HEADS = 4
GDN_DK = 128
GDN_WIDTH = 512
GDN_QKV = 1536
GDN_CONV = 4
IN_COLS = 3272
ROPE_THETA = 10000.0
NORM_EPS = 1e-6
ADAM_LR, ADAM_B1, ADAM_B2, ADAM_EPS, ADAM_WD, ADAM_STEP = 0.001, 0.9, 0.999, 1e-08, 0.01, 10

LANES = 128
N_CHIPS = 4
GDN_CHUNK = 128
VMEM_LIMIT = 56 * 1024 * 1024


def _cparams(sem=None):
    if sem is None:
        return pltpu.CompilerParams(vmem_limit_bytes=VMEM_LIMIT)
    return pltpu.CompilerParams(dimension_semantics=sem, vmem_limit_bytes=VMEM_LIMIT)


def _pick(dim, target):
    if dim <= target:
        return dim
    best = None
    for t in range(LANES, target + 1, LANES):
        if dim % t == 0:
            best = t
    assert best is not None, (dim, target)
    return best


_DN = {"nn": (((1,), (0,)), ((), ())), "nt": (((1,), (1,)), ((), ())), "tn": (((0,), (0,)), ((), ()))}


def _dot_raw(a, b, mode, exact):
    if exact:
        return lax.dot_general(a, b, _DN[mode], precision=lax.Precision.HIGHEST, preferred_element_type=F32)
    return lax.dot_general(a.astype(BF16), b.astype(BF16), _DN[mode], preferred_element_type=F32)


def _dot_split(a, b):
    a_hi, b_hi = a.astype(BF16), b.astype(BF16)
    a_lo, b_lo = (a - a_hi.astype(F32)).astype(BF16), (b - b_hi.astype(F32)).astype(BF16)
    dot = lambda u, w: lax.dot_general(u, w, _DN["nn"], preferred_element_type=F32)
    return dot(a_hi, b_hi) + (dot(a_hi, b_lo) + dot(a_lo, b_hi))


@functools.partial(jax.custom_vjp, nondiff_argnums=(2, 3))
def bdot(a, b, mode="nn", exact=False):
    return _dot_raw(a, b, mode, exact)


def _bdot_fwd(a, b, mode, exact):
    return _dot_raw(a, b, mode, exact), (a, b)


def _bdot_bwd(mode, exact, res, g):
    a, b = res
    if mode == "nn":
        return bdot(g, b, "nt", exact), bdot(a, g, "tn", exact)
    if mode == "nt":
        return bdot(g, b, "nn", exact), bdot(g, a, "tn", exact)
    return bdot(b, g, "nt", exact), bdot(a, g, "nn", exact)


bdot.defvjp(_bdot_fwd, _bdot_bwd)


@jax.custom_vjp
def roll_half(x):
    return pltpu.roll(x, 64, 1)


roll_half.defvjp(lambda x: (pltpu.roll(x, 64, 1), None), lambda _, g: (pltpu.roll(g, 64, 1),))


def _sigmoid(x):
    return 1.0 / (1.0 + jnp.exp(-x))


def _silu(x):
    return x * _sigmoid(x)


def _softplus(x):
    return jnp.maximum(x, 0.0) + jnp.log(1.0 + jnp.exp(-jnp.abs(x)))


def _rms(x, w):
    return x * lax.rsqrt(jnp.mean(x * x, axis=-1, keepdims=True) + NORM_EPS) * w


def _rw_fwd(fn, name, rows, params, out_dims, out_dtypes, acc_dims, tile):
    s = rows[0].shape[0]
    t = min(tile, s)
    n = s // t
    nr, npar, no, na = len(rows), len(params), len(out_dims), len(acc_dims)

    def body(*refs):
        r, p = refs[:nr], refs[nr:nr + npar]
        o, a = refs[nr + npar:nr + npar + no], refs[nr + npar + no:]
        outs = fn(*[x[...] for x in r], *[x[...] for x in p])
        for ref, val in zip(o, outs[:no]):
            ref[...] = val.astype(ref.dtype)
        if na:
            @pl.when(pl.program_id(0) == 0)
            def _():
                for ref in a:
                    ref[...] = jnp.zeros_like(ref)
            for ref, val in zip(a, outs[no:]):
                ref[...] += val

    in_specs = [pl.BlockSpec((t, x.shape[1]), lambda i: (i, 0)) for x in rows]
    in_specs += [pl.BlockSpec(x.shape, lambda i: (0, 0)) for x in params]
    out_specs = [pl.BlockSpec((t, d), lambda i: (i, 0)) for d in out_dims]
    out_specs += [pl.BlockSpec((1, d), lambda i: (0, 0)) for d in acc_dims]
    out_shape = [jax.ShapeDtypeStruct((s, d), dt) for d, dt in zip(out_dims, out_dtypes)]
    out_shape += [jax.ShapeDtypeStruct((1, d), F32) for d in acc_dims]
    res = pl.pallas_call(body, name=name, grid=(n,), in_specs=in_specs, out_specs=out_specs, out_shape=out_shape,
                         compiler_params=_cparams(("arbitrary",)))(*rows, *params)
    return tuple(res)


def _rw_bwd(fn, name, rows, params, row_cts, acc_cts, n_diff, tile):
    s = rows[0].shape[0]
    t = min(tile, s)
    n = s // t
    nr, npar, no, na = len(rows), len(params), len(row_cts), len(acc_cts)

    def body(*refs):
        r, p = refs[:nr], refs[nr:nr + npar]
        g, ga = refs[nr + npar:nr + npar + no], refs[nr + npar + no:nr + npar + no + na]
        dr, dp = refs[nr + npar + no + na:nr + npar + no + na + n_diff], refs[nr + npar + no + na + n_diff:]
        _, vjp = jax.vjp(fn, *[x[...] for x in r], *[x[...] for x in p])
        cts = vjp(tuple([x[...] for x in g] + [x[...] for x in ga]))
        for ref, val in zip(dr, cts[:n_diff]):
            ref[...] = val
        if npar:
            @pl.when(pl.program_id(0) == 0)
            def _():
                for ref in dp:
                    ref[...] = jnp.zeros_like(ref)
            for ref, val in zip(dp, cts[nr:]):
                ref[...] += val

    in_specs = [pl.BlockSpec((t, x.shape[1]), lambda i: (i, 0)) for x in rows]
    in_specs += [pl.BlockSpec(x.shape, lambda i: (0, 0)) for x in params]
    in_specs += [pl.BlockSpec((t, x.shape[1]), lambda i: (i, 0)) for x in row_cts]
    in_specs += [pl.BlockSpec(x.shape, lambda i: (0, 0)) for x in acc_cts]
    out_specs = [pl.BlockSpec((t, x.shape[1]), lambda i: (i, 0)) for x in rows[:n_diff]]
    out_specs += [pl.BlockSpec(x.shape, lambda i: (0, 0)) for x in params]
    out_shape = [jax.ShapeDtypeStruct(x.shape, F32) for x in rows[:n_diff]]
    out_shape += [jax.ShapeDtypeStruct(x.shape, F32) for x in params]
    res = pl.pallas_call(body, name=name, grid=(n,), in_specs=in_specs, out_specs=out_specs, out_shape=out_shape,
                         compiler_params=_cparams(("arbitrary",)))(*rows, *params, *row_cts, *acc_cts)
    return tuple(res[:n_diff]), tuple(res[n_diff:])


def make_rowwise(fn, name, out_dims, acc_dims=(), n_nondiff=0, tile=256):
    out_dtypes = (F32,) * len(out_dims)

    @jax.custom_vjp
    def op(rows, params):
        return _rw_fwd(fn, name, rows, params, out_dims, out_dtypes, acc_dims, tile)

    def fwd(rows, params):
        return op(rows, params), (rows, params)

    def bwd(res, cts):
        rows, params = res
        n_diff = len(rows) - n_nondiff
        d_rows, d_params = _rw_bwd(fn, name + "_bwd", rows, params, cts[:len(out_dims)], cts[len(out_dims):],
                                   n_diff, tile)
        d_rows = d_rows + tuple(jnp.zeros_like(x) for x in rows[n_diff:])
        return d_rows, d_params

    op.defvjp(fwd, bwd)
    return op


def _mm(a, b, mode, name):
    if mode == "nn":
        (m, k), (_, n) = a.shape, b.shape
    elif mode == "nt":
        (m, k), (n, _) = a.shape, b.shape
    else:
        (k, m), (_, n) = a.shape, b.shape
    tm = _pick(m, 512)
    tn = _pick(n, 1152)
    tk = _pick(k, 1152) if mode != "tn" else _pick(k, 512)
    nk = k // tk

    def body(a_ref, b_ref, o_ref, acc_ref):
        kk = pl.program_id(2)

        @pl.when(kk == 0)
        def _():
            acc_ref[...] = jnp.zeros_like(acc_ref)

        acc_ref[...] += _dot_raw(a_ref[...], b_ref[...], mode, False)

        @pl.when(kk == nk - 1)
        def _():
            o_ref[...] = acc_ref[...]

    if mode == "nn":
        a_spec = pl.BlockSpec((tm, tk), lambda i, j, kk: (i, kk))
        b_spec = pl.BlockSpec((tk, tn), lambda i, j, kk: (kk, j))
    elif mode == "nt":
        a_spec = pl.BlockSpec((tm, tk), lambda i, j, kk: (i, kk))
        b_spec = pl.BlockSpec((tn, tk), lambda i, j, kk: (j, kk))
    else:
        a_spec = pl.BlockSpec((tk, tm), lambda i, j, kk: (kk, i))
        b_spec = pl.BlockSpec((tk, tn), lambda i, j, kk: (kk, j))
    return pl.pallas_call(
        body, name=name, grid=(m // tm, n // tn, nk), in_specs=[a_spec, b_spec],
        out_specs=pl.BlockSpec((tm, tn), lambda i, j, kk: (i, j)),
        out_shape=jax.ShapeDtypeStruct((m, n), F32), scratch_shapes=[pltpu.VMEM((tm, tn), F32)],
        compiler_params=_cparams(("parallel", "parallel", "arbitrary")))(a, b)


def make_linear(name):
    @jax.custom_vjp
    def op(a, w):
        return _mm(a, w.astype(BF16), "nn", name)

    def fwd(a, w):
        w16 = w.astype(BF16)
        return _mm(a, w16, "nn", name), (a, w16)

    def bwd(res, g):
        a, w16 = res
        return _mm(g, w16, "nt", name + "_dx"), _mm(a, g, "tn", name + "_dw")

    op.defvjp(fwd, bwd)
    return op


def _fan_mm(xs, ws, transposed, name, tile=256):
    n_in, n_out = len(ws), len(ws[0])
    s = xs[0].shape[0]
    t = min(tile, s)
    out_dims = [ws[i][0].shape[0] for i in range(n_in)] if transposed else [ws[0][j].shape[1] for j in range(n_out)]
    flat_ws = [w for row in ws for w in row]

    def body(*refs):
        x_refs, w_refs, o_refs = refs[:len(xs)], refs[len(xs):len(xs) + len(flat_ws)], refs[len(xs) + len(flat_ws):]
        xv = [r[...].astype(BF16) for r in x_refs]
        for o, o_ref in enumerate(o_refs):
            if transposed:
                terms = [_dot_raw(xv[j], w_refs[o * n_out + j][...], "nt", False) for j in range(n_out)]
            else:
                terms = [_dot_raw(xv[i], w_refs[i * n_out + o][...], "nn", False) for i in range(n_in)]
            o_ref[...] = functools.reduce(lambda a, b: a + b, terms)

    in_specs = [pl.BlockSpec((t, x.shape[1]), lambda i: (i, 0)) for x in xs]
    in_specs += [pl.BlockSpec(w.shape, lambda i: (0, 0)) for w in flat_ws]
    res = pl.pallas_call(
        body, name=name, grid=(s // t,), in_specs=in_specs,
        out_specs=[pl.BlockSpec((t, d), lambda i: (i, 0)) for d in out_dims],
        out_shape=[jax.ShapeDtypeStruct((s, d), F32) for d in out_dims],
        compiler_params=_cparams(("parallel",)))(*xs, *flat_ws)
    return tuple(res)


def make_fan_linear(name):
    @jax.custom_vjp
    def op(xs, ws):
        return _fan_mm(xs, tuple(tuple(w.astype(BF16) for w in row) for row in ws), False, name)

    def fwd(xs, ws):
        ws16 = tuple(tuple(w.astype(BF16) for w in row) for row in ws)
        return _fan_mm(xs, ws16, False, name), (xs, ws16)

    def bwd(res, dys):
        xs, ws16 = res
        dxs = _fan_mm(dys, ws16, True, name + "_dx")
        dws = tuple(tuple(_mm(x, dy, "tn", name + "_dw") for dy in dys) for x in xs)
        return dxs, dws

    op.defvjp(fwd, bwd)
    return op


def _mla_prep_fn(qraw, kvraw, kr, t1, t2):
    kr_rot = kr * t1 + roll_half(kr) * t2
    qs, ks = [], []
    for h in range(MLA_HEADS):
        q_r = qraw[:, h * 256 + 128:(h + 1) * 256]
        qs += [qraw[:, h * 256:h * 256 + 128], q_r * t1 + roll_half(q_r) * t2]
        ks += [kvraw[:, h * 128:(h + 1) * 128], kr_rot]
    return jnp.concatenate(qs, axis=1), jnp.concatenate(ks, axis=1), kvraw[:, 512:]


def _flash_tile(s):
    return 512 if s >= 2048 else 128


FLASH_SCALE = (MLA_NOPE + MLA_ROPE) ** -0.5
STAT_ROWS = 8


def _as_rows(col_b):
    ones = jnp.full((STAT_ROWS, LANES), 1.0 / LANES, F32)
    return _dot_raw(ones, col_b, "nt", True)


def _flash_fwd(qf, kf, vf):
    s = qf.shape[0]
    t = _flash_tile(s)
    nb = s // t

    def body(q_ref, k_ref, v_ref, o_ref, lse_ref, m_sc, l_sc, acc_sc):
        i = pl.program_id(1)
        m_sc[...] = jnp.full_like(m_sc, -1e30)
        l_sc[...] = jnp.zeros_like(l_sc)
        acc_sc[...] = jnp.zeros_like(acc_sc)
        q = q_ref[...]

        def step(j, on_diagonal):
            rows = pl.ds(pl.multiple_of(j * t, t), t)
            st = _dot_raw(k_ref[rows, :], q, "nt", False) * FLASH_SCALE
            if on_diagonal:
                keep = lax.broadcasted_iota(jnp.int32, (t, t), 0) <= lax.broadcasted_iota(jnp.int32, (t, t), 1)
                st = jnp.where(keep, st, -1e30)
            m_old = m_sc[...]
            m_new = jnp.maximum(m_old, jnp.max(st, axis=0, keepdims=True))
            alpha = jnp.exp(m_old - m_new)
            pt = jnp.exp(st - m_new)
            l_sc[...] = alpha * l_sc[...] + jnp.sum(pt, axis=0, keepdims=True)
            acc_sc[...] = alpha * acc_sc[...] + _dot_raw(v_ref[rows, :], pt, "tn", False)
            m_sc[...] = m_new

        def two_steps(p, carry):
            step(2 * p, False)
            step(2 * p + 1, False)
            return carry

        lax.fori_loop(0, i // 2, two_steps, 0)

        @pl.when(i % 2 == 1)
        def _():
            step(i - 1, False)

        step(i, True)
        o_ref[...] = (acc_sc[...] / l_sc[...]).T
        lse_ref[0] = jnp.broadcast_to(m_sc[...] + jnp.log(l_sc[...]), (STAT_ROWS, t))

    return pl.pallas_call(
        body, name="flash_fwd", grid=(MLA_HEADS, nb),
        in_specs=[pl.BlockSpec((t, 256), lambda h, i: (i, h)), pl.BlockSpec((s, 256), lambda h, i: (0, h)),
                  pl.BlockSpec((s, 128), lambda h, i: (0, h))],
        out_specs=[pl.BlockSpec((t, 128), lambda h, i: (i, h)),
                   pl.BlockSpec((1, STAT_ROWS, t), lambda h, i: (h, 0, i))],
        out_shape=[jax.ShapeDtypeStruct((s, MLA_WIDTH), F32),
                   jax.ShapeDtypeStruct((MLA_HEADS, STAT_ROWS, s), F32)],
        scratch_shapes=[pltpu.VMEM((1, t), F32), pltpu.VMEM((1, t), F32), pltpu.VMEM((128, t), F32)],
        compiler_params=_cparams(("parallel", "arbitrary")))(qf, kf, vf)


def _flash_bwd_prep(o, do):
    s = o.shape[0]
    t = _flash_tile(s)

    def body(o_ref, do_ref, dl_ref, do16_ref):
        do_t = do_ref[...]
        delta = jnp.sum(do_t * o_ref[...], axis=1, keepdims=True)
        dl_ref[0] = _as_rows(jnp.broadcast_to(delta, (t, LANES)))
        do16_ref[...] = do_t.astype(BF16)

    blk = pl.BlockSpec((t, 128), lambda h, i: (i, h))
    return pl.pallas_call(
        body, name="flash_bwd_prep", grid=(MLA_HEADS, s // t), in_specs=[blk, blk],
        out_specs=[pl.BlockSpec((1, STAT_ROWS, t), lambda h, i: (h, 0, i)), blk],
        out_shape=[jax.ShapeDtypeStruct((MLA_HEADS, STAT_ROWS, s), F32), jax.ShapeDtypeStruct((s, MLA_WIDTH), BF16)],
        compiler_params=_cparams(("parallel", "parallel")))(o, do)


def _flash_bwd(qf, kf, vf, lse, delta, do16):
    s = qf.shape[0]
    t = _flash_tile(s)
    nb = s // t

    def body(q_ref, k_ref, v_ref, lse_ref, dl_ref, do_ref, dq_ref, dk_ref, dv_ref):
        j = pl.program_id(1)

        @pl.when(j == 0)
        def _():
            dq_ref[...] = jnp.zeros_like(dq_ref)

        dk_ref[...] = jnp.zeros_like(dk_ref)
        dv_ref[...] = jnp.zeros_like(dv_ref)
        k = k_ref[...]
        v = v_ref[...]

        def step(i, on_diagonal):
            rows = pl.ds(pl.multiple_of(i * t, t), t)
            q = q_ref[rows, :]
            do_t = do_ref[rows, :]
            st = _dot_raw(k, q, "nt", False) * FLASH_SCALE - lse_ref[0, 0:1, rows]
            if on_diagonal:
                keep = lax.broadcasted_iota(jnp.int32, (t, t), 0) <= lax.broadcasted_iota(jnp.int32, (t, t), 1)
                st = jnp.where(keep, st, -1e30)
            pt = jnp.exp(st)
            dst = pt * (_dot_raw(v, do_t, "nt", False) - dl_ref[0, 0:1, rows])
            dv_ref[...] += _dot_raw(pt, do_t, "nn", False)
            dk_ref[...] += _dot_raw(dst, q, "nn", False) * FLASH_SCALE
            dq_ref[rows, :] += _dot_raw(dst, k, "tn", False) * FLASH_SCALE

        def two_steps(p, carry):
            step(j + 1 + 2 * p, False)
            step(j + 2 + 2 * p, False)
            return carry

        step(j, True)
        below = nb - 1 - j
        lax.fori_loop(0, below // 2, two_steps, 0)

        @pl.when(below % 2 == 1)
        def _():
            step(nb - 1, False)

    stat = pl.BlockSpec((1, STAT_ROWS, s), lambda h, j: (h, 0, 0))
    return pl.pallas_call(
        body, name="flash_bwd", grid=(MLA_HEADS, nb),
        in_specs=[pl.BlockSpec((s, 256), lambda h, j: (0, h)), pl.BlockSpec((t, 256), lambda h, j: (j, h)),
                  pl.BlockSpec((t, 128), lambda h, j: (j, h)), stat, stat, pl.BlockSpec((s, 128), lambda h, j: (0, h))],
        out_specs=[pl.BlockSpec((s, 256), lambda h, j: (0, h)), pl.BlockSpec((t, 256), lambda h, j: (j, h)),
                   pl.BlockSpec((t, 128), lambda h, j: (j, h))],
        out_shape=[jax.ShapeDtypeStruct((s, 1024), F32), jax.ShapeDtypeStruct((s, 1024), F32),
                   jax.ShapeDtypeStruct((s, MLA_WIDTH), F32)],
        compiler_params=_cparams(("parallel", "arbitrary")))(qf, kf, vf, lse, delta, do16)


def _mla_prep(qraw, kvraw, kr, t1, t2):
    return _rw_fwd(_mla_prep_fn, "mla_prep", (qraw, kvraw, kr, t1, t2), (), (1024, 1024, 512), (BF16, BF16, BF16), (),
                   256)


@jax.custom_vjp
def mla_attention(qraw, kvraw, kr, t1, t2):
    return _flash_fwd(*_mla_prep(qraw, kvraw, kr, t1, t2))[0]


def _mla_attention_fwd(qraw, kvraw, kr, t1, t2):
    qf, kf, vf = _mla_prep(qraw, kvraw, kr, t1, t2)
    o, lse = _flash_fwd(qf, kf, vf)
    return o, (qraw, kvraw, kr, t1, t2, qf, kf, vf, o, lse)


def _mla_attention_bwd(res, do):
    qraw, kvraw, kr, t1, t2, qf, kf, vf, o, lse = res
    delta, do16 = _flash_bwd_prep(o, do)
    dq, dk, dv = _flash_bwd(qf, kf, vf, lse, delta, do16)
    (dqraw, dkvraw, dkr), _ = _rw_bwd(_mla_prep_fn, "mla_prep_bwd", (qraw, kvraw, kr, t1, t2), (), (dq, dk, dv), (),
                                      3, 256)
    return dqraw, dkvraw, dkr, jnp.zeros_like(t1), jnp.zeros_like(t2)


mla_attention.defvjp(_mla_attention_fwd, _mla_attention_bwd)


def _lane_pick(x, lane):
    ids = lax.broadcasted_iota(jnp.int32, x.shape, 1)
    col = jnp.sum(jnp.where(ids == lane, x, 0.0), axis=1, keepdims=True)
    return jnp.broadcast_to(col, x.shape)


GDN_HALO = 8


@functools.partial(jax.custom_vjp, nondiff_argnums=(1,))
def _roll_rows(x, d):
    return pltpu.roll(x, d, 0)


_roll_rows.defvjp(lambda x, d: (pltpu.roll(x, d, 0), None), lambda d, _, g: (pltpu.roll(g, g.shape[0] - d, 0),))


def _gdn_prep_fn(prev, cur, ab, w0, w1, w2, w3, a_log, dt_bias):
    xcat = jnp.concatenate([prev, cur], axis=0)
    x0, x1, x2 = [_roll_rows(xcat, GDN_CONV - 1 - j)[GDN_HALO:] for j in range(GDN_CONV - 1)]
    qkv = _silu(x0 * w0 + x1 * w1 + x2 * w2 + cur * w3)
    g_all = -jnp.exp(a_log) * _softplus(ab + dt_bias)
    beta_all = _sigmoid(ab)
    qs, ks, gs, bs = [], [], [], []
    for h in range(GDN_HEADS):
        q = qkv[:, h * 128:(h + 1) * 128]
        k = qkv[:, 512 + h * 128:512 + (h + 1) * 128]
        qs.append(q * lax.rsqrt(jnp.sum(q * q, axis=-1, keepdims=True) + NORM_EPS) * (GDN_DK ** -0.5))
        ks.append(k * lax.rsqrt(jnp.sum(k * k, axis=-1, keepdims=True) + NORM_EPS))
        gs.append(_lane_pick(g_all, h))
        bs.append(_lane_pick(beta_all, GDN_HEADS + h))
    cat = lambda xs: jnp.concatenate(xs, axis=1)
    return cat(qs), cat(ks), qkv[:, 1024:], cat(gs), cat(bs)


GDN_PREP_TILE = 256


def _gdn_prep_specs(s, params):
    t = min(GDN_PREP_TILE, s)
    prev = pl.BlockSpec((GDN_HALO, GDN_QKV), lambda i: (jnp.maximum(i * (t // GDN_HALO) - 1, 0), 0))
    rows = lambda d: pl.BlockSpec((t, d), lambda i: (i, 0))
    return t, [prev, rows(GDN_QKV), rows(LANES)] + [pl.BlockSpec(p.shape, lambda i: (0, 0)) for p in params]


def _gdn_prep_masked(first):
    return lambda prev, *rest: _gdn_prep_fn(prev * first, *rest)


def _gdn_prep_fwd(qkv, ab, params):
    s = qkv.shape[0]
    t, in_specs = _gdn_prep_specs(s, params)

    def body(prev_ref, cur_ref, ab_ref, *refs):
        p_refs, o_refs = refs[:len(params)], refs[len(params):]
        first = (pl.program_id(0) > 0).astype(F32)
        outs = _gdn_prep_masked(first)(prev_ref[...], cur_ref[...], ab_ref[...], *[p[...] for p in p_refs])
        for ref, val in zip(o_refs, outs):
            ref[...] = val

    return pl.pallas_call(
        body, name="gdn_prep", grid=(s // t,), in_specs=in_specs,
        out_specs=[pl.BlockSpec((t, GDN_WIDTH), lambda i: (i, 0))] * 5,
        out_shape=[jax.ShapeDtypeStruct((s, GDN_WIDTH), F32)] * 5,
        compiler_params=_cparams(("parallel",)))(qkv, qkv, ab, *params)


def _gdn_prep_bwd(qkv, ab, params, cts):
    s = qkv.shape[0]
    t, in_specs = _gdn_prep_specs(s, params)
    n = s // t
    npar = len(params)

    def body(prev_ref, cur_ref, ab_ref, *refs):
        p_refs, g_refs = refs[:npar], refs[npar:npar + 5]
        dcur_ref, dhalo_ref, dab_ref = refs[npar + 5:npar + 8]
        dp_refs = refs[npar + 8:]
        first = (pl.program_id(0) > 0).astype(F32)
        _, vjp = jax.vjp(_gdn_prep_masked(first), prev_ref[...], cur_ref[...], ab_ref[...], *[p[...] for p in p_refs])
        d_prev, d_cur, d_ab, *d_params = vjp(tuple(g[...] for g in g_refs))
        dcur_ref[...] = d_cur
        dhalo_ref[0] = d_prev
        dab_ref[...] = d_ab

        @pl.when(pl.program_id(0) == 0)
        def _():
            for ref in dp_refs:
                ref[...] = jnp.zeros_like(ref)

        for ref, val in zip(dp_refs, d_params):
            ref[...] += val

    rows = lambda d: pl.BlockSpec((t, d), lambda i: (i, 0))
    res = pl.pallas_call(
        body, name="gdn_prep_bwd", grid=(n,), in_specs=in_specs + [rows(GDN_WIDTH)] * 5,
        out_specs=[rows(GDN_QKV), pl.BlockSpec((1, GDN_HALO, GDN_QKV), lambda i: (i, 0, 0)), rows(LANES)]
        + [pl.BlockSpec(p.shape, lambda i: (0, 0)) for p in params],
        out_shape=[jax.ShapeDtypeStruct((s, GDN_QKV), F32), jax.ShapeDtypeStruct((n, GDN_HALO, GDN_QKV), F32),
                   jax.ShapeDtypeStruct((s, LANES), F32)] + [jax.ShapeDtypeStruct(p.shape, F32) for p in params],
        compiler_params=_cparams(("arbitrary",)))(qkv, qkv, ab, *params, *cts)
    return res[0], res[1], res[2], tuple(res[3:])


@jax.custom_vjp
def gdn_prep(qkv, ab, params):
    return tuple(_gdn_prep_fwd(qkv, ab, params))


def _gdn_prep_vjp_fwd(qkv, ab, params):
    return tuple(_gdn_prep_fwd(qkv, ab, params)), (qkv, ab, params)


def _gdn_prep_vjp_bwd(res, cts):
    qkv, ab, params = res
    d_cur, d_halo, d_ab, d_params = _gdn_prep_bwd(qkv, ab, params, cts)
    s, n = qkv.shape[0], d_halo.shape[0]
    t = s // n
    shifted = jnp.concatenate([d_halo[1:], jnp.zeros_like(d_halo[:1])], axis=0)
    halo_rows = jnp.concatenate([jnp.zeros((n, t - GDN_HALO, GDN_QKV), F32), shifted], axis=1)
    return d_cur + halo_rows.reshape(s, GDN_QKV), d_ab, d_params


gdn_prep.defvjp(_gdn_prep_vjp_fwd, _gdn_prep_vjp_bwd)


@jax.custom_vjp
def _unit_lower_inverse(lms):
    c = lms[0].shape[0]
    row = lax.broadcasted_iota(jnp.int32, (c, c), 0)
    col = lax.broadcasted_iota(jnp.int32, (c, c), 1)
    ts = [(row == col).astype(F32) - jnp.where((row >> 1) == (col >> 1), lm, 0.0) for lm in lms]
    for level in range(1, int(math.log2(c))):
        below = ((row >> (level + 1)) == (col >> (level + 1))) & ((row >> level) != (col >> level))
        mids = [_dot_split(t, jnp.where(below, lm, 0.0)) for t, lm in zip(ts, lms)]
        ts = [t - _dot_split(mid, t) for t, mid in zip(ts, mids)]
    return tuple(ts)


def _uli_fwd(lms):
    ts = _unit_lower_inverse(lms)
    return ts, ts


def _uli_bwd(ts, gs):
    mids = [bdot(t, g, "tn") for t, g in zip(ts, gs)]
    return (tuple(-bdot(mid, t, "nt") for t, mid in zip(ts, mids)),)


_unit_lower_inverse.defvjp(_uli_fwd, _uli_bwd)


def _gdn_chunk_fn(qs, ks, vs, gbs, bbs, s0s):
    heads = range(len(qs))
    c = qs[0].shape[0]
    row = lax.broadcasted_iota(jnp.int32, (c, c), 0)
    col = lax.broadcasted_iota(jnp.int32, (c, c), 1)
    incl, strict = row >= col, row > col
    tri = incl.astype(F32)
    gc = [bdot(tri, gbs[h], "nn", True) for h in heads]
    decay = [jnp.exp(jnp.where(incl, gc[h] - gc[h].T, -1e30)) for h in heads]
    g_last = [jnp.sum(gbs[h], axis=0, keepdims=True) for h in heads]
    eg = [jnp.exp(gc[h]) for h in heads]
    kb = [ks[h] * bbs[h] for h in heads]
    ts = _unit_lower_inverse(tuple(jnp.where(strict, bdot(kb[h], ks[h], "nt") * decay[h], 0.0) for h in heads))
    u = [bdot(ts[h], vs[h] * bbs[h]) for h in heads]
    w = [bdot(ts[h], kb[h] * eg[h]) for h in heads]
    qk = [bdot(qs[h], ks[h], "nt") * decay[h] for h in heads]
    v_new = [u[h] - bdot(w[h], s0s[h]) for h in heads]
    o = [bdot(qs[h] * eg[h], s0s[h]) + bdot(qk[h], v_new[h]) for h in heads]
    s1 = [s0s[h] * jnp.exp(g_last[h]) + bdot(ks[h] * jnp.exp(g_last[h] - gc[h]), v_new[h], "tn") for h in heads]
    return tuple(o), tuple(s1)


def _head_tiles(ref):
    return tuple(ref[:, h * 128:(h + 1) * 128] for h in range(GDN_HEADS))


def _gdn_fwd(q, k, v, gb, bb):
    s = q.shape[0]
    c = min(GDN_CHUNK, s)
    nc = s // c

    def body(q_ref, k_ref, v_ref, g_ref, b_ref, o_ref, st_ref, s_sc):
        @pl.when(pl.program_id(0) == 0)
        def _():
            s_sc[...] = jnp.zeros_like(s_sc)

        s0s = tuple(s_sc[h] for h in range(GDN_HEADS))
        for h in range(GDN_HEADS):
            st_ref[h, 0] = s0s[h]
        os, s1s = _gdn_chunk_fn(*[_head_tiles(ref) for ref in (q_ref, k_ref, v_ref, g_ref, b_ref)], s0s)
        for h in range(GDN_HEADS):
            o_ref[:, h * 128:(h + 1) * 128] = os[h]
            s_sc[h] = s1s[h]

    blk = pl.BlockSpec((c, GDN_WIDTH), lambda n: (n, 0))
    return pl.pallas_call(
        body, name="gdn_fwd", grid=(nc,), in_specs=[blk] * 5,
        out_specs=[blk, pl.BlockSpec((GDN_HEADS, 1, 128, 128), lambda n: (0, n, 0, 0))],
        out_shape=[jax.ShapeDtypeStruct((s, GDN_WIDTH), F32), jax.ShapeDtypeStruct((GDN_HEADS, nc, 128, 128), F32)],
        scratch_shapes=[pltpu.VMEM((GDN_HEADS, 128, 128), F32)],
        compiler_params=_cparams(("arbitrary",)))(q, k, v, gb, bb)


def _gdn_bwd(q, k, v, gb, bb, states, do):
    s = q.shape[0]
    c = min(GDN_CHUNK, s)
    nc = s // c

    def body(q_ref, k_ref, v_ref, g_ref, b_ref, st_ref, do_ref, dq_ref, dk_ref, dv_ref, dg_ref, db_ref, ds_sc):
        @pl.when(pl.program_id(0) == 0)
        def _():
            ds_sc[...] = jnp.zeros_like(ds_sc)

        s0s = tuple(st_ref[h, 0] for h in range(GDN_HEADS))
        _, vjp = jax.vjp(_gdn_chunk_fn, *[_head_tiles(ref) for ref in (q_ref, k_ref, v_ref, g_ref, b_ref)], s0s)
        *d_tiles, ds0s = vjp((_head_tiles(do_ref), tuple(ds_sc[h] for h in range(GDN_HEADS))))
        for h in range(GDN_HEADS):
            for ref, d in zip((dq_ref, dk_ref, dv_ref, dg_ref, db_ref), d_tiles):
                ref[:, h * 128:(h + 1) * 128] = d[h]
            ds_sc[h] = ds0s[h]

    blk = pl.BlockSpec((c, GDN_WIDTH), lambda n: (nc - 1 - n, 0))
    return pl.pallas_call(
        body, name="gdn_bwd", grid=(nc,),
        in_specs=[blk] * 5 + [pl.BlockSpec((GDN_HEADS, 1, 128, 128), lambda n: (0, nc - 1 - n, 0, 0)), blk],
        out_specs=[blk] * 5, out_shape=[jax.ShapeDtypeStruct((s, GDN_WIDTH), F32)] * 5,
        scratch_shapes=[pltpu.VMEM((GDN_HEADS, 128, 128), F32)],
        compiler_params=_cparams(("arbitrary",)))(q, k, v, gb, bb, states, do)


@jax.custom_vjp
def gdn_core(q, k, v, gb, bb):
    return _gdn_fwd(q, k, v, gb, bb)[0]


def _gdn_core_fwd(q, k, v, gb, bb):
    o, states = _gdn_fwd(q, k, v, gb, bb)
    return o, (q, k, v, gb, bb, states)


def _gdn_core_bwd(res, do):
    return tuple(_gdn_bwd(*res, do))


gdn_core.defvjp(_gdn_core_fwd, _gdn_core_bwd)


def _cact_fn(c):
    return (_silu(c),)


def _prenorm_fn(x, w, scale_raw, scale_b, shift_raw, shift_b):
    return (_rms(x, w) * (1.0 + scale_raw + scale_b) + shift_raw + shift_b,)


def _rmsnorm_fn(x, w):
    return (_rms(x, w),)


def _gate_fn(o, z):
    return (o * _silu(z),)


def _gdn_out_fn(o, z, w):
    parts = [_rms(o[:, h * 128:(h + 1) * 128], w) for h in range(GDN_HEADS)]
    return (jnp.concatenate(parts, axis=1) * _silu(z),)


def _post_fn(x, y, w, gate_raw, gate_b):
    return (x + (gate_raw + gate_b) * _rms(y, w),)


def _loss_fn(y, tgt):
    err = y - tgt
    part = jnp.sum(0.5 * jnp.mean(err * err, axis=-1, keepdims=True), axis=0, keepdims=True)
    return (jnp.broadcast_to(part, (1, LANES)),)


_OPS = dict(
    cact=make_rowwise(_cact_fn, "c_act", (D_MODEL,), tile=16),
    prenorm=make_rowwise(_prenorm_fn, "prenorm", (D_MODEL,)),
    qnorm=make_rowwise(_rmsnorm_fn, "q_norm", (MLA_Q_RANK,)),
    kvnorm=make_rowwise(_rmsnorm_fn, "kv_norm", (MLA_KV_RANK,)),
    gate=make_rowwise(_gate_fn, "mla_gate", (MLA_WIDTH,)),
    gdn_out=make_rowwise(_gdn_out_fn, "gdn_out", (GDN_WIDTH,)),
    post=make_rowwise(_post_fn, "postnorm", (D_MODEL,)),
    loss=make_rowwise(_loss_fn, "loss", (), acc_dims=(LANES,), n_nondiff=1),
    lin_mod=make_linear("lin_mod"), lin_in=make_fan_linear("lin_in"), lin_q=make_linear("lin_q"),
    lin_kv=make_linear("lin_kv"), lin_out=make_fan_linear("lin_out"),
)


def _swap_halves(w):
    half = w.shape[-1] // 2
    return jnp.concatenate([w[..., half:], w[..., :half]], axis=-1)


def _w_in_groups(w):
    k_pe = w[:, 640:704]
    ab = jnp.concatenate([w[:, 2752:2760], jnp.zeros((w.shape[0], LANES - 8), w.dtype)], axis=1)
    return (w[:, :384], w[:, 384:640], jnp.concatenate([k_pe, _swap_halves(k_pe)], axis=1), w[:, 704:1216],
            w[:, 1216:2752], ab, w[:, 2760:])


def _q_up_ext(w):
    parts = []
    for h in range(MLA_HEADS):
        rope = w[:, h * 192 + 128:(h + 1) * 192]
        parts += [w[:, h * 192:h * 192 + 128], rope, _swap_halves(rope)]
    return jnp.concatenate(parts, axis=1)


def _kv_up_perm(w):
    ks = [w[:, h * 256:h * 256 + 128] for h in range(MLA_HEADS)]
    vs = [w[:, h * 256 + 128:(h + 1) * 256] for h in range(MLA_HEADS)]
    return jnp.concatenate(ks + vs, axis=1)


def _pad_lanes(v):
    return jnp.pad(v, (0, LANES - v.shape[0]))[None, :]


def _local_loss(weights, x, c, positions, target):
    s = x.shape[0]
    half = MLA_ROPE // 2
    inv_freq = jnp.power(ROPE_THETA, -jnp.arange(half, dtype=F32) * 2.0 / MLA_ROPE)
    ang = positions.astype(F32)[:, None] * inv_freq
    cos, sin, zero = jnp.cos(ang), jnp.sin(ang), jnp.zeros((s, 2 * half), F32)
    t1 = jnp.concatenate([cos, cos, zero], axis=1)
    t2 = jnp.concatenate([-sin, sin, zero], axis=1)

    (c_act,) = _OPS["cact"]((jnp.pad(c, ((0, 15), (0, 0))),), ())
    for l in range(DEPTH):
        mod = _OPS["lin_mod"](c_act, weights["w_mod"][l])[0:1]
        b = weights["b_mod"][l][None, :]
        shift_raw, scale_raw, gate_raw = mod[:, :1024], mod[:, 1024:2048], mod[:, 2048:]
        shift_b, scale_b, gate_b = b[:, :1024], b[:, 1024:2048], b[:, 2048:]
        (h,) = _OPS["prenorm"]((x,), (weights["pre_norm_w"][l][None], scale_raw, scale_b, shift_raw, shift_b))
        q_lat, kv_lat, kr, z_mla, qkv, ab, z_gdn = _OPS["lin_in"]((h,), (_w_in_groups(weights["w_in"][l]),))
        (qn,) = _OPS["qnorm"]((q_lat,), (weights["mla_q_norm_w"][l][None],))
        qraw = _OPS["lin_q"](qn, _q_up_ext(weights["mla_q_up"][l]))
        (kvn,) = _OPS["kvnorm"]((kv_lat,), (weights["mla_kv_norm_w"][l][None],))
        kvraw = _OPS["lin_kv"](kvn, _kv_up_perm(weights["mla_kv_up"][l]))
        o_mla = mla_attention(qraw, kvraw, kr, t1, t2)
        (y_mla,) = _OPS["gate"]((o_mla, z_mla), ())
        cw = weights["gdn_conv_w"][l]
        params = tuple(cw[j][None] for j in range(GDN_CONV))
        params += (_pad_lanes(weights["gdn_a_log"][l]), _pad_lanes(weights["gdn_dt_bias"][l]))
        qg, kg, vg, gb, bb = gdn_prep(qkv, ab, params)
        o_gdn = gdn_core(qg, kg, vg, gb, bb)
        (y_gdn,) = _OPS["gdn_out"]((o_gdn, z_gdn), (weights["gdn_o_norm_w"][l][None],))
        w_o = weights["w_out"][l]
        (y,) = _OPS["lin_out"]((y_mla, y_gdn), ((w_o[:MLA_WIDTH],), (w_o[MLA_WIDTH:],)))
        (x,) = _OPS["post"]((x, y), (weights["post_norm_w"][l][None], gate_raw, gate_b))
    (acc,) = _OPS["loss"]((x, target), ())
    return acc[0, 0]


def _chip_index():
    return 2 * lax.axis_index("x") + lax.axis_index("y")


def _other_chips(x, y):
    return [(1 - x, y), (x, 1 - y), (1 - x, 1 - y)]


def _any_spec():
    return pl.BlockSpec(memory_space=pl.ANY)


def _half(ref, hc):
    n = ref.shape[0] // 2
    return ref.at[pl.ds(hc * n, n)]


def ag_weights(shards):
    n = len(shards)

    def body(*refs):
        ins, outs = refs[:n], refs[n:2 * n]
        send_sems, recv_sems, local_sems = refs[2 * n:]
        x, y, c = lax.axis_index("x"), lax.axis_index("y"), lax.axis_index("c")
        sibling = (x, y, 1 - c)
        chips = _other_chips(x, y)

        def copy(t, k, src, chip_xy, hc, to):
            return pltpu.make_async_remote_copy(
                src_ref=src, dst_ref=_half(outs[t].at[2 * chip_xy[0] + chip_xy[1]], hc),
                send_sem=send_sems.at[6 * t + k], recv_sem=recv_sems.at[6 * t + k], device_id=to, device_id_type=MESH)

        mine = [pltpu.make_async_copy(ins[t], outs[t].at[2 * x + y], local_sems.at[t]) for t in range(n)]
        first = [copy(t, k, _half(ins[t], c), (x, y), c, (*chip, c)) for k, chip in enumerate(chips) for t in range(n)]
        for cp in mine + first:
            cp.start()
        passed = []
        for k, chip in enumerate(chips):
            for t in range(n):
                landed = _half(outs[t].at[2 * chip[0] + chip[1]], c)
                copy(t, k, landed, chip, c, (x, y, c)).wait_recv()
                passed.append(copy(t, 3 + k, landed, chip, c, sibling))
                passed[-1].start()
        for k, chip in enumerate(chips):
            for t in range(n):
                copy(t, 3 + k, _half(ins[t], c), chip, 1 - c, (x, y, c)).wait_recv()
        for cp in first + passed:
            cp.wait_send()
        for cp in mine:
            cp.wait()

    return pl.pallas_call(
        body, name="ag_weights", in_specs=[_any_spec()] * n, out_specs=[_any_spec()] * n,
        out_shape=[jax.ShapeDtypeStruct((N_CHIPS,) + a.shape, a.dtype) for a in shards],
        scratch_shapes=[pltpu.SemaphoreType.DMA((6 * n,)), pltpu.SemaphoreType.DMA((6 * n,)),
                        pltpu.SemaphoreType.DMA((n,))],
        compiler_params=pltpu.CompilerParams(has_side_effects=True))(*shards)


def rs_pair(gs):
    n = len(gs)

    def body(*refs):
        ins, outs = refs[:n], refs[n:2 * n]
        send_sems, recv_sems = refs[2 * n:]
        x, y, c = lax.axis_index("x"), lax.axis_index("y"), lax.axis_index("c")
        lh = [g.shape[1] // 2 for g in gs]
        copies = [pltpu.make_async_remote_copy(
            src_ref=ins[t].at[:, pl.ds((1 - c) * lh[t], lh[t])], dst_ref=outs[t], send_sem=send_sems.at[t],
            recv_sem=recv_sems.at[t], device_id=(x, y, 1 - c), device_id_type=MESH) for t in range(n)]
        for cp in copies:
            cp.start()
        for cp in copies:
            cp.wait()

    return pl.pallas_call(
        body, name="rs_pair", in_specs=[_any_spec()] * n, out_specs=[_any_spec()] * n,
        out_shape=[jax.ShapeDtypeStruct((N_CHIPS, g.shape[1] // 2) + g.shape[2:], F32) for g in gs],
        scratch_shapes=[pltpu.SemaphoreType.DMA((n,)), pltpu.SemaphoreType.DMA((n,))],
        compiler_params=pltpu.CompilerParams(has_side_effects=True))(*gs)


def rs_cross(pairs):
    n = len(pairs)

    def body(*refs):
        ins, outs = refs[:n], refs[n:2 * n]
        send_sems, recv_sems = refs[2 * n:]
        x, y, c = lax.axis_index("x"), lax.axis_index("y"), lax.axis_index("c")
        copies = []
        for k, chip in enumerate(_other_chips(x, y)):
            for t in range(n):
                copies.append(pltpu.make_async_remote_copy(
                    src_ref=ins[t].at[2 * chip[0] + chip[1]], dst_ref=outs[t].at[k], send_sem=send_sems.at[3 * t + k],
                    recv_sem=recv_sems.at[3 * t + k], device_id=(*chip, c), device_id_type=MESH))
        for cp in copies:
            cp.start()
        for cp in copies:
            cp.wait()

    return pl.pallas_call(
        body, name="rs_cross", in_specs=[_any_spec()] * n, out_specs=[_any_spec()] * n,
        out_shape=[jax.ShapeDtypeStruct((3,) + p.shape[1:], p.dtype) for p in pairs],
        scratch_shapes=[pltpu.SemaphoreType.DMA((3 * n,)), pltpu.SemaphoreType.DMA((3 * n,))],
        compiler_params=pltpu.CompilerParams(has_side_effects=True))(*pairs)


def rs_share(halves):
    n = len(halves)

    def body(*refs):
        ins, outs = refs[:n], refs[n:2 * n]
        send_sems, recv_sems, local_sems = refs[2 * n:]
        x, y, c = lax.axis_index("x"), lax.axis_index("y"), lax.axis_index("c")
        mine = [pltpu.make_async_copy(ins[t], _half(outs[t], c), local_sems.at[t]) for t in range(n)]
        sends = [pltpu.make_async_remote_copy(
            src_ref=ins[t], dst_ref=_half(outs[t], c), send_sem=send_sems.at[t], recv_sem=recv_sems.at[t],
            device_id=(x, y, 1 - c), device_id_type=MESH) for t in range(n)]
        for cp in mine + sends:
            cp.start()
        for t in range(n):
            pltpu.make_async_remote_copy(
                src_ref=ins[t], dst_ref=_half(outs[t], 1 - c), send_sem=send_sems.at[t], recv_sem=recv_sems.at[t],
                device_id=(x, y, 1 - c), device_id_type=MESH).wait_recv()
        for cp in sends:
            cp.wait_send()
        for cp in mine:
            cp.wait()

    return pl.pallas_call(
        body, name="rs_share", in_specs=[_any_spec()] * n, out_specs=[_any_spec()] * n,
        out_shape=[jax.ShapeDtypeStruct((2 * h.shape[0],) + h.shape[1:], F32) for h in halves],
        scratch_shapes=[pltpu.SemaphoreType.DMA((n,)), pltpu.SemaphoreType.DMA((n,)), pltpu.SemaphoreType.DMA((n,))],
        compiler_params=pltpu.CompilerParams(has_side_effects=True))(*halves)


def _row_tile(rows, target):
    best = None
    for t in range(8, min(rows, target) + 1, 8):
        if rows % t == 0:
            best = t
    return rows if best is None else best


TILE_BYTES = 2 * 1024 * 1024


def _flat_rows(shape):
    rows = int(np.prod(shape[1:-1]))
    cols_padded = -(-shape[-1] // LANES) * LANES
    return rows, _row_tile(rows, max(8, TILE_BYTES // (4 * cols_padded)))


def pair_add(g, from_sibling, out_dtype):
    cols = g.shape[-1]
    rph, t = _flat_rows(from_sibling.shape)
    nt = rph // t
    c_arr = lax.axis_index("c").astype(jnp.int32).reshape(1)

    def body(c_ref, a_ref, b_ref, o_ref):
        o_ref[...] = (a_ref[...] + b_ref[...]).astype(o_ref.dtype)

    out = pl.pallas_call(
        body, name="pair_add",
        grid_spec=pltpu.PrefetchScalarGridSpec(
            num_scalar_prefetch=1, grid=(N_CHIPS, nt),
            in_specs=[pl.BlockSpec((t, cols), lambda j, i, c_ref: (j * 2 * nt + c_ref[0] * nt + i, 0)),
                      pl.BlockSpec((t, cols), lambda j, i, c_ref: (j * nt + i, 0))],
            out_specs=pl.BlockSpec((t, cols), lambda j, i, c_ref: (j * nt + i, 0))),
        out_shape=jax.ShapeDtypeStruct((N_CHIPS * rph, cols), out_dtype),
        compiler_params=_cparams(("parallel", "parallel")))(c_arr, g.reshape(-1, cols), from_sibling.reshape(-1, cols))
    return out.reshape(from_sibling.shape)


def chip_add(pairs, received):
    cols = pairs.shape[-1]
    rph, t = _flat_rows(pairs.shape)
    nt = rph // t
    j_arr = _chip_index().astype(jnp.int32).reshape(1)
    r2 = received.reshape(-1, cols)

    def body(j_ref, a_ref, r0_ref, r1_ref, r2_ref, o_ref):
        a, r0, r1, r2 = [ref[...].astype(F32) for ref in (a_ref, r0_ref, r1_ref, r2_ref)]
        o_ref[...] = (a + r0) + (r1 + r2)

    out = pl.pallas_call(
        body, name="chip_add",
        grid_spec=pltpu.PrefetchScalarGridSpec(
            num_scalar_prefetch=1, grid=(nt,),
            in_specs=[pl.BlockSpec((t, cols), lambda i, j_ref: (j_ref[0] * nt + i, 0)),
                      pl.BlockSpec((t, cols), lambda i, j_ref: (i, 0)),
                      pl.BlockSpec((t, cols), lambda i, j_ref: (nt + i, 0)),
                      pl.BlockSpec((t, cols), lambda i, j_ref: (2 * nt + i, 0))],
            out_specs=pl.BlockSpec((t, cols), lambda i, j_ref: (i, 0))),
        out_shape=jax.ShapeDtypeStruct((rph, cols), F32),
        compiler_params=_cparams(("parallel",)))(j_arr, pairs.reshape(-1, cols), r2, r2, r2)
    return out.reshape(pairs.shape[1:])


def reduce_grads(gs, cross_dtypes):
    pairs = [pair_add(g, r, dt) for g, r, dt in zip(gs, rs_pair(gs), cross_dtypes)]
    return rs_share([chip_add(p, r) for p, r in zip(pairs, rs_cross(pairs))])


def adamw(w, g, m, v):
    shape = w.shape
    cols = shape[-1]
    rows = int(np.prod(shape[:-1]))
    flat = lambda a: a.reshape(rows, cols)
    t = _row_tile(rows, 256)

    def body(w_ref, g_ref, m_ref, v_ref, d_ref, mo_ref, vo_ref):
        gv = g_ref[...]
        m_new = ADAM_B1 * m_ref[...] + (1.0 - ADAM_B1) * gv
        v_new = ADAM_B2 * v_ref[...] + (1.0 - ADAM_B2) * (gv * gv)
        m_hat = m_new / (1.0 - ADAM_B1 ** ADAM_STEP)
        v_hat = v_new / (1.0 - ADAM_B2 ** ADAM_STEP)
        d_ref[...] = -ADAM_LR * (m_hat / (jnp.sqrt(v_hat) + ADAM_EPS) + ADAM_WD * w_ref[...])
        mo_ref[...] = m_new
        vo_ref[...] = v_new

    spec = pl.BlockSpec((t, cols), lambda i: (i, 0))
    outs = pl.pallas_call(
        body, name="adamw", grid=(rows // t,), in_specs=[spec] * 4, out_specs=[spec] * 3,
        out_shape=[jax.ShapeDtypeStruct((rows, cols), F32)] * 3,
        compiler_params=_cparams(("parallel",)))(flat(w), flat(g), flat(m), flat(v))
    return tuple(o.reshape(shape) for o in outs)


SHARDED = (("w_mod", 2), ("w_in", 2), ("mla_q_up", 2), ("mla_kv_up", 2), ("gdn_conv_w", 2), ("w_out", 1))
REPLICATED = ("b_mod", "pre_norm_w", "post_norm_w", "mla_q_norm_w", "mla_kv_norm_w", "gdn_a_log", "gdn_dt_bias",
              "gdn_o_norm_w")
WEIGHT_ORDER = ("w_mod", "b_mod", "pre_norm_w", "post_norm_w", "w_in", "mla_q_norm_w", "mla_q_up", "mla_kv_norm_w",
                "mla_kv_up", "gdn_conv_w", "gdn_a_log", "gdn_dt_bias", "gdn_o_norm_w", "w_out")
EXACT_F32 = ("gdn_conv_w",)
SMALL_ROWS = 48


def _gather_weights(shards):
    names = [name for name, _ in SHARDED]
    gathered = ag_weights([shards[n] if n in EXACT_F32 else shards[n].astype(BF16) for n in names])
    full = {}
    for (name, axis), blk in zip(SHARDED, gathered):
        shp = shards[name].shape
        blk = jnp.moveaxis(blk.astype(F32), 0, axis)
        full[name] = blk.reshape(shp[:axis] + (N_CHIPS * shp[axis],) + shp[axis + 1:])
    return full


def _split_grads(grads):
    pieces = []
    for name, axis in SHARDED:
        g = grads[name]
        shp = g.shape
        g = g.reshape(shp[:axis] + (N_CHIPS, shp[axis] // N_CHIPS) + shp[axis + 1:])
        pieces.append(jnp.moveaxis(g, axis, 0))
    small = jnp.concatenate([grads[name] for name in REPLICATED], axis=1)
    small = jnp.pad(small, ((0, 0), (0, SMALL_ROWS * LANES - small.shape[1]))).reshape(DEPTH, SMALL_ROWS, LANES)
    pieces.append(jnp.broadcast_to(small[None], (N_CHIPS,) + small.shape))
    return pieces


def _unsplit_small(small, rep_shapes):
    flat = small.reshape(DEPTH, SMALL_ROWS * LANES)
    out, off = {}, 0
    for name in REPLICATED:
        size = rep_shapes[name][1]
        out[name] = flat[:, off:off + size]
        off += size
    return out


def kernel(x, c, positions, w_mod, b_mod, pre_norm_w, post_norm_w, w_in, mla_q_norm_w, mla_q_up, mla_kv_norm_w, mla_kv_up, gdn_conv_w, gdn_a_log, gdn_dt_bias, gdn_o_norm_w, w_out, loss_target, m_w_mod, m_b_mod, m_pre_norm_w, m_post_norm_w, m_w_in, m_mla_q_norm_w, m_mla_q_up, m_mla_kv_norm_w, m_mla_kv_up, m_gdn_conv_w, m_gdn_a_log, m_gdn_dt_bias, m_gdn_o_norm_w, m_w_out, v_w_mod, v_b_mod, v_pre_norm_w, v_post_norm_w, v_w_in, v_mla_q_norm_w, v_mla_q_up, v_mla_kv_norm_w, v_mla_kv_up, v_gdn_conv_w, v_gdn_a_log, v_gdn_dt_bias, v_gdn_o_norm_w, v_w_out):
    given = dict(w_mod=w_mod, b_mod=b_mod, pre_norm_w=pre_norm_w, post_norm_w=post_norm_w, w_in=w_in,
                 mla_q_norm_w=mla_q_norm_w, mla_q_up=mla_q_up, mla_kv_norm_w=mla_kv_norm_w, mla_kv_up=mla_kv_up,
                 gdn_conv_w=gdn_conv_w, gdn_a_log=gdn_a_log, gdn_dt_bias=gdn_dt_bias, gdn_o_norm_w=gdn_o_norm_w,
                 w_out=w_out)
    moments_m = dict(w_mod=m_w_mod, b_mod=m_b_mod, pre_norm_w=m_pre_norm_w, post_norm_w=m_post_norm_w, w_in=m_w_in,
                     mla_q_norm_w=m_mla_q_norm_w, mla_q_up=m_mla_q_up, mla_kv_norm_w=m_mla_kv_norm_w,
                     mla_kv_up=m_mla_kv_up, gdn_conv_w=m_gdn_conv_w, gdn_a_log=m_gdn_a_log,
                     gdn_dt_bias=m_gdn_dt_bias, gdn_o_norm_w=m_gdn_o_norm_w, w_out=m_w_out)
    moments_v = dict(w_mod=v_w_mod, b_mod=v_b_mod, pre_norm_w=v_pre_norm_w, post_norm_w=v_post_norm_w, w_in=v_w_in,
                     mla_q_norm_w=v_mla_q_norm_w, mla_q_up=v_mla_q_up, mla_kv_norm_w=v_mla_kv_norm_w,
                     mla_kv_up=v_mla_kv_up, gdn_conv_w=v_gdn_conv_w, gdn_a_log=v_gdn_a_log,
                     gdn_dt_bias=v_gdn_dt_bias, gdn_o_norm_w=v_gdn_o_norm_w, w_out=v_w_out)

    full = _gather_weights({name: given[name] for name, _ in SHARDED})
    for name in REPLICATED:
        full[name] = given[name]
    loss_local, (grads, grad_x) = jax.value_and_grad(_local_loss, argnums=(0, 1))(
        full, x[0], c, positions[0], loss_target[0])
    loss = lax.psum(loss_local, AXES)

    cross_dtypes = [F32 if name in EXACT_F32 else BF16 for name, _ in SHARDED] + [F32]
    reduced = reduce_grads(_split_grads(grads), cross_dtypes)
    grad_w = {name: g for (name, _), g in zip(SHARDED, reduced)}
    grad_w.update(_unsplit_small(reduced[-1], {name: given[name].shape for name in REPLICATED}))
    delta, new_m, new_v = {}, {}, {}
    for name in WEIGHT_ORDER:
        delta[name], new_m[name], new_v[name] = adamw(given[name], grad_w[name], moments_m[name], moments_v[name])
    return (loss, grad_x[None], *[grad_w[n] for n in WEIGHT_ORDER], *[delta[n] for n in WEIGHT_ORDER],
            *[new_m[n] for n in WEIGHT_ORDER], *[new_v[n] for n in WEIGHT_ORDER])
```

```python
import functools
import math

import numpy as np
import jax
import jax.numpy as jnp
from jax import lax
from jax.experimental import pallas as pl
from jax.experimental.pallas import tpu as pltpu

F32 = jnp.float32
BF16 = jnp.bfloat16
MESH = pl.DeviceIdType.MESH
AXES = ("x", "y", "c")

D_MODEL = 1024
DEPTH = 4
MLA_HEADS = 4
MLA_NOPE = 128
MLA_ROPE = 64
MLA_V = 128
MLA_Q_RANK = 384
MLA_KV_RANK = 256
MLA_WIDTH = 512
GDN_HEADS = 4
GDN_DK = 128
GDN_WIDTH = 512
GDN_QKV = 1536
GDN_CONV = 4
IN_COLS = 3272
ROPE_THETA = 10000.0
NORM_EPS = 1e-6
ADAM_LR, ADAM_B1, ADAM_B2, ADAM_EPS, ADAM_WD, ADAM_STEP = 0.001, 0.9, 0.999, 1e-08, 0.01, 10

LANES = 128
N_CHIPS = 4
GDN_CHUNK = 128
VMEM_LIMIT = 56 * 1024 * 1024


def _cparams(sem=None):
    if sem is None:
        return pltpu.CompilerParams(vmem_limit_bytes=VMEM_LIMIT)
    return pltpu.CompilerParams(dimension_semantics=sem, vmem_limit_bytes=VMEM_LIMIT)


def _pick(dim, target):
    if dim <= target:
        return dim
    best = None
    for t in range(LANES, target + 1, LANES):
        if dim % t == 0:
            best = t
    assert best is not None, (dim, target)
    return best


_DN = {"nn": (((1,), (0,)), ((), ())), "nt": (((1,), (1,)), ((), ())), "tn": (((0,), (0,)), ((), ()))}


def _dot_raw(a, b, mode, exact):
    if exact:
        return lax.dot_general(a, b, _DN[mode], precision=lax.Precision.HIGHEST, preferred_element_type=F32)
    return lax.dot_general(a.astype(BF16), b.astype(BF16), _DN[mode], preferred_element_type=F32)


def _dot_split(a, b):
    a_hi, b_hi = a.astype(BF16), b.astype(BF16)
    a_lo, b_lo = (a - a_hi.astype(F32)).astype(BF16), (b - b_hi.astype(F32)).astype(BF16)
    dot = lambda u, w: lax.dot_general(u, w, _DN["nn"], preferred_element_type=F32)
    return dot(a_hi, b_hi) + (dot(a_hi, b_lo) + dot(a_lo, b_hi))


@functools.partial(jax.custom_vjp, nondiff_argnums=(2, 3))
def bdot(a, b, mode="nn", exact=False):
    return _dot_raw(a, b, mode, exact)


def _bdot_fwd(a, b, mode, exact):
    return _dot_raw(a, b, mode, exact), (a, b)


def _bdot_bwd(mode, exact, res, g):
    a, b = res
    if mode == "nn":
        return bdot(g, b, "nt", exact), bdot(a, g, "tn", exact)
    if mode == "nt":
        return bdot(g, b, "nn", exact), bdot(g, a, "tn", exact)
    return bdot(b, g, "nt", exact), bdot(a, g, "nn", exact)


bdot.defvjp(_bdot_fwd, _bdot_bwd)


@jax.custom_vjp
def roll_half(x):
    return pltpu.roll(x, 64, 1)


roll_half.defvjp(lambda x: (pltpu.roll(x, 64, 1), None), lambda _, g: (pltpu.roll(g, 64, 1),))


def _sigmoid(x):
    return 1.0 / (1.0 + jnp.exp(-x))


def _silu(x):
    return x * _sigmoid(x)


def _softplus(x):
    return jnp.maximum(x, 0.0) + jnp.log(1.0 + jnp.exp(-jnp.abs(x)))


def _rms(x, w):
    return x * lax.rsqrt(jnp.mean(x * x, axis=-1, keepdims=True) + NORM_EPS) * w


def _rw_fwd(fn, name, rows, params, out_dims, out_dtypes, acc_dims, tile):
    s = rows[0].shape[0]
    t = min(tile, s)
    n = s // t
    nr, npar, no, na = len(rows), len(params), len(out_dims), len(acc_dims)

    def body(*refs):
        r, p = refs[:nr], refs[nr:nr + npar]
        o, a = refs[nr + npar:nr + npar + no], refs[nr + npar + no:]
        outs = fn(*[x[...] for x in r], *[x[...] for x in p])
        for ref, val in zip(o, outs[:no]):
            ref[...] = val.astype(ref.dtype)
        if na:
            @pl.when(pl.program_id(0) == 0)
            def _():
                for ref in a:
                    ref[...] = jnp.zeros_like(ref)
            for ref, val in zip(a, outs[no:]):
                ref[...] += val

    in_specs = [pl.BlockSpec((t, x.shape[1]), lambda i: (i, 0)) for x in rows]
    in_specs += [pl.BlockSpec(x.shape, lambda i: (0, 0)) for x in params]
    out_specs = [pl.BlockSpec((t, d), lambda i: (i, 0)) for d in out_dims]
    out_specs += [pl.BlockSpec((1, d), lambda i: (0, 0)) for d in acc_dims]
    out_shape = [jax.ShapeDtypeStruct((s, d), dt) for d, dt in zip(out_dims, out_dtypes)]
    out_shape += [jax.ShapeDtypeStruct((1, d), F32) for d in acc_dims]
    res = pl.pallas_call(body, name=name, grid=(n,), in_specs=in_specs, out_specs=out_specs, out_shape=out_shape,
                         compiler_params=_cparams(("arbitrary",)))(*rows, *params)
    return tuple(res)


def _rw_bwd(fn, name, rows, params, row_cts, acc_cts, n_diff, tile):
    s = rows[0].shape[0]
    t = min(tile, s)
    n = s // t
    nr, npar, no, na = len(rows), len(params), len(row_cts), len(acc_cts)

    def body(*refs):
        r, p = refs[:nr], refs[nr:nr + npar]
        g, ga = refs[nr + npar:nr + npar + no], refs[nr + npar + no:nr + npar + no + na]
        dr, dp = refs[nr + npar + no + na:nr + npar + no + na + n_diff], refs[nr + npar + no + na + n_diff:]
        _, vjp = jax.vjp(fn, *[x[...] for x in r], *[x[...] for x in p])
        cts = vjp(tuple([x[...] for x in g] + [x[...] for x in ga]))
        for ref, val in zip(dr, cts[:n_diff]):
            ref[...] = val
        if npar:
            @pl.when(pl.program_id(0) == 0)
            def _():
                for ref in dp:
                    ref[...] = jnp.zeros_like(ref)
            for ref, val in zip(dp, cts[nr:]):
                ref[...] += val

    in_specs = [pl.BlockSpec((t, x.shape[1]), lambda i: (i, 0)) for x in rows]
    in_specs += [pl.BlockSpec(x.shape, lambda i: (0, 0)) for x in params]
    in_specs += [pl.BlockSpec((t, x.shape[1]), lambda i: (i, 0)) for x in row_cts]
    in_specs += [pl.BlockSpec(x.shape, lambda i: (0, 0)) for x in acc_cts]
    out_specs = [pl.BlockSpec((t, x.shape[1]), lambda i: (i, 0)) for x in rows[:n_diff]]
    out_specs += [pl.BlockSpec(x.shape, lambda i: (0, 0)) for x in params]
    out_shape = [jax.ShapeDtypeStruct(x.shape, F32) for x in rows[:n_diff]]
    out_shape += [jax.ShapeDtypeStruct(x.shape, F32) for x in params]
    res = pl.pallas_call(body, name=name, grid=(n,), in_specs=in_specs, out_specs=out_specs, out_shape=out_shape,
                         compiler_params=_cparams(("arbitrary",)))(*rows, *params, *row_cts, *acc_cts)
    return tuple(res[:n_diff]), tuple(res[n_diff:])


def make_rowwise(fn, name, out_dims, acc_dims=(), n_nondiff=0, tile=256):
    out_dtypes = (F32,) * len(out_dims)

    @jax.custom_vjp
    def op(rows, params):
        return _rw_fwd(fn, name, rows, params, out_dims, out_dtypes, acc_dims, tile)

    def fwd(rows, params):
        return op(rows, params), (rows, params)

    def bwd(res, cts):
        rows, params = res
        n_diff = len(rows) - n_nondiff
        d_rows, d_params = _rw_bwd(fn, name + "_bwd", rows, params, cts[:len(out_dims)], cts[len(out_dims):],
                                   n_diff, tile)
        d_rows = d_rows + tuple(jnp.zeros_like(x) for x in rows[n_diff:])
        return d_rows, d_params

    op.defvjp(fwd, bwd)
    return op


def _mm(a, b, mode, name):
    if mode == "nn":
        (m, k), (_, n) = a.shape, b.shape
    elif mode == "nt":
        (m, k), (n, _) = a.shape, b.shape
    else:
        (k, m), (_, n) = a.shape, b.shape
    tm = _pick(m, 512)
    tn = _pick(n, 1152)
    tk = _pick(k, 1152) if mode != "tn" else _pick(k, 512)
    nk = k // tk

    def body(a_ref, b_ref, o_ref, acc_ref):
        kk = pl.program_id(2)

        @pl.when(kk == 0)
        def _():
            acc_ref[...] = jnp.zeros_like(acc_ref)

        acc_ref[...] += _dot_raw(a_ref[...], b_ref[...], mode, False)

        @pl.when(kk == nk - 1)
        def _():
            o_ref[...] = acc_ref[...]

    if mode == "nn":
        a_spec = pl.BlockSpec((tm, tk), lambda i, j, kk: (i, kk))
        b_spec = pl.BlockSpec((tk, tn), lambda i, j, kk: (kk, j))
    elif mode == "nt":
        a_spec = pl.BlockSpec((tm, tk), lambda i, j, kk: (i, kk))
        b_spec = pl.BlockSpec((tn, tk), lambda i, j, kk: (j, kk))
    else:
        a_spec = pl.BlockSpec((tk, tm), lambda i, j, kk: (kk, i))
        b_spec = pl.BlockSpec((tk, tn), lambda i, j, kk: (kk, j))
    return pl.pallas_call(
        body, name=name, grid=(m // tm, n // tn, nk), in_specs=[a_spec, b_spec],
        out_specs=pl.BlockSpec((tm, tn), lambda i, j, kk: (i, j)),
        out_shape=jax.ShapeDtypeStruct((m, n), F32), scratch_shapes=[pltpu.VMEM((tm, tn), F32)],
        compiler_params=_cparams(("parallel", "parallel", "arbitrary")))(a, b)


def make_linear(name):
    @jax.custom_vjp
    def op(a, w):
        return _mm(a, w.astype(BF16), "nn", name)

    def fwd(a, w):
        w16 = w.astype(BF16)
        return _mm(a, w16, "nn", name), (a, w16)

    def bwd(res, g):
        a, w16 = res
        return _mm(g, w16, "nt", name + "_dx"), _mm(a, g, "tn", name + "_dw")

    op.defvjp(fwd, bwd)
    return op


def _fan_mm(xs, ws, transposed, name, tile=256):
    n_in, n_out = len(ws), len(ws[0])
    s = xs[0].shape[0]
    t = min(tile, s)
    out_dims = [ws[i][0].shape[0] for i in range(n_in)] if transposed else [ws[0][j].shape[1] for j in range(n_out)]
    flat_ws = [w for row in ws for w in row]

    def body(*refs):
        x_refs, w_refs, o_refs = refs[:len(xs)], refs[len(xs):len(xs) + len(flat_ws)], refs[len(xs) + len(flat_ws):]
        xv = [r[...].astype(BF16) for r in x_refs]
        for o, o_ref in enumerate(o_refs):
            if transposed:
                terms = [_dot_raw(xv[j], w_refs[o * n_out + j][...], "nt", False) for j in range(n_out)]
            else:
                terms = [_dot_raw(xv[i], w_refs[i * n_out + o][...], "nn", False) for i in range(n_in)]
            o_ref[...] = functools.reduce(lambda a, b: a + b, terms)

    in_specs = [pl.BlockSpec((t, x.shape[1]), lambda i: (i, 0)) for x in xs]
    in_specs += [pl.BlockSpec(w.shape, lambda i: (0, 0)) for w in flat_ws]
    res = pl.pallas_call(
        body, name=name, grid=(s // t,), in_specs=in_specs,
        out_specs=[pl.BlockSpec((t, d), lambda i: (i, 0)) for d in out_dims],
        out_shape=[jax.ShapeDtypeStruct((s, d), F32) for d in out_dims],
        compiler_params=_cparams(("parallel",)))(*xs, *flat_ws)
    return tuple(res)


def _fan_dw(x, dys, name, tile=256):
    s, k = x.shape
    t = min(tile, s)

    def body(x_ref, *refs):
        dy_refs, o_refs = refs[:len(dys)], refs[len(dys):]

        @pl.when(pl.program_id(0) == 0)
        def _():
            for o_ref in o_refs:
                o_ref[...] = jnp.zeros_like(o_ref)

        xt = x_ref[...].astype(BF16)
        for dy_ref, o_ref in zip(dy_refs, o_refs):
            o_ref[...] += _dot_raw(xt, dy_ref[...], "tn", False)

    rows = lambda d: pl.BlockSpec((t, d), lambda i: (i, 0))
    res = pl.pallas_call(
        body, name=name, grid=(s // t,), in_specs=[rows(k)] + [rows(dy.shape[1]) for dy in dys],
        out_specs=[pl.BlockSpec((k, dy.shape[1]), lambda i: (0, 0)) for dy in dys],
        out_shape=[jax.ShapeDtypeStruct((k, dy.shape[1]), F32) for dy in dys],
        compiler_params=_cparams(("arbitrary",)))(x, *dys)
    return tuple(res)


def make_fan_linear(name):
    @jax.custom_vjp
    def op(xs, ws):
        return _fan_mm(xs, tuple(tuple(w.astype(BF16) for w in row) for row in ws), False, name)

    def fwd(xs, ws):
        ws16 = tuple(tuple(w.astype(BF16) for w in row) for row in ws)
        return _fan_mm(xs, ws16, False, name), (xs, ws16)

    def bwd(res, dys):
        xs, ws16 = res
        dxs = _fan_mm(dys, ws16, True, name + "_dx")
        dws = tuple(_fan_dw(x, dys, name + "_dw") for x in xs)
        return dxs, dws

    op.defvjp(fwd, bwd)
    return op


def _mla_prep_fn(qraw, kvraw, kr, t1, t2):
    kr_rot = kr * t1 + roll_half(kr) * t2
    qs, ks = [], []
    for h in range(MLA_HEADS):
        q_r = qraw[:, h * 256 + 128:(h + 1) * 256]
        qs += [qraw[:, h * 256:h * 256 + 128], q_r * t1 + roll_half(q_r) * t2]
        ks += [kvraw[:, h * 128:(h + 1) * 128], kr_rot]
    return jnp.concatenate(qs, axis=1), jnp.concatenate(ks, axis=1), kvraw[:, 512:]


def _flash_tile(s):
    return 512 if s >= 2048 else 128


FLASH_SCALE = (MLA_NOPE + MLA_ROPE) ** -0.5
STAT_ROWS = 8


def _as_rows(col_b):
    ones = jnp.full((STAT_ROWS, LANES), 1.0 / LANES, F32)
    return _dot_raw(ones, col_b, "nt", True)


def _flash_fwd(qf, kf, vf):
    s = qf.shape[0]
    t = _flash_tile(s)
    nb = s // t

    def body(q_ref, k_ref, v_ref, o_ref, lse_ref, m_sc, l_sc, acc_sc):
        i = pl.program_id(1)
        m_sc[...] = jnp.full_like(m_sc, -1e30)
        l_sc[...] = jnp.zeros_like(l_sc)
        acc_sc[...] = jnp.zeros_like(acc_sc)
        q = q_ref[...]

        def step(j, on_diagonal):
            rows = pl.ds(pl.multiple_of(j * t, t), t)
            st = _dot_raw(k_ref[rows, :], q, "nt", False) * FLASH_SCALE
            if on_diagonal:
                keep = lax.broadcasted_iota(jnp.int32, (t, t), 0) <= lax.broadcasted_iota(jnp.int32, (t, t), 1)
                st = jnp.where(keep, st, -1e30)
            m_old = m_sc[...]
            m_new = jnp.maximum(m_old, jnp.max(st, axis=0, keepdims=True))
            alpha = jnp.exp(m_old - m_new)
            pt = jnp.exp(st - m_new)
            l_sc[...] = alpha * l_sc[...] + jnp.sum(pt, axis=0, keepdims=True)
            acc_sc[...] = alpha * acc_sc[...] + _dot_raw(v_ref[rows, :], pt, "tn", False)
            m_sc[...] = m_new

        def two_steps(p, carry):
            step(2 * p, False)
            step(2 * p + 1, False)
            return carry

        lax.fori_loop(0, i // 2, two_steps, 0)

        @pl.when(i % 2 == 1)
        def _():
            step(i - 1, False)

        step(i, True)
        o_ref[...] = (acc_sc[...] / l_sc[...]).T
        lse_ref[0] = jnp.broadcast_to(m_sc[...] + jnp.log(l_sc[...]), (STAT_ROWS, t))

    return pl.pallas_call(
        body, name="flash_fwd", grid=(MLA_HEADS, nb),
        in_specs=[pl.BlockSpec((t, 256), lambda h, i: (i, h)), pl.BlockSpec((s, 256), lambda h, i: (0, h)),
                  pl.BlockSpec((s, 128), lambda h, i: (0, h))],
        out_specs=[pl.BlockSpec((t, 128), lambda h, i: (i, h)),
                   pl.BlockSpec((1, STAT_ROWS, t), lambda h, i: (h, 0, i))],
        out_shape=[jax.ShapeDtypeStruct((s, MLA_WIDTH), F32),
                   jax.ShapeDtypeStruct((MLA_HEADS, STAT_ROWS, s), F32)],
        scratch_shapes=[pltpu.VMEM((1, t), F32), pltpu.VMEM((1, t), F32), pltpu.VMEM((128, t), F32)],
        compiler_params=_cparams(("parallel", "arbitrary")))(qf, kf, vf)


def _flash_bwd_prep(o, do):
    s = o.shape[0]
    t = _flash_tile(s)

    def body(o_ref, do_ref, dl_ref, do16_ref):
        do_t = do_ref[...]
        delta = jnp.sum(do_t * o_ref[...], axis=1, keepdims=True)
        dl_ref[0] = _as_rows(jnp.broadcast_to(delta, (t, LANES)))
        do16_ref[...] = do_t.astype(BF16)

    blk = pl.BlockSpec((t, 128), lambda h, i: (i, h))
    return pl.pallas_call(
        body, name="flash_bwd_prep", grid=(MLA_HEADS, s // t), in_specs=[blk, blk],
        out_specs=[pl.BlockSpec((1, STAT_ROWS, t), lambda h, i: (h, 0, i)), blk],
        out_shape=[jax.ShapeDtypeStruct((MLA_HEADS, STAT_ROWS, s), F32), jax.ShapeDtypeStruct((s, MLA_WIDTH), BF16)],
        compiler_params=_cparams(("parallel", "parallel")))(o, do)


def _flash_bwd(qf, kf, vf, lse, delta, do16):
    s = qf.shape[0]
    t = _flash_tile(s)
    nb = s // t

    def body(q_ref, k_ref, v_ref, lse_ref, dl_ref, do_ref, dq_ref, dk_ref, dv_ref):
        j = pl.program_id(1)

        @pl.when(j == 0)
        def _():
            dq_ref[...] = jnp.zeros_like(dq_ref)

        dk_ref[...] = jnp.zeros_like(dk_ref)
        dv_ref[...] = jnp.zeros_like(dv_ref)
        k = k_ref[...]
        v = v_ref[...]

        def step(i, on_diagonal):
            rows = pl.ds(pl.multiple_of(i * t, t), t)
            q = q_ref[rows, :]
            do_t = do_ref[rows, :]
            st = _dot_raw(k, q, "nt", False) * FLASH_SCALE - lse_ref[0, 0:1, rows]
            if on_diagonal:
                keep = lax.broadcasted_iota(jnp.int32, (t, t), 0) <= lax.broadcasted_iota(jnp.int32, (t, t), 1)
                st = jnp.where(keep, st, -1e30)
            pt = jnp.exp(st)
            dst = pt * (_dot_raw(v, do_t, "nt", False) - dl_ref[0, 0:1, rows])
            dv_ref[...] += _dot_raw(pt, do_t, "nn", False)
            dk_ref[...] += _dot_raw(dst, q, "nn", False) * FLASH_SCALE
            dq_ref[rows, :] += _dot_raw(dst, k, "tn", False) * FLASH_SCALE

        def two_steps(p, carry):
            step(j + 1 + 2 * p, False)
            step(j + 2 + 2 * p, False)
            return carry

        step(j, True)
        below = nb - 1 - j
        lax.fori_loop(0, below // 2, two_steps, 0)

        @pl.when(below % 2 == 1)
        def _():
            step(nb - 1, False)

    stat = pl.BlockSpec((1, STAT_ROWS, s), lambda h, j: (h, 0, 0))
    return pl.pallas_call(
        body, name="flash_bwd", grid=(MLA_HEADS, nb),
        in_specs=[pl.BlockSpec((s, 256), lambda h, j: (0, h)), pl.BlockSpec((t, 256), lambda h, j: (j, h)),
                  pl.BlockSpec((t, 128), lambda h, j: (j, h)), stat, stat, pl.BlockSpec((s, 128), lambda h, j: (0, h))],
        out_specs=[pl.BlockSpec((s, 256), lambda h, j: (0, h)), pl.BlockSpec((t, 256), lambda h, j: (j, h)),
                   pl.BlockSpec((t, 128), lambda h, j: (j, h))],
        out_shape=[jax.ShapeDtypeStruct((s, 1024), F32), jax.ShapeDtypeStruct((s, 1024), F32),
                   jax.ShapeDtypeStruct((s, MLA_WIDTH), F32)],
        compiler_params=_cparams(("parallel", "arbitrary")))(qf, kf, vf, lse, delta, do16)


def _mla_prep(qraw, kvraw, kr, t1, t2):
    return _rw_fwd(_mla_prep_fn, "mla_prep", (qraw, kvraw, kr, t1, t2), (), (1024, 1024, 512), (BF16, BF16, BF16), (),
                   256)


@jax.custom_vjp
def mla_attention(qraw, kvraw, kr, t1, t2):
    return _flash_fwd(*_mla_prep(qraw, kvraw, kr, t1, t2))[0]


def _mla_attention_fwd(qraw, kvraw, kr, t1, t2):
    qf, kf, vf = _mla_prep(qraw, kvraw, kr, t1, t2)
    o, lse = _flash_fwd(qf, kf, vf)
    return o, (qraw, kvraw, kr, t1, t2, qf, kf, vf, o, lse)


def _mla_attention_bwd(res, do):
    qraw, kvraw, kr, t1, t2, qf, kf, vf, o, lse = res
    delta, do16 = _flash_bwd_prep(o, do)
    dq, dk, dv = _flash_bwd(qf, kf, vf, lse, delta, do16)
    (dqraw, dkvraw, dkr), _ = _rw_bwd(_mla_prep_fn, "mla_prep_bwd", (qraw, kvraw, kr, t1, t2), (), (dq, dk, dv), (),
                                      3, 256)
    return dqraw, dkvraw, dkr, jnp.zeros_like(t1), jnp.zeros_like(t2)


mla_attention.defvjp(_mla_attention_fwd, _mla_attention_bwd)


def _lane_pick(x, lane):
    ids = lax.broadcasted_iota(jnp.int32, x.shape, 1)
    col = jnp.sum(jnp.where(ids == lane, x, 0.0), axis=1, keepdims=True)
    return jnp.broadcast_to(col, x.shape)


GDN_HALO = 8


@functools.partial(jax.custom_vjp, nondiff_argnums=(1,))
def _roll_rows(x, d):
    return pltpu.roll(x, d, 0)


_roll_rows.defvjp(lambda x, d: (pltpu.roll(x, d, 0), None), lambda d, _, g: (pltpu.roll(g, g.shape[0] - d, 0),))


def _gdn_prep_fn(prev, cur, ab, w0, w1, w2, w3, a_log, dt_bias):
    xcat = jnp.concatenate([prev, cur], axis=0)
    x0, x1, x2 = [_roll_rows(xcat, GDN_CONV - 1 - j)[GDN_HALO:] for j in range(GDN_CONV - 1)]
    qkv = _silu(x0 * w0 + x1 * w1 + x2 * w2 + cur * w3)
    g_all = -jnp.exp(a_log) * _softplus(ab + dt_bias)
    beta_all = _sigmoid(ab)
    qs, ks, gs, bs = [], [], [], []
    for h in range(GDN_HEADS):
        q = qkv[:, h * 128:(h + 1) * 128]
        k = qkv[:, 512 + h * 128:512 + (h + 1) * 128]
        qs.append(q * lax.rsqrt(jnp.sum(q * q, axis=-1, keepdims=True) + NORM_EPS) * (GDN_DK ** -0.5))
        ks.append(k * lax.rsqrt(jnp.sum(k * k, axis=-1, keepdims=True) + NORM_EPS))
        gs.append(_lane_pick(g_all, h))
        bs.append(_lane_pick(beta_all, GDN_HEADS + h))
    cat = lambda xs: jnp.concatenate(xs, axis=1)
    return cat(qs), cat(ks), qkv[:, 1024:], cat(gs), cat(bs)


GDN_PREP_TILE = 256


def _gdn_prep_specs(s, params):
    t = min(GDN_PREP_TILE, s)
    prev = pl.BlockSpec((GDN_HALO, GDN_QKV), lambda i: (jnp.maximum(i * (t // GDN_HALO) - 1, 0), 0))
    rows = lambda d: pl.BlockSpec((t, d), lambda i: (i, 0))
    return t, [prev, rows(GDN_QKV), rows(LANES)] + [pl.BlockSpec(p.shape, lambda i: (0, 0)) for p in params]


def _gdn_prep_masked(first):
    return lambda prev, *rest: _gdn_prep_fn(prev * first, *rest)


def _gdn_prep_fwd(qkv, ab, params):
    s = qkv.shape[0]
    t, in_specs = _gdn_prep_specs(s, params)

    def body(prev_ref, cur_ref, ab_ref, *refs):
        p_refs, o_refs = refs[:len(params)], refs[len(params):]
        first = (pl.program_id(0) > 0).astype(F32)
        outs = _gdn_prep_masked(first)(prev_ref[...], cur_ref[...], ab_ref[...], *[p[...] for p in p_refs])
        for ref, val in zip(o_refs, outs):
            ref[...] = val

    return pl.pallas_call(
        body, name="gdn_prep", grid=(s // t,), in_specs=in_specs,
        out_specs=[pl.BlockSpec((t, GDN_WIDTH), lambda i: (i, 0))] * 5,
        out_shape=[jax.ShapeDtypeStruct((s, GDN_WIDTH), F32)] * 5,
        compiler_params=_cparams(("parallel",)))(qkv, qkv, ab, *params)


def _gdn_prep_bwd(qkv, ab, params, cts):
    s = qkv.shape[0]
    t, in_specs = _gdn_prep_specs(s, params)
    n = s // t
    npar = len(params)

    def body(prev_ref, cur_ref, ab_ref, *refs):
        p_refs, g_refs = refs[:npar], refs[npar:npar + 5]
        dcur_ref, dhalo_ref, dab_ref = refs[npar + 5:npar + 8]
        dp_refs = refs[npar + 8:]
        first = (pl.program_id(0) > 0).astype(F32)
        _, vjp = jax.vjp(_gdn_prep_masked(first), prev_ref[...], cur_ref[...], ab_ref[...], *[p[...] for p in p_refs])
        d_prev, d_cur, d_ab, *d_params = vjp(tuple(g[...] for g in g_refs))
        dcur_ref[...] = d_cur
        dhalo_ref[0] = d_prev
        dab_ref[...] = d_ab

        @pl.when(pl.program_id(0) == 0)
        def _():
            for ref in dp_refs:
                ref[...] = jnp.zeros_like(ref)

        for ref, val in zip(dp_refs, d_params):
            ref[...] += val

    rows = lambda d: pl.BlockSpec((t, d), lambda i: (i, 0))
    res = pl.pallas_call(
        body, name="gdn_prep_bwd", grid=(n,), in_specs=in_specs + [rows(GDN_WIDTH)] * 5,
        out_specs=[rows(GDN_QKV), pl.BlockSpec((1, GDN_HALO, GDN_QKV), lambda i: (i, 0, 0)), rows(LANES)]
        + [pl.BlockSpec(p.shape, lambda i: (0, 0)) for p in params],
        out_shape=[jax.ShapeDtypeStruct((s, GDN_QKV), F32), jax.ShapeDtypeStruct((n, GDN_HALO, GDN_QKV), F32),
                   jax.ShapeDtypeStruct((s, LANES), F32)] + [jax.ShapeDtypeStruct(p.shape, F32) for p in params],
        compiler_params=_cparams(("arbitrary",)))(qkv, qkv, ab, *params, *cts)
    return res[0], res[1], res[2], tuple(res[3:])


@jax.custom_vjp
def gdn_prep(qkv, ab, params):
    return tuple(_gdn_prep_fwd(qkv, ab, params))


def _gdn_prep_vjp_fwd(qkv, ab, params):
    return tuple(_gdn_prep_fwd(qkv, ab, params)), (qkv, ab, params)


def _gdn_prep_vjp_bwd(res, cts):
    qkv, ab, params = res
    d_cur, d_halo, d_ab, d_params = _gdn_prep_bwd(qkv, ab, params, cts)
    s, n = qkv.shape[0], d_halo.shape[0]
    t = s // n
    shifted = jnp.concatenate([d_halo[1:], jnp.zeros_like(d_halo[:1])], axis=0)
    halo_rows = jnp.concatenate([jnp.zeros((n, t - GDN_HALO, GDN_QKV), F32), shifted], axis=1)
    return d_cur + halo_rows.reshape(s, GDN_QKV), d_ab, d_params


gdn_prep.defvjp(_gdn_prep_vjp_fwd, _gdn_prep_vjp_bwd)


@jax.custom_vjp
def _unit_lower_inverse(lms):
    c = lms[0].shape[0]
    row = lax.broadcasted_iota(jnp.int32, (c, c), 0)
    col = lax.broadcasted_iota(jnp.int32, (c, c), 1)
    ts = [(row == col).astype(F32) - jnp.where((row >> 1) == (col >> 1), lm, 0.0) for lm in lms]
    for level in range(1, int(math.log2(c))):
        below = ((row >> (level + 1)) == (col >> (level + 1))) & ((row >> level) != (col >> level))
        mids = [_dot_split(t, jnp.where(below, lm, 0.0)) for t, lm in zip(ts, lms)]
        ts = [t - _dot_split(mid, t) for t, mid in zip(ts, mids)]
    return tuple(ts)


def _uli_fwd(lms):
    ts = _unit_lower_inverse(lms)
    return ts, ts


def _uli_bwd(ts, gs):
    mids = [bdot(t, g, "tn") for t, g in zip(ts, gs)]
    return (tuple(-bdot(mid, t, "nt") for t, mid in zip(ts, mids)),)


_unit_lower_inverse.defvjp(_uli_fwd, _uli_bwd)


@jax.custom_vjp
def _known_inverse(lms, ts):
    return ts


_known_inverse.defvjp(lambda lms, ts: (ts, ts),
                      lambda ts, gs: (_uli_bwd(ts, gs)[0], tuple(jnp.zeros_like(t) for t in ts)))


def _gdn_chunk_fn(qs, ks, vs, gbs, bbs, s0s, known_ts=None):
    heads = range(len(qs))
    c = qs[0].shape[0]
    row = lax.broadcasted_iota(jnp.int32, (c, c), 0)
    col = lax.broadcasted_iota(jnp.int32, (c, c), 1)
    incl, strict = row >= col, row > col
    tri = incl.astype(F32)
    gc = [bdot(tri, gbs[h], "nn", True) for h in heads]
    decay = [jnp.exp(jnp.where(incl, gc[h] - gc[h].T, -1e30)) for h in heads]
    g_last = [jnp.sum(gbs[h], axis=0, keepdims=True) for h in heads]
    eg = [jnp.exp(gc[h]) for h in heads]
    kb = [ks[h] * bbs[h] for h in heads]
    lms = tuple(jnp.where(strict, bdot(kb[h], ks[h], "nt") * decay[h], 0.0) for h in heads)
    ts = _unit_lower_inverse(lms) if known_ts is None else _known_inverse(lms, known_ts)
    u = [bdot(ts[h], vs[h] * bbs[h]) for h in heads]
    w = [bdot(ts[h], kb[h] * eg[h]) for h in heads]
    qk = [bdot(qs[h], ks[h], "nt") * decay[h] for h in heads]
    v_new = [u[h] - bdot(w[h], s0s[h]) for h in heads]
    o = [bdot(qs[h] * eg[h], s0s[h]) + bdot(qk[h], v_new[h]) for h in heads]
    s1 = [s0s[h] * jnp.exp(g_last[h]) + bdot(ks[h] * jnp.exp(g_last[h] - gc[h]), v_new[h], "tn") for h in heads]
    return (tuple(o), tuple(s1)), ts


def _head_tiles(ref):
    return tuple(ref[:, h * 128:(h + 1) * 128] for h in range(GDN_HEADS))


def _gdn_fwd(q, k, v, gb, bb):
    s = q.shape[0]
    c = min(GDN_CHUNK, s)
    nc = s // c

    def body(q_ref, k_ref, v_ref, g_ref, b_ref, o_ref, st_ref, inv_ref, s_sc):
        @pl.when(pl.program_id(0) == 0)
        def _():
            s_sc[...] = jnp.zeros_like(s_sc)

        s0s = tuple(s_sc[h] for h in range(GDN_HEADS))
        for h in range(GDN_HEADS):
            st_ref[h, 0] = s0s[h]
        (os, s1s), ts = _gdn_chunk_fn(*[_head_tiles(ref) for ref in (q_ref, k_ref, v_ref, g_ref, b_ref)], s0s)
        for h in range(GDN_HEADS):
            o_ref[:, h * 128:(h + 1) * 128] = os[h]
            inv_ref[h, 0] = ts[h]
            s_sc[h] = s1s[h]

    blk = pl.BlockSpec((c, GDN_WIDTH), lambda n: (n, 0))
    return pl.pallas_call(
        body, name="gdn_fwd", grid=(nc,), in_specs=[blk] * 5,
        out_specs=[blk, pl.BlockSpec((GDN_HEADS, 1, 128, 128), lambda n: (0, n, 0, 0)),
                   pl.BlockSpec((GDN_HEADS, 1, c, c), lambda n: (0, n, 0, 0))],
        out_shape=[jax.ShapeDtypeStruct((s, GDN_WIDTH), F32), jax.ShapeDtypeStruct((GDN_HEADS, nc, 128, 128), F32),
                   jax.ShapeDtypeStruct((GDN_HEADS, nc, c, c), F32)],
        scratch_shapes=[pltpu.VMEM((GDN_HEADS, 128, 128), F32)],
        compiler_params=_cparams(("arbitrary",)))(q, k, v, gb, bb)


def _gdn_bwd(q, k, v, gb, bb, states, inverses, do):
    s = q.shape[0]
    c = min(GDN_CHUNK, s)
    nc = s // c

    def body(q_ref, k_ref, v_ref, g_ref, b_ref, st_ref, inv_ref, do_ref, dq_ref, dk_ref, dv_ref, dg_ref, db_ref, ds_sc):
        @pl.when(pl.program_id(0) == 0)
        def _():
            ds_sc[...] = jnp.zeros_like(ds_sc)

        s0s = tuple(st_ref[h, 0] for h in range(GDN_HEADS))
        ts = tuple(inv_ref[h, 0] for h in range(GDN_HEADS))
        chunk = lambda *args: _gdn_chunk_fn(*args, known_ts=ts)[0]
        _, vjp = jax.vjp(chunk, *[_head_tiles(ref) for ref in (q_ref, k_ref, v_ref, g_ref, b_ref)], s0s)
        *d_tiles, ds0s = vjp((_head_tiles(do_ref), tuple(ds_sc[h] for h in range(GDN_HEADS))))
        for h in range(GDN_HEADS):
            for ref, d in zip((dq_ref, dk_ref, dv_ref, dg_ref, db_ref), d_tiles):
                ref[:, h * 128:(h + 1) * 128] = d[h]
            ds_sc[h] = ds0s[h]

    blk = pl.BlockSpec((c, GDN_WIDTH), lambda n: (nc - 1 - n, 0))
    return pl.pallas_call(
        body, name="gdn_bwd", grid=(nc,),
        in_specs=[blk] * 5 + [pl.BlockSpec((GDN_HEADS, 1, 128, 128), lambda n: (0, nc - 1 - n, 0, 0)),
                              pl.BlockSpec((GDN_HEADS, 1, c, c), lambda n: (0, nc - 1 - n, 0, 0)), blk],
        out_specs=[blk] * 5, out_shape=[jax.ShapeDtypeStruct((s, GDN_WIDTH), F32)] * 5,
        scratch_shapes=[pltpu.VMEM((GDN_HEADS, 128, 128), F32)],
        compiler_params=_cparams(("arbitrary",)))(q, k, v, gb, bb, states, inverses, do)


@jax.custom_vjp
def gdn_core(q, k, v, gb, bb):
    return _gdn_fwd(q, k, v, gb, bb)[0]


def _gdn_core_fwd(q, k, v, gb, bb):
    o, states, inverses = _gdn_fwd(q, k, v, gb, bb)
    return o, (q, k, v, gb, bb, states, inverses)


def _gdn_core_bwd(res, do):
    return tuple(_gdn_bwd(*res, do))


gdn_core.defvjp(_gdn_core_fwd, _gdn_core_bwd)


def _cact_fn(c):
    return (_silu(c),)


def _prenorm_fn(x, w, scale_raw, scale_b, shift_raw, shift_b):
    return (_rms(x, w) * (1.0 + scale_raw + scale_b) + shift_raw + shift_b,)


def _rmsnorm_fn(x, w):
    return (_rms(x, w),)


def _gate_fn(o, z):
    return (o * _silu(z),)


def _gdn_out_fn(o, z, w):
    parts = [_rms(o[:, h * 128:(h + 1) * 128], w) for h in range(GDN_HEADS)]
    return (jnp.concatenate(parts, axis=1) * _silu(z),)


def _post_fn(x, y, w, gate_raw, gate_b):
    return (x + (gate_raw + gate_b) * _rms(y, w),)


def _loss_fn(y, tgt):
    err = y - tgt
    part = jnp.sum(0.5 * jnp.mean(err * err, axis=-1, keepdims=True), axis=0, keepdims=True)
    return (jnp.broadcast_to(part, (1, LANES)),)


_OPS = dict(
    cact=make_rowwise(_cact_fn, "c_act", (D_MODEL,), tile=16),
    prenorm=make_rowwise(_prenorm_fn, "prenorm", (D_MODEL,)),
    qnorm=make_rowwise(_rmsnorm_fn, "q_norm", (MLA_Q_RANK,)),
    kvnorm=make_rowwise(_rmsnorm_fn, "kv_norm", (MLA_KV_RANK,)),
    gate=make_rowwise(_gate_fn, "mla_gate", (MLA_WIDTH,)),
    gdn_out=make_rowwise(_gdn_out_fn, "gdn_out", (GDN_WIDTH,)),
    post=make_rowwise(_post_fn, "postnorm", (D_MODEL,)),
    loss=make_rowwise(_loss_fn, "loss", (), acc_dims=(LANES,), n_nondiff=1),
    lin_mod=make_linear("lin_mod"), lin_in=make_fan_linear("lin_in"), lin_q=make_linear("lin_q"),
    lin_kv=make_linear("lin_kv"), lin_out=make_fan_linear("lin_out"),
)


def _swap_halves(w):
    half = w.shape[-1] // 2
    return jnp.concatenate([w[..., half:], w[..., :half]], axis=-1)


def _w_in_groups(w):
    k_pe = w[:, 640:704]
    ab = jnp.concatenate([w[:, 2752:2760], jnp.zeros((w.shape[0], LANES - 8), w.dtype)], axis=1)
    return (w[:, :384], w[:, 384:640], jnp.concatenate([k_pe, _swap_halves(k_pe)], axis=1), w[:, 704:1216],
            w[:, 1216:2752], ab, w[:, 2760:])


def _q_up_ext(w):
    parts = []
    for h in range(MLA_HEADS):
        rope = w[:, h * 192 + 128:(h + 1) * 192]
        parts += [w[:, h * 192:h * 192 + 128], rope, _swap_halves(rope)]
    return jnp.concatenate(parts, axis=1)


def _kv_up_perm(w):
    ks = [w[:, h * 256:h * 256 + 128] for h in range(MLA_HEADS)]
    vs = [w[:, h * 256 + 128:(h + 1) * 256] for h in range(MLA_HEADS)]
    return jnp.concatenate(ks + vs, axis=1)


def _pad_lanes(v):
    return jnp.pad(v, (0, LANES - v.shape[0]))[None, :]


def _local_loss(weights, x, c, positions, target):
    s = x.shape[0]
    half = MLA_ROPE // 2
    inv_freq = jnp.power(ROPE_THETA, -jnp.arange(half, dtype=F32) * 2.0 / MLA_ROPE)
    ang = positions.astype(F32)[:, None] * inv_freq
    cos, sin, zero = jnp.cos(ang), jnp.sin(ang), jnp.zeros((s, 2 * half), F32)
    t1 = jnp.concatenate([cos, cos, zero], axis=1)
    t2 = jnp.concatenate([-sin, sin, zero], axis=1)

    (c_act,) = _OPS["cact"]((jnp.pad(c, ((0, 15), (0, 0))),), ())
    for l in range(DEPTH):
        mod = _OPS["lin_mod"](c_act, weights["w_mod"][l])[0:1]
        b = weights["b_mod"][l][None, :]
        shift_raw, scale_raw, gate_raw = mod[:, :1024], mod[:, 1024:2048], mod[:, 2048:]
        shift_b, scale_b, gate_b = b[:, :1024], b[:, 1024:2048], b[:, 2048:]
        (h,) = _OPS["prenorm"]((x,), (weights["pre_norm_w"][l][None], scale_raw, scale_b, shift_raw, shift_b))
        q_lat, kv_lat, kr, z_mla, qkv, ab, z_gdn = _OPS["lin_in"]((h,), (_w_in_groups(weights["w_in"][l]),))
        (qn,) = _OPS["qnorm"]((q_lat,), (weights["mla_q_norm_w"][l][None],))
        qraw = _OPS["lin_q"](qn, _q_up_ext(weights["mla_q_up"][l]))
        (kvn,) = _OPS["kvnorm"]((kv_lat,), (weights["mla_kv_norm_w"][l][None],))
        kvraw = _OPS["lin_kv"](kvn, _kv_up_perm(weights["mla_kv_up"][l]))
        o_mla = mla_attention(qraw, kvraw, kr, t1, t2)
        (y_mla,) = _OPS["gate"]((o_mla, z_mla), ())
        cw = weights["gdn_conv_w"][l]
        params = tuple(cw[j][None] for j in range(GDN_CONV))
        params += (_pad_lanes(weights["gdn_a_log"][l]), _pad_lanes(weights["gdn_dt_bias"][l]))
        qg, kg, vg, gb, bb = gdn_prep(qkv, ab, params)
        o_gdn = gdn_core(qg, kg, vg, gb, bb)
        (y_gdn,) = _OPS["gdn_out"]((o_gdn, z_gdn), (weights["gdn_o_norm_w"][l][None],))
        w_o = weights["w_out"][l]
        (y,) = _OPS["lin_out"]((y_mla, y_gdn), ((w_o[:MLA_WIDTH],), (w_o[MLA_WIDTH:],)))
        (x,) = _OPS["post"]((x, y), (weights["post_norm_w"][l][None], gate_raw, gate_b))
    (acc,) = _OPS["loss"]((x, target), ())
    return acc[0, 0]


def _chip_index():
    return 2 * lax.axis_index("x") + lax.axis_index("y")


def _other_chips(x, y):
    return [(1 - x, y), (x, 1 - y), (1 - x, 1 - y)]


def _any_spec():
    return pl.BlockSpec(memory_space=pl.ANY)


def _half(ref, hc):
    n = ref.shape[0] // 2
    return ref.at[pl.ds(hc * n, n)]


def ag_weights(shards):
    n = len(shards)

    def body(*refs):
        ins, outs = refs[:n], refs[n:2 * n]
        send_sems, recv_sems = refs[2 * n:]
        x, y, c = lax.axis_index("x"), lax.axis_index("y"), lax.axis_index("c")
        sibling = (x, y, 1 - c)
        chips = _other_chips(x, y)

        def copy(t, k, src, chip_xy, hc, to):
            return pltpu.make_async_remote_copy(
                src_ref=src, dst_ref=_half(outs[t].at[2 * chip_xy[0] + chip_xy[1]], hc),
                send_sem=send_sems.at[6 * t + k], recv_sem=recv_sems.at[6 * t + k], device_id=to, device_id_type=MESH)

        first = [copy(t, k, _half(ins[t], c), (x, y), c, (*chip, c)) for k, chip in enumerate(chips) for t in range(n)]
        for cp in first:
            cp.start()
        passed = []
        for k, chip in enumerate(chips):
            for t in range(n):
                landed = _half(outs[t].at[2 * chip[0] + chip[1]], c)
                copy(t, k, landed, chip, c, (x, y, c)).wait_recv()
                passed.append(copy(t, 3 + k, landed, chip, c, sibling))
                passed[-1].start()
        for k, chip in enumerate(chips):
            for t in range(n):
                copy(t, 3 + k, _half(ins[t], c), chip, 1 - c, (x, y, c)).wait_recv()
        for cp in first + passed:
            cp.wait_send()

    return pl.pallas_call(
        body, name="ag_weights", in_specs=[_any_spec()] * n, out_specs=[_any_spec()] * n,
        out_shape=[jax.ShapeDtypeStruct((N_CHIPS,) + a.shape, a.dtype) for a in shards],
        scratch_shapes=[pltpu.SemaphoreType.DMA((6 * n,)), pltpu.SemaphoreType.DMA((6 * n,))],
        compiler_params=pltpu.CompilerParams(has_side_effects=True))(*shards)


def rs_pair(gs):
    n = len(gs)

    def body(*refs):
        ins, outs = refs[:n], refs[n:2 * n]
        send_sems, recv_sems = refs[2 * n:]
        x, y, c = lax.axis_index("x"), lax.axis_index("y"), lax.axis_index("c")
        lh = [g.shape[1] // 2 for g in gs]
        copies = [pltpu.make_async_remote_copy(
            src_ref=ins[t].at[:, pl.ds((1 - c) * lh[t], lh[t])], dst_ref=outs[t], send_sem=send_sems.at[t],
            recv_sem=recv_sems.at[t], device_id=(x, y, 1 - c), device_id_type=MESH) for t in range(n)]
        for cp in copies:
            cp.start()
        for cp in copies:
            cp.wait()

    return pl.pallas_call(
        body, name="rs_pair", in_specs=[_any_spec()] * n, out_specs=[_any_spec()] * n,
        out_shape=[jax.ShapeDtypeStruct((N_CHIPS, g.shape[1] // 2) + g.shape[2:], F32) for g in gs],
        scratch_shapes=[pltpu.SemaphoreType.DMA((n,)), pltpu.SemaphoreType.DMA((n,))],
        compiler_params=pltpu.CompilerParams(has_side_effects=True))(*gs)


def rs_cross(pairs):
    n = len(pairs)

    def body(*refs):
        ins, outs = refs[:n], refs[n:2 * n]
        send_sems, recv_sems = refs[2 * n:]
        x, y, c = lax.axis_index("x"), lax.axis_index("y"), lax.axis_index("c")
        copies = []
        for k, chip in enumerate(_other_chips(x, y)):
            for t in range(n):
                copies.append(pltpu.make_async_remote_copy(
                    src_ref=ins[t].at[2 * chip[0] + chip[1]], dst_ref=outs[t].at[k], send_sem=send_sems.at[3 * t + k],
                    recv_sem=recv_sems.at[3 * t + k], device_id=(*chip, c), device_id_type=MESH))
        for cp in copies:
            cp.start()
        for cp in copies:
            cp.wait()

    return pl.pallas_call(
        body, name="rs_cross", in_specs=[_any_spec()] * n, out_specs=[_any_spec()] * n,
        out_shape=[jax.ShapeDtypeStruct((3,) + p.shape[1:], p.dtype) for p in pairs],
        scratch_shapes=[pltpu.SemaphoreType.DMA((3 * n,)), pltpu.SemaphoreType.DMA((3 * n,))],
        compiler_params=pltpu.CompilerParams(has_side_effects=True))(*pairs)


def rs_share(blocks):
    n = len(blocks)

    def body(*refs):
        ins, outs = refs[:n], refs[n:2 * n]
        send_sems, recv_sems = refs[2 * n:]
        x, y, c = lax.axis_index("x"), lax.axis_index("y"), lax.axis_index("c")
        sends = [pltpu.make_async_remote_copy(
            src_ref=_half(ins[t], c), dst_ref=_half(outs[t], c), send_sem=send_sems.at[t], recv_sem=recv_sems.at[t],
            device_id=(x, y, 1 - c), device_id_type=MESH) for t in range(n)]
        for cp in sends:
            cp.start()
        for t in range(n):
            pltpu.make_async_remote_copy(
                src_ref=_half(ins[t], c), dst_ref=_half(outs[t], 1 - c), send_sem=send_sems.at[t],
                recv_sem=recv_sems.at[t], device_id=(x, y, 1 - c), device_id_type=MESH).wait_recv()
        for cp in sends:
            cp.wait_send()

    return pl.pallas_call(
        body, name="rs_share", in_specs=[_any_spec()] * n, out_specs=[_any_spec()] * n,
        out_shape=[jax.ShapeDtypeStruct(b.shape, F32) for b in blocks],
        input_output_aliases={t: t for t in range(n)},
        scratch_shapes=[pltpu.SemaphoreType.DMA((n,)), pltpu.SemaphoreType.DMA((n,))],
        compiler_params=pltpu.CompilerParams(has_side_effects=True))(*blocks)


def _row_tile(rows, target):
    best = None
    for t in range(8, min(rows, target) + 1, 8):
        if rows % t == 0:
            best = t
    return rows if best is None else best


TILE_BYTES = 2 * 1024 * 1024


def _flat_rows(shape):
    rows = int(np.prod(shape[1:-1]))
    cols_padded = -(-shape[-1] // LANES) * LANES
    return rows, _row_tile(rows, max(8, TILE_BYTES // (4 * cols_padded)))


def pair_add(g, from_sibling, out_dtype):
    cols = g.shape[-1]
    rph, t = _flat_rows(from_sibling.shape)
    nt = rph // t
    c_arr = lax.axis_index("c").astype(jnp.int32).reshape(1)

    def body(c_ref, a_ref, b_ref, o_ref):
        o_ref[...] = (a_ref[...] + b_ref[...]).astype(o_ref.dtype)

    out = pl.pallas_call(
        body, name="pair_add",
        grid_spec=pltpu.PrefetchScalarGridSpec(
            num_scalar_prefetch=1, grid=(N_CHIPS, nt),
            in_specs=[pl.BlockSpec((t, cols), lambda j, i, c_ref: (j * 2 * nt + c_ref[0] * nt + i, 0)),
                      pl.BlockSpec((t, cols), lambda j, i, c_ref: (j * nt + i, 0))],
            out_specs=pl.BlockSpec((t, cols), lambda j, i, c_ref: (j * nt + i, 0))),
        out_shape=jax.ShapeDtypeStruct((N_CHIPS * rph, cols), out_dtype),
        compiler_params=_cparams(("parallel", "parallel")))(c_arr, g.reshape(-1, cols), from_sibling.reshape(-1, cols))
    return out.reshape(from_sibling.shape)


def chip_add(pairs, received):
    cols = pairs.shape[-1]
    rph, t = _flat_rows(pairs.shape)
    nt = rph // t
    j_arr = _chip_index().astype(jnp.int32).reshape(1)
    c_arr = lax.axis_index("c").astype(jnp.int32).reshape(1)
    r2 = received.reshape(-1, cols)

    def body(j_ref, c_ref, a_ref, r0_ref, r1_ref, r2_ref, o_ref):
        a, r0, r1, r2 = [ref[...].astype(F32) for ref in (a_ref, r0_ref, r1_ref, r2_ref)]
        o_ref[...] = (a + r0) + (r1 + r2)

    out = pl.pallas_call(
        body, name="chip_add",
        grid_spec=pltpu.PrefetchScalarGridSpec(
            num_scalar_prefetch=2, grid=(nt,),
            in_specs=[pl.BlockSpec((t, cols), lambda i, j_ref, c_ref: (j_ref[0] * nt + i, 0)),
                      pl.BlockSpec((t, cols), lambda i, j_ref, c_ref: (i, 0)),
                      pl.BlockSpec((t, cols), lambda i, j_ref, c_ref: (nt + i, 0)),
                      pl.BlockSpec((t, cols), lambda i, j_ref, c_ref: (2 * nt + i, 0))],
            out_specs=pl.BlockSpec((t, cols), lambda i, j_ref, c_ref: (c_ref[0] * nt + i, 0))),
        out_shape=jax.ShapeDtypeStruct((2 * rph, cols), F32),
        compiler_params=_cparams(("parallel",)))(j_arr, c_arr, pairs.reshape(-1, cols), r2, r2, r2)
    return out.reshape((2 * pairs.shape[1],) + pairs.shape[2:])


def reduce_grads(gs, cross_dtypes):
    pairs = [pair_add(g, r, dt) for g, r, dt in zip(gs, rs_pair(gs), cross_dtypes)]
    return rs_share([chip_add(p, r) for p, r in zip(pairs, rs_cross(pairs))])


def adamw(w, g, m, v):
    shape = w.shape
    cols = shape[-1]
    rows = int(np.prod(shape[:-1]))
    flat = lambda a: a.reshape(rows, cols)
    t = _row_tile(rows, 256)

    def body(w_ref, g_ref, m_ref, v_ref, d_ref, mo_ref, vo_ref):
        gv = g_ref[...]
        m_new = ADAM_B1 * m_ref[...] + (1.0 - ADAM_B1) * gv
        v_new = ADAM_B2 * v_ref[...] + (1.0 - ADAM_B2) * (gv * gv)
        m_hat = m_new / (1.0 - ADAM_B1 ** ADAM_STEP)
        v_hat = v_new / (1.0 - ADAM_B2 ** ADAM_STEP)
        d_ref[...] = -ADAM_LR * (m_hat / (jnp.sqrt(v_hat) + ADAM_EPS) + ADAM_WD * w_ref[...])
        mo_ref[...] = m_new
        vo_ref[...] = v_new

    spec = pl.BlockSpec((t, cols), lambda i: (i, 0))
    outs = pl.pallas_call(
        body, name="adamw", grid=(rows // t,), in_specs=[spec] * 4, out_specs=[spec] * 3,
        out_shape=[jax.ShapeDtypeStruct((rows, cols), F32)] * 3,
        compiler_params=_cparams(("parallel",)))(flat(w), flat(g), flat(m), flat(v))
    return tuple(o.reshape(shape) for o in outs)


SHARDED = (("w_mod", 2), ("w_in", 2), ("mla_q_up", 2), ("mla_kv_up", 2), ("gdn_conv_w", 2), ("w_out", 1))
REPLICATED = ("b_mod", "pre_norm_w", "post_norm_w", "mla_q_norm_w", "mla_kv_norm_w", "gdn_a_log", "gdn_dt_bias",
              "gdn_o_norm_w")
WEIGHT_ORDER = ("w_mod", "b_mod", "pre_norm_w", "post_norm_w", "w_in", "mla_q_norm_w", "mla_q_up", "mla_kv_norm_w",
                "mla_kv_up", "gdn_conv_w", "gdn_a_log", "gdn_dt_bias", "gdn_o_norm_w", "w_out")
EXACT_F32 = ("gdn_conv_w",)
SMALL_ROWS = 48


def _gather_weights(shards):
    names = [name for name, _ in SHARDED]
    own = [shards[n] if n in EXACT_F32 else shards[n].astype(BF16) for n in names]
    gathered = ag_weights(own)
    full = {}
    for (name, axis), blk, mine in zip(SHARDED, gathered, own):
        shp = shards[name].shape
        blk = lax.dynamic_update_index_in_dim(blk, mine, _chip_index(), 0)
        blk = jnp.moveaxis(blk.astype(F32), 0, axis)
        full[name] = blk.reshape(shp[:axis] + (N_CHIPS * shp[axis],) + shp[axis + 1:])
    return full


def _split_grads(grads):
    pieces = []
    for name, axis in SHARDED:
        g = grads[name]
        shp = g.shape
        g = g.reshape(shp[:axis] + (N_CHIPS, shp[axis] // N_CHIPS) + shp[axis + 1:])
        pieces.append(jnp.moveaxis(g, axis, 0))
    small = jnp.concatenate([grads[name] for name in REPLICATED], axis=1)
    small = jnp.pad(small, ((0, 0), (0, SMALL_ROWS * LANES - small.shape[1]))).reshape(DEPTH, SMALL_ROWS, LANES)
    pieces.append(jnp.broadcast_to(small[None], (N_CHIPS,) + small.shape))
    return pieces


def _unsplit_small(small, rep_shapes):
    flat = small.reshape(DEPTH, SMALL_ROWS * LANES)
    out, off = {}, 0
    for name in REPLICATED:
        size = rep_shapes[name][1]
        out[name] = flat[:, off:off + size]
        off += size
    return out


def kernel(x, c, positions, w_mod, b_mod, pre_norm_w, post_norm_w, w_in, mla_q_norm_w, mla_q_up, mla_kv_norm_w, mla_kv_up, gdn_conv_w, gdn_a_log, gdn_dt_bias, gdn_o_norm_w, w_out, loss_target, m_w_mod, m_b_mod, m_pre_norm_w, m_post_norm_w, m_w_in, m_mla_q_norm_w, m_mla_q_up, m_mla_kv_norm_w, m_mla_kv_up, m_gdn_conv_w, m_gdn_a_log, m_gdn_dt_bias, m_gdn_o_norm_w, m_w_out, v_w_mod, v_b_mod, v_pre_norm_w, v_post_norm_w, v_w_in, v_mla_q_norm_w, v_mla_q_up, v_mla_kv_norm_w, v_mla_kv_up, v_gdn_conv_w, v_gdn_a_log, v_gdn_dt_bias, v_gdn_o_norm_w, v_w_out):
    given = dict(w_mod=w_mod, b_mod=b_mod, pre_norm_w=pre_norm_w, post_norm_w=post_norm_w, w_in=w_in,
                 mla_q_norm_w=mla_q_norm_w, mla_q_up=mla_q_up, mla_kv_norm_w=mla_kv_norm_w, mla_kv_up=mla_kv_up,
                 gdn_conv_w=gdn_conv_w, gdn_a_log=gdn_a_log, gdn_dt_bias=gdn_dt_bias, gdn_o_norm_w=gdn_o_norm_w,
                 w_out=w_out)
    moments_m = dict(w_mod=m_w_mod, b_mod=m_b_mod, pre_norm_w=m_pre_norm_w, post_norm_w=m_post_norm_w, w_in=m_w_in,
                     mla_q_norm_w=m_mla_q_norm_w, mla_q_up=m_mla_q_up, mla_kv_norm_w=m_mla_kv_norm_w,
                     mla_kv_up=m_mla_kv_up, gdn_conv_w=m_gdn_conv_w, gdn_a_log=m_gdn_a_log,
                     gdn_dt_bias=m_gdn_dt_bias, gdn_o_norm_w=m_gdn_o_norm_w, w_out=m_w_out)
    moments_v = dict(w_mod=v_w_mod, b_mod=v_b_mod, pre_norm_w=v_pre_norm_w, post_norm_w=v_post_norm_w, w_in=v_w_in,
                     mla_q_norm_w=v_mla_q_norm_w, mla_q_up=v_mla_q_up, mla_kv_norm_w=v_mla_kv_norm_w,
                     mla_kv_up=v_mla_kv_up, gdn_conv_w=v_gdn_conv_w, gdn_a_log=v_gdn_a_log,
                     gdn_dt_bias=v_gdn_dt_bias, gdn_o_norm_w=v_gdn_o_norm_w, w_out=v_w_out)

    full = _gather_weights({name: given[name] for name, _ in SHARDED})
    for name in REPLICATED:
        full[name] = given[name]
    loss_local, (grads, grad_x) = jax.value_and_grad(_local_loss, argnums=(0, 1))(
        full, x[0], c, positions[0], loss_target[0])
    loss = lax.psum(loss_local, AXES)

    cross_dtypes = [F32 if name in EXACT_F32 else BF16 for name, _ in SHARDED] + [F32]
    reduced = reduce_grads(_split_grads(grads), cross_dtypes)
    grad_w = {name: g for (name, _), g in zip(SHARDED, reduced)}
    grad_w.update(_unsplit_small(reduced[-1], {name: given[name].shape for name in REPLICATED}))
    delta, new_m, new_v = {}, {}, {}
    for name in WEIGHT_ORDER:
        delta[name], new_m[name], new_v[name] = adamw(given[name], grad_w[name], moments_m[name], moments_v[name])
    return (loss, grad_x[None], *[grad_w[n] for n in WEIGHT_ORDER], *[delta[n] for n in WEIGHT_ORDER],
            *[new_m[n] for n in WEIGHT_ORDER], *[new_v[n] for n in WEIGHT_ORDER])
```

```python
import functools
import math

import numpy as np
import jax
import jax.numpy as jnp
from jax import lax
from jax.experimental import pallas as pl
from jax.experimental.pallas import tpu as pltpu

F32 = jnp.float32
BF16 = jnp.bfloat16
MESH = pl.DeviceIdType.MESH
AXES = ("x", "y", "c")

D_MODEL = 1024
DEPTH = 4
MLA_HEADS = 4
MLA_NOPE = 128
MLA_ROPE = 64
MLA_V = 128
MLA_Q_RANK = 384
MLA_KV_RANK = 256
MLA_WIDTH = 512
GDN_HEADS = 4
GDN_DK = 128
GDN_WIDTH = 512
GDN_QKV = 1536
GDN_CONV = 4
IN_COLS = 3272
ROPE_THETA = 10000.0
NORM_EPS = 1e-6
ADAM_LR, ADAM_B1, ADAM_B2, ADAM_EPS, ADAM_WD, ADAM_STEP = 0.001, 0.9, 0.999, 1e-08, 0.01, 10

LANES = 128
N_CHIPS = 4
GDN_CHUNK = 128
VMEM_LIMIT = 56 * 1024 * 1024


def _cparams(sem=None):
    if sem is None:
        return pltpu.CompilerParams(vmem_limit_bytes=VMEM_LIMIT)
    return pltpu.CompilerParams(dimension_semantics=sem, vmem_limit_bytes=VMEM_LIMIT)


def _pick(dim, target):
    if dim <= target:
        return dim
    best = None
    for t in range(LANES, target + 1, LANES):
        if dim % t == 0:
            best = t
    assert best is not None, (dim, target)
    return best


_DN = {"nn": (((1,), (0,)), ((), ())), "nt": (((1,), (1,)), ((), ())), "tn": (((0,), (0,)), ((), ()))}


def _dot_raw(a, b, mode, exact):
    if exact:
        return lax.dot_general(a, b, _DN[mode], precision=lax.Precision.HIGHEST, preferred_element_type=F32)
    return lax.dot_general(a.astype(BF16), b.astype(BF16), _DN[mode], preferred_element_type=F32)


def _dot_split(a, b):
    a_hi, b_hi = a.astype(BF16), b.astype(BF16)
    a_lo, b_lo = (a - a_hi.astype(F32)).astype(BF16), (b - b_hi.astype(F32)).astype(BF16)
    dot = lambda u, w: lax.dot_general(u, w, _DN["nn"], preferred_element_type=F32)
    return dot(a_hi, b_hi) + (dot(a_hi, b_lo) + dot(a_lo, b_hi))


@functools.partial(jax.custom_vjp, nondiff_argnums=(2, 3))
def bdot(a, b, mode="nn", exact=False):
    return _dot_raw(a, b, mode, exact)


def _bdot_fwd(a, b, mode, exact):
    return _dot_raw(a, b, mode, exact), (a, b)


def _bdot_bwd(mode, exact, res, g):
    a, b = res
    if mode == "nn":
        return bdot(g, b, "nt", exact), bdot(a, g, "tn", exact)
    if mode == "nt":
        return bdot(g, b, "nn", exact), bdot(g, a, "tn", exact)
    return bdot(b, g, "nt", exact), bdot(a, g, "nn", exact)


bdot.defvjp(_bdot_fwd, _bdot_bwd)


@jax.custom_vjp
def roll_half(x):
    return pltpu.roll(x, 64, 1)


roll_half.defvjp(lambda x: (pltpu.roll(x, 64, 1), None), lambda _, g: (pltpu.roll(g, 64, 1),))


def _sigmoid(x):
    return 1.0 / (1.0 + jnp.exp(-x))


def _silu(x):
    return x * _sigmoid(x)


def _softplus(x):
    return jnp.maximum(x, 0.0) + jnp.log(1.0 + jnp.exp(-jnp.abs(x)))


def _rms(x, w):
    return x * lax.rsqrt(jnp.mean(x * x, axis=-1, keepdims=True) + NORM_EPS) * w


def _rw_fwd(fn, name, rows, params, out_dims, out_dtypes, acc_dims, tile):
    s = rows[0].shape[0]
    t = min(tile, s)
    n = s // t
    nr, npar, no, na = len(rows), len(params), len(out_dims), len(acc_dims)

    def body(*refs):
        r, p = refs[:nr], refs[nr:nr + npar]
        o, a = refs[nr + npar:nr + npar + no], refs[nr + npar + no:]
        outs = fn(*[x[...] for x in r], *[x[...] for x in p])
        for ref, val in zip(o, outs[:no]):
            ref[...] = val.astype(ref.dtype)
        if na:
            @pl.when(pl.program_id(0) == 0)
            def _():
                for ref in a:
                    ref[...] = jnp.zeros_like(ref)
            for ref, val in zip(a, outs[no:]):
                ref[...] += val

    in_specs = [pl.BlockSpec((t, x.shape[1]), lambda i: (i, 0)) for x in rows]
    in_specs += [pl.BlockSpec(x.shape, lambda i: (0, 0)) for x in params]
    out_specs = [pl.BlockSpec((t, d), lambda i: (i, 0)) for d in out_dims]
    out_specs += [pl.BlockSpec((1, d), lambda i: (0, 0)) for d in acc_dims]
    out_shape = [jax.ShapeDtypeStruct((s, d), dt) for d, dt in zip(out_dims, out_dtypes)]
    out_shape += [jax.ShapeDtypeStruct((1, d), F32) for d in acc_dims]
    res = pl.pallas_call(body, name=name, grid=(n,), in_specs=in_specs, out_specs=out_specs, out_shape=out_shape,
                         compiler_params=_cparams(("arbitrary",)))(*rows, *params)
    return tuple(res)


def _rw_bwd(fn, name, rows, params, row_cts, acc_cts, n_diff, tile):
    s = rows[0].shape[0]
    t = min(tile, s)
    n = s // t
    nr, npar, no, na = len(rows), len(params), len(row_cts), len(acc_cts)

    def body(*refs):
        r, p = refs[:nr], refs[nr:nr + npar]
        g, ga = refs[nr + npar:nr + npar + no], refs[nr + npar + no:nr + npar + no + na]
        dr, dp = refs[nr + npar + no + na:nr + npar + no + na + n_diff], refs[nr + npar + no + na + n_diff:]
        _, vjp = jax.vjp(fn, *[x[...] for x in r], *[x[...] for x in p])
        cts = vjp(tuple([x[...] for x in g] + [x[...] for x in ga]))
        for ref, val in zip(dr, cts[:n_diff]):
            ref[...] = val
        if npar:
            @pl.when(pl.program_id(0) == 0)
            def _():
                for ref in dp:
                    ref[...] = jnp.zeros_like(ref)
            for ref, val in zip(dp, cts[nr:]):
                ref[...] += val

    in_specs = [pl.BlockSpec((t, x.shape[1]), lambda i: (i, 0)) for x in rows]
    in_specs += [pl.BlockSpec(x.shape, lambda i: (0, 0)) for x in params]
    in_specs += [pl.BlockSpec((t, x.shape[1]), lambda i: (i, 0)) for x in row_cts]
    in_specs += [pl.BlockSpec(x.shape, lambda i: (0, 0)) for x in acc_cts]
    out_specs = [pl.BlockSpec((t, x.shape[1]), lambda i: (i, 0)) for x in rows[:n_diff]]
    out_specs += [pl.BlockSpec(x.shape, lambda i: (0, 0)) for x in params]
    out_shape = [jax.ShapeDtypeStruct(x.shape, F32) for x in rows[:n_diff]]
    out_shape += [jax.ShapeDtypeStruct(x.shape, F32) for x in params]
    res = pl.pallas_call(body, name=name, grid=(n,), in_specs=in_specs, out_specs=out_specs, out_shape=out_shape,
                         compiler_params=_cparams(("arbitrary",)))(*rows, *params, *row_cts, *acc_cts)
    return tuple(res[:n_diff]), tuple(res[n_diff:])


def make_rowwise(fn, name, out_dims, acc_dims=(), n_nondiff=0, tile=256):
    out_dtypes = (F32,) * len(out_dims)

    @jax.custom_vjp
    def op(rows, params):
        return _rw_fwd(fn, name, rows, params, out_dims, out_dtypes, acc_dims, tile)

    def fwd(rows, params):
        return op(rows, params), (rows, params)

    def bwd(res, cts):
        rows, params = res
        n_diff = len(rows) - n_nondiff
        d_rows, d_params = _rw_bwd(fn, name + "_bwd", rows, params, cts[:len(out_dims)], cts[len(out_dims):],
                                   n_diff, tile)
        d_rows = d_rows + tuple(jnp.zeros_like(x) for x in rows[n_diff:])
        return d_rows, d_params

    op.defvjp(fwd, bwd)
    return op


def _mm(a, b, mode, name):
    if mode == "nn":
        (m, k), (_, n) = a.shape, b.shape
    elif mode == "nt":
        (m, k), (n, _) = a.shape, b.shape
    else:
        (k, m), (_, n) = a.shape, b.shape
    tm = _pick(m, 512)
    tn = _pick(n, 1152)
    tk = _pick(k, 1152) if mode != "tn" else _pick(k, 512)
    nk = k // tk

    def body(a_ref, b_ref, o_ref, acc_ref):
        kk = pl.program_id(2)

        @pl.when(kk == 0)
        def _():
            acc_ref[...] = jnp.zeros_like(acc_ref)

        acc_ref[...] += _dot_raw(a_ref[...], b_ref[...], mode, False)

        @pl.when(kk == nk - 1)
        def _():
            o_ref[...] = acc_ref[...]

    if mode == "nn":
        a_spec = pl.BlockSpec((tm, tk), lambda i, j, kk: (i, kk))
        b_spec = pl.BlockSpec((tk, tn), lambda i, j, kk: (kk, j))
    elif mode == "nt":
        a_spec = pl.BlockSpec((tm, tk), lambda i, j, kk: (i, kk))
        b_spec = pl.BlockSpec((tn, tk), lambda i, j, kk: (j, kk))
    else:
        a_spec = pl.BlockSpec((tk, tm), lambda i, j, kk: (kk, i))
        b_spec = pl.BlockSpec((tk, tn), lambda i, j, kk: (kk, j))
    return pl.pallas_call(
        body, name=name, grid=(m // tm, n // tn, nk), in_specs=[a_spec, b_spec],
        out_specs=pl.BlockSpec((tm, tn), lambda i, j, kk: (i, j)),
        out_shape=jax.ShapeDtypeStruct((m, n), F32), scratch_shapes=[pltpu.VMEM((tm, tn), F32)],
        compiler_params=_cparams(("parallel", "parallel", "arbitrary")))(a, b)


def make_linear(name):
    @jax.custom_vjp
    def op(a, w):
        return _mm(a, w.astype(BF16), "nn", name)

    def fwd(a, w):
        w16 = w.astype(BF16)
        return _mm(a, w16, "nn", name), (a, w16)

    def bwd(res, g):
        a, w16 = res
        return _mm(g, w16, "nt", name + "_dx"), _mm(a, g, "tn", name + "_dw")

    op.defvjp(fwd, bwd)
    return op


def _fan_mm(xs, ws, transposed, name, tile=256):
    n_in, n_out = len(ws), len(ws[0])
    s = xs[0].shape[0]
    t = min(tile, s)
    out_dims = [ws[i][0].shape[0] for i in range(n_in)] if transposed else [ws[0][j].shape[1] for j in range(n_out)]
    flat_ws = [w for row in ws for w in row]

    def body(*refs):
        x_refs, w_refs, o_refs = refs[:len(xs)], refs[len(xs):len(xs) + len(flat_ws)], refs[len(xs) + len(flat_ws):]
        xv = [r[...].astype(BF16) for r in x_refs]
        for o, o_ref in enumerate(o_refs):
            if transposed:
                terms = [_dot_raw(xv[j], w_refs[o * n_out + j][...], "nt", False) for j in range(n_out)]
            else:
                terms = [_dot_raw(xv[i], w_refs[i * n_out + o][...], "nn", False) for i in range(n_in)]
            o_ref[...] = functools.reduce(lambda a, b: a + b, terms)

    in_specs = [pl.BlockSpec((t, x.shape[1]), lambda i: (i, 0)) for x in xs]
    in_specs += [pl.BlockSpec(w.shape, lambda i: (0, 0)) for w in flat_ws]
    res = pl.pallas_call(
        body, name=name, grid=(s // t,), in_specs=in_specs,
        out_specs=[pl.BlockSpec((t, d), lambda i: (i, 0)) for d in out_dims],
        out_shape=[jax.ShapeDtypeStruct((s, d), F32) for d in out_dims],
        compiler_params=_cparams(("parallel",)))(*xs, *flat_ws)
    return tuple(res)


def _fan_dw(x, dys, name, tile=256):
    s, k = x.shape
    t = min(tile, s)

    def body(x_ref, *refs):
        dy_refs, o_refs = refs[:len(dys)], refs[len(dys):]

        @pl.when(pl.program_id(0) == 0)
        def _():
            for o_ref in o_refs:
                o_ref[...] = jnp.zeros_like(o_ref)

        xt = x_ref[...].astype(BF16)
        for dy_ref, o_ref in zip(dy_refs, o_refs):
            o_ref[...] += _dot_raw(xt, dy_ref[...], "tn", False)

    rows = lambda d: pl.BlockSpec((t, d), lambda i: (i, 0))
    res = pl.pallas_call(
        body, name=name, grid=(s // t,), in_specs=[rows(k)] + [rows(dy.shape[1]) for dy in dys],
        out_specs=[pl.BlockSpec((k, dy.shape[1]), lambda i: (0, 0)) for dy in dys],
        out_shape=[jax.ShapeDtypeStruct((k, dy.shape[1]), F32) for dy in dys],
        compiler_params=_cparams(("arbitrary",)))(x, *dys)
    return tuple(res)


def make_fan_linear(name):
    @jax.custom_vjp
    def op(xs, ws):
        return _fan_mm(xs, tuple(tuple(w.astype(BF16) for w in row) for row in ws), False, name)

    def fwd(xs, ws):
        ws16 = tuple(tuple(w.astype(BF16) for w in row) for row in ws)
        return _fan_mm(xs, ws16, False, name), (xs, ws16)

    def bwd(res, dys):
        xs, ws16 = res
        dxs = _fan_mm(dys, ws16, True, name + "_dx")
        dws = tuple(_fan_dw(x, dys, name + "_dw") for x in xs)
        return dxs, dws

    op.defvjp(fwd, bwd)
    return op


def _mla_prep_fn(qraw, kvraw, kr, t1, t2):
    kr_rot = kr * t1 + roll_half(kr) * t2
    qs, ks = [], []
    for h in range(MLA_HEADS):
        q_r = qraw[:, h * 256 + 128:(h + 1) * 256]
        qs += [qraw[:, h * 256:h * 256 + 128], q_r * t1 + roll_half(q_r) * t2]
        ks += [kvraw[:, h * 128:(h + 1) * 128], kr_rot]
    return jnp.concatenate(qs, axis=1), jnp.concatenate(ks, axis=1), kvraw[:, 512:]


def _flash_tile(s):
    return 512 if s >= 2048 else 128


FLASH_SCALE = (MLA_NOPE + MLA_ROPE) ** -0.5
LOG2_E = 1.4426950408889634
FLASH_EXP2 = FLASH_SCALE * LOG2_E
STAT_ROWS = 8
FLASH_PAIR = 2


def _as_rows(col_b):
    ones = jnp.full((STAT_ROWS, LANES), 1.0 / LANES, F32)
    return _dot_raw(ones, col_b, "nt", True)


def _flash_fwd(qf, kf, vf):
    s = qf.shape[0]
    t = _flash_tile(s)
    nb = s // t
    pair = range(FLASH_PAIR)

    def body(q_ref, k_ref, v_ref, o_ref, lse_ref, m_sc, l_sc, acc_sc):
        i = pl.program_id(1)
        m_sc[...] = jnp.full_like(m_sc, -1e30)
        l_sc[...] = jnp.zeros_like(l_sc)
        acc_sc[...] = jnp.zeros_like(acc_sc)
        qs = [q_ref[:, hh * 256:(hh + 1) * 256] for hh in pair]

        def step(j, on_diagonal):
            rows = pl.ds(pl.multiple_of(j * t, t), t)
            sts = [_dot_raw(k_ref[rows, hh * 256:(hh + 1) * 256], qs[hh], "nt", False) for hh in pair]
            if on_diagonal:
                keep = lax.broadcasted_iota(jnp.int32, (t, t), 0) <= lax.broadcasted_iota(jnp.int32, (t, t), 1)
                sts = [jnp.where(keep, st, -1e30) for st in sts]
            m_olds = [m_sc[hh] for hh in pair]
            m_news = [jnp.maximum(m_olds[hh], jnp.max(sts[hh], axis=0, keepdims=True)) for hh in pair]
            alphas = [jnp.exp2((m_olds[hh] - m_news[hh]) * FLASH_EXP2) for hh in pair]
            pts = [jnp.exp2(sts[hh] * FLASH_EXP2 - m_news[hh] * FLASH_EXP2) for hh in pair]
            pvs = [_dot_raw(v_ref[rows, hh * 128:(hh + 1) * 128], pts[hh], "tn", False) for hh in pair]
            for hh in pair:
                l_sc[hh] = alphas[hh] * l_sc[hh] + jnp.sum(pts[hh], axis=0, keepdims=True)
                acc_sc[hh] = alphas[hh] * acc_sc[hh] + pvs[hh]
                m_sc[hh] = m_news[hh]

        def two_steps(p, carry):
            step(2 * p, False)
            step(2 * p + 1, False)
            return carry

        lax.fori_loop(0, i // 2, two_steps, 0)

        @pl.when(i % 2 == 1)
        def _():
            step(i - 1, False)

        step(i, True)
        for hh in pair:
            o_ref[:, hh * 128:(hh + 1) * 128] = (acc_sc[hh] / l_sc[hh]).T
            lse2 = m_sc[hh] * FLASH_EXP2 + jnp.log(l_sc[hh]) * LOG2_E
            lse_ref[hh] = jnp.broadcast_to(lse2, (STAT_ROWS, t))

    p = FLASH_PAIR
    return pl.pallas_call(
        body, name="flash_fwd", grid=(MLA_HEADS // p, nb),
        in_specs=[pl.BlockSpec((t, p * 256), lambda h, i: (i, h)), pl.BlockSpec((s, p * 256), lambda h, i: (0, h)),
                  pl.BlockSpec((s, p * 128), lambda h, i: (0, h))],
        out_specs=[pl.BlockSpec((t, p * 128), lambda h, i: (i, h)),
                   pl.BlockSpec((p, STAT_ROWS, t), lambda h, i: (h, 0, i))],
        out_shape=[jax.ShapeDtypeStruct((s, MLA_WIDTH), F32),
                   jax.ShapeDtypeStruct((MLA_HEADS, STAT_ROWS, s), F32)],
        scratch_shapes=[pltpu.VMEM((p, 1, t), F32), pltpu.VMEM((p, 1, t), F32), pltpu.VMEM((p, 128, t), F32)],
        compiler_params=_cparams(("parallel", "arbitrary")))(qf, kf, vf)


def _flash_bwd_prep(o, do):
    s = o.shape[0]
    t = _flash_tile(s)

    def body(o_ref, do_ref, dl_ref, do16_ref):
        do_t = do_ref[...]
        delta = jnp.sum(do_t * o_ref[...], axis=1, keepdims=True)
        dl_ref[0] = _as_rows(jnp.broadcast_to(delta, (t, LANES)))
        do16_ref[...] = do_t.astype(BF16)

    blk = pl.BlockSpec((t, 128), lambda h, i: (i, h))
    return pl.pallas_call(
        body, name="flash_bwd_prep", grid=(MLA_HEADS, s // t), in_specs=[blk, blk],
        out_specs=[pl.BlockSpec((1, STAT_ROWS, t), lambda h, i: (h, 0, i)), blk],
        out_shape=[jax.ShapeDtypeStruct((MLA_HEADS, STAT_ROWS, s), F32), jax.ShapeDtypeStruct((s, MLA_WIDTH), BF16)],
        compiler_params=_cparams(("parallel", "parallel")))(o, do)


def _flash_bwd(qf, kf, vf, lse, delta, do16):
    s = qf.shape[0]
    t = _flash_tile(s)
    nb = s // t

    def body(q_ref, k_ref, v_ref, lse_ref, dl_ref, do_ref, dq_ref, dk_ref, dv_ref):
        j = pl.program_id(1)

        @pl.when(j == 0)
        def _():
            dq_ref[...] = jnp.zeros_like(dq_ref)

        dk_ref[...] = jnp.zeros_like(dk_ref)
        dv_ref[...] = jnp.zeros_like(dv_ref)
        k = k_ref[...]
        v = v_ref[...]

        def step(i, on_diagonal):
            rows = pl.ds(pl.multiple_of(i * t, t), t)
            q = q_ref[rows, :]
            do_t = do_ref[rows, :]
            st = _dot_raw(k, q, "nt", False) * FLASH_EXP2 - lse_ref[0, 0:1, rows]
            if on_diagonal:
                keep = lax.broadcasted_iota(jnp.int32, (t, t), 0) <= lax.broadcasted_iota(jnp.int32, (t, t), 1)
                st = jnp.where(keep, st, -1e30)
            pt = jnp.exp2(st)
            dst = pt * (_dot_raw(v, do_t, "nt", False) - dl_ref[0, 0:1, rows])
            dv_ref[...] += _dot_raw(pt, do_t, "nn", False)
            dk_ref[...] += _dot_raw(dst, q, "nn", False)
            dq_ref[rows, :] += _dot_raw(dst, k, "tn", False)

        def two_steps(p, carry):
            step(j + 1 + 2 * p, False)
            step(j + 2 + 2 * p, False)
            return carry

        step(j, True)
        below = nb - 1 - j
        lax.fori_loop(0, below // 2, two_steps, 0)

        @pl.when(below % 2 == 1)
        def _():
            step(nb - 1, False)

        dk_ref[...] *= FLASH_SCALE

        @pl.when(j == nb - 1)
        def _():
            dq_ref[...] *= FLASH_SCALE

    stat = pl.BlockSpec((1, STAT_ROWS, s), lambda h, j: (h, 0, 0))
    return pl.pallas_call(
        body, name="flash_bwd", grid=(MLA_HEADS, nb),
        in_specs=[pl.BlockSpec((s, 256), lambda h, j: (0, h)), pl.BlockSpec((t, 256), lambda h, j: (j, h)),
                  pl.BlockSpec((t, 128), lambda h, j: (j, h)), stat, stat, pl.BlockSpec((s, 128), lambda h, j: (0, h))],
        out_specs=[pl.BlockSpec((s, 256), lambda h, j: (0, h)), pl.BlockSpec((t, 256), lambda h, j: (j, h)),
                   pl.BlockSpec((t, 128), lambda h, j: (j, h))],
        out_shape=[jax.ShapeDtypeStruct((s, 1024), F32), jax.ShapeDtypeStruct((s, 1024), F32),
                   jax.ShapeDtypeStruct((s, MLA_WIDTH), F32)],
        compiler_params=_cparams(("parallel", "arbitrary")))(qf, kf, vf, lse, delta, do16)


def _mla_prep(qraw, kvraw, kr, t1, t2):
    return _rw_fwd(_mla_prep_fn, "mla_prep", (qraw, kvraw, kr, t1, t2), (), (1024, 1024, 512), (BF16, BF16, BF16), (),
                   256)


@jax.custom_vjp
def mla_attention(qraw, kvraw, kr, t1, t2):
    return _flash_fwd(*_mla_prep(qraw, kvraw, kr, t1, t2))[0]


def _mla_attention_fwd(qraw, kvraw, kr, t1, t2):
    qf, kf, vf = _mla_prep(qraw, kvraw, kr, t1, t2)
    o, lse = _flash_fwd(qf, kf, vf)
    return o, (qraw, kvraw, kr, t1, t2, qf, kf, vf, o, lse)


def _mla_attention_bwd(res, do):
    qraw, kvraw, kr, t1, t2, qf, kf, vf, o, lse = res
    delta, do16 = _flash_bwd_prep(o, do)
    dq, dk, dv = _flash_bwd(qf, kf, vf, lse, delta, do16)
    (dqraw, dkvraw, dkr), _ = _rw_bwd(_mla_prep_fn, "mla_prep_bwd", (qraw, kvraw, kr, t1, t2), (), (dq, dk, dv), (),
                                      3, 256)
    return dqraw, dkvraw, dkr, jnp.zeros_like(t1), jnp.zeros_like(t2)


mla_attention.defvjp(_mla_attention_fwd, _mla_attention_bwd)


def _lane_pick(x, lane):
    ids = lax.broadcasted_iota(jnp.int32, x.shape, 1)
    col = jnp.sum(jnp.where(ids == lane, x, 0.0), axis=1, keepdims=True)
    return jnp.broadcast_to(col, x.shape)


GDN_HALO = 8


@functools.partial(jax.custom_vjp, nondiff_argnums=(1,))
def _roll_rows(x, d):
    return pltpu.roll(x, d, 0)


_roll_rows.defvjp(lambda x, d: (pltpu.roll(x, d, 0), None), lambda d, _, g: (pltpu.roll(g, g.shape[0] - d, 0),))


def _gdn_prep_fn(prev, cur, ab, w0, w1, w2, w3, a_log, dt_bias):
    xcat = jnp.concatenate([prev, cur], axis=0)
    x0, x1, x2 = [_roll_rows(xcat, GDN_CONV - 1 - j)[GDN_HALO:] for j in range(GDN_CONV - 1)]
    qkv = _silu(x0 * w0 + x1 * w1 + x2 * w2 + cur * w3)
    g_all = -jnp.exp(a_log) * _softplus(ab + dt_bias)
    beta_all = _sigmoid(ab)
    qs, ks, gs, bs = [], [], [], []
    for h in range(GDN_HEADS):
        q = qkv[:, h * 128:(h + 1) * 128]
        k = qkv[:, 512 + h * 128:512 + (h + 1) * 128]
        qs.append(q * lax.rsqrt(jnp.sum(q * q, axis=-1, keepdims=True) + NORM_EPS) * (GDN_DK ** -0.5))
        ks.append(k * lax.rsqrt(jnp.sum(k * k, axis=-1, keepdims=True) + NORM_EPS))
        gs.append(_lane_pick(g_all, h))
        bs.append(_lane_pick(beta_all, GDN_HEADS + h))
    cat = lambda xs: jnp.concatenate(xs, axis=1)
    return cat(qs), cat(ks), qkv[:, 1024:], cat(gs), cat(bs)


GDN_PREP_TILE = 256


def _gdn_prep_specs(s, params, reverse=False):
    t = min(GDN_PREP_TILE, s)
    n = s // t
    blk = (lambda i: n - 1 - i) if reverse else (lambda i: i)
    prev = pl.BlockSpec((GDN_HALO, GDN_QKV), lambda i: (jnp.maximum(blk(i) * (t // GDN_HALO) - 1, 0), 0))
    rows = lambda d: pl.BlockSpec((t, d), lambda i: (blk(i), 0))
    return t, rows, [prev, rows(GDN_QKV), rows(LANES)] + [pl.BlockSpec(p.shape, lambda i: (0, 0)) for p in params]


def _gdn_prep_masked(has_rows_before):
    return lambda prev, *rest: _gdn_prep_fn(prev * has_rows_before, *rest)


def _gdn_prep_fwd(qkv, ab, params):
    s = qkv.shape[0]
    t, rows, in_specs = _gdn_prep_specs(s, params)

    def body(prev_ref, cur_ref, ab_ref, *refs):
        p_refs, o_refs = refs[:len(params)], refs[len(params):]
        has_rows_before = (pl.program_id(0) > 0).astype(F32)
        outs = _gdn_prep_masked(has_rows_before)(prev_ref[...], cur_ref[...], ab_ref[...], *[p[...] for p in p_refs])
        for ref, val in zip(o_refs, outs):
            ref[...] = val

    return pl.pallas_call(
        body, name="gdn_prep", grid=(s // t,), in_specs=in_specs,
        out_specs=[rows(GDN_WIDTH)] * 5, out_shape=[jax.ShapeDtypeStruct((s, GDN_WIDTH), F32)] * 5,
        compiler_params=_cparams(("parallel",)))(qkv, qkv, ab, *params)


def _gdn_prep_bwd(qkv, ab, params, cts):
    s = qkv.shape[0]
    t, rows, in_specs = _gdn_prep_specs(s, params, reverse=True)
    n = s // t
    npar = len(params)

    def body(prev_ref, cur_ref, ab_ref, *refs):
        p_refs, g_refs = refs[:npar], refs[npar:npar + 5]
        dcur_ref, dab_ref = refs[npar + 5:npar + 7]
        dp_refs, carry_sc = refs[npar + 7:-1], refs[-1]

        @pl.when(pl.program_id(0) == 0)
        def _():
            carry_sc[...] = jnp.zeros_like(carry_sc)
            for ref in dp_refs:
                ref[...] = jnp.zeros_like(ref)

        has_rows_before = (pl.program_id(0) < n - 1).astype(F32)
        _, vjp = jax.vjp(_gdn_prep_masked(has_rows_before), prev_ref[...], cur_ref[...], ab_ref[...],
                         *[p[...] for p in p_refs])
        d_prev, d_cur, d_ab, *d_params = vjp(tuple(g[...] for g in g_refs))
        dcur_ref[...] = d_cur
        dcur_ref[t - GDN_HALO:, :] += carry_sc[...]
        carry_sc[...] = d_prev
        dab_ref[...] = d_ab
        for ref, val in zip(dp_refs, d_params):
            ref[...] += val

    res = pl.pallas_call(
        body, name="gdn_prep_bwd", grid=(n,), in_specs=in_specs + [rows(GDN_WIDTH)] * 5,
        out_specs=[rows(GDN_QKV), rows(LANES)] + [pl.BlockSpec(p.shape, lambda i: (0, 0)) for p in params],
        out_shape=[jax.ShapeDtypeStruct((s, GDN_QKV), F32), jax.ShapeDtypeStruct((s, LANES), F32)]
        + [jax.ShapeDtypeStruct(p.shape, F32) for p in params],
        scratch_shapes=[pltpu.VMEM((GDN_HALO, GDN_QKV), F32)],
        compiler_params=_cparams(("arbitrary",)))(qkv, qkv, ab, *params, *cts)
    return res[0], res[1], tuple(res[2:])


@jax.custom_vjp
def gdn_prep(qkv, ab, params):
    return tuple(_gdn_prep_fwd(qkv, ab, params))


def _gdn_prep_vjp_fwd(qkv, ab, params):
    return tuple(_gdn_prep_fwd(qkv, ab, params)), (qkv, ab, params)


def _gdn_prep_vjp_bwd(res, cts):
    qkv, ab, params = res
    return _gdn_prep_bwd(qkv, ab, params, cts)


gdn_prep.defvjp(_gdn_prep_vjp_fwd, _gdn_prep_vjp_bwd)


@jax.custom_vjp
def _unit_lower_inverse(lms):
    c = lms[0].shape[0]
    row = lax.broadcasted_iota(jnp.int32, (c, c), 0)
    col = lax.broadcasted_iota(jnp.int32, (c, c), 1)
    ts = [(row == col).astype(F32) - jnp.where((row >> 1) == (col >> 1), lm, 0.0) for lm in lms]
    for level in range(1, int(math.log2(c))):
        below = ((row >> (level + 1)) == (col >> (level + 1))) & ((row >> level) != (col >> level))
        mids = [_dot_split(t, jnp.where(below, lm, 0.0)) for t, lm in zip(ts, lms)]
        ts = [t - _dot_split(mid, t) for t, mid in zip(ts, mids)]
    return tuple(ts)


def _uli_fwd(lms):
    ts = _unit_lower_inverse(lms)
    return ts, ts


def _uli_bwd(ts, gs):
    mids = [bdot(t, g, "tn") for t, g in zip(ts, gs)]
    return (tuple(-bdot(mid, t, "nt") for t, mid in zip(ts, mids)),)


_unit_lower_inverse.defvjp(_uli_fwd, _uli_bwd)


@jax.custom_vjp
def _known_inverse(lms, ts):
    return ts


_known_inverse.defvjp(lambda lms, ts: (ts, ts),
                      lambda ts, gs: (_uli_bwd(ts, gs)[0], tuple(jnp.zeros_like(t) for t in ts)))


def _gdn_chunk_fn(qs, ks, vs, gbs, bbs, s0s, known_ts=None):
    heads = range(len(qs))
    c = qs[0].shape[0]
    row = lax.broadcasted_iota(jnp.int32, (c, c), 0)
    col = lax.broadcasted_iota(jnp.int32, (c, c), 1)
    incl, strict = row >= col, row > col
    tri = incl.astype(F32)
    gc = [bdot(tri, gbs[h], "nn", True) for h in heads]
    decay = [jnp.exp(jnp.where(incl, gc[h] - gc[h].T, -1e30)) for h in heads]
    g_last = [jnp.sum(gbs[h], axis=0, keepdims=True) for h in heads]
    eg = [jnp.exp(gc[h]) for h in heads]
    kb = [ks[h] * bbs[h] for h in heads]
    lms = tuple(jnp.where(strict, bdot(kb[h], ks[h], "nt") * decay[h], 0.0) for h in heads)
    ts = _unit_lower_inverse(lms) if known_ts is None else _known_inverse(lms, known_ts)
    u = [bdot(ts[h], vs[h] * bbs[h]) for h in heads]
    w = [bdot(ts[h], kb[h] * eg[h]) for h in heads]
    qk = [bdot(qs[h], ks[h], "nt") * decay[h] for h in heads]
    v_new = [u[h] - bdot(w[h], s0s[h]) for h in heads]
    o = [bdot(qs[h] * eg[h], s0s[h]) + bdot(qk[h], v_new[h]) for h in heads]
    s1 = [s0s[h] * jnp.exp(g_last[h]) + bdot(ks[h] * jnp.exp(g_last[h] - gc[h]), v_new[h], "tn") for h in heads]
    return (tuple(o), tuple(s1)), ts


def _head_tiles(ref):
    return tuple(ref[:, h * 128:(h + 1) * 128] for h in range(GDN_HEADS))


def _gdn_fwd(q, k, v, gb, bb):
    s = q.shape[0]
    c = min(GDN_CHUNK, s)
    nc = s // c

    def body(q_ref, k_ref, v_ref, g_ref, b_ref, o_ref, st_ref, inv_ref, s_sc):
        @pl.when(pl.program_id(0) == 0)
        def _():
            s_sc[...] = jnp.zeros_like(s_sc)

        s0s = tuple(s_sc[h] for h in range(GDN_HEADS))
        for h in range(GDN_HEADS):
            st_ref[h, 0] = s0s[h]
        (os, s1s), ts = _gdn_chunk_fn(*[_head_tiles(ref) for ref in (q_ref, k_ref, v_ref, g_ref, b_ref)], s0s)
        for h in range(GDN_HEADS):
            o_ref[:, h * 128:(h + 1) * 128] = os[h]
            inv_ref[h, 0] = ts[h]
            s_sc[h] = s1s[h]

    blk = pl.BlockSpec((c, GDN_WIDTH), lambda n: (n, 0))
    return pl.pallas_call(
        body, name="gdn_fwd", grid=(nc,), in_specs=[blk] * 5,
        out_specs=[blk, pl.BlockSpec((GDN_HEADS, 1, 128, 128), lambda n: (0, n, 0, 0)),
                   pl.BlockSpec((GDN_HEADS, 1, c, c), lambda n: (0, n, 0, 0))],
        out_shape=[jax.ShapeDtypeStruct((s, GDN_WIDTH), F32), jax.ShapeDtypeStruct((GDN_HEADS, nc, 128, 128), F32),
                   jax.ShapeDtypeStruct((GDN_HEADS, nc, c, c), F32)],
        scratch_shapes=[pltpu.VMEM((GDN_HEADS, 128, 128), F32)],
        compiler_params=_cparams(("arbitrary",)))(q, k, v, gb, bb)


def _gdn_bwd(q, k, v, gb, bb, states, inverses, do):
    s = q.shape[0]
    c = min(GDN_CHUNK, s)
    nc = s // c

    def body(q_ref, k_ref, v_ref, g_ref, b_ref, st_ref, inv_ref, do_ref, dq_ref, dk_ref, dv_ref, dg_ref, db_ref, ds_sc):
        @pl.when(pl.program_id(0) == 0)
        def _():
            ds_sc[...] = jnp.zeros_like(ds_sc)

        s0s = tuple(st_ref[h, 0] for h in range(GDN_HEADS))
        ts = tuple(inv_ref[h, 0] for h in range(GDN_HEADS))
        chunk = lambda *args: _gdn_chunk_fn(*args, known_ts=ts)[0]
        _, vjp = jax.vjp(chunk, *[_head_tiles(ref) for ref in (q_ref, k_ref, v_ref, g_ref, b_ref)], s0s)
        *d_tiles, ds0s = vjp((_head_tiles(do_ref), tuple(ds_sc[h] for h in range(GDN_HEADS))))
        for h in range(GDN_HEADS):
            for ref, d in zip((dq_ref, dk_ref, dv_ref, dg_ref, db_ref), d_tiles):
                ref[:, h * 128:(h + 1) * 128] = d[h]
            ds_sc[h] = ds0s[h]

    blk = pl.BlockSpec((c, GDN_WIDTH), lambda n: (nc - 1 - n, 0))
    return pl.pallas_call(
        body, name="gdn_bwd", grid=(nc,),
        in_specs=[blk] * 5 + [pl.BlockSpec((GDN_HEADS, 1, 128, 128), lambda n: (0, nc - 1 - n, 0, 0)),
                              pl.BlockSpec((GDN_HEADS, 1, c, c), lambda n: (0, nc - 1 - n, 0, 0)), blk],
        out_specs=[blk] * 5, out_shape=[jax.ShapeDtypeStruct((s, GDN_WIDTH), F32)] * 5,
        scratch_shapes=[pltpu.VMEM((GDN_HEADS, 128, 128), F32)],
        compiler_params=_cparams(("arbitrary",)))(q, k, v, gb, bb, states, inverses, do)


@jax.custom_vjp
def gdn_core(q, k, v, gb, bb):
    return _gdn_fwd(q, k, v, gb, bb)[0]


def _gdn_core_fwd(q, k, v, gb, bb):
    o, states, inverses = _gdn_fwd(q, k, v, gb, bb)
    return o, (q, k, v, gb, bb, states, inverses)


def _gdn_core_bwd(res, do):
    return tuple(_gdn_bwd(*res, do))


gdn_core.defvjp(_gdn_core_fwd, _gdn_core_bwd)


def _cact_fn(c):
    return (_silu(c),)


def _prenorm_fn(x, w, scale_raw, scale_b, shift_raw, shift_b):
    return (_rms(x, w) * (1.0 + scale_raw + scale_b) + shift_raw + shift_b,)


def _rmsnorm_fn(x, w):
    return (_rms(x, w),)


def _gate_fn(o, z):
    return (o * _silu(z),)


def _gdn_out_fn(o, z, w):
    parts = [_rms(o[:, h * 128:(h + 1) * 128], w) for h in range(GDN_HEADS)]
    return (jnp.concatenate(parts, axis=1) * _silu(z),)


def _post_fn(x, y, w, gate_raw, gate_b):
    return (x + (gate_raw + gate_b) * _rms(y, w),)


def _loss_fn(y, tgt):
    err = y - tgt
    part = jnp.sum(0.5 * jnp.mean(err * err, axis=-1, keepdims=True), axis=0, keepdims=True)
    return (jnp.broadcast_to(part, (1, LANES)),)


_OPS = dict(
    cact=make_rowwise(_cact_fn, "c_act", (D_MODEL,), tile=16),
    prenorm=make_rowwise(_prenorm_fn, "prenorm", (D_MODEL,)),
    qnorm=make_rowwise(_rmsnorm_fn, "q_norm", (MLA_Q_RANK,)),
    kvnorm=make_rowwise(_rmsnorm_fn, "kv_norm", (MLA_KV_RANK,)),
    gate=make_rowwise(_gate_fn, "mla_gate", (MLA_WIDTH,)),
    gdn_out=make_rowwise(_gdn_out_fn, "gdn_out", (GDN_WIDTH,)),
    post=make_rowwise(_post_fn, "postnorm", (D_MODEL,)),
    loss=make_rowwise(_loss_fn, "loss", (), acc_dims=(LANES,), n_nondiff=1),
    lin_mod=make_linear("lin_mod"), lin_in=make_fan_linear("lin_in"), lin_q=make_linear("lin_q"),
    lin_kv=make_linear("lin_kv"), lin_out=make_fan_linear("lin_out"),
)


def _swap_halves(w):
    half = w.shape[-1] // 2
    return jnp.concatenate([w[..., half:], w[..., :half]], axis=-1)


def _w_in_groups(w):
    k_pe = w[:, 640:704]
    ab = jnp.concatenate([w[:, 2752:2760], jnp.zeros((w.shape[0], LANES - 8), w.dtype)], axis=1)
    return (w[:, :384], w[:, 384:640], jnp.concatenate([k_pe, _swap_halves(k_pe)], axis=1), w[:, 704:1216],
            w[:, 1216:2752], ab, w[:, 2760:])


def _q_up_ext(w):
    parts = []
    for h in range(MLA_HEADS):
        rope = w[:, h * 192 + 128:(h + 1) * 192]
        parts += [w[:, h * 192:h * 192 + 128], rope, _swap_halves(rope)]
    return jnp.concatenate(parts, axis=1)


def _kv_up_perm(w):
    ks = [w[:, h * 256:h * 256 + 128] for h in range(MLA_HEADS)]
    vs = [w[:, h * 256 + 128:(h + 1) * 256] for h in range(MLA_HEADS)]
    return jnp.concatenate(ks + vs, axis=1)


def _pad_lanes(v):
    return jnp.pad(v, (0, LANES - v.shape[0]))[None, :]


def _local_loss(weights, x, c, positions, target):
    s = x.shape[0]
    half = MLA_ROPE // 2
    inv_freq = jnp.power(ROPE_THETA, -jnp.arange(half, dtype=F32) * 2.0 / MLA_ROPE)
    ang = positions.astype(F32)[:, None] * inv_freq
    cos, sin, zero = jnp.cos(ang), jnp.sin(ang), jnp.zeros((s, 2 * half), F32)
    t1 = jnp.concatenate([cos, cos, zero], axis=1)
    t2 = jnp.concatenate([-sin, sin, zero], axis=1)

    (c_act,) = _OPS["cact"]((jnp.pad(c, ((0, 15), (0, 0))),), ())
    for l in range(DEPTH):
        mod = _OPS["lin_mod"](c_act, weights["w_mod"][l])[0:1]
        b = weights["b_mod"][l][None, :]
        shift_raw, scale_raw, gate_raw = mod[:, :1024], mod[:, 1024:2048], mod[:, 2048:]
        shift_b, scale_b, gate_b = b[:, :1024], b[:, 1024:2048], b[:, 2048:]
        (h,) = _OPS["prenorm"]((x,), (weights["pre_norm_w"][l][None], scale_raw, scale_b, shift_raw, shift_b))
        q_lat, kv_lat, kr, z_mla, qkv, ab, z_gdn = _OPS["lin_in"]((h,), (_w_in_groups(weights["w_in"][l]),))
        (qn,) = _OPS["qnorm"]((q_lat,), (weights["mla_q_norm_w"][l][None],))
        qraw = _OPS["lin_q"](qn, _q_up_ext(weights["mla_q_up"][l]))
        (kvn,) = _OPS["kvnorm"]((kv_lat,), (weights["mla_kv_norm_w"][l][None],))
        kvraw = _OPS["lin_kv"](kvn, _kv_up_perm(weights["mla_kv_up"][l]))
        o_mla = mla_attention(qraw, kvraw, kr, t1, t2)
        (y_mla,) = _OPS["gate"]((o_mla, z_mla), ())
        cw = weights["gdn_conv_w"][l]
        params = tuple(cw[j][None] for j in range(GDN_CONV))
        params += (_pad_lanes(weights["gdn_a_log"][l]), _pad_lanes(weights["gdn_dt_bias"][l]))
        qg, kg, vg, gb, bb = gdn_prep(qkv, ab, params)
        o_gdn = gdn_core(qg, kg, vg, gb, bb)
        (y_gdn,) = _OPS["gdn_out"]((o_gdn, z_gdn), (weights["gdn_o_norm_w"][l][None],))
        w_o = weights["w_out"][l]
        (y,) = _OPS["lin_out"]((y_mla, y_gdn), ((w_o[:MLA_WIDTH],), (w_o[MLA_WIDTH:],)))
        (x,) = _OPS["post"]((x, y), (weights["post_norm_w"][l][None], gate_raw, gate_b))
    (acc,) = _OPS["loss"]((x, target), ())
    return acc[0, 0]


def _chip_index():
    return 2 * lax.axis_index("x") + lax.axis_index("y")


def _other_chips(x, y):
    return [(1 - x, y), (x, 1 - y), (1 - x, 1 - y)]


def _any_spec():
    return pl.BlockSpec(memory_space=pl.ANY)


def _half(ref, hc):
    n = ref.shape[0] // 2
    return ref.at[pl.ds(hc * n, n)]


def ag_weights(shards):
    n = len(shards)

    def body(*refs):
        ins, outs = refs[:n], refs[n:2 * n]
        send_sems, recv_sems = refs[2 * n:]
        x, y, c = lax.axis_index("x"), lax.axis_index("y"), lax.axis_index("c")
        sibling = (x, y, 1 - c)
        chips = _other_chips(x, y)

        def copy(t, k, src, chip_xy, hc, to):
            return pltpu.make_async_remote_copy(
                src_ref=src, dst_ref=_half(outs[t].at[2 * chip_xy[0] + chip_xy[1]], hc),
                send_sem=send_sems.at[6 * t + k], recv_sem=recv_sems.at[6 * t + k], device_id=to, device_id_type=MESH)

        first = [copy(t, k, _half(ins[t], c), (x, y), c, (*chip, c)) for k, chip in enumerate(chips) for t in range(n)]
        for cp in first:
            cp.start()
        passed = []
        for k, chip in enumerate(chips):
            for t in range(n):
                landed = _half(outs[t].at[2 * chip[0] + chip[1]], c)
                copy(t, k, landed, chip, c, (x, y, c)).wait_recv()
                passed.append(copy(t, 3 + k, landed, chip, c, sibling))
                passed[-1].start()
        for k, chip in enumerate(chips):
            for t in range(n):
                copy(t, 3 + k, _half(ins[t], c), chip, 1 - c, (x, y, c)).wait_recv()
        for cp in first + passed:
            cp.wait_send()

    return pl.pallas_call(
        body, name="ag_weights", in_specs=[_any_spec()] * n, out_specs=[_any_spec()] * n,
        out_shape=[jax.ShapeDtypeStruct((N_CHIPS,) + a.shape, a.dtype) for a in shards],
        scratch_shapes=[pltpu.SemaphoreType.DMA((6 * n,)), pltpu.SemaphoreType.DMA((6 * n,))],
        compiler_params=pltpu.CompilerParams(has_side_effects=True))(*shards)


def rs_pair(gs):
    n = len(gs)

    def body(*refs):
        ins, outs = refs[:n], refs[n:2 * n]
        send_sems, recv_sems = refs[2 * n:]
        x, y, c = lax.axis_index("x"), lax.axis_index("y"), lax.axis_index("c")
        lh = [g.shape[1] // 2 for g in gs]
        copies = [pltpu.make_async_remote_copy(
            src_ref=ins[t].at[:, pl.ds((1 - c) * lh[t], lh[t])], dst_ref=outs[t], send_sem=send_sems.at[t],
            recv_sem=recv_sems.at[t], device_id=(x, y, 1 - c), device_id_type=MESH) for t in range(n)]
        for cp in copies:
            cp.start()
        for cp in copies:
            cp.wait()

    return pl.pallas_call(
        body, name="rs_pair", in_specs=[_any_spec()] * n, out_specs=[_any_spec()] * n,
        out_shape=[jax.ShapeDtypeStruct((N_CHIPS, g.shape[1] // 2) + g.shape[2:], F32) for g in gs],
        scratch_shapes=[pltpu.SemaphoreType.DMA((n,)), pltpu.SemaphoreType.DMA((n,))],
        compiler_params=pltpu.CompilerParams(has_side_effects=True))(*gs)


def rs_cross(pairs):
    n = len(pairs)

    def body(*refs):
        ins, outs = refs[:n], refs[n:2 * n]
        send_sems, recv_sems = refs[2 * n:]
        x, y, c = lax.axis_index("x"), lax.axis_index("y"), lax.axis_index("c")
        copies = []
        for k, chip in enumerate(_other_chips(x, y)):
            for t in range(n):
                copies.append(pltpu.make_async_remote_copy(
                    src_ref=ins[t].at[2 * chip[0] + chip[1]], dst_ref=outs[t].at[k], send_sem=send_sems.at[3 * t + k],
                    recv_sem=recv_sems.at[3 * t + k], device_id=(*chip, c), device_id_type=MESH))
        for cp in copies:
            cp.start()
        for cp in copies:
            cp.wait()

    return pl.pallas_call(
        body, name="rs_cross", in_specs=[_any_spec()] * n, out_specs=[_any_spec()] * n,
        out_shape=[jax.ShapeDtypeStruct((3,) + p.shape[1:], p.dtype) for p in pairs],
        scratch_shapes=[pltpu.SemaphoreType.DMA((3 * n,)), pltpu.SemaphoreType.DMA((3 * n,))],
        compiler_params=pltpu.CompilerParams(has_side_effects=True))(*pairs)


def rs_share(blocks):
    n = len(blocks)

    def body(*refs):
        ins, outs = refs[:n], refs[n:2 * n]
        send_sems, recv_sems = refs[2 * n:]
        x, y, c = lax.axis_index("x"), lax.axis_index("y"), lax.axis_index("c")
        sends = [pltpu.make_async_remote_copy(
            src_ref=_half(ins[t], c), dst_ref=_half(outs[t], c), send_sem=send_sems.at[t], recv_sem=recv_sems.at[t],
            device_id=(x, y, 1 - c), device_id_type=MESH) for t in range(n)]
        for cp in sends:
            cp.start()
        for t in range(n):
            pltpu.make_async_remote_copy(
                src_ref=_half(ins[t], c), dst_ref=_half(outs[t], 1 - c), send_sem=send_sems.at[t],
                recv_sem=recv_sems.at[t], device_id=(x, y, 1 - c), device_id_type=MESH).wait_recv()
        for cp in sends:
            cp.wait_send()

    return pl.pallas_call(
        body, name="rs_share", in_specs=[_any_spec()] * n, out_specs=[_any_spec()] * n,
        out_shape=[jax.ShapeDtypeStruct(b.shape, F32) for b in blocks],
        input_output_aliases={t: t for t in range(n)},
        scratch_shapes=[pltpu.SemaphoreType.DMA((n,)), pltpu.SemaphoreType.DMA((n,))],
        compiler_params=pltpu.CompilerParams(has_side_effects=True))(*blocks)


def _row_tile(rows, target):
    best = None
    for t in range(8, min(rows, target) + 1, 8):
        if rows % t == 0:
            best = t
    return rows if best is None else best


TILE_BYTES = 2 * 1024 * 1024


def _flat_rows(shape):
    rows = int(np.prod(shape[1:-1]))
    cols_padded = -(-shape[-1] // LANES) * LANES
    return rows, _row_tile(rows, max(8, TILE_BYTES // (4 * cols_padded)))


def pair_add(g, from_sibling, out_dtype):
    cols = g.shape[-1]
    rph, t = _flat_rows(from_sibling.shape)
    nt = rph // t
    c_arr = lax.axis_index("c").astype(jnp.int32).reshape(1)

    def body(c_ref, a_ref, b_ref, o_ref):
        o_ref[...] = (a_ref[...] + b_ref[...]).astype(o_ref.dtype)

    out = pl.pallas_call(
        body, name="pair_add",
        grid_spec=pltpu.PrefetchScalarGridSpec(
            num_scalar_prefetch=1, grid=(N_CHIPS, nt),
            in_specs=[pl.BlockSpec((t, cols), lambda j, i, c_ref: (j * 2 * nt + c_ref[0] * nt + i, 0)),
                      pl.BlockSpec((t, cols), lambda j, i, c_ref: (j * nt + i, 0))],
            out_specs=pl.BlockSpec((t, cols), lambda j, i, c_ref: (j * nt + i, 0))),
        out_shape=jax.ShapeDtypeStruct((N_CHIPS * rph, cols), out_dtype),
        compiler_params=_cparams(("parallel", "parallel")))(c_arr, g.reshape(-1, cols), from_sibling.reshape(-1, cols))
    return out.reshape(from_sibling.shape)


def chip_add(pairs, received):
    cols = pairs.shape[-1]
    rph, t = _flat_rows(pairs.shape)
    nt = rph // t
    j_arr = _chip_index().astype(jnp.int32).reshape(1)
    c_arr = lax.axis_index("c").astype(jnp.int32).reshape(1)
    r2 = received.reshape(-1, cols)

    def body(j_ref, c_ref, a_ref, r0_ref, r1_ref, r2_ref, o_ref):
        a, r0, r1, r2 = [ref[...].astype(F32) for ref in (a_ref, r0_ref, r1_ref, r2_ref)]
        o_ref[...] = (a + r0) + (r1 + r2)

    out = pl.pallas_call(
        body, name="chip_add",
        grid_spec=pltpu.PrefetchScalarGridSpec(
            num_scalar_prefetch=2, grid=(nt,),
            in_specs=[pl.BlockSpec((t, cols), lambda i, j_ref, c_ref: (j_ref[0] * nt + i, 0)),
                      pl.BlockSpec((t, cols), lambda i, j_ref, c_ref: (i, 0)),
                      pl.BlockSpec((t, cols), lambda i, j_ref, c_ref: (nt + i, 0)),
                      pl.BlockSpec((t, cols), lambda i, j_ref, c_ref: (2 * nt + i, 0))],
            out_specs=pl.BlockSpec((t, cols), lambda i, j_ref, c_ref: (c_ref[0] * nt + i, 0))),
        out_shape=jax.ShapeDtypeStruct((2 * rph, cols), F32),
        compiler_params=_cparams(("parallel",)))(j_arr, c_arr, pairs.reshape(-1, cols), r2, r2, r2)
    return out.reshape((2 * pairs.shape[1],) + pairs.shape[2:])


def reduce_grads(gs, cross_dtypes):
    pairs = [pair_add(g, r, dt) for g, r, dt in zip(gs, rs_pair(gs), cross_dtypes)]
    return rs_share([chip_add(p, r) for p, r in zip(pairs, rs_cross(pairs))])


def adamw(w, g, m, v):
    shape = w.shape
    cols = shape[-1]
    rows = int(np.prod(shape[:-1]))
    flat = lambda a: a.reshape(rows, cols)
    t = _row_tile(rows, 256)

    def body(w_ref, g_ref, m_ref, v_ref, d_ref, mo_ref, vo_ref):
        gv = g_ref[...]
        m_new = ADAM_B1 * m_ref[...] + (1.0 - ADAM_B1) * gv
        v_new = ADAM_B2 * v_ref[...] + (1.0 - ADAM_B2) * (gv * gv)
        m_hat = m_new / (1.0 - ADAM_B1 ** ADAM_STEP)
        v_hat = v_new / (1.0 - ADAM_B2 ** ADAM_STEP)
        d_ref[...] = -ADAM_LR * (m_hat / (jnp.sqrt(v_hat) + ADAM_EPS) + ADAM_WD * w_ref[...])
        mo_ref[...] = m_new
        vo_ref[...] = v_new

    spec = pl.BlockSpec((t, cols), lambda i: (i, 0))
    outs = pl.pallas_call(
        body, name="adamw", grid=(rows // t,), in_specs=[spec] * 4, out_specs=[spec] * 3,
        out_shape=[jax.ShapeDtypeStruct((rows, cols), F32)] * 3,
        compiler_params=_cparams(("parallel",)))(flat(w), flat(g), flat(m), flat(v))
    return tuple(o.reshape(shape) for o in outs)


SHARDED = (("w_mod", 2), ("w_in", 2), ("mla_q_up", 2), ("mla_kv_up", 2), ("gdn_conv_w", 2), ("w_out", 1))
REPLICATED = ("b_mod", "pre_norm_w", "post_norm_w", "mla_q_norm_w", "mla_kv_norm_w", "gdn_a_log", "gdn_dt_bias",
              "gdn_o_norm_w")
WEIGHT_ORDER = ("w_mod", "b_mod", "pre_norm_w", "post_norm_w", "w_in", "mla_q_norm_w", "mla_q_up", "mla_kv_norm_w",
                "mla_kv_up", "gdn_conv_w", "gdn_a_log", "gdn_dt_bias", "gdn_o_norm_w", "w_out")
EXACT_F32 = ("gdn_conv_w",)
SMALL_ROWS = 48


def _gather_weights(shards):
    names = [name for name, _ in SHARDED]
    own = [shards[n] if n in EXACT_F32 else shards[n].astype(BF16) for n in names]
    gathered = ag_weights(own)
    full = {}
    for (name, axis), blk, mine in zip(SHARDED, gathered, own):
        shp = shards[name].shape
        blk = lax.dynamic_update_index_in_dim(blk, mine, _chip_index(), 0)
        blk = jnp.moveaxis(blk.astype(F32), 0, axis)
        full[name] = blk.reshape(shp[:axis] + (N_CHIPS * shp[axis],) + shp[axis + 1:])
    return full


def _split_grads(grads):
    pieces = []
    for name, axis in SHARDED:
        g = grads[name]
        shp = g.shape
        g = g.reshape(shp[:axis] + (N_CHIPS, shp[axis] // N_CHIPS) + shp[axis + 1:])
        pieces.append(jnp.moveaxis(g, axis, 0))
    small = jnp.concatenate([grads[name] for name in REPLICATED], axis=1)
    small = jnp.pad(small, ((0, 0), (0, SMALL_ROWS * LANES - small.shape[1]))).reshape(DEPTH, SMALL_ROWS, LANES)
    pieces.append(jnp.broadcast_to(small[None], (N_CHIPS,) + small.shape))
    return pieces


def _unsplit_small(small, rep_shapes):
    flat = small.reshape(DEPTH, SMALL_ROWS * LANES)
    out, off = {}, 0
    for name in REPLICATED:
        size = rep_shapes[name][1]
        out[name] = flat[:, off:off + size]
        off += size
    return out


def kernel(x, c, positions, w_mod, b_mod, pre_norm_w, post_norm_w, w_in, mla_q_norm_w, mla_q_up, mla_kv_norm_w, mla_kv_up, gdn_conv_w, gdn_a_log, gdn_dt_bias, gdn_o_norm_w, w_out, loss_target, m_w_mod, m_b_mod, m_pre_norm_w, m_post_norm_w, m_w_in, m_mla_q_norm_w, m_mla_q_up, m_mla_kv_norm_w, m_mla_kv_up, m_gdn_conv_w, m_gdn_a_log, m_gdn_dt_bias, m_gdn_o_norm_w, m_w_out, v_w_mod, v_b_mod, v_pre_norm_w, v_post_norm_w, v_w_in, v_mla_q_norm_w, v_mla_q_up, v_mla_kv_norm_w, v_mla_kv_up, v_gdn_conv_w, v_gdn_a_log, v_gdn_dt_bias, v_gdn_o_norm_w, v_w_out):
    given = dict(w_mod=w_mod, b_mod=b_mod, pre_norm_w=pre_norm_w, post_norm_w=post_norm_w, w_in=w_in,
                 mla_q_norm_w=mla_q_norm_w, mla_q_up=mla_q_up, mla_kv_norm_w=mla_kv_norm_w, mla_kv_up=mla_kv_up,
                 gdn_conv_w=gdn_conv_w, gdn_a_log=gdn_a_log, gdn_dt_bias=gdn_dt_bias, gdn_o_norm_w=gdn_o_norm_w,
                 w_out=w_out)
    moments_m = dict(w_mod=m_w_mod, b_mod=m_b_mod, pre_norm_w=m_pre_norm_w, post_norm_w=m_post_norm_w, w_in=m_w_in,
                     mla_q_norm_w=m_mla_q_norm_w, mla_q_up=m_mla_q_up, mla_kv_norm_w=m_mla_kv_norm_w,
                     mla_kv_up=m_mla_kv_up, gdn_conv_w=m_gdn_conv_w, gdn_a_log=m_gdn_a_log,
                     gdn_dt_bias=m_gdn_dt_bias, gdn_o_norm_w=m_gdn_o_norm_w, w_out=m_w_out)
    moments_v = dict(w_mod=v_w_mod, b_mod=v_b_mod, pre_norm_w=v_pre_norm_w, post_norm_w=v_post_norm_w, w_in=v_w_in,
                     mla_q_norm_w=v_mla_q_norm_w, mla_q_up=v_mla_q_up, mla_kv_norm_w=v_mla_kv_norm_w,
                     mla_kv_up=v_mla_kv_up, gdn_conv_w=v_gdn_conv_w, gdn_a_log=v_gdn_a_log,
                     gdn_dt_bias=v_gdn_dt_bias, gdn_o_norm_w=v_gdn_o_norm_w, w_out=v_w_out)

    full = _gather_weights({name: given[name] for name, _ in SHARDED})
    for name in REPLICATED:
        full[name] = given[name]
    loss_local, (grads, grad_x) = jax.value_and_grad(_local_loss, argnums=(0, 1))(
        full, x[0], c, positions[0], loss_target[0])
    loss = lax.psum(loss_local, AXES)

    cross_dtypes = [F32 if name in EXACT_F32 else BF16 for name, _ in SHARDED] + [F32]
    reduced = reduce_grads(_split_grads(grads), cross_dtypes)
    grad_w = {name: g for (name, _), g in zip(SHARDED, reduced)}
    grad_w.update(_unsplit_small(reduced[-1], {name: given[name].shape for name in REPLICATED}))
    delta, new_m, new_v = {}, {}, {}
    for name in WEIGHT_ORDER:
        delta[name], new_m[name], new_v[name] = adamw(given[name], grad_w[name], moments_m[name], moments_v[name])
    return (loss, grad_x[None], *[grad_w[n] for n in WEIGHT_ORDER], *[delta[n] for n in WEIGHT_ORDER],
            *[new_m[n] for n in WEIGHT_ORDER], *[new_v[n] for n in WEIGHT_ORDER])
```

```python
import functools
import math

import numpy as np
import jax
import jax.numpy as jnp
from jax import lax
from jax.experimental import pallas as pl
from jax.experimental.pallas import tpu as pltpu

F32 = jnp.float32
BF16 = jnp.bfloat16
MESH = pl.DeviceIdType.MESH
AXES = ("x", "y", "c")

D_MODEL = 1024
DEPTH = 4
MLA_HEADS = 4
MLA_NOPE = 128
MLA_ROPE = 64
MLA_V = 128
MLA_Q_RANK = 384
MLA_KV_RANK = 256
MLA_WIDTH = 512
GDN_HEADS = 4
GDN_DK = 128
GDN_WIDTH = 512
GDN_QKV = 1536
GDN_CONV = 4
IN_COLS = 3272
ROPE_THETA = 10000.0
NORM_EPS = 1e-6
ADAM_LR, ADAM_B1, ADAM_B2, ADAM_EPS, ADAM_WD, ADAM_STEP = 0.001, 0.9, 0.999, 1e-08, 0.01, 10

LANES = 128
N_CHIPS = 4
GDN_CHUNK = 128
VMEM_LIMIT = 56 * 1024 * 1024


def _cparams(sem=None):
    if sem is None:
        return pltpu.CompilerParams(vmem_limit_bytes=VMEM_LIMIT)
    return pltpu.CompilerParams(dimension_semantics=sem, vmem_limit_bytes=VMEM_LIMIT)


def _pick(dim, target):
    if dim <= target:
        return dim
    best = None
    for t in range(LANES, target + 1, LANES):
        if dim % t == 0:
            best = t
    assert best is not None, (dim, target)
    return best


_DN = {"nn": (((1,), (0,)), ((), ())), "nt": (((1,), (1,)), ((), ())), "tn": (((0,), (0,)), ((), ()))}


def _dot_raw(a, b, mode, exact):
    if exact:
        return lax.dot_general(a, b, _DN[mode], precision=lax.Precision.HIGHEST, preferred_element_type=F32)
    return lax.dot_general(a.astype(BF16), b.astype(BF16), _DN[mode], preferred_element_type=F32)


def _dot_split(a, b):
    a_hi, b_hi = a.astype(BF16), b.astype(BF16)
    a_lo, b_lo = (a - a_hi.astype(F32)).astype(BF16), (b - b_hi.astype(F32)).astype(BF16)
    dot = lambda u, w: lax.dot_general(u, w, _DN["nn"], preferred_element_type=F32)
    return dot(a_hi, b_hi) + (dot(a_hi, b_lo) + dot(a_lo, b_hi))


@functools.partial(jax.custom_vjp, nondiff_argnums=(2, 3))
def bdot(a, b, mode="nn", exact=False):
    return _dot_raw(a, b, mode, exact)


def _bdot_fwd(a, b, mode, exact):
    return _dot_raw(a, b, mode, exact), (a, b)


def _bdot_bwd(mode, exact, res, g):
    a, b = res
    if mode == "nn":
        return bdot(g, b, "nt", exact), bdot(a, g, "tn", exact)
    if mode == "nt":
        return bdot(g, b, "nn", exact), bdot(g, a, "tn", exact)
    return bdot(b, g, "nt", exact), bdot(a, g, "nn", exact)


bdot.defvjp(_bdot_fwd, _bdot_bwd)


@jax.custom_vjp
def roll_half(x):
    return pltpu.roll(x, 64, 1)


roll_half.defvjp(lambda x: (pltpu.roll(x, 64, 1), None), lambda _, g: (pltpu.roll(g, 64, 1),))


def _sigmoid(x):
    return 1.0 / (1.0 + jnp.exp(-x))


def _silu(x):
    return x * _sigmoid(x)


def _softplus(x):
    return jnp.maximum(x, 0.0) + jnp.log(1.0 + jnp.exp(-jnp.abs(x)))


def _rms(x, w):
    return x * lax.rsqrt(jnp.mean(x * x, axis=-1, keepdims=True) + NORM_EPS) * w


def _rw_fwd(fn, name, rows, params, out_dims, out_dtypes, acc_dims, tile):
    s = rows[0].shape[0]
    t = min(tile, s)
    n = s // t
    nr, npar, no, na = len(rows), len(params), len(out_dims), len(acc_dims)

    def body(*refs):
        r, p = refs[:nr], refs[nr:nr + npar]
        o, a = refs[nr + npar:nr + npar + no], refs[nr + npar + no:]
        outs = fn(*[x[...] for x in r], *[x[...] for x in p])
        for ref, val in zip(o, outs[:no]):
            ref[...] = val.astype(ref.dtype)
        if na:
            @pl.when(pl.program_id(0) == 0)
            def _():
                for ref in a:
                    ref[...] = jnp.zeros_like(ref)
            for ref, val in zip(a, outs[no:]):
                ref[...] += val

    in_specs = [pl.BlockSpec((t, x.shape[1]), lambda i: (i, 0)) for x in rows]
    in_specs += [pl.BlockSpec(x.shape, lambda i: (0, 0)) for x in params]
    out_specs = [pl.BlockSpec((t, d), lambda i: (i, 0)) for d in out_dims]
    out_specs += [pl.BlockSpec((1, d), lambda i: (0, 0)) for d in acc_dims]
    out_shape = [jax.ShapeDtypeStruct((s, d), dt) for d, dt in zip(out_dims, out_dtypes)]
    out_shape += [jax.ShapeDtypeStruct((1, d), F32) for d in acc_dims]
    res = pl.pallas_call(body, name=name, grid=(n,), in_specs=in_specs, out_specs=out_specs, out_shape=out_shape,
                         compiler_params=_cparams(("arbitrary",)))(*rows, *params)
    return tuple(res)


def _rw_bwd(fn, name, rows, params, row_cts, acc_cts, n_diff, tile):
    s = rows[0].shape[0]
    t = min(tile, s)
    n = s // t
    nr, npar, no, na = len(rows), len(params), len(row_cts), len(acc_cts)

    def body(*refs):
        r, p = refs[:nr], refs[nr:nr + npar]
        g, ga = refs[nr + npar:nr + npar + no], refs[nr + npar + no:nr + npar + no + na]
        dr, dp = refs[nr + npar + no + na:nr + npar + no + na + n_diff], refs[nr + npar + no + na + n_diff:]
        _, vjp = jax.vjp(fn, *[x[...] for x in r], *[x[...] for x in p])
        cts = vjp(tuple([x[...] for x in g] + [x[...] for x in ga]))
        for ref, val in zip(dr, cts[:n_diff]):
            ref[...] = val
        if npar:
            @pl.when(pl.program_id(0) == 0)
            def _():
                for ref in dp:
                    ref[...] = jnp.zeros_like(ref)
            for ref, val in zip(dp, cts[nr:]):
                ref[...] += val

    in_specs = [pl.BlockSpec((t, x.shape[1]), lambda i: (i, 0)) for x in rows]
    in_specs += [pl.BlockSpec(x.shape, lambda i: (0, 0)) for x in params]
    in_specs += [pl.BlockSpec((t, x.shape[1]), lambda i: (i, 0)) for x in row_cts]
    in_specs += [pl.BlockSpec(x.shape, lambda i: (0, 0)) for x in acc_cts]
    out_specs = [pl.BlockSpec((t, x.shape[1]), lambda i: (i, 0)) for x in rows[:n_diff]]
    out_specs += [pl.BlockSpec(x.shape, lambda i: (0, 0)) for x in params]
    out_shape = [jax.ShapeDtypeStruct(x.shape, F32) for x in rows[:n_diff]]
    out_shape += [jax.ShapeDtypeStruct(x.shape, F32) for x in params]
    res = pl.pallas_call(body, name=name, grid=(n,), in_specs=in_specs, out_specs=out_specs, out_shape=out_shape,
                         compiler_params=_cparams(("arbitrary",)))(*rows, *params, *row_cts, *acc_cts)
    return tuple(res[:n_diff]), tuple(res[n_diff:])


def make_rowwise(fn, name, out_dims, acc_dims=(), n_nondiff=0, tile=256):
    out_dtypes = (F32,) * len(out_dims)

    @jax.custom_vjp
    def op(rows, params):
        return _rw_fwd(fn, name, rows, params, out_dims, out_dtypes, acc_dims, tile)

    def fwd(rows, params):
        return op(rows, params), (rows, params)

    def bwd(res, cts):
        rows, params = res
        n_diff = len(rows) - n_nondiff
        d_rows, d_params = _rw_bwd(fn, name + "_bwd", rows, params, cts[:len(out_dims)], cts[len(out_dims):],
                                   n_diff, tile)
        d_rows = d_rows + tuple(jnp.zeros_like(x) for x in rows[n_diff:])
        return d_rows, d_params

    op.defvjp(fwd, bwd)
    return op


def _mm(a, b, mode, name):
    if mode == "nn":
        (m, k), (_, n) = a.shape, b.shape
    elif mode == "nt":
        (m, k), (n, _) = a.shape, b.shape
    else:
        (k, m), (_, n) = a.shape, b.shape
    tm = _pick(m, 512)
    tn = _pick(n, 1152)
    tk = _pick(k, 1152) if mode != "tn" else _pick(k, 512)
    nk = k // tk

    def body(a_ref, b_ref, o_ref, acc_ref):
        kk = pl.program_id(2)

        @pl.when(kk == 0)
        def _():
            acc_ref[...] = jnp.zeros_like(acc_ref)

        acc_ref[...] += _dot_raw(a_ref[...], b_ref[...], mode, False)

        @pl.when(kk == nk - 1)
        def _():
            o_ref[...] = acc_ref[...]

    if mode == "nn":
        a_spec = pl.BlockSpec((tm, tk), lambda i, j, kk: (i, kk))
        b_spec = pl.BlockSpec((tk, tn), lambda i, j, kk: (kk, j))
    elif mode == "nt":
        a_spec = pl.BlockSpec((tm, tk), lambda i, j, kk: (i, kk))
        b_spec = pl.BlockSpec((tn, tk), lambda i, j, kk: (j, kk))
    else:
        a_spec = pl.BlockSpec((tk, tm), lambda i, j, kk: (kk, i))
        b_spec = pl.BlockSpec((tk, tn), lambda i, j, kk: (kk, j))
    return pl.pallas_call(
        body, name=name, grid=(m // tm, n // tn, nk), in_specs=[a_spec, b_spec],
        out_specs=pl.BlockSpec((tm, tn), lambda i, j, kk: (i, j)),
        out_shape=jax.ShapeDtypeStruct((m, n), F32), scratch_shapes=[pltpu.VMEM((tm, tn), F32)],
        compiler_params=_cparams(("parallel", "parallel", "arbitrary")))(a, b)


def make_linear(name):
    @jax.custom_vjp
    def op(a, w):
        return _mm(a, w.astype(BF16), "nn", name)

    def fwd(a, w):
        w16 = w.astype(BF16)
        return _mm(a, w16, "nn", name), (a, w16)

    def bwd(res, g):
        a, w16 = res
        return _mm(g, w16, "nt", name + "_dx"), _mm(a, g, "tn", name + "_dw")

    op.defvjp(fwd, bwd)
    return op


def _fan_mm(xs, ws, transposed, name, tile=256):
    n_in, n_out = len(ws), len(ws[0])
    s = xs[0].shape[0]
    t = min(tile, s)
    out_dims = [ws[i][0].shape[0] for i in range(n_in)] if transposed else [ws[0][j].shape[1] for j in range(n_out)]
    flat_ws = [w for row in ws for w in row]

    def body(*refs):
        x_refs, w_refs, o_refs = refs[:len(xs)], refs[len(xs):len(xs) + len(flat_ws)], refs[len(xs) + len(flat_ws):]
        xv = [r[...].astype(BF16) for r in x_refs]
        for o, o_ref in enumerate(o_refs):
            if transposed:
                terms = [_dot_raw(xv[j], w_refs[o * n_out + j][...], "nt", False) for j in range(n_out)]
            else:
                terms = [_dot_raw(xv[i], w_refs[i * n_out + o][...], "nn", False) for i in range(n_in)]
            o_ref[...] = functools.reduce(lambda a, b: a + b, terms)

    in_specs = [pl.BlockSpec((t, x.shape[1]), lambda i: (i, 0)) for x in xs]
    in_specs += [pl.BlockSpec(w.shape, lambda i: (0, 0)) for w in flat_ws]
    res = pl.pallas_call(
        body, name=name, grid=(s // t,), in_specs=in_specs,
        out_specs=[pl.BlockSpec((t, d), lambda i: (i, 0)) for d in out_dims],
        out_shape=[jax.ShapeDtypeStruct((s, d), F32) for d in out_dims],
        compiler_params=_cparams(("parallel",)))(*xs, *flat_ws)
    return tuple(res)


def _fan_dw(x, dys, name, tile=256):
    s, k = x.shape
    t = min(tile, s)

    def body(x_ref, *refs):
        dy_refs, o_refs = refs[:len(dys)], refs[len(dys):]

        @pl.when(pl.program_id(0) == 0)
        def _():
            for o_ref in o_refs:
                o_ref[...] = jnp.zeros_like(o_ref)

        xt = x_ref[...].astype(BF16)
        for dy_ref, o_ref in zip(dy_refs, o_refs):
            o_ref[...] += _dot_raw(xt, dy_ref[...], "tn", False)

    rows = lambda d: pl.BlockSpec((t, d), lambda i: (i, 0))
    res = pl.pallas_call(
        body, name=name, grid=(s // t,), in_specs=[rows(k)] + [rows(dy.shape[1]) for dy in dys],
        out_specs=[pl.BlockSpec((k, dy.shape[1]), lambda i: (0, 0)) for dy in dys],
        out_shape=[jax.ShapeDtypeStruct((k, dy.shape[1]), F32) for dy in dys],
        compiler_params=_cparams(("arbitrary",)))(x, *dys)
    return tuple(res)


def make_fan_linear(name):
    @jax.custom_vjp
    def op(xs, ws):
        return _fan_mm(xs, tuple(tuple(w.astype(BF16) for w in row) for row in ws), False, name)

    def fwd(xs, ws):
        ws16 = tuple(tuple(w.astype(BF16) for w in row) for row in ws)
        return _fan_mm(xs, ws16, False, name), (xs, ws16)

    def bwd(res, dys):
        xs, ws16 = res
        dxs = _fan_mm(dys, ws16, True, name + "_dx")
        dws = tuple(_fan_dw(x, dys, name + "_dw") for x in xs)
        return dxs, dws

    op.defvjp(fwd, bwd)
    return op


def _mla_prep_fn(qraw, kvraw, kr, t1, t2):
    kr_rot = kr * t1 + roll_half(kr) * t2
    qs, ks = [], []
    for h in range(MLA_HEADS):
        q_r = qraw[:, h * 256 + 128:(h + 1) * 256]
        qs += [qraw[:, h * 256:h * 256 + 128], q_r * t1 + roll_half(q_r) * t2]
        ks += [kvraw[:, h * 128:(h + 1) * 128], kr_rot]
    return jnp.concatenate(qs, axis=1), jnp.concatenate(ks, axis=1), kvraw[:, 512:]


def _flash_tile(s):
    return 512 if s >= 2048 else 128


FLASH_SCALE = (MLA_NOPE + MLA_ROPE) ** -0.5
LOG2_E = 1.4426950408889634
FLASH_EXP2 = FLASH_SCALE * LOG2_E
STAT_ROWS = 8
FLASH_PAIR = 2
FLASH_STRIP = 32


def _as_rows(col_b):
    ones = jnp.full((STAT_ROWS, LANES), 1.0 / LANES, F32)
    return _dot_raw(ones, col_b, "nt", True)


def _flash_fwd(qf, kf, vf):
    s = qf.shape[0]
    t = _flash_tile(s)
    nb = s // t
    pair = range(FLASH_PAIR)

    ck = min(FLASH_STRIP, t)

    def body(q_ref, k_ref, v_ref, o_ref, lse_ref, m_sc, l_sc, acc_sc, st_sc, pt_sc):
        i = pl.program_id(1)
        m_sc[...] = jnp.full_like(m_sc, -1e30)
        l_sc[...] = jnp.zeros_like(l_sc)
        acc_sc[...] = jnp.zeros_like(acc_sc)
        qs = [q_ref[:, hh * 256:(hh + 1) * 256] for hh in pair]

        def step(j, on_diagonal):
            rows = pl.ds(pl.multiple_of(j * t, t), t)
            for hh in pair:
                st = _dot_raw(k_ref[rows, hh * 256:(hh + 1) * 256], qs[hh], "nt", False)
                if on_diagonal:
                    keep = lax.broadcasted_iota(jnp.int32, (t, t), 0) <= lax.broadcasted_iota(jnp.int32, (t, t), 1)
                    st = jnp.where(keep, st, -1e30)
                st_sc[hh] = st
            m_olds = [m_sc[hh] for hh in pair]
            m_news = [jnp.maximum(m_olds[hh], jnp.max(st_sc[hh], axis=0, keepdims=True)) for hh in pair]
            alphas = [jnp.exp2((m_olds[hh] - m_news[hh]) * FLASH_EXP2) for hh in pair]
            shifts = [m_news[hh] * FLASH_EXP2 for hh in pair]
            sums = [jnp.zeros((8, t), F32) for _ in pair]
            for hh in pair:
                for r in range(0, t, ck):
                    p = jnp.exp2(st_sc[hh, r:r + ck, :] * FLASH_EXP2 - shifts[hh])
                    pt_sc[hh, r:r + ck, :] = p.astype(BF16)
                    sums[hh] = sums[hh] + functools.reduce(lambda a, b: a + b, [p[u:u + 8] for u in range(0, ck, 8)])
            pvs = [_dot_raw(v_ref[rows, hh * 128:(hh + 1) * 128], pt_sc[hh], "tn", False) for hh in pair]
            for hh in pair:
                l_sc[hh] = alphas[hh] * l_sc[hh] + jnp.sum(sums[hh], axis=0, keepdims=True)
                acc_sc[hh] = alphas[hh] * acc_sc[hh] + pvs[hh]
                m_sc[hh] = m_news[hh]

        def two_steps(p, carry):
            step(2 * p, False)
            step(2 * p + 1, False)
            return carry

        lax.fori_loop(0, i // 2, two_steps, 0)

        @pl.when(i % 2 == 1)
        def _():
            step(i - 1, False)

        step(i, True)
        for hh in pair:
            o_ref[:, hh * 128:(hh + 1) * 128] = (acc_sc[hh] / l_sc[hh]).T
            lse2 = m_sc[hh] * FLASH_EXP2 + jnp.log(l_sc[hh]) * LOG2_E
            lse_ref[hh] = jnp.broadcast_to(lse2, (STAT_ROWS, t))

    p = FLASH_PAIR
    return pl.pallas_call(
        body, name="flash_fwd", grid=(MLA_HEADS // p, nb),
        in_specs=[pl.BlockSpec((t, p * 256), lambda h, i: (i, h)), pl.BlockSpec((s, p * 256), lambda h, i: (0, h)),
                  pl.BlockSpec((s, p * 128), lambda h, i: (0, h))],
        out_specs=[pl.BlockSpec((t, p * 128), lambda h, i: (i, h)),
                   pl.BlockSpec((p, STAT_ROWS, t), lambda h, i: (h, 0, i))],
        out_shape=[jax.ShapeDtypeStruct((s, MLA_WIDTH), F32),
                   jax.ShapeDtypeStruct((MLA_HEADS, STAT_ROWS, s), F32)],
        scratch_shapes=[pltpu.VMEM((p, 1, t), F32), pltpu.VMEM((p, 1, t), F32), pltpu.VMEM((p, 128, t), F32),
                        pltpu.VMEM((p, t, t), F32), pltpu.VMEM((p, t, t), BF16)],
        compiler_params=_cparams(("parallel", "arbitrary")))(qf, kf, vf)


def _flash_bwd_prep(o, do):
    s = o.shape[0]
    t = _flash_tile(s)

    def body(o_ref, do_ref, dl_ref, do16_ref):
        do_t = do_ref[...]
        delta = jnp.sum(do_t * o_ref[...], axis=1, keepdims=True)
        dl_ref[0] = _as_rows(jnp.broadcast_to(delta, (t, LANES)))
        do16_ref[...] = do_t.astype(BF16)

    blk = pl.BlockSpec((t, 128), lambda h, i: (i, h))
    return pl.pallas_call(
        body, name="flash_bwd_prep", grid=(MLA_HEADS, s // t), in_specs=[blk, blk],
        out_specs=[pl.BlockSpec((1, STAT_ROWS, t), lambda h, i: (h, 0, i)), blk],
        out_shape=[jax.ShapeDtypeStruct((MLA_HEADS, STAT_ROWS, s), F32), jax.ShapeDtypeStruct((s, MLA_WIDTH), BF16)],
        compiler_params=_cparams(("parallel", "parallel")))(o, do)


def _flash_bwd(qf, kf, vf, lse, delta, do16):
    s = qf.shape[0]
    t = _flash_tile(s)
    nb = s // t

    def body(q_ref, k_ref, v_ref, lse_ref, dl_ref, do_ref, dq_ref, dk_ref, dv_ref):
        j = pl.program_id(1)

        @pl.when(j == 0)
        def _():
            dq_ref[...] = jnp.zeros_like(dq_ref)

        dk_ref[...] = jnp.zeros_like(dk_ref)
        dv_ref[...] = jnp.zeros_like(dv_ref)
        k = k_ref[...]
        v = v_ref[...]

        def step(i, on_diagonal):
            rows = pl.ds(pl.multiple_of(i * t, t), t)
            q = q_ref[rows, :]
            do_t = do_ref[rows, :]
            st = _dot_raw(k, q, "nt", False) * FLASH_EXP2 - lse_ref[0, 0:1, rows]
            if on_diagonal:
                keep = lax.broadcasted_iota(jnp.int32, (t, t), 0) <= lax.broadcasted_iota(jnp.int32, (t, t), 1)
                st = jnp.where(keep, st, -1e30)
            pt = jnp.exp2(st)
            dst = pt * (_dot_raw(v, do_t, "nt", False) - dl_ref[0, 0:1, rows])
            dv_ref[...] += _dot_raw(pt, do_t, "nn", False)
            dk_ref[...] += _dot_raw(dst, q, "nn", False)
            dq_ref[rows, :] += _dot_raw(dst, k, "tn", False)

        def two_steps(p, carry):
            step(j + 1 + 2 * p, False)
            step(j + 2 + 2 * p, False)
            return carry

        step(j, True)
        below = nb - 1 - j
        lax.fori_loop(0, below // 2, two_steps, 0)

        @pl.when(below % 2 == 1)
        def _():
            step(nb - 1, False)

        dk_ref[...] *= FLASH_SCALE

        @pl.when(j == nb - 1)
        def _():
            dq_ref[...] *= FLASH_SCALE

    stat = pl.BlockSpec((1, STAT_ROWS, s), lambda h, j: (h, 0, 0))
    return pl.pallas_call(
        body, name="flash_bwd", grid=(MLA_HEADS, nb),
        in_specs=[pl.BlockSpec((s, 256), lambda h, j: (0, h)), pl.BlockSpec((t, 256), lambda h, j: (j, h)),
                  pl.BlockSpec((t, 128), lambda h, j: (j, h)), stat, stat, pl.BlockSpec((s, 128), lambda h, j: (0, h))],
        out_specs=[pl.BlockSpec((s, 256), lambda h, j: (0, h)), pl.BlockSpec((t, 256), lambda h, j: (j, h)),
                   pl.BlockSpec((t, 128), lambda h, j: (j, h))],
        out_shape=[jax.ShapeDtypeStruct((s, 1024), F32), jax.ShapeDtypeStruct((s, 1024), F32),
                   jax.ShapeDtypeStruct((s, MLA_WIDTH), F32)],
        compiler_params=_cparams(("parallel", "arbitrary")))(qf, kf, vf, lse, delta, do16)


def _mla_front_fn(q_lat, kv_lat, kr, t1, t2, q_norm_w, q_up, kv_norm_w, kv_up):
    qraw = bdot(_rms(q_lat, q_norm_w), q_up)
    kvraw = bdot(_rms(kv_lat, kv_norm_w), kv_up)
    return _mla_prep_fn(qraw, kvraw, kr, t1, t2)


def _mla_front(rows, params):
    return _rw_fwd(_mla_front_fn, "mla_front", rows, params, (1024, 1024, 512), (BF16, BF16, BF16), (), 256)


@jax.custom_vjp
def mla_attention(rows, params):
    return _flash_fwd(*_mla_front(rows, params))[0]


def _mla_attention_fwd(rows, params):
    qf, kf, vf = _mla_front(rows, params)
    o, lse = _flash_fwd(qf, kf, vf)
    return o, (rows, params, qf, kf, vf, o, lse)


def _mla_attention_bwd(res, do):
    rows, params, qf, kf, vf, o, lse = res
    delta, do16 = _flash_bwd_prep(o, do)
    d_rows, d_params = _rw_bwd(_mla_front_fn, "mla_front_bwd", rows, params, _flash_bwd(qf, kf, vf, lse, delta, do16),
                               (), 3, 256)
    return d_rows + (jnp.zeros_like(rows[3]), jnp.zeros_like(rows[4])), d_params


mla_attention.defvjp(_mla_attention_fwd, _mla_attention_bwd)


def _lane_pick(x, lane):
    ids = lax.broadcasted_iota(jnp.int32, x.shape, 1)
    col = jnp.sum(jnp.where(ids == lane, x, 0.0), axis=1, keepdims=True)
    return jnp.broadcast_to(col, x.shape)


GDN_HALO = 8


@functools.partial(jax.custom_vjp, nondiff_argnums=(1,))
def _roll_rows(x, d):
    return pltpu.roll(x, d, 0)


_roll_rows.defvjp(lambda x, d: (pltpu.roll(x, d, 0), None), lambda d, _, g: (pltpu.roll(g, g.shape[0] - d, 0),))


def _gdn_prep_fn(prev, cur, ab, w0, w1, w2, w3, a_log, dt_bias):
    xcat = jnp.concatenate([prev, cur], axis=0)
    x0, x1, x2 = [_roll_rows(xcat, GDN_CONV - 1 - j)[GDN_HALO:] for j in range(GDN_CONV - 1)]
    qkv = _silu(x0 * w0 + x1 * w1 + x2 * w2 + cur * w3)
    g_all = -jnp.exp(a_log) * _softplus(ab + dt_bias)
    beta_all = _sigmoid(ab)
    qs, ks, gs, bs = [], [], [], []
    for h in range(GDN_HEADS):
        q = qkv[:, h * 128:(h + 1) * 128]
        k = qkv[:, 512 + h * 128:512 + (h + 1) * 128]
        qs.append(q * lax.rsqrt(jnp.sum(q * q, axis=-1, keepdims=True) + NORM_EPS) * (GDN_DK ** -0.5))
        ks.append(k * lax.rsqrt(jnp.sum(k * k, axis=-1, keepdims=True) + NORM_EPS))
        gs.append(_lane_pick(g_all, h))
        bs.append(_lane_pick(beta_all, GDN_HEADS + h))
    cat = lambda xs: jnp.concatenate(xs, axis=1)
    return cat(qs), cat(ks), qkv[:, 1024:], cat(gs), cat(bs)


GDN_PREP_TILE = 256


def _gdn_prep_specs(s, params, reverse=False):
    t = min(GDN_PREP_TILE, s)
    n = s // t
    blk = (lambda i: n - 1 - i) if reverse else (lambda i: i)
    prev = pl.BlockSpec((GDN_HALO, GDN_QKV), lambda i: (jnp.maximum(blk(i) * (t // GDN_HALO) - 1, 0), 0))
    rows = lambda d: pl.BlockSpec((t, d), lambda i: (blk(i), 0))
    return t, rows, [prev, rows(GDN_QKV), rows(LANES)] + [pl.BlockSpec(p.shape, lambda i: (0, 0)) for p in params]


def _gdn_prep_masked(has_rows_before):
    return lambda prev, *rest: _gdn_prep_fn(prev * has_rows_before, *rest)


def _gdn_prep_fwd(qkv, ab, params):
    s = qkv.shape[0]
    t, rows, in_specs = _gdn_prep_specs(s, params)

    def body(prev_ref, cur_ref, ab_ref, *refs):
        p_refs, o_refs = refs[:len(params)], refs[len(params):]
        has_rows_before = (pl.program_id(0) > 0).astype(F32)
        outs = _gdn_prep_masked(has_rows_before)(prev_ref[...], cur_ref[...], ab_ref[...], *[p[...] for p in p_refs])
        for ref, val in zip(o_refs, outs):
            ref[...] = val

    return pl.pallas_call(
        body, name="gdn_prep", grid=(s // t,), in_specs=in_specs,
        out_specs=[rows(GDN_WIDTH)] * 5, out_shape=[jax.ShapeDtypeStruct((s, GDN_WIDTH), F32)] * 5,
        compiler_params=_cparams(("parallel",)))(qkv, qkv, ab, *params)


def _gdn_prep_bwd(qkv, ab, params, cts):
    s = qkv.shape[0]
    t, rows, in_specs = _gdn_prep_specs(s, params, reverse=True)
    n = s // t
    npar = len(params)

    def body(prev_ref, cur_ref, ab_ref, *refs):
        p_refs, g_refs = refs[:npar], refs[npar:npar + 5]
        dcur_ref, dab_ref = refs[npar + 5:npar + 7]
        dp_refs, carry_sc = refs[npar + 7:-1], refs[-1]

        @pl.when(pl.program_id(0) == 0)
        def _():
            carry_sc[...] = jnp.zeros_like(carry_sc)
            for ref in dp_refs:
                ref[...] = jnp.zeros_like(ref)

        has_rows_before = (pl.program_id(0) < n - 1).astype(F32)
        _, vjp = jax.vjp(_gdn_prep_masked(has_rows_before), prev_ref[...], cur_ref[...], ab_ref[...],
                         *[p[...] for p in p_refs])
        d_prev, d_cur, d_ab, *d_params = vjp(tuple(g[...] for g in g_refs))
        dcur_ref[...] = d_cur
        dcur_ref[t - GDN_HALO:, :] += carry_sc[...]
        carry_sc[...] = d_prev
        dab_ref[...] = d_ab
        for ref, val in zip(dp_refs, d_params):
            ref[...] += val

    res = pl.pallas_call(
        body, name="gdn_prep_bwd", grid=(n,), in_specs=in_specs + [rows(GDN_WIDTH)] * 5,
        out_specs=[rows(GDN_QKV), rows(LANES)] + [pl.BlockSpec(p.shape, lambda i: (0, 0)) for p in params],
        out_shape=[jax.ShapeDtypeStruct((s, GDN_QKV), F32), jax.ShapeDtypeStruct((s, LANES), F32)]
        + [jax.ShapeDtypeStruct(p.shape, F32) for p in params],
        scratch_shapes=[pltpu.VMEM((GDN_HALO, GDN_QKV), F32)],
        compiler_params=_cparams(("arbitrary",)))(qkv, qkv, ab, *params, *cts)
    return res[0], res[1], tuple(res[2:])


@jax.custom_vjp
def gdn_prep(qkv, ab, params):
    return tuple(_gdn_prep_fwd(qkv, ab, params))


def _gdn_prep_vjp_fwd(qkv, ab, params):
    return tuple(_gdn_prep_fwd(qkv, ab, params)), (qkv, ab, params)


def _gdn_prep_vjp_bwd(res, cts):
    qkv, ab, params = res
    return _gdn_prep_bwd(qkv, ab, params, cts)


gdn_prep.defvjp(_gdn_prep_vjp_fwd, _gdn_prep_vjp_bwd)


@jax.custom_vjp
def _unit_lower_inverse(lms):
    c = lms[0].shape[0]
    row = lax.broadcasted_iota(jnp.int32, (c, c), 0)
    col = lax.broadcasted_iota(jnp.int32, (c, c), 1)
    ts = [(row == col).astype(F32) - jnp.where((row >> 1) == (col >> 1), lm, 0.0) for lm in lms]
    for level in range(1, int(math.log2(c))):
        below = ((row >> (level + 1)) == (col >> (level + 1))) & ((row >> level) != (col >> level))
        mids = [_dot_split(t, jnp.where(below, lm, 0.0)) for t, lm in zip(ts, lms)]
        ts = [t - _dot_split(mid, t) for t, mid in zip(ts, mids)]
    return tuple(ts)


def _uli_fwd(lms):
    ts = _unit_lower_inverse(lms)
    return ts, ts


def _uli_bwd(ts, gs):
    mids = [bdot(t, g, "tn") for t, g in zip(ts, gs)]
    return (tuple(-bdot(mid, t, "nt") for t, mid in zip(ts, mids)),)


_unit_lower_inverse.defvjp(_uli_fwd, _uli_bwd)


@jax.custom_vjp
def _known_inverse(lms, ts):
    return ts


_known_inverse.defvjp(lambda lms, ts: (ts, ts),
                      lambda ts, gs: (_uli_bwd(ts, gs)[0], tuple(jnp.zeros_like(t) for t in ts)))


def _gdn_chunk_fn(qs, ks, vs, gbs, bbs, s0s, known_ts=None):
    heads = range(len(qs))
    c = qs[0].shape[0]
    row = lax.broadcasted_iota(jnp.int32, (c, c), 0)
    col = lax.broadcasted_iota(jnp.int32, (c, c), 1)
    incl, strict = row >= col, row > col
    tri = incl.astype(F32)
    gc = [bdot(tri, gbs[h], "nn", True) for h in heads]
    decay = [jnp.exp(jnp.where(incl, gc[h] - gc[h].T, -1e30)) for h in heads]
    g_last = [jnp.sum(gbs[h], axis=0, keepdims=True) for h in heads]
    eg = [jnp.exp(gc[h]) for h in heads]
    kb = [ks[h] * bbs[h] for h in heads]
    lms = tuple(jnp.where(strict, bdot(kb[h], ks[h], "nt") * decay[h], 0.0) for h in heads)
    ts = _unit_lower_inverse(lms) if known_ts is None else _known_inverse(lms, known_ts)
    u = [bdot(ts[h], vs[h] * bbs[h]) for h in heads]
    w = [bdot(ts[h], kb[h] * eg[h]) for h in heads]
    qk = [bdot(qs[h], ks[h], "nt") * decay[h] for h in heads]
    v_new = [u[h] - bdot(w[h], s0s[h]) for h in heads]
    o = [bdot(qs[h] * eg[h], s0s[h]) + bdot(qk[h], v_new[h]) for h in heads]
    s1 = [s0s[h] * jnp.exp(g_last[h]) + bdot(ks[h] * jnp.exp(g_last[h] - gc[h]), v_new[h], "tn") for h in heads]
    return (tuple(o), tuple(s1)), ts


def _head_tiles(ref):
    return tuple(ref[:, h * 128:(h + 1) * 128] for h in range(GDN_HEADS))


def _gdn_fwd(q, k, v, gb, bb):
    s = q.shape[0]
    c = min(GDN_CHUNK, s)
    nc = s // c

    def body(q_ref, k_ref, v_ref, g_ref, b_ref, o_ref, st_ref, inv_ref, s_sc):
        @pl.when(pl.program_id(0) == 0)
        def _():
            s_sc[...] = jnp.zeros_like(s_sc)

        s0s = tuple(s_sc[h] for h in range(GDN_HEADS))
        for h in range(GDN_HEADS):
            st_ref[h, 0] = s0s[h]
        (os, s1s), ts = _gdn_chunk_fn(*[_head_tiles(ref) for ref in (q_ref, k_ref, v_ref, g_ref, b_ref)], s0s)
        for h in range(GDN_HEADS):
            o_ref[:, h * 128:(h + 1) * 128] = os[h]
            inv_ref[h, 0] = ts[h]
            s_sc[h] = s1s[h]

    blk = pl.BlockSpec((c, GDN_WIDTH), lambda n: (n, 0))
    return pl.pallas_call(
        body, name="gdn_fwd", grid=(nc,), in_specs=[blk] * 5,
        out_specs=[blk, pl.BlockSpec((GDN_HEADS, 1, 128, 128), lambda n: (0, n, 0, 0)),
                   pl.BlockSpec((GDN_HEADS, 1, c, c), lambda n: (0, n, 0, 0))],
        out_shape=[jax.ShapeDtypeStruct((s, GDN_WIDTH), F32), jax.ShapeDtypeStruct((GDN_HEADS, nc, 128, 128), F32),
                   jax.ShapeDtypeStruct((GDN_HEADS, nc, c, c), F32)],
        scratch_shapes=[pltpu.VMEM((GDN_HEADS, 128, 128), F32)],
        compiler_params=_cparams(("arbitrary",)))(q, k, v, gb, bb)


def _gdn_bwd(q, k, v, gb, bb, states, inverses, do):
    s = q.shape[0]
    c = min(GDN_CHUNK, s)
    nc = s // c

    def body(q_ref, k_ref, v_ref, g_ref, b_ref, st_ref, inv_ref, do_ref, dq_ref, dk_ref, dv_ref, dg_ref, db_ref, ds_sc):
        @pl.when(pl.program_id(0) == 0)
        def _():
            ds_sc[...] = jnp.zeros_like(ds_sc)

        s0s = tuple(st_ref[h, 0] for h in range(GDN_HEADS))
        ts = tuple(inv_ref[h, 0] for h in range(GDN_HEADS))
        chunk = lambda *args: _gdn_chunk_fn(*args, known_ts=ts)[0]
        _, vjp = jax.vjp(chunk, *[_head_tiles(ref) for ref in (q_ref, k_ref, v_ref, g_ref, b_ref)], s0s)
        *d_tiles, ds0s = vjp((_head_tiles(do_ref), tuple(ds_sc[h] for h in range(GDN_HEADS))))
        for h in range(GDN_HEADS):
            for ref, d in zip((dq_ref, dk_ref, dv_ref, dg_ref, db_ref), d_tiles):
                ref[:, h * 128:(h + 1) * 128] = d[h]
            ds_sc[h] = ds0s[h]

    blk = pl.BlockSpec((c, GDN_WIDTH), lambda n: (nc - 1 - n, 0))
    return pl.pallas_call(
        body, name="gdn_bwd", grid=(nc,),
        in_specs=[blk] * 5 + [pl.BlockSpec((GDN_HEADS, 1, 128, 128), lambda n: (0, nc - 1 - n, 0, 0)),
                              pl.BlockSpec((GDN_HEADS, 1, c, c), lambda n: (0, nc - 1 - n, 0, 0)), blk],
        out_specs=[blk] * 5, out_shape=[jax.ShapeDtypeStruct((s, GDN_WIDTH), F32)] * 5,
        scratch_shapes=[pltpu.VMEM((GDN_HEADS, 128, 128), F32)],
        compiler_params=_cparams(("arbitrary",)))(q, k, v, gb, bb, states, inverses, do)


@jax.custom_vjp
def gdn_core(q, k, v, gb, bb):
    return _gdn_fwd(q, k, v, gb, bb)[0]


def _gdn_core_fwd(q, k, v, gb, bb):
    o, states, inverses = _gdn_fwd(q, k, v, gb, bb)
    return o, (q, k, v, gb, bb, states, inverses)


def _gdn_core_bwd(res, do):
    return tuple(_gdn_bwd(*res, do))


gdn_core.defvjp(_gdn_core_fwd, _gdn_core_bwd)


def _cact_fn(c):
    return (_silu(c),)


def _prenorm_fn(x, w, scale_raw, scale_b, shift_raw, shift_b):
    return (_rms(x, w) * (1.0 + scale_raw + scale_b) + shift_raw + shift_b,)


def _tail_fn(x, o_mla, z_mla, o_gdn, z_gdn, o_norm_w, w_out, post_w, gate_raw, gate_b):
    y_mla = o_mla * _silu(z_mla)
    parts = [_rms(o_gdn[:, h * 128:(h + 1) * 128], o_norm_w) for h in range(GDN_HEADS)]
    y_gdn = jnp.concatenate(parts, axis=1) * _silu(z_gdn)
    y = bdot(y_mla, w_out[:MLA_WIDTH]) + bdot(y_gdn, w_out[MLA_WIDTH:])
    return (x + (gate_raw + gate_b) * _rms(y, post_w),)


def _loss_fn(y, tgt):
    err = y - tgt
    part = jnp.sum(0.5 * jnp.mean(err * err, axis=-1, keepdims=True), axis=0, keepdims=True)
    return (jnp.broadcast_to(part, (1, LANES)),)


_OPS = dict(
    cact=make_rowwise(_cact_fn, "c_act", (D_MODEL,), tile=16),
    prenorm=make_rowwise(_prenorm_fn, "prenorm", (D_MODEL,)),
    tail=make_rowwise(_tail_fn, "tail", (D_MODEL,)),
    loss=make_rowwise(_loss_fn, "loss", (), acc_dims=(LANES,), n_nondiff=1),
    lin_mod=make_linear("lin_mod"), lin_in=make_fan_linear("lin_in"),
)


def _swap_halves(w):
    half = w.shape[-1] // 2
    return jnp.concatenate([w[..., half:], w[..., :half]], axis=-1)


def _w_in_groups(w):
    k_pe = w[:, 640:704]
    ab = jnp.concatenate([w[:, 2752:2760], jnp.zeros((w.shape[0], LANES - 8), w.dtype)], axis=1)
    return (w[:, :384], w[:, 384:640], jnp.concatenate([k_pe, _swap_halves(k_pe)], axis=1), w[:, 704:1216],
            w[:, 1216:2752], ab, w[:, 2760:])


def _q_up_ext(w):
    parts = []
    for h in range(MLA_HEADS):
        rope = w[:, h * 192 + 128:(h + 1) * 192]
        parts += [w[:, h * 192:h * 192 + 128], rope, _swap_halves(rope)]
    return jnp.concatenate(parts, axis=1)


def _kv_up_perm(w):
    ks = [w[:, h * 256:h * 256 + 128] for h in range(MLA_HEADS)]
    vs = [w[:, h * 256 + 128:(h + 1) * 256] for h in range(MLA_HEADS)]
    return jnp.concatenate(ks + vs, axis=1)


def _pad_lanes(v):
    return jnp.pad(v, (0, LANES - v.shape[0]))[None, :]


def _local_loss(weights, x, c, positions, target):
    s = x.shape[0]
    half = MLA_ROPE // 2
    inv_freq = jnp.power(ROPE_THETA, -jnp.arange(half, dtype=F32) * 2.0 / MLA_ROPE)
    ang = positions.astype(F32)[:, None] * inv_freq
    cos, sin, zero = jnp.cos(ang), jnp.sin(ang), jnp.zeros((s, 2 * half), F32)
    t1 = jnp.concatenate([cos, cos, zero], axis=1)
    t2 = jnp.concatenate([-sin, sin, zero], axis=1)

    (c_act,) = _OPS["cact"]((jnp.pad(c, ((0, 15), (0, 0))),), ())
    for l in range(DEPTH):
        mod = _OPS["lin_mod"](c_act, weights["w_mod"][l])[0:1]
        b = weights["b_mod"][l][None, :]
        shift_raw, scale_raw, gate_raw = mod[:, :1024], mod[:, 1024:2048], mod[:, 2048:]
        shift_b, scale_b, gate_b = b[:, :1024], b[:, 1024:2048], b[:, 2048:]
        (h,) = _OPS["prenorm"]((x,), (weights["pre_norm_w"][l][None], scale_raw, scale_b, shift_raw, shift_b))
        q_lat, kv_lat, kr, z_mla, qkv, ab, z_gdn = _OPS["lin_in"]((h,), (_w_in_groups(weights["w_in"][l]),))
        o_mla = mla_attention((q_lat, kv_lat, kr, t1, t2),
                              (weights["mla_q_norm_w"][l][None], _q_up_ext(weights["mla_q_up"][l]),
                               weights["mla_kv_norm_w"][l][None], _kv_up_perm(weights["mla_kv_up"][l])))
        cw = weights["gdn_conv_w"][l]
        params = tuple(cw[j][None] for j in range(GDN_CONV))
        params += (_pad_lanes(weights["gdn_a_log"][l]), _pad_lanes(weights["gdn_dt_bias"][l]))
        o_gdn = gdn_core(*gdn_prep(qkv, ab, params))
        (x,) = _OPS["tail"]((x, o_mla, z_mla, o_gdn, z_gdn),
                            (weights["gdn_o_norm_w"][l][None], weights["w_out"][l], weights["post_norm_w"][l][None],
                             gate_raw, gate_b))
    (acc,) = _OPS["loss"]((x, target), ())
    return acc[0, 0]


def _chip_index():
    return 2 * lax.axis_index("x") + lax.axis_index("y")


def _other_chips(x, y):
    return [(1 - x, y), (x, 1 - y), (1 - x, 1 - y)]


def _any_spec():
    return pl.BlockSpec(memory_space=pl.ANY)


def _half(ref, hc):
    n = ref.shape[0] // 2
    return ref.at[pl.ds(hc * n, n)]


def ag_weights(shards):
    n = len(shards)

    def body(*refs):
        ins, outs = refs[:n], refs[n:2 * n]
        send_sems, recv_sems = refs[2 * n:]
        x, y, c = lax.axis_index("x"), lax.axis_index("y"), lax.axis_index("c")
        sibling = (x, y, 1 - c)
        chips = _other_chips(x, y)

        def copy(t, k, src, chip_xy, hc, to):
            return pltpu.make_async_remote_copy(
                src_ref=src, dst_ref=_half(outs[t].at[2 * chip_xy[0] + chip_xy[1]], hc),
                send_sem=send_sems.at[6 * t + k], recv_sem=recv_sems.at[6 * t + k], device_id=to, device_id_type=MESH)

        first = [copy(t, k, _half(ins[t], c), (x, y), c, (*chip, c)) for k, chip in enumerate(chips) for t in range(n)]
        for cp in first:
            cp.start()
        passed = []
        for k, chip in enumerate(chips):
            for t in range(n):
                landed = _half(outs[t].at[2 * chip[0] + chip[1]], c)
                copy(t, k, landed, chip, c, (x, y, c)).wait_recv()
                passed.append(copy(t, 3 + k, landed, chip, c, sibling))
                passed[-1].start()
        for k, chip in enumerate(chips):
            for t in range(n):
                copy(t, 3 + k, _half(ins[t], c), chip, 1 - c, (x, y, c)).wait_recv()
        for cp in first + passed:
            cp.wait_send()

    return pl.pallas_call(
        body, name="ag_weights", in_specs=[_any_spec()] * n, out_specs=[_any_spec()] * n,
        out_shape=[jax.ShapeDtypeStruct((N_CHIPS,) + a.shape, a.dtype) for a in shards],
        scratch_shapes=[pltpu.SemaphoreType.DMA((6 * n,)), pltpu.SemaphoreType.DMA((6 * n,))],
        compiler_params=pltpu.CompilerParams(has_side_effects=True))(*shards)


def rs_pair(gs):
    n = len(gs)

    def body(*refs):
        ins, outs = refs[:n], refs[n:2 * n]
        send_sems, recv_sems = refs[2 * n:]
        x, y, c = lax.axis_index("x"), lax.axis_index("y"), lax.axis_index("c")
        lh = [g.shape[1] // 2 for g in gs]
        copies = [pltpu.make_async_remote_copy(
            src_ref=ins[t].at[:, pl.ds((1 - c) * lh[t], lh[t])], dst_ref=outs[t], send_sem=send_sems.at[t],
            recv_sem=recv_sems.at[t], device_id=(x, y, 1 - c), device_id_type=MESH) for t in range(n)]
        for cp in copies:
            cp.start()
        for cp in copies:
            cp.wait()

    return pl.pallas_call(
        body, name="rs_pair", in_specs=[_any_spec()] * n, out_specs=[_any_spec()] * n,
        out_shape=[jax.ShapeDtypeStruct((N_CHIPS, g.shape[1] // 2) + g.shape[2:], F32) for g in gs],
        scratch_shapes=[pltpu.SemaphoreType.DMA((n,)), pltpu.SemaphoreType.DMA((n,))],
        compiler_params=pltpu.CompilerParams(has_side_effects=True))(*gs)


def rs_cross(pairs):
    n = len(pairs)

    def body(*refs):
        ins, outs = refs[:n], refs[n:2 * n]
        send_sems, recv_sems = refs[2 * n:]
        x, y, c = lax.axis_index("x"), lax.axis_index("y"), lax.axis_index("c")
        copies = []
        for k, chip in enumerate(_other_chips(x, y)):
            for t in range(n):
                copies.append(pltpu.make_async_remote_copy(
                    src_ref=ins[t].at[2 * chip[0] + chip[1]], dst_ref=outs[t].at[k], send_sem=send_sems.at[3 * t + k],
                    recv_sem=recv_sems.at[3 * t + k], device_id=(*chip, c), device_id_type=MESH))
        for cp in copies:
            cp.start()
        for cp in copies:
            cp.wait()

    return pl.pallas_call(
        body, name="rs_cross", in_specs=[_any_spec()] * n, out_specs=[_any_spec()] * n,
        out_shape=[jax.ShapeDtypeStruct((3,) + p.shape[1:], p.dtype) for p in pairs],
        scratch_shapes=[pltpu.SemaphoreType.DMA((3 * n,)), pltpu.SemaphoreType.DMA((3 * n,))],
        compiler_params=pltpu.CompilerParams(has_side_effects=True))(*pairs)


def rs_share(blocks):
    n = len(blocks)

    def body(*refs):
        ins, outs = refs[:n], refs[n:2 * n]
        send_sems, recv_sems = refs[2 * n:]
        x, y, c = lax.axis_index("x"), lax.axis_index("y"), lax.axis_index("c")
        sends = [pltpu.make_async_remote_copy(
            src_ref=_half(ins[t], c), dst_ref=_half(outs[t], c), send_sem=send_sems.at[t], recv_sem=recv_sems.at[t],
            device_id=(x, y, 1 - c), device_id_type=MESH) for t in range(n)]
        for cp in sends:
            cp.start()
        for t in range(n):
            pltpu.make_async_remote_copy(
                src_ref=_half(ins[t], c), dst_ref=_half(outs[t], 1 - c), send_sem=send_sems.at[t],
                recv_sem=recv_sems.at[t], device_id=(x, y, 1 - c), device_id_type=MESH).wait_recv()
        for cp in sends:
            cp.wait_send()

    return pl.pallas_call(
        body, name="rs_share", in_specs=[_any_spec()] * n, out_specs=[_any_spec()] * n,
        out_shape=[jax.ShapeDtypeStruct(b.shape, F32) for b in blocks],
        input_output_aliases={t: t for t in range(n)},
        scratch_shapes=[pltpu.SemaphoreType.DMA((n,)), pltpu.SemaphoreType.DMA((n,))],
        compiler_params=pltpu.CompilerParams(has_side_effects=True))(*blocks)


def _row_tile(rows, target):
    best = None
    for t in range(8, min(rows, target) + 1, 8):
        if rows % t == 0:
            best = t
    return rows if best is None else best


TILE_BYTES = 2 * 1024 * 1024


def _flat_rows(shape):
    rows = int(np.prod(shape[1:-1]))
    cols_padded = -(-shape[-1] // LANES) * LANES
    return rows, _row_tile(rows, max(8, TILE_BYTES // (4 * cols_padded)))


def pair_add(g, from_sibling, out_dtype):
    cols = g.shape[-1]
    rph, t = _flat_rows(from_sibling.shape)
    nt = rph // t
    c_arr = lax.axis_index("c").astype(jnp.int32).reshape(1)

    def body(c_ref, a_ref, b_ref, o_ref):
        o_ref[...] = (a_ref[...] + b_ref[...]).astype(o_ref.dtype)

    out = pl.pallas_call(
        body, name="pair_add",
        grid_spec=pltpu.PrefetchScalarGridSpec(
            num_scalar_prefetch=1, grid=(N_CHIPS, nt),
            in_specs=[pl.BlockSpec((t, cols), lambda j, i, c_ref: (j * 2 * nt + c_ref[0] * nt + i, 0)),
                      pl.BlockSpec((t, cols), lambda j, i, c_ref: (j * nt + i, 0))],
            out_specs=pl.BlockSpec((t, cols), lambda j, i, c_ref: (j * nt + i, 0))),
        out_shape=jax.ShapeDtypeStruct((N_CHIPS * rph, cols), out_dtype),
        compiler_params=_cparams(("parallel", "parallel")))(c_arr, g.reshape(-1, cols), from_sibling.reshape(-1, cols))
    return out.reshape(from_sibling.shape)


def chip_add(pairs, received):
    cols = pairs.shape[-1]
    rph, t = _flat_rows(pairs.shape)
    nt = rph // t
    j_arr = _chip_index().astype(jnp.int32).reshape(1)
    c_arr = lax.axis_index("c").astype(jnp.int32).reshape(1)
    r2 = received.reshape(-1, cols)

    def body(j_ref, c_ref, a_ref, r0_ref, r1_ref, r2_ref, o_ref):
        a, r0, r1, r2 = [ref[...].astype(F32) for ref in (a_ref, r0_ref, r1_ref, r2_ref)]
        o_ref[...] = (a + r0) + (r1 + r2)

    out = pl.pallas_call(
        body, name="chip_add",
        grid_spec=pltpu.PrefetchScalarGridSpec(
            num_scalar_prefetch=2, grid=(nt,),
            in_specs=[pl.BlockSpec((t, cols), lambda i, j_ref, c_ref: (j_ref[0] * nt + i, 0)),
                      pl.BlockSpec((t, cols), lambda i, j_ref, c_ref: (i, 0)),
                      pl.BlockSpec((t, cols), lambda i, j_ref, c_ref: (nt + i, 0)),
                      pl.BlockSpec((t, cols), lambda i, j_ref, c_ref: (2 * nt + i, 0))],
            out_specs=pl.BlockSpec((t, cols), lambda i, j_ref, c_ref: (c_ref[0] * nt + i, 0))),
        out_shape=jax.ShapeDtypeStruct((2 * rph, cols), F32),
        compiler_params=_cparams(("parallel",)))(j_arr, c_arr, pairs.reshape(-1, cols), r2, r2, r2)
    return out.reshape((2 * pairs.shape[1],) + pairs.shape[2:])


def reduce_grads(gs, cross_dtypes):
    pairs = [pair_add(g, r, dt) for g, r, dt in zip(gs, rs_pair(gs), cross_dtypes)]
    return rs_share([chip_add(p, r) for p, r in zip(pairs, rs_cross(pairs))])


def adamw(w, g, m, v):
    shape = w.shape
    cols = shape[-1]
    rows = int(np.prod(shape[:-1]))
    flat = lambda a: a.reshape(rows, cols)
    t = _row_tile(rows, 256)

    def body(w_ref, g_ref, m_ref, v_ref, d_ref, mo_ref, vo_ref):
        gv = g_ref[...]
        m_new = ADAM_B1 * m_ref[...] + (1.0 - ADAM_B1) * gv
        v_new = ADAM_B2 * v_ref[...] + (1.0 - ADAM_B2) * (gv * gv)
        m_hat = m_new / (1.0 - ADAM_B1 ** ADAM_STEP)
        v_hat = v_new / (1.0 - ADAM_B2 ** ADAM_STEP)
        d_ref[...] = -ADAM_LR * (m_hat / (jnp.sqrt(v_hat) + ADAM_EPS) + ADAM_WD * w_ref[...])
        mo_ref[...] = m_new
        vo_ref[...] = v_new

    spec = pl.BlockSpec((t, cols), lambda i: (i, 0))
    outs = pl.pallas_call(
        body, name="adamw", grid=(rows // t,), in_specs=[spec] * 4, out_specs=[spec] * 3,
        out_shape=[jax.ShapeDtypeStruct((rows, cols), F32)] * 3,
        compiler_params=_cparams(("parallel",)))(flat(w), flat(g), flat(m), flat(v))
    return tuple(o.reshape(shape) for o in outs)


SHARDED = (("w_mod", 2), ("w_in", 2), ("mla_q_up", 2), ("mla_kv_up", 2), ("gdn_conv_w", 2), ("w_out", 1))
REPLICATED = ("b_mod", "pre_norm_w", "post_norm_w", "mla_q_norm_w", "mla_kv_norm_w", "gdn_a_log", "gdn_dt_bias",
              "gdn_o_norm_w")
WEIGHT_ORDER = ("w_mod", "b_mod", "pre_norm_w", "post_norm_w", "w_in", "mla_q_norm_w", "mla_q_up", "mla_kv_norm_w",
                "mla_kv_up", "gdn_conv_w", "gdn_a_log", "gdn_dt_bias", "gdn_o_norm_w", "w_out")
EXACT_F32 = ("gdn_conv_w",)
SMALL_ROWS = 48


def _gather_weights(shards):
    names = [name for name, _ in SHARDED]
    own = [shards[n] if n in EXACT_F32 else shards[n].astype(BF16) for n in names]
    gathered = ag_weights(own)
    full = {}
    for (name, axis), blk, mine in zip(SHARDED, gathered, own):
        shp = shards[name].shape
        blk = lax.dynamic_update_index_in_dim(blk, mine, _chip_index(), 0)
        blk = jnp.moveaxis(blk.astype(F32), 0, axis)
        full[name] = blk.reshape(shp[:axis] + (N_CHIPS * shp[axis],) + shp[axis + 1:])
    return full


def _split_grads(grads):
    pieces = []
    for name, axis in SHARDED:
        g = grads[name]
        shp = g.shape
        g = g.reshape(shp[:axis] + (N_CHIPS, shp[axis] // N_CHIPS) + shp[axis + 1:])
        pieces.append(jnp.moveaxis(g, axis, 0))
    small = jnp.concatenate([grads[name] for name in REPLICATED], axis=1)
    small = jnp.pad(small, ((0, 0), (0, SMALL_ROWS * LANES - small.shape[1]))).reshape(DEPTH, SMALL_ROWS, LANES)
    pieces.append(jnp.broadcast_to(small[None], (N_CHIPS,) + small.shape))
    return pieces


def _unsplit_small(small, rep_shapes):
    flat = small.reshape(DEPTH, SMALL_ROWS * LANES)
    out, off = {}, 0
    for name in REPLICATED:
        size = rep_shapes[name][1]
        out[name] = flat[:, off:off + size]
        off += size
    return out


def kernel(x, c, positions, w_mod, b_mod, pre_norm_w, post_norm_w, w_in, mla_q_norm_w, mla_q_up, mla_kv_norm_w, mla_kv_up, gdn_conv_w, gdn_a_log, gdn_dt_bias, gdn_o_norm_w, w_out, loss_target, m_w_mod, m_b_mod, m_pre_norm_w, m_post_norm_w, m_w_in, m_mla_q_norm_w, m_mla_q_up, m_mla_kv_norm_w, m_mla_kv_up, m_gdn_conv_w, m_gdn_a_log, m_gdn_dt_bias, m_gdn_o_norm_w, m_w_out, v_w_mod, v_b_mod, v_pre_norm_w, v_post_norm_w, v_w_in, v_mla_q_norm_w, v_mla_q_up, v_mla_kv_norm_w, v_mla_kv_up, v_gdn_conv_w, v_gdn_a_log, v_gdn_dt_bias, v_gdn_o_norm_w, v_w_out):
    given = dict(w_mod=w_mod, b_mod=b_mod, pre_norm_w=pre_norm_w, post_norm_w=post_norm_w, w_in=w_in,
                 mla_q_norm_w=mla_q_norm_w, mla_q_up=mla_q_up, mla_kv_norm_w=mla_kv_norm_w, mla_kv_up=mla_kv_up,
                 gdn_conv_w=gdn_conv_w, gdn_a_log=gdn_a_log, gdn_dt_bias=gdn_dt_bias, gdn_o_norm_w=gdn_o_norm_w,
                 w_out=w_out)
    moments_m = dict(w_mod=m_w_mod, b_mod=m_b_mod, pre_norm_w=m_pre_norm_w, post_norm_w=m_post_norm_w, w_in=m_w_in,
                     mla_q_norm_w=m_mla_q_norm_w, mla_q_up=m_mla_q_up, mla_kv_norm_w=m_mla_kv_norm_w,
                     mla_kv_up=m_mla_kv_up, gdn_conv_w=m_gdn_conv_w, gdn_a_log=m_gdn_a_log,
                     gdn_dt_bias=m_gdn_dt_bias, gdn_o_norm_w=m_gdn_o_norm_w, w_out=m_w_out)
    moments_v = dict(w_mod=v_w_mod, b_mod=v_b_mod, pre_norm_w=v_pre_norm_w, post_norm_w=v_post_norm_w, w_in=v_w_in,
                     mla_q_norm_w=v_mla_q_norm_w, mla_q_up=v_mla_q_up, mla_kv_norm_w=v_mla_kv_norm_w,
                     mla_kv_up=v_mla_kv_up, gdn_conv_w=v_gdn_conv_w, gdn_a_log=v_gdn_a_log,
                     gdn_dt_bias=v_gdn_dt_bias, gdn_o_norm_w=v_gdn_o_norm_w, w_out=v_w_out)

    full = _gather_weights({name: given[name] for name, _ in SHARDED})
    for name in REPLICATED:
        full[name] = given[name]
    loss_local, (grads, grad_x) = jax.value_and_grad(_local_loss, argnums=(0, 1))(
        full, x[0], c, positions[0], loss_target[0])
    loss = lax.psum(loss_local, AXES)

    cross_dtypes = [F32 if name in EXACT_F32 else BF16 for name, _ in SHARDED] + [F32]
    reduced = reduce_grads(_split_grads(grads), cross_dtypes)
    grad_w = {name: g for (name, _), g in zip(SHARDED, reduced)}
    grad_w.update(_unsplit_small(reduced[-1], {name: given[name].shape for name in REPLICATED}))
    delta, new_m, new_v = {}, {}, {}
    for name in WEIGHT_ORDER:
        delta[name], new_m[name], new_v[name] = adamw(given[name], grad_w[name], moments_m[name], moments_v[name])
    return (loss, grad_x[None], *[grad_w[n] for n in WEIGHT_ORDER], *[delta[n] for n in WEIGHT_ORDER],
            *[new_m[n] for n in WEIGHT_ORDER], *[new_v[n] for n in WEIGHT_ORDER])
```

```python
import functools
import math

import numpy as np
import jax
import jax.numpy as jnp
from jax import lax
from jax.experimental import pallas as pl
from jax.experimental.pallas import tpu as pltpu

F32 = jnp.float32
BF16 = jnp.bfloat16
MESH = pl.DeviceIdType.MESH
AXES = ("x", "y", "c")

D_MODEL = 1024
DEPTH = 4
MLA_HEADS = 4
MLA_NOPE = 128
MLA_ROPE = 64
MLA_V = 128
MLA_Q_RANK = 384
MLA_KV_RANK = 256
MLA_WIDTH = 512
GDN_HEADS = 4
GDN_DK = 128
GDN_WIDTH = 512
GDN_QKV = 1536
GDN_CONV = 4
IN_COLS = 3272
ROPE_THETA = 10000.0
NORM_EPS = 1e-6
ADAM_LR, ADAM_B1, ADAM_B2, ADAM_EPS, ADAM_WD, ADAM_STEP = 0.001, 0.9, 0.999, 1e-08, 0.01, 10

LANES = 128
N_CHIPS = 4
GDN_CHUNK = 128
VMEM_LIMIT = 56 * 1024 * 1024


def _cparams(sem=None):
    if sem is None:
        return pltpu.CompilerParams(vmem_limit_bytes=VMEM_LIMIT)
    return pltpu.CompilerParams(dimension_semantics=sem, vmem_limit_bytes=VMEM_LIMIT)


def _pick(dim, target):
    if dim <= target:
        return dim
    best = None
    for t in range(LANES, target + 1, LANES):
        if dim % t == 0:
            best = t
    assert best is not None, (dim, target)
    return best


_DN = {"nn": (((1,), (0,)), ((), ())), "nt": (((1,), (1,)), ((), ())), "tn": (((0,), (0,)), ((), ()))}


def _dot_raw(a, b, mode, exact):
    if exact:
        return lax.dot_general(a, b, _DN[mode], precision=lax.Precision.HIGHEST, preferred_element_type=F32)
    return lax.dot_general(a.astype(BF16), b.astype(BF16), _DN[mode], preferred_element_type=F32)


def _dot_split(a, b):
    a_hi, b_hi = a.astype(BF16), b.astype(BF16)
    a_lo, b_lo = (a - a_hi.astype(F32)).astype(BF16), (b - b_hi.astype(F32)).astype(BF16)
    dot = lambda u, w: lax.dot_general(u, w, _DN["nn"], preferred_element_type=F32)
    return dot(a_hi, b_hi) + (dot(a_hi, b_lo) + dot(a_lo, b_hi))


@functools.partial(jax.custom_vjp, nondiff_argnums=(2, 3))
def bdot(a, b, mode="nn", exact=False):
    return _dot_raw(a, b, mode, exact)


def _bdot_fwd(a, b, mode, exact):
    return _dot_raw(a, b, mode, exact), (a, b)


def _bdot_bwd(mode, exact, res, g):
    a, b = res
    if mode == "nn":
        return bdot(g, b, "nt", exact), bdot(a, g, "tn", exact)
    if mode == "nt":
        return bdot(g, b, "nn", exact), bdot(g, a, "tn", exact)
    return bdot(b, g, "nt", exact), bdot(a, g, "nn", exact)


bdot.defvjp(_bdot_fwd, _bdot_bwd)


@jax.custom_vjp
def roll_half(x):
    return pltpu.roll(x, 64, 1)


roll_half.defvjp(lambda x: (pltpu.roll(x, 64, 1), None), lambda _, g: (pltpu.roll(g, 64, 1),))


def _sigmoid(x):
    return 1.0 / (1.0 + jnp.exp(-x))


def _silu(x):
    return x * _sigmoid(x)


def _softplus(x):
    return jnp.maximum(x, 0.0) + jnp.log(1.0 + jnp.exp(-jnp.abs(x)))


def _rms(x, w):
    return x * lax.rsqrt(jnp.mean(x * x, axis=-1, keepdims=True) + NORM_EPS) * w


def _rw_fwd(fn, name, rows, params, out_dims, out_dtypes, acc_dims, tile):
    s = rows[0].shape[0]
    t = min(tile, s)
    n = s // t
    nr, npar, no, na = len(rows), len(params), len(out_dims), len(acc_dims)

    def body(*refs):
        r, p = refs[:nr], refs[nr:nr + npar]
        o, a = refs[nr + npar:nr + npar + no], refs[nr + npar + no:]
        outs = fn(*[x[...] for x in r], *[x[...] for x in p])
        for ref, val in zip(o, outs[:no]):
            ref[...] = val.astype(ref.dtype)
        if na:
            @pl.when(pl.program_id(0) == 0)
            def _():
                for ref in a:
                    ref[...] = jnp.zeros_like(ref)
            for ref, val in zip(a, outs[no:]):
                ref[...] += val

    in_specs = [pl.BlockSpec((t, x.shape[1]), lambda i: (i, 0)) for x in rows]
    in_specs += [pl.BlockSpec(x.shape, lambda i: (0, 0)) for x in params]
    out_specs = [pl.BlockSpec((t, d), lambda i: (i, 0)) for d in out_dims]
    out_specs += [pl.BlockSpec((1, d), lambda i: (0, 0)) for d in acc_dims]
    out_shape = [jax.ShapeDtypeStruct((s, d), dt) for d, dt in zip(out_dims, out_dtypes)]
    out_shape += [jax.ShapeDtypeStruct((1, d), F32) for d in acc_dims]
    res = pl.pallas_call(body, name=name, grid=(n,), in_specs=in_specs, out_specs=out_specs, out_shape=out_shape,
                         compiler_params=_cparams(("arbitrary",)))(*rows, *params)
    return tuple(res)


def _rw_bwd(fn, name, rows, params, row_cts, acc_cts, n_diff, tile):
    s = rows[0].shape[0]
    t = min(tile, s)
    n = s // t
    nr, npar, no, na = len(rows), len(params), len(row_cts), len(acc_cts)

    def body(*refs):
        r, p = refs[:nr], refs[nr:nr + npar]
        g, ga = refs[nr + npar:nr + npar + no], refs[nr + npar + no:nr + npar + no + na]
        dr, dp = refs[nr + npar + no + na:nr + npar + no + na + n_diff], refs[nr + npar + no + na + n_diff:]
        _, vjp = jax.vjp(fn, *[x[...] for x in r], *[x[...] for x in p])
        cts = vjp(tuple([x[...] for x in g] + [x[...] for x in ga]))
        for ref, val in zip(dr, cts[:n_diff]):
            ref[...] = val
        if npar:
            @pl.when(pl.program_id(0) == 0)
            def _():
                for ref in dp:
                    ref[...] = jnp.zeros_like(ref)
            for ref, val in zip(dp, cts[nr:]):
                ref[...] += val

    in_specs = [pl.BlockSpec((t, x.shape[1]), lambda i: (i, 0)) for x in rows]
    in_specs += [pl.BlockSpec(x.shape, lambda i: (0, 0)) for x in params]
    in_specs += [pl.BlockSpec((t, x.shape[1]), lambda i: (i, 0)) for x in row_cts]
    in_specs += [pl.BlockSpec(x.shape, lambda i: (0, 0)) for x in acc_cts]
    out_specs = [pl.BlockSpec((t, x.shape[1]), lambda i: (i, 0)) for x in rows[:n_diff]]
    out_specs += [pl.BlockSpec(x.shape, lambda i: (0, 0)) for x in params]
    out_shape = [jax.ShapeDtypeStruct(x.shape, F32) for x in rows[:n_diff]]
    out_shape += [jax.ShapeDtypeStruct(x.shape, F32) for x in params]
    res = pl.pallas_call(body, name=name, grid=(n,), in_specs=in_specs, out_specs=out_specs, out_shape=out_shape,
                         compiler_params=_cparams(("arbitrary",)))(*rows, *params, *row_cts, *acc_cts)
    return tuple(res[:n_diff]), tuple(res[n_diff:])


def make_rowwise(fn, name, out_dims, acc_dims=(), n_nondiff=0, tile=256):
    out_dtypes = (F32,) * len(out_dims)

    @jax.custom_vjp
    def op(rows, params):
        return _rw_fwd(fn, name, rows, params, out_dims, out_dtypes, acc_dims, tile)

    def fwd(rows, params):
        return op(rows, params), (rows, params)

    def bwd(res, cts):
        rows, params = res
        n_diff = len(rows) - n_nondiff
        d_rows, d_params = _rw_bwd(fn, name + "_bwd", rows, params, cts[:len(out_dims)], cts[len(out_dims):],
                                   n_diff, tile)
        d_rows = d_rows + tuple(jnp.zeros_like(x) for x in rows[n_diff:])
        return d_rows, d_params

    op.defvjp(fwd, bwd)
    return op


def _mm(a, b, mode, name):
    if mode == "nn":
        (m, k), (_, n) = a.shape, b.shape
    elif mode == "nt":
        (m, k), (n, _) = a.shape, b.shape
    else:
        (k, m), (_, n) = a.shape, b.shape
    tm = _pick(m, 512)
    tn = _pick(n, 1152)
    tk = _pick(k, 1152) if mode != "tn" else _pick(k, 512)
    nk = k // tk

    def body(a_ref, b_ref, o_ref, acc_ref):
        kk = pl.program_id(2)

        @pl.when(kk == 0)
        def _():
            acc_ref[...] = jnp.zeros_like(acc_ref)

        acc_ref[...] += _dot_raw(a_ref[...], b_ref[...], mode, False)

        @pl.when(kk == nk - 1)
        def _():
            o_ref[...] = acc_ref[...]

    if mode == "nn":
        a_spec = pl.BlockSpec((tm, tk), lambda i, j, kk: (i, kk))
        b_spec = pl.BlockSpec((tk, tn), lambda i, j, kk: (kk, j))
    elif mode == "nt":
        a_spec = pl.BlockSpec((tm, tk), lambda i, j, kk: (i, kk))
        b_spec = pl.BlockSpec((tn, tk), lambda i, j, kk: (j, kk))
    else:
        a_spec = pl.BlockSpec((tk, tm), lambda i, j, kk: (kk, i))
        b_spec = pl.BlockSpec((tk, tn), lambda i, j, kk: (kk, j))
    return pl.pallas_call(
        body, name=name, grid=(m // tm, n // tn, nk), in_specs=[a_spec, b_spec],
        out_specs=pl.BlockSpec((tm, tn), lambda i, j, kk: (i, j)),
        out_shape=jax.ShapeDtypeStruct((m, n), F32), scratch_shapes=[pltpu.VMEM((tm, tn), F32)],
        compiler_params=_cparams(("parallel", "parallel", "arbitrary")))(a, b)


def make_linear(name):
    @jax.custom_vjp
    def op(a, w):
        return _mm(a, w.astype(BF16), "nn", name)

    def fwd(a, w):
        w16 = w.astype(BF16)
        return _mm(a, w16, "nn", name), (a, w16)

    def bwd(res, g):
        a, w16 = res
        return _mm(g, w16, "nt", name + "_dx"), _mm(a, g, "tn", name + "_dw")

    op.defvjp(fwd, bwd)
    return op


def _mla_prep_fn(qraw, kvraw, kr, t1, t2):
    kr_rot = kr * t1 + roll_half(kr) * t2
    qs, ks = [], []
    for h in range(MLA_HEADS):
        q_r = qraw[:, h * 256 + 128:(h + 1) * 256]
        qs += [qraw[:, h * 256:h * 256 + 128], q_r * t1 + roll_half(q_r) * t2]
        ks += [kvraw[:, h * 128:(h + 1) * 128], kr_rot]
    return jnp.concatenate(qs, axis=1), jnp.concatenate(ks, axis=1), kvraw[:, 512:]


def _flash_tile(s):
    return 512 if s >= 2048 else 128


FLASH_SCALE = (MLA_NOPE + MLA_ROPE) ** -0.5
LOG2_E = 1.4426950408889634
FLASH_EXP2 = FLASH_SCALE * LOG2_E
STAT_ROWS = 8
FLASH_PAIR = 2
FLASH_STRIP = 32


def _as_rows(col_b):
    ones = jnp.full((STAT_ROWS, LANES), 1.0 / LANES, F32)
    return _dot_raw(ones, col_b, "nt", True)


def _flash_fwd(qf, kf, vf):
    s = qf.shape[0]
    t = _flash_tile(s)
    nb = s // t
    pair = range(FLASH_PAIR)

    ck = min(FLASH_STRIP, t)

    def body(q_ref, k_ref, v_ref, o_ref, lse_ref, m_sc, l_sc, acc_sc, st_sc, pt_sc):
        i = pl.program_id(1)
        m_sc[...] = jnp.full_like(m_sc, -1e30)
        l_sc[...] = jnp.zeros_like(l_sc)
        acc_sc[...] = jnp.zeros_like(acc_sc)
        qs = [q_ref[:, hh * 256:(hh + 1) * 256] for hh in pair]

        def step(j, on_diagonal):
            rows = pl.ds(pl.multiple_of(j * t, t), t)
            for hh in pair:
                st = _dot_raw(k_ref[rows, hh * 256:(hh + 1) * 256], qs[hh], "nt", False)
                if on_diagonal:
                    keep = lax.broadcasted_iota(jnp.int32, (t, t), 0) <= lax.broadcasted_iota(jnp.int32, (t, t), 1)
                    st = jnp.where(keep, st, -1e30)
                st_sc[hh] = st
            m_olds = [m_sc[hh] for hh in pair]
            m_news = [jnp.maximum(m_olds[hh], jnp.max(st_sc[hh], axis=0, keepdims=True)) for hh in pair]
            alphas = [jnp.exp2((m_olds[hh] - m_news[hh]) * FLASH_EXP2) for hh in pair]
            shifts = [m_news[hh] * FLASH_EXP2 for hh in pair]
            sums = [jnp.zeros((8, t), F32) for _ in pair]
            for hh in pair:
                for r in range(0, t, ck):
                    p = jnp.exp2(st_sc[hh, r:r + ck, :] * FLASH_EXP2 - shifts[hh])
                    pt_sc[hh, r:r + ck, :] = p.astype(BF16)
                    sums[hh] = sums[hh] + functools.reduce(lambda a, b: a + b, [p[u:u + 8] for u in range(0, ck, 8)])
            pvs = [_dot_raw(v_ref[rows, hh * 128:(hh + 1) * 128], pt_sc[hh], "tn", False) for hh in pair]
            for hh in pair:
                l_sc[hh] = alphas[hh] * l_sc[hh] + jnp.sum(sums[hh], axis=0, keepdims=True)
                acc_sc[hh] = alphas[hh] * acc_sc[hh] + pvs[hh]
                m_sc[hh] = m_news[hh]

        def two_steps(p, carry):
            step(2 * p, False)
            step(2 * p + 1, False)
            return carry

        lax.fori_loop(0, i // 2, two_steps, 0)

        @pl.when(i % 2 == 1)
        def _():
            step(i - 1, False)

        step(i, True)
        for hh in pair:
            o_ref[:, hh * 128:(hh + 1) * 128] = (acc_sc[hh] / l_sc[hh]).T
            lse2 = m_sc[hh] * FLASH_EXP2 + jnp.log(l_sc[hh]) * LOG2_E
            lse_ref[hh] = jnp.broadcast_to(lse2, (STAT_ROWS, t))

    p = FLASH_PAIR
    return pl.pallas_call(
        body, name="flash_fwd", grid=(MLA_HEADS // p, nb),
        in_specs=[pl.BlockSpec((t, p * 256), lambda h, i: (i, h)), pl.BlockSpec((s, p * 256), lambda h, i: (0, h)),
                  pl.BlockSpec((s, p * 128), lambda h, i: (0, h))],
        out_specs=[pl.BlockSpec((t, p * 128), lambda h, i: (i, h)),
                   pl.BlockSpec((p, STAT_ROWS, t), lambda h, i: (h, 0, i))],
        out_shape=[jax.ShapeDtypeStruct((s, MLA_WIDTH), F32),
                   jax.ShapeDtypeStruct((MLA_HEADS, STAT_ROWS, s), F32)],
        scratch_shapes=[pltpu.VMEM((p, 1, t), F32), pltpu.VMEM((p, 1, t), F32), pltpu.VMEM((p, 128, t), F32),
                        pltpu.VMEM((p, t, t), F32), pltpu.VMEM((p, t, t), BF16)],
        compiler_params=_cparams(("parallel", "arbitrary")))(qf, kf, vf)


def _flash_bwd_prep(o, do):
    s = o.shape[0]
    t = _flash_tile(s)

    def body(o_ref, do_ref, dl_ref, do16_ref):
        do_t = do_ref[...]
        delta = jnp.sum(do_t * o_ref[...], axis=1, keepdims=True)
        dl_ref[0] = _as_rows(jnp.broadcast_to(delta, (t, LANES)))
        do16_ref[...] = do_t.astype(BF16)

    blk = pl.BlockSpec((t, 128), lambda h, i: (i, h))
    return pl.pallas_call(
        body, name="flash_bwd_prep", grid=(MLA_HEADS, s // t), in_specs=[blk, blk],
        out_specs=[pl.BlockSpec((1, STAT_ROWS, t), lambda h, i: (h, 0, i)), blk],
        out_shape=[jax.ShapeDtypeStruct((MLA_HEADS, STAT_ROWS, s), F32), jax.ShapeDtypeStruct((s, MLA_WIDTH), BF16)],
        compiler_params=_cparams(("parallel", "parallel")))(o, do)


def _flash_bwd(qf, kf, vf, lse, delta, do16):
    s = qf.shape[0]
    t = _flash_tile(s)
    nb = s // t

    def body(q_ref, k_ref, v_ref, lse_ref, dl_ref, do_ref, dq_ref, dk_ref, dv_ref):
        j = pl.program_id(1)

        @pl.when(j == 0)
        def _():
            dq_ref[...] = jnp.zeros_like(dq_ref)

        dk_ref[...] = jnp.zeros_like(dk_ref)
        dv_ref[...] = jnp.zeros_like(dv_ref)
        k = k_ref[...]
        v = v_ref[...]

        def step(i, on_diagonal):
            rows = pl.ds(pl.multiple_of(i * t, t), t)
            q = q_ref[rows, :]
            do_t = do_ref[rows, :]
            st = _dot_raw(k, q, "nt", False) * FLASH_EXP2 - lse_ref[0, 0:1, rows]
            if on_diagonal:
                keep = lax.broadcasted_iota(jnp.int32, (t, t), 0) <= lax.broadcasted_iota(jnp.int32, (t, t), 1)
                st = jnp.where(keep, st, -1e30)
            pt = jnp.exp2(st)
            dst = pt * (_dot_raw(v, do_t, "nt", False) - dl_ref[0, 0:1, rows])
            dv_ref[...] += _dot_raw(pt, do_t, "nn", False)
            dk_ref[...] += _dot_raw(dst, q, "nn", False)
            dq_ref[rows, :] += _dot_raw(dst, k, "tn", False)

        def two_steps(p, carry):
            step(j + 1 + 2 * p, False)
            step(j + 2 + 2 * p, False)
            return carry

        step(j, True)
        below = nb - 1 - j
        lax.fori_loop(0, below // 2, two_steps, 0)

        @pl.when(below % 2 == 1)
        def _():
            step(nb - 1, False)

        dk_ref[...] *= FLASH_SCALE

        @pl.when(j == nb - 1)
        def _():
            dq_ref[...] *= FLASH_SCALE

    stat = pl.BlockSpec((1, STAT_ROWS, s), lambda h, j: (h, 0, 0))
    return pl.pallas_call(
        body, name="flash_bwd", grid=(MLA_HEADS, nb),
        in_specs=[pl.BlockSpec((s, 256), lambda h, j: (0, h)), pl.BlockSpec((t, 256), lambda h, j: (j, h)),
                  pl.BlockSpec((t, 128), lambda h, j: (j, h)), stat, stat, pl.BlockSpec((s, 128), lambda h, j: (0, h))],
        out_specs=[pl.BlockSpec((s, 256), lambda h, j: (0, h)), pl.BlockSpec((t, 256), lambda h, j: (j, h)),
                   pl.BlockSpec((t, 128), lambda h, j: (j, h))],
        out_shape=[jax.ShapeDtypeStruct((s, 1024), F32), jax.ShapeDtypeStruct((s, 1024), F32),
                   jax.ShapeDtypeStruct((s, MLA_WIDTH), F32)],
        compiler_params=_cparams(("parallel", "arbitrary")))(qf, kf, vf, lse, delta, do16)


def _mla_front_fn(q_lat, kv_lat, kr, t1, t2, q_norm_w, q_up, kv_norm_w, kv_up):
    qraw = bdot(_rms(q_lat, q_norm_w), q_up)
    kvraw = bdot(_rms(kv_lat, kv_norm_w), kv_up)
    return _mla_prep_fn(qraw, kvraw, kr, t1, t2)


def _mla_front(rows, params):
    return _rw_fwd(_mla_front_fn, "mla_front", rows, params, (1024, 1024, 512), (BF16, BF16, BF16), (), 256)


@jax.custom_vjp
def mla_attention(rows, params):
    return _flash_fwd(*_mla_front(rows, params))[0]


def _mla_attention_fwd(rows, params):
    qf, kf, vf = _mla_front(rows, params)
    o, lse = _flash_fwd(qf, kf, vf)
    return o, (rows, params, qf, kf, vf, o, lse)


def _mla_attention_bwd(res, do):
    rows, params, qf, kf, vf, o, lse = res
    delta, do16 = _flash_bwd_prep(o, do)
    d_rows, d_params = _rw_bwd(_mla_front_fn, "mla_front_bwd", rows, params, _flash_bwd(qf, kf, vf, lse, delta, do16),
                               (), 3, 256)
    return d_rows + (jnp.zeros_like(rows[3]), jnp.zeros_like(rows[4])), d_params


mla_attention.defvjp(_mla_attention_fwd, _mla_attention_bwd)


def _lane_pick(x, lane):
    ids = lax.broadcasted_iota(jnp.int32, x.shape, 1)
    col = jnp.sum(jnp.where(ids == lane, x, 0.0), axis=1, keepdims=True)
    return jnp.broadcast_to(col, x.shape)


GDN_HALO = 8


@functools.partial(jax.custom_vjp, nondiff_argnums=(1,))
def _roll_rows(x, d):
    return pltpu.roll(x, d, 0)


_roll_rows.defvjp(lambda x, d: (pltpu.roll(x, d, 0), None), lambda d, _, g: (pltpu.roll(g, g.shape[0] - d, 0),))


def _gdn_prep_fn(prev, cur, ab, w0, w1, w2, w3, a_log, dt_bias):
    xcat = jnp.concatenate([prev, cur], axis=0)
    x0, x1, x2 = [_roll_rows(xcat, GDN_CONV - 1 - j)[GDN_HALO:] for j in range(GDN_CONV - 1)]
    qkv = _silu(x0 * w0 + x1 * w1 + x2 * w2 + cur * w3)
    g_all = -jnp.exp(a_log) * _softplus(ab + dt_bias)
    beta_all = _sigmoid(ab)
    qs, ks, gs, bs = [], [], [], []
    for h in range(GDN_HEADS):
        q = qkv[:, h * 128:(h + 1) * 128]
        k = qkv[:, 512 + h * 128:512 + (h + 1) * 128]
        qs.append(q * lax.rsqrt(jnp.sum(q * q, axis=-1, keepdims=True) + NORM_EPS) * (GDN_DK ** -0.5))
        ks.append(k * lax.rsqrt(jnp.sum(k * k, axis=-1, keepdims=True) + NORM_EPS))
        gs.append(_lane_pick(g_all, h))
        bs.append(_lane_pick(beta_all, GDN_HEADS + h))
    cat = lambda xs: jnp.concatenate(xs, axis=1)
    return cat(qs), cat(ks), qkv[:, 1024:], cat(gs), cat(bs)


GDN_PREP_TILE = 256


def _gdn_prep_specs(s, params, reverse=False):
    t = min(GDN_PREP_TILE, s)
    n = s // t
    blk = (lambda i: n - 1 - i) if reverse else (lambda i: i)
    prev = pl.BlockSpec((GDN_HALO, GDN_QKV), lambda i: (jnp.maximum(blk(i) * (t // GDN_HALO) - 1, 0), 0))
    rows = lambda d: pl.BlockSpec((t, d), lambda i: (blk(i), 0))
    return t, rows, [prev, rows(GDN_QKV), rows(LANES)] + [pl.BlockSpec(p.shape, lambda i: (0, 0)) for p in params]


def _gdn_prep_masked(has_rows_before):
    return lambda prev, *rest: _gdn_prep_fn(prev * has_rows_before, *rest)


def _gdn_prep_fwd(qkv, ab, params):
    s = qkv.shape[0]
    t, rows, in_specs = _gdn_prep_specs(s, params)

    def body(prev_ref, cur_ref, ab_ref, *refs):
        p_refs, o_refs = refs[:len(params)], refs[len(params):]
        has_rows_before = (pl.program_id(0) > 0).astype(F32)
        outs = _gdn_prep_masked(has_rows_before)(prev_ref[...], cur_ref[...], ab_ref[...], *[p[...] for p in p_refs])
        for ref, val in zip(o_refs, outs):
            ref[...] = val

    return pl.pallas_call(
        body, name="gdn_prep", grid=(s // t,), in_specs=in_specs,
        out_specs=[rows(GDN_WIDTH)] * 5, out_shape=[jax.ShapeDtypeStruct((s, GDN_WIDTH), F32)] * 5,
        compiler_params=_cparams(("parallel",)))(qkv, qkv, ab, *params)


def _gdn_prep_bwd(qkv, ab, params, cts):
    s = qkv.shape[0]
    t, rows, in_specs = _gdn_prep_specs(s, params, reverse=True)
    n = s // t
    npar = len(params)

    def body(prev_ref, cur_ref, ab_ref, *refs):
        p_refs, g_refs = refs[:npar], refs[npar:npar + 5]
        dcur_ref, dab_ref = refs[npar + 5:npar + 7]
        dp_refs, carry_sc = refs[npar + 7:-1], refs[-1]

        @pl.when(pl.program_id(0) == 0)
        def _():
            carry_sc[...] = jnp.zeros_like(carry_sc)
            for ref in dp_refs:
                ref[...] = jnp.zeros_like(ref)

        has_rows_before = (pl.program_id(0) < n - 1).astype(F32)
        _, vjp = jax.vjp(_gdn_prep_masked(has_rows_before), prev_ref[...], cur_ref[...], ab_ref[...],
                         *[p[...] for p in p_refs])
        d_prev, d_cur, d_ab, *d_params = vjp(tuple(g[...] for g in g_refs))
        dcur_ref[...] = d_cur
        dcur_ref[t - GDN_HALO:, :] += carry_sc[...]
        carry_sc[...] = d_prev
        dab_ref[...] = d_ab
        for ref, val in zip(dp_refs, d_params):
            ref[...] += val

    res = pl.pallas_call(
        body, name="gdn_prep_bwd", grid=(n,), in_specs=in_specs + [rows(GDN_WIDTH)] * 5,
        out_specs=[rows(GDN_QKV), rows(LANES)] + [pl.BlockSpec(p.shape, lambda i: (0, 0)) for p in params],
        out_shape=[jax.ShapeDtypeStruct((s, GDN_QKV), F32), jax.ShapeDtypeStruct((s, LANES), F32)]
        + [jax.ShapeDtypeStruct(p.shape, F32) for p in params],
        scratch_shapes=[pltpu.VMEM((GDN_HALO, GDN_QKV), F32)],
        compiler_params=_cparams(("arbitrary",)))(qkv, qkv, ab, *params, *cts)
    return res[0], res[1], tuple(res[2:])


@jax.custom_vjp
def gdn_prep(qkv, ab, params):
    return tuple(_gdn_prep_fwd(qkv, ab, params))


def _gdn_prep_vjp_fwd(qkv, ab, params):
    return tuple(_gdn_prep_fwd(qkv, ab, params)), (qkv, ab, params)


def _gdn_prep_vjp_bwd(res, cts):
    qkv, ab, params = res
    return _gdn_prep_bwd(qkv, ab, params, cts)


gdn_prep.defvjp(_gdn_prep_vjp_fwd, _gdn_prep_vjp_bwd)


@jax.custom_vjp
def _unit_lower_inverse(lms):
    c = lms[0].shape[0]
    row = lax.broadcasted_iota(jnp.int32, (c, c), 0)
    col = lax.broadcasted_iota(jnp.int32, (c, c), 1)
    ts = [(row == col).astype(F32) - jnp.where((row >> 1) == (col >> 1), lm, 0.0) for lm in lms]
    for level in range(1, int(math.log2(c))):
        below = ((row >> (level + 1)) == (col >> (level + 1))) & ((row >> level) != (col >> level))
        mids = [_dot_split(t, jnp.where(below, lm, 0.0)) for t, lm in zip(ts, lms)]
        ts = [t - _dot_split(mid, t) for t, mid in zip(ts, mids)]
    return tuple(ts)


def _uli_fwd(lms):
    ts = _unit_lower_inverse(lms)
    return ts, ts


def _uli_bwd(ts, gs):
    mids = [bdot(t, g, "tn") for t, g in zip(ts, gs)]
    return (tuple(-bdot(mid, t, "nt") for t, mid in zip(ts, mids)),)


_unit_lower_inverse.defvjp(_uli_fwd, _uli_bwd)


@jax.custom_vjp
def _known_inverse(lms, ts):
    return ts


_known_inverse.defvjp(lambda lms, ts: (ts, ts),
                      lambda ts, gs: (_uli_bwd(ts, gs)[0], tuple(jnp.zeros_like(t) for t in ts)))


def _gdn_chunk_fn(qs, ks, vs, gbs, bbs, s0s, known_ts=None):
    heads = range(len(qs))
    c = qs[0].shape[0]
    row = lax.broadcasted_iota(jnp.int32, (c, c), 0)
    col = lax.broadcasted_iota(jnp.int32, (c, c), 1)
    incl, strict = row >= col, row > col
    tri = incl.astype(F32)
    gc = [bdot(tri, gbs[h], "nn", True) for h in heads]
    decay = [jnp.exp(jnp.where(incl, gc[h] - gc[h].T, -1e30)) for h in heads]
    g_last = [jnp.sum(gbs[h], axis=0, keepdims=True) for h in heads]
    eg = [jnp.exp(gc[h]) for h in heads]
    kb = [ks[h] * bbs[h] for h in heads]
    lms = tuple(jnp.where(strict, bdot(kb[h], ks[h], "nt") * decay[h], 0.0) for h in heads)
    ts = _unit_lower_inverse(lms) if known_ts is None else _known_inverse(lms, known_ts)
    u = [bdot(ts[h], vs[h] * bbs[h]) for h in heads]
    w = [bdot(ts[h], kb[h] * eg[h]) for h in heads]
    qk = [bdot(qs[h], ks[h], "nt") * decay[h] for h in heads]
    v_new = [u[h] - bdot(w[h], s0s[h]) for h in heads]
    o = [bdot(qs[h] * eg[h], s0s[h]) + bdot(qk[h], v_new[h]) for h in heads]
    s1 = [s0s[h] * jnp.exp(g_last[h]) + bdot(ks[h] * jnp.exp(g_last[h] - gc[h]), v_new[h], "tn") for h in heads]
    return (tuple(o), tuple(s1)), ts


def _head_tiles(ref):
    return tuple(ref[:, h * 128:(h + 1) * 128] for h in range(GDN_HEADS))


def _gdn_fwd(q, k, v, gb, bb):
    s = q.shape[0]
    c = min(GDN_CHUNK, s)
    nc = s // c

    def body(q_ref, k_ref, v_ref, g_ref, b_ref, o_ref, st_ref, inv_ref, s_sc):
        @pl.when(pl.program_id(0) == 0)
        def _():
            s_sc[...] = jnp.zeros_like(s_sc)

        s0s = tuple(s_sc[h] for h in range(GDN_HEADS))
        for h in range(GDN_HEADS):
            st_ref[h, 0] = s0s[h]
        (os, s1s), ts = _gdn_chunk_fn(*[_head_tiles(ref) for ref in (q_ref, k_ref, v_ref, g_ref, b_ref)], s0s)
        for h in range(GDN_HEADS):
            o_ref[:, h * 128:(h + 1) * 128] = os[h]
            inv_ref[h, 0] = ts[h]
            s_sc[h] = s1s[h]

    blk = pl.BlockSpec((c, GDN_WIDTH), lambda n: (n, 0))
    return pl.pallas_call(
        body, name="gdn_fwd", grid=(nc,), in_specs=[blk] * 5,
        out_specs=[blk, pl.BlockSpec((GDN_HEADS, 1, 128, 128), lambda n: (0, n, 0, 0)),
                   pl.BlockSpec((GDN_HEADS, 1, c, c), lambda n: (0, n, 0, 0))],
        out_shape=[jax.ShapeDtypeStruct((s, GDN_WIDTH), F32), jax.ShapeDtypeStruct((GDN_HEADS, nc, 128, 128), F32),
                   jax.ShapeDtypeStruct((GDN_HEADS, nc, c, c), F32)],
        scratch_shapes=[pltpu.VMEM((GDN_HEADS, 128, 128), F32)],
        compiler_params=_cparams(("arbitrary",)))(q, k, v, gb, bb)


def _gdn_bwd(q, k, v, gb, bb, states, inverses, do):
    s = q.shape[0]
    c = min(GDN_CHUNK, s)
    nc = s // c

    def body(q_ref, k_ref, v_ref, g_ref, b_ref, st_ref, inv_ref, do_ref, dq_ref, dk_ref, dv_ref, dg_ref, db_ref, ds_sc):
        @pl.when(pl.program_id(0) == 0)
        def _():
            ds_sc[...] = jnp.zeros_like(ds_sc)

        s0s = tuple(st_ref[h, 0] for h in range(GDN_HEADS))
        ts = tuple(inv_ref[h, 0] for h in range(GDN_HEADS))
        chunk = lambda *args: _gdn_chunk_fn(*args, known_ts=ts)[0]
        _, vjp = jax.vjp(chunk, *[_head_tiles(ref) for ref in (q_ref, k_ref, v_ref, g_ref, b_ref)], s0s)
        *d_tiles, ds0s = vjp((_head_tiles(do_ref), tuple(ds_sc[h] for h in range(GDN_HEADS))))
        for h in range(GDN_HEADS):
            for ref, d in zip((dq_ref, dk_ref, dv_ref, dg_ref, db_ref), d_tiles):
                ref[:, h * 128:(h + 1) * 128] = d[h]
            ds_sc[h] = ds0s[h]

    blk = pl.BlockSpec((c, GDN_WIDTH), lambda n: (nc - 1 - n, 0))
    return pl.pallas_call(
        body, name="gdn_bwd", grid=(nc,),
        in_specs=[blk] * 5 + [pl.BlockSpec((GDN_HEADS, 1, 128, 128), lambda n: (0, nc - 1 - n, 0, 0)),
                              pl.BlockSpec((GDN_HEADS, 1, c, c), lambda n: (0, nc - 1 - n, 0, 0)), blk],
        out_specs=[blk] * 5, out_shape=[jax.ShapeDtypeStruct((s, GDN_WIDTH), F32)] * 5,
        scratch_shapes=[pltpu.VMEM((GDN_HEADS, 128, 128), F32)],
        compiler_params=_cparams(("arbitrary",)))(q, k, v, gb, bb, states, inverses, do)


@jax.custom_vjp
def gdn_core(q, k, v, gb, bb):
    return _gdn_fwd(q, k, v, gb, bb)[0]


def _gdn_core_fwd(q, k, v, gb, bb):
    o, states, inverses = _gdn_fwd(q, k, v, gb, bb)
    return o, (q, k, v, gb, bb, states, inverses)


def _gdn_core_bwd(res, do):
    return tuple(_gdn_bwd(*res, do))


gdn_core.defvjp(_gdn_core_fwd, _gdn_core_bwd)


def _cact_fn(c):
    return (_silu(c),)


def _prenorm_fn(x, w, scale_raw, scale_b, shift_raw, shift_b):
    return (_rms(x, w) * (1.0 + scale_raw + scale_b) + shift_raw + shift_b,)


def _tail_fn(x, o_mla, z_mla, o_gdn, z_gdn, o_norm_w, w_out, post_w, gate_raw, gate_b):
    y_mla = o_mla * _silu(z_mla)
    parts = [_rms(o_gdn[:, h * 128:(h + 1) * 128], o_norm_w) for h in range(GDN_HEADS)]
    y_gdn = jnp.concatenate(parts, axis=1) * _silu(z_gdn)
    y = bdot(y_mla, w_out[:MLA_WIDTH]) + bdot(y_gdn, w_out[MLA_WIDTH:])
    return (x + (gate_raw + gate_b) * _rms(y, post_w),)


def _loss_fn(y, tgt):
    err = y - tgt
    part = jnp.sum(0.5 * jnp.mean(err * err, axis=-1, keepdims=True), axis=0, keepdims=True)
    return (jnp.broadcast_to(part, (1, LANES)),)


def _resident(shape):
    return pl.BlockSpec(shape, lambda i: (0,) * len(shape), pipeline_mode=pl.Buffered(1))


def _front_fwd(x, norm_params, ws16):
    s = x.shape[0]
    t = min(512, s)

    def body(x_ref, *refs):
        p_refs, w_refs, o_refs = refs[:len(norm_params)], refs[len(norm_params):-len(ws16)], refs[-len(ws16):]
        (h,) = _prenorm_fn(x_ref[...], *[p[...] for p in p_refs])
        h16 = h.astype(BF16)
        for w_ref, o_ref in zip(w_refs, o_refs):
            o_ref[...] = _dot_raw(h16, w_ref[...], "nn", False)

    rows = lambda d: pl.BlockSpec((t, d), lambda i: (i, 0))
    res = pl.pallas_call(
        body, name="front", grid=(s // t,),
        in_specs=[rows(D_MODEL)] + [_resident(p.shape) for p in norm_params] + [_resident(w.shape) for w in ws16],
        out_specs=[rows(w.shape[1]) for w in ws16],
        out_shape=[jax.ShapeDtypeStruct((s, w.shape[1]), F32) for w in ws16],
        compiler_params=_cparams(("parallel",)))(x, *norm_params, *ws16)
    return tuple(res)


def _front_bwd(x, norm_params, ws16, dys):
    s = x.shape[0]
    t = min(512, s)
    npar, ng = len(norm_params), len(ws16)

    def body(x_ref, *refs):
        p_refs, w_refs, g_refs = refs[:npar], refs[npar:npar + ng], refs[npar + ng:npar + 2 * ng]
        dx_ref = refs[npar + 2 * ng]
        dp_refs, dw_refs = refs[npar + 2 * ng + 1:2 * npar + 2 * ng + 1], refs[2 * npar + 2 * ng + 1:]

        @pl.when(pl.program_id(0) == 0)
        def _():
            for ref in dp_refs + dw_refs:
                ref[...] = jnp.zeros_like(ref)

        (h,), vjp = jax.vjp(_prenorm_fn, x_ref[...], *[p[...] for p in p_refs])
        h16 = h.astype(BF16)
        dys16 = [g[...].astype(BF16) for g in g_refs]
        dh = functools.reduce(lambda a, b: a + b,
                              [_dot_raw(dy, w_ref[...], "nt", False) for dy, w_ref in zip(dys16, w_refs)])
        for dy, dw_ref in zip(dys16, dw_refs):
            dw_ref[...] += _dot_raw(h16, dy, "tn", False)
        dx, *d_params = vjp((dh,))
        dx_ref[...] = dx
        for ref, val in zip(dp_refs, d_params):
            ref[...] += val

    rows = lambda d: pl.BlockSpec((t, d), lambda i: (i, 0))
    res = pl.pallas_call(
        body, name="front_bwd", grid=(s // t,),
        in_specs=[rows(D_MODEL)] + [_resident(p.shape) for p in norm_params] + [_resident(w.shape) for w in ws16]
        + [rows(w.shape[1]) for w in ws16],
        out_specs=[rows(D_MODEL)] + [_resident(p.shape) for p in norm_params] + [_resident(w.shape) for w in ws16],
        out_shape=[jax.ShapeDtypeStruct(x.shape, F32)] + [jax.ShapeDtypeStruct(p.shape, F32) for p in norm_params]
        + [jax.ShapeDtypeStruct(w.shape, F32) for w in ws16],
        compiler_params=_cparams(("arbitrary",)))(x, *norm_params, *ws16, *dys)
    return res[0], tuple(res[1:1 + npar]), tuple(res[1 + npar:])


@jax.custom_vjp
def front(x, norm_params, ws):
    return _front_fwd(x, norm_params, tuple(w.astype(BF16) for w in ws))


def _front_vjp_fwd(x, norm_params, ws):
    ws16 = tuple(w.astype(BF16) for w in ws)
    return _front_fwd(x, norm_params, ws16), (x, norm_params, ws16)


def _front_vjp_bwd(res, dys):
    return _front_bwd(*res, dys)


front.defvjp(_front_vjp_fwd, _front_vjp_bwd)


_OPS = dict(
    cact=make_rowwise(_cact_fn, "c_act", (D_MODEL,), tile=16),
    tail=make_rowwise(_tail_fn, "tail", (D_MODEL,)),
    loss=make_rowwise(_loss_fn, "loss", (), acc_dims=(LANES,), n_nondiff=1),
    lin_mod=make_linear("lin_mod"),
)


def _swap_halves(w):
    half = w.shape[-1] // 2
    return jnp.concatenate([w[..., half:], w[..., :half]], axis=-1)


def _w_in_groups(w):
    k_pe = w[:, 640:704]
    ab = jnp.concatenate([w[:, 2752:2760], jnp.zeros((w.shape[0], LANES - 8), w.dtype)], axis=1)
    return (w[:, :384], w[:, 384:640], jnp.concatenate([k_pe, _swap_halves(k_pe)], axis=1), w[:, 704:1216],
            w[:, 1216:2752], ab, w[:, 2760:])


def _q_up_ext(w):
    parts = []
    for h in range(MLA_HEADS):
        rope = w[:, h * 192 + 128:(h + 1) * 192]
        parts += [w[:, h * 192:h * 192 + 128], rope, _swap_halves(rope)]
    return jnp.concatenate(parts, axis=1)


def _kv_up_perm(w):
    ks = [w[:, h * 256:h * 256 + 128] for h in range(MLA_HEADS)]
    vs = [w[:, h * 256 + 128:(h + 1) * 256] for h in range(MLA_HEADS)]
    return jnp.concatenate(ks + vs, axis=1)


def _pad_lanes(v):
    return jnp.pad(v, (0, LANES - v.shape[0]))[None, :]


def _local_loss(weights, x, c, positions, target):
    s = x.shape[0]
    half = MLA_ROPE // 2
    inv_freq = jnp.power(ROPE_THETA, -jnp.arange(half, dtype=F32) * 2.0 / MLA_ROPE)
    ang = positions.astype(F32)[:, None] * inv_freq
    cos, sin, zero = jnp.cos(ang), jnp.sin(ang), jnp.zeros((s, 2 * half), F32)
    t1 = jnp.concatenate([cos, cos, zero], axis=1)
    t2 = jnp.concatenate([-sin, sin, zero], axis=1)

    (c_act,) = _OPS["cact"]((jnp.pad(c, ((0, 15), (0, 0))),), ())
    for l in range(DEPTH):
        mod = _OPS["lin_mod"](c_act, weights["w_mod"][l])[0:1]
        b = weights["b_mod"][l][None, :]
        shift_raw, scale_raw, gate_raw = mod[:, :1024], mod[:, 1024:2048], mod[:, 2048:]
        shift_b, scale_b, gate_b = b[:, :1024], b[:, 1024:2048], b[:, 2048:]
        q_lat, kv_lat, kr, z_mla, qkv, ab, z_gdn = front(
            x, (weights["pre_norm_w"][l][None], scale_raw, scale_b, shift_raw, shift_b),
            _w_in_groups(weights["w_in"][l]))
        o_mla = mla_attention((q_lat, kv_lat, kr, t1, t2),
                              (weights["mla_q_norm_w"][l][None], _q_up_ext(weights["mla_q_up"][l]),
                               weights["mla_kv_norm_w"][l][None], _kv_up_perm(weights["mla_kv_up"][l])))
        cw = weights["gdn_conv_w"][l]
        params = tuple(cw[j][None] for j in range(GDN_CONV))
        params += (_pad_lanes(weights["gdn_a_log"][l]), _pad_lanes(weights["gdn_dt_bias"][l]))
        o_gdn = gdn_core(*gdn_prep(qkv, ab, params))
        (x,) = _OPS["tail"]((x, o_mla, z_mla, o_gdn, z_gdn),
                            (weights["gdn_o_norm_w"][l][None], weights["w_out"][l], weights["post_norm_w"][l][None],
                             gate_raw, gate_b))
    (acc,) = _OPS["loss"]((x, target), ())
    return acc[0, 0]


def _chip_index():
    return 2 * lax.axis_index("x") + lax.axis_index("y")


def _other_chips(x, y):
    return [(1 - x, y), (x, 1 - y), (1 - x, 1 - y)]


def _any_spec():
    return pl.BlockSpec(memory_space=pl.ANY)


def _half(ref, hc):
    n = ref.shape[0] // 2
    return ref.at[pl.ds(hc * n, n)]


def ag_weights(shards):
    n = len(shards)

    def body(*refs):
        ins, outs = refs[:n], refs[n:2 * n]
        send_sems, recv_sems = refs[2 * n:]
        x, y, c = lax.axis_index("x"), lax.axis_index("y"), lax.axis_index("c")
        sibling = (x, y, 1 - c)
        chips = _other_chips(x, y)

        def copy(t, k, src, chip_xy, hc, to):
            return pltpu.make_async_remote_copy(
                src_ref=src, dst_ref=_half(outs[t].at[2 * chip_xy[0] + chip_xy[1]], hc),
                send_sem=send_sems.at[6 * t + k], recv_sem=recv_sems.at[6 * t + k], device_id=to, device_id_type=MESH)

        first = [copy(t, k, _half(ins[t], c), (x, y), c, (*chip, c)) for k, chip in enumerate(chips) for t in range(n)]
        for cp in first:
            cp.start()
        passed = []
        for k, chip in enumerate(chips):
            for t in range(n):
                landed = _half(outs[t].at[2 * chip[0] + chip[1]], c)
                copy(t, k, landed, chip, c, (x, y, c)).wait_recv()
                passed.append(copy(t, 3 + k, landed, chip, c, sibling))
                passed[-1].start()
        for k, chip in enumerate(chips):
            for t in range(n):
                copy(t, 3 + k, _half(ins[t], c), chip, 1 - c, (x, y, c)).wait_recv()
        for cp in first + passed:
            cp.wait_send()

    return pl.pallas_call(
        body, name="ag_weights", in_specs=[_any_spec()] * n, out_specs=[_any_spec()] * n,
        out_shape=[jax.ShapeDtypeStruct((N_CHIPS,) + a.shape, a.dtype) for a in shards],
        scratch_shapes=[pltpu.SemaphoreType.DMA((6 * n,)), pltpu.SemaphoreType.DMA((6 * n,))],
        compiler_params=pltpu.CompilerParams(has_side_effects=True))(*shards)


def rs_pair(gs):
    n = len(gs)

    def body(*refs):
        ins, outs = refs[:n], refs[n:2 * n]
        send_sems, recv_sems = refs[2 * n:]
        x, y, c = lax.axis_index("x"), lax.axis_index("y"), lax.axis_index("c")
        lh = [g.shape[1] // 2 for g in gs]
        copies = [pltpu.make_async_remote_copy(
            src_ref=ins[t].at[:, pl.ds((1 - c) * lh[t], lh[t])], dst_ref=outs[t], send_sem=send_sems.at[t],
            recv_sem=recv_sems.at[t], device_id=(x, y, 1 - c), device_id_type=MESH) for t in range(n)]
        for cp in copies:
            cp.start()
        for cp in copies:
            cp.wait()

    return pl.pallas_call(
        body, name="rs_pair", in_specs=[_any_spec()] * n, out_specs=[_any_spec()] * n,
        out_shape=[jax.ShapeDtypeStruct((N_CHIPS, g.shape[1] // 2) + g.shape[2:], F32) for g in gs],
        scratch_shapes=[pltpu.SemaphoreType.DMA((n,)), pltpu.SemaphoreType.DMA((n,))],
        compiler_params=pltpu.CompilerParams(has_side_effects=True))(*gs)


def rs_cross(pairs):
    n = len(pairs)

    def body(*refs):
        ins, outs = refs[:n], refs[n:2 * n]
        send_sems, recv_sems = refs[2 * n:]
        x, y, c = lax.axis_index("x"), lax.axis_index("y"), lax.axis_index("c")
        copies = []
        for k, chip in enumerate(_other_chips(x, y)):
            for t in range(n):
                copies.append(pltpu.make_async_remote_copy(
                    src_ref=ins[t].at[2 * chip[0] + chip[1]], dst_ref=outs[t].at[k], send_sem=send_sems.at[3 * t + k],
                    recv_sem=recv_sems.at[3 * t + k], device_id=(*chip, c), device_id_type=MESH))
        for cp in copies:
            cp.start()
        for cp in copies:
            cp.wait()

    return pl.pallas_call(
        body, name="rs_cross", in_specs=[_any_spec()] * n, out_specs=[_any_spec()] * n,
        out_shape=[jax.ShapeDtypeStruct((3,) + p.shape[1:], p.dtype) for p in pairs],
        scratch_shapes=[pltpu.SemaphoreType.DMA((3 * n,)), pltpu.SemaphoreType.DMA((3 * n,))],
        compiler_params=pltpu.CompilerParams(has_side_effects=True))(*pairs)


def rs_share(blocks):
    n = len(blocks)

    def body(*refs):
        ins, outs = refs[:n], refs[n:2 * n]
        send_sems, recv_sems = refs[2 * n:]
        x, y, c = lax.axis_index("x"), lax.axis_index("y"), lax.axis_index("c")
        sends = [pltpu.make_async_remote_copy(
            src_ref=_half(ins[t], c), dst_ref=_half(outs[t], c), send_sem=send_sems.at[t], recv_sem=recv_sems.at[t],
            device_id=(x, y, 1 - c), device_id_type=MESH) for t in range(n)]
        for cp in sends:
            cp.start()
        for t in range(n):
            pltpu.make_async_remote_copy(
                src_ref=_half(ins[t], c), dst_ref=_half(outs[t], 1 - c), send_sem=send_sems.at[t],
                recv_sem=recv_sems.at[t], device_id=(x, y, 1 - c), device_id_type=MESH).wait_recv()
        for cp in sends:
            cp.wait_send()

    return pl.pallas_call(
        body, name="rs_share", in_specs=[_any_spec()] * n, out_specs=[_any_spec()] * n,
        out_shape=[jax.ShapeDtypeStruct(b.shape, F32) for b in blocks],
        input_output_aliases={t: t for t in range(n)},
        scratch_shapes=[pltpu.SemaphoreType.DMA((n,)), pltpu.SemaphoreType.DMA((n,))],
        compiler_params=pltpu.CompilerParams(has_side_effects=True))(*blocks)


def _row_tile(rows, target):
    best = None
    for t in range(8, min(rows, target) + 1, 8):
        if rows % t == 0:
            best = t
    return rows if best is None else best


TILE_BYTES = 2 * 1024 * 1024


def _flat_rows(shape):
    rows = int(np.prod(shape[1:-1]))
    cols_padded = -(-shape[-1] // LANES) * LANES
    return rows, _row_tile(rows, max(8, TILE_BYTES // (4 * cols_padded)))


def pair_add(g, from_sibling, out_dtype):
    cols = g.shape[-1]
    rph, t = _flat_rows(from_sibling.shape)
    nt = rph // t
    c_arr = lax.axis_index("c").astype(jnp.int32).reshape(1)

    def body(c_ref, a_ref, b_ref, o_ref):
        o_ref[...] = (a_ref[...] + b_ref[...]).astype(o_ref.dtype)

    out = pl.pallas_call(
        body, name="pair_add",
        grid_spec=pltpu.PrefetchScalarGridSpec(
            num_scalar_prefetch=1, grid=(N_CHIPS, nt),
            in_specs=[pl.BlockSpec((t, cols), lambda j, i, c_ref: (j * 2 * nt + c_ref[0] * nt + i, 0)),
                      pl.BlockSpec((t, cols), lambda j, i, c_ref: (j * nt + i, 0))],
            out_specs=pl.BlockSpec((t, cols), lambda j, i, c_ref: (j * nt + i, 0))),
        out_shape=jax.ShapeDtypeStruct((N_CHIPS * rph, cols), out_dtype),
        compiler_params=_cparams(("parallel", "parallel")))(c_arr, g.reshape(-1, cols), from_sibling.reshape(-1, cols))
    return out.reshape(from_sibling.shape)


def chip_add(pairs, received):
    cols = pairs.shape[-1]
    rph, t = _flat_rows(pairs.shape)
    nt = rph // t
    j_arr = _chip_index().astype(jnp.int32).reshape(1)
    c_arr = lax.axis_index("c").astype(jnp.int32).reshape(1)
    r2 = received.reshape(-1, cols)

    def body(j_ref, c_ref, a_ref, r0_ref, r1_ref, r2_ref, o_ref):
        a, r0, r1, r2 = [ref[...].astype(F32) for ref in (a_ref, r0_ref, r1_ref, r2_ref)]
        o_ref[...] = (a + r0) + (r1 + r2)

    out = pl.pallas_call(
        body, name="chip_add",
        grid_spec=pltpu.PrefetchScalarGridSpec(
            num_scalar_prefetch=2, grid=(nt,),
            in_specs=[pl.BlockSpec((t, cols), lambda i, j_ref, c_ref: (j_ref[0] * nt + i, 0)),
                      pl.BlockSpec((t, cols), lambda i, j_ref, c_ref: (i, 0)),
                      pl.BlockSpec((t, cols), lambda i, j_ref, c_ref: (nt + i, 0)),
                      pl.BlockSpec((t, cols), lambda i, j_ref, c_ref: (2 * nt + i, 0))],
            out_specs=pl.BlockSpec((t, cols), lambda i, j_ref, c_ref: (c_ref[0] * nt + i, 0))),
        out_shape=jax.ShapeDtypeStruct((2 * rph, cols), F32),
        compiler_params=_cparams(("parallel",)))(j_arr, c_arr, pairs.reshape(-1, cols), r2, r2, r2)
    return out.reshape((2 * pairs.shape[1],) + pairs.shape[2:])


def reduce_grads(gs, cross_dtypes):
    pairs = [pair_add(g, r, dt) for g, r, dt in zip(gs, rs_pair(gs), cross_dtypes)]
    return rs_share([chip_add(p, r) for p, r in zip(pairs, rs_cross(pairs))])


def adamw(w, g, m, v):
    shape = w.shape
    cols = shape[-1]
    rows = int(np.prod(shape[:-1]))
    flat = lambda a: a.reshape(rows, cols)
    t = _row_tile(rows, 256)

    def body(w_ref, g_ref, m_ref, v_ref, d_ref, mo_ref, vo_ref):
        gv = g_ref[...]
        m_new = ADAM_B1 * m_ref[...] + (1.0 - ADAM_B1) * gv
        v_new = ADAM_B2 * v_ref[...] + (1.0 - ADAM_B2) * (gv * gv)
        m_hat = m_new / (1.0 - ADAM_B1 ** ADAM_STEP)
        v_hat = v_new / (1.0 - ADAM_B2 ** ADAM_STEP)
        d_ref[...] = -ADAM_LR * (m_hat / (jnp.sqrt(v_hat) + ADAM_EPS) + ADAM_WD * w_ref[...])
        mo_ref[...] = m_new
        vo_ref[...] = v_new

    spec = pl.BlockSpec((t, cols), lambda i: (i, 0))
    outs = pl.pallas_call(
        body, name="adamw", grid=(rows // t,), in_specs=[spec] * 4, out_specs=[spec] * 3,
        out_shape=[jax.ShapeDtypeStruct((rows, cols), F32)] * 3,
        compiler_params=_cparams(("parallel",)))(flat(w), flat(g), flat(m), flat(v))
    return tuple(o.reshape(shape) for o in outs)


SHARDED = (("w_mod", 2), ("w_in", 2), ("mla_q_up", 2), ("mla_kv_up", 2), ("gdn_conv_w", 2), ("w_out", 1))
REPLICATED = ("b_mod", "pre_norm_w", "post_norm_w", "mla_q_norm_w", "mla_kv_norm_w", "gdn_a_log", "gdn_dt_bias",
              "gdn_o_norm_w")
WEIGHT_ORDER = ("w_mod", "b_mod", "pre_norm_w", "post_norm_w", "w_in", "mla_q_norm_w", "mla_q_up", "mla_kv_norm_w",
                "mla_kv_up", "gdn_conv_w", "gdn_a_log", "gdn_dt_bias", "gdn_o_norm_w", "w_out")
EXACT_F32 = ("gdn_conv_w",)
SMALL_ROWS = 48


def _gather_weights(shards):
    names = [name for name, _ in SHARDED]
    own = [shards[n] if n in EXACT_F32 else shards[n].astype(BF16) for n in names]
    gathered = ag_weights(own)
    full = {}
    for (name, axis), blk, mine in zip(SHARDED, gathered, own):
        shp = shards[name].shape
        blk = lax.dynamic_update_index_in_dim(blk, mine, _chip_index(), 0)
        blk = jnp.moveaxis(blk.astype(F32), 0, axis)
        full[name] = blk.reshape(shp[:axis] + (N_CHIPS * shp[axis],) + shp[axis + 1:])
    return full


def _split_grads(grads):
    pieces = []
    for name, axis in SHARDED:
        g = grads[name]
        shp = g.shape
        g = g.reshape(shp[:axis] + (N_CHIPS, shp[axis] // N_CHIPS) + shp[axis + 1:])
        pieces.append(jnp.moveaxis(g, axis, 0))
    small = jnp.concatenate([grads[name] for name in REPLICATED], axis=1)
    small = jnp.pad(small, ((0, 0), (0, SMALL_ROWS * LANES - small.shape[1]))).reshape(DEPTH, SMALL_ROWS, LANES)
    pieces.append(jnp.broadcast_to(small[None], (N_CHIPS,) + small.shape))
    return pieces


def _unsplit_small(small, rep_shapes):
    flat = small.reshape(DEPTH, SMALL_ROWS * LANES)
    out, off = {}, 0
    for name in REPLICATED:
        size = rep_shapes[name][1]
        out[name] = flat[:, off:off + size]
        off += size
    return out


def kernel(x, c, positions, w_mod, b_mod, pre_norm_w, post_norm_w, w_in, mla_q_norm_w, mla_q_up, mla_kv_norm_w, mla_kv_up, gdn_conv_w, gdn_a_log, gdn_dt_bias, gdn_o_norm_w, w_out, loss_target, m_w_mod, m_b_mod, m_pre_norm_w, m_post_norm_w, m_w_in, m_mla_q_norm_w, m_mla_q_up, m_mla_kv_norm_w, m_mla_kv_up, m_gdn_conv_w, m_gdn_a_log, m_gdn_dt_bias, m_gdn_o_norm_w, m_w_out, v_w_mod, v_b_mod, v_pre_norm_w, v_post_norm_w, v_w_in, v_mla_q_norm_w, v_mla_q_up, v_mla_kv_norm_w, v_mla_kv_up, v_gdn_conv_w, v_gdn_a_log, v_gdn_dt_bias, v_gdn_o_norm_w, v_w_out):
    given = dict(w_mod=w_mod, b_mod=b_mod, pre_norm_w=pre_norm_w, post_norm_w=post_norm_w, w_in=w_in,
                 mla_q_norm_w=mla_q_norm_w, mla_q_up=mla_q_up, mla_kv_norm_w=mla_kv_norm_w, mla_kv_up=mla_kv_up,
                 gdn_conv_w=gdn_conv_w, gdn_a_log=gdn_a_log, gdn_dt_bias=gdn_dt_bias, gdn_o_norm_w=gdn_o_norm_w,
                 w_out=w_out)
    moments_m = dict(w_mod=m_w_mod, b_mod=m_b_mod, pre_norm_w=m_pre_norm_w, post_norm_w=m_post_norm_w, w_in=m_w_in,
                     mla_q_norm_w=m_mla_q_norm_w, mla_q_up=m_mla_q_up, mla_kv_norm_w=m_mla_kv_norm_w,
                     mla_kv_up=m_mla_kv_up, gdn_conv_w=m_gdn_conv_w, gdn_a_log=m_gdn_a_log,
                     gdn_dt_bias=m_gdn_dt_bias, gdn_o_norm_w=m_gdn_o_norm_w, w_out=m_w_out)
    moments_v = dict(w_mod=v_w_mod, b_mod=v_b_mod, pre_norm_w=v_pre_norm_w, post_norm_w=v_post_norm_w, w_in=v_w_in,
                     mla_q_norm_w=v_mla_q_norm_w, mla_q_up=v_mla_q_up, mla_kv_norm_w=v_mla_kv_norm_w,
                     mla_kv_up=v_mla_kv_up, gdn_conv_w=v_gdn_conv_w, gdn_a_log=v_gdn_a_log,
                     gdn_dt_bias=v_gdn_dt_bias, gdn_o_norm_w=v_gdn_o_norm_w, w_out=v_w_out)

    full = _gather_weights({name: given[name] for name, _ in SHARDED})
    for name in REPLICATED:
        full[name] = given[name]
    loss_local, (grads, grad_x) = jax.value_and_grad(_local_loss, argnums=(0, 1))(
        full, x[0], c, positions[0], loss_target[0])
    loss = lax.psum(loss_local, AXES)

    cross_dtypes = [F32 if name in EXACT_F32 else BF16 for name, _ in SHARDED] + [F32]
    reduced = reduce_grads(_split_grads(grads), cross_dtypes)
    grad_w = {name: g for (name, _), g in zip(SHARDED, reduced)}
    grad_w.update(_unsplit_small(reduced[-1], {name: given[name].shape for name in REPLICATED}))
    delta, new_m, new_v = {}, {}, {}
    for name in WEIGHT_ORDER:
        delta[name], new_m[name], new_v[name] = adamw(given[name], grad_w[name], moments_m[name], moments_v[name])
    return (loss, grad_x[None], *[grad_w[n] for n in WEIGHT_ORDER], *[delta[n] for n in WEIGHT_ORDER],
            *[new_m[n] for n in WEIGHT_ORDER], *[new_v[n] for n in WEIGHT_ORDER])
```

```python
import functools
import math

import numpy as np
import jax
import jax.numpy as jnp
from jax import lax
from jax.experimental import pallas as pl
from jax.experimental.pallas import tpu as pltpu

F32 = jnp.float32
BF16 = jnp.bfloat16
MESH = pl.DeviceIdType.MESH
AXES = ("x", "y", "c")

D_MODEL = 1024
DEPTH = 4
MLA_HEADS = 4
MLA_NOPE = 128
MLA_ROPE = 64
MLA_V = 128
MLA_Q_RANK = 384
MLA_KV_RANK = 256
MLA_WIDTH = 512
GDN_HEADS = 4
GDN_DK = 128
GDN_WIDTH = 512
GDN_QKV = 1536
GDN_CONV = 4
IN_COLS = 3272
ROPE_THETA = 10000.0
NORM_EPS = 1e-6
ADAM_LR, ADAM_B1, ADAM_B2, ADAM_EPS, ADAM_WD, ADAM_STEP = 0.001, 0.9, 0.999, 1e-08, 0.01, 10

LANES = 128
N_CHIPS = 4
GDN_CHUNK = 128
VMEM_LIMIT = 56 * 1024 * 1024


def _cparams(sem=None):
    if sem is None:
        return pltpu.CompilerParams(vmem_limit_bytes=VMEM_LIMIT)
    return pltpu.CompilerParams(dimension_semantics=sem, vmem_limit_bytes=VMEM_LIMIT)


def _pick(dim, target):
    if dim <= target:
        return dim
    best = None
    for t in range(LANES, target + 1, LANES):
        if dim % t == 0:
            best = t
    assert best is not None, (dim, target)
    return best


def _resident(shape):
    return pl.BlockSpec(shape, lambda i: (0,) * len(shape), pipeline_mode=pl.Buffered(1))


_DN = {"nn": (((1,), (0,)), ((), ())), "nt": (((1,), (1,)), ((), ())), "tn": (((0,), (0,)), ((), ()))}


def _dot_raw(a, b, mode, exact):
    if exact:
        return lax.dot_general(a, b, _DN[mode], precision=lax.Precision.HIGHEST, preferred_element_type=F32)
    return lax.dot_general(a.astype(BF16), b.astype(BF16), _DN[mode], preferred_element_type=F32)


def _dot_split(a, b):
    a_hi, b_hi = a.astype(BF16), b.astype(BF16)
    a_lo, b_lo = (a - a_hi.astype(F32)).astype(BF16), (b - b_hi.astype(F32)).astype(BF16)
    dot = lambda u, w: lax.dot_general(u, w, _DN["nn"], preferred_element_type=F32)
    return dot(a_hi, b_hi) + (dot(a_hi, b_lo) + dot(a_lo, b_hi))


@functools.partial(jax.custom_vjp, nondiff_argnums=(2, 3))
def bdot(a, b, mode="nn", exact=False):
    return _dot_raw(a, b, mode, exact)


def _bdot_fwd(a, b, mode, exact):
    return _dot_raw(a, b, mode, exact), (a, b)


def _bdot_bwd(mode, exact, res, g):
    a, b = res
    if mode == "nn":
        return bdot(g, b, "nt", exact), bdot(a, g, "tn", exact)
    if mode == "nt":
        return bdot(g, b, "nn", exact), bdot(g, a, "tn", exact)
    return bdot(b, g, "nt", exact), bdot(a, g, "nn", exact)


bdot.defvjp(_bdot_fwd, _bdot_bwd)


@jax.custom_vjp
def roll_half(x):
    return pltpu.roll(x, 64, 1)


roll_half.defvjp(lambda x: (pltpu.roll(x, 64, 1), None), lambda _, g: (pltpu.roll(g, 64, 1),))


def _sigmoid(x):
    return 1.0 / (1.0 + jnp.exp(-x))


def _silu(x):
    return x * _sigmoid(x)


def _softplus(x):
    return jnp.maximum(x, 0.0) + jnp.log(1.0 + jnp.exp(-jnp.abs(x)))


def _rms(x, w):
    return x * lax.rsqrt(jnp.mean(x * x, axis=-1, keepdims=True) + NORM_EPS) * w


def _rw_fwd(fn, name, rows, params, out_dims, out_dtypes, acc_dims, tile):
    s = rows[0].shape[0]
    t = min(tile, s)
    n = s // t
    nr, npar, no, na = len(rows), len(params), len(out_dims), len(acc_dims)

    def body(*refs):
        r, p = refs[:nr], refs[nr:nr + npar]
        o, a = refs[nr + npar:nr + npar + no], refs[nr + npar + no:]
        outs = fn(*[x[...] for x in r], *[x[...] for x in p])
        for ref, val in zip(o, outs[:no]):
            ref[...] = val.astype(ref.dtype)
        if na:
            @pl.when(pl.program_id(0) == 0)
            def _():
                for ref in a:
                    ref[...] = jnp.zeros_like(ref)
            for ref, val in zip(a, outs[no:]):
                ref[...] += val

    in_specs = [pl.BlockSpec((t, x.shape[1]), lambda i: (i, 0)) for x in rows]
    in_specs += [_resident(x.shape) for x in params]
    out_specs = [pl.BlockSpec((t, d), lambda i: (i, 0)) for d in out_dims]
    out_specs += [_resident((1, d)) for d in acc_dims]
    out_shape = [jax.ShapeDtypeStruct((s, d), dt) for d, dt in zip(out_dims, out_dtypes)]
    out_shape += [jax.ShapeDtypeStruct((1, d), F32) for d in acc_dims]
    res = pl.pallas_call(body, name=name, grid=(n,), in_specs=in_specs, out_specs=out_specs, out_shape=out_shape,
                         compiler_params=_cparams(("arbitrary",)))(*rows, *params)
    return tuple(res)


def _rw_bwd(fn, name, rows, params, row_cts, acc_cts, n_diff, tile):
    s = rows[0].shape[0]
    t = min(tile, s)
    n = s // t
    nr, npar, no, na = len(rows), len(params), len(row_cts), len(acc_cts)

    def body(*refs):
        r, p = refs[:nr], refs[nr:nr + npar]
        g, ga = refs[nr + npar:nr + npar + no], refs[nr + npar + no:nr + npar + no + na]
        dr, dp = refs[nr + npar + no + na:nr + npar + no + na + n_diff], refs[nr + npar + no + na + n_diff:]
        _, vjp = jax.vjp(fn, *[x[...] for x in r], *[x[...] for x in p])
        cts = vjp(tuple([x[...] for x in g] + [x[...] for x in ga]))
        for ref, val in zip(dr, cts[:n_diff]):
            ref[...] = val
        if npar:
            @pl.when(pl.program_id(0) == 0)
            def _():
                for ref in dp:
                    ref[...] = jnp.zeros_like(ref)
            for ref, val in zip(dp, cts[nr:]):
                ref[...] += val

    in_specs = [pl.BlockSpec((t, x.shape[1]), lambda i: (i, 0)) for x in rows]
    in_specs += [_resident(x.shape) for x in params]
    in_specs += [pl.BlockSpec((t, x.shape[1]), lambda i: (i, 0)) for x in row_cts]
    in_specs += [_resident(x.shape) for x in acc_cts]
    out_specs = [pl.BlockSpec((t, x.shape[1]), lambda i: (i, 0)) for x in rows[:n_diff]]
    out_specs += [_resident(x.shape) for x in params]
    out_shape = [jax.ShapeDtypeStruct(x.shape, F32) for x in rows[:n_diff]]
    out_shape += [jax.ShapeDtypeStruct(x.shape, F32) for x in params]
    res = pl.pallas_call(body, name=name, grid=(n,), in_specs=in_specs, out_specs=out_specs, out_shape=out_shape,
                         compiler_params=_cparams(("arbitrary",)))(*rows, *params, *row_cts, *acc_cts)
    return tuple(res[:n_diff]), tuple(res[n_diff:])


def make_rowwise(fn, name, out_dims, acc_dims=(), n_nondiff=0, tile=256):
    out_dtypes = (F32,) * len(out_dims)

    @jax.custom_vjp
    def op(rows, params):
        return _rw_fwd(fn, name, rows, params, out_dims, out_dtypes, acc_dims, tile)

    def fwd(rows, params):
        return op(rows, params), (rows, params)

    def bwd(res, cts):
        rows, params = res
        n_diff = len(rows) - n_nondiff
        d_rows, d_params = _rw_bwd(fn, name + "_bwd", rows, params, cts[:len(out_dims)], cts[len(out_dims):],
                                   n_diff, tile)
        d_rows = d_rows + tuple(jnp.zeros_like(x) for x in rows[n_diff:])
        return d_rows, d_params

    op.defvjp(fwd, bwd)
    return op


def _mm(a, b, mode, name):
    if mode == "nn":
        (m, k), (_, n) = a.shape, b.shape
    elif mode == "nt":
        (m, k), (n, _) = a.shape, b.shape
    else:
        (k, m), (_, n) = a.shape, b.shape
    tm = _pick(m, 512)
    tn = _pick(n, 1152)
    tk = _pick(k, 1152) if mode != "tn" else _pick(k, 512)
    nk = k // tk

    def body(a_ref, b_ref, o_ref, acc_ref):
        kk = pl.program_id(2)

        @pl.when(kk == 0)
        def _():
            acc_ref[...] = jnp.zeros_like(acc_ref)

        acc_ref[...] += _dot_raw(a_ref[...], b_ref[...], mode, False)

        @pl.when(kk == nk - 1)
        def _():
            o_ref[...] = acc_ref[...]

    if mode == "nn":
        a_spec = pl.BlockSpec((tm, tk), lambda i, j, kk: (i, kk))
        b_spec = pl.BlockSpec((tk, tn), lambda i, j, kk: (kk, j))
    elif mode == "nt":
        a_spec = pl.BlockSpec((tm, tk), lambda i, j, kk: (i, kk))
        b_spec = pl.BlockSpec((tn, tk), lambda i, j, kk: (j, kk))
    else:
        a_spec = pl.BlockSpec((tk, tm), lambda i, j, kk: (kk, i))
        b_spec = pl.BlockSpec((tk, tn), lambda i, j, kk: (kk, j))
    return pl.pallas_call(
        body, name=name, grid=(m // tm, n // tn, nk), in_specs=[a_spec, b_spec],
        out_specs=pl.BlockSpec((tm, tn), lambda i, j, kk: (i, j)),
        out_shape=jax.ShapeDtypeStruct((m, n), F32), scratch_shapes=[pltpu.VMEM((tm, tn), F32)],
        compiler_params=_cparams(("parallel", "parallel", "arbitrary")))(a, b)


def make_linear(name):
    @jax.custom_vjp
    def op(a, w):
        return _mm(a, w.astype(BF16), "nn", name)

    def fwd(a, w):
        w16 = w.astype(BF16)
        return _mm(a, w16, "nn", name), (a, w16)

    def bwd(res, g):
        a, w16 = res
        return _mm(g, w16, "nt", name + "_dx"), _mm(a, g, "tn", name + "_dw")

    op.defvjp(fwd, bwd)
    return op


def _mla_prep_fn(qraw, kvraw, kr, t1, t2):
    kr_rot = kr * t1 + roll_half(kr) * t2
    qs, ks = [], []
    for h in range(MLA_HEADS):
        q_r = qraw[:, h * 256 + 128:(h + 1) * 256]
        qs += [qraw[:, h * 256:h * 256 + 128], q_r * t1 + roll_half(q_r) * t2]
        ks += [kvraw[:, h * 128:(h + 1) * 128], kr_rot]
    return jnp.concatenate(qs, axis=1), jnp.concatenate(ks, axis=1), kvraw[:, 512:]


def _flash_tile(s):
    return 512 if s >= 2048 else 128


FLASH_SCALE = (MLA_NOPE + MLA_ROPE) ** -0.5
LOG2_E = 1.4426950408889634
FLASH_EXP2 = FLASH_SCALE * LOG2_E
STAT_ROWS = 8
FLASH_PAIR = 2
FLASH_STRIP = 32


def _as_rows(col_b):
    ones = jnp.full((STAT_ROWS, LANES), 1.0 / LANES, F32)
    return _dot_raw(ones, col_b, "nt", True)


def _flash_fwd(qf, kf, vf):
    s = qf.shape[0]
    t = _flash_tile(s)
    nb = s // t
    pair = range(FLASH_PAIR)

    ck = min(FLASH_STRIP, t)

    def body(q_ref, k_ref, v_ref, o_ref, lse_ref, m_sc, l_sc, acc_sc, st_sc, pt_sc):
        i = pl.program_id(1)
        m_sc[...] = jnp.full_like(m_sc, -1e30)
        l_sc[...] = jnp.zeros_like(l_sc)
        acc_sc[...] = jnp.zeros_like(acc_sc)
        qs = [q_ref[:, hh * 256:(hh + 1) * 256] for hh in pair]

        def step(j, on_diagonal):
            rows = pl.ds(pl.multiple_of(j * t, t), t)
            for hh in pair:
                st = _dot_raw(k_ref[rows, hh * 256:(hh + 1) * 256], qs[hh], "nt", False)
                if on_diagonal:
                    keep = lax.broadcasted_iota(jnp.int32, (t, t), 0) <= lax.broadcasted_iota(jnp.int32, (t, t), 1)
                    st = jnp.where(keep, st, -1e30)
                st_sc[hh] = st
            m_olds = [m_sc[hh] for hh in pair]
            m_news = [jnp.maximum(m_olds[hh], jnp.max(st_sc[hh], axis=0, keepdims=True)) for hh in pair]
            alphas = [jnp.exp2((m_olds[hh] - m_news[hh]) * FLASH_EXP2) for hh in pair]
            shifts = [m_news[hh] * FLASH_EXP2 for hh in pair]
            sums = [jnp.zeros((8, t), F32) for _ in pair]
            for hh in pair:
                for r in range(0, t, ck):
                    p = jnp.exp2(st_sc[hh, r:r + ck, :] * FLASH_EXP2 - shifts[hh])
                    pt_sc[hh, r:r + ck, :] = p.astype(BF16)
                    sums[hh] = sums[hh] + functools.reduce(lambda a, b: a + b, [p[u:u + 8] for u in range(0, ck, 8)])
            pvs = [_dot_raw(v_ref[rows, hh * 128:(hh + 1) * 128], pt_sc[hh], "tn", False) for hh in pair]
            for hh in pair:
                l_sc[hh] = alphas[hh] * l_sc[hh] + jnp.sum(sums[hh], axis=0, keepdims=True)
                acc_sc[hh] = alphas[hh] * acc_sc[hh] + pvs[hh]
                m_sc[hh] = m_news[hh]

        def two_steps(p, carry):
            step(2 * p, False)
            step(2 * p + 1, False)
            return carry

        lax.fori_loop(0, i // 2, two_steps, 0)

        @pl.when(i % 2 == 1)
        def _():
            step(i - 1, False)

        step(i, True)
        for hh in pair:
            o_ref[:, hh * 128:(hh + 1) * 128] = (acc_sc[hh] / l_sc[hh]).T
            lse2 = m_sc[hh] * FLASH_EXP2 + jnp.log(l_sc[hh]) * LOG2_E
            lse_ref[hh] = jnp.broadcast_to(lse2, (STAT_ROWS, t))

    p = FLASH_PAIR
    return pl.pallas_call(
        body, name="flash_fwd", grid=(MLA_HEADS // p, nb),
        in_specs=[pl.BlockSpec((t, p * 256), lambda h, i: (i, h)), pl.BlockSpec((s, p * 256), lambda h, i: (0, h)),
                  pl.BlockSpec((s, p * 128), lambda h, i: (0, h))],
        out_specs=[pl.BlockSpec((t, p * 128), lambda h, i: (i, h)),
                   pl.BlockSpec((p, STAT_ROWS, t), lambda h, i: (h, 0, i))],
        out_shape=[jax.ShapeDtypeStruct((s, MLA_WIDTH), F32),
                   jax.ShapeDtypeStruct((MLA_HEADS, STAT_ROWS, s), F32)],
        scratch_shapes=[pltpu.VMEM((p, 1, t), F32), pltpu.VMEM((p, 1, t), F32), pltpu.VMEM((p, 128, t), F32),
                        pltpu.VMEM((p, t, t), F32), pltpu.VMEM((p, t, t), BF16)],
        compiler_params=_cparams(("parallel", "arbitrary")))(qf, kf, vf)


def _flash_bwd(qf, kf, vf, lse, o, do):
    s = qf.shape[0]
    t = _flash_tile(s)
    nb = s // t

    def body(q_ref, k_ref, v_ref, lse_ref, o_ref, do_ref, dq_ref, dk_ref, dv_ref, dl_sc, do16_sc):
        j = pl.program_id(1)

        @pl.when(j == 0)
        def _():
            dq_ref[...] = jnp.zeros_like(dq_ref)

            def stage(b, carry):
                rows = pl.ds(pl.multiple_of(b * t, t), t)
                do_t = do_ref[rows, :]
                delta = jnp.sum(do_t * o_ref[rows, :], axis=1, keepdims=True)
                dl_sc[:, rows] = _as_rows(jnp.broadcast_to(delta, (t, LANES)))
                do16_sc[rows, :] = do_t.astype(BF16)
                return carry

            lax.fori_loop(0, nb, stage, 0)

        dk_ref[...] = jnp.zeros_like(dk_ref)
        dv_ref[...] = jnp.zeros_like(dv_ref)
        k = k_ref[...]
        v = v_ref[...]

        def step(i, on_diagonal):
            rows = pl.ds(pl.multiple_of(i * t, t), t)
            q = q_ref[rows, :]
            do_t = do16_sc[rows, :]
            st = _dot_raw(k, q, "nt", False) * FLASH_EXP2 - lse_ref[0, 0:1, rows]
            if on_diagonal:
                keep = lax.broadcasted_iota(jnp.int32, (t, t), 0) <= lax.broadcasted_iota(jnp.int32, (t, t), 1)
                st = jnp.where(keep, st, -1e30)
            pt = jnp.exp2(st)
            dst = pt * (_dot_raw(v, do_t, "nt", False) - dl_sc[0:1, rows])
            dv_ref[...] += _dot_raw(pt, do_t, "nn", False)
            dk_ref[...] += _dot_raw(dst, q, "nn", False)
            dq_ref[rows, :] += _dot_raw(dst, k, "tn", False)

        def two_steps(p, carry):
            step(j + 1 + 2 * p, False)
            step(j + 2 + 2 * p, False)
            return carry

        step(j, True)
        below = nb - 1 - j
        lax.fori_loop(0, below // 2, two_steps, 0)

        @pl.when(below % 2 == 1)
        def _():
            step(nb - 1, False)

        dk_ref[...] *= FLASH_SCALE

        @pl.when(j == nb - 1)
        def _():
            dq_ref[...] *= FLASH_SCALE

    per_head = lambda d: pl.BlockSpec((s, d), lambda h, j: (0, h), pipeline_mode=pl.Buffered(1))
    return pl.pallas_call(
        body, name="flash_bwd", grid=(MLA_HEADS, nb),
        in_specs=[per_head(256), pl.BlockSpec((t, 256), lambda h, j: (j, h)),
                  pl.BlockSpec((t, 128), lambda h, j: (j, h)),
                  pl.BlockSpec((1, STAT_ROWS, s), lambda h, j: (h, 0, 0)), per_head(128), per_head(128)],
        out_specs=[pl.BlockSpec((s, 256), lambda h, j: (0, h)), pl.BlockSpec((t, 256), lambda h, j: (j, h)),
                   pl.BlockSpec((t, 128), lambda h, j: (j, h))],
        out_shape=[jax.ShapeDtypeStruct((s, 1024), F32), jax.ShapeDtypeStruct((s, 1024), F32),
                   jax.ShapeDtypeStruct((s, MLA_WIDTH), F32)],
        scratch_shapes=[pltpu.VMEM((STAT_ROWS, s), F32), pltpu.VMEM((s, 128), BF16)],
        compiler_params=_cparams(("parallel", "arbitrary")))(qf, kf, vf, lse, o, do)


MLA_FRONT_TILE = 512


def _mla_front_fn(q_lat, kv_lat, kr, t1, t2, q_norm_w, q_up, kv_norm_w, kv_up):
    qraw = bdot(_rms(q_lat, q_norm_w), q_up)
    kvraw = bdot(_rms(kv_lat, kv_norm_w), kv_up)
    return _mla_prep_fn(qraw, kvraw, kr, t1, t2)


def _mla_front(rows, params):
    return _rw_fwd(_mla_front_fn, "mla_front", rows, params, (1024, 1024, 512), (BF16, BF16, BF16), (),
                   MLA_FRONT_TILE)


@jax.custom_vjp
def mla_attention(rows, params):
    return _flash_fwd(*_mla_front(rows, params))[0]


def _mla_attention_fwd(rows, params):
    qf, kf, vf = _mla_front(rows, params)
    o, lse = _flash_fwd(qf, kf, vf)
    return o, (rows, params, qf, kf, vf, o, lse)


def _mla_attention_bwd(res, do):
    rows, params, qf, kf, vf, o, lse = res
    d_rows, d_params = _rw_bwd(_mla_front_fn, "mla_front_bwd", rows, params, _flash_bwd(qf, kf, vf, lse, o, do), (),
                               3, MLA_FRONT_TILE)
    return d_rows + (jnp.zeros_like(rows[3]), jnp.zeros_like(rows[4])), d_params


mla_attention.defvjp(_mla_attention_fwd, _mla_attention_bwd)


def _lane_pick(x, lane):
    ids = lax.broadcasted_iota(jnp.int32, x.shape, 1)
    col = jnp.sum(jnp.where(ids == lane, x, 0.0), axis=1, keepdims=True)
    return jnp.broadcast_to(col, x.shape)


GDN_HALO = 8


@functools.partial(jax.custom_vjp, nondiff_argnums=(1,))
def _roll_rows(x, d):
    return pltpu.roll(x, d, 0)


_roll_rows.defvjp(lambda x, d: (pltpu.roll(x, d, 0), None), lambda d, _, g: (pltpu.roll(g, g.shape[0] - d, 0),))


def _gdn_prep_fn(prev, cur, ab, w0, w1, w2, w3, a_log, dt_bias):
    xcat = jnp.concatenate([prev, cur], axis=0)
    x0, x1, x2 = [_roll_rows(xcat, GDN_CONV - 1 - j)[GDN_HALO:] for j in range(GDN_CONV - 1)]
    qkv = _silu(x0 * w0 + x1 * w1 + x2 * w2 + cur * w3)
    g_all = -jnp.exp(a_log) * _softplus(ab + dt_bias)
    beta_all = _sigmoid(ab)
    qs, ks, gs, bs = [], [], [], []
    for h in range(GDN_HEADS):
        q = qkv[:, h * 128:(h + 1) * 128]
        k = qkv[:, 512 + h * 128:512 + (h + 1) * 128]
        qs.append(q * lax.rsqrt(jnp.sum(q * q, axis=-1, keepdims=True) + NORM_EPS) * (GDN_DK ** -0.5))
        ks.append(k * lax.rsqrt(jnp.sum(k * k, axis=-1, keepdims=True) + NORM_EPS))
        gs.append(_lane_pick(g_all, h))
        bs.append(_lane_pick(beta_all, GDN_HEADS + h))
    cat = lambda xs: jnp.concatenate(xs, axis=1)
    return cat(qs), cat(ks), qkv[:, 1024:], cat(gs), cat(bs)


GDN_PREP_TILE = 256


def _gdn_prep_specs(s, params, reverse=False):
    t = min(GDN_PREP_TILE, s)
    n = s // t
    blk = (lambda i: n - 1 - i) if reverse else (lambda i: i)
    prev = pl.BlockSpec((GDN_HALO, GDN_QKV), lambda i: (jnp.maximum(blk(i) * (t // GDN_HALO) - 1, 0), 0))
    rows = lambda d: pl.BlockSpec((t, d), lambda i: (blk(i), 0))
    return t, rows, [prev, rows(GDN_QKV), rows(LANES)] + [pl.BlockSpec(p.shape, lambda i: (0, 0)) for p in params]


def _gdn_prep_masked(has_rows_before):
    return lambda prev, *rest: _gdn_prep_fn(prev * has_rows_before, *rest)


def _gdn_prep_fwd(qkv, ab, params):
    s = qkv.shape[0]
    t, rows, in_specs = _gdn_prep_specs(s, params)

    def body(prev_ref, cur_ref, ab_ref, *refs):
        p_refs, o_refs = refs[:len(params)], refs[len(params):]
        has_rows_before = (pl.program_id(0) > 0).astype(F32)
        outs = _gdn_prep_masked(has_rows_before)(prev_ref[...], cur_ref[...], ab_ref[...], *[p[...] for p in p_refs])
        for ref, val in zip(o_refs, outs):
            ref[...] = val

    return pl.pallas_call(
        body, name="gdn_prep", grid=(s // t,), in_specs=in_specs,
        out_specs=[rows(GDN_WIDTH)] * 5, out_shape=[jax.ShapeDtypeStruct((s, GDN_WIDTH), F32)] * 5,
        compiler_params=_cparams(("parallel",)))(qkv, qkv, ab, *params)


def _gdn_prep_bwd(qkv, ab, params, cts):
    s = qkv.shape[0]
    t, rows, in_specs = _gdn_prep_specs(s, params, reverse=True)
    n = s // t
    npar = len(params)

    def body(prev_ref, cur_ref, ab_ref, *refs):
        p_refs, g_refs = refs[:npar], refs[npar:npar + 5]
        dcur_ref, dab_ref = refs[npar + 5:npar + 7]
        dp_refs, carry_sc = refs[npar + 7:-1], refs[-1]

        @pl.when(pl.program_id(0) == 0)
        def _():
            carry_sc[...] = jnp.zeros_like(carry_sc)
            for ref in dp_refs:
                ref[...] = jnp.zeros_like(ref)

        has_rows_before = (pl.program_id(0) < n - 1).astype(F32)
        _, vjp = jax.vjp(_gdn_prep_masked(has_rows_before), prev_ref[...], cur_ref[...], ab_ref[...],
                         *[p[...] for p in p_refs])
        d_prev, d_cur, d_ab, *d_params = vjp(tuple(g[...] for g in g_refs))
        dcur_ref[...] = d_cur
        dcur_ref[t - GDN_HALO:, :] += carry_sc[...]
        carry_sc[...] = d_prev
        dab_ref[...] = d_ab
        for ref, val in zip(dp_refs, d_params):
            ref[...] += val

    res = pl.pallas_call(
        body, name="gdn_prep_bwd", grid=(n,), in_specs=in_specs + [rows(GDN_WIDTH)] * 5,
        out_specs=[rows(GDN_QKV), rows(LANES)] + [pl.BlockSpec(p.shape, lambda i: (0, 0)) for p in params],
        out_shape=[jax.ShapeDtypeStruct((s, GDN_QKV), F32), jax.ShapeDtypeStruct((s, LANES), F32)]
        + [jax.ShapeDtypeStruct(p.shape, F32) for p in params],
        scratch_shapes=[pltpu.VMEM((GDN_HALO, GDN_QKV), F32)],
        compiler_params=_cparams(("arbitrary",)))(qkv, qkv, ab, *params, *cts)
    return res[0], res[1], tuple(res[2:])


@jax.custom_vjp
def gdn_prep(qkv, ab, params):
    return tuple(_gdn_prep_fwd(qkv, ab, params))


def _gdn_prep_vjp_fwd(qkv, ab, params):
    return tuple(_gdn_prep_fwd(qkv, ab, params)), (qkv, ab, params)


def _gdn_prep_vjp_bwd(res, cts):
    qkv, ab, params = res
    return _gdn_prep_bwd(qkv, ab, params, cts)


gdn_prep.defvjp(_gdn_prep_vjp_fwd, _gdn_prep_vjp_bwd)


@jax.custom_vjp
def _unit_lower_inverse(lms):
    c = lms[0].shape[0]
    row = lax.broadcasted_iota(jnp.int32, (c, c), 0)
    col = lax.broadcasted_iota(jnp.int32, (c, c), 1)
    ts = [(row == col).astype(F32) - jnp.where((row >> 1) == (col >> 1), lm, 0.0) for lm in lms]
    for level in range(1, int(math.log2(c))):
        below = ((row >> (level + 1)) == (col >> (level + 1))) & ((row >> level) != (col >> level))
        mids = [_dot_split(t, jnp.where(below, lm, 0.0)) for t, lm in zip(ts, lms)]
        ts = [t - _dot_split(mid, t) for t, mid in zip(ts, mids)]
    return tuple(ts)


def _uli_fwd(lms):
    ts = _unit_lower_inverse(lms)
    return ts, ts


def _uli_bwd(ts, gs):
    mids = [bdot(t, g, "tn") for t, g in zip(ts, gs)]
    return (tuple(-bdot(mid, t, "nt") for t, mid in zip(ts, mids)),)


_unit_lower_inverse.defvjp(_uli_fwd, _uli_bwd)


@jax.custom_vjp
def _known_inverse(lms, ts):
    return ts


_known_inverse.defvjp(lambda lms, ts: (ts, ts),
                      lambda ts, gs: (_uli_bwd(ts, gs)[0], tuple(jnp.zeros_like(t) for t in ts)))


def _gdn_chunk_fn(qs, ks, vs, gbs, bbs, s0s, known_ts=None):
    heads = range(len(qs))
    c = qs[0].shape[0]
    row = lax.broadcasted_iota(jnp.int32, (c, c), 0)
    col = lax.broadcasted_iota(jnp.int32, (c, c), 1)
    incl, strict = row >= col, row > col
    tri = incl.astype(F32)
    gc = [bdot(tri, gbs[h], "nn", True) for h in heads]
    decay = [jnp.exp(jnp.where(incl, gc[h] - gc[h].T, -1e30)) for h in heads]
    g_last = [jnp.sum(gbs[h], axis=0, keepdims=True) for h in heads]
    eg = [jnp.exp(gc[h]) for h in heads]
    kb = [ks[h] * bbs[h] for h in heads]
    lms = tuple(jnp.where(strict, bdot(kb[h], ks[h], "nt") * decay[h], 0.0) for h in heads)
    ts = _unit_lower_inverse(lms) if known_ts is None else _known_inverse(lms, known_ts)
    u = [bdot(ts[h], vs[h] * bbs[h]) for h in heads]
    w = [bdot(ts[h], kb[h] * eg[h]) for h in heads]
    qk = [bdot(qs[h], ks[h], "nt") * decay[h] for h in heads]
    v_new = [u[h] - bdot(w[h], s0s[h]) for h in heads]
    o = [bdot(qs[h] * eg[h], s0s[h]) + bdot(qk[h], v_new[h]) for h in heads]
    s1 = [s0s[h] * jnp.exp(g_last[h]) + bdot(ks[h] * jnp.exp(g_last[h] - gc[h]), v_new[h], "tn") for h in heads]
    return (tuple(o), tuple(s1)), ts


def _head_tiles(ref):
    return tuple(ref[:, h * 128:(h + 1) * 128] for h in range(GDN_HEADS))


def _gdn_fwd(q, k, v, gb, bb):
    s = q.shape[0]
    c = min(GDN_CHUNK, s)
    nc = s // c

    def body(q_ref, k_ref, v_ref, g_ref, b_ref, o_ref, st_ref, inv_ref, s_sc):
        @pl.when(pl.program_id(0) == 0)
        def _():
            s_sc[...] = jnp.zeros_like(s_sc)

        s0s = tuple(s_sc[h] for h in range(GDN_HEADS))
        for h in range(GDN_HEADS):
            st_ref[h, 0] = s0s[h]
        (os, s1s), ts = _gdn_chunk_fn(*[_head_tiles(ref) for ref in (q_ref, k_ref, v_ref, g_ref, b_ref)], s0s)
        for h in range(GDN_HEADS):
            o_ref[:, h * 128:(h + 1) * 128] = os[h]
            inv_ref[h, 0] = ts[h]
            s_sc[h] = s1s[h]

    blk = pl.BlockSpec((c, GDN_WIDTH), lambda n: (n, 0))
    return pl.pallas_call(
        body, name="gdn_fwd", grid=(nc,), in_specs=[blk] * 5,
        out_specs=[blk, pl.BlockSpec((GDN_HEADS, 1, 128, 128), lambda n: (0, n, 0, 0)),
                   pl.BlockSpec((GDN_HEADS, 1, c, c), lambda n: (0, n, 0, 0))],
        out_shape=[jax.ShapeDtypeStruct((s, GDN_WIDTH), F32), jax.ShapeDtypeStruct((GDN_HEADS, nc, 128, 128), F32),
                   jax.ShapeDtypeStruct((GDN_HEADS, nc, c, c), F32)],
        scratch_shapes=[pltpu.VMEM((GDN_HEADS, 128, 128), F32)],
        compiler_params=_cparams(("arbitrary",)))(q, k, v, gb, bb)


def _gdn_bwd(q, k, v, gb, bb, states, inverses, do):
    s = q.shape[0]
    c = min(GDN_CHUNK, s)
    nc = s // c

    def body(q_ref, k_ref, v_ref, g_ref, b_ref, st_ref, inv_ref, do_ref, dq_ref, dk_ref, dv_ref, dg_ref, db_ref, ds_sc):
        @pl.when(pl.program_id(0) == 0)
        def _():
            ds_sc[...] = jnp.zeros_like(ds_sc)

        s0s = tuple(st_ref[h, 0] for h in range(GDN_HEADS))
        ts = tuple(inv_ref[h, 0] for h in range(GDN_HEADS))
        chunk = lambda *args: _gdn_chunk_fn(*args, known_ts=ts)[0]
        _, vjp = jax.vjp(chunk, *[_head_tiles(ref) for ref in (q_ref, k_ref, v_ref, g_ref, b_ref)], s0s)
        *d_tiles, ds0s = vjp((_head_tiles(do_ref), tuple(ds_sc[h] for h in range(GDN_HEADS))))
        for h in range(GDN_HEADS):
            for ref, d in zip((dq_ref, dk_ref, dv_ref, dg_ref, db_ref), d_tiles):
                ref[:, h * 128:(h + 1) * 128] = d[h]
            ds_sc[h] = ds0s[h]

    blk = pl.BlockSpec((c, GDN_WIDTH), lambda n: (nc - 1 - n, 0))
    return pl.pallas_call(
        body, name="gdn_bwd", grid=(nc,),
        in_specs=[blk] * 5 + [pl.BlockSpec((GDN_HEADS, 1, 128, 128), lambda n: (0, nc - 1 - n, 0, 0)),
                              pl.BlockSpec((GDN_HEADS, 1, c, c), lambda n: (0, nc - 1 - n, 0, 0)), blk],
        out_specs=[blk] * 5, out_shape=[jax.ShapeDtypeStruct((s, GDN_WIDTH), F32)] * 5,
        scratch_shapes=[pltpu.VMEM((GDN_HEADS, 128, 128), F32)],
        compiler_params=_cparams(("arbitrary",)))(q, k, v, gb, bb, states, inverses, do)


@jax.custom_vjp
def gdn_core(q, k, v, gb, bb):
    return _gdn_fwd(q, k, v, gb, bb)[0]


def _gdn_core_fwd(q, k, v, gb, bb):
    o, states, inverses = _gdn_fwd(q, k, v, gb, bb)
    return o, (q, k, v, gb, bb, states, inverses)


def _gdn_core_bwd(res, do):
    return tuple(_gdn_bwd(*res, do))


gdn_core.defvjp(_gdn_core_fwd, _gdn_core_bwd)


def _cact_fn(c):
    return (_silu(c),)


def _prenorm_fn(x, w, scale_raw, scale_b, shift_raw, shift_b):
    return (_rms(x, w) * (1.0 + scale_raw + scale_b) + shift_raw + shift_b,)


def _tail_fn(x, o_mla, z_mla, o_gdn, z_gdn, o_norm_w, w_out, post_w, gate_raw, gate_b):
    y_mla = o_mla * _silu(z_mla)
    parts = [_rms(o_gdn[:, h * 128:(h + 1) * 128], o_norm_w) for h in range(GDN_HEADS)]
    y_gdn = jnp.concatenate(parts, axis=1) * _silu(z_gdn)
    y = bdot(y_mla, w_out[:MLA_WIDTH]) + bdot(y_gdn, w_out[MLA_WIDTH:])
    return (x + (gate_raw + gate_b) * _rms(y, post_w),)


def _loss_fn(y, tgt):
    err = y - tgt
    part = jnp.sum(0.5 * jnp.mean(err * err, axis=-1, keepdims=True), axis=0, keepdims=True)
    return (jnp.broadcast_to(part, (1, LANES)),)


def _front_fwd(x, norm_params, ws16):
    s = x.shape[0]
    t = min(512, s)

    def body(x_ref, *refs):
        p_refs, w_refs, o_refs = refs[:len(norm_params)], refs[len(norm_params):-len(ws16)], refs[-len(ws16):]
        (h,) = _prenorm_fn(x_ref[...], *[p[...] for p in p_refs])
        h16 = h.astype(BF16)
        for w_ref, o_ref in zip(w_refs, o_refs):
            o_ref[...] = _dot_raw(h16, w_ref[...], "nn", False)

    rows = lambda d: pl.BlockSpec((t, d), lambda i: (i, 0))
    res = pl.pallas_call(
        body, name="front", grid=(s // t,),
        in_specs=[rows(D_MODEL)] + [_resident(p.shape) for p in norm_params] + [_resident(w.shape) for w in ws16],
        out_specs=[rows(w.shape[1]) for w in ws16],
        out_shape=[jax.ShapeDtypeStruct((s, w.shape[1]), F32) for w in ws16],
        compiler_params=_cparams(("parallel",)))(x, *norm_params, *ws16)
    return tuple(res)


def _front_bwd(x, norm_params, ws16, dys):
    s = x.shape[0]
    t = min(512, s)
    npar, ng = len(norm_params), len(ws16)

    def body(x_ref, *refs):
        p_refs, w_refs, g_refs = refs[:npar], refs[npar:npar + ng], refs[npar + ng:npar + 2 * ng]
        dx_ref = refs[npar + 2 * ng]
        dp_refs, dw_refs = refs[npar + 2 * ng + 1:2 * npar + 2 * ng + 1], refs[2 * npar + 2 * ng + 1:]

        @pl.when(pl.program_id(0) == 0)
        def _():
            for ref in dp_refs + dw_refs:
                ref[...] = jnp.zeros_like(ref)

        (h,), vjp = jax.vjp(_prenorm_fn, x_ref[...], *[p[...] for p in p_refs])
        h16 = h.astype(BF16)
        dys16 = [g[...].astype(BF16) for g in g_refs]
        dh = functools.reduce(lambda a, b: a + b,
                              [_dot_raw(dy, w_ref[...], "nt", False) for dy, w_ref in zip(dys16, w_refs)])
        for dy, dw_ref in zip(dys16, dw_refs):
            dw_ref[...] += _dot_raw(h16, dy, "tn", False)
        dx, *d_params = vjp((dh,))
        dx_ref[...] = dx
        for ref, val in zip(dp_refs, d_params):
            ref[...] += val

    rows = lambda d: pl.BlockSpec((t, d), lambda i: (i, 0))
    res = pl.pallas_call(
        body, name="front_bwd", grid=(s // t,),
        in_specs=[rows(D_MODEL)] + [_resident(p.shape) for p in norm_params] + [_resident(w.shape) for w in ws16]
        + [rows(w.shape[1]) for w in ws16],
        out_specs=[rows(D_MODEL)] + [_resident(p.shape) for p in norm_params] + [_resident(w.shape) for w in ws16],
        out_shape=[jax.ShapeDtypeStruct(x.shape, F32)] + [jax.ShapeDtypeStruct(p.shape, F32) for p in norm_params]
        + [jax.ShapeDtypeStruct(w.shape, F32) for w in ws16],
        compiler_params=_cparams(("arbitrary",)))(x, *norm_params, *ws16, *dys)
    return res[0], tuple(res[1:1 + npar]), tuple(res[1 + npar:])


@jax.custom_vjp
def front(x, norm_params, ws):
    return _front_fwd(x, norm_params, tuple(w.astype(BF16) for w in ws))


def _front_vjp_fwd(x, norm_params, ws):
    ws16 = tuple(w.astype(BF16) for w in ws)
    return _front_fwd(x, norm_params, ws16), (x, norm_params, ws16)


def _front_vjp_bwd(res, dys):
    return _front_bwd(*res, dys)


front.defvjp(_front_vjp_fwd, _front_vjp_bwd)


_OPS = dict(
    cact=make_rowwise(_cact_fn, "c_act", (D_MODEL,), tile=16),
    tail=make_rowwise(_tail_fn, "tail", (D_MODEL,), tile=512),
    loss=make_rowwise(_loss_fn, "loss", (), acc_dims=(LANES,), n_nondiff=1),
    lin_mod=make_linear("lin_mod"),
)


def _swap_halves(w):
    half = w.shape[-1] // 2
    return jnp.concatenate([w[..., half:], w[..., :half]], axis=-1)


def _w_in_groups(w):
    k_pe = w[:, 640:704]
    ab = jnp.concatenate([w[:, 2752:2760], jnp.zeros((w.shape[0], LANES - 8), w.dtype)], axis=1)
    return (w[:, :384], w[:, 384:640], jnp.concatenate([k_pe, _swap_halves(k_pe)], axis=1), w[:, 704:1216],
            w[:, 1216:2752], ab, w[:, 2760:])


def _q_up_ext(w):
    parts = []
    for h in range(MLA_HEADS):
        rope = w[:, h * 192 + 128:(h + 1) * 192]
        parts += [w[:, h * 192:h * 192 + 128], rope, _swap_halves(rope)]
    return jnp.concatenate(parts, axis=1)


def _kv_up_perm(w):
    ks = [w[:, h * 256:h * 256 + 128] for h in range(MLA_HEADS)]
    vs = [w[:, h * 256 + 128:(h + 1) * 256] for h in range(MLA_HEADS)]
    return jnp.concatenate(ks + vs, axis=1)


def _pad_lanes(v):
    return jnp.pad(v, (0, LANES - v.shape[0]))[None, :]


def _local_loss(weights, x, c, positions, target):
    s = x.shape[0]
    half = MLA_ROPE // 2
    inv_freq = jnp.power(ROPE_THETA, -jnp.arange(half, dtype=F32) * 2.0 / MLA_ROPE)
    ang = positions.astype(F32)[:, None] * inv_freq
    cos, sin, zero = jnp.cos(ang), jnp.sin(ang), jnp.zeros((s, 2 * half), F32)
    t1 = jnp.concatenate([cos, cos, zero], axis=1)
    t2 = jnp.concatenate([-sin, sin, zero], axis=1)

    (c_act,) = _OPS["cact"]((jnp.pad(c, ((0, 15), (0, 0))),), ())
    for l in range(DEPTH):
        mod = _OPS["lin_mod"](c_act, weights["w_mod"][l])[0:1]
        b = weights["b_mod"][l][None, :]
        shift_raw, scale_raw, gate_raw = mod[:, :1024], mod[:, 1024:2048], mod[:, 2048:]
        shift_b, scale_b, gate_b = b[:, :1024], b[:, 1024:2048], b[:, 2048:]
        q_lat, kv_lat, kr, z_mla, qkv, ab, z_gdn = front(
            x, (weights["pre_norm_w"][l][None], scale_raw, scale_b, shift_raw, shift_b),
            _w_in_groups(weights["w_in"][l]))
        o_mla = mla_attention((q_lat, kv_lat, kr, t1, t2),
                              (weights["mla_q_norm_w"][l][None], _q_up_ext(weights["mla_q_up"][l]),
                               weights["mla_kv_norm_w"][l][None], _kv_up_perm(weights["mla_kv_up"][l])))
        cw = weights["gdn_conv_w"][l]
        params = tuple(cw[j][None] for j in range(GDN_CONV))
        params += (_pad_lanes(weights["gdn_a_log"][l]), _pad_lanes(weights["gdn_dt_bias"][l]))
        o_gdn = gdn_core(*gdn_prep(qkv, ab, params))
        (x,) = _OPS["tail"]((x, o_mla, z_mla, o_gdn, z_gdn),
                            (weights["gdn_o_norm_w"][l][None], weights["w_out"][l], weights["post_norm_w"][l][None],
                             gate_raw, gate_b))
    (acc,) = _OPS["loss"]((x, target), ())
    return acc[0, 0]


def _chip_index():
    return 2 * lax.axis_index("x") + lax.axis_index("y")


def _other_chips(x, y):
    return [(1 - x, y), (x, 1 - y), (1 - x, 1 - y)]


def _any_spec():
    return pl.BlockSpec(memory_space=pl.ANY)


def _half(ref, hc):
    n = ref.shape[0] // 2
    return ref.at[pl.ds(hc * n, n)]


def ag_weights(shards):
    n = len(shards)

    def body(*refs):
        ins, outs = refs[:n], refs[n:2 * n]
        send_sems, recv_sems = refs[2 * n:]
        x, y, c = lax.axis_index("x"), lax.axis_index("y"), lax.axis_index("c")
        sibling = (x, y, 1 - c)
        chips = _other_chips(x, y)

        def copy(t, k, src, chip_xy, hc, to):
            return pltpu.make_async_remote_copy(
                src_ref=src, dst_ref=_half(outs[t].at[2 * chip_xy[0] + chip_xy[1]], hc),
                send_sem=send_sems.at[6 * t + k], recv_sem=recv_sems.at[6 * t + k], device_id=to, device_id_type=MESH)

        first = [copy(t, k, _half(ins[t], c), (x, y), c, (*chip, c)) for k, chip in enumerate(chips) for t in range(n)]
        for cp in first:
            cp.start()
        passed = []
        for k, chip in enumerate(chips):
            for t in range(n):
                landed = _half(outs[t].at[2 * chip[0] + chip[1]], c)
                copy(t, k, landed, chip, c, (x, y, c)).wait_recv()
                passed.append(copy(t, 3 + k, landed, chip, c, sibling))
                passed[-1].start()
        for k, chip in enumerate(chips):
            for t in range(n):
                copy(t, 3 + k, _half(ins[t], c), chip, 1 - c, (x, y, c)).wait_recv()
        for cp in first + passed:
            cp.wait_send()

    return pl.pallas_call(
        body, name="ag_weights", in_specs=[_any_spec()] * n, out_specs=[_any_spec()] * n,
        out_shape=[jax.ShapeDtypeStruct((N_CHIPS,) + a.shape, a.dtype) for a in shards],
        scratch_shapes=[pltpu.SemaphoreType.DMA((6 * n,)), pltpu.SemaphoreType.DMA((6 * n,))],
        compiler_params=pltpu.CompilerParams(has_side_effects=True))(*shards)


def rs_pair(gs):
    n = len(gs)

    def body(*refs):
        ins, outs = refs[:n], refs[n:2 * n]
        send_sems, recv_sems = refs[2 * n:]
        x, y, c = lax.axis_index("x"), lax.axis_index("y"), lax.axis_index("c")
        lh = [g.shape[1] // 2 for g in gs]
        copies = [pltpu.make_async_remote_copy(
            src_ref=ins[t].at[:, pl.ds((1 - c) * lh[t], lh[t])], dst_ref=outs[t], send_sem=send_sems.at[t],
            recv_sem=recv_sems.at[t], device_id=(x, y, 1 - c), device_id_type=MESH) for t in range(n)]
        for cp in copies:
            cp.start()
        for cp in copies:
            cp.wait()

    return pl.pallas_call(
        body, name="rs_pair", in_specs=[_any_spec()] * n, out_specs=[_any_spec()] * n,
        out_shape=[jax.ShapeDtypeStruct((N_CHIPS, g.shape[1] // 2) + g.shape[2:], F32) for g in gs],
        scratch_shapes=[pltpu.SemaphoreType.DMA((n,)), pltpu.SemaphoreType.DMA((n,))],
        compiler_params=pltpu.CompilerParams(has_side_effects=True))(*gs)


def rs_cross(pairs):
    n = len(pairs)

    def body(*refs):
        ins, outs = refs[:n], refs[n:2 * n]
        send_sems, recv_sems = refs[2 * n:]
        x, y, c = lax.axis_index("x"), lax.axis_index("y"), lax.axis_index("c")
        copies = []
        for k, chip in enumerate(_other_chips(x, y)):
            for t in range(n):
                copies.append(pltpu.make_async_remote_copy(
                    src_ref=ins[t].at[2 * chip[0] + chip[1]], dst_ref=outs[t].at[k], send_sem=send_sems.at[3 * t + k],
                    recv_sem=recv_sems.at[3 * t + k], device_id=(*chip, c), device_id_type=MESH))
        for cp in copies:
            cp.start()
        for cp in copies:
            cp.wait()

    return pl.pallas_call(
        body, name="rs_cross", in_specs=[_any_spec()] * n, out_specs=[_any_spec()] * n,
        out_shape=[jax.ShapeDtypeStruct((3,) + p.shape[1:], p.dtype) for p in pairs],
        scratch_shapes=[pltpu.SemaphoreType.DMA((3 * n,)), pltpu.SemaphoreType.DMA((3 * n,))],
        compiler_params=pltpu.CompilerParams(has_side_effects=True))(*pairs)


def rs_share(blocks):
    n = len(blocks)

    def body(*refs):
        ins, outs = refs[:n], refs[n:2 * n]
        send_sems, recv_sems = refs[2 * n:]
        x, y, c = lax.axis_index("x"), lax.axis_index("y"), lax.axis_index("c")
        sends = [pltpu.make_async_remote_copy(
            src_ref=_half(ins[t], c), dst_ref=_half(outs[t], c), send_sem=send_sems.at[t], recv_sem=recv_sems.at[t],
            device_id=(x, y, 1 - c), device_id_type=MESH) for t in range(n)]
        for cp in sends:
            cp.start()
        for t in range(n):
            pltpu.make_async_remote_copy(
                src_ref=_half(ins[t], c), dst_ref=_half(outs[t], 1 - c), send_sem=send_sems.at[t],
                recv_sem=recv_sems.at[t], device_id=(x, y, 1 - c), device_id_type=MESH).wait_recv()
        for cp in sends:
            cp.wait_send()

    return pl.pallas_call(
        body, name="rs_share", in_specs=[_any_spec()] * n, out_specs=[_any_spec()] * n,
        out_shape=[jax.ShapeDtypeStruct(b.shape, F32) for b in blocks],
        input_output_aliases={t: t for t in range(n)},
        scratch_shapes=[pltpu.SemaphoreType.DMA((n,)), pltpu.SemaphoreType.DMA((n,))],
        compiler_params=pltpu.CompilerParams(has_side_effects=True))(*blocks)


def _row_tile(rows, target):
    best = None
    for t in range(8, min(rows, target) + 1, 8):
        if rows % t == 0:
            best = t
    return rows if best is None else best


TILE_BYTES = 2 * 1024 * 1024


def _flat_rows(shape):
    rows = int(np.prod(shape[1:-1]))
    cols_padded = -(-shape[-1] // LANES) * LANES
    return rows, _row_tile(rows, max(8, TILE_BYTES // (4 * cols_padded)))


def pair_add(g, from_sibling, out_dtype):
    cols = g.shape[-1]
    rph, t = _flat_rows(from_sibling.shape)
    nt = rph // t
    c_arr = lax.axis_index("c").astype(jnp.int32).reshape(1)

    def body(c_ref, a_ref, b_ref, o_ref):
        o_ref[...] = (a_ref[...] + b_ref[...]).astype(o_ref.dtype)

    out = pl.pallas_call(
        body, name="pair_add",
        grid_spec=pltpu.PrefetchScalarGridSpec(
            num_scalar_prefetch=1, grid=(N_CHIPS, nt),
            in_specs=[pl.BlockSpec((t, cols), lambda j, i, c_ref: (j * 2 * nt + c_ref[0] * nt + i, 0)),
                      pl.BlockSpec((t, cols), lambda j, i, c_ref: (j * nt + i, 0))],
            out_specs=pl.BlockSpec((t, cols), lambda j, i, c_ref: (j * nt + i, 0))),
        out_shape=jax.ShapeDtypeStruct((N_CHIPS * rph, cols), out_dtype),
        compiler_params=_cparams(("parallel", "parallel")))(c_arr, g.reshape(-1, cols), from_sibling.reshape(-1, cols))
    return out.reshape(from_sibling.shape)


def chip_add(pairs, received):
    cols = pairs.shape[-1]
    rph, t = _flat_rows(pairs.shape)
    nt = rph // t
    j_arr = _chip_index().astype(jnp.int32).reshape(1)
    c_arr = lax.axis_index("c").astype(jnp.int32).reshape(1)
    r2 = received.reshape(-1, cols)

    def body(j_ref, c_ref, a_ref, r0_ref, r1_ref, r2_ref, o_ref):
        a, r0, r1, r2 = [ref[...].astype(F32) for ref in (a_ref, r0_ref, r1_ref, r2_ref)]
        o_ref[...] = (a + r0) + (r1 + r2)

    out = pl.pallas_call(
        body, name="chip_add",
        grid_spec=pltpu.PrefetchScalarGridSpec(
            num_scalar_prefetch=2, grid=(nt,),
            in_specs=[pl.BlockSpec((t, cols), lambda i, j_ref, c_ref: (j_ref[0] * nt + i, 0)),
                      pl.BlockSpec((t, cols), lambda i, j_ref, c_ref: (i, 0)),
                      pl.BlockSpec((t, cols), lambda i, j_ref, c_ref: (nt + i, 0)),
                      pl.BlockSpec((t, cols), lambda i, j_ref, c_ref: (2 * nt + i, 0))],
            out_specs=pl.BlockSpec((t, cols), lambda i, j_ref, c_ref: (c_ref[0] * nt + i, 0))),
        out_shape=jax.ShapeDtypeStruct((2 * rph, cols), F32),
        compiler_params=_cparams(("parallel",)))(j_arr, c_arr, pairs.reshape(-1, cols), r2, r2, r2)
    return out.reshape((2 * pairs.shape[1],) + pairs.shape[2:])


def reduce_grads(gs, cross_dtypes):
    pairs = [pair_add(g, r, dt) for g, r, dt in zip(gs, rs_pair(gs), cross_dtypes)]
    return rs_share([chip_add(p, r) for p, r in zip(pairs, rs_cross(pairs))])


def adamw(w, g, m, v):
    shape = w.shape
    cols = shape[-1]
    rows = int(np.prod(shape[:-1]))
    flat = lambda a: a.reshape(rows, cols)
    t = _row_tile(rows, 256)

    def body(w_ref, g_ref, m_ref, v_ref, d_ref, mo_ref, vo_ref):
        gv = g_ref[...]
        m_new = ADAM_B1 * m_ref[...] + (1.0 - ADAM_B1) * gv
        v_new = ADAM_B2 * v_ref[...] + (1.0 - ADAM_B2) * (gv * gv)
        m_hat = m_new / (1.0 - ADAM_B1 ** ADAM_STEP)
        v_hat = v_new / (1.0 - ADAM_B2 ** ADAM_STEP)
        d_ref[...] = -ADAM_LR * (m_hat / (jnp.sqrt(v_hat) + ADAM_EPS) + ADAM_WD * w_ref[...])
        mo_ref[...] = m_new
        vo_ref[...] = v_new

    spec = pl.BlockSpec((t, cols), lambda i: (i, 0))
    outs = pl.pallas_call(
        body, name="adamw", grid=(rows // t,), in_specs=[spec] * 4, out_specs=[spec] * 3,
        out_shape=[jax.ShapeDtypeStruct((rows, cols), F32)] * 3,
        compiler_params=_cparams(("parallel",)))(flat(w), flat(g), flat(m), flat(v))
    return tuple(o.reshape(shape) for o in outs)


SHARDED = (("w_mod", 2), ("w_in", 2), ("mla_q_up", 2), ("mla_kv_up", 2), ("gdn_conv_w", 2), ("w_out", 1))
REPLICATED = ("b_mod", "pre_norm_w", "post_norm_w", "mla_q_norm_w", "mla_kv_norm_w", "gdn_a_log", "gdn_dt_bias",
              "gdn_o_norm_w")
WEIGHT_ORDER = ("w_mod", "b_mod", "pre_norm_w", "post_norm_w", "w_in", "mla_q_norm_w", "mla_q_up", "mla_kv_norm_w",
                "mla_kv_up", "gdn_conv_w", "gdn_a_log", "gdn_dt_bias", "gdn_o_norm_w", "w_out")
EXACT_F32 = ("gdn_conv_w",)
SMALL_ROWS = 48


def _gather_weights(shards):
    names = [name for name, _ in SHARDED]
    own = [shards[n] if n in EXACT_F32 else shards[n].astype(BF16) for n in names]
    gathered = ag_weights(own)
    full = {}
    for (name, axis), blk, mine in zip(SHARDED, gathered, own):
        shp = shards[name].shape
        blk = lax.dynamic_update_index_in_dim(blk, mine, _chip_index(), 0)
        blk = jnp.moveaxis(blk.astype(F32), 0, axis)
        full[name] = blk.reshape(shp[:axis] + (N_CHIPS * shp[axis],) + shp[axis + 1:])
    return full


def _split_grads(grads):
    pieces = []
    for name, axis in SHARDED:
        g = grads[name]
        shp = g.shape
        g = g.reshape(shp[:axis] + (N_CHIPS, shp[axis] // N_CHIPS) + shp[axis + 1:])
        pieces.append(jnp.moveaxis(g, axis, 0))
    small = jnp.concatenate([grads[name] for name in REPLICATED], axis=1)
    small = jnp.pad(small, ((0, 0), (0, SMALL_ROWS * LANES - small.shape[1]))).reshape(DEPTH, SMALL_ROWS, LANES)
    pieces.append(jnp.broadcast_to(small[None], (N_CHIPS,) + small.shape))
    return pieces


def _unsplit_small(small, rep_shapes):
    flat = small.reshape(DEPTH, SMALL_ROWS * LANES)
    out, off = {}, 0
    for name in REPLICATED:
        size = rep_shapes[name][1]
        out[name] = flat[:, off:off + size]
        off += size
    return out


def kernel(x, c, positions, w_mod, b_mod, pre_norm_w, post_norm_w, w_in, mla_q_norm_w, mla_q_up, mla_kv_norm_w, mla_kv_up, gdn_conv_w, gdn_a_log, gdn_dt_bias, gdn_o_norm_w, w_out, loss_target, m_w_mod, m_b_mod, m_pre_norm_w, m_post_norm_w, m_w_in, m_mla_q_norm_w, m_mla_q_up, m_mla_kv_norm_w, m_mla_kv_up, m_gdn_conv_w, m_gdn_a_log, m_gdn_dt_bias, m_gdn_o_norm_w, m_w_out, v_w_mod, v_b_mod, v_pre_norm_w, v_post_norm_w, v_w_in, v_mla_q_norm_w, v_mla_q_up, v_mla_kv_norm_w, v_mla_kv_up, v_gdn_conv_w, v_gdn_a_log, v_gdn_dt_bias, v_gdn_o_norm_w, v_w_out):
    given = dict(w_mod=w_mod, b_mod=b_mod, pre_norm_w=pre_norm_w, post_norm_w=post_norm_w, w_in=w_in,
                 mla_q_norm_w=mla_q_norm_w, mla_q_up=mla_q_up, mla_kv_norm_w=mla_kv_norm_w, mla_kv_up=mla_kv_up,
                 gdn_conv_w=gdn_conv_w, gdn_a_log=gdn_a_log, gdn_dt_bias=gdn_dt_bias, gdn_o_norm_w=gdn_o_norm_w,
                 w_out=w_out)
    moments_m = dict(w_mod=m_w_mod, b_mod=m_b_mod, pre_norm_w=m_pre_norm_w, post_norm_w=m_post_norm_w, w_in=m_w_in,
                     mla_q_norm_w=m_mla_q_norm_w, mla_q_up=m_mla_q_up, mla_kv_norm_w=m_mla_kv_norm_w,
                     mla_kv_up=m_mla_kv_up, gdn_conv_w=m_gdn_conv_w, gdn_a_log=m_gdn_a_log,
                     gdn_dt_bias=m_gdn_dt_bias, gdn_o_norm_w=m_gdn_o_norm_w, w_out=m_w_out)
    moments_v = dict(w_mod=v_w_mod, b_mod=v_b_mod, pre_norm_w=v_pre_norm_w, post_norm_w=v_post_norm_w, w_in=v_w_in,
                     mla_q_norm_w=v_mla_q_norm_w, mla_q_up=v_mla_q_up, mla_kv_norm_w=v_mla_kv_norm_w,
                     mla_kv_up=v_mla_kv_up, gdn_conv_w=v_gdn_conv_w, gdn_a_log=v_gdn_a_log,
                     gdn_dt_bias=v_gdn_dt_bias, gdn_o_norm_w=v_gdn_o_norm_w, w_out=v_w_out)

    full = _gather_weights({name: given[name] for name, _ in SHARDED})
    for name in REPLICATED:
        full[name] = given[name]
    loss_local, (grads, grad_x) = jax.value_and_grad(_local_loss, argnums=(0, 1))(
        full, x[0], c, positions[0], loss_target[0])
    loss = lax.psum(loss_local, AXES)

    cross_dtypes = [F32 if name in EXACT_F32 else BF16 for name, _ in SHARDED] + [F32]
    reduced = reduce_grads(_split_grads(grads), cross_dtypes)
    grad_w = {name: g for (name, _), g in zip(SHARDED, reduced)}
    grad_w.update(_unsplit_small(reduced[-1], {name: given[name].shape for name in REPLICATED}))
    delta, new_m, new_v = {}, {}, {}
    for name in WEIGHT_ORDER:
        delta[name], new_m[name], new_v[name] = adamw(given[name], grad_w[name], moments_m[name], moments_v[name])
    return (loss, grad_x[None], *[grad_w[n] for n in WEIGHT_ORDER], *[delta[n] for n in WEIGHT_ORDER],
            *[new_m[n] for n in WEIGHT_ORDER], *[new_v[n] for n in WEIGHT_ORDER])
```

```python
import functools
import math

import numpy as np
import jax
import jax.numpy as jnp
from jax import lax
from jax.experimental import pallas as pl
from jax.experimental.pallas import tpu as pltpu

F32 = jnp.float32
BF16 = jnp.bfloat16
MESH = pl.DeviceIdType.MESH
AXES = ("x", "y", "c")

D_MODEL = 1024
DEPTH = 4
MLA_HEADS = 4
MLA_NOPE = 128
MLA_ROPE = 64
MLA_V = 128
MLA_Q_RANK = 384
MLA_KV_RANK = 256
MLA_WIDTH = 512
GDN_HEADS = 4
GDN_DK = 128
GDN_WIDTH = 512
GDN_QKV = 1536
GDN_CONV = 4
IN_COLS = 3272
ROPE_THETA = 10000.0
NORM_EPS = 1e-6
ADAM_LR, ADAM_B1, ADAM_B2, ADAM_EPS, ADAM_WD, ADAM_STEP = 0.001, 0.9, 0.999, 1e-08, 0.01, 10

LANES = 128
N_CHIPS = 4
GDN_CHUNK = 128
VMEM_LIMIT = 56 * 1024 * 1024


def _cparams(sem=None):
    if sem is None:
        return pltpu.CompilerParams(vmem_limit_bytes=VMEM_LIMIT)
    return pltpu.CompilerParams(dimension_semantics=sem, vmem_limit_bytes=VMEM_LIMIT)


def _pick(dim, target):
    if dim <= target:
        return dim
    best = None
    for t in range(LANES, target + 1, LANES):
        if dim % t == 0:
            best = t
    assert best is not None, (dim, target)
    return best


def _resident(shape):
    return pl.BlockSpec(shape, lambda i: (0,) * len(shape), pipeline_mode=pl.Buffered(1))


_DN = {"nn": (((1,), (0,)), ((), ())), "nt": (((1,), (1,)), ((), ())), "tn": (((0,), (0,)), ((), ()))}


def _dot_raw(a, b, mode, exact):
    if exact:
        return lax.dot_general(a, b, _DN[mode], precision=lax.Precision.HIGHEST, preferred_element_type=F32)
    return lax.dot_general(a.astype(BF16), b.astype(BF16), _DN[mode], preferred_element_type=F32)


def _dot_split(a, b):
    a_hi, b_hi = a.astype(BF16), b.astype(BF16)
    a_lo, b_lo = (a - a_hi.astype(F32)).astype(BF16), (b - b_hi.astype(F32)).astype(BF16)
    dot = lambda u, w: lax.dot_general(u, w, _DN["nn"], preferred_element_type=F32)
    return dot(a_hi, b_hi) + (dot(a_hi, b_lo) + dot(a_lo, b_hi))


@functools.partial(jax.custom_vjp, nondiff_argnums=(2, 3))
def bdot(a, b, mode="nn", exact=False):
    return _dot_raw(a, b, mode, exact)


def _bdot_fwd(a, b, mode, exact):
    return _dot_raw(a, b, mode, exact), (a, b)


def _bdot_bwd(mode, exact, res, g):
    a, b = res
    if mode == "nn":
        return bdot(g, b, "nt", exact), bdot(a, g, "tn", exact)
    if mode == "nt":
        return bdot(g, b, "nn", exact), bdot(g, a, "tn", exact)
    return bdot(b, g, "nt", exact), bdot(a, g, "nn", exact)


bdot.defvjp(_bdot_fwd, _bdot_bwd)


@jax.custom_vjp
def roll_half(x):
    return pltpu.roll(x, 64, 1)


roll_half.defvjp(lambda x: (pltpu.roll(x, 64, 1), None), lambda _, g: (pltpu.roll(g, 64, 1),))


def _sigmoid(x):
    return 1.0 / (1.0 + jnp.exp(-x))


def _silu(x):
    return x * _sigmoid(x)


def _softplus(x):
    return jnp.maximum(x, 0.0) + jnp.log(1.0 + jnp.exp(-jnp.abs(x)))


def _rms(x, w):
    return x * lax.rsqrt(jnp.mean(x * x, axis=-1, keepdims=True) + NORM_EPS) * w


def _rw_fwd(fn, name, rows, params, out_dims, out_dtypes, acc_dims, tile):
    s = rows[0].shape[0]
    t = min(tile, s)
    n = s // t
    nr, npar, no, na = len(rows), len(params), len(out_dims), len(acc_dims)

    def body(*refs):
        r, p = refs[:nr], refs[nr:nr + npar]
        o, a = refs[nr + npar:nr + npar + no], refs[nr + npar + no:]
        outs = fn(*[x[...] for x in r], *[x[...] for x in p])
        for ref, val in zip(o, outs[:no]):
            ref[...] = val.astype(ref.dtype)
        if na:
            @pl.when(pl.program_id(0) == 0)
            def _():
                for ref in a:
                    ref[...] = jnp.zeros_like(ref)
            for ref, val in zip(a, outs[no:]):
                ref[...] += val

    in_specs = [pl.BlockSpec((t, x.shape[1]), lambda i: (i, 0)) for x in rows]
    in_specs += [_resident(x.shape) for x in params]
    out_specs = [pl.BlockSpec((t, d), lambda i: (i, 0)) for d in out_dims]
    out_specs += [_resident((1, d)) for d in acc_dims]
    out_shape = [jax.ShapeDtypeStruct((s, d), dt) for d, dt in zip(out_dims, out_dtypes)]
    out_shape += [jax.ShapeDtypeStruct((1, d), F32) for d in acc_dims]
    res = pl.pallas_call(body, name=name, grid=(n,), in_specs=in_specs, out_specs=out_specs, out_shape=out_shape,
                         compiler_params=_cparams(("arbitrary",)))(*rows, *params)
    return tuple(res)


def _rw_bwd(fn, name, rows, params, row_cts, acc_cts, n_diff, tile):
    s = rows[0].shape[0]
    t = min(tile, s)
    n = s // t
    nr, npar, no, na = len(rows), len(params), len(row_cts), len(acc_cts)

    def body(*refs):
        r, p = refs[:nr], refs[nr:nr + npar]
        g, ga = refs[nr + npar:nr + npar + no], refs[nr + npar + no:nr + npar + no + na]
        dr, dp = refs[nr + npar + no + na:nr + npar + no + na + n_diff], refs[nr + npar + no + na + n_diff:]
        _, vjp = jax.vjp(fn, *[x[...] for x in r], *[x[...] for x in p])
        cts = vjp(tuple([x[...] for x in g] + [x[...] for x in ga]))
        for ref, val in zip(dr, cts[:n_diff]):
            ref[...] = val
        if npar:
            @pl.when(pl.program_id(0) == 0)
            def _():
                for ref in dp:
                    ref[...] = jnp.zeros_like(ref)
            for ref, val in zip(dp, cts[nr:]):
                ref[...] += val

    in_specs = [pl.BlockSpec((t, x.shape[1]), lambda i: (i, 0)) for x in rows]
    in_specs += [_resident(x.shape) for x in params]
    in_specs += [pl.BlockSpec((t, x.shape[1]), lambda i: (i, 0)) for x in row_cts]
    in_specs += [_resident(x.shape) for x in acc_cts]
    out_specs = [pl.BlockSpec((t, x.shape[1]), lambda i: (i, 0)) for x in rows[:n_diff]]
    out_specs += [_resident(x.shape) for x in params]
    out_shape = [jax.ShapeDtypeStruct(x.shape, F32) for x in rows[:n_diff]]
    out_shape += [jax.ShapeDtypeStruct(x.shape, F32) for x in params]
    res = pl.pallas_call(body, name=name, grid=(n,), in_specs=in_specs, out_specs=out_specs, out_shape=out_shape,
                         compiler_params=_cparams(("arbitrary",)))(*rows, *params, *row_cts, *acc_cts)
    return tuple(res[:n_diff]), tuple(res[n_diff:])


def make_rowwise(fn, name, out_dims, acc_dims=(), n_nondiff=0, tile=256):
    out_dtypes = (F32,) * len(out_dims)

    @jax.custom_vjp
    def op(rows, params):
        return _rw_fwd(fn, name, rows, params, out_dims, out_dtypes, acc_dims, tile)

    def fwd(rows, params):
        return op(rows, params), (rows, params)

    def bwd(res, cts):
        rows, params = res
        n_diff = len(rows) - n_nondiff
        d_rows, d_params = _rw_bwd(fn, name + "_bwd", rows, params, cts[:len(out_dims)], cts[len(out_dims):],
                                   n_diff, tile)
        d_rows = d_rows + tuple(jnp.zeros_like(x) for x in rows[n_diff:])
        return d_rows, d_params

    op.defvjp(fwd, bwd)
    return op


def _mm(a, b, mode, name):
    if mode == "nn":
        (m, k), (_, n) = a.shape, b.shape
    elif mode == "nt":
        (m, k), (n, _) = a.shape, b.shape
    else:
        (k, m), (_, n) = a.shape, b.shape
    tm = _pick(m, 512)
    tn = _pick(n, 1152)
    tk = _pick(k, 1152) if mode != "tn" else _pick(k, 512)
    nk = k // tk

    def body(a_ref, b_ref, o_ref, acc_ref):
        kk = pl.program_id(2)

        @pl.when(kk == 0)
        def _():
            acc_ref[...] = jnp.zeros_like(acc_ref)

        acc_ref[...] += _dot_raw(a_ref[...], b_ref[...], mode, False)

        @pl.when(kk == nk - 1)
        def _():
            o_ref[...] = acc_ref[...]

    if mode == "nn":
        a_spec = pl.BlockSpec((tm, tk), lambda i, j, kk: (i, kk))
        b_spec = pl.BlockSpec((tk, tn), lambda i, j, kk: (kk, j))
    elif mode == "nt":
        a_spec = pl.BlockSpec((tm, tk), lambda i, j, kk: (i, kk))
        b_spec = pl.BlockSpec((tn, tk), lambda i, j, kk: (j, kk))
    else:
        a_spec = pl.BlockSpec((tk, tm), lambda i, j, kk: (kk, i))
        b_spec = pl.BlockSpec((tk, tn), lambda i, j, kk: (kk, j))
    return pl.pallas_call(
        body, name=name, grid=(m // tm, n // tn, nk), in_specs=[a_spec, b_spec],
        out_specs=pl.BlockSpec((tm, tn), lambda i, j, kk: (i, j)),
        out_shape=jax.ShapeDtypeStruct((m, n), F32), scratch_shapes=[pltpu.VMEM((tm, tn), F32)],
        compiler_params=_cparams(("parallel", "parallel", "arbitrary")))(a, b)


def _mla_prep_fn(qraw, kvraw, kr, t1, t2):
    kr_rot = kr * t1 + roll_half(kr) * t2
    qs, ks = [], []
    for h in range(MLA_HEADS):
        q_r = qraw[:, h * 256 + 128:(h + 1) * 256]
        qs += [qraw[:, h * 256:h * 256 + 128], q_r * t1 + roll_half(q_r) * t2]
        ks += [kvraw[:, h * 128:(h + 1) * 128], kr_rot]
    return jnp.concatenate(qs, axis=1), jnp.concatenate(ks, axis=1), kvraw[:, 512:]


def _flash_tile(s):
    return 512 if s >= 2048 else 128


FLASH_SCALE = (MLA_NOPE + MLA_ROPE) ** -0.5
LOG2_E = 1.4426950408889634
FLASH_EXP2 = FLASH_SCALE * LOG2_E
STAT_ROWS = 8
FLASH_PAIR = 2
FLASH_STRIP = 32


def _as_rows(col_b):
    ones = jnp.full((STAT_ROWS, LANES), 1.0 / LANES, F32)
    return _dot_raw(ones, col_b, "nt", True)


def _flash_fwd(qf, kf, vf):
    s = qf.shape[0]
    t = _flash_tile(s)
    nb = s // t
    pair = range(FLASH_PAIR)

    ck = min(FLASH_STRIP, t)

    def body(q_ref, k_ref, v_ref, o_ref, lse_ref, m_sc, l_sc, acc_sc, st_sc, pt_sc):
        i = pl.program_id(1)
        m_sc[...] = jnp.full_like(m_sc, -1e30)
        l_sc[...] = jnp.zeros_like(l_sc)
        acc_sc[...] = jnp.zeros_like(acc_sc)
        qs = [q_ref[:, hh * 256:(hh + 1) * 256] for hh in pair]

        def step(j, on_diagonal):
            rows = pl.ds(pl.multiple_of(j * t, t), t)
            for hh in pair:
                st = _dot_raw(k_ref[rows, hh * 256:(hh + 1) * 256], qs[hh], "nt", False)
                if on_diagonal:
                    keep = lax.broadcasted_iota(jnp.int32, (t, t), 0) <= lax.broadcasted_iota(jnp.int32, (t, t), 1)
                    st = jnp.where(keep, st, -1e30)
                st_sc[hh] = st
            m_olds = [m_sc[hh] for hh in pair]
            m_news = [jnp.maximum(m_olds[hh], jnp.max(st_sc[hh], axis=0, keepdims=True)) for hh in pair]
            alphas = [jnp.exp2((m_olds[hh] - m_news[hh]) * FLASH_EXP2) for hh in pair]
            shifts = [m_news[hh] * FLASH_EXP2 for hh in pair]
            sums = [jnp.zeros((8, t), F32) for _ in pair]
            for hh in pair:
                for r in range(0, t, ck):
                    p = jnp.exp2(st_sc[hh, r:r + ck, :] * FLASH_EXP2 - shifts[hh])
                    pt_sc[hh, r:r + ck, :] = p.astype(BF16)
                    sums[hh] = sums[hh] + functools.reduce(lambda a, b: a + b, [p[u:u + 8] for u in range(0, ck, 8)])
            pvs = [_dot_raw(v_ref[rows, hh * 128:(hh + 1) * 128], pt_sc[hh], "tn", False) for hh in pair]
            for hh in pair:
                l_sc[hh] = alphas[hh] * l_sc[hh] + jnp.sum(sums[hh], axis=0, keepdims=True)
                acc_sc[hh] = alphas[hh] * acc_sc[hh] + pvs[hh]
                m_sc[hh] = m_news[hh]

        def two_steps(p, carry):
            step(2 * p, False)
            step(2 * p + 1, False)
            return carry

        lax.fori_loop(0, i // 2, two_steps, 0)

        @pl.when(i % 2 == 1)
        def _():
            step(i - 1, False)

        step(i, True)
        for hh in pair:
            o_ref[:, hh * 128:(hh + 1) * 128] = (acc_sc[hh] / l_sc[hh]).T
            lse2 = m_sc[hh] * FLASH_EXP2 + jnp.log(l_sc[hh]) * LOG2_E
            lse_ref[hh] = jnp.broadcast_to(lse2, (STAT_ROWS, t))

    p = FLASH_PAIR
    return pl.pallas_call(
        body, name="flash_fwd", grid=(MLA_HEADS // p, nb),
        in_specs=[pl.BlockSpec((t, p * 256), lambda h, i: (i, h)), pl.BlockSpec((s, p * 256), lambda h, i: (0, h)),
                  pl.BlockSpec((s, p * 128), lambda h, i: (0, h))],
        out_specs=[pl.BlockSpec((t, p * 128), lambda h, i: (i, h)),
                   pl.BlockSpec((p, STAT_ROWS, t), lambda h, i: (h, 0, i))],
        out_shape=[jax.ShapeDtypeStruct((s, MLA_WIDTH), F32),
                   jax.ShapeDtypeStruct((MLA_HEADS, STAT_ROWS, s), F32)],
        scratch_shapes=[pltpu.VMEM((p, 1, t), F32), pltpu.VMEM((p, 1, t), F32), pltpu.VMEM((p, 128, t), F32),
                        pltpu.VMEM((p, t, t), F32), pltpu.VMEM((p, t, t), BF16)],
        compiler_params=_cparams(("parallel", "arbitrary")))(qf, kf, vf)


def _flash_bwd(qf, kf, vf, lse, o, do):
    s = qf.shape[0]
    t = _flash_tile(s)
    nb = s // t

    def body(q_ref, k_ref, v_ref, lse_ref, o_ref, do_ref, dq_ref, dk_ref, dv_ref, dl_sc, do16_sc):
        j = pl.program_id(1)

        @pl.when(j == 0)
        def _():
            dq_ref[...] = jnp.zeros_like(dq_ref)

            def stage(b, carry):
                rows = pl.ds(pl.multiple_of(b * t, t), t)
                do_t = do_ref[rows, :]
                delta = jnp.sum(do_t * o_ref[rows, :], axis=1, keepdims=True)
                dl_sc[:, rows] = _as_rows(jnp.broadcast_to(delta, (t, LANES)))
                do16_sc[rows, :] = do_t.astype(BF16)
                return carry

            lax.fori_loop(0, nb, stage, 0)

        dk_ref[...] = jnp.zeros_like(dk_ref)
        dv_ref[...] = jnp.zeros_like(dv_ref)
        k = k_ref[...]
        v = v_ref[...]

        def step(i, on_diagonal):
            rows = pl.ds(pl.multiple_of(i * t, t), t)
            q = q_ref[rows, :]
            do_t = do16_sc[rows, :]
            st = _dot_raw(k, q, "nt", False) * FLASH_EXP2 - lse_ref[0, 0:1, rows]
            if on_diagonal:
                keep = lax.broadcasted_iota(jnp.int32, (t, t), 0) <= lax.broadcasted_iota(jnp.int32, (t, t), 1)
                st = jnp.where(keep, st, -1e30)
            pt = jnp.exp2(st)
            dst = pt * (_dot_raw(v, do_t, "nt", False) - dl_sc[0:1, rows])
            dv_ref[...] += _dot_raw(pt, do_t, "nn", False)
            dk_ref[...] += _dot_raw(dst, q, "nn", False)
            dq_ref[rows, :] += _dot_raw(dst, k, "tn", False)

        def two_steps(p, carry):
            step(j + 1 + 2 * p, False)
            step(j + 2 + 2 * p, False)
            return carry

        step(j, True)
        below = nb - 1 - j
        lax.fori_loop(0, below // 2, two_steps, 0)

        @pl.when(below % 2 == 1)
        def _():
            step(nb - 1, False)

        dk_ref[...] *= FLASH_SCALE

        @pl.when(j == nb - 1)
        def _():
            dq_ref[...] *= FLASH_SCALE

    per_head = lambda d: pl.BlockSpec((s, d), lambda h, j: (0, h), pipeline_mode=pl.Buffered(1))
    return pl.pallas_call(
        body, name="flash_bwd", grid=(MLA_HEADS, nb),
        in_specs=[per_head(256), pl.BlockSpec((t, 256), lambda h, j: (j, h)),
                  pl.BlockSpec((t, 128), lambda h, j: (j, h)),
                  pl.BlockSpec((1, STAT_ROWS, s), lambda h, j: (h, 0, 0)), per_head(128), per_head(128)],
        out_specs=[pl.BlockSpec((s, 256), lambda h, j: (0, h)), pl.BlockSpec((t, 256), lambda h, j: (j, h)),
                   pl.BlockSpec((t, 128), lambda h, j: (j, h))],
        out_shape=[jax.ShapeDtypeStruct((s, 1024), F32), jax.ShapeDtypeStruct((s, 1024), F32),
                   jax.ShapeDtypeStruct((s, MLA_WIDTH), F32)],
        scratch_shapes=[pltpu.VMEM((STAT_ROWS, s), F32), pltpu.VMEM((s, 128), BF16)],
        compiler_params=_cparams(("parallel", "arbitrary")))(qf, kf, vf, lse, o, do)


MLA_FRONT_TILE = 512


def _mla_front_fn(q_lat, kv_lat, kr, t1, t2, q_norm_w, q_up, kv_norm_w, kv_up):
    qraw = bdot(_rms(q_lat, q_norm_w), q_up)
    kvraw = bdot(_rms(kv_lat, kv_norm_w), kv_up)
    return _mla_prep_fn(qraw, kvraw, kr, t1, t2)


def _mla_front(rows, params):
    return _rw_fwd(_mla_front_fn, "mla_front", rows, params, (1024, 1024, 512), (BF16, BF16, BF16), (),
                   MLA_FRONT_TILE)


@jax.custom_vjp
def mla_attention(rows, params):
    return _flash_fwd(*_mla_front(rows, params))[0]


def _mla_attention_fwd(rows, params):
    qf, kf, vf = _mla_front(rows, params)
    o, lse = _flash_fwd(qf, kf, vf)
    return o, (rows, params, qf, kf, vf, o, lse)


def _mla_attention_bwd(res, do):
    rows, params, qf, kf, vf, o, lse = res
    d_rows, d_params = _rw_bwd(_mla_front_fn, "mla_front_bwd", rows, params, _flash_bwd(qf, kf, vf, lse, o, do), (),
                               3, MLA_FRONT_TILE)
    return d_rows + (jnp.zeros_like(rows[3]), jnp.zeros_like(rows[4])), d_params


mla_attention.defvjp(_mla_attention_fwd, _mla_attention_bwd)


def _lane_pick(x, lane):
    ids = lax.broadcasted_iota(jnp.int32, x.shape, 1)
    col = jnp.sum(jnp.where(ids == lane, x, 0.0), axis=1, keepdims=True)
    return jnp.broadcast_to(col, x.shape)


GDN_HALO = 8


@functools.partial(jax.custom_vjp, nondiff_argnums=(1,))
def _roll_rows(x, d):
    return pltpu.roll(x, d, 0)


_roll_rows.defvjp(lambda x, d: (pltpu.roll(x, d, 0), None), lambda d, _, g: (pltpu.roll(g, g.shape[0] - d, 0),))


def _gdn_prep_fn(prev, cur, ab, w0, w1, w2, w3, a_log, dt_bias):
    xcat = jnp.concatenate([prev, cur], axis=0)
    x0, x1, x2 = [_roll_rows(xcat, GDN_CONV - 1 - j)[GDN_HALO:] for j in range(GDN_CONV - 1)]
    qkv = _silu(x0 * w0 + x1 * w1 + x2 * w2 + cur * w3)
    g_all = -jnp.exp(a_log) * _softplus(ab + dt_bias)
    beta_all = _sigmoid(ab)
    qs, ks, gs, bs = [], [], [], []
    for h in range(GDN_HEADS):
        q = qkv[:, h * 128:(h + 1) * 128]
        k = qkv[:, 512 + h * 128:512 + (h + 1) * 128]
        qs.append(q * lax.rsqrt(jnp.sum(q * q, axis=-1, keepdims=True) + NORM_EPS) * (GDN_DK ** -0.5))
        ks.append(k * lax.rsqrt(jnp.sum(k * k, axis=-1, keepdims=True) + NORM_EPS))
        gs.append(_lane_pick(g_all, h))
        bs.append(_lane_pick(beta_all, GDN_HEADS + h))
    cat = lambda xs: jnp.concatenate(xs, axis=1)
    return cat(qs), cat(ks), qkv[:, 1024:], cat(gs), cat(bs)


GDN_PREP_TILE = 256


def _gdn_prep_specs(s, params, reverse=False):
    t = min(GDN_PREP_TILE, s)
    n = s // t
    blk = (lambda i: n - 1 - i) if reverse else (lambda i: i)
    prev = pl.BlockSpec((GDN_HALO, GDN_QKV), lambda i: (jnp.maximum(blk(i) * (t // GDN_HALO) - 1, 0), 0))
    rows = lambda d: pl.BlockSpec((t, d), lambda i: (blk(i), 0))
    return t, rows, [prev, rows(GDN_QKV), rows(LANES)] + [pl.BlockSpec(p.shape, lambda i: (0, 0)) for p in params]


def _gdn_prep_masked(has_rows_before):
    return lambda prev, *rest: _gdn_prep_fn(prev * has_rows_before, *rest)


def _gdn_prep_fwd(qkv, ab, params):
    s = qkv.shape[0]
    t, rows, in_specs = _gdn_prep_specs(s, params)

    def body(prev_ref, cur_ref, ab_ref, *refs):
        p_refs, o_refs = refs[:len(params)], refs[len(params):]
        has_rows_before = (pl.program_id(0) > 0).astype(F32)
        outs = _gdn_prep_masked(has_rows_before)(prev_ref[...], cur_ref[...], ab_ref[...], *[p[...] for p in p_refs])
        for ref, val in zip(o_refs, outs):
            ref[...] = val

    return pl.pallas_call(
        body, name="gdn_prep", grid=(s // t,), in_specs=in_specs,
        out_specs=[rows(GDN_WIDTH)] * 5, out_shape=[jax.ShapeDtypeStruct((s, GDN_WIDTH), F32)] * 5,
        compiler_params=_cparams(("parallel",)))(qkv, qkv, ab, *params)


def _gdn_prep_bwd(qkv, ab, params, cts):
    s = qkv.shape[0]
    t, rows, in_specs = _gdn_prep_specs(s, params, reverse=True)
    n = s // t
    npar = len(params)

    def body(prev_ref, cur_ref, ab_ref, *refs):
        p_refs, g_refs = refs[:npar], refs[npar:npar + 5]
        dcur_ref, dab_ref = refs[npar + 5:npar + 7]
        dp_refs, carry_sc = refs[npar + 7:-1], refs[-1]

        @pl.when(pl.program_id(0) == 0)
        def _():
            carry_sc[...] = jnp.zeros_like(carry_sc)
            for ref in dp_refs:
                ref[...] = jnp.zeros_like(ref)

        has_rows_before = (pl.program_id(0) < n - 1).astype(F32)
        _, vjp = jax.vjp(_gdn_prep_masked(has_rows_before), prev_ref[...], cur_ref[...], ab_ref[...],
                         *[p[...] for p in p_refs])
        d_prev, d_cur, d_ab, *d_params = vjp(tuple(g[...] for g in g_refs))
        dcur_ref[...] = d_cur
        dcur_ref[t - GDN_HALO:, :] += carry_sc[...]
        carry_sc[...] = d_prev
        dab_ref[...] = d_ab
        for ref, val in zip(dp_refs, d_params):
            ref[...] += val

    res = pl.pallas_call(
        body, name="gdn_prep_bwd", grid=(n,), in_specs=in_specs + [rows(GDN_WIDTH)] * 5,
        out_specs=[rows(GDN_QKV), rows(LANES)] + [pl.BlockSpec(p.shape, lambda i: (0, 0)) for p in params],
        out_shape=[jax.ShapeDtypeStruct((s, GDN_QKV), F32), jax.ShapeDtypeStruct((s, LANES), F32)]
        + [jax.ShapeDtypeStruct(p.shape, F32) for p in params],
        scratch_shapes=[pltpu.VMEM((GDN_HALO, GDN_QKV), F32)],
        compiler_params=_cparams(("arbitrary",)))(qkv, qkv, ab, *params, *cts)
    return res[0], res[1], tuple(res[2:])


@jax.custom_vjp
def gdn_prep(qkv, ab, params):
    return tuple(_gdn_prep_fwd(qkv, ab, params))


def _gdn_prep_vjp_fwd(qkv, ab, params):
    return tuple(_gdn_prep_fwd(qkv, ab, params)), (qkv, ab, params)


def _gdn_prep_vjp_bwd(res, cts):
    qkv, ab, params = res
    return _gdn_prep_bwd(qkv, ab, params, cts)


gdn_prep.defvjp(_gdn_prep_vjp_fwd, _gdn_prep_vjp_bwd)


@jax.custom_vjp
def _unit_lower_inverse(lms):
    c = lms[0].shape[0]
    row = lax.broadcasted_iota(jnp.int32, (c, c), 0)
    col = lax.broadcasted_iota(jnp.int32, (c, c), 1)
    ts = [(row == col).astype(F32) - jnp.where((row >> 1) == (col >> 1), lm, 0.0) for lm in lms]
    for level in range(1, int(math.log2(c))):
        below = ((row >> (level + 1)) == (col >> (level + 1))) & ((row >> level) != (col >> level))
        mids = [_dot_split(t, jnp.where(below, lm, 0.0)) for t, lm in zip(ts, lms)]
        ts = [t - _dot_split(mid, t) for t, mid in zip(ts, mids)]
    return tuple(ts)


def _uli_fwd(lms):
    ts = _unit_lower_inverse(lms)
    return ts, ts


def _uli_bwd(ts, gs):
    mids = [bdot(t, g, "tn") for t, g in zip(ts, gs)]
    return (tuple(-bdot(mid, t, "nt") for t, mid in zip(ts, mids)),)


_unit_lower_inverse.defvjp(_uli_fwd, _uli_bwd)


@jax.custom_vjp
def _known_inverse(lms, ts):
    return ts


_known_inverse.defvjp(lambda lms, ts: (ts, ts),
                      lambda ts, gs: (_uli_bwd(ts, gs)[0], tuple(jnp.zeros_like(t) for t in ts)))


def _gdn_chunk_fn(qs, ks, vs, gbs, bbs, s0s, known_ts=None):
    heads = range(len(qs))
    c = qs[0].shape[0]
    row = lax.broadcasted_iota(jnp.int32, (c, c), 0)
    col = lax.broadcasted_iota(jnp.int32, (c, c), 1)
    incl, strict = row >= col, row > col
    tri = incl.astype(F32)
    gc = [bdot(tri, gbs[h], "nn", True) for h in heads]
    decay = [jnp.exp(jnp.where(incl, gc[h] - gc[h].T, -1e30)) for h in heads]
    g_last = [jnp.sum(gbs[h], axis=0, keepdims=True) for h in heads]
    eg = [jnp.exp(gc[h]) for h in heads]
    kb = [ks[h] * bbs[h] for h in heads]
    lms = tuple(jnp.where(strict, bdot(kb[h], ks[h], "nt") * decay[h], 0.0) for h in heads)
    ts = _unit_lower_inverse(lms) if known_ts is None else _known_inverse(lms, known_ts)
    u = [bdot(ts[h], vs[h] * bbs[h]) for h in heads]
    w = [bdot(ts[h], kb[h] * eg[h]) for h in heads]
    qk = [bdot(qs[h], ks[h], "nt") * decay[h] for h in heads]
    v_new = [u[h] - bdot(w[h], s0s[h]) for h in heads]
    o = [bdot(qs[h] * eg[h], s0s[h]) + bdot(qk[h], v_new[h]) for h in heads]
    s1 = [s0s[h] * jnp.exp(g_last[h]) + bdot(ks[h] * jnp.exp(g_last[h] - gc[h]), v_new[h], "tn") for h in heads]
    return (tuple(o), tuple(s1)), ts


def _head_tiles(ref):
    return tuple(ref[:, h * 128:(h + 1) * 128] for h in range(GDN_HEADS))


def _gdn_fwd(q, k, v, gb, bb):
    s = q.shape[0]
    c = min(GDN_CHUNK, s)
    nc = s // c

    def body(q_ref, k_ref, v_ref, g_ref, b_ref, o_ref, st_ref, inv_ref, s_sc):
        @pl.when(pl.program_id(0) == 0)
        def _():
            s_sc[...] = jnp.zeros_like(s_sc)

        s0s = tuple(s_sc[h] for h in range(GDN_HEADS))
        for h in range(GDN_HEADS):
            st_ref[h, 0] = s0s[h]
        (os, s1s), ts = _gdn_chunk_fn(*[_head_tiles(ref) for ref in (q_ref, k_ref, v_ref, g_ref, b_ref)], s0s)
        for h in range(GDN_HEADS):
            o_ref[:, h * 128:(h + 1) * 128] = os[h]
            inv_ref[h, 0] = ts[h]
            s_sc[h] = s1s[h]

    blk = pl.BlockSpec((c, GDN_WIDTH), lambda n: (n, 0))
    return pl.pallas_call(
        body, name="gdn_fwd", grid=(nc,), in_specs=[blk] * 5,
        out_specs=[blk, pl.BlockSpec((GDN_HEADS, 1, 128, 128), lambda n: (0, n, 0, 0)),
                   pl.BlockSpec((GDN_HEADS, 1, c, c), lambda n: (0, n, 0, 0))],
        out_shape=[jax.ShapeDtypeStruct((s, GDN_WIDTH), F32), jax.ShapeDtypeStruct((GDN_HEADS, nc, 128, 128), F32),
                   jax.ShapeDtypeStruct((GDN_HEADS, nc, c, c), F32)],
        scratch_shapes=[pltpu.VMEM((GDN_HEADS, 128, 128), F32)],
        compiler_params=_cparams(("arbitrary",)))(q, k, v, gb, bb)


def _gdn_bwd(q, k, v, gb, bb, states, inverses, do):
    s = q.shape[0]
    c = min(GDN_CHUNK, s)
    nc = s // c

    def body(q_ref, k_ref, v_ref, g_ref, b_ref, st_ref, inv_ref, do_ref, dq_ref, dk_ref, dv_ref, dg_ref, db_ref, ds_sc):
        @pl.when(pl.program_id(0) == 0)
        def _():
            ds_sc[...] = jnp.zeros_like(ds_sc)

        s0s = tuple(st_ref[h, 0] for h in range(GDN_HEADS))
        ts = tuple(inv_ref[h, 0] for h in range(GDN_HEADS))
        chunk = lambda *args: _gdn_chunk_fn(*args, known_ts=ts)[0]
        _, vjp = jax.vjp(chunk, *[_head_tiles(ref) for ref in (q_ref, k_ref, v_ref, g_ref, b_ref)], s0s)
        *d_tiles, ds0s = vjp((_head_tiles(do_ref), tuple(ds_sc[h] for h in range(GDN_HEADS))))
        for h in range(GDN_HEADS):
            for ref, d in zip((dq_ref, dk_ref, dv_ref, dg_ref, db_ref), d_tiles):
                ref[:, h * 128:(h + 1) * 128] = d[h]
            ds_sc[h] = ds0s[h]

    blk = pl.BlockSpec((c, GDN_WIDTH), lambda n: (nc - 1 - n, 0))
    return pl.pallas_call(
        body, name="gdn_bwd", grid=(nc,),
        in_specs=[blk] * 5 + [pl.BlockSpec((GDN_HEADS, 1, 128, 128), lambda n: (0, nc - 1 - n, 0, 0)),
                              pl.BlockSpec((GDN_HEADS, 1, c, c), lambda n: (0, nc - 1 - n, 0, 0)), blk],
        out_specs=[blk] * 5, out_shape=[jax.ShapeDtypeStruct((s, GDN_WIDTH), F32)] * 5,
        scratch_shapes=[pltpu.VMEM((GDN_HEADS, 128, 128), F32)],
        compiler_params=_cparams(("arbitrary",)))(q, k, v, gb, bb, states, inverses, do)


@jax.custom_vjp
def gdn_core(q, k, v, gb, bb):
    return _gdn_fwd(q, k, v, gb, bb)[0]


def _gdn_core_fwd(q, k, v, gb, bb):
    o, states, inverses = _gdn_fwd(q, k, v, gb, bb)
    return o, (q, k, v, gb, bb, states, inverses)


def _gdn_core_bwd(res, do):
    return tuple(_gdn_bwd(*res, do))


gdn_core.defvjp(_gdn_core_fwd, _gdn_core_bwd)


def _cact_fn(c):
    return (_silu(c),)


def _prenorm_fn(x, w, scale_raw, scale_b, shift_raw, shift_b):
    return (_rms(x, w) * (1.0 + scale_raw + scale_b) + shift_raw + shift_b,)


def _tail_fn(x, o_mla, z_mla, o_gdn, z_gdn, o_norm_w, w_out, post_w, gate_raw, gate_b):
    y_mla = o_mla * _silu(z_mla)
    parts = [_rms(o_gdn[:, h * 128:(h + 1) * 128], o_norm_w) for h in range(GDN_HEADS)]
    y_gdn = jnp.concatenate(parts, axis=1) * _silu(z_gdn)
    y = bdot(y_mla, w_out[:MLA_WIDTH]) + bdot(y_gdn, w_out[MLA_WIDTH:])
    return (x + (gate_raw + gate_b) * _rms(y, post_w),)


def _loss_fn(y, tgt):
    err = y - tgt
    part = jnp.sum(0.5 * jnp.mean(err * err, axis=-1, keepdims=True), axis=0, keepdims=True)
    return (jnp.broadcast_to(part, (1, LANES)),)


def _front_fwd(x, norm_params, ws16):
    s = x.shape[0]
    t = min(512, s)

    def body(x_ref, *refs):
        p_refs, w_refs, o_refs = refs[:len(norm_params)], refs[len(norm_params):-len(ws16)], refs[-len(ws16):]
        (h,) = _prenorm_fn(x_ref[...], *[p[...] for p in p_refs])
        h16 = h.astype(BF16)
        for w_ref, o_ref in zip(w_refs, o_refs):
            o_ref[...] = _dot_raw(h16, w_ref[...], "nn", False)

    rows = lambda d: pl.BlockSpec((t, d), lambda i: (i, 0))
    res = pl.pallas_call(
        body, name="front", grid=(s // t,),
        in_specs=[rows(D_MODEL)] + [_resident(p.shape) for p in norm_params] + [_resident(w.shape) for w in ws16],
        out_specs=[rows(w.shape[1]) for w in ws16],
        out_shape=[jax.ShapeDtypeStruct((s, w.shape[1]), F32) for w in ws16],
        compiler_params=_cparams(("parallel",)))(x, *norm_params, *ws16)
    return tuple(res)


def _front_bwd(x, norm_params, ws16, dys):
    s = x.shape[0]
    t = min(512, s)
    npar, ng = len(norm_params), len(ws16)

    def body(x_ref, *refs):
        p_refs, w_refs, g_refs = refs[:npar], refs[npar:npar + ng], refs[npar + ng:npar + 2 * ng]
        dx_ref = refs[npar + 2 * ng]
        dp_refs, dw_refs = refs[npar + 2 * ng + 1:2 * npar + 2 * ng + 1], refs[2 * npar + 2 * ng + 1:]

        @pl.when(pl.program_id(0) == 0)
        def _():
            for ref in dp_refs + dw_refs:
                ref[...] = jnp.zeros_like(ref)

        (h,), vjp = jax.vjp(_prenorm_fn, x_ref[...], *[p[...] for p in p_refs])
        h16 = h.astype(BF16)
        dys16 = [g[...].astype(BF16) for g in g_refs]
        dh = functools.reduce(lambda a, b: a + b,
                              [_dot_raw(dy, w_ref[...], "nt", False) for dy, w_ref in zip(dys16, w_refs)])
        for dy, dw_ref in zip(dys16, dw_refs):
            dw_ref[...] += _dot_raw(h16, dy, "tn", False)
        dx, *d_params = vjp((dh,))
        dx_ref[...] = dx
        for ref, val in zip(dp_refs, d_params):
            ref[...] += val

    rows = lambda d: pl.BlockSpec((t, d), lambda i: (i, 0))
    res = pl.pallas_call(
        body, name="front_bwd", grid=(s // t,),
        in_specs=[rows(D_MODEL)] + [_resident(p.shape) for p in norm_params] + [_resident(w.shape) for w in ws16]
        + [rows(w.shape[1]) for w in ws16],
        out_specs=[rows(D_MODEL)] + [_resident(p.shape) for p in norm_params] + [_resident(w.shape) for w in ws16],
        out_shape=[jax.ShapeDtypeStruct(x.shape, F32)] + [jax.ShapeDtypeStruct(p.shape, F32) for p in norm_params]
        + [jax.ShapeDtypeStruct(w.shape, F32) for w in ws16],
        compiler_params=_cparams(("arbitrary",)))(x, *norm_params, *ws16, *dys)
    return res[0], tuple(res[1:1 + npar]), tuple(res[1 + npar:])


@jax.custom_vjp
def front(x, norm_params, ws):
    return _front_fwd(x, norm_params, tuple(w.astype(BF16) for w in ws))


def _front_vjp_fwd(x, norm_params, ws):
    ws16 = tuple(w.astype(BF16) for w in ws)
    return _front_fwd(x, norm_params, ws16), (x, norm_params, ws16)


def _front_vjp_bwd(res, dys):
    return _front_bwd(*res, dys)


front.defvjp(_front_vjp_fwd, _front_vjp_bwd)


_OPS = dict(
    cact=make_rowwise(_cact_fn, "c_act", (D_MODEL,), tile=16),
    tail=make_rowwise(_tail_fn, "tail", (D_MODEL,), tile=512),
    loss=make_rowwise(_loss_fn, "loss", (), acc_dims=(LANES,), n_nondiff=1),
)


def _swap_halves(w):
    half = w.shape[-1] // 2
    return jnp.concatenate([w[..., half:], w[..., :half]], axis=-1)


def _w_in_groups(w):
    k_pe = w[:, 640:704]
    ab = jnp.concatenate([w[:, 2752:2760], jnp.zeros((w.shape[0], LANES - 8), w.dtype)], axis=1)
    return (w[:, :384], w[:, 384:640], jnp.concatenate([k_pe, _swap_halves(k_pe)], axis=1), w[:, 704:1216],
            w[:, 1216:2752], ab, w[:, 2760:])


def _q_up_ext(w):
    parts = []
    for h in range(MLA_HEADS):
        rope = w[:, h * 192 + 128:(h + 1) * 192]
        parts += [w[:, h * 192:h * 192 + 128], rope, _swap_halves(rope)]
    return jnp.concatenate(parts, axis=1)


def _kv_up_perm(w):
    ks = [w[:, h * 256:h * 256 + 128] for h in range(MLA_HEADS)]
    vs = [w[:, h * 256 + 128:(h + 1) * 256] for h in range(MLA_HEADS)]
    return jnp.concatenate(ks + vs, axis=1)


def _pad_lanes(v):
    return jnp.pad(v, (0, LANES - v.shape[0]))[None, :]


def _local_loss(weights, mods, x, positions, target):
    s = x.shape[0]
    half = MLA_ROPE // 2
    inv_freq = jnp.power(ROPE_THETA, -jnp.arange(half, dtype=F32) * 2.0 / MLA_ROPE)
    ang = positions.astype(F32)[:, None] * inv_freq
    cos, sin, zero = jnp.cos(ang), jnp.sin(ang), jnp.zeros((s, 2 * half), F32)
    t1 = jnp.concatenate([cos, cos, zero], axis=1)
    t2 = jnp.concatenate([-sin, sin, zero], axis=1)

    for l in range(DEPTH):
        mod = mods[l]
        b = weights["b_mod"][l][None, :]
        shift_raw, scale_raw, gate_raw = mod[:, :1024], mod[:, 1024:2048], mod[:, 2048:]
        shift_b, scale_b, gate_b = b[:, :1024], b[:, 1024:2048], b[:, 2048:]
        q_lat, kv_lat, kr, z_mla, qkv, ab, z_gdn = front(
            x, (weights["pre_norm_w"][l][None], scale_raw, scale_b, shift_raw, shift_b),
            _w_in_groups(weights["w_in"][l]))
        o_mla = mla_attention((q_lat, kv_lat, kr, t1, t2),
                              (weights["mla_q_norm_w"][l][None], _q_up_ext(weights["mla_q_up"][l]),
                               weights["mla_kv_norm_w"][l][None], _kv_up_perm(weights["mla_kv_up"][l])))
        cw = weights["gdn_conv_w"][l]
        params = tuple(cw[j][None] for j in range(GDN_CONV))
        params += (_pad_lanes(weights["gdn_a_log"][l]), _pad_lanes(weights["gdn_dt_bias"][l]))
        o_gdn = gdn_core(*gdn_prep(qkv, ab, params))
        (x,) = _OPS["tail"]((x, o_mla, z_mla, o_gdn, z_gdn),
                            (weights["gdn_o_norm_w"][l][None], weights["w_out"][l], weights["post_norm_w"][l][None],
                             gate_raw, gate_b))
    (acc,) = _OPS["loss"]((x, target), ())
    return acc[0, 0]


def _chip_index():
    return 2 * lax.axis_index("x") + lax.axis_index("y")


def _other_chips(x, y):
    return [(1 - x, y), (x, 1 - y), (1 - x, 1 - y)]


def _any_spec():
    return pl.BlockSpec(memory_space=pl.ANY)


def _half(ref, hc):
    n = ref.shape[0] // 2
    return ref.at[pl.ds(hc * n, n)]


def ag_weights(shards):
    n = len(shards)

    def body(*refs):
        ins, outs = refs[:n], refs[n:2 * n]
        send_sems, recv_sems = refs[2 * n:]
        x, y, c = lax.axis_index("x"), lax.axis_index("y"), lax.axis_index("c")
        sibling = (x, y, 1 - c)
        chips = _other_chips(x, y)

        def copy(t, k, src, chip_xy, hc, to):
            return pltpu.make_async_remote_copy(
                src_ref=src, dst_ref=_half(outs[t].at[2 * chip_xy[0] + chip_xy[1]], hc),
                send_sem=send_sems.at[6 * t + k], recv_sem=recv_sems.at[6 * t + k], device_id=to, device_id_type=MESH)

        first = [copy(t, k, _half(ins[t], c), (x, y), c, (*chip, c)) for k, chip in enumerate(chips) for t in range(n)]
        for cp in first:
            cp.start()
        passed = []
        for k, chip in enumerate(chips):
            for t in range(n):
                landed = _half(outs[t].at[2 * chip[0] + chip[1]], c)
                copy(t, k, landed, chip, c, (x, y, c)).wait_recv()
                passed.append(copy(t, 3 + k, landed, chip, c, sibling))
                passed[-1].start()
        for k, chip in enumerate(chips):
            for t in range(n):
                copy(t, 3 + k, _half(ins[t], c), chip, 1 - c, (x, y, c)).wait_recv()
        for cp in first + passed:
            cp.wait_send()

    return pl.pallas_call(
        body, name="ag_weights", in_specs=[_any_spec()] * n, out_specs=[_any_spec()] * n,
        out_shape=[jax.ShapeDtypeStruct((N_CHIPS,) + a.shape, a.dtype) for a in shards],
        scratch_shapes=[pltpu.SemaphoreType.DMA((6 * n,)), pltpu.SemaphoreType.DMA((6 * n,))],
        compiler_params=pltpu.CompilerParams(has_side_effects=True))(*shards)


def all_gather_rows(x):
    r, cols = x.shape
    flips = [(k >> 2 & 1, k >> 1 & 1, k & 1) for k in range(1, 8)]

    def body(x_ref, out_ref, send_sems, recv_sems):
        here = (lax.axis_index("x"), lax.axis_index("y"), lax.axis_index("c"))
        peers = [tuple(1 - p if f else p for p, f in zip(here, flip)) for flip in flips]
        slot = lambda dev: 4 * dev[0] + 2 * dev[1] + dev[2]
        out_ref[slot(here)] = x_ref[...]
        copies = [pltpu.make_async_remote_copy(
            src_ref=x_ref, dst_ref=out_ref.at[slot(here)], send_sem=send_sems.at[k], recv_sem=recv_sems.at[k],
            device_id=peer, device_id_type=MESH) for k, peer in enumerate(peers)]
        for cp in copies:
            cp.start()
        for k, peer in enumerate(peers):
            pltpu.make_async_remote_copy(
                src_ref=x_ref, dst_ref=out_ref.at[slot(peer)], send_sem=send_sems.at[k], recv_sem=recv_sems.at[k],
                device_id=peer, device_id_type=MESH).wait_recv()
        for cp in copies:
            cp.wait_send()

    return pl.pallas_call(
        body, name="all_gather_rows", in_specs=[pl.BlockSpec(memory_space=pltpu.VMEM)],
        out_specs=pl.BlockSpec(memory_space=pltpu.VMEM), out_shape=jax.ShapeDtypeStruct((8, r, cols), F32),
        scratch_shapes=[pltpu.SemaphoreType.DMA((7,)), pltpu.SemaphoreType.DMA((7,))],
        compiler_params=pltpu.CompilerParams(has_side_effects=True, vmem_limit_bytes=VMEM_LIMIT))(x)


def rs_pair(gs):
    n = len(gs)

    def body(*refs):
        ins, outs = refs[:n], refs[n:2 * n]
        send_sems, recv_sems = refs[2 * n:]
        x, y, c = lax.axis_index("x"), lax.axis_index("y"), lax.axis_index("c")
        lh = [g.shape[1] // 2 for g in gs]
        copies = [pltpu.make_async_remote_copy(
            src_ref=ins[t].at[:, pl.ds((1 - c) * lh[t], lh[t])], dst_ref=outs[t], send_sem=send_sems.at[t],
            recv_sem=recv_sems.at[t], device_id=(x, y, 1 - c), device_id_type=MESH) for t in range(n)]
        for cp in copies:
            cp.start()
        for cp in copies:
            cp.wait()

    return pl.pallas_call(
        body, name="rs_pair", in_specs=[_any_spec()] * n, out_specs=[_any_spec()] * n,
        out_shape=[jax.ShapeDtypeStruct((N_CHIPS, g.shape[1] // 2) + g.shape[2:], F32) for g in gs],
        scratch_shapes=[pltpu.SemaphoreType.DMA((n,)), pltpu.SemaphoreType.DMA((n,))],
        compiler_params=pltpu.CompilerParams(has_side_effects=True))(*gs)


def rs_cross(pairs):
    n = len(pairs)

    def body(*refs):
        ins, outs = refs[:n], refs[n:2 * n]
        send_sems, recv_sems = refs[2 * n:]
        x, y, c = lax.axis_index("x"), lax.axis_index("y"), lax.axis_index("c")
        copies = []
        for k, chip in enumerate(_other_chips(x, y)):
            for t in range(n):
                copies.append(pltpu.make_async_remote_copy(
                    src_ref=ins[t].at[2 * chip[0] + chip[1]], dst_ref=outs[t].at[k], send_sem=send_sems.at[3 * t + k],
                    recv_sem=recv_sems.at[3 * t + k], device_id=(*chip, c), device_id_type=MESH))
        for cp in copies:
            cp.start()
        for cp in copies:
            cp.wait()

    return pl.pallas_call(
        body, name="rs_cross", in_specs=[_any_spec()] * n, out_specs=[_any_spec()] * n,
        out_shape=[jax.ShapeDtypeStruct((3,) + p.shape[1:], p.dtype) for p in pairs],
        scratch_shapes=[pltpu.SemaphoreType.DMA((3 * n,)), pltpu.SemaphoreType.DMA((3 * n,))],
        compiler_params=pltpu.CompilerParams(has_side_effects=True))(*pairs)


def rs_share(blocks):
    n = len(blocks)

    def body(*refs):
        ins, outs = refs[:n], refs[n:2 * n]
        send_sems, recv_sems = refs[2 * n:]
        x, y, c = lax.axis_index("x"), lax.axis_index("y"), lax.axis_index("c")
        sends = [pltpu.make_async_remote_copy(
            src_ref=_half(ins[t], c), dst_ref=_half(outs[t], c), send_sem=send_sems.at[t], recv_sem=recv_sems.at[t],
            device_id=(x, y, 1 - c), device_id_type=MESH) for t in range(n)]
        for cp in sends:
            cp.start()
        for t in range(n):
            pltpu.make_async_remote_copy(
                src_ref=_half(ins[t], c), dst_ref=_half(outs[t], 1 - c), send_sem=send_sems.at[t],
                recv_sem=recv_sems.at[t], device_id=(x, y, 1 - c), device_id_type=MESH).wait_recv()
        for cp in sends:
            cp.wait_send()

    return pl.pallas_call(
        body, name="rs_share", in_specs=[_any_spec()] * n, out_specs=[_any_spec()] * n,
        out_shape=[jax.ShapeDtypeStruct(b.shape, F32) for b in blocks],
        input_output_aliases={t: t for t in range(n)},
        scratch_shapes=[pltpu.SemaphoreType.DMA((n,)), pltpu.SemaphoreType.DMA((n,))],
        compiler_params=pltpu.CompilerParams(has_side_effects=True))(*blocks)


def _row_tile(rows, target):
    best = None
    for t in range(8, min(rows, target) + 1, 8):
        if rows % t == 0:
            best = t
    return rows if best is None else best


TILE_BYTES = 2 * 1024 * 1024


def _flat_rows(shape):
    rows = int(np.prod(shape[1:-1]))
    cols_padded = -(-shape[-1] // LANES) * LANES
    return rows, _row_tile(rows, max(8, TILE_BYTES // (4 * cols_padded)))


def pair_add(g, from_sibling, out_dtype):
    cols = g.shape[-1]
    rph, t = _flat_rows(from_sibling.shape)
    nt = rph // t
    c_arr = lax.axis_index("c").astype(jnp.int32).reshape(1)

    def body(c_ref, a_ref, b_ref, o_ref):
        o_ref[...] = (a_ref[...] + b_ref[...]).astype(o_ref.dtype)

    out = pl.pallas_call(
        body, name="pair_add",
        grid_spec=pltpu.PrefetchScalarGridSpec(
            num_scalar_prefetch=1, grid=(N_CHIPS, nt),
            in_specs=[pl.BlockSpec((t, cols), lambda j, i, c_ref: (j * 2 * nt + c_ref[0] * nt + i, 0)),
                      pl.BlockSpec((t, cols), lambda j, i, c_ref: (j * nt + i, 0))],
            out_specs=pl.BlockSpec((t, cols), lambda j, i, c_ref: (j * nt + i, 0))),
        out_shape=jax.ShapeDtypeStruct((N_CHIPS * rph, cols), out_dtype),
        compiler_params=_cparams(("parallel", "parallel")))(c_arr, g.reshape(-1, cols), from_sibling.reshape(-1, cols))
    return out.reshape(from_sibling.shape)


def chip_add(pairs, received):
    cols = pairs.shape[-1]
    rph, t = _flat_rows(pairs.shape)
    nt = rph // t
    j_arr = _chip_index().astype(jnp.int32).reshape(1)
    c_arr = lax.axis_index("c").astype(jnp.int32).reshape(1)
    r2 = received.reshape(-1, cols)

    def body(j_ref, c_ref, a_ref, r0_ref, r1_ref, r2_ref, o_ref):
        a, r0, r1, r2 = [ref[...].astype(F32) for ref in (a_ref, r0_ref, r1_ref, r2_ref)]
        o_ref[...] = (a + r0) + (r1 + r2)

    out = pl.pallas_call(
        body, name="chip_add",
        grid_spec=pltpu.PrefetchScalarGridSpec(
            num_scalar_prefetch=2, grid=(nt,),
            in_specs=[pl.BlockSpec((t, cols), lambda i, j_ref, c_ref: (j_ref[0] * nt + i, 0)),
                      pl.BlockSpec((t, cols), lambda i, j_ref, c_ref: (i, 0)),
                      pl.BlockSpec((t, cols), lambda i, j_ref, c_ref: (nt + i, 0)),
                      pl.BlockSpec((t, cols), lambda i, j_ref, c_ref: (2 * nt + i, 0))],
            out_specs=pl.BlockSpec((t, cols), lambda i, j_ref, c_ref: (c_ref[0] * nt + i, 0))),
        out_shape=jax.ShapeDtypeStruct((2 * rph, cols), F32),
        compiler_params=_cparams(("parallel",)))(j_arr, c_arr, pairs.reshape(-1, cols), r2, r2, r2)
    return out.reshape((2 * pairs.shape[1],) + pairs.shape[2:])


def reduce_grads(gs, cross_dtypes):
    pairs = [pair_add(g, r, dt) for g, r, dt in zip(gs, rs_pair(gs), cross_dtypes)]
    return rs_share([chip_add(p, r) for p, r in zip(pairs, rs_cross(pairs))])


def adamw(w, g, m, v):
    shape = w.shape
    cols = shape[-1]
    rows = int(np.prod(shape[:-1]))
    flat = lambda a: a.reshape(rows, cols)
    t = _row_tile(rows, 256)

    def body(w_ref, g_ref, m_ref, v_ref, d_ref, mo_ref, vo_ref):
        gv = g_ref[...]
        m_new = ADAM_B1 * m_ref[...] + (1.0 - ADAM_B1) * gv
        v_new = ADAM_B2 * v_ref[...] + (1.0 - ADAM_B2) * (gv * gv)
        m_hat = m_new / (1.0 - ADAM_B1 ** ADAM_STEP)
        v_hat = v_new / (1.0 - ADAM_B2 ** ADAM_STEP)
        d_ref[...] = -ADAM_LR * (m_hat / (jnp.sqrt(v_hat) + ADAM_EPS) + ADAM_WD * w_ref[...])
        mo_ref[...] = m_new
        vo_ref[...] = v_new

    spec = pl.BlockSpec((t, cols), lambda i: (i, 0))
    outs = pl.pallas_call(
        body, name="adamw", grid=(rows // t,), in_specs=[spec] * 4, out_specs=[spec] * 3,
        out_shape=[jax.ShapeDtypeStruct((rows, cols), F32)] * 3,
        compiler_params=_cparams(("parallel",)))(flat(w), flat(g), flat(m), flat(v))
    return tuple(o.reshape(shape) for o in outs)


SHARDED = (("w_in", 2), ("mla_q_up", 2), ("mla_kv_up", 2), ("gdn_conv_w", 2), ("w_out", 1))
REPLICATED = ("b_mod", "pre_norm_w", "post_norm_w", "mla_q_norm_w", "mla_kv_norm_w", "gdn_a_log", "gdn_dt_bias",
              "gdn_o_norm_w")
WEIGHT_ORDER = ("w_mod", "b_mod", "pre_norm_w", "post_norm_w", "w_in", "mla_q_norm_w", "mla_q_up", "mla_kv_norm_w",
                "mla_kv_up", "gdn_conv_w", "gdn_a_log", "gdn_dt_bias", "gdn_o_norm_w", "w_out")
EXACT_F32 = ("gdn_conv_w",)
SMALL_ROWS = 48


def _gather_weights(shards):
    names = [name for name, _ in SHARDED]
    own = [shards[n] if n in EXACT_F32 else shards[n].astype(BF16) for n in names]
    gathered = ag_weights(own)
    full = {}
    for (name, axis), blk, mine in zip(SHARDED, gathered, own):
        shp = shards[name].shape
        blk = lax.dynamic_update_index_in_dim(blk, mine, _chip_index(), 0)
        blk = jnp.moveaxis(blk.astype(F32), 0, axis)
        full[name] = blk.reshape(shp[:axis] + (N_CHIPS * shp[axis],) + shp[axis + 1:])
    return full


def _split_grads(grads):
    pieces = []
    for name, axis in SHARDED:
        g = grads[name]
        shp = g.shape
        g = g.reshape(shp[:axis] + (N_CHIPS, shp[axis] // N_CHIPS) + shp[axis + 1:])
        pieces.append(jnp.moveaxis(g, axis, 0))
    small = jnp.concatenate([grads[name] for name in REPLICATED], axis=1)
    small = jnp.pad(small, ((0, 0), (0, SMALL_ROWS * LANES - small.shape[1]))).reshape(DEPTH, SMALL_ROWS, LANES)
    pieces.append(jnp.broadcast_to(small[None], (N_CHIPS,) + small.shape))
    return pieces


def _unsplit_small(small, rep_shapes):
    flat = small.reshape(DEPTH, SMALL_ROWS * LANES)
    out, off = {}, 0
    for name in REPLICATED:
        size = rep_shapes[name][1]
        out[name] = flat[:, off:off + size]
        off += size
    return out


MOD_ROWS = 16


def _device_slot():
    return 4 * lax.axis_index("x") + 2 * lax.axis_index("y") + lax.axis_index("c")


def _adaln_projection(c, w_mod_shard):
    (c_act,) = _OPS["cact"]((jnp.pad(c, ((0, 7), (0, 0))),), ())
    c_acts = jnp.pad(all_gather_rows(c_act)[:, 0, :], ((0, MOD_ROWS - 8), (0, 0)))
    part = jnp.concatenate([_mm(c_acts, w_mod_shard[l].astype(BF16), "nn", "mod_proj") for l in range(DEPTH)], axis=0)
    parts = all_gather_rows(part)[::2].reshape(N_CHIPS, DEPTH, MOD_ROWS, -1)
    mods = jnp.moveaxis(parts, 0, 2).reshape(DEPTH, MOD_ROWS, -1)
    return c_acts, lax.dynamic_slice_in_dim(mods, _device_slot(), 1, axis=1)


def _adaln_weight_grad(c_acts, d_mods):
    cols = d_mods.shape[-1] // N_CHIPS
    rows = jnp.pad(d_mods[:, 0, :], ((0, 8 - DEPTH), (0, 0)))
    all_rows = all_gather_rows(rows)[:, :DEPTH, :]
    mine = lax.dynamic_slice_in_dim(all_rows, _chip_index() * cols, cols, axis=2)
    mine = jnp.pad(mine, ((0, MOD_ROWS - 8), (0, 0), (0, 0)))
    return jnp.stack([_mm(c_acts, mine[:, l, :], "tn", "mod_dw") for l in range(DEPTH)])


def kernel(x, c, positions, w_mod, b_mod, pre_norm_w, post_norm_w, w_in, mla_q_norm_w, mla_q_up, mla_kv_norm_w, mla_kv_up, gdn_conv_w, gdn_a_log, gdn_dt_bias, gdn_o_norm_w, w_out, loss_target, m_w_mod, m_b_mod, m_pre_norm_w, m_post_norm_w, m_w_in, m_mla_q_norm_w, m_mla_q_up, m_mla_kv_norm_w, m_mla_kv_up, m_gdn_conv_w, m_gdn_a_log, m_gdn_dt_bias, m_gdn_o_norm_w, m_w_out, v_w_mod, v_b_mod, v_pre_norm_w, v_post_norm_w, v_w_in, v_mla_q_norm_w, v_mla_q_up, v_mla_kv_norm_w, v_mla_kv_up, v_gdn_conv_w, v_gdn_a_log, v_gdn_dt_bias, v_gdn_o_norm_w, v_w_out):
    given = dict(w_mod=w_mod, b_mod=b_mod, pre_norm_w=pre_norm_w, post_norm_w=post_norm_w, w_in=w_in,
                 mla_q_norm_w=mla_q_norm_w, mla_q_up=mla_q_up, mla_kv_norm_w=mla_kv_norm_w, mla_kv_up=mla_kv_up,
                 gdn_conv_w=gdn_conv_w, gdn_a_log=gdn_a_log, gdn_dt_bias=gdn_dt_bias, gdn_o_norm_w=gdn_o_norm_w,
                 w_out=w_out)
    moments_m = dict(w_mod=m_w_mod, b_mod=m_b_mod, pre_norm_w=m_pre_norm_w, post_norm_w=m_post_norm_w, w_in=m_w_in,
                     mla_q_norm_w=m_mla_q_norm_w, mla_q_up=m_mla_q_up, mla_kv_norm_w=m_mla_kv_norm_w,
                     mla_kv_up=m_mla_kv_up, gdn_conv_w=m_gdn_conv_w, gdn_a_log=m_gdn_a_log,
                     gdn_dt_bias=m_gdn_dt_bias, gdn_o_norm_w=m_gdn_o_norm_w, w_out=m_w_out)
    moments_v = dict(w_mod=v_w_mod, b_mod=v_b_mod, pre_norm_w=v_pre_norm_w, post_norm_w=v_post_norm_w, w_in=v_w_in,
                     mla_q_norm_w=v_mla_q_norm_w, mla_q_up=v_mla_q_up, mla_kv_norm_w=v_mla_kv_norm_w,
                     mla_kv_up=v_mla_kv_up, gdn_conv_w=v_gdn_conv_w, gdn_a_log=v_gdn_a_log,
                     gdn_dt_bias=v_gdn_dt_bias, gdn_o_norm_w=v_gdn_o_norm_w, w_out=v_w_out)

    full = _gather_weights({name: given[name] for name, _ in SHARDED})
    for name in REPLICATED:
        full[name] = given[name]
    c_acts, mods = _adaln_projection(c, w_mod)
    loss_local, (grads, d_mods, grad_x) = jax.value_and_grad(_local_loss, argnums=(0, 1, 2))(
        full, mods, x[0], positions[0], loss_target[0])
    loss = lax.psum(loss_local, AXES)

    cross_dtypes = [F32 if name in EXACT_F32 else BF16 for name, _ in SHARDED] + [F32]
    reduced = reduce_grads(_split_grads(grads), cross_dtypes)
    grad_w = {name: g for (name, _), g in zip(SHARDED, reduced)}
    grad_w.update(_unsplit_small(reduced[-1], {name: given[name].shape for name in REPLICATED}))
    grad_w["w_mod"] = _adaln_weight_grad(c_acts, d_mods)
    delta, new_m, new_v = {}, {}, {}
    for name in WEIGHT_ORDER:
        delta[name], new_m[name], new_v[name] = adamw(given[name], grad_w[name], moments_m[name], moments_v[name])
    return (loss, grad_x[None], *[grad_w[n] for n in WEIGHT_ORDER], *[delta[n] for n in WEIGHT_ORDER],
            *[new_m[n] for n in WEIGHT_ORDER], *[new_v[n] for n in WEIGHT_ORDER])
```

```python
import functools
import math

import numpy as np
import jax
import jax.numpy as jnp
from jax import lax
from jax.experimental import pallas as pl
from jax.experimental.pallas import tpu as pltpu

F32 = jnp.float32
BF16 = jnp.bfloat16
MESH = pl.DeviceIdType.MESH
AXES = ("x", "y", "c")

D_MODEL = 1024
DEPTH = 4
MLA_HEADS = 4
MLA_NOPE = 128
MLA_ROPE = 64
MLA_V = 128
MLA_Q_RANK = 384
MLA_KV_RANK = 256
MLA_WIDTH = 512
GDN_HEADS = 4
GDN_DK = 128
GDN_WIDTH = 512
GDN_QKV = 1536
GDN_CONV = 4
IN_COLS = 3272
ROPE_THETA = 10000.0
NORM_EPS = 1e-6
ADAM_LR, ADAM_B1, ADAM_B2, ADAM_EPS, ADAM_WD, ADAM_STEP = 0.001, 0.9, 0.999, 1e-08, 0.01, 10

LANES = 128
N_CHIPS = 4
GDN_CHUNK = 128
VMEM_LIMIT = 56 * 1024 * 1024


def _cparams(sem=None):
    if sem is None:
        return pltpu.CompilerParams(vmem_limit_bytes=VMEM_LIMIT)
    return pltpu.CompilerParams(dimension_semantics=sem, vmem_limit_bytes=VMEM_LIMIT)


def _pick(dim, target):
    if dim <= target:
        return dim
    best = None
    for t in range(LANES, target + 1, LANES):
        if dim % t == 0:
            best = t
    assert best is not None, (dim, target)
    return best


def _resident(shape):
    return pl.BlockSpec(shape, lambda i: (0,) * len(shape), pipeline_mode=pl.Buffered(1))


_DN = {"nn": (((1,), (0,)), ((), ())), "nt": (((1,), (1,)), ((), ())), "tn": (((0,), (0,)), ((), ()))}


def _dot_raw(a, b, mode, exact):
    if exact:
        return lax.dot_general(a, b, _DN[mode], precision=lax.Precision.HIGHEST, preferred_element_type=F32)
    return lax.dot_general(a.astype(BF16), b.astype(BF16), _DN[mode], preferred_element_type=F32)


def _dot_split(a, b):
    a_hi, b_hi = a.astype(BF16), b.astype(BF16)
    a_lo, b_lo = (a - a_hi.astype(F32)).astype(BF16), (b - b_hi.astype(F32)).astype(BF16)
    dot = lambda u, w: lax.dot_general(u, w, _DN["nn"], preferred_element_type=F32)
    return dot(a_hi, b_hi) + (dot(a_hi, b_lo) + dot(a_lo, b_hi))


@functools.partial(jax.custom_vjp, nondiff_argnums=(2, 3))
def bdot(a, b, mode="nn", exact=False):
    return _dot_raw(a, b, mode, exact)


def _bdot_fwd(a, b, mode, exact):
    return _dot_raw(a, b, mode, exact), (a, b)


def _bdot_bwd(mode, exact, res, g):
    a, b = res
    if mode == "nn":
        return bdot(g, b, "nt", exact), bdot(a, g, "tn", exact)
    if mode == "nt":
        return bdot(g, b, "nn", exact), bdot(g, a, "tn", exact)
    return bdot(b, g, "nt", exact), bdot(a, g, "nn", exact)


bdot.defvjp(_bdot_fwd, _bdot_bwd)


@jax.custom_vjp
def roll_half(x):
    return pltpu.roll(x, 64, 1)


roll_half.defvjp(lambda x: (pltpu.roll(x, 64, 1), None), lambda _, g: (pltpu.roll(g, 64, 1),))


def _sigmoid(x):
    return 1.0 / (1.0 + jnp.exp(-x))


def _silu(x):
    return x * _sigmoid(x)


def _softplus(x):
    return jnp.maximum(x, 0.0) + jnp.log(1.0 + jnp.exp(-jnp.abs(x)))


def _rms(x, w):
    return x * lax.rsqrt(jnp.mean(x * x, axis=-1, keepdims=True) + NORM_EPS) * w


def _rw_fwd(fn, name, rows, params, out_dims, out_dtypes, acc_dims, tile):
    s = rows[0].shape[0]
    t = min(tile, s)
    n = s // t
    nr, npar, no, na = len(rows), len(params), len(out_dims), len(acc_dims)

    def body(*refs):
        r, p = refs[:nr], refs[nr:nr + npar]
        o, a = refs[nr + npar:nr + npar + no], refs[nr + npar + no:]
        outs = fn(*[x[...] for x in r], *[x[...] for x in p])
        for ref, val in zip(o, outs[:no]):
            ref[...] = val.astype(ref.dtype)
        if na:
            @pl.when(pl.program_id(0) == 0)
            def _():
                for ref in a:
                    ref[...] = jnp.zeros_like(ref)
            for ref, val in zip(a, outs[no:]):
                ref[...] += val

    in_specs = [pl.BlockSpec((t, x.shape[1]), lambda i: (i, 0)) for x in rows]
    in_specs += [_resident(x.shape) for x in params]
    out_specs = [pl.BlockSpec((t, d), lambda i: (i, 0)) for d in out_dims]
    out_specs += [_resident((1, d)) for d in acc_dims]
    out_shape = [jax.ShapeDtypeStruct((s, d), dt) for d, dt in zip(out_dims, out_dtypes)]
    out_shape += [jax.ShapeDtypeStruct((1, d), F32) for d in acc_dims]
    res = pl.pallas_call(body, name=name, grid=(n,), in_specs=in_specs, out_specs=out_specs, out_shape=out_shape,
                         compiler_params=_cparams(("arbitrary",)))(*rows, *params)
    return tuple(res)


def _rw_bwd(fn, name, rows, params, row_cts, acc_cts, n_diff, tile):
    s = rows[0].shape[0]
    t = min(tile, s)
    n = s // t
    nr, npar, no, na = len(rows), len(params), len(row_cts), len(acc_cts)

    def body(*refs):
        r, p = refs[:nr], refs[nr:nr + npar]
        g, ga = refs[nr + npar:nr + npar + no], refs[nr + npar + no:nr + npar + no + na]
        dr, dp = refs[nr + npar + no + na:nr + npar + no + na + n_diff], refs[nr + npar + no + na + n_diff:]
        _, vjp = jax.vjp(fn, *[x[...] for x in r], *[x[...] for x in p])
        cts = vjp(tuple([x[...] for x in g] + [x[...] for x in ga]))
        for ref, val in zip(dr, cts[:n_diff]):
            ref[...] = val
        if npar:
            @pl.when(pl.program_id(0) == 0)
            def _():
                for ref in dp:
                    ref[...] = jnp.zeros_like(ref)
            for ref, val in zip(dp, cts[nr:]):
                ref[...] += val

    in_specs = [pl.BlockSpec((t, x.shape[1]), lambda i: (i, 0)) for x in rows]
    in_specs += [_resident(x.shape) for x in params]
    in_specs += [pl.BlockSpec((t, x.shape[1]), lambda i: (i, 0)) for x in row_cts]
    in_specs += [_resident(x.shape) for x in acc_cts]
    out_specs = [pl.BlockSpec((t, x.shape[1]), lambda i: (i, 0)) for x in rows[:n_diff]]
    out_specs += [_resident(x.shape) for x in params]
    out_shape = [jax.ShapeDtypeStruct(x.shape, F32) for x in rows[:n_diff]]
    out_shape += [jax.ShapeDtypeStruct(x.shape, F32) for x in params]
    res = pl.pallas_call(body, name=name, grid=(n,), in_specs=in_specs, out_specs=out_specs, out_shape=out_shape,
                         compiler_params=_cparams(("arbitrary",)))(*rows, *params, *row_cts, *acc_cts)
    return tuple(res[:n_diff]), tuple(res[n_diff:])


def make_rowwise(fn, name, out_dims, acc_dims=(), n_nondiff=0, tile=256):
    out_dtypes = (F32,) * len(out_dims)

    @jax.custom_vjp
    def op(rows, params):
        return _rw_fwd(fn, name, rows, params, out_dims, out_dtypes, acc_dims, tile)

    def fwd(rows, params):
        return op(rows, params), (rows, params)

    def bwd(res, cts):
        rows, params = res
        n_diff = len(rows) - n_nondiff
        d_rows, d_params = _rw_bwd(fn, name + "_bwd", rows, params, cts[:len(out_dims)], cts[len(out_dims):],
                                   n_diff, tile)
        d_rows = d_rows + tuple(jnp.zeros_like(x) for x in rows[n_diff:])
        return d_rows, d_params

    op.defvjp(fwd, bwd)
    return op


def _mm(a, b, mode, name):
    if mode == "nn":
        (m, k), (_, n) = a.shape, b.shape
    elif mode == "nt":
        (m, k), (n, _) = a.shape, b.shape
    else:
        (k, m), (_, n) = a.shape, b.shape
    tm = _pick(m, 512)
    tn = _pick(n, 1152)
    tk = _pick(k, 1152) if mode != "tn" else _pick(k, 512)
    nk = k // tk

    def body(a_ref, b_ref, o_ref, acc_ref):
        kk = pl.program_id(2)

        @pl.when(kk == 0)
        def _():
            acc_ref[...] = jnp.zeros_like(acc_ref)

        acc_ref[...] += _dot_raw(a_ref[...], b_ref[...], mode, False)

        @pl.when(kk == nk - 1)
        def _():
            o_ref[...] = acc_ref[...]

    if mode == "nn":
        a_spec = pl.BlockSpec((tm, tk), lambda i, j, kk: (i, kk))
        b_spec = pl.BlockSpec((tk, tn), lambda i, j, kk: (kk, j))
    elif mode == "nt":
        a_spec = pl.BlockSpec((tm, tk), lambda i, j, kk: (i, kk))
        b_spec = pl.BlockSpec((tn, tk), lambda i, j, kk: (j, kk))
    else:
        a_spec = pl.BlockSpec((tk, tm), lambda i, j, kk: (kk, i))
        b_spec = pl.BlockSpec((tk, tn), lambda i, j, kk: (kk, j))
    return pl.pallas_call(
        body, name=name, grid=(m // tm, n // tn, nk), in_specs=[a_spec, b_spec],
        out_specs=pl.BlockSpec((tm, tn), lambda i, j, kk: (i, j)),
        out_shape=jax.ShapeDtypeStruct((m, n), F32), scratch_shapes=[pltpu.VMEM((tm, tn), F32)],
        compiler_params=_cparams(("parallel", "parallel", "arbitrary")))(a, b)


def _mla_prep_fn(qraw, kvraw, kr, t1, t2):
    kr_rot = kr * t1 + roll_half(kr) * t2
    qs, ks = [], []
    for h in range(MLA_HEADS):
        q_r = qraw[:, h * 256 + 128:(h + 1) * 256]
        qs += [qraw[:, h * 256:h * 256 + 128], q_r * t1 + roll_half(q_r) * t2]
        ks += [kvraw[:, h * 128:(h + 1) * 128], kr_rot]
    return jnp.concatenate(qs, axis=1), jnp.concatenate(ks, axis=1), kvraw[:, 512:]


def _flash_tile(s):
    return 512 if s >= 2048 else 128


FLASH_SCALE = (MLA_NOPE + MLA_ROPE) ** -0.5
LOG2_E = 1.4426950408889634
FLASH_EXP2 = FLASH_SCALE * LOG2_E
STAT_ROWS = 8
FLASH_PAIR = 2
FLASH_STRIP = 32


def _as_rows(col_b):
    ones = jnp.full((STAT_ROWS, LANES), 1.0 / LANES, F32)
    return _dot_raw(ones, col_b, "nt", True)


def _flash_fwd(qf, kf, vf):
    s = qf.shape[0]
    t = _flash_tile(s)
    nb = s // t
    pair = range(FLASH_PAIR)

    ck = min(FLASH_STRIP, t)

    def body(q_ref, k_ref, v_ref, o_ref, lse_ref, m_sc, l_sc, acc_sc, st_sc, pt_sc):
        i = pl.program_id(1)
        m_sc[...] = jnp.full_like(m_sc, -1e30)
        l_sc[...] = jnp.zeros_like(l_sc)
        acc_sc[...] = jnp.zeros_like(acc_sc)
        qs = [q_ref[:, hh * 256:(hh + 1) * 256] for hh in pair]

        def step(j, on_diagonal):
            rows = pl.ds(pl.multiple_of(j * t, t), t)
            for hh in pair:
                st = _dot_raw(k_ref[rows, hh * 256:(hh + 1) * 256], qs[hh], "nt", False)
                if on_diagonal:
                    keep = lax.broadcasted_iota(jnp.int32, (t, t), 0) <= lax.broadcasted_iota(jnp.int32, (t, t), 1)
                    st = jnp.where(keep, st, -1e30)
                st_sc[hh] = st
            m_olds = [m_sc[hh] for hh in pair]
            m_news = [jnp.maximum(m_olds[hh], jnp.max(st_sc[hh], axis=0, keepdims=True)) for hh in pair]
            alphas = [jnp.exp2((m_olds[hh] - m_news[hh]) * FLASH_EXP2) for hh in pair]
            shifts = [m_news[hh] * FLASH_EXP2 for hh in pair]
            sums = [jnp.zeros((8, t), F32) for _ in pair]
            for hh in pair:
                for r in range(0, t, ck):
                    p = jnp.exp2(st_sc[hh, r:r + ck, :] * FLASH_EXP2 - shifts[hh])
                    pt_sc[hh, r:r + ck, :] = p.astype(BF16)
                    sums[hh] = sums[hh] + functools.reduce(lambda a, b: a + b, [p[u:u + 8] for u in range(0, ck, 8)])
            pvs = [_dot_raw(v_ref[rows, hh * 128:(hh + 1) * 128], pt_sc[hh], "tn", False) for hh in pair]
            for hh in pair:
                l_sc[hh] = alphas[hh] * l_sc[hh] + jnp.sum(sums[hh], axis=0, keepdims=True)
                acc_sc[hh] = alphas[hh] * acc_sc[hh] + pvs[hh]
                m_sc[hh] = m_news[hh]

        def two_steps(p, carry):
            step(2 * p, False)
            step(2 * p + 1, False)
            return carry

        lax.fori_loop(0, i // 2, two_steps, 0)

        @pl.when(i % 2 == 1)
        def _():
            step(i - 1, False)

        step(i, True)
        for hh in pair:
            o_ref[:, hh * 128:(hh + 1) * 128] = (acc_sc[hh] / l_sc[hh]).T
            lse2 = m_sc[hh] * FLASH_EXP2 + jnp.log(l_sc[hh]) * LOG2_E
            lse_ref[hh] = jnp.broadcast_to(lse2, (STAT_ROWS, t))

    p = FLASH_PAIR
    return pl.pallas_call(
        body, name="flash_fwd", grid=(MLA_HEADS // p, nb),
        in_specs=[pl.BlockSpec((t, p * 256), lambda h, i: (i, h)), pl.BlockSpec((s, p * 256), lambda h, i: (0, h)),
                  pl.BlockSpec((s, p * 128), lambda h, i: (0, h))],
        out_specs=[pl.BlockSpec((t, p * 128), lambda h, i: (i, h)),
                   pl.BlockSpec((p, STAT_ROWS, t), lambda h, i: (h, 0, i))],
        out_shape=[jax.ShapeDtypeStruct((s, MLA_WIDTH), F32),
                   jax.ShapeDtypeStruct((MLA_HEADS, STAT_ROWS, s), F32)],
        scratch_shapes=[pltpu.VMEM((p, 1, t), F32), pltpu.VMEM((p, 1, t), F32), pltpu.VMEM((p, 128, t), F32),
                        pltpu.VMEM((p, t, t), F32), pltpu.VMEM((p, t, t), BF16)],
        compiler_params=_cparams(("parallel", "arbitrary")))(qf, kf, vf)


def _flash_bwd(qf, kf, vf, lse, o, do):
    s = qf.shape[0]
    t = _flash_tile(s)
    nb = s // t

    def body(q_ref, k_ref, v_ref, lse_ref, o_ref, do_ref, dq_ref, dk_ref, dv_ref, dl_sc, do16_sc):
        j = pl.program_id(1)

        @pl.when(j == 0)
        def _():
            dq_ref[...] = jnp.zeros_like(dq_ref)

            def stage(b, carry):
                rows = pl.ds(pl.multiple_of(b * t, t), t)
                do_t = do_ref[rows, :]
                delta = jnp.sum(do_t * o_ref[rows, :], axis=1, keepdims=True)
                dl_sc[:, rows] = _as_rows(jnp.broadcast_to(delta, (t, LANES)))
                do16_sc[rows, :] = do_t.astype(BF16)
                return carry

            lax.fori_loop(0, nb, stage, 0)

        dk_ref[...] = jnp.zeros_like(dk_ref)
        dv_ref[...] = jnp.zeros_like(dv_ref)
        k = k_ref[...]
        v = v_ref[...]

        def step(i, on_diagonal):
            rows = pl.ds(pl.multiple_of(i * t, t), t)
            q = q_ref[rows, :]
            do_t = do16_sc[rows, :]
            st = _dot_raw(k, q, "nt", False) * FLASH_EXP2 - lse_ref[0, 0:1, rows]
            if on_diagonal:
                keep = lax.broadcasted_iota(jnp.int32, (t, t), 0) <= lax.broadcasted_iota(jnp.int32, (t, t), 1)
                st = jnp.where(keep, st, -1e30)
            pt = jnp.exp2(st)
            dst = pt * (_dot_raw(v, do_t, "nt", False) - dl_sc[0:1, rows])
            dv_ref[...] += _dot_raw(pt, do_t, "nn", False)
            dk_ref[...] += _dot_raw(dst, q, "nn", False)
            dq_ref[rows, :] += _dot_raw(dst, k, "tn", False)

        def two_steps(p, carry):
            step(j + 1 + 2 * p, False)
            step(j + 2 + 2 * p, False)
            return carry

        step(j, True)
        below = nb - 1 - j
        lax.fori_loop(0, below // 2, two_steps, 0)

        @pl.when(below % 2 == 1)
        def _():
            step(nb - 1, False)

        dk_ref[...] *= FLASH_SCALE

        @pl.when(j == nb - 1)
        def _():
            dq_ref[...] *= FLASH_SCALE

    per_head = lambda d: pl.BlockSpec((s, d), lambda h, j: (0, h), pipeline_mode=pl.Buffered(1))
    return pl.pallas_call(
        body, name="flash_bwd", grid=(MLA_HEADS, nb),
        in_specs=[per_head(256), pl.BlockSpec((t, 256), lambda h, j: (j, h)),
                  pl.BlockSpec((t, 128), lambda h, j: (j, h)),
                  pl.BlockSpec((1, STAT_ROWS, s), lambda h, j: (h, 0, 0)), per_head(128), per_head(128)],
        out_specs=[pl.BlockSpec((s, 256), lambda h, j: (0, h)), pl.BlockSpec((t, 256), lambda h, j: (j, h)),
                   pl.BlockSpec((t, 128), lambda h, j: (j, h))],
        out_shape=[jax.ShapeDtypeStruct((s, 1024), F32), jax.ShapeDtypeStruct((s, 1024), F32),
                   jax.ShapeDtypeStruct((s, MLA_WIDTH), F32)],
        scratch_shapes=[pltpu.VMEM((STAT_ROWS, s), F32), pltpu.VMEM((s, 128), BF16)],
        compiler_params=_cparams(("parallel", "arbitrary")))(qf, kf, vf, lse, o, do)


MLA_FRONT_TILE = 512


def _mla_front_fn(q_lat, kv_lat, kr, t1, t2, q_norm_w, q_up, kv_norm_w, kv_up):
    qraw = bdot(_rms(q_lat, q_norm_w), q_up)
    kvraw = bdot(_rms(kv_lat, kv_norm_w), kv_up)
    return _mla_prep_fn(qraw, kvraw, kr, t1, t2)


def _mla_front(rows, params):
    return _rw_fwd(_mla_front_fn, "mla_front", rows, params, (1024, 1024, 512), (BF16, BF16, BF16), (),
                   MLA_FRONT_TILE)


@jax.custom_vjp
def mla_attention(rows, params):
    return _flash_fwd(*_mla_front(rows, params))[0]


def _mla_attention_fwd(rows, params):
    qf, kf, vf = _mla_front(rows, params)
    o, lse = _flash_fwd(qf, kf, vf)
    return o, (rows, params, qf, kf, vf, o, lse)


def _mla_attention_bwd(res, do):
    rows, params, qf, kf, vf, o, lse = res
    d_rows, d_params = _rw_bwd(_mla_front_fn, "mla_front_bwd", rows, params, _flash_bwd(qf, kf, vf, lse, o, do), (),
                               3, MLA_FRONT_TILE)
    return d_rows + (jnp.zeros_like(rows[3]), jnp.zeros_like(rows[4])), d_params


mla_attention.defvjp(_mla_attention_fwd, _mla_attention_bwd)


def _lane_pick(x, lane):
    ids = lax.broadcasted_iota(jnp.int32, x.shape, 1)
    col = jnp.sum(jnp.where(ids == lane, x, 0.0), axis=1, keepdims=True)
    return jnp.broadcast_to(col, x.shape)


GDN_HALO = 8


@functools.partial(jax.custom_vjp, nondiff_argnums=(1,))
def _roll_rows(x, d):
    return pltpu.roll(x, d, 0)


_roll_rows.defvjp(lambda x, d: (pltpu.roll(x, d, 0), None), lambda d, _, g: (pltpu.roll(g, g.shape[0] - d, 0),))


def _gdn_prep_fn(prev, cur, ab, w0, w1, w2, w3, a_log, dt_bias):
    xcat = jnp.concatenate([prev, cur], axis=0)
    x0, x1, x2 = [_roll_rows(xcat, GDN_CONV - 1 - j)[GDN_HALO:] for j in range(GDN_CONV - 1)]
    qkv = _silu(x0 * w0 + x1 * w1 + x2 * w2 + cur * w3)
    g_all = -jnp.exp(a_log) * _softplus(ab + dt_bias)
    beta_all = _sigmoid(ab)
    qs, ks, gs, bs = [], [], [], []
    for h in range(GDN_HEADS):
        q = qkv[:, h * 128:(h + 1) * 128]
        k = qkv[:, 512 + h * 128:512 + (h + 1) * 128]
        qs.append(q * lax.rsqrt(jnp.sum(q * q, axis=-1, keepdims=True) + NORM_EPS) * (GDN_DK ** -0.5))
        ks.append(k * lax.rsqrt(jnp.sum(k * k, axis=-1, keepdims=True) + NORM_EPS))
        gs.append(_lane_pick(g_all, h))
        bs.append(_lane_pick(beta_all, GDN_HEADS + h))
    cat = lambda xs: jnp.concatenate(xs, axis=1)
    return cat(qs), cat(ks), qkv[:, 1024:], cat(gs), cat(bs)


GDN_PREP_TILE = 256


def _gdn_prep_specs(s, params, reverse=False):
    t = min(GDN_PREP_TILE, s)
    n = s // t
    blk = (lambda i: n - 1 - i) if reverse else (lambda i: i)
    prev = pl.BlockSpec((GDN_HALO, GDN_QKV), lambda i: (jnp.maximum(blk(i) * (t // GDN_HALO) - 1, 0), 0))
    rows = lambda d: pl.BlockSpec((t, d), lambda i: (blk(i), 0))
    return t, rows, [prev, rows(GDN_QKV), rows(LANES)] + [pl.BlockSpec(p.shape, lambda i: (0, 0)) for p in params]


def _gdn_prep_masked(has_rows_before):
    return lambda prev, *rest: _gdn_prep_fn(prev * has_rows_before, *rest)


def _gdn_prep_fwd(qkv, ab, params):
    s = qkv.shape[0]
    t, rows, in_specs = _gdn_prep_specs(s, params)

    def body(prev_ref, cur_ref, ab_ref, *refs):
        p_refs, o_refs = refs[:len(params)], refs[len(params):]
        has_rows_before = (pl.program_id(0) > 0).astype(F32)
        outs = _gdn_prep_masked(has_rows_before)(prev_ref[...], cur_ref[...], ab_ref[...], *[p[...] for p in p_refs])
        for ref, val in zip(o_refs, outs):
            ref[...] = val

    return pl.pallas_call(
        body, name="gdn_prep", grid=(s // t,), in_specs=in_specs,
        out_specs=[rows(GDN_WIDTH)] * 5, out_shape=[jax.ShapeDtypeStruct((s, GDN_WIDTH), F32)] * 5,
        compiler_params=_cparams(("parallel",)))(qkv, qkv, ab, *params)


def _gdn_prep_bwd(qkv, ab, params, cts):
    s = qkv.shape[0]
    t, rows, in_specs = _gdn_prep_specs(s, params, reverse=True)
    n = s // t
    npar = len(params)

    def body(prev_ref, cur_ref, ab_ref, *refs):
        p_refs, g_refs = refs[:npar], refs[npar:npar + 5]
        dcur_ref, dab_ref = refs[npar + 5:npar + 7]
        dp_refs, carry_sc = refs[npar + 7:-1], refs[-1]

        @pl.when(pl.program_id(0) == 0)
        def _():
            carry_sc[...] = jnp.zeros_like(carry_sc)
            for ref in dp_refs:
                ref[...] = jnp.zeros_like(ref)

        has_rows_before = (pl.program_id(0) < n - 1).astype(F32)
        _, vjp = jax.vjp(_gdn_prep_masked(has_rows_before), prev_ref[...], cur_ref[...], ab_ref[...],
                         *[p[...] for p in p_refs])
        d_prev, d_cur, d_ab, *d_params = vjp(tuple(g[...] for g in g_refs))
        dcur_ref[...] = d_cur
        dcur_ref[t - GDN_HALO:, :] += carry_sc[...]
        carry_sc[...] = d_prev
        dab_ref[...] = d_ab
        for ref, val in zip(dp_refs, d_params):
            ref[...] += val

    res = pl.pallas_call(
        body, name="gdn_prep_bwd", grid=(n,), in_specs=in_specs + [rows(GDN_WIDTH)] * 5,
        out_specs=[rows(GDN_QKV), rows(LANES)] + [pl.BlockSpec(p.shape, lambda i: (0, 0)) for p in params],
        out_shape=[jax.ShapeDtypeStruct((s, GDN_QKV), F32), jax.ShapeDtypeStruct((s, LANES), F32)]
        + [jax.ShapeDtypeStruct(p.shape, F32) for p in params],
        scratch_shapes=[pltpu.VMEM((GDN_HALO, GDN_QKV), F32)],
        compiler_params=_cparams(("arbitrary",)))(qkv, qkv, ab, *params, *cts)
    return res[0], res[1], tuple(res[2:])


@jax.custom_vjp
def gdn_prep(qkv, ab, params):
    return tuple(_gdn_prep_fwd(qkv, ab, params))


def _gdn_prep_vjp_fwd(qkv, ab, params):
    return tuple(_gdn_prep_fwd(qkv, ab, params)), (qkv, ab, params)


def _gdn_prep_vjp_bwd(res, cts):
    qkv, ab, params = res
    return _gdn_prep_bwd(qkv, ab, params, cts)


gdn_prep.defvjp(_gdn_prep_vjp_fwd, _gdn_prep_vjp_bwd)


def _tri_dot(x, mode):
    c = x.shape[0]
    tri = (lax.broadcasted_iota(jnp.int32, (c, c), 0) >= lax.broadcasted_iota(jnp.int32, (c, c), 1)).astype(BF16)
    hi = x.astype(BF16)
    rest = x - hi.astype(F32)
    mid = rest.astype(BF16)
    lo = (rest - mid.astype(F32)).astype(BF16)
    dot = lambda part: lax.dot_general(tri, part, _DN[mode], preferred_element_type=F32)
    return dot(hi) + (dot(mid) + dot(lo))


@jax.custom_vjp
def _chunk_cumsum(x):
    return _tri_dot(x, "nn")


_chunk_cumsum.defvjp(lambda x: (_tri_dot(x, "nn"), None), lambda _, g: (_tri_dot(g, "tn"),))


@jax.custom_vjp
def _unit_lower_inverse(lms):
    c = lms[0].shape[0]
    row = lax.broadcasted_iota(jnp.int32, (c, c), 0)
    col = lax.broadcasted_iota(jnp.int32, (c, c), 1)
    ts = [(row == col).astype(F32) - jnp.where((row >> 1) == (col >> 1), lm, 0.0) for lm in lms]
    for level in range(1, int(math.log2(c))):
        below = ((row >> (level + 1)) == (col >> (level + 1))) & ((row >> level) != (col >> level))
        mids = [_dot_split(t, jnp.where(below, lm, 0.0)) for t, lm in zip(ts, lms)]
        ts = [t - _dot_split(mid, t) for t, mid in zip(ts, mids)]
    return tuple(ts)


def _uli_fwd(lms):
    ts = _unit_lower_inverse(lms)
    return ts, ts


def _uli_bwd(ts, gs):
    mids = [bdot(t, g, "tn") for t, g in zip(ts, gs)]
    return (tuple(-bdot(mid, t, "nt") for t, mid in zip(ts, mids)),)


_unit_lower_inverse.defvjp(_uli_fwd, _uli_bwd)


@jax.custom_vjp
def _known_inverse(lms, ts):
    return ts


_known_inverse.defvjp(lambda lms, ts: (ts, ts),
                      lambda ts, gs: (_uli_bwd(ts, gs)[0], tuple(jnp.zeros_like(t) for t in ts)))


def _gdn_chunk_fn(qs, ks, vs, gbs, bbs, s0s, known_ts=None):
    heads = range(len(qs))
    c = qs[0].shape[0]
    row = lax.broadcasted_iota(jnp.int32, (c, c), 0)
    col = lax.broadcasted_iota(jnp.int32, (c, c), 1)
    incl, strict = row >= col, row > col
    gc = [_chunk_cumsum(gbs[h]) for h in heads]
    decay = [jnp.exp(jnp.where(incl, gc[h] - gc[h].T, -1e30)) for h in heads]
    g_last = [jnp.sum(gbs[h], axis=0, keepdims=True) for h in heads]
    eg = [jnp.exp(gc[h]) for h in heads]
    kb = [ks[h] * bbs[h] for h in heads]
    lms = tuple(jnp.where(strict, bdot(kb[h], ks[h], "nt") * decay[h], 0.0) for h in heads)
    ts = _unit_lower_inverse(lms) if known_ts is None else _known_inverse(lms, known_ts)
    u = [bdot(ts[h], vs[h] * bbs[h]) for h in heads]
    w = [bdot(ts[h], kb[h] * eg[h]) for h in heads]
    qk = [bdot(qs[h], ks[h], "nt") * decay[h] for h in heads]
    v_new = [u[h] - bdot(w[h], s0s[h]) for h in heads]
    o = [bdot(qs[h] * eg[h], s0s[h]) + bdot(qk[h], v_new[h]) for h in heads]
    s1 = [s0s[h] * jnp.exp(g_last[h]) + bdot(ks[h] * jnp.exp(g_last[h] - gc[h]), v_new[h], "tn") for h in heads]
    return (tuple(o), tuple(s1)), ts


def _head_tiles(ref):
    return tuple(ref[:, h * 128:(h + 1) * 128] for h in range(GDN_HEADS))


def _gdn_fwd(q, k, v, gb, bb):
    s = q.shape[0]
    c = min(GDN_CHUNK, s)
    nc = s // c

    def body(q_ref, k_ref, v_ref, g_ref, b_ref, o_ref, st_ref, inv_ref, s_sc):
        @pl.when(pl.program_id(0) == 0)
        def _():
            s_sc[...] = jnp.zeros_like(s_sc)

        s0s = tuple(s_sc[h] for h in range(GDN_HEADS))
        for h in range(GDN_HEADS):
            st_ref[h, 0] = s0s[h]
        (os, s1s), ts = _gdn_chunk_fn(*[_head_tiles(ref) for ref in (q_ref, k_ref, v_ref, g_ref, b_ref)], s0s)
        for h in range(GDN_HEADS):
            o_ref[:, h * 128:(h + 1) * 128] = os[h]
            inv_ref[h, 0] = ts[h]
            s_sc[h] = s1s[h]

    blk = pl.BlockSpec((c, GDN_WIDTH), lambda n: (n, 0))
    return pl.pallas_call(
        body, name="gdn_fwd", grid=(nc,), in_specs=[blk] * 5,
        out_specs=[blk, pl.BlockSpec((GDN_HEADS, 1, 128, 128), lambda n: (0, n, 0, 0)),
                   pl.BlockSpec((GDN_HEADS, 1, c, c), lambda n: (0, n, 0, 0))],
        out_shape=[jax.ShapeDtypeStruct((s, GDN_WIDTH), F32), jax.ShapeDtypeStruct((GDN_HEADS, nc, 128, 128), F32),
                   jax.ShapeDtypeStruct((GDN_HEADS, nc, c, c), F32)],
        scratch_shapes=[pltpu.VMEM((GDN_HEADS, 128, 128), F32)],
        compiler_params=_cparams(("arbitrary",)))(q, k, v, gb, bb)


def _gdn_bwd(q, k, v, gb, bb, states, inverses, do):
    s = q.shape[0]
    c = min(GDN_CHUNK, s)
    nc = s // c

    def body(q_ref, k_ref, v_ref, g_ref, b_ref, st_ref, inv_ref, do_ref, dq_ref, dk_ref, dv_ref, dg_ref, db_ref, ds_sc):
        @pl.when(pl.program_id(0) == 0)
        def _():
            ds_sc[...] = jnp.zeros_like(ds_sc)

        s0s = tuple(st_ref[h, 0] for h in range(GDN_HEADS))
        ts = tuple(inv_ref[h, 0] for h in range(GDN_HEADS))
        chunk = lambda *args: _gdn_chunk_fn(*args, known_ts=ts)[0]
        _, vjp = jax.vjp(chunk, *[_head_tiles(ref) for ref in (q_ref, k_ref, v_ref, g_ref, b_ref)], s0s)
        *d_tiles, ds0s = vjp((_head_tiles(do_ref), tuple(ds_sc[h] for h in range(GDN_HEADS))))
        for h in range(GDN_HEADS):
            for ref, d in zip((dq_ref, dk_ref, dv_ref, dg_ref, db_ref), d_tiles):
                ref[:, h * 128:(h + 1) * 128] = d[h]
            ds_sc[h] = ds0s[h]

    blk = pl.BlockSpec((c, GDN_WIDTH), lambda n: (nc - 1 - n, 0))
    return pl.pallas_call(
        body, name="gdn_bwd", grid=(nc,),
        in_specs=[blk] * 5 + [pl.BlockSpec((GDN_HEADS, 1, 128, 128), lambda n: (0, nc - 1 - n, 0, 0)),
                              pl.BlockSpec((GDN_HEADS, 1, c, c), lambda n: (0, nc - 1 - n, 0, 0)), blk],
        out_specs=[blk] * 5, out_shape=[jax.ShapeDtypeStruct((s, GDN_WIDTH), F32)] * 5,
        scratch_shapes=[pltpu.VMEM((GDN_HEADS, 128, 128), F32)],
        compiler_params=_cparams(("arbitrary",)))(q, k, v, gb, bb, states, inverses, do)


@jax.custom_vjp
def gdn_core(q, k, v, gb, bb):
    return _gdn_fwd(q, k, v, gb, bb)[0]


def _gdn_core_fwd(q, k, v, gb, bb):
    o, states, inverses = _gdn_fwd(q, k, v, gb, bb)
    return o, (q, k, v, gb, bb, states, inverses)


def _gdn_core_bwd(res, do):
    return tuple(_gdn_bwd(*res, do))


gdn_core.defvjp(_gdn_core_fwd, _gdn_core_bwd)


def _cact_fn(c):
    return (_silu(c),)


def _prenorm_fn(x, w, scale_raw, scale_b, shift_raw, shift_b):
    return (_rms(x, w) * (1.0 + scale_raw + scale_b) + shift_raw + shift_b,)


def _tail_fn(x, o_mla, z_mla, o_gdn, z_gdn, o_norm_w, w_out, post_w, gate_raw, gate_b):
    y_mla = o_mla * _silu(z_mla)
    parts = [_rms(o_gdn[:, h * 128:(h + 1) * 128], o_norm_w) for h in range(GDN_HEADS)]
    y_gdn = jnp.concatenate(parts, axis=1) * _silu(z_gdn)
    y = bdot(y_mla, w_out[:MLA_WIDTH]) + bdot(y_gdn, w_out[MLA_WIDTH:])
    return (x + (gate_raw + gate_b) * _rms(y, post_w),)


def _loss_fn(y, tgt):
    err = y - tgt
    part = jnp.sum(0.5 * jnp.mean(err * err, axis=-1, keepdims=True), axis=0, keepdims=True)
    return (jnp.broadcast_to(part, (1, LANES)),)


def _front_fwd(x, norm_params, ws16):
    s = x.shape[0]
    t = min(512, s)

    def body(x_ref, *refs):
        p_refs, w_refs, o_refs = refs[:len(norm_params)], refs[len(norm_params):-len(ws16)], refs[-len(ws16):]
        (h,) = _prenorm_fn(x_ref[...], *[p[...] for p in p_refs])
        h16 = h.astype(BF16)
        for w_ref, o_ref in zip(w_refs, o_refs):
            o_ref[...] = _dot_raw(h16, w_ref[...], "nn", False)

    rows = lambda d: pl.BlockSpec((t, d), lambda i: (i, 0))
    res = pl.pallas_call(
        body, name="front", grid=(s // t,),
        in_specs=[rows(D_MODEL)] + [_resident(p.shape) for p in norm_params] + [_resident(w.shape) for w in ws16],
        out_specs=[rows(w.shape[1]) for w in ws16],
        out_shape=[jax.ShapeDtypeStruct((s, w.shape[1]), F32) for w in ws16],
        compiler_params=_cparams(("parallel",)))(x, *norm_params, *ws16)
    return tuple(res)


def _front_bwd(x, norm_params, ws16, dys, dx_skip):
    s = x.shape[0]
    t = min(512, s)
    npar, ng = len(norm_params), len(ws16)

    def body(x_ref, *refs):
        p_refs, w_refs, g_refs = refs[:npar], refs[npar:npar + ng], refs[npar + ng:npar + 2 * ng]
        skip_ref, dx_ref = refs[npar + 2 * ng], refs[npar + 2 * ng + 1]
        dp_refs, dw_refs = refs[npar + 2 * ng + 2:2 * npar + 2 * ng + 2], refs[2 * npar + 2 * ng + 2:]

        @pl.when(pl.program_id(0) == 0)
        def _():
            for ref in dp_refs + dw_refs:
                ref[...] = jnp.zeros_like(ref)

        (h,), vjp = jax.vjp(_prenorm_fn, x_ref[...], *[p[...] for p in p_refs])
        h16 = h.astype(BF16)
        dys16 = [g[...].astype(BF16) for g in g_refs]
        dh = functools.reduce(lambda a, b: a + b,
                              [_dot_raw(dy, w_ref[...], "nt", False) for dy, w_ref in zip(dys16, w_refs)])
        for dy, dw_ref in zip(dys16, dw_refs):
            dw_ref[...] += _dot_raw(h16, dy, "tn", False)
        dx, *d_params = vjp((dh,))
        dx_ref[...] = dx + skip_ref[...]
        for ref, val in zip(dp_refs, d_params):
            ref[...] += val

    rows = lambda d: pl.BlockSpec((t, d), lambda i: (i, 0))
    res = pl.pallas_call(
        body, name="front_bwd", grid=(s // t,),
        in_specs=[rows(D_MODEL)] + [_resident(p.shape) for p in norm_params] + [_resident(w.shape) for w in ws16]
        + [rows(w.shape[1]) for w in ws16] + [rows(D_MODEL)],
        out_specs=[rows(D_MODEL)] + [_resident(p.shape) for p in norm_params] + [_resident(w.shape) for w in ws16],
        out_shape=[jax.ShapeDtypeStruct(x.shape, F32)] + [jax.ShapeDtypeStruct(p.shape, F32) for p in norm_params]
        + [jax.ShapeDtypeStruct(w.shape, F32) for w in ws16],
        compiler_params=_cparams(("arbitrary",)))(x, *norm_params, *ws16, *dys, dx_skip)
    return res[0], tuple(res[1:1 + npar]), tuple(res[1 + npar:])


@jax.custom_vjp
def front(x, norm_params, ws):
    return _front_fwd(x, norm_params, tuple(w.astype(BF16) for w in ws)) + (x,)


def _front_vjp_fwd(x, norm_params, ws):
    ws16 = tuple(w.astype(BF16) for w in ws)
    return _front_fwd(x, norm_params, ws16) + (x,), (x, norm_params, ws16)


def _front_vjp_bwd(res, cts):
    return _front_bwd(*res, cts[:-1], cts[-1])


front.defvjp(_front_vjp_fwd, _front_vjp_bwd)


_OPS = dict(
    cact=make_rowwise(_cact_fn, "c_act", (D_MODEL,), tile=16),
    tail=make_rowwise(_tail_fn, "tail", (D_MODEL,), tile=512),
    loss=make_rowwise(_loss_fn, "loss", (), acc_dims=(LANES,), n_nondiff=1),
)


def _swap_halves(w):
    half = w.shape[-1] // 2
    return jnp.concatenate([w[..., half:], w[..., :half]], axis=-1)


def _w_in_groups(w):
    k_pe = w[:, 640:704]
    ab = jnp.concatenate([w[:, 2752:2760], jnp.zeros((w.shape[0], LANES - 8), w.dtype)], axis=1)
    return (w[:, :384], w[:, 384:640], jnp.concatenate([k_pe, _swap_halves(k_pe)], axis=1), w[:, 704:1216],
            w[:, 1216:2752], ab, w[:, 2760:])


def _q_up_ext(w):
    parts = []
    for h in range(MLA_HEADS):
        rope = w[:, h * 192 + 128:(h + 1) * 192]
        parts += [w[:, h * 192:h * 192 + 128], rope, _swap_halves(rope)]
    return jnp.concatenate(parts, axis=1)


def _kv_up_perm(w):
    ks = [w[:, h * 256:h * 256 + 128] for h in range(MLA_HEADS)]
    vs = [w[:, h * 256 + 128:(h + 1) * 256] for h in range(MLA_HEADS)]
    return jnp.concatenate(ks + vs, axis=1)


def _pad_lanes(v):
    return jnp.pad(v, (0, LANES - v.shape[0]))[None, :]


def _local_loss(weights, mods, x, positions, target):
    s = x.shape[0]
    half = MLA_ROPE // 2
    inv_freq = jnp.power(ROPE_THETA, -jnp.arange(half, dtype=F32) * 2.0 / MLA_ROPE)
    ang = positions.astype(F32)[:, None] * inv_freq
    cos, sin, zero = jnp.cos(ang), jnp.sin(ang), jnp.zeros((s, 2 * half), F32)
    t1 = jnp.concatenate([cos, cos, zero], axis=1)
    t2 = jnp.concatenate([-sin, sin, zero], axis=1)

    for l in range(DEPTH):
        mod = mods[l]
        b = weights["b_mod"][l][None, :]
        shift_raw, scale_raw, gate_raw = mod[:, :1024], mod[:, 1024:2048], mod[:, 2048:]
        shift_b, scale_b, gate_b = b[:, :1024], b[:, 1024:2048], b[:, 2048:]
        q_lat, kv_lat, kr, z_mla, qkv, ab, z_gdn, x = front(
            x, (weights["pre_norm_w"][l][None], scale_raw, scale_b, shift_raw, shift_b),
            _w_in_groups(weights["w_in"][l]))
        o_mla = mla_attention((q_lat, kv_lat, kr, t1, t2),
                              (weights["mla_q_norm_w"][l][None], _q_up_ext(weights["mla_q_up"][l]),
                               weights["mla_kv_norm_w"][l][None], _kv_up_perm(weights["mla_kv_up"][l])))
        cw = weights["gdn_conv_w"][l]
        params = tuple(cw[j][None] for j in range(GDN_CONV))
        params += (_pad_lanes(weights["gdn_a_log"][l]), _pad_lanes(weights["gdn_dt_bias"][l]))
        o_gdn = gdn_core(*gdn_prep(qkv, ab, params))
        (x,) = _OPS["tail"]((x, o_mla, z_mla, o_gdn, z_gdn),
                            (weights["gdn_o_norm_w"][l][None], weights["w_out"][l], weights["post_norm_w"][l][None],
                             gate_raw, gate_b))
    (acc,) = _OPS["loss"]((x, target), ())
    return acc[0, 0]


def _chip_index():
    return 2 * lax.axis_index("x") + lax.axis_index("y")


def _other_chips(x, y):
    return [(1 - x, y), (x, 1 - y), (1 - x, 1 - y)]


def _any_spec():
    return pl.BlockSpec(memory_space=pl.ANY)


def _half(ref, hc):
    n = ref.shape[0] // 2
    return ref.at[pl.ds(hc * n, n)]


def ag_weights(shards):
    n = len(shards)

    def body(*refs):
        ins, outs = refs[:n], refs[n:2 * n]
        send_sems, recv_sems = refs[2 * n:]
        x, y, c = lax.axis_index("x"), lax.axis_index("y"), lax.axis_index("c")
        sibling = (x, y, 1 - c)
        chips = _other_chips(x, y)

        def copy(t, k, src, chip_xy, hc, to):
            return pltpu.make_async_remote_copy(
                src_ref=src, dst_ref=_half(outs[t].at[2 * chip_xy[0] + chip_xy[1]], hc),
                send_sem=send_sems.at[6 * t + k], recv_sem=recv_sems.at[6 * t + k], device_id=to, device_id_type=MESH)

        first = [copy(t, k, _half(ins[t], c), (x, y), c, (*chip, c)) for k, chip in enumerate(chips) for t in range(n)]
        for cp in first:
            cp.start()
        passed = []
        for k, chip in enumerate(chips):
            for t in range(n):
                landed = _half(outs[t].at[2 * chip[0] + chip[1]], c)
                copy(t, k, landed, chip, c, (x, y, c)).wait_recv()
                passed.append(copy(t, 3 + k, landed, chip, c, sibling))
                passed[-1].start()
        for k, chip in enumerate(chips):
            for t in range(n):
                copy(t, 3 + k, _half(ins[t], c), chip, 1 - c, (x, y, c)).wait_recv()
        for cp in first + passed:
            cp.wait_send()

    return pl.pallas_call(
        body, name="ag_weights", in_specs=[_any_spec()] * n, out_specs=[_any_spec()] * n,
        out_shape=[jax.ShapeDtypeStruct((N_CHIPS,) + a.shape, a.dtype) for a in shards],
        scratch_shapes=[pltpu.SemaphoreType.DMA((6 * n,)), pltpu.SemaphoreType.DMA((6 * n,))],
        compiler_params=pltpu.CompilerParams(has_side_effects=True))(*shards)


def all_gather_rows(x):
    r, cols = x.shape
    flips = [(k >> 2 & 1, k >> 1 & 1, k & 1) for k in range(1, 8)]

    def body(x_ref, out_ref, send_sems, recv_sems):
        here = (lax.axis_index("x"), lax.axis_index("y"), lax.axis_index("c"))
        peers = [tuple(1 - p if f else p for p, f in zip(here, flip)) for flip in flips]
        slot = lambda dev: 4 * dev[0] + 2 * dev[1] + dev[2]
        out_ref[slot(here)] = x_ref[...]
        copies = [pltpu.make_async_remote_copy(
            src_ref=x_ref, dst_ref=out_ref.at[slot(here)], send_sem=send_sems.at[k], recv_sem=recv_sems.at[k],
            device_id=peer, device_id_type=MESH) for k, peer in enumerate(peers)]
        for cp in copies:
            cp.start()
        for k, peer in enumerate(peers):
            pltpu.make_async_remote_copy(
                src_ref=x_ref, dst_ref=out_ref.at[slot(peer)], send_sem=send_sems.at[k], recv_sem=recv_sems.at[k],
                device_id=peer, device_id_type=MESH).wait_recv()
        for cp in copies:
            cp.wait_send()

    return pl.pallas_call(
        body, name="all_gather_rows", in_specs=[pl.BlockSpec(memory_space=pltpu.VMEM)],
        out_specs=pl.BlockSpec(memory_space=pltpu.VMEM), out_shape=jax.ShapeDtypeStruct((8, r, cols), F32),
        scratch_shapes=[pltpu.SemaphoreType.DMA((7,)), pltpu.SemaphoreType.DMA((7,))],
        compiler_params=pltpu.CompilerParams(has_side_effects=True, vmem_limit_bytes=VMEM_LIMIT))(x)


def rs_pair(gs):
    n = len(gs)

    def body(*refs):
        ins, outs = refs[:n], refs[n:2 * n]
        send_sems, recv_sems = refs[2 * n:]
        x, y, c = lax.axis_index("x"), lax.axis_index("y"), lax.axis_index("c")
        lh = [g.shape[1] // 2 for g in gs]
        copies = [pltpu.make_async_remote_copy(
            src_ref=ins[t].at[:, pl.ds((1 - c) * lh[t], lh[t])], dst_ref=outs[t], send_sem=send_sems.at[t],
            recv_sem=recv_sems.at[t], device_id=(x, y, 1 - c), device_id_type=MESH) for t in range(n)]
        for cp in copies:
            cp.start()
        for cp in copies:
            cp.wait()

    return pl.pallas_call(
        body, name="rs_pair", in_specs=[_any_spec()] * n, out_specs=[_any_spec()] * n,
        out_shape=[jax.ShapeDtypeStruct((N_CHIPS, g.shape[1] // 2) + g.shape[2:], F32) for g in gs],
        scratch_shapes=[pltpu.SemaphoreType.DMA((n,)), pltpu.SemaphoreType.DMA((n,))],
        compiler_params=pltpu.CompilerParams(has_side_effects=True))(*gs)


def rs_cross(pairs):
    n = len(pairs)

    def body(*refs):
        ins, outs = refs[:n], refs[n:2 * n]
        send_sems, recv_sems = refs[2 * n:]
        x, y, c = lax.axis_index("x"), lax.axis_index("y"), lax.axis_index("c")
        copies = []
        for k, chip in enumerate(_other_chips(x, y)):
            for t in range(n):
                copies.append(pltpu.make_async_remote_copy(
                    src_ref=ins[t].at[2 * chip[0] + chip[1]], dst_ref=outs[t].at[k], send_sem=send_sems.at[3 * t + k],
                    recv_sem=recv_sems.at[3 * t + k], device_id=(*chip, c), device_id_type=MESH))
        for cp in copies:
            cp.start()
        for cp in copies:
            cp.wait()

    return pl.pallas_call(
        body, name="rs_cross", in_specs=[_any_spec()] * n, out_specs=[_any_spec()] * n,
        out_shape=[jax.ShapeDtypeStruct((3,) + p.shape[1:], p.dtype) for p in pairs],
        scratch_shapes=[pltpu.SemaphoreType.DMA((3 * n,)), pltpu.SemaphoreType.DMA((3 * n,))],
        compiler_params=pltpu.CompilerParams(has_side_effects=True))(*pairs)


def rs_share(blocks):
    n = len(blocks)

    def body(*refs):
        ins, outs = refs[:n], refs[n:2 * n]
        send_sems, recv_sems = refs[2 * n:]
        x, y, c = lax.axis_index("x"), lax.axis_index("y"), lax.axis_index("c")
        sends = [pltpu.make_async_remote_copy(
            src_ref=_half(ins[t], c), dst_ref=_half(outs[t], c), send_sem=send_sems.at[t], recv_sem=recv_sems.at[t],
            device_id=(x, y, 1 - c), device_id_type=MESH) for t in range(n)]
        for cp in sends:
            cp.start()
        for t in range(n):
            pltpu.make_async_remote_copy(
                src_ref=_half(ins[t], c), dst_ref=_half(outs[t], 1 - c), send_sem=send_sems.at[t],
                recv_sem=recv_sems.at[t], device_id=(x, y, 1 - c), device_id_type=MESH).wait_recv()
        for cp in sends:
            cp.wait_send()

    return pl.pallas_call(
        body, name="rs_share", in_specs=[_any_spec()] * n, out_specs=[_any_spec()] * n,
        out_shape=[jax.ShapeDtypeStruct(b.shape, F32) for b in blocks],
        input_output_aliases={t: t for t in range(n)},
        scratch_shapes=[pltpu.SemaphoreType.DMA((n,)), pltpu.SemaphoreType.DMA((n,))],
        compiler_params=pltpu.CompilerParams(has_side_effects=True))(*blocks)


def _row_tile(rows, target):
    best = None
    for t in range(8, min(rows, target) + 1, 8):
        if rows % t == 0:
            best = t
    return rows if best is None else best


TILE_BYTES = 2 * 1024 * 1024


def _flat_rows(shape):
    rows = int(np.prod(shape[1:-1]))
    cols_padded = -(-shape[-1] // LANES) * LANES
    return rows, _row_tile(rows, max(8, TILE_BYTES // (4 * cols_padded)))


def pair_add(g, from_sibling, out_dtype):
    cols = g.shape[-1]
    rph, t = _flat_rows(from_sibling.shape)
    nt = rph // t
    c_arr = lax.axis_index("c").astype(jnp.int32).reshape(1)

    def body(c_ref, a_ref, b_ref, o_ref):
        o_ref[...] = (a_ref[...] + b_ref[...]).astype(o_ref.dtype)

    out = pl.pallas_call(
        body, name="pair_add",
        grid_spec=pltpu.PrefetchScalarGridSpec(
            num_scalar_prefetch=1, grid=(N_CHIPS, nt),
            in_specs=[pl.BlockSpec((t, cols), lambda j, i, c_ref: (j * 2 * nt + c_ref[0] * nt + i, 0)),
                      pl.BlockSpec((t, cols), lambda j, i, c_ref: (j * nt + i, 0))],
            out_specs=pl.BlockSpec((t, cols), lambda j, i, c_ref: (j * nt + i, 0))),
        out_shape=jax.ShapeDtypeStruct((N_CHIPS * rph, cols), out_dtype),
        compiler_params=_cparams(("parallel", "parallel")))(c_arr, g.reshape(-1, cols), from_sibling.reshape(-1, cols))
    return out.reshape(from_sibling.shape)


def chip_add(pairs, received):
    cols = pairs.shape[-1]
    rph, t = _flat_rows(pairs.shape)
    nt = rph // t
    j_arr = _chip_index().astype(jnp.int32).reshape(1)
    c_arr = lax.axis_index("c").astype(jnp.int32).reshape(1)
    r2 = received.reshape(-1, cols)

    def body(j_ref, c_ref, a_ref, r0_ref, r1_ref, r2_ref, o_ref):
        a, r0, r1, r2 = [ref[...].astype(F32) for ref in (a_ref, r0_ref, r1_ref, r2_ref)]
        o_ref[...] = (a + r0) + (r1 + r2)

    out = pl.pallas_call(
        body, name="chip_add",
        grid_spec=pltpu.PrefetchScalarGridSpec(
            num_scalar_prefetch=2, grid=(nt,),
            in_specs=[pl.BlockSpec((t, cols), lambda i, j_ref, c_ref: (j_ref[0] * nt + i, 0)),
                      pl.BlockSpec((t, cols), lambda i, j_ref, c_ref: (i, 0)),
                      pl.BlockSpec((t, cols), lambda i, j_ref, c_ref: (nt + i, 0)),
                      pl.BlockSpec((t, cols), lambda i, j_ref, c_ref: (2 * nt + i, 0))],
            out_specs=pl.BlockSpec((t, cols), lambda i, j_ref, c_ref: (c_ref[0] * nt + i, 0))),
        out_shape=jax.ShapeDtypeStruct((2 * rph, cols), F32),
        compiler_params=_cparams(("parallel",)))(j_arr, c_arr, pairs.reshape(-1, cols), r2, r2, r2)
    return out.reshape((2 * pairs.shape[1],) + pairs.shape[2:])


def reduce_grads(gs, cross_dtypes):
    pairs = [pair_add(g, r, dt) for g, r, dt in zip(gs, rs_pair(gs), cross_dtypes)]
    return rs_share([chip_add(p, r) for p, r in zip(pairs, rs_cross(pairs))])


def adamw(w, g, m, v):
    shape = w.shape
    cols = shape[-1]
    rows = int(np.prod(shape[:-1]))
    flat = lambda a: a.reshape(rows, cols)
    t = _row_tile(rows, 256)

    def body(w_ref, g_ref, m_ref, v_ref, d_ref, mo_ref, vo_ref):
        gv = g_ref[...]
        m_new = ADAM_B1 * m_ref[...] + (1.0 - ADAM_B1) * gv
        v_new = ADAM_B2 * v_ref[...] + (1.0 - ADAM_B2) * (gv * gv)
        m_hat = m_new / (1.0 - ADAM_B1 ** ADAM_STEP)
        v_hat = v_new / (1.0 - ADAM_B2 ** ADAM_STEP)
        d_ref[...] = -ADAM_LR * (m_hat / (jnp.sqrt(v_hat) + ADAM_EPS) + ADAM_WD * w_ref[...])
        mo_ref[...] = m_new
        vo_ref[...] = v_new

    spec = pl.BlockSpec((t, cols), lambda i: (i, 0))
    outs = pl.pallas_call(
        body, name="adamw", grid=(rows // t,), in_specs=[spec] * 4, out_specs=[spec] * 3,
        out_shape=[jax.ShapeDtypeStruct((rows, cols), F32)] * 3,
        compiler_params=_cparams(("parallel",)))(flat(w), flat(g), flat(m), flat(v))
    return tuple(o.reshape(shape) for o in outs)


SHARDED = (("w_in", 2), ("mla_q_up", 2), ("mla_kv_up", 2), ("gdn_conv_w", 2), ("w_out", 1))
REPLICATED = ("b_mod", "pre_norm_w", "post_norm_w", "mla_q_norm_w", "mla_kv_norm_w", "gdn_a_log", "gdn_dt_bias",
              "gdn_o_norm_w")
WEIGHT_ORDER = ("w_mod", "b_mod", "pre_norm_w", "post_norm_w", "w_in", "mla_q_norm_w", "mla_q_up", "mla_kv_norm_w",
                "mla_kv_up", "gdn_conv_w", "gdn_a_log", "gdn_dt_bias", "gdn_o_norm_w", "w_out")
EXACT_F32 = ("gdn_conv_w",)
SMALL_ROWS = 48


def _gather_weights(shards):
    names = [name for name, _ in SHARDED]
    own = [shards[n] if n in EXACT_F32 else shards[n].astype(BF16) for n in names]
    gathered = ag_weights(own)
    full = {}
    for (name, axis), blk, mine in zip(SHARDED, gathered, own):
        shp = shards[name].shape
        blk = lax.dynamic_update_index_in_dim(blk, mine, _chip_index(), 0)
        blk = jnp.moveaxis(blk.astype(F32), 0, axis)
        full[name] = blk.reshape(shp[:axis] + (N_CHIPS * shp[axis],) + shp[axis + 1:])
    return full


def _split_grads(grads):
    pieces = []
    for name, axis in SHARDED:
        g = grads[name]
        shp = g.shape
        g = g.reshape(shp[:axis] + (N_CHIPS, shp[axis] // N_CHIPS) + shp[axis + 1:])
        pieces.append(jnp.moveaxis(g, axis, 0))
    small = jnp.concatenate([grads[name] for name in REPLICATED], axis=1)
    small = jnp.pad(small, ((0, 0), (0, SMALL_ROWS * LANES - small.shape[1]))).reshape(DEPTH, SMALL_ROWS, LANES)
    pieces.append(jnp.broadcast_to(small[None], (N_CHIPS,) + small.shape))
    return pieces


def _unsplit_small(small, rep_shapes):
    flat = small.reshape(DEPTH, SMALL_ROWS * LANES)
    out, off = {}, 0
    for name in REPLICATED:
        size = rep_shapes[name][1]
        out[name] = flat[:, off:off + size]
        off += size
    return out


MOD_ROWS = 16


def _device_slot():
    return 4 * lax.axis_index("x") + 2 * lax.axis_index("y") + lax.axis_index("c")


def _adaln_projection(c, w_mod_shard):
    (c_act,) = _OPS["cact"]((jnp.pad(c, ((0, 7), (0, 0))),), ())
    c_acts = jnp.pad(all_gather_rows(c_act)[:, 0, :], ((0, MOD_ROWS - 8), (0, 0)))
    part = jnp.concatenate([_mm(c_acts, w_mod_shard[l].astype(BF16), "nn", "mod_proj") for l in range(DEPTH)], axis=0)
    parts = all_gather_rows(part)[::2].reshape(N_CHIPS, DEPTH, MOD_ROWS, -1)
    mods = jnp.moveaxis(parts, 0, 2).reshape(DEPTH, MOD_ROWS, -1)
    return c_acts, lax.dynamic_slice_in_dim(mods, _device_slot(), 1, axis=1)


def _adaln_weight_grad(c_acts, d_mods):
    cols = d_mods.shape[-1] // N_CHIPS
    rows = jnp.pad(d_mods[:, 0, :], ((0, 8 - DEPTH), (0, 0)))
    all_rows = all_gather_rows(rows)[:, :DEPTH, :]
    mine = lax.dynamic_slice_in_dim(all_rows, _chip_index() * cols, cols, axis=2)
    mine = jnp.pad(mine, ((0, MOD_ROWS - 8), (0, 0), (0, 0)))
    return jnp.stack([_mm(c_acts, mine[:, l, :], "tn", "mod_dw") for l in range(DEPTH)])


def kernel(x, c, positions, w_mod, b_mod, pre_norm_w, post_norm_w, w_in, mla_q_norm_w, mla_q_up, mla_kv_norm_w, mla_kv_up, gdn_conv_w, gdn_a_log, gdn_dt_bias, gdn_o_norm_w, w_out, loss_target, m_w_mod, m_b_mod, m_pre_norm_w, m_post_norm_w, m_w_in, m_mla_q_norm_w, m_mla_q_up, m_mla_kv_norm_w, m_mla_kv_up, m_gdn_conv_w, m_gdn_a_log, m_gdn_dt_bias, m_gdn_o_norm_w, m_w_out, v_w_mod, v_b_mod, v_pre_norm_w, v_post_norm_w, v_w_in, v_mla_q_norm_w, v_mla_q_up, v_mla_kv_norm_w, v_mla_kv_up, v_gdn_conv_w, v_gdn_a_log, v_gdn_dt_bias, v_gdn_o_norm_w, v_w_out):
    given = dict(w_mod=w_mod, b_mod=b_mod, pre_norm_w=pre_norm_w, post_norm_w=post_norm_w, w_in=w_in,
                 mla_q_norm_w=mla_q_norm_w, mla_q_up=mla_q_up, mla_kv_norm_w=mla_kv_norm_w, mla_kv_up=mla_kv_up,
                 gdn_conv_w=gdn_conv_w, gdn_a_log=gdn_a_log, gdn_dt_bias=gdn_dt_bias, gdn_o_norm_w=gdn_o_norm_w,
                 w_out=w_out)
    moments_m = dict(w_mod=m_w_mod, b_mod=m_b_mod, pre_norm_w=m_pre_norm_w, post_norm_w=m_post_norm_w, w_in=m_w_in,
                     mla_q_norm_w=m_mla_q_norm_w, mla_q_up=m_mla_q_up, mla_kv_norm_w=m_mla_kv_norm_w,
                     mla_kv_up=m_mla_kv_up, gdn_conv_w=m_gdn_conv_w, gdn_a_log=m_gdn_a_log,
                     gdn_dt_bias=m_gdn_dt_bias, gdn_o_norm_w=m_gdn_o_norm_w, w_out=m_w_out)
    moments_v = dict(w_mod=v_w_mod, b_mod=v_b_mod, pre_norm_w=v_pre_norm_w, post_norm_w=v_post_norm_w, w_in=v_w_in,
                     mla_q_norm_w=v_mla_q_norm_w, mla_q_up=v_mla_q_up, mla_kv_norm_w=v_mla_kv_norm_w,
                     mla_kv_up=v_mla_kv_up, gdn_conv_w=v_gdn_conv_w, gdn_a_log=v_gdn_a_log,
                     gdn_dt_bias=v_gdn_dt_bias, gdn_o_norm_w=v_gdn_o_norm_w, w_out=v_w_out)

    full = _gather_weights({name: given[name] for name, _ in SHARDED})
    for name in REPLICATED:
        full[name] = given[name]
    c_acts, mods = _adaln_projection(c, w_mod)
    loss_local, (grads, d_mods, grad_x) = jax.value_and_grad(_local_loss, argnums=(0, 1, 2))(
        full, mods, x[0], positions[0], loss_target[0])
    loss = lax.psum(loss_local, AXES)

    cross_dtypes = [F32 if name in EXACT_F32 else BF16 for name, _ in SHARDED] + [F32]
    reduced = reduce_grads(_split_grads(grads), cross_dtypes)
    grad_w = {name: g for (name, _), g in zip(SHARDED, reduced)}
    grad_w.update(_unsplit_small(reduced[-1], {name: given[name].shape for name in REPLICATED}))
    grad_w["w_mod"] = _adaln_weight_grad(c_acts, d_mods)
    delta, new_m, new_v = {}, {}, {}
    for name in WEIGHT_ORDER:
        delta[name], new_m[name], new_v[name] = adamw(given[name], grad_w[name], moments_m[name], moments_v[name])
    return (loss, grad_x[None], *[grad_w[n] for n in WEIGHT_ORDER], *[delta[n] for n in WEIGHT_ORDER],
            *[new_m[n] for n in WEIGHT_ORDER], *[new_v[n] for n in WEIGHT_ORDER])
```

```python
import functools
import math

import numpy as np
import jax
import jax.numpy as jnp
from jax import lax
from jax.experimental import pallas as pl
from jax.experimental.pallas import tpu as pltpu

F32 = jnp.float32
BF16 = jnp.bfloat16
MESH = pl.DeviceIdType.MESH
AXES = ("x", "y", "c")

D_MODEL = 1024
DEPTH = 4
MLA_HEADS = 4
MLA_NOPE = 128
MLA_ROPE = 64
MLA_V = 128
MLA_Q_RANK = 384
MLA_KV_RANK = 256
MLA_WIDTH = 512
GDN_HEADS = 4
GDN_DK = 128
GDN_WIDTH = 512
GDN_QKV = 1536
GDN_CONV = 4
IN_COLS = 3272
ROPE_THETA = 10000.0
NORM_EPS = 1e-6
ADAM_LR, ADAM_B1, ADAM_B2, ADAM_EPS, ADAM_WD, ADAM_STEP = 0.001, 0.9, 0.999, 1e-08, 0.01, 10

LANES = 128
N_CHIPS = 4
GDN_CHUNK = 128
VMEM_LIMIT = 56 * 1024 * 1024


def _cparams(sem=None):
    if sem is None:
        return pltpu.CompilerParams(vmem_limit_bytes=VMEM_LIMIT)
    return pltpu.CompilerParams(dimension_semantics=sem, vmem_limit_bytes=VMEM_LIMIT)


def _pick(dim, target):
    if dim <= target:
        return dim
    best = None
    for t in range(LANES, target + 1, LANES):
        if dim % t == 0:
            best = t
    assert best is not None, (dim, target)
    return best


def _resident(shape):
    return pl.BlockSpec(shape, lambda i: (0,) * len(shape), pipeline_mode=pl.Buffered(1))


_DN = {"nn": (((1,), (0,)), ((), ())), "nt": (((1,), (1,)), ((), ())), "tn": (((0,), (0,)), ((), ()))}


def _dot_raw(a, b, mode, exact):
    if exact:
        return lax.dot_general(a, b, _DN[mode], precision=lax.Precision.HIGHEST, preferred_element_type=F32)
    return lax.dot_general(a.astype(BF16), b.astype(BF16), _DN[mode], preferred_element_type=F32)


def _dot_split(a, b):
    a_hi, b_hi = a.astype(BF16), b.astype(BF16)
    a_lo, b_lo = (a - a_hi.astype(F32)).astype(BF16), (b - b_hi.astype(F32)).astype(BF16)
    dot = lambda u, w: lax.dot_general(u, w, _DN["nn"], preferred_element_type=F32)
    return dot(a_hi, b_hi) + (dot(a_hi, b_lo) + dot(a_lo, b_hi))


@functools.partial(jax.custom_vjp, nondiff_argnums=(2, 3))
def bdot(a, b, mode="nn", exact=False):
    return _dot_raw(a, b, mode, exact)


def _bdot_fwd(a, b, mode, exact):
    return _dot_raw(a, b, mode, exact), (a, b)


def _bdot_bwd(mode, exact, res, g):
    a, b = res
    if mode == "nn":
        return bdot(g, b, "nt", exact), bdot(a, g, "tn", exact)
    if mode == "nt":
        return bdot(g, b, "nn", exact), bdot(g, a, "tn", exact)
    return bdot(b, g, "nt", exact), bdot(a, g, "nn", exact)


bdot.defvjp(_bdot_fwd, _bdot_bwd)


@jax.custom_vjp
def roll_half(x):
    return pltpu.roll(x, 64, 1)


roll_half.defvjp(lambda x: (pltpu.roll(x, 64, 1), None), lambda _, g: (pltpu.roll(g, 64, 1),))


def _sigmoid(x):
    return 1.0 / (1.0 + jnp.exp(-x))


def _silu(x):
    return x * _sigmoid(x)


def _softplus(x):
    return jnp.maximum(x, 0.0) + jnp.log(1.0 + jnp.exp(-jnp.abs(x)))


def _rms(x, w):
    return x * lax.rsqrt(jnp.mean(x * x, axis=-1, keepdims=True) + NORM_EPS) * w


def _rw_fwd(fn, name, rows, params, out_dims, out_dtypes, acc_dims, tile):
    s = rows[0].shape[0]
    t = min(tile, s)
    n = s // t
    nr, npar, no, na = len(rows), len(params), len(out_dims), len(acc_dims)

    def body(*refs):
        r, p = refs[:nr], refs[nr:nr + npar]
        o, a = refs[nr + npar:nr + npar + no], refs[nr + npar + no:]
        outs = fn(*[x[...] for x in r], *[x[...] for x in p])
        for ref, val in zip(o, outs[:no]):
            ref[...] = val.astype(ref.dtype)
        if na:
            @pl.when(pl.program_id(0) == 0)
            def _():
                for ref in a:
                    ref[...] = jnp.zeros_like(ref)
            for ref, val in zip(a, outs[no:]):
                ref[...] += val

    in_specs = [pl.BlockSpec((t, x.shape[1]), lambda i: (i, 0)) for x in rows]
    in_specs += [_resident(x.shape) for x in params]
    out_specs = [pl.BlockSpec((t, d), lambda i: (i, 0)) for d in out_dims]
    out_specs += [_resident((1, d)) for d in acc_dims]
    out_shape = [jax.ShapeDtypeStruct((s, d), dt) for d, dt in zip(out_dims, out_dtypes)]
    out_shape += [jax.ShapeDtypeStruct((1, d), F32) for d in acc_dims]
    res = pl.pallas_call(body, name=name, grid=(n,), in_specs=in_specs, out_specs=out_specs, out_shape=out_shape,
                         compiler_params=_cparams(("arbitrary",)))(*rows, *params)
    return tuple(res)


def _rw_bwd(fn, name, rows, params, row_cts, acc_cts, n_diff, tile):
    s = rows[0].shape[0]
    t = min(tile, s)
    n = s // t
    nr, npar, no, na = len(rows), len(params), len(row_cts), len(acc_cts)

    def body(*refs):
        r, p = refs[:nr], refs[nr:nr + npar]
        g, ga = refs[nr + npar:nr + npar + no], refs[nr + npar + no:nr + npar + no + na]
        dr, dp = refs[nr + npar + no + na:nr + npar + no + na + n_diff], refs[nr + npar + no + na + n_diff:]
        _, vjp = jax.vjp(fn, *[x[...] for x in r], *[x[...] for x in p])
        cts = vjp(tuple([x[...] for x in g] + [x[...] for x in ga]))
        for ref, val in zip(dr, cts[:n_diff]):
            ref[...] = val
        if npar:
            @pl.when(pl.program_id(0) == 0)
            def _():
                for ref in dp:
                    ref[...] = jnp.zeros_like(ref)
            for ref, val in zip(dp, cts[nr:]):
                ref[...] += val

    in_specs = [pl.BlockSpec((t, x.shape[1]), lambda i: (i, 0)) for x in rows]
    in_specs += [_resident(x.shape) for x in params]
    in_specs += [pl.BlockSpec((t, x.shape[1]), lambda i: (i, 0)) for x in row_cts]
    in_specs += [_resident(x.shape) for x in acc_cts]
    out_specs = [pl.BlockSpec((t, x.shape[1]), lambda i: (i, 0)) for x in rows[:n_diff]]
    out_specs += [_resident(x.shape) for x in params]
    out_shape = [jax.ShapeDtypeStruct(x.shape, F32) for x in rows[:n_diff]]
    out_shape += [jax.ShapeDtypeStruct(x.shape, F32) for x in params]
    res = pl.pallas_call(body, name=name, grid=(n,), in_specs=in_specs, out_specs=out_specs, out_shape=out_shape,
                         compiler_params=_cparams(("arbitrary",)))(*rows, *params, *row_cts, *acc_cts)
    return tuple(res[:n_diff]), tuple(res[n_diff:])


def make_rowwise(fn, name, out_dims, acc_dims=(), n_nondiff=0, tile=256):
    out_dtypes = (F32,) * len(out_dims)

    @jax.custom_vjp
    def op(rows, params):
        return _rw_fwd(fn, name, rows, params, out_dims, out_dtypes, acc_dims, tile)

    def fwd(rows, params):
        return op(rows, params), (rows, params)

    def bwd(res, cts):
        rows, params = res
        n_diff = len(rows) - n_nondiff
        d_rows, d_params = _rw_bwd(fn, name + "_bwd", rows, params, cts[:len(out_dims)], cts[len(out_dims):],
                                   n_diff, tile)
        d_rows = d_rows + tuple(jnp.zeros_like(x) for x in rows[n_diff:])
        return d_rows, d_params

    op.defvjp(fwd, bwd)
    return op


def _mm(a, b, mode, name):
    if mode == "nn":
        (m, k), (_, n) = a.shape, b.shape
    elif mode == "nt":
        (m, k), (n, _) = a.shape, b.shape
    else:
        (k, m), (_, n) = a.shape, b.shape
    tm = _pick(m, 512)
    tn = _pick(n, 1152)
    tk = _pick(k, 1152) if mode != "tn" else _pick(k, 512)
    nk = k // tk

    def body(a_ref, b_ref, o_ref, acc_ref):
        kk = pl.program_id(2)

        @pl.when(kk == 0)
        def _():
            acc_ref[...] = jnp.zeros_like(acc_ref)

        acc_ref[...] += _dot_raw(a_ref[...], b_ref[...], mode, False)

        @pl.when(kk == nk - 1)
        def _():
            o_ref[...] = acc_ref[...]

    if mode == "nn":
        a_spec = pl.BlockSpec((tm, tk), lambda i, j, kk: (i, kk))
        b_spec = pl.BlockSpec((tk, tn), lambda i, j, kk: (kk, j))
    elif mode == "nt":
        a_spec = pl.BlockSpec((tm, tk), lambda i, j, kk: (i, kk))
        b_spec = pl.BlockSpec((tn, tk), lambda i, j, kk: (j, kk))
    else:
        a_spec = pl.BlockSpec((tk, tm), lambda i, j, kk: (kk, i))
        b_spec = pl.BlockSpec((tk, tn), lambda i, j, kk: (kk, j))
    return pl.pallas_call(
        body, name=name, grid=(m // tm, n // tn, nk), in_specs=[a_spec, b_spec],
        out_specs=pl.BlockSpec((tm, tn), lambda i, j, kk: (i, j)),
        out_shape=jax.ShapeDtypeStruct((m, n), F32), scratch_shapes=[pltpu.VMEM((tm, tn), F32)],
        compiler_params=_cparams(("parallel", "parallel", "arbitrary")))(a, b)


def _mla_prep_fn(qraw, kvraw, kr, t1, t2):
    kr_rot = kr * t1 + roll_half(kr) * t2
    qs, ks = [], []
    for h in range(MLA_HEADS):
        q_r = qraw[:, h * 256 + 128:(h + 1) * 256]
        qs += [qraw[:, h * 256:h * 256 + 128], q_r * t1 + roll_half(q_r) * t2]
        ks += [kvraw[:, h * 128:(h + 1) * 128], kr_rot]
    return jnp.concatenate(qs, axis=1), jnp.concatenate(ks, axis=1), kvraw[:, 512:]


def _flash_tile(s):
    return 512 if s >= 2048 else 128


FLASH_SCALE = (MLA_NOPE + MLA_ROPE) ** -0.5
LOG2_E = 1.4426950408889634
FLASH_EXP2 = FLASH_SCALE * LOG2_E
STAT_ROWS = 8
FLASH_PAIR = 2
FLASH_STRIP = 32


def _as_rows(col_b):
    ones = jnp.full((STAT_ROWS, LANES), 1.0 / LANES, F32)
    return _dot_raw(ones, col_b, "nt", True)


def _flash_fwd(qf, kf, vf):
    s = qf.shape[0]
    t = _flash_tile(s)
    nb = s // t
    pair = range(FLASH_PAIR)

    ck = min(FLASH_STRIP, t)

    def body(q_ref, k_ref, v_ref, o_ref, lse_ref, m_sc, l_sc, acc_sc, st_sc, pt_sc):
        i = pl.program_id(1)
        m_sc[...] = jnp.full_like(m_sc, -1e30)
        l_sc[...] = jnp.zeros_like(l_sc)
        acc_sc[...] = jnp.zeros_like(acc_sc)
        qs = [q_ref[:, hh * 256:(hh + 1) * 256] for hh in pair]

        def step(j, on_diagonal):
            rows = pl.ds(pl.multiple_of(j * t, t), t)
            for hh in pair:
                st = _dot_raw(k_ref[rows, hh * 256:(hh + 1) * 256], qs[hh], "nt", False)
                if on_diagonal:
                    keep = lax.broadcasted_iota(jnp.int32, (t, t), 0) <= lax.broadcasted_iota(jnp.int32, (t, t), 1)
                    st = jnp.where(keep, st, -1e30)
                st_sc[hh] = st
            m_olds = [m_sc[hh] for hh in pair]
            m_news = [jnp.maximum(m_olds[hh], jnp.max(st_sc[hh], axis=0, keepdims=True)) for hh in pair]
            alphas = [jnp.exp2((m_olds[hh] - m_news[hh]) * FLASH_EXP2) for hh in pair]
            shifts = [m_news[hh] * FLASH_EXP2 for hh in pair]
            sums = [jnp.zeros((8, t), F32) for _ in pair]
            for hh in pair:
                for r in range(0, t, ck):
                    p = jnp.exp2(st_sc[hh, r:r + ck, :] * FLASH_EXP2 - shifts[hh])
                    pt_sc[hh, r:r + ck, :] = p.astype(BF16)
                    sums[hh] = sums[hh] + functools.reduce(lambda a, b: a + b, [p[u:u + 8] for u in range(0, ck, 8)])
            pvs = [_dot_raw(v_ref[rows, hh * 128:(hh + 1) * 128], pt_sc[hh], "tn", False) for hh in pair]
            for hh in pair:
                l_sc[hh] = alphas[hh] * l_sc[hh] + jnp.sum(sums[hh], axis=0, keepdims=True)
                acc_sc[hh] = alphas[hh] * acc_sc[hh] + pvs[hh]
                m_sc[hh] = m_news[hh]

        def two_steps(p, carry):
            step(2 * p, False)
            step(2 * p + 1, False)
            return carry

        lax.fori_loop(0, i // 2, two_steps, 0)

        @pl.when(i % 2 == 1)
        def _():
            step(i - 1, False)

        step(i, True)
        for hh in pair:
            o_ref[:, hh * 128:(hh + 1) * 128] = (acc_sc[hh] / l_sc[hh]).T
            lse2 = m_sc[hh] * FLASH_EXP2 + jnp.log(l_sc[hh]) * LOG2_E
            lse_ref[hh] = jnp.broadcast_to(lse2, (STAT_ROWS, t))

    p = FLASH_PAIR
    return pl.pallas_call(
        body, name="flash_fwd", grid=(MLA_HEADS // p, nb),
        in_specs=[pl.BlockSpec((t, p * 256), lambda h, i: (i, h)), pl.BlockSpec((s, p * 256), lambda h, i: (0, h)),
                  pl.BlockSpec((s, p * 128), lambda h, i: (0, h))],
        out_specs=[pl.BlockSpec((t, p * 128), lambda h, i: (i, h)),
                   pl.BlockSpec((p, STAT_ROWS, t), lambda h, i: (h, 0, i))],
        out_shape=[jax.ShapeDtypeStruct((s, MLA_WIDTH), F32),
                   jax.ShapeDtypeStruct((MLA_HEADS, STAT_ROWS, s), F32)],
        scratch_shapes=[pltpu.VMEM((p, 1, t), F32), pltpu.VMEM((p, 1, t), F32), pltpu.VMEM((p, 128, t), F32),
                        pltpu.VMEM((p, t, t), F32), pltpu.VMEM((p, t, t), BF16)],
        compiler_params=_cparams(("parallel", "arbitrary")))(qf, kf, vf)


def _flash_bwd(qf, kf, vf, lse, o, do):
    s = qf.shape[0]
    t = _flash_tile(s)
    nb = s // t

    def body(q_ref, k_ref, v_ref, lse_ref, o_ref, do_ref, dq_ref, dk_ref, dv_ref, dl_sc, do16_sc):
        j = pl.program_id(1)

        @pl.when(j == 0)
        def _():
            dq_ref[...] = jnp.zeros_like(dq_ref)

            def stage(b, carry):
                rows = pl.ds(pl.multiple_of(b * t, t), t)
                do_t = do_ref[rows, :]
                delta = jnp.sum(do_t * o_ref[rows, :], axis=1, keepdims=True)
                dl_sc[:, rows] = _as_rows(jnp.broadcast_to(delta, (t, LANES)))
                do16_sc[rows, :] = do_t.astype(BF16)
                return carry

            lax.fori_loop(0, nb, stage, 0)

        dk_ref[...] = jnp.zeros_like(dk_ref)
        dv_ref[...] = jnp.zeros_like(dv_ref)
        k = k_ref[...]
        v = v_ref[...]

        def step(i, on_diagonal):
            rows = pl.ds(pl.multiple_of(i * t, t), t)
            q = q_ref[rows, :]
            do_t = do16_sc[rows, :]
            st = _dot_raw(k, q, "nt", False) * FLASH_EXP2 - lse_ref[0, 0:1, rows]
            if on_diagonal:
                keep = lax.broadcasted_iota(jnp.int32, (t, t), 0) <= lax.broadcasted_iota(jnp.int32, (t, t), 1)
                st = jnp.where(keep, st, -1e30)
            pt = jnp.exp2(st)
            dst = pt * (_dot_raw(v, do_t, "nt", False) - dl_sc[0:1, rows])
            dv_ref[...] += _dot_raw(pt, do_t, "nn", False)
            dk_ref[...] += _dot_raw(dst, q, "nn", False)
            dq_ref[rows, :] += _dot_raw(dst, k, "tn", False)

        def two_steps(p, carry):
            step(j + 1 + 2 * p, False)
            step(j + 2 + 2 * p, False)
            return carry

        step(j, True)
        below = nb - 1 - j
        lax.fori_loop(0, below // 2, two_steps, 0)

        @pl.when(below % 2 == 1)
        def _():
            step(nb - 1, False)

        dk_ref[...] *= FLASH_SCALE

        @pl.when(j == nb - 1)
        def _():
            dq_ref[...] *= FLASH_SCALE

    per_head = lambda d: pl.BlockSpec((s, d), lambda h, j: (0, h), pipeline_mode=pl.Buffered(1))
    return pl.pallas_call(
        body, name="flash_bwd", grid=(MLA_HEADS, nb),
        in_specs=[per_head(256), pl.BlockSpec((t, 256), lambda h, j: (j, h)),
                  pl.BlockSpec((t, 128), lambda h, j: (j, h)),
                  pl.BlockSpec((1, STAT_ROWS, s), lambda h, j: (h, 0, 0)), per_head(128), per_head(128)],
        out_specs=[pl.BlockSpec((s, 256), lambda h, j: (0, h)), pl.BlockSpec((t, 256), lambda h, j: (j, h)),
                   pl.BlockSpec((t, 128), lambda h, j: (j, h))],
        out_shape=[jax.ShapeDtypeStruct((s, 1024), F32), jax.ShapeDtypeStruct((s, 1024), F32),
                   jax.ShapeDtypeStruct((s, MLA_WIDTH), F32)],
        scratch_shapes=[pltpu.VMEM((STAT_ROWS, s), F32), pltpu.VMEM((s, 128), BF16)],
        compiler_params=_cparams(("parallel", "arbitrary")))(qf, kf, vf, lse, o, do)


MLA_FRONT_TILE = 512


def _mla_front_fn(q_lat, kv_lat, kr, t1, t2, q_norm_w, q_up, kv_norm_w, kv_up):
    qraw = bdot(_rms(q_lat, q_norm_w), q_up)
    kvraw = bdot(_rms(kv_lat, kv_norm_w), kv_up)
    return _mla_prep_fn(qraw, kvraw, kr, t1, t2)


def _mla_front(rows, params):
    return _rw_fwd(_mla_front_fn, "mla_front", rows, params, (1024, 1024, 512), (BF16, BF16, BF16), (),
                   MLA_FRONT_TILE)


@jax.custom_vjp
def mla_attention(rows, params):
    return _flash_fwd(*_mla_front(rows, params))[0]


def _mla_attention_fwd(rows, params):
    qf, kf, vf = _mla_front(rows, params)
    o, lse = _flash_fwd(qf, kf, vf)
    return o, (rows, params, qf, kf, vf, o, lse)


def _mla_attention_bwd(res, do):
    rows, params, qf, kf, vf, o, lse = res
    d_rows, d_params = _rw_bwd(_mla_front_fn, "mla_front_bwd", rows, params, _flash_bwd(qf, kf, vf, lse, o, do), (),
                               3, MLA_FRONT_TILE)
    return d_rows + (jnp.zeros_like(rows[3]), jnp.zeros_like(rows[4])), d_params


mla_attention.defvjp(_mla_attention_fwd, _mla_attention_bwd)


def _lane_pick(x, lane):
    ids = lax.broadcasted_iota(jnp.int32, x.shape, 1)
    col = jnp.sum(jnp.where(ids == lane, x, 0.0), axis=1, keepdims=True)
    return jnp.broadcast_to(col, x.shape)


GDN_HALO = 8


@functools.partial(jax.custom_vjp, nondiff_argnums=(1,))
def _roll_rows(x, d):
    return pltpu.roll(x, d, 0)


_roll_rows.defvjp(lambda x, d: (pltpu.roll(x, d, 0), None), lambda d, _, g: (pltpu.roll(g, g.shape[0] - d, 0),))


def _gdn_prep_fn(prev, cur, ab, w0, w1, w2, w3, a_log, dt_bias):
    xcat = jnp.concatenate([prev, cur], axis=0)
    x0, x1, x2 = [_roll_rows(xcat, GDN_CONV - 1 - j)[GDN_HALO:] for j in range(GDN_CONV - 1)]
    qkv = _silu(x0 * w0 + x1 * w1 + x2 * w2 + cur * w3)
    g_all = -jnp.exp(a_log) * _softplus(ab + dt_bias)
    beta_all = _sigmoid(ab)
    qs, ks, gs, bs = [], [], [], []
    for h in range(GDN_HEADS):
        q = qkv[:, h * 128:(h + 1) * 128]
        k = qkv[:, 512 + h * 128:512 + (h + 1) * 128]
        qs.append(q * lax.rsqrt(jnp.sum(q * q, axis=-1, keepdims=True) + NORM_EPS) * (GDN_DK ** -0.5))
        ks.append(k * lax.rsqrt(jnp.sum(k * k, axis=-1, keepdims=True) + NORM_EPS))
        gs.append(_lane_pick(g_all, h))
        bs.append(_lane_pick(beta_all, GDN_HEADS + h))
    cat = lambda xs: jnp.concatenate(xs, axis=1)
    return cat(qs), cat(ks), qkv[:, 1024:], cat(gs), cat(bs)


GDN_PREP_TILE = 256


def _gdn_prep_specs(s, params, reverse=False):
    t = min(GDN_PREP_TILE, s)
    n = s // t
    blk = (lambda i: n - 1 - i) if reverse else (lambda i: i)
    prev = pl.BlockSpec((GDN_HALO, GDN_QKV), lambda i: (jnp.maximum(blk(i) * (t // GDN_HALO) - 1, 0), 0))
    rows = lambda d: pl.BlockSpec((t, d), lambda i: (blk(i), 0))
    return t, rows, [prev, rows(GDN_QKV), rows(LANES)] + [pl.BlockSpec(p.shape, lambda i: (0, 0)) for p in params]


def _gdn_prep_masked(has_rows_before):
    return lambda prev, *rest: _gdn_prep_fn(prev * has_rows_before, *rest)


def _gdn_prep_fwd(qkv, ab, params):
    s = qkv.shape[0]
    t, rows, in_specs = _gdn_prep_specs(s, params)

    def body(prev_ref, cur_ref, ab_ref, *refs):
        p_refs, o_refs = refs[:len(params)], refs[len(params):]
        has_rows_before = (pl.program_id(0) > 0).astype(F32)
        outs = _gdn_prep_masked(has_rows_before)(prev_ref[...], cur_ref[...], ab_ref[...], *[p[...] for p in p_refs])
        for ref, val in zip(o_refs, outs):
            ref[...] = val

    return pl.pallas_call(
        body, name="gdn_prep", grid=(s // t,), in_specs=in_specs,
        out_specs=[rows(GDN_WIDTH)] * 5, out_shape=[jax.ShapeDtypeStruct((s, GDN_WIDTH), F32)] * 5,
        compiler_params=_cparams(("parallel",)))(qkv, qkv, ab, *params)


def _gdn_prep_bwd(qkv, ab, params, cts):
    s = qkv.shape[0]
    t, rows, in_specs = _gdn_prep_specs(s, params, reverse=True)
    n = s // t
    npar = len(params)

    def body(prev_ref, cur_ref, ab_ref, *refs):
        p_refs, g_refs = refs[:npar], refs[npar:npar + 5]
        dcur_ref, dab_ref = refs[npar + 5:npar + 7]
        dp_refs, carry_sc = refs[npar + 7:-1], refs[-1]

        @pl.when(pl.program_id(0) == 0)
        def _():
            carry_sc[...] = jnp.zeros_like(carry_sc)
            for ref in dp_refs:
                ref[...] = jnp.zeros_like(ref)

        has_rows_before = (pl.program_id(0) < n - 1).astype(F32)
        _, vjp = jax.vjp(_gdn_prep_masked(has_rows_before), prev_ref[...], cur_ref[...], ab_ref[...],
                         *[p[...] for p in p_refs])
        d_prev, d_cur, d_ab, *d_params = vjp(tuple(g[...] for g in g_refs))
        dcur_ref[...] = d_cur
        dcur_ref[t - GDN_HALO:, :] += carry_sc[...]
        carry_sc[...] = d_prev
        dab_ref[...] = d_ab
        for ref, val in zip(dp_refs, d_params):
            ref[...] += val

    res = pl.pallas_call(
        body, name="gdn_prep_bwd", grid=(n,), in_specs=in_specs + [rows(GDN_WIDTH)] * 5,
        out_specs=[rows(GDN_QKV), rows(LANES)] + [pl.BlockSpec(p.shape, lambda i: (0, 0)) for p in params],
        out_shape=[jax.ShapeDtypeStruct((s, GDN_QKV), F32), jax.ShapeDtypeStruct((s, LANES), F32)]
        + [jax.ShapeDtypeStruct(p.shape, F32) for p in params],
        scratch_shapes=[pltpu.VMEM((GDN_HALO, GDN_QKV), F32)],
        compiler_params=_cparams(("arbitrary",)))(qkv, qkv, ab, *params, *cts)
    return res[0], res[1], tuple(res[2:])


@jax.custom_vjp
def gdn_prep(qkv, ab, params):
    return tuple(_gdn_prep_fwd(qkv, ab, params))


def _gdn_prep_vjp_fwd(qkv, ab, params):
    return tuple(_gdn_prep_fwd(qkv, ab, params)), (qkv, ab, params)


def _gdn_prep_vjp_bwd(res, cts):
    qkv, ab, params = res
    return _gdn_prep_bwd(qkv, ab, params, cts)


gdn_prep.defvjp(_gdn_prep_vjp_fwd, _gdn_prep_vjp_bwd)


def _tri_dot(x, mode):
    c = x.shape[0]
    tri = (lax.broadcasted_iota(jnp.int32, (c, c), 0) >= lax.broadcasted_iota(jnp.int32, (c, c), 1)).astype(BF16)
    hi = x.astype(BF16)
    rest = x - hi.astype(F32)
    mid = rest.astype(BF16)
    lo = (rest - mid.astype(F32)).astype(BF16)
    dot = lambda part: lax.dot_general(tri, part, _DN[mode], preferred_element_type=F32)
    return dot(hi) + (dot(mid) + dot(lo))


@jax.custom_vjp
def _chunk_cumsum(x):
    return _tri_dot(x, "nn")


_chunk_cumsum.defvjp(lambda x: (_tri_dot(x, "nn"), None), lambda _, g: (_tri_dot(g, "tn"),))


@jax.custom_vjp
def _unit_lower_inverse(lms):
    c = lms[0].shape[0]
    row = lax.broadcasted_iota(jnp.int32, (c, c), 0)
    col = lax.broadcasted_iota(jnp.int32, (c, c), 1)
    ts = [(row == col).astype(F32) - jnp.where((row >> 1) == (col >> 1), lm, 0.0) for lm in lms]
    for level in range(1, int(math.log2(c))):
        below = ((row >> (level + 1)) == (col >> (level + 1))) & ((row >> level) != (col >> level))
        mids = [_dot_split(t, jnp.where(below, lm, 0.0)) for t, lm in zip(ts, lms)]
        ts = [t - _dot_split(mid, t) for t, mid in zip(ts, mids)]
    return tuple(ts)


def _uli_fwd(lms):
    ts = _unit_lower_inverse(lms)
    return ts, ts


def _uli_bwd(ts, gs):
    mids = [bdot(t, g, "tn") for t, g in zip(ts, gs)]
    return (tuple(-bdot(mid, t, "nt") for t, mid in zip(ts, mids)),)


_unit_lower_inverse.defvjp(_uli_fwd, _uli_bwd)


@jax.custom_vjp
def _known_inverse(lms, ts):
    return ts


_known_inverse.defvjp(lambda lms, ts: (ts, ts),
                      lambda ts, gs: (_uli_bwd(ts, gs)[0], tuple(jnp.zeros_like(t) for t in ts)))


GDN_STEP_CHUNKS = 2


def _gdn_chunk_fn(qs, ks, vs, gbs, bbs, s0s, known_ts=None):
    chains, n_heads = range(len(qs)), len(s0s)
    c = qs[0].shape[0]
    row = lax.broadcasted_iota(jnp.int32, (c, c), 0)
    col = lax.broadcasted_iota(jnp.int32, (c, c), 1)
    incl, strict = row >= col, row > col
    gc = [_chunk_cumsum(gbs[i]) for i in chains]
    decay = [jnp.exp(jnp.where(incl, gc[i] - gc[i].T, -1e30)) for i in chains]
    g_last = [jnp.sum(gbs[i], axis=0, keepdims=True) for i in chains]
    eg = [jnp.exp(gc[i]) for i in chains]
    kb = [ks[i] * bbs[i] for i in chains]
    lms = tuple(jnp.where(strict, bdot(kb[i], ks[i], "nt") * decay[i], 0.0) for i in chains)
    ts = _unit_lower_inverse(lms) if known_ts is None else _known_inverse(lms, known_ts)
    u = [bdot(ts[i], vs[i] * bbs[i]) for i in chains]
    w = [bdot(ts[i], kb[i] * eg[i]) for i in chains]
    qk = [bdot(qs[i], ks[i], "nt") * decay[i] for i in chains]
    q_dec = [qs[i] * eg[i] for i in chains]
    k_dec = [ks[i] * jnp.exp(g_last[i] - gc[i]) for i in chains]
    states, outs = list(s0s), []
    for first in range(0, len(qs), n_heads):
        here = range(first, first + n_heads)
        v_new = [u[i] - bdot(w[i], states[i - first]) for i in here]
        outs += [bdot(q_dec[i], states[i - first]) + bdot(qk[i], v_new[i - first]) for i in here]
        states = [states[i - first] * jnp.exp(g_last[i]) + bdot(k_dec[i], v_new[i - first], "tn") for i in here]
    return (tuple(outs), tuple(states)), ts


def _chain_tiles(ref, c):
    return tuple(ref[p * c:(p + 1) * c, h * 128:(h + 1) * 128]
                 for p in range(ref.shape[0] // c) for h in range(GDN_HEADS))


def _gdn_step(s):
    c = min(GDN_CHUNK, s)
    p = min(GDN_STEP_CHUNKS, s // c)
    return c, p, s // (c * p)


def _gdn_fwd(q, k, v, gb, bb):
    s = q.shape[0]
    c, p, steps = _gdn_step(s)

    def body(q_ref, k_ref, v_ref, g_ref, b_ref, o_ref, st_ref, inv_ref, s_sc):
        @pl.when(pl.program_id(0) == 0)
        def _():
            s_sc[...] = jnp.zeros_like(s_sc)

        s0s = tuple(s_sc[h] for h in range(GDN_HEADS))
        for h in range(GDN_HEADS):
            st_ref[h, 0] = s0s[h]
        (os, s1s), ts = _gdn_chunk_fn(*[_chain_tiles(ref, c) for ref in (q_ref, k_ref, v_ref, g_ref, b_ref)], s0s)
        for i in range(p * GDN_HEADS):
            cc, h = divmod(i, GDN_HEADS)
            o_ref[cc * c:(cc + 1) * c, h * 128:(h + 1) * 128] = os[i]
            inv_ref[h, cc] = ts[i]
        for h in range(GDN_HEADS):
            s_sc[h] = s1s[h]

    blk = pl.BlockSpec((p * c, GDN_WIDTH), lambda n: (n, 0))
    return pl.pallas_call(
        body, name="gdn_fwd", grid=(steps,), in_specs=[blk] * 5,
        out_specs=[blk, pl.BlockSpec((GDN_HEADS, 1, 128, 128), lambda n: (0, n, 0, 0)),
                   pl.BlockSpec((GDN_HEADS, p, c, c), lambda n: (0, n, 0, 0))],
        out_shape=[jax.ShapeDtypeStruct((s, GDN_WIDTH), F32), jax.ShapeDtypeStruct((GDN_HEADS, steps, 128, 128), F32),
                   jax.ShapeDtypeStruct((GDN_HEADS, steps * p, c, c), F32)],
        scratch_shapes=[pltpu.VMEM((GDN_HEADS, 128, 128), F32)],
        compiler_params=_cparams(("arbitrary",)))(q, k, v, gb, bb)


def _gdn_bwd(q, k, v, gb, bb, states, inverses, do):
    s = q.shape[0]
    c, p, steps = _gdn_step(s)

    def body(q_ref, k_ref, v_ref, g_ref, b_ref, st_ref, inv_ref, do_ref, dq_ref, dk_ref, dv_ref, dg_ref, db_ref, ds_sc):
        @pl.when(pl.program_id(0) == 0)
        def _():
            ds_sc[...] = jnp.zeros_like(ds_sc)

        s0s = tuple(st_ref[h, 0] for h in range(GDN_HEADS))
        ts = tuple(inv_ref[h, cc] for cc in range(p) for h in range(GDN_HEADS))
        chunks = lambda *args: _gdn_chunk_fn(*args, known_ts=ts)[0]
        _, vjp = jax.vjp(chunks, *[_chain_tiles(ref, c) for ref in (q_ref, k_ref, v_ref, g_ref, b_ref)], s0s)
        *d_tiles, ds0s = vjp((_chain_tiles(do_ref, c), tuple(ds_sc[h] for h in range(GDN_HEADS))))
        for i in range(p * GDN_HEADS):
            cc, h = divmod(i, GDN_HEADS)
            for ref, d in zip((dq_ref, dk_ref, dv_ref, dg_ref, db_ref), d_tiles):
                ref[cc * c:(cc + 1) * c, h * 128:(h + 1) * 128] = d[i]
        for h in range(GDN_HEADS):
            ds_sc[h] = ds0s[h]

    blk = pl.BlockSpec((p * c, GDN_WIDTH), lambda n: (steps - 1 - n, 0))
    return pl.pallas_call(
        body, name="gdn_bwd", grid=(steps,),
        in_specs=[blk] * 5 + [pl.BlockSpec((GDN_HEADS, 1, 128, 128), lambda n: (0, steps - 1 - n, 0, 0)),
                              pl.BlockSpec((GDN_HEADS, p, c, c), lambda n: (0, steps - 1 - n, 0, 0)), blk],
        out_specs=[blk] * 5, out_shape=[jax.ShapeDtypeStruct((s, GDN_WIDTH), F32)] * 5,
        scratch_shapes=[pltpu.VMEM((GDN_HEADS, 128, 128), F32)],
        compiler_params=_cparams(("arbitrary",)))(q, k, v, gb, bb, states, inverses, do)


@jax.custom_vjp
def gdn_core(q, k, v, gb, bb):
    return _gdn_fwd(q, k, v, gb, bb)[0]


def _gdn_core_fwd(q, k, v, gb, bb):
    o, states, inverses = _gdn_fwd(q, k, v, gb, bb)
    return o, (q, k, v, gb, bb, states, inverses)


def _gdn_core_bwd(res, do):
    return tuple(_gdn_bwd(*res, do))


gdn_core.defvjp(_gdn_core_fwd, _gdn_core_bwd)


def _cact_fn(c):
    return (_silu(c),)


def _prenorm_fn(x, w, scale_raw, scale_b, shift_raw, shift_b):
    return (_rms(x, w) * (1.0 + scale_raw + scale_b) + shift_raw + shift_b,)


def _tail_fn(x, o_mla, z_mla, o_gdn, z_gdn, o_norm_w, w_out, post_w, gate_raw, gate_b):
    y_mla = o_mla * _silu(z_mla)
    parts = [_rms(o_gdn[:, h * 128:(h + 1) * 128], o_norm_w) for h in range(GDN_HEADS)]
    y_gdn = jnp.concatenate(parts, axis=1) * _silu(z_gdn)
    y = bdot(y_mla, w_out[:MLA_WIDTH]) + bdot(y_gdn, w_out[MLA_WIDTH:])
    return (x + (gate_raw + gate_b) * _rms(y, post_w),)


def _loss_fn(y, tgt):
    err = y - tgt
    part = jnp.sum(0.5 * jnp.mean(err * err, axis=-1, keepdims=True), axis=0, keepdims=True)
    return (jnp.broadcast_to(part, (1, LANES)),)


def _front_fwd(x, norm_params, ws16):
    s = x.shape[0]
    t = min(512, s)

    def body(x_ref, *refs):
        p_refs, w_refs, o_refs = refs[:len(norm_params)], refs[len(norm_params):-len(ws16)], refs[-len(ws16):]
        (h,) = _prenorm_fn(x_ref[...], *[p[...] for p in p_refs])
        h16 = h.astype(BF16)
        for w_ref, o_ref in zip(w_refs, o_refs):
            o_ref[...] = _dot_raw(h16, w_ref[...], "nn", False)

    rows = lambda d: pl.BlockSpec((t, d), lambda i: (i, 0))
    res = pl.pallas_call(
        body, name="front", grid=(s // t,),
        in_specs=[rows(D_MODEL)] + [_resident(p.shape) for p in norm_params] + [_resident(w.shape) for w in ws16],
        out_specs=[rows(w.shape[1]) for w in ws16],
        out_shape=[jax.ShapeDtypeStruct((s, w.shape[1]), F32) for w in ws16],
        compiler_params=_cparams(("parallel",)))(x, *norm_params, *ws16)
    return tuple(res)


def _front_bwd(x, norm_params, ws16, dys, dx_skip):
    s = x.shape[0]
    t = min(512, s)
    npar, ng = len(norm_params), len(ws16)

    def body(x_ref, *refs):
        p_refs, w_refs, g_refs = refs[:npar], refs[npar:npar + ng], refs[npar + ng:npar + 2 * ng]
        skip_ref, dx_ref = refs[npar + 2 * ng], refs[npar + 2 * ng + 1]
        dp_refs, dw_refs = refs[npar + 2 * ng + 2:2 * npar + 2 * ng + 2], refs[2 * npar + 2 * ng + 2:]

        @pl.when(pl.program_id(0) == 0)
        def _():
            for ref in dp_refs + dw_refs:
                ref[...] = jnp.zeros_like(ref)

        (h,), vjp = jax.vjp(_prenorm_fn, x_ref[...], *[p[...] for p in p_refs])
        h16 = h.astype(BF16)
        dys16 = [g[...].astype(BF16) for g in g_refs]
        dh = functools.reduce(lambda a, b: a + b,
                              [_dot_raw(dy, w_ref[...], "nt", False) for dy, w_ref in zip(dys16, w_refs)])
        for dy, dw_ref in zip(dys16, dw_refs):
            dw_ref[...] += _dot_raw(h16, dy, "tn", False)
        dx, *d_params = vjp((dh,))
        dx_ref[...] = dx + skip_ref[...]
        for ref, val in zip(dp_refs, d_params):
            ref[...] += val

    rows = lambda d: pl.BlockSpec((t, d), lambda i: (i, 0))
    res = pl.pallas_call(
        body, name="front_bwd", grid=(s // t,),
        in_specs=[rows(D_MODEL)] + [_resident(p.shape) for p in norm_params] + [_resident(w.shape) for w in ws16]
        + [rows(w.shape[1]) for w in ws16] + [rows(D_MODEL)],
        out_specs=[rows(D_MODEL)] + [_resident(p.shape) for p in norm_params] + [_resident(w.shape) for w in ws16],
        out_shape=[jax.ShapeDtypeStruct(x.shape, F32)] + [jax.ShapeDtypeStruct(p.shape, F32) for p in norm_params]
        + [jax.ShapeDtypeStruct(w.shape, F32) for w in ws16],
        compiler_params=_cparams(("arbitrary",)))(x, *norm_params, *ws16, *dys, dx_skip)
    return res[0], tuple(res[1:1 + npar]), tuple(res[1 + npar:])


@jax.custom_vjp
def front(x, norm_params, ws):
    return _front_fwd(x, norm_params, tuple(w.astype(BF16) for w in ws)) + (x,)


def _front_vjp_fwd(x, norm_params, ws):
    ws16 = tuple(w.astype(BF16) for w in ws)
    return _front_fwd(x, norm_params, ws16) + (x,), (x, norm_params, ws16)


def _front_vjp_bwd(res, cts):
    return _front_bwd(*res, cts[:-1], cts[-1])


front.defvjp(_front_vjp_fwd, _front_vjp_bwd)


_OPS = dict(
    cact=make_rowwise(_cact_fn, "c_act", (D_MODEL,), tile=16),
    tail=make_rowwise(_tail_fn, "tail", (D_MODEL,), tile=512),
    loss=make_rowwise(_loss_fn, "loss", (), acc_dims=(LANES,), n_nondiff=1),
)


def _swap_halves(w):
    half = w.shape[-1] // 2
    return jnp.concatenate([w[..., half:], w[..., :half]], axis=-1)


def _w_in_groups(w):
    k_pe = w[:, 640:704]
    ab = jnp.concatenate([w[:, 2752:2760], jnp.zeros((w.shape[0], LANES - 8), w.dtype)], axis=1)
    return (w[:, :384], w[:, 384:640], jnp.concatenate([k_pe, _swap_halves(k_pe)], axis=1), w[:, 704:1216],
            w[:, 1216:2752], ab, w[:, 2760:])


def _q_up_ext(w):
    parts = []
    for h in range(MLA_HEADS):
        rope = w[:, h * 192 + 128:(h + 1) * 192]
        parts += [w[:, h * 192:h * 192 + 128], rope, _swap_halves(rope)]
    return jnp.concatenate(parts, axis=1)


def _kv_up_perm(w):
    ks = [w[:, h * 256:h * 256 + 128] for h in range(MLA_HEADS)]
    vs = [w[:, h * 256 + 128:(h + 1) * 256] for h in range(MLA_HEADS)]
    return jnp.concatenate(ks + vs, axis=1)


def _pad_lanes(v):
    return jnp.pad(v, (0, LANES - v.shape[0]))[None, :]


def _local_loss(weights, mods, x, positions, target):
    s = x.shape[0]
    half = MLA_ROPE // 2
    inv_freq = jnp.power(ROPE_THETA, -jnp.arange(half, dtype=F32) * 2.0 / MLA_ROPE)
    ang = positions.astype(F32)[:, None] * inv_freq
    cos, sin, zero = jnp.cos(ang), jnp.sin(ang), jnp.zeros((s, 2 * half), F32)
    t1 = jnp.concatenate([cos, cos, zero], axis=1)
    t2 = jnp.concatenate([-sin, sin, zero], axis=1)

    for l in range(DEPTH):
        mod = mods[l]
        b = weights["b_mod"][l][None, :]
        shift_raw, scale_raw, gate_raw = mod[:, :1024], mod[:, 1024:2048], mod[:, 2048:]
        shift_b, scale_b, gate_b = b[:, :1024], b[:, 1024:2048], b[:, 2048:]
        q_lat, kv_lat, kr, z_mla, qkv, ab, z_gdn, x = front(
            x, (weights["pre_norm_w"][l][None], scale_raw, scale_b, shift_raw, shift_b),
            _w_in_groups(weights["w_in"][l]))
        o_mla = mla_attention((q_lat, kv_lat, kr, t1, t2),
                              (weights["mla_q_norm_w"][l][None], _q_up_ext(weights["mla_q_up"][l]),
                               weights["mla_kv_norm_w"][l][None], _kv_up_perm(weights["mla_kv_up"][l])))
        cw = weights["gdn_conv_w"][l]
        params = tuple(cw[j][None] for j in range(GDN_CONV))
        params += (_pad_lanes(weights["gdn_a_log"][l]), _pad_lanes(weights["gdn_dt_bias"][l]))
        o_gdn = gdn_core(*gdn_prep(qkv, ab, params))
        (x,) = _OPS["tail"]((x, o_mla, z_mla, o_gdn, z_gdn),
                            (weights["gdn_o_norm_w"][l][None], weights["w_out"][l], weights["post_norm_w"][l][None],
                             gate_raw, gate_b))
    (acc,) = _OPS["loss"]((x, target), ())
    return acc[0, 0]


def _chip_index():
    return 2 * lax.axis_index("x") + lax.axis_index("y")


def _other_chips(x, y):
    return [(1 - x, y), (x, 1 - y), (1 - x, 1 - y)]


def _any_spec():
    return pl.BlockSpec(memory_space=pl.ANY)


def _half(ref, hc):
    n = ref.shape[0] // 2
    return ref.at[pl.ds(hc * n, n)]


def ag_weights(shards):
    n = len(shards)

    def body(*refs):
        ins, outs = refs[:n], refs[n:2 * n]
        send_sems, recv_sems = refs[2 * n:]
        x, y, c = lax.axis_index("x"), lax.axis_index("y"), lax.axis_index("c")
        sibling = (x, y, 1 - c)
        chips = _other_chips(x, y)

        def copy(t, k, src, chip_xy, hc, to):
            return pltpu.make_async_remote_copy(
                src_ref=src, dst_ref=_half(outs[t].at[2 * chip_xy[0] + chip_xy[1]], hc),
                send_sem=send_sems.at[6 * t + k], recv_sem=recv_sems.at[6 * t + k], device_id=to, device_id_type=MESH)

        first = [copy(t, k, _half(ins[t], c), (x, y), c, (*chip, c)) for k, chip in enumerate(chips) for t in range(n)]
        for cp in first:
            cp.start()
        passed = []
        for k, chip in enumerate(chips):
            for t in range(n):
                landed = _half(outs[t].at[2 * chip[0] + chip[1]], c)
                copy(t, k, landed, chip, c, (x, y, c)).wait_recv()
                passed.append(copy(t, 3 + k, landed, chip, c, sibling))
                passed[-1].start()
        for k, chip in enumerate(chips):
            for t in range(n):
                copy(t, 3 + k, _half(ins[t], c), chip, 1 - c, (x, y, c)).wait_recv()
        for cp in first + passed:
            cp.wait_send()

    return pl.pallas_call(
        body, name="ag_weights", in_specs=[_any_spec()] * n, out_specs=[_any_spec()] * n,
        out_shape=[jax.ShapeDtypeStruct((N_CHIPS,) + a.shape, a.dtype) for a in shards],
        scratch_shapes=[pltpu.SemaphoreType.DMA((6 * n,)), pltpu.SemaphoreType.DMA((6 * n,))],
        compiler_params=pltpu.CompilerParams(has_side_effects=True))(*shards)


def all_gather_rows(x):
    r, cols = x.shape
    flips = [(k >> 2 & 1, k >> 1 & 1, k & 1) for k in range(1, 8)]

    def body(x_ref, out_ref, send_sems, recv_sems):
        here = (lax.axis_index("x"), lax.axis_index("y"), lax.axis_index("c"))
        peers = [tuple(1 - p if f else p for p, f in zip(here, flip)) for flip in flips]
        slot = lambda dev: 4 * dev[0] + 2 * dev[1] + dev[2]
        out_ref[slot(here)] = x_ref[...]
        copies = [pltpu.make_async_remote_copy(
            src_ref=x_ref, dst_ref=out_ref.at[slot(here)], send_sem=send_sems.at[k], recv_sem=recv_sems.at[k],
            device_id=peer, device_id_type=MESH) for k, peer in enumerate(peers)]
        for cp in copies:
            cp.start()
        for k, peer in enumerate(peers):
            pltpu.make_async_remote_copy(
                src_ref=x_ref, dst_ref=out_ref.at[slot(peer)], send_sem=send_sems.at[k], recv_sem=recv_sems.at[k],
                device_id=peer, device_id_type=MESH).wait_recv()
        for cp in copies:
            cp.wait_send()

    return pl.pallas_call(
        body, name="all_gather_rows", in_specs=[pl.BlockSpec(memory_space=pltpu.VMEM)],
        out_specs=pl.BlockSpec(memory_space=pltpu.VMEM), out_shape=jax.ShapeDtypeStruct((8, r, cols), F32),
        scratch_shapes=[pltpu.SemaphoreType.DMA((7,)), pltpu.SemaphoreType.DMA((7,))],
        compiler_params=pltpu.CompilerParams(has_side_effects=True, vmem_limit_bytes=VMEM_LIMIT))(x)


def rs_pair(gs):
    n = len(gs)

    def body(*refs):
        ins, outs = refs[:n], refs[n:2 * n]
        send_sems, recv_sems = refs[2 * n:]
        x, y, c = lax.axis_index("x"), lax.axis_index("y"), lax.axis_index("c")
        lh = [g.shape[1] // 2 for g in gs]
        copies = [pltpu.make_async_remote_copy(
            src_ref=ins[t].at[:, pl.ds((1 - c) * lh[t], lh[t])], dst_ref=outs[t], send_sem=send_sems.at[t],
            recv_sem=recv_sems.at[t], device_id=(x, y, 1 - c), device_id_type=MESH) for t in range(n)]
        for cp in copies:
            cp.start()
        for cp in copies:
            cp.wait()

    return pl.pallas_call(
        body, name="rs_pair", in_specs=[_any_spec()] * n, out_specs=[_any_spec()] * n,
        out_shape=[jax.ShapeDtypeStruct((N_CHIPS, g.shape[1] // 2) + g.shape[2:], F32) for g in gs],
        scratch_shapes=[pltpu.SemaphoreType.DMA((n,)), pltpu.SemaphoreType.DMA((n,))],
        compiler_params=pltpu.CompilerParams(has_side_effects=True))(*gs)


def rs_cross(pairs):
    n = len(pairs)

    def body(*refs):
        ins, outs = refs[:n], refs[n:2 * n]
        send_sems, recv_sems = refs[2 * n:]
        x, y, c = lax.axis_index("x"), lax.axis_index("y"), lax.axis_index("c")
        copies = []
        for k, chip in enumerate(_other_chips(x, y)):
            for t in range(n):
                copies.append(pltpu.make_async_remote_copy(
                    src_ref=ins[t].at[2 * chip[0] + chip[1]], dst_ref=outs[t].at[k], send_sem=send_sems.at[3 * t + k],
                    recv_sem=recv_sems.at[3 * t + k], device_id=(*chip, c), device_id_type=MESH))
        for cp in copies:
            cp.start()
        for cp in copies:
            cp.wait()

    return pl.pallas_call(
        body, name="rs_cross", in_specs=[_any_spec()] * n, out_specs=[_any_spec()] * n,
        out_shape=[jax.ShapeDtypeStruct((3,) + p.shape[1:], p.dtype) for p in pairs],
        scratch_shapes=[pltpu.SemaphoreType.DMA((3 * n,)), pltpu.SemaphoreType.DMA((3 * n,))],
        compiler_params=pltpu.CompilerParams(has_side_effects=True))(*pairs)


def rs_share(blocks):
    n = len(blocks)

    def body(*refs):
        ins, outs = refs[:n], refs[n:2 * n]
        send_sems, recv_sems = refs[2 * n:]
        x, y, c = lax.axis_index("x"), lax.axis_index("y"), lax.axis_index("c")
        sends = [pltpu.make_async_remote_copy(
            src_ref=_half(ins[t], c), dst_ref=_half(outs[t], c), send_sem=send_sems.at[t], recv_sem=recv_sems.at[t],
            device_id=(x, y, 1 - c), device_id_type=MESH) for t in range(n)]
        for cp in sends:
            cp.start()
        for t in range(n):
            pltpu.make_async_remote_copy(
                src_ref=_half(ins[t], c), dst_ref=_half(outs[t], 1 - c), send_sem=send_sems.at[t],
                recv_sem=recv_sems.at[t], device_id=(x, y, 1 - c), device_id_type=MESH).wait_recv()
        for cp in sends:
            cp.wait_send()

    return pl.pallas_call(
        body, name="rs_share", in_specs=[_any_spec()] * n, out_specs=[_any_spec()] * n,
        out_shape=[jax.ShapeDtypeStruct(b.shape, F32) for b in blocks],
        input_output_aliases={t: t for t in range(n)},
        scratch_shapes=[pltpu.SemaphoreType.DMA((n,)), pltpu.SemaphoreType.DMA((n,))],
        compiler_params=pltpu.CompilerParams(has_side_effects=True))(*blocks)


def _row_tile(rows, target):
    best = None
    for t in range(8, min(rows, target) + 1, 8):
        if rows % t == 0:
            best = t
    return rows if best is None else best


TILE_BYTES = 2 * 1024 * 1024


def _flat_rows(shape):
    rows = int(np.prod(shape[1:-1]))
    cols_padded = -(-shape[-1] // LANES) * LANES
    return rows, _row_tile(rows, max(8, TILE_BYTES // (4 * cols_padded)))


def pair_add(g, from_sibling, out_dtype):
    cols = g.shape[-1]
    rph, t = _flat_rows(from_sibling.shape)
    nt = rph // t
    c_arr = lax.axis_index("c").astype(jnp.int32).reshape(1)

    def body(c_ref, a_ref, b_ref, o_ref):
        o_ref[...] = (a_ref[...] + b_ref[...]).astype(o_ref.dtype)

    out = pl.pallas_call(
        body, name="pair_add",
        grid_spec=pltpu.PrefetchScalarGridSpec(
            num_scalar_prefetch=1, grid=(N_CHIPS, nt),
            in_specs=[pl.BlockSpec((t, cols), lambda j, i, c_ref: (j * 2 * nt + c_ref[0] * nt + i, 0)),
                      pl.BlockSpec((t, cols), lambda j, i, c_ref: (j * nt + i, 0))],
            out_specs=pl.BlockSpec((t, cols), lambda j, i, c_ref: (j * nt + i, 0))),
        out_shape=jax.ShapeDtypeStruct((N_CHIPS * rph, cols), out_dtype),
        compiler_params=_cparams(("parallel", "parallel")))(c_arr, g.reshape(-1, cols), from_sibling.reshape(-1, cols))
    return out.reshape(from_sibling.shape)


def chip_add(pairs, received):
    cols = pairs.shape[-1]
    rph, t = _flat_rows(pairs.shape)
    nt = rph // t
    j_arr = _chip_index().astype(jnp.int32).reshape(1)
    c_arr = lax.axis_index("c").astype(jnp.int32).reshape(1)
    r2 = received.reshape(-1, cols)

    def body(j_ref, c_ref, a_ref, r0_ref, r1_ref, r2_ref, o_ref):
        a, r0, r1, r2 = [ref[...].astype(F32) for ref in (a_ref, r0_ref, r1_ref, r2_ref)]
        o_ref[...] = (a + r0) + (r1 + r2)

    out = pl.pallas_call(
        body, name="chip_add",
        grid_spec=pltpu.PrefetchScalarGridSpec(
            num_scalar_prefetch=2, grid=(nt,),
            in_specs=[pl.BlockSpec((t, cols), lambda i, j_ref, c_ref: (j_ref[0] * nt + i, 0)),
                      pl.BlockSpec((t, cols), lambda i, j_ref, c_ref: (i, 0)),
                      pl.BlockSpec((t, cols), lambda i, j_ref, c_ref: (nt + i, 0)),
                      pl.BlockSpec((t, cols), lambda i, j_ref, c_ref: (2 * nt + i, 0))],
            out_specs=pl.BlockSpec((t, cols), lambda i, j_ref, c_ref: (c_ref[0] * nt + i, 0))),
        out_shape=jax.ShapeDtypeStruct((2 * rph, cols), F32),
        compiler_params=_cparams(("parallel",)))(j_arr, c_arr, pairs.reshape(-1, cols), r2, r2, r2)
    return out.reshape((2 * pairs.shape[1],) + pairs.shape[2:])


def reduce_grads(gs, cross_dtypes):
    pairs = [pair_add(g, r, dt) for g, r, dt in zip(gs, rs_pair(gs), cross_dtypes)]
    return rs_share([chip_add(p, r) for p, r in zip(pairs, rs_cross(pairs))])


def adamw(w, g, m, v):
    shape = w.shape
    cols = shape[-1]
    rows = int(np.prod(shape[:-1]))
    flat = lambda a: a.reshape(rows, cols)
    t = _row_tile(rows, 256)

    def body(w_ref, g_ref, m_ref, v_ref, d_ref, mo_ref, vo_ref):
        gv = g_ref[...]
        m_new = ADAM_B1 * m_ref[...] + (1.0 - ADAM_B1) * gv
        v_new = ADAM_B2 * v_ref[...] + (1.0 - ADAM_B2) * (gv * gv)
        m_hat = m_new / (1.0 - ADAM_B1 ** ADAM_STEP)
        v_hat = v_new / (1.0 - ADAM_B2 ** ADAM_STEP)
        d_ref[...] = -ADAM_LR * (m_hat / (jnp.sqrt(v_hat) + ADAM_EPS) + ADAM_WD * w_ref[...])
        mo_ref[...] = m_new
        vo_ref[...] = v_new

    spec = pl.BlockSpec((t, cols), lambda i: (i, 0))
    outs = pl.pallas_call(
        body, name="adamw", grid=(rows // t,), in_specs=[spec] * 4, out_specs=[spec] * 3,
        out_shape=[jax.ShapeDtypeStruct((rows, cols), F32)] * 3,
        compiler_params=_cparams(("parallel",)))(flat(w), flat(g), flat(m), flat(v))
    return tuple(o.reshape(shape) for o in outs)


SHARDED = (("w_in", 2), ("mla_q_up", 2), ("mla_kv_up", 2), ("gdn_conv_w", 2), ("w_out", 1))
REPLICATED = ("b_mod", "pre_norm_w", "post_norm_w", "mla_q_norm_w", "mla_kv_norm_w", "gdn_a_log", "gdn_dt_bias",
              "gdn_o_norm_w")
WEIGHT_ORDER = ("w_mod", "b_mod", "pre_norm_w", "post_norm_w", "w_in", "mla_q_norm_w", "mla_q_up", "mla_kv_norm_w",
                "mla_kv_up", "gdn_conv_w", "gdn_a_log", "gdn_dt_bias", "gdn_o_norm_w", "w_out")
EXACT_F32 = ("gdn_conv_w",)
SMALL_ROWS = 48


def _gather_weights(shards):
    names = [name for name, _ in SHARDED]
    own = [shards[n] if n in EXACT_F32 else shards[n].astype(BF16) for n in names]
    gathered = ag_weights(own)
    full = {}
    for (name, axis), blk, mine in zip(SHARDED, gathered, own):
        shp = shards[name].shape
        blk = lax.dynamic_update_index_in_dim(blk, mine, _chip_index(), 0)
        blk = jnp.moveaxis(blk.astype(F32), 0, axis)
        full[name] = blk.reshape(shp[:axis] + (N_CHIPS * shp[axis],) + shp[axis + 1:])
    return full


def _split_grads(grads):
    pieces = []
    for name, axis in SHARDED:
        g = grads[name]
        shp = g.shape
        g = g.reshape(shp[:axis] + (N_CHIPS, shp[axis] // N_CHIPS) + shp[axis + 1:])
        pieces.append(jnp.moveaxis(g, axis, 0))
    small = jnp.concatenate([grads[name] for name in REPLICATED], axis=1)
    small = jnp.pad(small, ((0, 0), (0, SMALL_ROWS * LANES - small.shape[1]))).reshape(DEPTH, SMALL_ROWS, LANES)
    pieces.append(jnp.broadcast_to(small[None], (N_CHIPS,) + small.shape))
    return pieces


def _unsplit_small(small, rep_shapes):
    flat = small.reshape(DEPTH, SMALL_ROWS * LANES)
    out, off = {}, 0
    for name in REPLICATED:
        size = rep_shapes[name][1]
        out[name] = flat[:, off:off + size]
        off += size
    return out


MOD_ROWS = 16


def _device_slot():
    return 4 * lax.axis_index("x") + 2 * lax.axis_index("y") + lax.axis_index("c")


def _adaln_projection(c, w_mod_shard):
    (c_act,) = _OPS["cact"]((jnp.pad(c, ((0, 7), (0, 0))),), ())
    c_acts = jnp.pad(all_gather_rows(c_act)[:, 0, :], ((0, MOD_ROWS - 8), (0, 0)))
    part = jnp.concatenate([_mm(c_acts, w_mod_shard[l].astype(BF16), "nn", "mod_proj") for l in range(DEPTH)], axis=0)
    parts = all_gather_rows(part)[::2].reshape(N_CHIPS, DEPTH, MOD_ROWS, -1)
    mods = jnp.moveaxis(parts, 0, 2).reshape(DEPTH, MOD_ROWS, -1)
    return c_acts, lax.dynamic_slice_in_dim(mods, _device_slot(), 1, axis=1)


def _adaln_weight_grad(c_acts, d_mods):
    cols = d_mods.shape[-1] // N_CHIPS
    rows = jnp.pad(d_mods[:, 0, :], ((0, 8 - DEPTH), (0, 0)))
    all_rows = all_gather_rows(rows)[:, :DEPTH, :]
    mine = lax.dynamic_slice_in_dim(all_rows, _chip_index() * cols, cols, axis=2)
    mine = jnp.pad(mine, ((0, MOD_ROWS - 8), (0, 0), (0, 0)))
    return jnp.stack([_mm(c_acts, mine[:, l, :], "tn", "mod_dw") for l in range(DEPTH)])


def kernel(x, c, positions, w_mod, b_mod, pre_norm_w, post_norm_w, w_in, mla_q_norm_w, mla_q_up, mla_kv_norm_w, mla_kv_up, gdn_conv_w, gdn_a_log, gdn_dt_bias, gdn_o_norm_w, w_out, loss_target, m_w_mod, m_b_mod, m_pre_norm_w, m_post_norm_w, m_w_in, m_mla_q_norm_w, m_mla_q_up, m_mla_kv_norm_w, m_mla_kv_up, m_gdn_conv_w, m_gdn_a_log, m_gdn_dt_bias, m_gdn_o_norm_w, m_w_out, v_w_mod, v_b_mod, v_pre_norm_w, v_post_norm_w, v_w_in, v_mla_q_norm_w, v_mla_q_up, v_mla_kv_norm_w, v_mla_kv_up, v_gdn_conv_w, v_gdn_a_log, v_gdn_dt_bias, v_gdn_o_norm_w, v_w_out):
    given = dict(w_mod=w_mod, b_mod=b_mod, pre_norm_w=pre_norm_w, post_norm_w=post_norm_w, w_in=w_in,
                 mla_q_norm_w=mla_q_norm_w, mla_q_up=mla_q_up, mla_kv_norm_w=mla_kv_norm_w, mla_kv_up=mla_kv_up,
                 gdn_conv_w=gdn_conv_w, gdn_a_log=gdn_a_log, gdn_dt_bias=gdn_dt_bias, gdn_o_norm_w=gdn_o_norm_w,
                 w_out=w_out)
    moments_m = dict(w_mod=m_w_mod, b_mod=m_b_mod, pre_norm_w=m_pre_norm_w, post_norm_w=m_post_norm_w, w_in=m_w_in,
                     mla_q_norm_w=m_mla_q_norm_w, mla_q_up=m_mla_q_up, mla_kv_norm_w=m_mla_kv_norm_w,
                     mla_kv_up=m_mla_kv_up, gdn_conv_w=m_gdn_conv_w, gdn_a_log=m_gdn_a_log,
                     gdn_dt_bias=m_gdn_dt_bias, gdn_o_norm_w=m_gdn_o_norm_w, w_out=m_w_out)
    moments_v = dict(w_mod=v_w_mod, b_mod=v_b_mod, pre_norm_w=v_pre_norm_w, post_norm_w=v_post_norm_w, w_in=v_w_in,
                     mla_q_norm_w=v_mla_q_norm_w, mla_q_up=v_mla_q_up, mla_kv_norm_w=v_mla_kv_norm_w,
                     mla_kv_up=v_mla_kv_up, gdn_conv_w=v_gdn_conv_w, gdn_a_log=v_gdn_a_log,
                     gdn_dt_bias=v_gdn_dt_bias, gdn_o_norm_w=v_gdn_o_norm_w, w_out=v_w_out)

    full = _gather_weights({name: given[name] for name, _ in SHARDED})
    for name in REPLICATED:
        full[name] = given[name]
    c_acts, mods = _adaln_projection(c, w_mod)
    loss_local, (grads, d_mods, grad_x) = jax.value_and_grad(_local_loss, argnums=(0, 1, 2))(
        full, mods, x[0], positions[0], loss_target[0])
    loss = lax.psum(loss_local, AXES)

    cross_dtypes = [F32 if name in EXACT_F32 else BF16 for name, _ in SHARDED] + [F32]
    reduced = reduce_grads(_split_grads(grads), cross_dtypes)
    grad_w = {name: g for (name, _), g in zip(SHARDED, reduced)}
    grad_w.update(_unsplit_small(reduced[-1], {name: given[name].shape for name in REPLICATED}))
    grad_w["w_mod"] = _adaln_weight_grad(c_acts, d_mods)
    delta, new_m, new_v = {}, {}, {}
    for name in WEIGHT_ORDER:
        delta[name], new_m[name], new_v[name] = adamw(given[name], grad_w[name], moments_m[name], moments_v[name])
    return (loss, grad_x[None], *[grad_w[n] for n in WEIGHT_ORDER], *[delta[n] for n in WEIGHT_ORDER],
            *[new_m[n] for n in WEIGHT_ORDER], *[new_v[n] for n in WEIGHT_ORDER])
```

```python
import functools
import math

import numpy as np
import jax
import jax.numpy as jnp
from jax import lax
from jax.experimental import pallas as pl
from jax.experimental.pallas import tpu as pltpu

F32 = jnp.float32
BF16 = jnp.bfloat16
MESH = pl.DeviceIdType.MESH
AXES = ("x", "y", "c")

D_MODEL = 1024
DEPTH = 4
MLA_HEADS = 4
MLA_NOPE = 128
MLA_ROPE = 64
MLA_WIDTH = 512
GDN_HEADS = 4
GDN_DK = 128
GDN_WIDTH = 512
GDN_QKV = 1536
GDN_CONV = 4
ROPE_THETA = 10000.0
NORM_EPS = 1e-6
ADAM_LR, ADAM_B1, ADAM_B2, ADAM_EPS, ADAM_WD, ADAM_STEP = 0.001, 0.9, 0.999, 1e-08, 0.01, 10

LANES = 128
N_CHIPS = 4
GDN_CHUNK = 128
VMEM_LIMIT = 56 * 1024 * 1024


def _cparams(sem=None):
    if sem is None:
        return pltpu.CompilerParams(vmem_limit_bytes=VMEM_LIMIT)
    return pltpu.CompilerParams(dimension_semantics=sem, vmem_limit_bytes=VMEM_LIMIT)


def _pick(dim, target):
    if dim <= target:
        return dim
    best = None
    for t in range(LANES, target + 1, LANES):
        if dim % t == 0:
            best = t
    assert best is not None, (dim, target)
    return best


def _resident(shape):
    return pl.BlockSpec(shape, lambda i: (0,) * len(shape), pipeline_mode=pl.Buffered(1))


_DN = {"nn": (((1,), (0,)), ((), ())), "nt": (((1,), (1,)), ((), ())), "tn": (((0,), (0,)), ((), ()))}


def _dot_raw(a, b, mode, exact):
    if exact:
        return lax.dot_general(a, b, _DN[mode], precision=lax.Precision.HIGHEST, preferred_element_type=F32)
    return lax.dot_general(a.astype(BF16), b.astype(BF16), _DN[mode], preferred_element_type=F32)


def _dot_split(a, b):
    a_hi, b_hi = a.astype(BF16), b.astype(BF16)
    a_lo, b_lo = (a - a_hi.astype(F32)).astype(BF16), (b - b_hi.astype(F32)).astype(BF16)
    dot = lambda u, w: lax.dot_general(u, w, _DN["nn"], preferred_element_type=F32)
    return dot(a_hi, b_hi) + (dot(a_hi, b_lo) + dot(a_lo, b_hi))


@functools.partial(jax.custom_vjp, nondiff_argnums=(2, 3))
def bdot(a, b, mode="nn", exact=False):
    return _dot_raw(a, b, mode, exact)


def _bdot_fwd(a, b, mode, exact):
    return _dot_raw(a, b, mode, exact), (a, b)


def _bdot_bwd(mode, exact, res, g):
    a, b = res
    if mode == "nn":
        return bdot(g, b, "nt", exact), bdot(a, g, "tn", exact)
    if mode == "nt":
        return bdot(g, b, "nn", exact), bdot(g, a, "tn", exact)
    return bdot(b, g, "nt", exact), bdot(a, g, "nn", exact)


bdot.defvjp(_bdot_fwd, _bdot_bwd)


@jax.custom_vjp
def roll_half(x):
    return pltpu.roll(x, 64, 1)


roll_half.defvjp(lambda x: (pltpu.roll(x, 64, 1), None), lambda _, g: (pltpu.roll(g, 64, 1),))


def _sigmoid(x):
    return 1.0 / (1.0 + jnp.exp(-x))


def _silu(x):
    return x * _sigmoid(x)


def _softplus(x):
    return jnp.maximum(x, 0.0) + jnp.log(1.0 + jnp.exp(-jnp.abs(x)))


def _rms(x, w):
    return x * lax.rsqrt(jnp.mean(x * x, axis=-1, keepdims=True) + NORM_EPS) * w


def _rw_fwd(fn, name, rows, params, out_dims, out_dtypes, acc_dims, tile):
    s = rows[0].shape[0]
    t = min(tile, s)
    n = s // t
    nr, npar, no, na = len(rows), len(params), len(out_dims), len(acc_dims)

    def body(*refs):
        r, p = refs[:nr], refs[nr:nr + npar]
        o, a = refs[nr + npar:nr + npar + no], refs[nr + npar + no:]
        outs = fn(*[x[...] for x in r], *[x[...] for x in p])
        for ref, val in zip(o, outs[:no]):
            ref[...] = val.astype(ref.dtype)
        if na:
            @pl.when(pl.program_id(0) == 0)
            def _():
                for ref in a:
                    ref[...] = jnp.zeros_like(ref)
            for ref, val in zip(a, outs[no:]):
                ref[...] += val

    in_specs = [pl.BlockSpec((t, x.shape[1]), lambda i: (i, 0)) for x in rows]
    in_specs += [_resident(x.shape) for x in params]
    out_specs = [pl.BlockSpec((t, d), lambda i: (i, 0)) for d in out_dims]
    out_specs += [_resident((1, d)) for d in acc_dims]
    out_shape = [jax.ShapeDtypeStruct((s, d), dt) for d, dt in zip(out_dims, out_dtypes)]
    out_shape += [jax.ShapeDtypeStruct((1, d), F32) for d in acc_dims]
    res = pl.pallas_call(body, name=name, grid=(n,), in_specs=in_specs, out_specs=out_specs, out_shape=out_shape,
                         compiler_params=_cparams(("arbitrary",)))(*rows, *params)
    return tuple(res)


def _rw_bwd(fn, name, rows, params, row_cts, acc_cts, n_diff, tile):
    s = rows[0].shape[0]
    t = min(tile, s)
    n = s // t
    nr, npar, no, na = len(rows), len(params), len(row_cts), len(acc_cts)

    def body(*refs):
        r, p = refs[:nr], refs[nr:nr + npar]
        g, ga = refs[nr + npar:nr + npar + no], refs[nr + npar + no:nr + npar + no + na]
        dr, dp = refs[nr + npar + no + na:nr + npar + no + na + n_diff], refs[nr + npar + no + na + n_diff:]
        _, vjp = jax.vjp(fn, *[x[...] for x in r], *[x[...] for x in p])
        cts = vjp(tuple([x[...] for x in g] + [x[...] for x in ga]))
        for ref, val in zip(dr, cts[:n_diff]):
            ref[...] = val
        if npar:
            @pl.when(pl.program_id(0) == 0)
            def _():
                for ref in dp:
                    ref[...] = jnp.zeros_like(ref)
            for ref, val in zip(dp, cts[nr:]):
                ref[...] += val

    in_specs = [pl.BlockSpec((t, x.shape[1]), lambda i: (i, 0)) for x in rows]
    in_specs += [_resident(x.shape) for x in params]
    in_specs += [pl.BlockSpec((t, x.shape[1]), lambda i: (i, 0)) for x in row_cts]
    in_specs += [_resident(x.shape) for x in acc_cts]
    out_specs = [pl.BlockSpec((t, x.shape[1]), lambda i: (i, 0)) for x in rows[:n_diff]]
    out_specs += [_resident(x.shape) for x in params]
    out_shape = [jax.ShapeDtypeStruct(x.shape, F32) for x in rows[:n_diff]]
    out_shape += [jax.ShapeDtypeStruct(x.shape, F32) for x in params]
    res = pl.pallas_call(body, name=name, grid=(n,), in_specs=in_specs, out_specs=out_specs, out_shape=out_shape,
                         compiler_params=_cparams(("arbitrary",)))(*rows, *params, *row_cts, *acc_cts)
    return tuple(res[:n_diff]), tuple(res[n_diff:])


def make_rowwise(fn, name, out_dims, acc_dims=(), n_nondiff=0, tile=256):
    out_dtypes = (F32,) * len(out_dims)

    @jax.custom_vjp
    def op(rows, params):
        return _rw_fwd(fn, name, rows, params, out_dims, out_dtypes, acc_dims, tile)

    def fwd(rows, params):
        return op(rows, params), (rows, params)

    def bwd(res, cts):
        rows, params = res
        n_diff = len(rows) - n_nondiff
        d_rows, d_params = _rw_bwd(fn, name + "_bwd", rows, params, cts[:len(out_dims)], cts[len(out_dims):],
                                   n_diff, tile)
        d_rows = d_rows + tuple(jnp.zeros_like(x) for x in rows[n_diff:])
        return d_rows, d_params

    op.defvjp(fwd, bwd)
    return op


def _mm(a, b, mode, name):
    if mode == "nn":
        (m, k), (_, n) = a.shape, b.shape
    elif mode == "nt":
        (m, k), (n, _) = a.shape, b.shape
    else:
        (k, m), (_, n) = a.shape, b.shape
    tm = _pick(m, 512)
    tn = _pick(n, 1152)
    tk = _pick(k, 1152) if mode != "tn" else _pick(k, 512)
    nk = k // tk

    def body(a_ref, b_ref, o_ref, acc_ref):
        kk = pl.program_id(2)

        @pl.when(kk == 0)
        def _():
            acc_ref[...] = jnp.zeros_like(acc_ref)

        acc_ref[...] += _dot_raw(a_ref[...], b_ref[...], mode, False)

        @pl.when(kk == nk - 1)
        def _():
            o_ref[...] = acc_ref[...]

    if mode == "nn":
        a_spec = pl.BlockSpec((tm, tk), lambda i, j, kk: (i, kk))
        b_spec = pl.BlockSpec((tk, tn), lambda i, j, kk: (kk, j))
    elif mode == "nt":
        a_spec = pl.BlockSpec((tm, tk), lambda i, j, kk: (i, kk))
        b_spec = pl.BlockSpec((tn, tk), lambda i, j, kk: (j, kk))
    else:
        a_spec = pl.BlockSpec((tk, tm), lambda i, j, kk: (kk, i))
        b_spec = pl.BlockSpec((tk, tn), lambda i, j, kk: (kk, j))
    return pl.pallas_call(
        body, name=name, grid=(m // tm, n // tn, nk), in_specs=[a_spec, b_spec],
        out_specs=pl.BlockSpec((tm, tn), lambda i, j, kk: (i, j)),
        out_shape=jax.ShapeDtypeStruct((m, n), F32), scratch_shapes=[pltpu.VMEM((tm, tn), F32)],
        compiler_params=_cparams(("parallel", "parallel", "arbitrary")))(a, b)


def _mla_prep_fn(qraw, kvraw, kr, t1, t2):
    kr_rot = kr * t1 + roll_half(kr) * t2
    qs, ks = [], []
    for h in range(MLA_HEADS):
        q_r = qraw[:, h * 256 + 128:(h + 1) * 256]
        qs += [qraw[:, h * 256:h * 256 + 128], q_r * t1 + roll_half(q_r) * t2]
        ks += [kvraw[:, h * 128:(h + 1) * 128], kr_rot]
    return jnp.concatenate(qs, axis=1), jnp.concatenate(ks, axis=1), kvraw[:, 512:]


def _flash_tile(s):
    return 512 if s >= 2048 else 128


FLASH_SCALE = (MLA_NOPE + MLA_ROPE) ** -0.5
LOG2_E = 1.4426950408889634
FLASH_EXP2 = FLASH_SCALE * LOG2_E
STAT_ROWS = 8
FLASH_PAIR = 2
FLASH_STRIP = 32


def _as_rows(col_b):
    ones = jnp.full((STAT_ROWS, LANES), 1.0 / LANES, F32)
    return _dot_raw(ones, col_b, "nt", True)


def _flash_fwd(qf, kf, vf):
    s = qf.shape[0]
    t = _flash_tile(s)
    nb = s // t
    pair = range(FLASH_PAIR)

    ck = min(FLASH_STRIP, t)

    def body(q_ref, k_ref, v_ref, o_ref, lse_ref, m_sc, l_sc, acc_sc, st_sc, pt_sc):
        i = pl.program_id(1)
        m_sc[...] = jnp.full_like(m_sc, -1e30)
        l_sc[...] = jnp.zeros_like(l_sc)
        acc_sc[...] = jnp.zeros_like(acc_sc)
        qs = [q_ref[:, hh * 256:(hh + 1) * 256] for hh in pair]

        def step(j, on_diagonal):
            rows = pl.ds(pl.multiple_of(j * t, t), t)
            for hh in pair:
                st = _dot_raw(k_ref[rows, hh * 256:(hh + 1) * 256], qs[hh], "nt", False)
                if on_diagonal:
                    keep = lax.broadcasted_iota(jnp.int32, (t, t), 0) <= lax.broadcasted_iota(jnp.int32, (t, t), 1)
                    st = jnp.where(keep, st, -1e30)
                st_sc[hh] = st
            m_olds = [m_sc[hh] for hh in pair]
            m_news = [jnp.maximum(m_olds[hh], jnp.max(st_sc[hh], axis=0, keepdims=True)) for hh in pair]
            alphas = [jnp.exp2((m_olds[hh] - m_news[hh]) * FLASH_EXP2) for hh in pair]
            shifts = [m_news[hh] * FLASH_EXP2 for hh in pair]
            sums = [jnp.zeros((8, t), F32) for _ in pair]
            for hh in pair:
                for r in range(0, t, ck):
                    p = jnp.exp2(st_sc[hh, r:r + ck, :] * FLASH_EXP2 - shifts[hh])
                    pt_sc[hh, r:r + ck, :] = p.astype(BF16)
                    sums[hh] = sums[hh] + functools.reduce(lambda a, b: a + b, [p[u:u + 8] for u in range(0, ck, 8)])
            pvs = [_dot_raw(v_ref[rows, hh * 128:(hh + 1) * 128], pt_sc[hh], "tn", False) for hh in pair]
            for hh in pair:
                l_sc[hh] = alphas[hh] * l_sc[hh] + jnp.sum(sums[hh], axis=0, keepdims=True)
                acc_sc[hh] = alphas[hh] * acc_sc[hh] + pvs[hh]
                m_sc[hh] = m_news[hh]

        def two_steps(p, carry):
            step(2 * p, False)
            step(2 * p + 1, False)
            return carry

        lax.fori_loop(0, i // 2, two_steps, 0)

        @pl.when(i % 2 == 1)
        def _():
            step(i - 1, False)

        step(i, True)
        for hh in pair:
            o_ref[:, hh * 128:(hh + 1) * 128] = (acc_sc[hh] / l_sc[hh]).T
            lse2 = m_sc[hh] * FLASH_EXP2 + jnp.log(l_sc[hh]) * LOG2_E
            lse_ref[hh] = jnp.broadcast_to(lse2, (STAT_ROWS, t))

    p = FLASH_PAIR
    return pl.pallas_call(
        body, name="flash_fwd", grid=(MLA_HEADS // p, nb),
        in_specs=[pl.BlockSpec((t, p * 256), lambda h, i: (i, h)), pl.BlockSpec((s, p * 256), lambda h, i: (0, h)),
                  pl.BlockSpec((s, p * 128), lambda h, i: (0, h))],
        out_specs=[pl.BlockSpec((t, p * 128), lambda h, i: (i, h)),
                   pl.BlockSpec((p, STAT_ROWS, t), lambda h, i: (h, 0, i))],
        out_shape=[jax.ShapeDtypeStruct((s, MLA_WIDTH), F32),
                   jax.ShapeDtypeStruct((MLA_HEADS, STAT_ROWS, s), F32)],
        scratch_shapes=[pltpu.VMEM((p, 1, t), F32), pltpu.VMEM((p, 1, t), F32), pltpu.VMEM((p, 128, t), F32),
                        pltpu.VMEM((p, t, t), F32), pltpu.VMEM((p, t, t), BF16)],
        compiler_params=_cparams(("parallel", "arbitrary")))(qf, kf, vf)


def _flash_bwd(qf, kf, vf, lse, o, do):
    s = qf.shape[0]
    t = _flash_tile(s)
    nb = s // t

    def body(q_ref, k_ref, v_ref, lse_ref, o_ref, do_ref, dq_ref, dk_ref, dv_ref, dl_sc, do16_sc):
        j = pl.program_id(1)

        @pl.when(j == 0)
        def _():
            dq_ref[...] = jnp.zeros_like(dq_ref)

            def stage(b, carry):
                rows = pl.ds(pl.multiple_of(b * t, t), t)
                do_t = do_ref[rows, :]
                delta = jnp.sum(do_t * o_ref[rows, :], axis=1, keepdims=True)
                dl_sc[:, rows] = _as_rows(jnp.broadcast_to(delta, (t, LANES)))
                do16_sc[rows, :] = do_t.astype(BF16)
                return carry

            lax.fori_loop(0, nb, stage, 0, unroll=min(4, nb))

        dk_ref[...] = jnp.zeros_like(dk_ref)
        dv_ref[...] = jnp.zeros_like(dv_ref)
        k = k_ref[...]
        v = v_ref[...]

        def step(i, on_diagonal):
            rows = pl.ds(pl.multiple_of(i * t, t), t)
            q = q_ref[rows, :]
            do_t = do16_sc[rows, :]
            st = _dot_raw(k, q, "nt", False) * FLASH_EXP2 - lse_ref[0, 0:1, rows]
            if on_diagonal:
                keep = lax.broadcasted_iota(jnp.int32, (t, t), 0) <= lax.broadcasted_iota(jnp.int32, (t, t), 1)
                st = jnp.where(keep, st, -1e30)
            pt = jnp.exp2(st)
            dst = pt * (_dot_raw(v, do_t, "nt", False) - dl_sc[0:1, rows])
            dv_ref[...] += _dot_raw(pt, do_t, "nn", False)
            dk_ref[...] += _dot_raw(dst, q, "nn", False)
            dq_ref[rows, :] += _dot_raw(dst, k, "tn", False)

        def two_steps(p, carry):
            step(j + 1 + 2 * p, False)
            step(j + 2 + 2 * p, False)
            return carry

        step(j, True)
        below = nb - 1 - j
        lax.fori_loop(0, below // 2, two_steps, 0)

        @pl.when(below % 2 == 1)
        def _():
            step(nb - 1, False)

        dk_ref[...] *= FLASH_SCALE

        @pl.when(j == nb - 1)
        def _():
            dq_ref[...] *= FLASH_SCALE

    per_head = lambda d: pl.BlockSpec((s, d), lambda h, j: (0, h), pipeline_mode=pl.Buffered(1))
    return pl.pallas_call(
        body, name="flash_bwd", grid=(MLA_HEADS, nb),
        in_specs=[per_head(256), pl.BlockSpec((t, 256), lambda h, j: (j, h)),
                  pl.BlockSpec((t, 128), lambda h, j: (j, h)),
                  pl.BlockSpec((1, STAT_ROWS, s), lambda h, j: (h, 0, 0)), per_head(128), per_head(128)],
        out_specs=[pl.BlockSpec((s, 256), lambda h, j: (0, h)), pl.BlockSpec((t, 256), lambda h, j: (j, h)),
                   pl.BlockSpec((t, 128), lambda h, j: (j, h))],
        out_shape=[jax.ShapeDtypeStruct((s, 1024), F32), jax.ShapeDtypeStruct((s, 1024), F32),
                   jax.ShapeDtypeStruct((s, MLA_WIDTH), F32)],
        scratch_shapes=[pltpu.VMEM((STAT_ROWS, s), F32), pltpu.VMEM((s, 128), BF16)],
        compiler_params=_cparams(("parallel", "arbitrary")))(qf, kf, vf, lse, o, do)


MLA_FRONT_TILE = 512


def _mla_front_fn(q_lat, kv_lat, kr, t1, t2, q_norm_w, q_up, kv_norm_w, kv_up):
    qraw = bdot(_rms(q_lat, q_norm_w), q_up)
    kvraw = bdot(_rms(kv_lat, kv_norm_w), kv_up)
    return _mla_prep_fn(qraw, kvraw, kr, t1, t2)


def _mla_front(rows, params):
    return _rw_fwd(_mla_front_fn, "mla_front", rows, params, (1024, 1024, 512), (BF16, BF16, BF16), (),
                   MLA_FRONT_TILE)


@jax.custom_vjp
def mla_attention(rows, params):
    return _flash_fwd(*_mla_front(rows, params))[0]


def _mla_attention_fwd(rows, params):
    qf, kf, vf = _mla_front(rows, params)
    o, lse = _flash_fwd(qf, kf, vf)
    return o, (rows, params, qf, kf, vf, o, lse)


def _mla_attention_bwd(res, do):
    rows, params, qf, kf, vf, o, lse = res
    d_rows, d_params = _rw_bwd(_mla_front_fn, "mla_front_bwd", rows, params, _flash_bwd(qf, kf, vf, lse, o, do), (),
                               3, MLA_FRONT_TILE)
    return d_rows + (jnp.zeros_like(rows[3]), jnp.zeros_like(rows[4])), d_params


mla_attention.defvjp(_mla_attention_fwd, _mla_attention_bwd)


def _lane_pick(x, lane):
    ids = lax.broadcasted_iota(jnp.int32, x.shape, 1)
    col = jnp.sum(jnp.where(ids == lane, x, 0.0), axis=1, keepdims=True)
    return jnp.broadcast_to(col, x.shape)


GDN_HALO = 8


@functools.partial(jax.custom_vjp, nondiff_argnums=(1,))
def _roll_rows(x, d):
    return pltpu.roll(x, d, 0)


_roll_rows.defvjp(lambda x, d: (pltpu.roll(x, d, 0), None), lambda d, _, g: (pltpu.roll(g, g.shape[0] - d, 0),))


def _gdn_prep_fn(prev, cur, ab, w0, w1, w2, w3, a_log, dt_bias):
    xcat = jnp.concatenate([prev, cur], axis=0)
    x0, x1, x2 = [_roll_rows(xcat, GDN_CONV - 1 - j)[GDN_HALO:] for j in range(GDN_CONV - 1)]
    qkv = _silu(x0 * w0 + x1 * w1 + x2 * w2 + cur * w3)
    g_all = -jnp.exp(a_log) * _softplus(ab + dt_bias)
    beta_all = _sigmoid(ab)
    qs, ks, gs, bs = [], [], [], []
    for h in range(GDN_HEADS):
        q = qkv[:, h * 128:(h + 1) * 128]
        k = qkv[:, 512 + h * 128:512 + (h + 1) * 128]
        qs.append(q * lax.rsqrt(jnp.sum(q * q, axis=-1, keepdims=True) + NORM_EPS) * (GDN_DK ** -0.5))
        ks.append(k * lax.rsqrt(jnp.sum(k * k, axis=-1, keepdims=True) + NORM_EPS))
        gs.append(_lane_pick(g_all, h))
        bs.append(_lane_pick(beta_all, GDN_HEADS + h))
    cat = lambda xs: jnp.concatenate(xs, axis=1)
    return cat(qs), cat(ks), qkv[:, 1024:], cat(gs), cat(bs)


GDN_PREP_TILE = 256


def _gdn_prep_specs(s, params, reverse=False):
    t = min(GDN_PREP_TILE, s)
    n = s // t
    blk = (lambda i: n - 1 - i) if reverse else (lambda i: i)
    prev = pl.BlockSpec((GDN_HALO, GDN_QKV), lambda i: (jnp.maximum(blk(i) * (t // GDN_HALO) - 1, 0), 0))
    rows = lambda d: pl.BlockSpec((t, d), lambda i: (blk(i), 0))
    return t, rows, [prev, rows(GDN_QKV), rows(LANES)] + [pl.BlockSpec(p.shape, lambda i: (0, 0)) for p in params]


def _gdn_prep_masked(has_rows_before):
    return lambda prev, *rest: _gdn_prep_fn(prev * has_rows_before, *rest)


def _gdn_prep_fwd(qkv, ab, params):
    s = qkv.shape[0]
    t, rows, in_specs = _gdn_prep_specs(s, params)

    def body(prev_ref, cur_ref, ab_ref, *refs):
        p_refs, o_refs = refs[:len(params)], refs[len(params):]
        has_rows_before = (pl.program_id(0) > 0).astype(F32)
        outs = _gdn_prep_masked(has_rows_before)(prev_ref[...], cur_ref[...], ab_ref[...], *[p[...] for p in p_refs])
        for ref, val in zip(o_refs, outs):
            ref[...] = val

    return pl.pallas_call(
        body, name="gdn_prep", grid=(s // t,), in_specs=in_specs,
        out_specs=[rows(GDN_WIDTH)] * 5, out_shape=[jax.ShapeDtypeStruct((s, GDN_WIDTH), F32)] * 5,
        compiler_params=_cparams(("parallel",)))(qkv, qkv, ab, *params)


def _gdn_prep_bwd(qkv, ab, params, cts):
    s = qkv.shape[0]
    t, rows, in_specs = _gdn_prep_specs(s, params, reverse=True)
    n = s // t
    npar = len(params)

    def body(prev_ref, cur_ref, ab_ref, *refs):
        p_refs, g_refs = refs[:npar], refs[npar:npar + 5]
        dcur_ref, dab_ref = refs[npar + 5:npar + 7]
        dp_refs, carry_sc = refs[npar + 7:-1], refs[-1]

        @pl.when(pl.program_id(0) == 0)
        def _():
            carry_sc[...] = jnp.zeros_like(carry_sc)
            for ref in dp_refs:
                ref[...] = jnp.zeros_like(ref)

        has_rows_before = (pl.program_id(0) < n - 1).astype(F32)
        _, vjp = jax.vjp(_gdn_prep_masked(has_rows_before), prev_ref[...], cur_ref[...], ab_ref[...],
                         *[p[...] for p in p_refs])
        d_prev, d_cur, d_ab, *d_params = vjp(tuple(g[...] for g in g_refs))
        dcur_ref[...] = d_cur
        dcur_ref[t - GDN_HALO:, :] += carry_sc[...]
        carry_sc[...] = d_prev
        dab_ref[...] = d_ab
        for ref, val in zip(dp_refs, d_params):
            ref[...] += val

    res = pl.pallas_call(
        body, name="gdn_prep_bwd", grid=(n,), in_specs=in_specs + [rows(GDN_WIDTH)] * 5,
        out_specs=[rows(GDN_QKV), rows(LANES)] + [pl.BlockSpec(p.shape, lambda i: (0, 0)) for p in params],
        out_shape=[jax.ShapeDtypeStruct((s, GDN_QKV), F32), jax.ShapeDtypeStruct((s, LANES), F32)]
        + [jax.ShapeDtypeStruct(p.shape, F32) for p in params],
        scratch_shapes=[pltpu.VMEM((GDN_HALO, GDN_QKV), F32)],
        compiler_params=_cparams(("arbitrary",)))(qkv, qkv, ab, *params, *cts)
    return res[0], res[1], tuple(res[2:])


@jax.custom_vjp
def gdn_prep(qkv, ab, params):
    return tuple(_gdn_prep_fwd(qkv, ab, params))


def _gdn_prep_vjp_fwd(qkv, ab, params):
    return tuple(_gdn_prep_fwd(qkv, ab, params)), (qkv, ab, params)


def _gdn_prep_vjp_bwd(res, cts):
    qkv, ab, params = res
    return _gdn_prep_bwd(qkv, ab, params, cts)


gdn_prep.defvjp(_gdn_prep_vjp_fwd, _gdn_prep_vjp_bwd)


def _tri_dot(x, mode):
    c = x.shape[0]
    tri = (lax.broadcasted_iota(jnp.int32, (c, c), 0) >= lax.broadcasted_iota(jnp.int32, (c, c), 1)).astype(BF16)
    hi = x.astype(BF16)
    rest = x - hi.astype(F32)
    mid = rest.astype(BF16)
    lo = (rest - mid.astype(F32)).astype(BF16)
    dot = lambda part: lax.dot_general(tri, part, _DN[mode], preferred_element_type=F32)
    return dot(hi) + (dot(mid) + dot(lo))


@jax.custom_vjp
def _chunk_cumsum(x):
    return _tri_dot(x, "nn")


_chunk_cumsum.defvjp(lambda x: (_tri_dot(x, "nn"), None), lambda _, g: (_tri_dot(g, "tn"),))


@jax.custom_vjp
def _unit_lower_inverse(lms):
    c = lms[0].shape[0]
    row = lax.broadcasted_iota(jnp.int32, (c, c), 0)
    col = lax.broadcasted_iota(jnp.int32, (c, c), 1)
    ts = [(row == col).astype(F32) - jnp.where((row >> 1) == (col >> 1), lm, 0.0) for lm in lms]
    for level in range(1, int(math.log2(c))):
        below = ((row >> (level + 1)) == (col >> (level + 1))) & ((row >> level) != (col >> level))
        mids = [_dot_split(t, jnp.where(below, lm, 0.0)) for t, lm in zip(ts, lms)]
        ts = [t - _dot_split(mid, t) for t, mid in zip(ts, mids)]
    return tuple(ts)


def _uli_fwd(lms):
    ts = _unit_lower_inverse(lms)
    return ts, ts


def _uli_bwd(ts, gs):
    mids = [bdot(t, g, "tn") for t, g in zip(ts, gs)]
    return (tuple(-bdot(mid, t, "nt") for t, mid in zip(ts, mids)),)


_unit_lower_inverse.defvjp(_uli_fwd, _uli_bwd)


@jax.custom_vjp
def _known_inverse(lms, ts):
    return ts


_known_inverse.defvjp(lambda lms, ts: (ts, ts),
                      lambda ts, gs: (_uli_bwd(ts, gs)[0], tuple(jnp.zeros_like(t) for t in ts)))


GDN_STEP_CHUNKS = 4


def _gdn_chunk_fn(qs, ks, vs, gbs, bbs, s0s, known_ts=None):
    chains, n_heads = range(len(qs)), len(s0s)
    c = qs[0].shape[0]
    row = lax.broadcasted_iota(jnp.int32, (c, c), 0)
    col = lax.broadcasted_iota(jnp.int32, (c, c), 1)
    incl, strict = row >= col, row > col
    gc = [_chunk_cumsum(gbs[i]) for i in chains]
    decay = [jnp.exp(jnp.where(incl, gc[i] - gc[i].T, -1e30)) for i in chains]
    g_last = [jnp.sum(gbs[i], axis=0, keepdims=True) for i in chains]
    eg = [jnp.exp(gc[i]) for i in chains]
    kb = [ks[i] * bbs[i] for i in chains]
    lms = tuple(jnp.where(strict, bdot(kb[i], ks[i], "nt") * decay[i], 0.0) for i in chains)
    ts = _unit_lower_inverse(lms) if known_ts is None else _known_inverse(lms, known_ts)
    u = [bdot(ts[i], vs[i] * bbs[i]) for i in chains]
    w = [bdot(ts[i], kb[i] * eg[i]) for i in chains]
    qk = [bdot(qs[i], ks[i], "nt") * decay[i] for i in chains]
    q_dec = [qs[i] * eg[i] for i in chains]
    k_dec = [ks[i] * jnp.exp(g_last[i] - gc[i]) for i in chains]
    states, outs = list(s0s), []
    for first in range(0, len(qs), n_heads):
        here = range(first, first + n_heads)
        v_new = [u[i] - bdot(w[i], states[i - first]) for i in here]
        outs += [bdot(q_dec[i], states[i - first]) + bdot(qk[i], v_new[i - first]) for i in here]
        states = [states[i - first] * jnp.exp(g_last[i]) + bdot(k_dec[i], v_new[i - first], "tn") for i in here]
    return (tuple(outs), tuple(states)), ts


def _chain_tiles(ref, c):
    return tuple(ref[p * c:(p + 1) * c, h * 128:(h + 1) * 128]
                 for p in range(ref.shape[0] // c) for h in range(GDN_HEADS))


def _gdn_step(s):
    c = min(GDN_CHUNK, s)
    p = min(GDN_STEP_CHUNKS, s // c)
    return c, p, s // (c * p)


def _gdn_fwd(q, k, v, gb, bb):
    s = q.shape[0]
    c, p, steps = _gdn_step(s)

    def body(q_ref, k_ref, v_ref, g_ref, b_ref, o_ref, st_ref, inv_ref, s_sc):
        @pl.when(pl.program_id(0) == 0)
        def _():
            s_sc[...] = jnp.zeros_like(s_sc)

        s0s = tuple(s_sc[h] for h in range(GDN_HEADS))
        for h in range(GDN_HEADS):
            st_ref[h, 0] = s0s[h]
        (os, s1s), ts = _gdn_chunk_fn(*[_chain_tiles(ref, c) for ref in (q_ref, k_ref, v_ref, g_ref, b_ref)], s0s)
        for i in range(p * GDN_HEADS):
            cc, h = divmod(i, GDN_HEADS)
            o_ref[cc * c:(cc + 1) * c, h * 128:(h + 1) * 128] = os[i]
            inv_ref[h, cc] = ts[i]
        for h in range(GDN_HEADS):
            s_sc[h] = s1s[h]

    blk = pl.BlockSpec((p * c, GDN_WIDTH), lambda n: (n, 0))
    return pl.pallas_call(
        body, name="gdn_fwd", grid=(steps,), in_specs=[blk] * 5,
        out_specs=[blk, pl.BlockSpec((GDN_HEADS, 1, 128, 128), lambda n: (0, n, 0, 0)),
                   pl.BlockSpec((GDN_HEADS, p, c, c), lambda n: (0, n, 0, 0))],
        out_shape=[jax.ShapeDtypeStruct((s, GDN_WIDTH), F32), jax.ShapeDtypeStruct((GDN_HEADS, steps, 128, 128), F32),
                   jax.ShapeDtypeStruct((GDN_HEADS, steps * p, c, c), F32)],
        scratch_shapes=[pltpu.VMEM((GDN_HEADS, 128, 128), F32)],
        compiler_params=_cparams(("arbitrary",)))(q, k, v, gb, bb)


def _gdn_bwd(q, k, v, gb, bb, states, inverses, do):
    s = q.shape[0]
    c, p, steps = _gdn_step(s)

    def body(q_ref, k_ref, v_ref, g_ref, b_ref, st_ref, inv_ref, do_ref, dq_ref, dk_ref, dv_ref, dg_ref, db_ref, ds_sc):
        @pl.when(pl.program_id(0) == 0)
        def _():
            ds_sc[...] = jnp.zeros_like(ds_sc)

        s0s = tuple(st_ref[h, 0] for h in range(GDN_HEADS))
        ts = tuple(inv_ref[h, cc] for cc in range(p) for h in range(GDN_HEADS))
        chunks = lambda *args: _gdn_chunk_fn(*args, known_ts=ts)[0]
        _, vjp = jax.vjp(chunks, *[_chain_tiles(ref, c) for ref in (q_ref, k_ref, v_ref, g_ref, b_ref)], s0s)
        *d_tiles, ds0s = vjp((_chain_tiles(do_ref, c), tuple(ds_sc[h] for h in range(GDN_HEADS))))
        for i in range(p * GDN_HEADS):
            cc, h = divmod(i, GDN_HEADS)
            for ref, d in zip((dq_ref, dk_ref, dv_ref, dg_ref, db_ref), d_tiles):
                ref[cc * c:(cc + 1) * c, h * 128:(h + 1) * 128] = d[i]
        for h in range(GDN_HEADS):
            ds_sc[h] = ds0s[h]

    blk = pl.BlockSpec((p * c, GDN_WIDTH), lambda n: (steps - 1 - n, 0))
    return pl.pallas_call(
        body, name="gdn_bwd", grid=(steps,),
        in_specs=[blk] * 5 + [pl.BlockSpec((GDN_HEADS, 1, 128, 128), lambda n: (0, steps - 1 - n, 0, 0)),
                              pl.BlockSpec((GDN_HEADS, p, c, c), lambda n: (0, steps - 1 - n, 0, 0)), blk],
        out_specs=[blk] * 5, out_shape=[jax.ShapeDtypeStruct((s, GDN_WIDTH), F32)] * 5,
        scratch_shapes=[pltpu.VMEM((GDN_HEADS, 128, 128), F32)],
        compiler_params=_cparams(("arbitrary",)))(q, k, v, gb, bb, states, inverses, do)


@jax.custom_vjp
def gdn_core(q, k, v, gb, bb):
    return _gdn_fwd(q, k, v, gb, bb)[0]


def _gdn_core_fwd(q, k, v, gb, bb):
    o, states, inverses = _gdn_fwd(q, k, v, gb, bb)
    return o, (q, k, v, gb, bb, states, inverses)


def _gdn_core_bwd(res, do):
    return tuple(_gdn_bwd(*res, do))


gdn_core.defvjp(_gdn_core_fwd, _gdn_core_bwd)


def _cact_fn(c):
    return (_silu(c),)


def _prenorm_fn(x, w, scale_raw, scale_b, shift_raw, shift_b):
    return (_rms(x, w) * (1.0 + scale_raw + scale_b) + shift_raw + shift_b,)


def _tail_fn(x, o_mla, z_mla, o_gdn, z_gdn, o_norm_w, w_out, post_w, gate_raw, gate_b):
    y_mla = o_mla * _silu(z_mla)
    parts = [_rms(o_gdn[:, h * 128:(h + 1) * 128], o_norm_w) for h in range(GDN_HEADS)]
    y_gdn = jnp.concatenate(parts, axis=1) * _silu(z_gdn)
    y = bdot(y_mla, w_out[:MLA_WIDTH]) + bdot(y_gdn, w_out[MLA_WIDTH:])
    return (x + (gate_raw + gate_b) * _rms(y, post_w),)


def _loss_fn(y, tgt):
    err = y - tgt
    part = jnp.sum(0.5 * jnp.mean(err * err, axis=-1, keepdims=True), axis=0, keepdims=True)
    return (jnp.broadcast_to(part, (1, LANES)),)


def _front_fwd(x, norm_params, ws16):
    s = x.shape[0]
    t = min(512, s)

    def body(x_ref, *refs):
        p_refs, w_refs, o_refs = refs[:len(norm_params)], refs[len(norm_params):-len(ws16)], refs[-len(ws16):]
        (h,) = _prenorm_fn(x_ref[...], *[p[...] for p in p_refs])
        h16 = h.astype(BF16)
        for w_ref, o_ref in zip(w_refs, o_refs):
            o_ref[...] = _dot_raw(h16, w_ref[...], "nn", False)

    rows = lambda d: pl.BlockSpec((t, d), lambda i: (i, 0))
    res = pl.pallas_call(
        body, name="front", grid=(s // t,),
        in_specs=[rows(D_MODEL)] + [_resident(p.shape) for p in norm_params] + [_resident(w.shape) for w in ws16],
        out_specs=[rows(w.shape[1]) for w in ws16],
        out_shape=[jax.ShapeDtypeStruct((s, w.shape[1]), F32) for w in ws16],
        compiler_params=_cparams(("parallel",)))(x, *norm_params, *ws16)
    return tuple(res)


def _front_bwd(x, norm_params, ws16, dys, dx_skip):
    s = x.shape[0]
    t = min(512, s)
    npar, ng = len(norm_params), len(ws16)

    def body(x_ref, *refs):
        p_refs, w_refs, g_refs = refs[:npar], refs[npar:npar + ng], refs[npar + ng:npar + 2 * ng]
        skip_ref, dx_ref = refs[npar + 2 * ng], refs[npar + 2 * ng + 1]
        dp_refs, dw_refs = refs[npar + 2 * ng + 2:2 * npar + 2 * ng + 2], refs[2 * npar + 2 * ng + 2:]

        @pl.when(pl.program_id(0) == 0)
        def _():
            for ref in dp_refs + dw_refs:
                ref[...] = jnp.zeros_like(ref)

        (h,), vjp = jax.vjp(_prenorm_fn, x_ref[...], *[p[...] for p in p_refs])
        h16 = h.astype(BF16)
        dys16 = [g[...].astype(BF16) for g in g_refs]
        dh = functools.reduce(lambda a, b: a + b,
                              [_dot_raw(dy, w_ref[...], "nt", False) for dy, w_ref in zip(dys16, w_refs)])
        for dy, dw_ref in zip(dys16, dw_refs):
            dw_ref[...] += _dot_raw(h16, dy, "tn", False)
        dx, *d_params = vjp((dh,))
        dx_ref[...] = dx + skip_ref[...]
        for ref, val in zip(dp_refs, d_params):
            ref[...] += val

    rows = lambda d: pl.BlockSpec((t, d), lambda i: (i, 0))
    res = pl.pallas_call(
        body, name="front_bwd", grid=(s // t,),
        in_specs=[rows(D_MODEL)] + [_resident(p.shape) for p in norm_params] + [_resident(w.shape) for w in ws16]
        + [rows(w.shape[1]) for w in ws16] + [rows(D_MODEL)],
        out_specs=[rows(D_MODEL)] + [_resident(p.shape) for p in norm_params] + [_resident(w.shape) for w in ws16],
        out_shape=[jax.ShapeDtypeStruct(x.shape, F32)] + [jax.ShapeDtypeStruct(p.shape, F32) for p in norm_params]
        + [jax.ShapeDtypeStruct(w.shape, F32) for w in ws16],
        compiler_params=_cparams(("arbitrary",)))(x, *norm_params, *ws16, *dys, dx_skip)
    return res[0], tuple(res[1:1 + npar]), tuple(res[1 + npar:])


@jax.custom_vjp
def front(x, norm_params, ws):
    return _front_fwd(x, norm_params, tuple(w.astype(BF16) for w in ws)) + (x,)


def _front_vjp_fwd(x, norm_params, ws):
    ws16 = tuple(w.astype(BF16) for w in ws)
    return _front_fwd(x, norm_params, ws16) + (x,), (x, norm_params, ws16)


def _front_vjp_bwd(res, cts):
    return _front_bwd(*res, cts[:-1], cts[-1])


front.defvjp(_front_vjp_fwd, _front_vjp_bwd)


_OPS = dict(
    cact=make_rowwise(_cact_fn, "c_act", (D_MODEL,), tile=16),
    tail=make_rowwise(_tail_fn, "tail", (D_MODEL,), tile=512),
    loss=make_rowwise(_loss_fn, "loss", (), acc_dims=(LANES,), n_nondiff=1),
)


def _swap_halves(w):
    half = w.shape[-1] // 2
    return jnp.concatenate([w[..., half:], w[..., :half]], axis=-1)


def _w_in_groups(w):
    k_pe = w[:, 640:704]
    ab = jnp.concatenate([w[:, 2752:2760], jnp.zeros((w.shape[0], LANES - 8), w.dtype)], axis=1)
    return (w[:, :384], w[:, 384:640], jnp.concatenate([k_pe, _swap_halves(k_pe)], axis=1), w[:, 704:1216],
            w[:, 1216:2752], ab, w[:, 2760:])


def _q_up_ext(w):
    parts = []
    for h in range(MLA_HEADS):
        rope = w[:, h * 192 + 128:(h + 1) * 192]
        parts += [w[:, h * 192:h * 192 + 128], rope, _swap_halves(rope)]
    return jnp.concatenate(parts, axis=1)


def _kv_up_perm(w):
    ks = [w[:, h * 256:h * 256 + 128] for h in range(MLA_HEADS)]
    vs = [w[:, h * 256 + 128:(h + 1) * 256] for h in range(MLA_HEADS)]
    return jnp.concatenate(ks + vs, axis=1)


def _pad_lanes(v):
    return jnp.pad(v, (0, LANES - v.shape[0]))[None, :]


def _local_loss(weights, mods, x, positions, target):
    s = x.shape[0]
    half = MLA_ROPE // 2
    inv_freq = jnp.power(ROPE_THETA, -jnp.arange(half, dtype=F32) * 2.0 / MLA_ROPE)
    ang = positions.astype(F32)[:, None] * inv_freq
    cos, sin, zero = jnp.cos(ang), jnp.sin(ang), jnp.zeros((s, 2 * half), F32)
    t1 = jnp.concatenate([cos, cos, zero], axis=1)
    t2 = jnp.concatenate([-sin, sin, zero], axis=1)

    for l in range(DEPTH):
        mod = mods[l]
        b = weights["b_mod"][l][None, :]
        shift_raw, scale_raw, gate_raw = mod[:, :1024], mod[:, 1024:2048], mod[:, 2048:]
        shift_b, scale_b, gate_b = b[:, :1024], b[:, 1024:2048], b[:, 2048:]
        q_lat, kv_lat, kr, z_mla, qkv, ab, z_gdn, x = front(
            x, (weights["pre_norm_w"][l][None], scale_raw, scale_b, shift_raw, shift_b),
            _w_in_groups(weights["w_in"][l]))
        o_mla = mla_attention((q_lat, kv_lat, kr, t1, t2),
                              (weights["mla_q_norm_w"][l][None], _q_up_ext(weights["mla_q_up"][l]),
                               weights["mla_kv_norm_w"][l][None], _kv_up_perm(weights["mla_kv_up"][l])))
        cw = weights["gdn_conv_w"][l]
        params = tuple(cw[j][None] for j in range(GDN_CONV))
        params += (_pad_lanes(weights["gdn_a_log"][l]), _pad_lanes(weights["gdn_dt_bias"][l]))
        o_gdn = gdn_core(*gdn_prep(qkv, ab, params))
        (x,) = _OPS["tail"]((x, o_mla, z_mla, o_gdn, z_gdn),
                            (weights["gdn_o_norm_w"][l][None], weights["w_out"][l], weights["post_norm_w"][l][None],
                             gate_raw, gate_b))
    (acc,) = _OPS["loss"]((x, target), ())
    return acc[0, 0]


def _chip_index():
    return 2 * lax.axis_index("x") + lax.axis_index("y")


def _other_chips(x, y):
    return [(1 - x, y), (x, 1 - y), (1 - x, 1 - y)]


def _any_spec():
    return pl.BlockSpec(memory_space=pl.ANY)


def _half(ref, hc):
    n = ref.shape[0] // 2
    return ref.at[pl.ds(hc * n, n)]


def ag_weights(shards):
    n = len(shards)

    def body(*refs):
        ins, outs = refs[:n], refs[n:2 * n]
        send_sems, recv_sems = refs[2 * n:]
        x, y, c = lax.axis_index("x"), lax.axis_index("y"), lax.axis_index("c")
        sibling = (x, y, 1 - c)
        chips = _other_chips(x, y)

        def copy(t, k, src, chip_xy, hc, to):
            return pltpu.make_async_remote_copy(
                src_ref=src, dst_ref=_half(outs[t].at[2 * chip_xy[0] + chip_xy[1]], hc),
                send_sem=send_sems.at[6 * t + k], recv_sem=recv_sems.at[6 * t + k], device_id=to, device_id_type=MESH)

        first = [copy(t, k, _half(ins[t], c), (x, y), c, (*chip, c)) for k, chip in enumerate(chips) for t in range(n)]
        for cp in first:
            cp.start()
        passed = []
        for k, chip in enumerate(chips):
            for t in range(n):
                landed = _half(outs[t].at[2 * chip[0] + chip[1]], c)
                copy(t, k, landed, chip, c, (x, y, c)).wait_recv()
                passed.append(copy(t, 3 + k, landed, chip, c, sibling))
                passed[-1].start()
        for k, chip in enumerate(chips):
            for t in range(n):
                copy(t, 3 + k, _half(ins[t], c), chip, 1 - c, (x, y, c)).wait_recv()
        for cp in first + passed:
            cp.wait_send()

    return pl.pallas_call(
        body, name="ag_weights", in_specs=[_any_spec()] * n, out_specs=[_any_spec()] * n,
        out_shape=[jax.ShapeDtypeStruct((N_CHIPS,) + a.shape, a.dtype) for a in shards],
        scratch_shapes=[pltpu.SemaphoreType.DMA((6 * n,)), pltpu.SemaphoreType.DMA((6 * n,))],
        compiler_params=pltpu.CompilerParams(has_side_effects=True))(*shards)


def all_gather_rows(x):
    r, cols = x.shape
    flips = [(k >> 2 & 1, k >> 1 & 1, k & 1) for k in range(1, 8)]

    def body(x_ref, out_ref, send_sems, recv_sems):
        here = (lax.axis_index("x"), lax.axis_index("y"), lax.axis_index("c"))
        peers = [tuple(1 - p if f else p for p, f in zip(here, flip)) for flip in flips]
        slot = lambda dev: 4 * dev[0] + 2 * dev[1] + dev[2]
        out_ref[slot(here)] = x_ref[...]
        copies = [pltpu.make_async_remote_copy(
            src_ref=x_ref, dst_ref=out_ref.at[slot(here)], send_sem=send_sems.at[k], recv_sem=recv_sems.at[k],
            device_id=peer, device_id_type=MESH) for k, peer in enumerate(peers)]
        for cp in copies:
            cp.start()
        for k, peer in enumerate(peers):
            pltpu.make_async_remote_copy(
                src_ref=x_ref, dst_ref=out_ref.at[slot(peer)], send_sem=send_sems.at[k], recv_sem=recv_sems.at[k],
                device_id=peer, device_id_type=MESH).wait_recv()
        for cp in copies:
            cp.wait_send()

    return pl.pallas_call(
        body, name="all_gather_rows", in_specs=[pl.BlockSpec(memory_space=pltpu.VMEM)],
        out_specs=pl.BlockSpec(memory_space=pltpu.VMEM), out_shape=jax.ShapeDtypeStruct((8, r, cols), F32),
        scratch_shapes=[pltpu.SemaphoreType.DMA((7,)), pltpu.SemaphoreType.DMA((7,))],
        compiler_params=pltpu.CompilerParams(has_side_effects=True, vmem_limit_bytes=VMEM_LIMIT))(x)


def rs_pair(gs):
    n = len(gs)

    def body(*refs):
        ins, outs = refs[:n], refs[n:2 * n]
        send_sems, recv_sems = refs[2 * n:]
        x, y, c = lax.axis_index("x"), lax.axis_index("y"), lax.axis_index("c")
        lh = [g.shape[1] // 2 for g in gs]
        copies = [pltpu.make_async_remote_copy(
            src_ref=ins[t].at[:, pl.ds((1 - c) * lh[t], lh[t])], dst_ref=outs[t], send_sem=send_sems.at[t],
            recv_sem=recv_sems.at[t], device_id=(x, y, 1 - c), device_id_type=MESH) for t in range(n)]
        for cp in copies:
            cp.start()
        for cp in copies:
            cp.wait()

    return pl.pallas_call(
        body, name="rs_pair", in_specs=[_any_spec()] * n, out_specs=[_any_spec()] * n,
        out_shape=[jax.ShapeDtypeStruct((N_CHIPS, g.shape[1] // 2) + g.shape[2:], F32) for g in gs],
        scratch_shapes=[pltpu.SemaphoreType.DMA((n,)), pltpu.SemaphoreType.DMA((n,))],
        compiler_params=pltpu.CompilerParams(has_side_effects=True))(*gs)


def rs_cross(pairs):
    n = len(pairs)

    def body(*refs):
        ins, outs = refs[:n], refs[n:2 * n]
        send_sems, recv_sems = refs[2 * n:]
        x, y, c = lax.axis_index("x"), lax.axis_index("y"), lax.axis_index("c")
        copies = []
        for k, chip in enumerate(_other_chips(x, y)):
            for t in range(n):
                copies.append(pltpu.make_async_remote_copy(
                    src_ref=ins[t].at[2 * chip[0] + chip[1]], dst_ref=outs[t].at[k], send_sem=send_sems.at[3 * t + k],
                    recv_sem=recv_sems.at[3 * t + k], device_id=(*chip, c), device_id_type=MESH))
        for cp in copies:
            cp.start()
        for cp in copies:
            cp.wait()

    return pl.pallas_call(
        body, name="rs_cross", in_specs=[_any_spec()] * n, out_specs=[_any_spec()] * n,
        out_shape=[jax.ShapeDtypeStruct((3,) + p.shape[1:], p.dtype) for p in pairs],
        scratch_shapes=[pltpu.SemaphoreType.DMA((3 * n,)), pltpu.SemaphoreType.DMA((3 * n,))],
        compiler_params=pltpu.CompilerParams(has_side_effects=True))(*pairs)


def rs_share(blocks):
    n = len(blocks)

    def body(*refs):
        ins, outs = refs[:n], refs[n:2 * n]
        send_sems, recv_sems = refs[2 * n:]
        x, y, c = lax.axis_index("x"), lax.axis_index("y"), lax.axis_index("c")
        sends = [pltpu.make_async_remote_copy(
            src_ref=_half(ins[t], c), dst_ref=_half(outs[t], c), send_sem=send_sems.at[t], recv_sem=recv_sems.at[t],
            device_id=(x, y, 1 - c), device_id_type=MESH) for t in range(n)]
        for cp in sends:
            cp.start()
        for t in range(n):
            pltpu.make_async_remote_copy(
                src_ref=_half(ins[t], c), dst_ref=_half(outs[t], 1 - c), send_sem=send_sems.at[t],
                recv_sem=recv_sems.at[t], device_id=(x, y, 1 - c), device_id_type=MESH).wait_recv()
        for cp in sends:
            cp.wait_send()

    return pl.pallas_call(
        body, name="rs_share", in_specs=[_any_spec()] * n, out_specs=[_any_spec()] * n,
        out_shape=[jax.ShapeDtypeStruct(b.shape, F32) for b in blocks],
        input_output_aliases={t: t for t in range(n)},
        scratch_shapes=[pltpu.SemaphoreType.DMA((n,)), pltpu.SemaphoreType.DMA((n,))],
        compiler_params=pltpu.CompilerParams(has_side_effects=True))(*blocks)


def _row_tile(rows, target):
    best = None
    for t in range(8, min(rows, target) + 1, 8):
        if rows % t == 0:
            best = t
    return rows if best is None else best


TILE_BYTES = 2 * 1024 * 1024


def _flat_rows(shape):
    rows = int(np.prod(shape[1:-1]))
    cols_padded = -(-shape[-1] // LANES) * LANES
    return rows, _row_tile(rows, max(8, TILE_BYTES // (4 * cols_padded)))


def pair_add(g, from_sibling, out_dtype):
    cols = g.shape[-1]
    rph, t = _flat_rows(from_sibling.shape)
    nt = rph // t
    c_arr = lax.axis_index("c").astype(jnp.int32).reshape(1)

    def body(c_ref, a_ref, b_ref, o_ref):
        o_ref[...] = (a_ref[...] + b_ref[...]).astype(o_ref.dtype)

    out = pl.pallas_call(
        body, name="pair_add",
        grid_spec=pltpu.PrefetchScalarGridSpec(
            num_scalar_prefetch=1, grid=(N_CHIPS, nt),
            in_specs=[pl.BlockSpec((t, cols), lambda j, i, c_ref: (j * 2 * nt + c_ref[0] * nt + i, 0)),
                      pl.BlockSpec((t, cols), lambda j, i, c_ref: (j * nt + i, 0))],
            out_specs=pl.BlockSpec((t, cols), lambda j, i, c_ref: (j * nt + i, 0))),
        out_shape=jax.ShapeDtypeStruct((N_CHIPS * rph, cols), out_dtype),
        compiler_params=_cparams(("parallel", "parallel")))(c_arr, g.reshape(-1, cols), from_sibling.reshape(-1, cols))
    return out.reshape(from_sibling.shape)


def chip_add(pairs, received):
    cols = pairs.shape[-1]
    rph, t = _flat_rows(pairs.shape)
    nt = rph // t
    j_arr = _chip_index().astype(jnp.int32).reshape(1)
    c_arr = lax.axis_index("c").astype(jnp.int32).reshape(1)
    r2 = received.reshape(-1, cols)

    def body(j_ref, c_ref, a_ref, r0_ref, r1_ref, r2_ref, o_ref):
        a, r0, r1, r2 = [ref[...].astype(F32) for ref in (a_ref, r0_ref, r1_ref, r2_ref)]
        o_ref[...] = (a + r0) + (r1 + r2)

    out = pl.pallas_call(
        body, name="chip_add",
        grid_spec=pltpu.PrefetchScalarGridSpec(
            num_scalar_prefetch=2, grid=(nt,),
            in_specs=[pl.BlockSpec((t, cols), lambda i, j_ref, c_ref: (j_ref[0] * nt + i, 0)),
                      pl.BlockSpec((t, cols), lambda i, j_ref, c_ref: (i, 0)),
                      pl.BlockSpec((t, cols), lambda i, j_ref, c_ref: (nt + i, 0)),
                      pl.BlockSpec((t, cols), lambda i, j_ref, c_ref: (2 * nt + i, 0))],
            out_specs=pl.BlockSpec((t, cols), lambda i, j_ref, c_ref: (c_ref[0] * nt + i, 0))),
        out_shape=jax.ShapeDtypeStruct((2 * rph, cols), F32),
        compiler_params=_cparams(("parallel",)))(j_arr, c_arr, pairs.reshape(-1, cols), r2, r2, r2)
    return out.reshape((2 * pairs.shape[1],) + pairs.shape[2:])


def reduce_grads(gs, cross_dtypes):
    pairs = [pair_add(g, r, dt) for g, r, dt in zip(gs, rs_pair(gs), cross_dtypes)]
    return rs_share([chip_add(p, r) for p, r in zip(pairs, rs_cross(pairs))])


def adamw(w, g, m, v):
    shape = w.shape
    cols = shape[-1]
    rows = int(np.prod(shape[:-1]))
    flat = lambda a: a.reshape(rows, cols)
    t = _row_tile(rows, 256)

    def body(w_ref, g_ref, m_ref, v_ref, d_ref, mo_ref, vo_ref):
        gv = g_ref[...]
        m_new = ADAM_B1 * m_ref[...] + (1.0 - ADAM_B1) * gv
        v_new = ADAM_B2 * v_ref[...] + (1.0 - ADAM_B2) * (gv * gv)
        m_hat = m_new / (1.0 - ADAM_B1 ** ADAM_STEP)
        v_hat = v_new / (1.0 - ADAM_B2 ** ADAM_STEP)
        d_ref[...] = -ADAM_LR * (m_hat / (jnp.sqrt(v_hat) + ADAM_EPS) + ADAM_WD * w_ref[...])
        mo_ref[...] = m_new
        vo_ref[...] = v_new

    spec = pl.BlockSpec((t, cols), lambda i: (i, 0))
    outs = pl.pallas_call(
        body, name="adamw", grid=(rows // t,), in_specs=[spec] * 4, out_specs=[spec] * 3,
        out_shape=[jax.ShapeDtypeStruct((rows, cols), F32)] * 3,
        compiler_params=_cparams(("parallel",)))(flat(w), flat(g), flat(m), flat(v))
    return tuple(o.reshape(shape) for o in outs)


SHARDED = (("w_in", 2), ("mla_q_up", 2), ("mla_kv_up", 2), ("gdn_conv_w", 2), ("w_out", 1))
REPLICATED = ("b_mod", "pre_norm_w", "post_norm_w", "mla_q_norm_w", "mla_kv_norm_w", "gdn_a_log", "gdn_dt_bias",
              "gdn_o_norm_w")
WEIGHT_ORDER = ("w_mod", "b_mod", "pre_norm_w", "post_norm_w", "w_in", "mla_q_norm_w", "mla_q_up", "mla_kv_norm_w",
                "mla_kv_up", "gdn_conv_w", "gdn_a_log", "gdn_dt_bias", "gdn_o_norm_w", "w_out")
EXACT_F32 = ("gdn_conv_w",)
SMALL_ROWS = 48


def _gather_weights(shards):
    names = [name for name, _ in SHARDED]
    own = [shards[n] if n in EXACT_F32 else shards[n].astype(BF16) for n in names]
    gathered = ag_weights(own)
    full = {}
    for (name, axis), blk, mine in zip(SHARDED, gathered, own):
        shp = shards[name].shape
        blk = lax.dynamic_update_index_in_dim(blk, mine, _chip_index(), 0)
        blk = jnp.moveaxis(blk.astype(F32), 0, axis)
        full[name] = blk.reshape(shp[:axis] + (N_CHIPS * shp[axis],) + shp[axis + 1:])
    return full


def _split_grads(grads):
    pieces = []
    for name, axis in SHARDED:
        g = grads[name]
        shp = g.shape
        g = g.reshape(shp[:axis] + (N_CHIPS, shp[axis] // N_CHIPS) + shp[axis + 1:])
        pieces.append(jnp.moveaxis(g, axis, 0))
    small = jnp.concatenate([grads[name] for name in REPLICATED], axis=1)
    small = jnp.pad(small, ((0, 0), (0, SMALL_ROWS * LANES - small.shape[1]))).reshape(DEPTH, SMALL_ROWS, LANES)
    pieces.append(jnp.broadcast_to(small[None], (N_CHIPS,) + small.shape))
    return pieces


def _unsplit_small(small, rep_shapes):
    flat = small.reshape(DEPTH, SMALL_ROWS * LANES)
    out, off = {}, 0
    for name in REPLICATED:
        size = rep_shapes[name][1]
        out[name] = flat[:, off:off + size]
        off += size
    return out


MOD_ROWS = 16


def _device_slot():
    return 4 * lax.axis_index("x") + 2 * lax.axis_index("y") + lax.axis_index("c")


def _adaln_projection(c, w_mod_shard):
    (c_act,) = _OPS["cact"]((jnp.pad(c, ((0, 7), (0, 0))),), ())
    c_acts = jnp.pad(all_gather_rows(c_act)[:, 0, :], ((0, MOD_ROWS - 8), (0, 0)))
    part = jnp.concatenate([_mm(c_acts, w_mod_shard[l].astype(BF16), "nn", "mod_proj") for l in range(DEPTH)], axis=0)
    parts = all_gather_rows(part)[::2].reshape(N_CHIPS, DEPTH, MOD_ROWS, -1)
    mods = jnp.moveaxis(parts, 0, 2).reshape(DEPTH, MOD_ROWS, -1)
    return c_acts, lax.dynamic_slice_in_dim(mods, _device_slot(), 1, axis=1)


def _adaln_weight_grad(c_acts, d_mods):
    cols = d_mods.shape[-1] // N_CHIPS
    rows = jnp.pad(d_mods[:, 0, :], ((0, 8 - DEPTH), (0, 0)))
    all_rows = all_gather_rows(rows)[:, :DEPTH, :]
    mine = lax.dynamic_slice_in_dim(all_rows, _chip_index() * cols, cols, axis=2)
    mine = jnp.pad(mine, ((0, MOD_ROWS - 8), (0, 0), (0, 0)))
    return jnp.stack([_mm(c_acts, mine[:, l, :], "tn", "mod_dw") for l in range(DEPTH)])


def kernel(x, c, positions, w_mod, b_mod, pre_norm_w, post_norm_w, w_in, mla_q_norm_w, mla_q_up, mla_kv_norm_w, mla_kv_up, gdn_conv_w, gdn_a_log, gdn_dt_bias, gdn_o_norm_w, w_out, loss_target, m_w_mod, m_b_mod, m_pre_norm_w, m_post_norm_w, m_w_in, m_mla_q_norm_w, m_mla_q_up, m_mla_kv_norm_w, m_mla_kv_up, m_gdn_conv_w, m_gdn_a_log, m_gdn_dt_bias, m_gdn_o_norm_w, m_w_out, v_w_mod, v_b_mod, v_pre_norm_w, v_post_norm_w, v_w_in, v_mla_q_norm_w, v_mla_q_up, v_mla_kv_norm_w, v_mla_kv_up, v_gdn_conv_w, v_gdn_a_log, v_gdn_dt_bias, v_gdn_o_norm_w, v_w_out):
    given = dict(w_mod=w_mod, b_mod=b_mod, pre_norm_w=pre_norm_w, post_norm_w=post_norm_w, w_in=w_in,
                 mla_q_norm_w=mla_q_norm_w, mla_q_up=mla_q_up, mla_kv_norm_w=mla_kv_norm_w, mla_kv_up=mla_kv_up,
                 gdn_conv_w=gdn_conv_w, gdn_a_log=gdn_a_log, gdn_dt_bias=gdn_dt_bias, gdn_o_norm_w=gdn_o_norm_w,
                 w_out=w_out)
    moments_m = dict(w_mod=m_w_mod, b_mod=m_b_mod, pre_norm_w=m_pre_norm_w, post_norm_w=m_post_norm_w, w_in=m_w_in,
                     mla_q_norm_w=m_mla_q_norm_w, mla_q_up=m_mla_q_up, mla_kv_norm_w=m_mla_kv_norm_w,
                     mla_kv_up=m_mla_kv_up, gdn_conv_w=m_gdn_conv_w, gdn_a_log=m_gdn_a_log,
                     gdn_dt_bias=m_gdn_dt_bias, gdn_o_norm_w=m_gdn_o_norm_w, w_out=m_w_out)
    moments_v = dict(w_mod=v_w_mod, b_mod=v_b_mod, pre_norm_w=v_pre_norm_w, post_norm_w=v_post_norm_w, w_in=v_w_in,
                     mla_q_norm_w=v_mla_q_norm_w, mla_q_up=v_mla_q_up, mla_kv_norm_w=v_mla_kv_norm_w,
                     mla_kv_up=v_mla_kv_up, gdn_conv_w=v_gdn_conv_w, gdn_a_log=v_gdn_a_log,
                     gdn_dt_bias=v_gdn_dt_bias, gdn_o_norm_w=v_gdn_o_norm_w, w_out=v_w_out)

    full = _gather_weights({name: given[name] for name, _ in SHARDED})
    for name in REPLICATED:
        full[name] = given[name]
    c_acts, mods = _adaln_projection(c, w_mod)
    loss_local, (grads, d_mods, grad_x) = jax.value_and_grad(_local_loss, argnums=(0, 1, 2))(
        full, mods, x[0], positions[0], loss_target[0])
    loss = lax.psum(loss_local, AXES)

    cross_dtypes = [F32 if name in EXACT_F32 else BF16 for name, _ in SHARDED] + [F32]
    reduced = reduce_grads(_split_grads(grads), cross_dtypes)
    grad_w = {name: g for (name, _), g in zip(SHARDED, reduced)}
    grad_w.update(_unsplit_small(reduced[-1], {name: given[name].shape for name in REPLICATED}))
    grad_w["w_mod"] = _adaln_weight_grad(c_acts, d_mods)
    delta, new_m, new_v = {}, {}, {}
    for name in WEIGHT_ORDER:
        delta[name], new_m[name], new_v[name] = adamw(given[name], grad_w[name], moments_m[name], moments_v[name])
    return (loss, grad_x[None], *[grad_w[n] for n in WEIGHT_ORDER], *[delta[n] for n in WEIGHT_ORDER],
            *[new_m[n] for n in WEIGHT_ORDER], *[new_v[n] for n in WEIGHT_ORDER])
```

```python
import functools
import math

import numpy as np
import jax
import jax.numpy as jnp
from jax import lax
from jax.experimental import pallas as pl
from jax.experimental.pallas import tpu as pltpu

F32 = jnp.float32
BF16 = jnp.bfloat16
MESH = pl.DeviceIdType.MESH
AXES = ("x", "y", "c")

D_MODEL = 1024
DEPTH = 4
MLA_HEADS = 4
MLA_NOPE = 128
MLA_ROPE = 64
MLA_WIDTH = 512
GDN_HEADS = 4
GDN_DK = 128
GDN_WIDTH = 512
GDN_QKV = 1536
GDN_CONV = 4
ROPE_THETA = 10000.0
NORM_EPS = 1e-6
ADAM_LR, ADAM_B1, ADAM_B2, ADAM_EPS, ADAM_WD, ADAM_STEP = 0.001, 0.9, 0.999, 1e-08, 0.01, 10

LANES = 128
N_CHIPS = 4
GDN_CHUNK = 128
VMEM_LIMIT = 56 * 1024 * 1024


def _cparams(sem=None):
    if sem is None:
        return pltpu.CompilerParams(vmem_limit_bytes=VMEM_LIMIT)
    return pltpu.CompilerParams(dimension_semantics=sem, vmem_limit_bytes=VMEM_LIMIT)


def _pick(dim, target):
    if dim <= target:
        return dim
    best = None
    for t in range(LANES, target + 1, LANES):
        if dim % t == 0:
            best = t
    assert best is not None, (dim, target)
    return best


def _resident(shape):
    return pl.BlockSpec(shape, lambda i: (0,) * len(shape), pipeline_mode=pl.Buffered(1))


_DN = {"nn": (((1,), (0,)), ((), ())), "nt": (((1,), (1,)), ((), ())), "tn": (((0,), (0,)), ((), ()))}


def _dot_raw(a, b, mode, exact):
    if exact:
        return lax.dot_general(a, b, _DN[mode], precision=lax.Precision.HIGHEST, preferred_element_type=F32)
    return lax.dot_general(a.astype(BF16), b.astype(BF16), _DN[mode], preferred_element_type=F32)


def _dot_split(a, b):
    a_hi, b_hi = a.astype(BF16), b.astype(BF16)
    a_lo, b_lo = (a - a_hi.astype(F32)).astype(BF16), (b - b_hi.astype(F32)).astype(BF16)
    dot = lambda u, w: lax.dot_general(u, w, _DN["nn"], preferred_element_type=F32)
    return dot(a_hi, b_hi) + (dot(a_hi, b_lo) + dot(a_lo, b_hi))


@functools.partial(jax.custom_vjp, nondiff_argnums=(2, 3))
def bdot(a, b, mode="nn", exact=False):
    return _dot_raw(a, b, mode, exact)


def _bdot_fwd(a, b, mode, exact):
    return _dot_raw(a, b, mode, exact), (a, b)


def _bdot_bwd(mode, exact, res, g):
    a, b = res
    if mode == "nn":
        return bdot(g, b, "nt", exact), bdot(a, g, "tn", exact)
    if mode == "nt":
        return bdot(g, b, "nn", exact), bdot(g, a, "tn", exact)
    return bdot(b, g, "nt", exact), bdot(a, g, "nn", exact)


bdot.defvjp(_bdot_fwd, _bdot_bwd)


@jax.custom_vjp
def roll_half(x):
    return pltpu.roll(x, 64, 1)


roll_half.defvjp(lambda x: (pltpu.roll(x, 64, 1), None), lambda _, g: (pltpu.roll(g, 64, 1),))


def _sigmoid(x):
    return 1.0 / (1.0 + jnp.exp(-x))


def _silu(x):
    return x * _sigmoid(x)


def _softplus(x):
    return jnp.maximum(x, 0.0) + jnp.log(1.0 + jnp.exp(-jnp.abs(x)))


def _rms(x, w):
    return x * lax.rsqrt(jnp.mean(x * x, axis=-1, keepdims=True) + NORM_EPS) * w


def _rw_fwd(fn, name, rows, params, out_dims, out_dtypes, acc_dims, tile):
    s = rows[0].shape[0]
    t = min(tile, s)
    n = s // t
    nr, npar, no, na = len(rows), len(params), len(out_dims), len(acc_dims)

    def body(*refs):
        r, p = refs[:nr], refs[nr:nr + npar]
        o, a = refs[nr + npar:nr + npar + no], refs[nr + npar + no:]
        outs = fn(*[x[...] for x in r], *[x[...] for x in p])
        for ref, val in zip(o, outs[:no]):
            ref[...] = val.astype(ref.dtype)
        if na:
            @pl.when(pl.program_id(0) == 0)
            def _():
                for ref in a:
                    ref[...] = jnp.zeros_like(ref)
            for ref, val in zip(a, outs[no:]):
                ref[...] += val

    in_specs = [pl.BlockSpec((t, x.shape[1]), lambda i: (i, 0)) for x in rows]
    in_specs += [_resident(x.shape) for x in params]
    out_specs = [pl.BlockSpec((t, d), lambda i: (i, 0)) for d in out_dims]
    out_specs += [_resident((1, d)) for d in acc_dims]
    out_shape = [jax.ShapeDtypeStruct((s, d), dt) for d, dt in zip(out_dims, out_dtypes)]
    out_shape += [jax.ShapeDtypeStruct((1, d), F32) for d in acc_dims]
    res = pl.pallas_call(body, name=name, grid=(n,), in_specs=in_specs, out_specs=out_specs, out_shape=out_shape,
                         compiler_params=_cparams(("arbitrary",)))(*rows, *params)
    return tuple(res)


def _rw_bwd(fn, name, rows, params, row_cts, acc_cts, n_diff, tile):
    s = rows[0].shape[0]
    t = min(tile, s)
    n = s // t
    nr, npar, no, na = len(rows), len(params), len(row_cts), len(acc_cts)

    def body(*refs):
        r, p = refs[:nr], refs[nr:nr + npar]
        g, ga = refs[nr + npar:nr + npar + no], refs[nr + npar + no:nr + npar + no + na]
        dr, dp = refs[nr + npar + no + na:nr + npar + no + na + n_diff], refs[nr + npar + no + na + n_diff:]
        _, vjp = jax.vjp(fn, *[x[...] for x in r], *[x[...] for x in p])
        cts = vjp(tuple([x[...] for x in g] + [x[...] for x in ga]))
        for ref, val in zip(dr, cts[:n_diff]):
            ref[...] = val
        if npar:
            @pl.when(pl.program_id(0) == 0)
            def _():
                for ref in dp:
                    ref[...] = jnp.zeros_like(ref)
            for ref, val in zip(dp, cts[nr:]):
                ref[...] += val

    in_specs = [pl.BlockSpec((t, x.shape[1]), lambda i: (i, 0)) for x in rows]
    in_specs += [_resident(x.shape) for x in params]
    in_specs += [pl.BlockSpec((t, x.shape[1]), lambda i: (i, 0)) for x in row_cts]
    in_specs += [_resident(x.shape) for x in acc_cts]
    out_specs = [pl.BlockSpec((t, x.shape[1]), lambda i: (i, 0)) for x in rows[:n_diff]]
    out_specs += [_resident(x.shape) for x in params]
    out_shape = [jax.ShapeDtypeStruct(x.shape, F32) for x in rows[:n_diff]]
    out_shape += [jax.ShapeDtypeStruct(x.shape, F32) for x in params]
    res = pl.pallas_call(body, name=name, grid=(n,), in_specs=in_specs, out_specs=out_specs, out_shape=out_shape,
                         compiler_params=_cparams(("arbitrary",)))(*rows, *params, *row_cts, *acc_cts)
    return tuple(res[:n_diff]), tuple(res[n_diff:])


def make_rowwise(fn, name, out_dims, acc_dims=(), n_nondiff=0, tile=256):
    out_dtypes = (F32,) * len(out_dims)

    @jax.custom_vjp
    def op(rows, params):
        return _rw_fwd(fn, name, rows, params, out_dims, out_dtypes, acc_dims, tile)

    def fwd(rows, params):
        return op(rows, params), (rows, params)

    def bwd(res, cts):
        rows, params = res
        n_diff = len(rows) - n_nondiff
        d_rows, d_params = _rw_bwd(fn, name + "_bwd", rows, params, cts[:len(out_dims)], cts[len(out_dims):],
                                   n_diff, tile)
        d_rows = d_rows + tuple(jnp.zeros_like(x) for x in rows[n_diff:])
        return d_rows, d_params

    op.defvjp(fwd, bwd)
    return op


def _mm(a, b, mode, name):
    if mode == "nn":
        (m, k), (_, n) = a.shape, b.shape
    elif mode == "nt":
        (m, k), (n, _) = a.shape, b.shape
    else:
        (k, m), (_, n) = a.shape, b.shape
    tm = _pick(m, 512)
    tn = _pick(n, 1152)
    tk = _pick(k, 1152) if mode != "tn" else _pick(k, 512)
    nk = k // tk

    def body(a_ref, b_ref, o_ref, acc_ref):
        kk = pl.program_id(2)

        @pl.when(kk == 0)
        def _():
            acc_ref[...] = jnp.zeros_like(acc_ref)

        acc_ref[...] += _dot_raw(a_ref[...], b_ref[...], mode, False)

        @pl.when(kk == nk - 1)
        def _():
            o_ref[...] = acc_ref[...]

    if mode == "nn":
        a_spec = pl.BlockSpec((tm, tk), lambda i, j, kk: (i, kk))
        b_spec = pl.BlockSpec((tk, tn), lambda i, j, kk: (kk, j))
    elif mode == "nt":
        a_spec = pl.BlockSpec((tm, tk), lambda i, j, kk: (i, kk))
        b_spec = pl.BlockSpec((tn, tk), lambda i, j, kk: (j, kk))
    else:
        a_spec = pl.BlockSpec((tk, tm), lambda i, j, kk: (kk, i))
        b_spec = pl.BlockSpec((tk, tn), lambda i, j, kk: (kk, j))
    return pl.pallas_call(
        body, name=name, grid=(m // tm, n // tn, nk), in_specs=[a_spec, b_spec],
        out_specs=pl.BlockSpec((tm, tn), lambda i, j, kk: (i, j)),
        out_shape=jax.ShapeDtypeStruct((m, n), F32), scratch_shapes=[pltpu.VMEM((tm, tn), F32)],
        compiler_params=_cparams(("parallel", "parallel", "arbitrary")))(a, b)


def _mla_prep_fn(qraw, kvraw, kr, t1, t2):
    kr_rot = kr * t1 + roll_half(kr) * t2
    qs, ks = [], []
    for h in range(MLA_HEADS):
        q_r = qraw[:, h * 256 + 128:(h + 1) * 256]
        qs += [qraw[:, h * 256:h * 256 + 128], q_r * t1 + roll_half(q_r) * t2]
        ks += [kvraw[:, h * 128:(h + 1) * 128], kr_rot]
    return jnp.concatenate(qs, axis=1), jnp.concatenate(ks, axis=1), kvraw[:, 512:]


def _flash_tile(s):
    return 512 if s >= 2048 else 128


FLASH_SCALE = (MLA_NOPE + MLA_ROPE) ** -0.5
LOG2_E = 1.4426950408889634
FLASH_EXP2 = FLASH_SCALE * LOG2_E
STAT_ROWS = 8
FLASH_PAIR = 4
FLASH_STRIP = 32


def _as_rows(col_b):
    ones = jnp.full((STAT_ROWS, LANES), 1.0 / LANES, F32)
    return _dot_raw(ones, col_b, "nt", True)


def _flash_fwd(qf, kf, vf):
    s = qf.shape[0]
    t = _flash_tile(s)
    nb = s // t
    pair = range(FLASH_PAIR)

    ck = min(FLASH_STRIP, t)

    def body(q_ref, k_ref, v_ref, o_ref, lse_ref, m_sc, l_sc, acc_sc, st_sc, pt_sc):
        i = pl.program_id(1)
        m_sc[...] = jnp.full_like(m_sc, -1e30)
        l_sc[...] = jnp.zeros_like(l_sc)
        acc_sc[...] = jnp.zeros_like(acc_sc)
        qs = [q_ref[:, hh * 256:(hh + 1) * 256] for hh in pair]

        def step(j, on_diagonal):
            rows = pl.ds(pl.multiple_of(j * t, t), t)
            for hh in pair:
                st = _dot_raw(k_ref[rows, hh * 256:(hh + 1) * 256], qs[hh], "nt", False)
                if on_diagonal:
                    keep = lax.broadcasted_iota(jnp.int32, (t, t), 0) <= lax.broadcasted_iota(jnp.int32, (t, t), 1)
                    st = jnp.where(keep, st, -1e30)
                st_sc[hh] = st
            m_olds = [m_sc[hh] for hh in pair]
            m_news = [jnp.maximum(m_olds[hh], jnp.max(st_sc[hh], axis=0, keepdims=True)) for hh in pair]
            alphas = [jnp.exp2((m_olds[hh] - m_news[hh]) * FLASH_EXP2) for hh in pair]
            shifts = [m_news[hh] * FLASH_EXP2 for hh in pair]
            sums = [jnp.zeros((8, t), F32) for _ in pair]
            for hh in pair:
                for r in range(0, t, ck):
                    p = jnp.exp2(st_sc[hh, r:r + ck, :] * FLASH_EXP2 - shifts[hh])
                    pt_sc[hh, r:r + ck, :] = p.astype(BF16)
                    sums[hh] = sums[hh] + functools.reduce(lambda a, b: a + b, [p[u:u + 8] for u in range(0, ck, 8)])
            pvs = [_dot_raw(v_ref[rows, hh * 128:(hh + 1) * 128], pt_sc[hh], "tn", False) for hh in pair]
            for hh in pair:
                l_sc[hh] = alphas[hh] * l_sc[hh] + jnp.sum(sums[hh], axis=0, keepdims=True)
                acc_sc[hh] = alphas[hh] * acc_sc[hh] + pvs[hh]
                m_sc[hh] = m_news[hh]

        def two_steps(p, carry):
            step(2 * p, False)
            step(2 * p + 1, False)
            return carry

        lax.fori_loop(0, i // 2, two_steps, 0)

        @pl.when(i % 2 == 1)
        def _():
            step(i - 1, False)

        step(i, True)
        for hh in pair:
            o_ref[:, hh * 128:(hh + 1) * 128] = (acc_sc[hh] / l_sc[hh]).T
            lse2 = m_sc[hh] * FLASH_EXP2 + jnp.log(l_sc[hh]) * LOG2_E
            lse_ref[hh] = jnp.broadcast_to(lse2, (STAT_ROWS, t))

    p = FLASH_PAIR
    return pl.pallas_call(
        body, name="flash_fwd", grid=(MLA_HEADS // p, nb),
        in_specs=[pl.BlockSpec((t, p * 256), lambda h, i: (i, h)),
                  pl.BlockSpec((s, p * 256), lambda h, i: (0, h), pipeline_mode=pl.Buffered(1)),
                  pl.BlockSpec((s, p * 128), lambda h, i: (0, h), pipeline_mode=pl.Buffered(1))],
        out_specs=[pl.BlockSpec((t, p * 128), lambda h, i: (i, h)),
                   pl.BlockSpec((p, STAT_ROWS, t), lambda h, i: (h, 0, i))],
        out_shape=[jax.ShapeDtypeStruct((s, MLA_WIDTH), F32),
                   jax.ShapeDtypeStruct((MLA_HEADS, STAT_ROWS, s), F32)],
        scratch_shapes=[pltpu.VMEM((p, 1, t), F32), pltpu.VMEM((p, 1, t), F32), pltpu.VMEM((p, 128, t), F32),
                        pltpu.VMEM((p, t, t), F32), pltpu.VMEM((p, t, t), BF16)],
        compiler_params=_cparams(("parallel", "arbitrary")))(qf, kf, vf)


def _flash_bwd(qf, kf, vf, lse, o, do):
    s = qf.shape[0]
    t = _flash_tile(s)
    nb = s // t

    def body(q_ref, k_ref, v_ref, lse_ref, o_ref, do_ref, dq_ref, dk_ref, dv_ref, dl_sc, do16_sc):
        j = pl.program_id(1)

        @pl.when(j == 0)
        def _():
            dq_ref[...] = jnp.zeros_like(dq_ref)

            def stage(b, carry):
                rows = pl.ds(pl.multiple_of(b * t, t), t)
                do_t = do_ref[rows, :]
                delta = jnp.sum(do_t * o_ref[rows, :], axis=1, keepdims=True)
                dl_sc[:, rows] = _as_rows(jnp.broadcast_to(delta, (t, LANES)))
                do16_sc[rows, :] = do_t.astype(BF16)
                return carry

            lax.fori_loop(0, nb, stage, 0, unroll=min(4, nb))

        dk_ref[...] = jnp.zeros_like(dk_ref)
        dv_ref[...] = jnp.zeros_like(dv_ref)
        k = k_ref[...]
        v = v_ref[...]

        def step(i, on_diagonal):
            rows = pl.ds(pl.multiple_of(i * t, t), t)
            q = q_ref[rows, :]
            do_t = do16_sc[rows, :]
            st = _dot_raw(k, q, "nt", False) * FLASH_EXP2 - lse_ref[0, 0:1, rows]
            if on_diagonal:
                keep = lax.broadcasted_iota(jnp.int32, (t, t), 0) <= lax.broadcasted_iota(jnp.int32, (t, t), 1)
                st = jnp.where(keep, st, -1e30)
            pt = jnp.exp2(st)
            dst = pt * (_dot_raw(v, do_t, "nt", False) - dl_sc[0:1, rows])
            dv_ref[...] += _dot_raw(pt, do_t, "nn", False)
            dk_ref[...] += _dot_raw(dst, q, "nn", False)
            dq_ref[rows, :] += _dot_raw(dst, k, "tn", False)

        def two_steps(p, carry):
            step(j + 1 + 2 * p, False)
            step(j + 2 + 2 * p, False)
            return carry

        step(j, True)
        below = nb - 1 - j
        lax.fori_loop(0, below // 2, two_steps, 0)

        @pl.when(below % 2 == 1)
        def _():
            step(nb - 1, False)

        dk_ref[...] *= FLASH_SCALE

        @pl.when(j == nb - 1)
        def _():
            dq_ref[...] *= FLASH_SCALE

    per_head = lambda d: pl.BlockSpec((s, d), lambda h, j: (0, h), pipeline_mode=pl.Buffered(1))
    return pl.pallas_call(
        body, name="flash_bwd", grid=(MLA_HEADS, nb),
        in_specs=[per_head(256), pl.BlockSpec((t, 256), lambda h, j: (j, h)),
                  pl.BlockSpec((t, 128), lambda h, j: (j, h)),
                  pl.BlockSpec((1, STAT_ROWS, s), lambda h, j: (h, 0, 0)), per_head(128), per_head(128)],
        out_specs=[pl.BlockSpec((s, 256), lambda h, j: (0, h)), pl.BlockSpec((t, 256), lambda h, j: (j, h)),
                   pl.BlockSpec((t, 128), lambda h, j: (j, h))],
        out_shape=[jax.ShapeDtypeStruct((s, 1024), F32), jax.ShapeDtypeStruct((s, 1024), F32),
                   jax.ShapeDtypeStruct((s, MLA_WIDTH), F32)],
        scratch_shapes=[pltpu.VMEM((STAT_ROWS, s), F32), pltpu.VMEM((s, 128), BF16)],
        compiler_params=_cparams(("parallel", "arbitrary")))(qf, kf, vf, lse, o, do)


MLA_FRONT_TILE = 512


def _mla_front_fn(q_lat, kv_lat, kr, t1, t2, q_norm_w, q_up, kv_norm_w, kv_up):
    qraw = bdot(_rms(q_lat, q_norm_w), q_up)
    kvraw = bdot(_rms(kv_lat, kv_norm_w), kv_up)
    return _mla_prep_fn(qraw, kvraw, kr, t1, t2)


def _mla_front(rows, params):
    return _rw_fwd(_mla_front_fn, "mla_front", rows, params, (1024, 1024, 512), (BF16, BF16, BF16), (),
                   MLA_FRONT_TILE)


@jax.custom_vjp
def mla_attention(rows, params):
    return _flash_fwd(*_mla_front(rows, params))[0]


def _mla_attention_fwd(rows, params):
    qf, kf, vf = _mla_front(rows, params)
    o, lse = _flash_fwd(qf, kf, vf)
    return o, (rows, params, qf, kf, vf, o, lse)


def _mla_attention_bwd(res, do):
    rows, params, qf, kf, vf, o, lse = res
    d_rows, d_params = _rw_bwd(_mla_front_fn, "mla_front_bwd", rows, params, _flash_bwd(qf, kf, vf, lse, o, do), (),
                               3, MLA_FRONT_TILE)
    return d_rows + (jnp.zeros_like(rows[3]), jnp.zeros_like(rows[4])), d_params


mla_attention.defvjp(_mla_attention_fwd, _mla_attention_bwd)


def _lane_pick(x, lane):
    ids = lax.broadcasted_iota(jnp.int32, x.shape, 1)
    col = jnp.sum(jnp.where(ids == lane, x, 0.0), axis=1, keepdims=True)
    return jnp.broadcast_to(col, x.shape)


GDN_HALO = 8


@functools.partial(jax.custom_vjp, nondiff_argnums=(1,))
def _roll_rows(x, d):
    return pltpu.roll(x, d, 0)


_roll_rows.defvjp(lambda x, d: (pltpu.roll(x, d, 0), None), lambda d, _, g: (pltpu.roll(g, g.shape[0] - d, 0),))


def _gdn_prep_fn(prev, cur, ab, w0, w1, w2, w3, a_log, dt_bias):
    xcat = jnp.concatenate([prev, cur], axis=0)
    x0, x1, x2 = [_roll_rows(xcat, GDN_CONV - 1 - j)[GDN_HALO:] for j in range(GDN_CONV - 1)]
    qkv = _silu(x0 * w0 + x1 * w1 + x2 * w2 + cur * w3)
    g_all = -jnp.exp(a_log) * _softplus(ab + dt_bias)
    beta_all = _sigmoid(ab)
    qs, ks, gs, bs = [], [], [], []
    for h in range(GDN_HEADS):
        q = qkv[:, h * 128:(h + 1) * 128]
        k = qkv[:, 512 + h * 128:512 + (h + 1) * 128]
        qs.append(q * lax.rsqrt(jnp.sum(q * q, axis=-1, keepdims=True) + NORM_EPS) * (GDN_DK ** -0.5))
        ks.append(k * lax.rsqrt(jnp.sum(k * k, axis=-1, keepdims=True) + NORM_EPS))
        gs.append(_lane_pick(g_all, h))
        bs.append(_lane_pick(beta_all, GDN_HEADS + h))
    cat = lambda xs: jnp.concatenate(xs, axis=1)
    return cat(qs), cat(ks), qkv[:, 1024:], cat(gs), cat(bs)


GDN_PREP_TILE = 256


def _gdn_prep_specs(s, params, reverse=False):
    t = min(GDN_PREP_TILE, s)
    n = s // t
    blk = (lambda i: n - 1 - i) if reverse else (lambda i: i)
    prev = pl.BlockSpec((GDN_HALO, GDN_QKV), lambda i: (jnp.maximum(blk(i) * (t // GDN_HALO) - 1, 0), 0))
    rows = lambda d: pl.BlockSpec((t, d), lambda i: (blk(i), 0))
    return t, rows, [prev, rows(GDN_QKV), rows(LANES)] + [pl.BlockSpec(p.shape, lambda i: (0, 0)) for p in params]


def _gdn_prep_masked(has_rows_before):
    return lambda prev, *rest: _gdn_prep_fn(prev * has_rows_before, *rest)


def _gdn_prep_fwd(qkv, ab, params):
    s = qkv.shape[0]
    t, rows, in_specs = _gdn_prep_specs(s, params)

    def body(prev_ref, cur_ref, ab_ref, *refs):
        p_refs, o_refs = refs[:len(params)], refs[len(params):]
        has_rows_before = (pl.program_id(0) > 0).astype(F32)
        outs = _gdn_prep_masked(has_rows_before)(prev_ref[...], cur_ref[...], ab_ref[...], *[p[...] for p in p_refs])
        for ref, val in zip(o_refs, outs):
            ref[...] = val

    return pl.pallas_call(
        body, name="gdn_prep", grid=(s // t,), in_specs=in_specs,
        out_specs=[rows(GDN_WIDTH)] * 5, out_shape=[jax.ShapeDtypeStruct((s, GDN_WIDTH), F32)] * 5,
        compiler_params=_cparams(("parallel",)))(qkv, qkv, ab, *params)


def _gdn_prep_bwd(qkv, ab, params, cts):
    s = qkv.shape[0]
    t, rows, in_specs = _gdn_prep_specs(s, params, reverse=True)
    n = s // t
    npar = len(params)

    def body(prev_ref, cur_ref, ab_ref, *refs):
        p_refs, g_refs = refs[:npar], refs[npar:npar + 5]
        dcur_ref, dab_ref = refs[npar + 5:npar + 7]
        dp_refs, carry_sc = refs[npar + 7:-1], refs[-1]

        @pl.when(pl.program_id(0) == 0)
        def _():
            carry_sc[...] = jnp.zeros_like(carry_sc)
            for ref in dp_refs:
                ref[...] = jnp.zeros_like(ref)

        has_rows_before = (pl.program_id(0) < n - 1).astype(F32)
        _, vjp = jax.vjp(_gdn_prep_masked(has_rows_before), prev_ref[...], cur_ref[...], ab_ref[...],
                         *[p[...] for p in p_refs])
        d_prev, d_cur, d_ab, *d_params = vjp(tuple(g[...] for g in g_refs))
        dcur_ref[...] = d_cur
        dcur_ref[t - GDN_HALO:, :] += carry_sc[...]
        carry_sc[...] = d_prev
        dab_ref[...] = d_ab
        for ref, val in zip(dp_refs, d_params):
            ref[...] += val

    res = pl.pallas_call(
        body, name="gdn_prep_bwd", grid=(n,), in_specs=in_specs + [rows(GDN_WIDTH)] * 5,
        out_specs=[rows(GDN_QKV), rows(LANES)] + [pl.BlockSpec(p.shape, lambda i: (0, 0)) for p in params],
        out_shape=[jax.ShapeDtypeStruct((s, GDN_QKV), F32), jax.ShapeDtypeStruct((s, LANES), F32)]
        + [jax.ShapeDtypeStruct(p.shape, F32) for p in params],
        scratch_shapes=[pltpu.VMEM((GDN_HALO, GDN_QKV), F32)],
        compiler_params=_cparams(("arbitrary",)))(qkv, qkv, ab, *params, *cts)
    return res[0], res[1], tuple(res[2:])


@jax.custom_vjp
def gdn_prep(qkv, ab, params):
    return tuple(_gdn_prep_fwd(qkv, ab, params))


def _gdn_prep_vjp_fwd(qkv, ab, params):
    return tuple(_gdn_prep_fwd(qkv, ab, params)), (qkv, ab, params)


def _gdn_prep_vjp_bwd(res, cts):
    qkv, ab, params = res
    return _gdn_prep_bwd(qkv, ab, params, cts)


gdn_prep.defvjp(_gdn_prep_vjp_fwd, _gdn_prep_vjp_bwd)


def _tri_dot(x, mode):
    c = x.shape[0]
    tri = (lax.broadcasted_iota(jnp.int32, (c, c), 0) >= lax.broadcasted_iota(jnp.int32, (c, c), 1)).astype(BF16)
    hi = x.astype(BF16)
    rest = x - hi.astype(F32)
    mid = rest.astype(BF16)
    lo = (rest - mid.astype(F32)).astype(BF16)
    dot = lambda part: lax.dot_general(tri, part, _DN[mode], preferred_element_type=F32)
    return dot(hi) + (dot(mid) + dot(lo))


@jax.custom_vjp
def _chunk_cumsum(x):
    return _tri_dot(x, "nn")


_chunk_cumsum.defvjp(lambda x: (_tri_dot(x, "nn"), None), lambda _, g: (_tri_dot(g, "tn"),))


@jax.custom_vjp
def _unit_lower_inverse(lms):
    c = lms[0].shape[0]
    row = lax.broadcasted_iota(jnp.int32, (c, c), 0)
    col = lax.broadcasted_iota(jnp.int32, (c, c), 1)
    ts = [(row == col).astype(F32) - jnp.where((row >> 1) == (col >> 1), lm, 0.0) for lm in lms]
    for level in range(1, int(math.log2(c))):
        below = ((row >> (level + 1)) == (col >> (level + 1))) & ((row >> level) != (col >> level))
        mids = [_dot_split(t, jnp.where(below, lm, 0.0)) for t, lm in zip(ts, lms)]
        ts = [t - _dot_split(mid, t) for t, mid in zip(ts, mids)]
    return tuple(ts)


def _uli_fwd(lms):
    ts = _unit_lower_inverse(lms)
    return ts, ts


def _uli_bwd(ts, gs):
    mids = [bdot(t, g, "tn") for t, g in zip(ts, gs)]
    return (tuple(-bdot(mid, t, "nt") for t, mid in zip(ts, mids)),)


_unit_lower_inverse.defvjp(_uli_fwd, _uli_bwd)


@jax.custom_vjp
def _known_inverse(lms, ts):
    return ts


_known_inverse.defvjp(lambda lms, ts: (ts, ts),
                      lambda ts, gs: (_uli_bwd(ts, gs)[0], tuple(jnp.zeros_like(t) for t in ts)))


GDN_STEP_CHUNKS = 4


def _gdn_chunk_fn(qs, ks, vs, gbs, bbs, s0s, known_ts=None):
    chains, n_heads = range(len(qs)), len(s0s)
    c = qs[0].shape[0]
    row = lax.broadcasted_iota(jnp.int32, (c, c), 0)
    col = lax.broadcasted_iota(jnp.int32, (c, c), 1)
    incl, strict = row >= col, row > col
    gc = [_chunk_cumsum(gbs[i]) for i in chains]
    decay = [jnp.exp(jnp.where(incl, gc[i] - gc[i].T, -1e30)) for i in chains]
    g_last = [jnp.sum(gbs[i], axis=0, keepdims=True) for i in chains]
    eg = [jnp.exp(gc[i]) for i in chains]
    kb = [ks[i] * bbs[i] for i in chains]
    lms = tuple(jnp.where(strict, bdot(kb[i], ks[i], "nt") * decay[i], 0.0) for i in chains)
    ts = _unit_lower_inverse(lms) if known_ts is None else _known_inverse(lms, known_ts)
    u = [bdot(ts[i], vs[i] * bbs[i]) for i in chains]
    w = [bdot(ts[i], kb[i] * eg[i]) for i in chains]
    qk = [bdot(qs[i], ks[i], "nt") * decay[i] for i in chains]
    q_dec = [qs[i] * eg[i] for i in chains]
    k_dec = [ks[i] * jnp.exp(g_last[i] - gc[i]) for i in chains]
    states, outs = list(s0s), []
    for first in range(0, len(qs), n_heads):
        here = range(first, first + n_heads)
        v_new = [u[i] - bdot(w[i], states[i - first]) for i in here]
        outs += [bdot(q_dec[i], states[i - first]) + bdot(qk[i], v_new[i - first]) for i in here]
        states = [states[i - first] * jnp.exp(g_last[i]) + bdot(k_dec[i], v_new[i - first], "tn") for i in here]
    return (tuple(outs), tuple(states)), ts


def _chain_tiles(ref, c):
    return tuple(ref[p * c:(p + 1) * c, h * 128:(h + 1) * 128]
                 for p in range(ref.shape[0] // c) for h in range(GDN_HEADS))


def _gdn_step(s):
    c = min(GDN_CHUNK, s)
    p = min(GDN_STEP_CHUNKS, s // c)
    return c, p, s // (c * p)


def _gdn_fwd(q, k, v, gb, bb):
    s = q.shape[0]
    c, p, steps = _gdn_step(s)

    def body(q_ref, k_ref, v_ref, g_ref, b_ref, o_ref, st_ref, inv_ref, s_sc):
        @pl.when(pl.program_id(0) == 0)
        def _():
            s_sc[...] = jnp.zeros_like(s_sc)

        s0s = tuple(s_sc[h] for h in range(GDN_HEADS))
        for h in range(GDN_HEADS):
            st_ref[h, 0] = s0s[h]
        (os, s1s), ts = _gdn_chunk_fn(*[_chain_tiles(ref, c) for ref in (q_ref, k_ref, v_ref, g_ref, b_ref)], s0s)
        for i in range(p * GDN_HEADS):
            cc, h = divmod(i, GDN_HEADS)
            o_ref[cc * c:(cc + 1) * c, h * 128:(h + 1) * 128] = os[i]
            inv_ref[h, cc] = ts[i]
        for h in range(GDN_HEADS):
            s_sc[h] = s1s[h]

    blk = pl.BlockSpec((p * c, GDN_WIDTH), lambda n: (n, 0))
    return pl.pallas_call(
        body, name="gdn_fwd", grid=(steps,), in_specs=[blk] * 5,
        out_specs=[blk, pl.BlockSpec((GDN_HEADS, 1, 128, 128), lambda n: (0, n, 0, 0)),
                   pl.BlockSpec((GDN_HEADS, p, c, c), lambda n: (0, n, 0, 0))],
        out_shape=[jax.ShapeDtypeStruct((s, GDN_WIDTH), F32), jax.ShapeDtypeStruct((GDN_HEADS, steps, 128, 128), F32),
                   jax.ShapeDtypeStruct((GDN_HEADS, steps * p, c, c), F32)],
        scratch_shapes=[pltpu.VMEM((GDN_HEADS, 128, 128), F32)],
        compiler_params=_cparams(("arbitrary",)))(q, k, v, gb, bb)


def _gdn_bwd(q, k, v, gb, bb, states, inverses, do):
    s = q.shape[0]
    c, p, steps = _gdn_step(s)

    def body(q_ref, k_ref, v_ref, g_ref, b_ref, st_ref, inv_ref, do_ref, dq_ref, dk_ref, dv_ref, dg_ref, db_ref, ds_sc):
        @pl.when(pl.program_id(0) == 0)
        def _():
            ds_sc[...] = jnp.zeros_like(ds_sc)

        s0s = tuple(st_ref[h, 0] for h in range(GDN_HEADS))
        ts = tuple(inv_ref[h, cc] for cc in range(p) for h in range(GDN_HEADS))
        chunks = lambda *args: _gdn_chunk_fn(*args, known_ts=ts)[0]
        _, vjp = jax.vjp(chunks, *[_chain_tiles(ref, c) for ref in (q_ref, k_ref, v_ref, g_ref, b_ref)], s0s)
        *d_tiles, ds0s = vjp((_chain_tiles(do_ref, c), tuple(ds_sc[h] for h in range(GDN_HEADS))))
        for i in range(p * GDN_HEADS):
            cc, h = divmod(i, GDN_HEADS)
            for ref, d in zip((dq_ref, dk_ref, dv_ref, dg_ref, db_ref), d_tiles):
                ref[cc * c:(cc + 1) * c, h * 128:(h + 1) * 128] = d[i]
        for h in range(GDN_HEADS):
            ds_sc[h] = ds0s[h]

    blk = pl.BlockSpec((p * c, GDN_WIDTH), lambda n: (steps - 1 - n, 0))
    return pl.pallas_call(
        body, name="gdn_bwd", grid=(steps,),
        in_specs=[blk] * 5 + [pl.BlockSpec((GDN_HEADS, 1, 128, 128), lambda n: (0, steps - 1 - n, 0, 0)),
                              pl.BlockSpec((GDN_HEADS, p, c, c), lambda n: (0, steps - 1 - n, 0, 0)), blk],
        out_specs=[blk] * 5, out_shape=[jax.ShapeDtypeStruct((s, GDN_WIDTH), F32)] * 5,
        scratch_shapes=[pltpu.VMEM((GDN_HEADS, 128, 128), F32)],
        compiler_params=_cparams(("arbitrary",)))(q, k, v, gb, bb, states, inverses, do)


@jax.custom_vjp
def gdn_core(q, k, v, gb, bb):
    return _gdn_fwd(q, k, v, gb, bb)[0]


def _gdn_core_fwd(q, k, v, gb, bb):
    o, states, inverses = _gdn_fwd(q, k, v, gb, bb)
    return o, (q, k, v, gb, bb, states, inverses)


def _gdn_core_bwd(res, do):
    return tuple(_gdn_bwd(*res, do))


gdn_core.defvjp(_gdn_core_fwd, _gdn_core_bwd)


def _cact_fn(c):
    return (_silu(c),)


def _prenorm_fn(x, w, scale_raw, scale_b, shift_raw, shift_b):
    return (_rms(x, w) * (1.0 + scale_raw + scale_b) + shift_raw + shift_b,)


def _tail_fn(x, o_mla, z_mla, o_gdn, z_gdn, o_norm_w, w_out, post_w, gate_raw, gate_b):
    y_mla = o_mla * _silu(z_mla)
    parts = [_rms(o_gdn[:, h * 128:(h + 1) * 128], o_norm_w) for h in range(GDN_HEADS)]
    y_gdn = jnp.concatenate(parts, axis=1) * _silu(z_gdn)
    y = bdot(y_mla, w_out[:MLA_WIDTH]) + bdot(y_gdn, w_out[MLA_WIDTH:])
    return (x + (gate_raw + gate_b) * _rms(y, post_w),)


def _loss_fn(y, tgt):
    err = y - tgt
    part = jnp.sum(0.5 * jnp.mean(err * err, axis=-1, keepdims=True), axis=0, keepdims=True)
    return (jnp.broadcast_to(part, (1, LANES)),)


def _front_fwd(x, norm_params, ws16):
    s = x.shape[0]
    t = min(512, s)

    def body(x_ref, *refs):
        p_refs, w_refs, o_refs = refs[:len(norm_params)], refs[len(norm_params):-len(ws16)], refs[-len(ws16):]
        (h,) = _prenorm_fn(x_ref[...], *[p[...] for p in p_refs])
        h16 = h.astype(BF16)
        for w_ref, o_ref in zip(w_refs, o_refs):
            o_ref[...] = _dot_raw(h16, w_ref[...], "nn", False)

    rows = lambda d: pl.BlockSpec((t, d), lambda i: (i, 0))
    res = pl.pallas_call(
        body, name="front", grid=(s // t,),
        in_specs=[rows(D_MODEL)] + [_resident(p.shape) for p in norm_params] + [_resident(w.shape) for w in ws16],
        out_specs=[rows(w.shape[1]) for w in ws16],
        out_shape=[jax.ShapeDtypeStruct((s, w.shape[1]), F32) for w in ws16],
        compiler_params=_cparams(("parallel",)))(x, *norm_params, *ws16)
    return tuple(res)


def _front_bwd(x, norm_params, ws16, dys, dx_skip):
    s = x.shape[0]
    t = min(512, s)
    npar, ng = len(norm_params), len(ws16)

    def body(x_ref, *refs):
        p_refs, w_refs, g_refs = refs[:npar], refs[npar:npar + ng], refs[npar + ng:npar + 2 * ng]
        skip_ref, dx_ref = refs[npar + 2 * ng], refs[npar + 2 * ng + 1]
        dp_refs, dw_refs = refs[npar + 2 * ng + 2:2 * npar + 2 * ng + 2], refs[2 * npar + 2 * ng + 2:]

        @pl.when(pl.program_id(0) == 0)
        def _():
            for ref in dp_refs + dw_refs:
                ref[...] = jnp.zeros_like(ref)

        (h,), vjp = jax.vjp(_prenorm_fn, x_ref[...], *[p[...] for p in p_refs])
        h16 = h.astype(BF16)
        dys16 = [g[...].astype(BF16) for g in g_refs]
        dh = functools.reduce(lambda a, b: a + b,
                              [_dot_raw(dy, w_ref[...], "nt", False) for dy, w_ref in zip(dys16, w_refs)])
        for dy, dw_ref in zip(dys16, dw_refs):
            dw_ref[...] += _dot_raw(h16, dy, "tn", False)
        dx, *d_params = vjp((dh,))
        dx_ref[...] = dx + skip_ref[...]
        for ref, val in zip(dp_refs, d_params):
            ref[...] += val

    rows = lambda d: pl.BlockSpec((t, d), lambda i: (i, 0))
    res = pl.pallas_call(
        body, name="front_bwd", grid=(s // t,),
        in_specs=[rows(D_MODEL)] + [_resident(p.shape) for p in norm_params] + [_resident(w.shape) for w in ws16]
        + [rows(w.shape[1]) for w in ws16] + [rows(D_MODEL)],
        out_specs=[rows(D_MODEL)] + [_resident(p.shape) for p in norm_params] + [_resident(w.shape) for w in ws16],
        out_shape=[jax.ShapeDtypeStruct(x.shape, F32)] + [jax.ShapeDtypeStruct(p.shape, F32) for p in norm_params]
        + [jax.ShapeDtypeStruct(w.shape, F32) for w in ws16],
        compiler_params=_cparams(("arbitrary",)))(x, *norm_params, *ws16, *dys, dx_skip)
    return res[0], tuple(res[1:1 + npar]), tuple(res[1 + npar:])


@jax.custom_vjp
def front(x, norm_params, ws):
    return _front_fwd(x, norm_params, tuple(w.astype(BF16) for w in ws)) + (x,)


def _front_vjp_fwd(x, norm_params, ws):
    ws16 = tuple(w.astype(BF16) for w in ws)
    return _front_fwd(x, norm_params, ws16) + (x,), (x, norm_params, ws16)


def _front_vjp_bwd(res, cts):
    return _front_bwd(*res, cts[:-1], cts[-1])


front.defvjp(_front_vjp_fwd, _front_vjp_bwd)


_OPS = dict(
    cact=make_rowwise(_cact_fn, "c_act", (D_MODEL,), tile=16),
    tail=make_rowwise(_tail_fn, "tail", (D_MODEL,), tile=512),
    loss=make_rowwise(_loss_fn, "loss", (), acc_dims=(LANES,), n_nondiff=1),
)


def _swap_halves(w):
    half = w.shape[-1] // 2
    return jnp.concatenate([w[..., half:], w[..., :half]], axis=-1)


def _w_in_groups(w):
    k_pe = w[:, 640:704]
    ab = jnp.concatenate([w[:, 2752:2760], jnp.zeros((w.shape[0], LANES - 8), w.dtype)], axis=1)
    return (w[:, :384], w[:, 384:640], jnp.concatenate([k_pe, _swap_halves(k_pe)], axis=1), w[:, 704:1216],
            w[:, 1216:2752], ab, w[:, 2760:])


def _q_up_ext(w):
    parts = []
    for h in range(MLA_HEADS):
        rope = w[:, h * 192 + 128:(h + 1) * 192]
        parts += [w[:, h * 192:h * 192 + 128], rope, _swap_halves(rope)]
    return jnp.concatenate(parts, axis=1)


def _kv_up_perm(w):
    ks = [w[:, h * 256:h * 256 + 128] for h in range(MLA_HEADS)]
    vs = [w[:, h * 256 + 128:(h + 1) * 256] for h in range(MLA_HEADS)]
    return jnp.concatenate(ks + vs, axis=1)


def _pad_lanes(v):
    return jnp.pad(v, (0, LANES - v.shape[0]))[None, :]


def _local_loss(weights, mods, x, positions, target):
    s = x.shape[0]
    half = MLA_ROPE // 2
    inv_freq = jnp.power(ROPE_THETA, -jnp.arange(half, dtype=F32) * 2.0 / MLA_ROPE)
    ang = positions.astype(F32)[:, None] * inv_freq
    cos, sin, zero = jnp.cos(ang), jnp.sin(ang), jnp.zeros((s, 2 * half), F32)
    t1 = jnp.concatenate([cos, cos, zero], axis=1)
    t2 = jnp.concatenate([-sin, sin, zero], axis=1)

    for l in range(DEPTH):
        mod = mods[l]
        b = weights["b_mod"][l][None, :]
        shift_raw, scale_raw, gate_raw = mod[:, :1024], mod[:, 1024:2048], mod[:, 2048:]
        shift_b, scale_b, gate_b = b[:, :1024], b[:, 1024:2048], b[:, 2048:]
        q_lat, kv_lat, kr, z_mla, qkv, ab, z_gdn, x = front(
            x, (weights["pre_norm_w"][l][None], scale_raw, scale_b, shift_raw, shift_b),
            _w_in_groups(weights["w_in"][l]))
        o_mla = mla_attention((q_lat, kv_lat, kr, t1, t2),
                              (weights["mla_q_norm_w"][l][None], _q_up_ext(weights["mla_q_up"][l]),
                               weights["mla_kv_norm_w"][l][None], _kv_up_perm(weights["mla_kv_up"][l])))
        cw = weights["gdn_conv_w"][l]
        params = tuple(cw[j][None] for j in range(GDN_CONV))
        params += (_pad_lanes(weights["gdn_a_log"][l]), _pad_lanes(weights["gdn_dt_bias"][l]))
        o_gdn = gdn_core(*gdn_prep(qkv, ab, params))
        (x,) = _OPS["tail"]((x, o_mla, z_mla, o_gdn, z_gdn),
                            (weights["gdn_o_norm_w"][l][None], weights["w_out"][l], weights["post_norm_w"][l][None],
                             gate_raw, gate_b))
    (acc,) = _OPS["loss"]((x, target), ())
    return acc[0, 0]


def _chip_index():
    return 2 * lax.axis_index("x") + lax.axis_index("y")


def _other_chips(x, y):
    return [(1 - x, y), (x, 1 - y), (1 - x, 1 - y)]


def _any_spec():
    return pl.BlockSpec(memory_space=pl.ANY)


def _half(ref, hc):
    n = ref.shape[0] // 2
    return ref.at[pl.ds(hc * n, n)]


def ag_weights(shards):
    n = len(shards)

    def body(*refs):
        ins, outs = refs[:n], refs[n:2 * n]
        send_sems, recv_sems = refs[2 * n:]
        x, y, c = lax.axis_index("x"), lax.axis_index("y"), lax.axis_index("c")
        sibling = (x, y, 1 - c)
        chips = _other_chips(x, y)

        def copy(t, k, src, chip_xy, hc, to):
            return pltpu.make_async_remote_copy(
                src_ref=src, dst_ref=_half(outs[t].at[2 * chip_xy[0] + chip_xy[1]], hc),
                send_sem=send_sems.at[6 * t + k], recv_sem=recv_sems.at[6 * t + k], device_id=to, device_id_type=MESH)

        first = [copy(t, k, _half(ins[t], c), (x, y), c, (*chip, c)) for k, chip in enumerate(chips) for t in range(n)]
        for cp in first:
            cp.start()
        passed = []
        for k, chip in enumerate(chips):
            for t in range(n):
                landed = _half(outs[t].at[2 * chip[0] + chip[1]], c)
                copy(t, k, landed, chip, c, (x, y, c)).wait_recv()
                passed.append(copy(t, 3 + k, landed, chip, c, sibling))
                passed[-1].start()
        for k, chip in enumerate(chips):
            for t in range(n):
                copy(t, 3 + k, _half(ins[t], c), chip, 1 - c, (x, y, c)).wait_recv()
        for cp in first + passed:
            cp.wait_send()

    return pl.pallas_call(
        body, name="ag_weights", in_specs=[_any_spec()] * n, out_specs=[_any_spec()] * n,
        out_shape=[jax.ShapeDtypeStruct((N_CHIPS,) + a.shape, a.dtype) for a in shards],
        scratch_shapes=[pltpu.SemaphoreType.DMA((6 * n,)), pltpu.SemaphoreType.DMA((6 * n,))],
        compiler_params=pltpu.CompilerParams(has_side_effects=True))(*shards)


def all_gather_rows(x):
    r, cols = x.shape
    flips = [(k >> 2 & 1, k >> 1 & 1, k & 1) for k in range(1, 8)]

    def body(x_ref, out_ref, send_sems, recv_sems):
        here = (lax.axis_index("x"), lax.axis_index("y"), lax.axis_index("c"))
        peers = [tuple(1 - p if f else p for p, f in zip(here, flip)) for flip in flips]
        slot = lambda dev: 4 * dev[0] + 2 * dev[1] + dev[2]
        out_ref[slot(here)] = x_ref[...]
        copies = [pltpu.make_async_remote_copy(
            src_ref=x_ref, dst_ref=out_ref.at[slot(here)], send_sem=send_sems.at[k], recv_sem=recv_sems.at[k],
            device_id=peer, device_id_type=MESH) for k, peer in enumerate(peers)]
        for cp in copies:
            cp.start()
        for k, peer in enumerate(peers):
            pltpu.make_async_remote_copy(
                src_ref=x_ref, dst_ref=out_ref.at[slot(peer)], send_sem=send_sems.at[k], recv_sem=recv_sems.at[k],
                device_id=peer, device_id_type=MESH).wait_recv()
        for cp in copies:
            cp.wait_send()

    return pl.pallas_call(
        body, name="all_gather_rows", in_specs=[pl.BlockSpec(memory_space=pltpu.VMEM)],
        out_specs=pl.BlockSpec(memory_space=pltpu.VMEM), out_shape=jax.ShapeDtypeStruct((8, r, cols), F32),
        scratch_shapes=[pltpu.SemaphoreType.DMA((7,)), pltpu.SemaphoreType.DMA((7,))],
        compiler_params=pltpu.CompilerParams(has_side_effects=True, vmem_limit_bytes=VMEM_LIMIT))(x)


def rs_pair(gs):
    n = len(gs)

    def body(*refs):
        ins, outs = refs[:n], refs[n:2 * n]
        send_sems, recv_sems = refs[2 * n:]
        x, y, c = lax.axis_index("x"), lax.axis_index("y"), lax.axis_index("c")
        lh = [g.shape[1] // 2 for g in gs]
        copies = [pltpu.make_async_remote_copy(
            src_ref=ins[t].at[:, pl.ds((1 - c) * lh[t], lh[t])], dst_ref=outs[t], send_sem=send_sems.at[t],
            recv_sem=recv_sems.at[t], device_id=(x, y, 1 - c), device_id_type=MESH) for t in range(n)]
        for cp in copies:
            cp.start()
        for cp in copies:
            cp.wait()

    return pl.pallas_call(
        body, name="rs_pair", in_specs=[_any_spec()] * n, out_specs=[_any_spec()] * n,
        out_shape=[jax.ShapeDtypeStruct((N_CHIPS, g.shape[1] // 2) + g.shape[2:], F32) for g in gs],
        scratch_shapes=[pltpu.SemaphoreType.DMA((n,)), pltpu.SemaphoreType.DMA((n,))],
        compiler_params=pltpu.CompilerParams(has_side_effects=True))(*gs)


def rs_cross(pairs):
    n = len(pairs)

    def body(*refs):
        ins, outs = refs[:n], refs[n:2 * n]
        send_sems, recv_sems = refs[2 * n:]
        x, y, c = lax.axis_index("x"), lax.axis_index("y"), lax.axis_index("c")
        copies = []
        for k, chip in enumerate(_other_chips(x, y)):
            for t in range(n):
                copies.append(pltpu.make_async_remote_copy(
                    src_ref=ins[t].at[2 * chip[0] + chip[1]], dst_ref=outs[t].at[k], send_sem=send_sems.at[3 * t + k],
                    recv_sem=recv_sems.at[3 * t + k], device_id=(*chip, c), device_id_type=MESH))
        for cp in copies:
            cp.start()
        for cp in copies:
            cp.wait()

    return pl.pallas_call(
        body, name="rs_cross", in_specs=[_any_spec()] * n, out_specs=[_any_spec()] * n,
        out_shape=[jax.ShapeDtypeStruct((3,) + p.shape[1:], p.dtype) for p in pairs],
        scratch_shapes=[pltpu.SemaphoreType.DMA((3 * n,)), pltpu.SemaphoreType.DMA((3 * n,))],
        compiler_params=pltpu.CompilerParams(has_side_effects=True))(*pairs)


def rs_share(blocks):
    n = len(blocks)

    def body(*refs):
        ins, outs = refs[:n], refs[n:2 * n]
        send_sems, recv_sems = refs[2 * n:]
        x, y, c = lax.axis_index("x"), lax.axis_index("y"), lax.axis_index("c")
        sends = [pltpu.make_async_remote_copy(
            src_ref=_half(ins[t], c), dst_ref=_half(outs[t], c), send_sem=send_sems.at[t], recv_sem=recv_sems.at[t],
            device_id=(x, y, 1 - c), device_id_type=MESH) for t in range(n)]
        for cp in sends:
            cp.start()
        for t in range(n):
            pltpu.make_async_remote_copy(
                src_ref=_half(ins[t], c), dst_ref=_half(outs[t], 1 - c), send_sem=send_sems.at[t],
                recv_sem=recv_sems.at[t], device_id=(x, y, 1 - c), device_id_type=MESH).wait_recv()
        for cp in sends:
            cp.wait_send()

    return pl.pallas_call(
        body, name="rs_share", in_specs=[_any_spec()] * n, out_specs=[_any_spec()] * n,
        out_shape=[jax.ShapeDtypeStruct(b.shape, F32) for b in blocks],
        input_output_aliases={t: t for t in range(n)},
        scratch_shapes=[pltpu.SemaphoreType.DMA((n,)), pltpu.SemaphoreType.DMA((n,))],
        compiler_params=pltpu.CompilerParams(has_side_effects=True))(*blocks)


def _row_tile(rows, target):
    best = None
    for t in range(8, min(rows, target) + 1, 8):
        if rows % t == 0:
            best = t
    return rows if best is None else best


TILE_BYTES = 2 * 1024 * 1024


def _flat_rows(shape):
    rows = int(np.prod(shape[1:-1]))
    cols_padded = -(-shape[-1] // LANES) * LANES
    return rows, _row_tile(rows, max(8, TILE_BYTES // (4 * cols_padded)))


def pair_add(g, from_sibling, out_dtype):
    cols = g.shape[-1]
    rph, t = _flat_rows(from_sibling.shape)
    nt = rph // t
    c_arr = lax.axis_index("c").astype(jnp.int32).reshape(1)

    def body(c_ref, a_ref, b_ref, o_ref):
        o_ref[...] = (a_ref[...] + b_ref[...]).astype(o_ref.dtype)

    out = pl.pallas_call(
        body, name="pair_add",
        grid_spec=pltpu.PrefetchScalarGridSpec(
            num_scalar_prefetch=1, grid=(N_CHIPS, nt),
            in_specs=[pl.BlockSpec((t, cols), lambda j, i, c_ref: (j * 2 * nt + c_ref[0] * nt + i, 0)),
                      pl.BlockSpec((t, cols), lambda j, i, c_ref: (j * nt + i, 0))],
            out_specs=pl.BlockSpec((t, cols), lambda j, i, c_ref: (j * nt + i, 0))),
        out_shape=jax.ShapeDtypeStruct((N_CHIPS * rph, cols), out_dtype),
        compiler_params=_cparams(("parallel", "parallel")))(c_arr, g.reshape(-1, cols), from_sibling.reshape(-1, cols))
    return out.reshape(from_sibling.shape)


def chip_add(pairs, received):
    cols = pairs.shape[-1]
    rph, t = _flat_rows(pairs.shape)
    nt = rph // t
    j_arr = _chip_index().astype(jnp.int32).reshape(1)
    c_arr = lax.axis_index("c").astype(jnp.int32).reshape(1)
    r2 = received.reshape(-1, cols)

    def body(j_ref, c_ref, a_ref, r0_ref, r1_ref, r2_ref, o_ref):
        a, r0, r1, r2 = [ref[...].astype(F32) for ref in (a_ref, r0_ref, r1_ref, r2_ref)]
        o_ref[...] = (a + r0) + (r1 + r2)

    out = pl.pallas_call(
        body, name="chip_add",
        grid_spec=pltpu.PrefetchScalarGridSpec(
            num_scalar_prefetch=2, grid=(nt,),
            in_specs=[pl.BlockSpec((t, cols), lambda i, j_ref, c_ref: (j_ref[0] * nt + i, 0)),
                      pl.BlockSpec((t, cols), lambda i, j_ref, c_ref: (i, 0)),
                      pl.BlockSpec((t, cols), lambda i, j_ref, c_ref: (nt + i, 0)),
                      pl.BlockSpec((t, cols), lambda i, j_ref, c_ref: (2 * nt + i, 0))],
            out_specs=pl.BlockSpec((t, cols), lambda i, j_ref, c_ref: (c_ref[0] * nt + i, 0))),
        out_shape=jax.ShapeDtypeStruct((2 * rph, cols), F32),
        compiler_params=_cparams(("parallel",)))(j_arr, c_arr, pairs.reshape(-1, cols), r2, r2, r2)
    return out.reshape((2 * pairs.shape[1],) + pairs.shape[2:])


def reduce_grads(gs, cross_dtypes):
    pairs = [pair_add(g, r, dt) for g, r, dt in zip(gs, rs_pair(gs), cross_dtypes)]
    return rs_share([chip_add(p, r) for p, r in zip(pairs, rs_cross(pairs))])


def adamw(w, g, m, v):
    shape = w.shape
    cols = shape[-1]
    rows = int(np.prod(shape[:-1]))
    flat = lambda a: a.reshape(rows, cols)
    t = _row_tile(rows, 256)

    def body(w_ref, g_ref, m_ref, v_ref, d_ref, mo_ref, vo_ref):
        gv = g_ref[...]
        m_new = ADAM_B1 * m_ref[...] + (1.0 - ADAM_B1) * gv
        v_new = ADAM_B2 * v_ref[...] + (1.0 - ADAM_B2) * (gv * gv)
        m_hat = m_new / (1.0 - ADAM_B1 ** ADAM_STEP)
        v_hat = v_new / (1.0 - ADAM_B2 ** ADAM_STEP)
        d_ref[...] = -ADAM_LR * (m_hat / (jnp.sqrt(v_hat) + ADAM_EPS) + ADAM_WD * w_ref[...])
        mo_ref[...] = m_new
        vo_ref[...] = v_new

    spec = pl.BlockSpec((t, cols), lambda i: (i, 0))
    outs = pl.pallas_call(
        body, name="adamw", grid=(rows // t,), in_specs=[spec] * 4, out_specs=[spec] * 3,
        out_shape=[jax.ShapeDtypeStruct((rows, cols), F32)] * 3,
        compiler_params=_cparams(("parallel",)))(flat(w), flat(g), flat(m), flat(v))
    return tuple(o.reshape(shape) for o in outs)


SHARDED = (("w_in", 2), ("mla_q_up", 2), ("mla_kv_up", 2), ("gdn_conv_w", 2), ("w_out", 1))
REPLICATED = ("b_mod", "pre_norm_w", "post_norm_w", "mla_q_norm_w", "mla_kv_norm_w", "gdn_a_log", "gdn_dt_bias",
              "gdn_o_norm_w")
WEIGHT_ORDER = ("w_mod", "b_mod", "pre_norm_w", "post_norm_w", "w_in", "mla_q_norm_w", "mla_q_up", "mla_kv_norm_w",
                "mla_kv_up", "gdn_conv_w", "gdn_a_log", "gdn_dt_bias", "gdn_o_norm_w", "w_out")
EXACT_F32 = ("gdn_conv_w",)
SMALL_ROWS = 48


def _gather_weights(shards):
    names = [name for name, _ in SHARDED]
    own = [shards[n] if n in EXACT_F32 else shards[n].astype(BF16) for n in names]
    gathered = ag_weights(own)
    full = {}
    for (name, axis), blk, mine in zip(SHARDED, gathered, own):
        shp = shards[name].shape
        blk = lax.dynamic_update_index_in_dim(blk, mine, _chip_index(), 0)
        blk = jnp.moveaxis(blk.astype(F32), 0, axis)
        full[name] = blk.reshape(shp[:axis] + (N_CHIPS * shp[axis],) + shp[axis + 1:])
    return full


def _split_grads(grads):
    pieces = []
    for name, axis in SHARDED:
        g = grads[name]
        shp = g.shape
        g = g.reshape(shp[:axis] + (N_CHIPS, shp[axis] // N_CHIPS) + shp[axis + 1:])
        pieces.append(jnp.moveaxis(g, axis, 0))
    small = jnp.concatenate([grads[name] for name in REPLICATED], axis=1)
    small = jnp.pad(small, ((0, 0), (0, SMALL_ROWS * LANES - small.shape[1]))).reshape(DEPTH, SMALL_ROWS, LANES)
    pieces.append(jnp.broadcast_to(small[None], (N_CHIPS,) + small.shape))
    return pieces


def _unsplit_small(small, rep_shapes):
    flat = small.reshape(DEPTH, SMALL_ROWS * LANES)
    out, off = {}, 0
    for name in REPLICATED:
        size = rep_shapes[name][1]
        out[name] = flat[:, off:off + size]
        off += size
    return out


MOD_ROWS = 16


def _device_slot():
    return 4 * lax.axis_index("x") + 2 * lax.axis_index("y") + lax.axis_index("c")


def _adaln_projection(c, w_mod_shard):
    (c_act,) = _OPS["cact"]((jnp.pad(c, ((0, 7), (0, 0))),), ())
    c_acts = jnp.pad(all_gather_rows(c_act)[:, 0, :], ((0, MOD_ROWS - 8), (0, 0)))
    part = jnp.concatenate([_mm(c_acts, w_mod_shard[l].astype(BF16), "nn", "mod_proj") for l in range(DEPTH)], axis=0)
    parts = all_gather_rows(part)[::2].reshape(N_CHIPS, DEPTH, MOD_ROWS, -1)
    mods = jnp.moveaxis(parts, 0, 2).reshape(DEPTH, MOD_ROWS, -1)
    return c_acts, lax.dynamic_slice_in_dim(mods, _device_slot(), 1, axis=1)


def _adaln_weight_grad(c_acts, d_mods):
    cols = d_mods.shape[-1] // N_CHIPS
    rows = jnp.pad(d_mods[:, 0, :], ((0, 8 - DEPTH), (0, 0)))
    all_rows = all_gather_rows(rows)[:, :DEPTH, :]
    mine = lax.dynamic_slice_in_dim(all_rows, _chip_index() * cols, cols, axis=2)
    mine = jnp.pad(mine, ((0, MOD_ROWS - 8), (0, 0), (0, 0)))
    return jnp.stack([_mm(c_acts, mine[:, l, :], "tn", "mod_dw") for l in range(DEPTH)])


def kernel(x, c, positions, w_mod, b_mod, pre_norm_w, post_norm_w, w_in, mla_q_norm_w, mla_q_up, mla_kv_norm_w, mla_kv_up, gdn_conv_w, gdn_a_log, gdn_dt_bias, gdn_o_norm_w, w_out, loss_target, m_w_mod, m_b_mod, m_pre_norm_w, m_post_norm_w, m_w_in, m_mla_q_norm_w, m_mla_q_up, m_mla_kv_norm_w, m_mla_kv_up, m_gdn_conv_w, m_gdn_a_log, m_gdn_dt_bias, m_gdn_o_norm_w, m_w_out, v_w_mod, v_b_mod, v_pre_norm_w, v_post_norm_w, v_w_in, v_mla_q_norm_w, v_mla_q_up, v_mla_kv_norm_w, v_mla_kv_up, v_gdn_conv_w, v_gdn_a_log, v_gdn_dt_bias, v_gdn_o_norm_w, v_w_out):
    given = dict(w_mod=w_mod, b_mod=b_mod, pre_norm_w=pre_norm_w, post_norm_w=post_norm_w, w_in=w_in,
                 mla_q_norm_w=mla_q_norm_w, mla_q_up=mla_q_up, mla_kv_norm_w=mla_kv_norm_w, mla_kv_up=mla_kv_up,
                 gdn_conv_w=gdn_conv_w, gdn_a_log=gdn_a_log, gdn_dt_bias=gdn_dt_bias, gdn_o_norm_w=gdn_o_norm_w,
                 w_out=w_out)
    moments_m = dict(w_mod=m_w_mod, b_mod=m_b_mod, pre_norm_w=m_pre_norm_w, post_norm_w=m_post_norm_w, w_in=m_w_in,
                     mla_q_norm_w=m_mla_q_norm_w, mla_q_up=m_mla_q_up, mla_kv_norm_w=m_mla_kv_norm_w,
                     mla_kv_up=m_mla_kv_up, gdn_conv_w=m_gdn_conv_w, gdn_a_log=m_gdn_a_log,
                     gdn_dt_bias=m_gdn_dt_bias, gdn_o_norm_w=m_gdn_o_norm_w, w_out=m_w_out)
    moments_v = dict(w_mod=v_w_mod, b_mod=v_b_mod, pre_norm_w=v_pre_norm_w, post_norm_w=v_post_norm_w, w_in=v_w_in,
                     mla_q_norm_w=v_mla_q_norm_w, mla_q_up=v_mla_q_up, mla_kv_norm_w=v_mla_kv_norm_w,
                     mla_kv_up=v_mla_kv_up, gdn_conv_w=v_gdn_conv_w, gdn_a_log=v_gdn_a_log,
                     gdn_dt_bias=v_gdn_dt_bias, gdn_o_norm_w=v_gdn_o_norm_w, w_out=v_w_out)

    full = _gather_weights({name: given[name] for name, _ in SHARDED})
    for name in REPLICATED:
        full[name] = given[name]
    c_acts, mods = _adaln_projection(c, w_mod)
    loss_local, (grads, d_mods, grad_x) = jax.value_and_grad(_local_loss, argnums=(0, 1, 2))(
        full, mods, x[0], positions[0], loss_target[0])
    loss = lax.psum(loss_local, AXES)

    cross_dtypes = [F32 if name in EXACT_F32 else BF16 for name, _ in SHARDED] + [F32]
    reduced = reduce_grads(_split_grads(grads), cross_dtypes)
    grad_w = {name: g for (name, _), g in zip(SHARDED, reduced)}
    grad_w.update(_unsplit_small(reduced[-1], {name: given[name].shape for name in REPLICATED}))
    grad_w["w_mod"] = _adaln_weight_grad(c_acts, d_mods)
    delta, new_m, new_v = {}, {}, {}
    for name in WEIGHT_ORDER:
        delta[name], new_m[name], new_v[name] = adamw(given[name], grad_w[name], moments_m[name], moments_v[name])
    return (loss, grad_x[None], *[grad_w[n] for n in WEIGHT_ORDER], *[delta[n] for n in WEIGHT_ORDER],
            *[new_m[n] for n in WEIGHT_ORDER], *[new_v[n] for n in WEIGHT_ORDER])
```

```python
import functools
import math

import numpy as np
import jax
import jax.numpy as jnp
from jax import lax
from jax.experimental import pallas as pl
from jax.experimental.pallas import tpu as pltpu

F32 = jnp.float32
BF16 = jnp.bfloat16
MESH = pl.DeviceIdType.MESH
AXES = ("x", "y", "c")

D_MODEL = 1024
DEPTH = 4
MLA_HEADS = 4
MLA_NOPE = 128
MLA_ROPE = 64
MLA_WIDTH = 512
GDN_HEADS = 4
GDN_DK = 128
GDN_WIDTH = 512
GDN_QKV = 1536
GDN_CONV = 4
ROPE_THETA = 10000.0
NORM_EPS = 1e-6
ADAM_LR, ADAM_B1, ADAM_B2, ADAM_EPS, ADAM_WD, ADAM_STEP = 0.001, 0.9, 0.999, 1e-08, 0.01, 10

LANES = 128
N_CHIPS = 4
GDN_CHUNK = 128
VMEM_LIMIT = 56 * 1024 * 1024


def _cparams(sem=None):
    if sem is None:
        return pltpu.CompilerParams(vmem_limit_bytes=VMEM_LIMIT)
    return pltpu.CompilerParams(dimension_semantics=sem, vmem_limit_bytes=VMEM_LIMIT)


def _pick(dim, target):
    if dim <= target:
        return dim
    best = None
    for t in range(LANES, target + 1, LANES):
        if dim % t == 0:
            best = t
    assert best is not None, (dim, target)
    return best


def _resident(shape):
    return pl.BlockSpec(shape, lambda i: (0,) * len(shape), pipeline_mode=pl.Buffered(1))


_DN = {"nn": (((1,), (0,)), ((), ())), "nt": (((1,), (1,)), ((), ())), "tn": (((0,), (0,)), ((), ()))}


def _dot_raw(a, b, mode, exact):
    if exact:
        return lax.dot_general(a, b, _DN[mode], precision=lax.Precision.HIGHEST, preferred_element_type=F32)
    return lax.dot_general(a.astype(BF16), b.astype(BF16), _DN[mode], preferred_element_type=F32)


def _dot_split(a, b):
    a_hi, b_hi = a.astype(BF16), b.astype(BF16)
    a_lo, b_lo = (a - a_hi.astype(F32)).astype(BF16), (b - b_hi.astype(F32)).astype(BF16)
    dot = lambda u, w: lax.dot_general(u, w, _DN["nn"], preferred_element_type=F32)
    return dot(a_hi, b_hi) + (dot(a_hi, b_lo) + dot(a_lo, b_hi))


@functools.partial(jax.custom_vjp, nondiff_argnums=(2, 3))
def bdot(a, b, mode="nn", exact=False):
    return _dot_raw(a, b, mode, exact)


def _bdot_fwd(a, b, mode, exact):
    return _dot_raw(a, b, mode, exact), (a, b)


def _bdot_bwd(mode, exact, res, g):
    a, b = res
    if mode == "nn":
        return bdot(g, b, "nt", exact), bdot(a, g, "tn", exact)
    if mode == "nt":
        return bdot(g, b, "nn", exact), bdot(g, a, "tn", exact)
    return bdot(b, g, "nt", exact), bdot(a, g, "nn", exact)


bdot.defvjp(_bdot_fwd, _bdot_bwd)


@jax.custom_vjp
def roll_half(x):
    return pltpu.roll(x, 64, 1)


roll_half.defvjp(lambda x: (pltpu.roll(x, 64, 1), None), lambda _, g: (pltpu.roll(g, 64, 1),))


def _sigmoid(x):
    return 1.0 / (1.0 + jnp.exp(-x))


def _silu(x):
    return x * _sigmoid(x)


def _softplus(x):
    return jnp.maximum(x, 0.0) + jnp.log(1.0 + jnp.exp(-jnp.abs(x)))


def _rms(x, w):
    return x * lax.rsqrt(jnp.mean(x * x, axis=-1, keepdims=True) + NORM_EPS) * w


def _rw_fwd(fn, name, rows, params, out_dims, out_dtypes, acc_dims, tile):
    s = rows[0].shape[0]
    t = min(tile, s)
    n = s // t
    nr, npar, no, na = len(rows), len(params), len(out_dims), len(acc_dims)

    def body(*refs):
        r, p = refs[:nr], refs[nr:nr + npar]
        o, a = refs[nr + npar:nr + npar + no], refs[nr + npar + no:]
        outs = fn(*[x[...] for x in r], *[x[...] for x in p])
        for ref, val in zip(o, outs[:no]):
            ref[...] = val.astype(ref.dtype)
        if na:
            @pl.when(pl.program_id(0) == 0)
            def _():
                for ref in a:
                    ref[...] = jnp.zeros_like(ref)
            for ref, val in zip(a, outs[no:]):
                ref[...] += val

    in_specs = [pl.BlockSpec((t, x.shape[1]), lambda i: (i, 0)) for x in rows]
    in_specs += [_resident(x.shape) for x in params]
    out_specs = [pl.BlockSpec((t, d), lambda i: (i, 0)) for d in out_dims]
    out_specs += [_resident((1, d)) for d in acc_dims]
    out_shape = [jax.ShapeDtypeStruct((s, d), dt) for d, dt in zip(out_dims, out_dtypes)]
    out_shape += [jax.ShapeDtypeStruct((1, d), F32) for d in acc_dims]
    res = pl.pallas_call(body, name=name, grid=(n,), in_specs=in_specs, out_specs=out_specs, out_shape=out_shape,
                         compiler_params=_cparams(("arbitrary",)))(*rows, *params)
    return tuple(res)


def _rw_bwd(fn, name, rows, params, row_cts, acc_cts, n_diff, tile):
    s = rows[0].shape[0]
    t = min(tile, s)
    n = s // t
    nr, npar, no, na = len(rows), len(params), len(row_cts), len(acc_cts)

    def body(*refs):
        r, p = refs[:nr], refs[nr:nr + npar]
        g, ga = refs[nr + npar:nr + npar + no], refs[nr + npar + no:nr + npar + no + na]
        dr, dp = refs[nr + npar + no + na:nr + npar + no + na + n_diff], refs[nr + npar + no + na + n_diff:]
        _, vjp = jax.vjp(fn, *[x[...] for x in r], *[x[...] for x in p])
        cts = vjp(tuple([x[...] for x in g] + [x[...] for x in ga]))
        for ref, val in zip(dr, cts[:n_diff]):
            ref[...] = val
        if npar:
            @pl.when(pl.program_id(0) == 0)
            def _():
                for ref in dp:
                    ref[...] = jnp.zeros_like(ref)
            for ref, val in zip(dp, cts[nr:]):
                ref[...] += val

    in_specs = [pl.BlockSpec((t, x.shape[1]), lambda i: (i, 0)) for x in rows]
    in_specs += [_resident(x.shape) for x in params]
    in_specs += [pl.BlockSpec((t, x.shape[1]), lambda i: (i, 0)) for x in row_cts]
    in_specs += [_resident(x.shape) for x in acc_cts]
    out_specs = [pl.BlockSpec((t, x.shape[1]), lambda i: (i, 0)) for x in rows[:n_diff]]
    out_specs += [_resident(x.shape) for x in params]
    out_shape = [jax.ShapeDtypeStruct(x.shape, F32) for x in rows[:n_diff]]
    out_shape += [jax.ShapeDtypeStruct(x.shape, F32) for x in params]
    res = pl.pallas_call(body, name=name, grid=(n,), in_specs=in_specs, out_specs=out_specs, out_shape=out_shape,
                         compiler_params=_cparams(("arbitrary",)))(*rows, *params, *row_cts, *acc_cts)
    return tuple(res[:n_diff]), tuple(res[n_diff:])


def make_rowwise(fn, name, out_dims, acc_dims=(), n_nondiff=0, tile=256):
    out_dtypes = (F32,) * len(out_dims)

    @jax.custom_vjp
    def op(rows, params):
        return _rw_fwd(fn, name, rows, params, out_dims, out_dtypes, acc_dims, tile)

    def fwd(rows, params):
        return op(rows, params), (rows, params)

    def bwd(res, cts):
        rows, params = res
        n_diff = len(rows) - n_nondiff
        d_rows, d_params = _rw_bwd(fn, name + "_bwd", rows, params, cts[:len(out_dims)], cts[len(out_dims):],
                                   n_diff, tile)
        d_rows = d_rows + tuple(jnp.zeros_like(x) for x in rows[n_diff:])
        return d_rows, d_params

    op.defvjp(fwd, bwd)
    return op


def _mm(a, b, mode, name):
    if mode == "nn":
        (m, k), (_, n) = a.shape, b.shape
    elif mode == "nt":
        (m, k), (n, _) = a.shape, b.shape
    else:
        (k, m), (_, n) = a.shape, b.shape
    tm = _pick(m, 512)
    tn = _pick(n, 1152)
    tk = _pick(k, 1152) if mode != "tn" else _pick(k, 512)
    nk = k // tk

    def body(a_ref, b_ref, o_ref, acc_ref):
        kk = pl.program_id(2)

        @pl.when(kk == 0)
        def _():
            acc_ref[...] = jnp.zeros_like(acc_ref)

        acc_ref[...] += _dot_raw(a_ref[...], b_ref[...], mode, False)

        @pl.when(kk == nk - 1)
        def _():
            o_ref[...] = acc_ref[...]

    if mode == "nn":
        a_spec = pl.BlockSpec((tm, tk), lambda i, j, kk: (i, kk))
        b_spec = pl.BlockSpec((tk, tn), lambda i, j, kk: (kk, j))
    elif mode == "nt":
        a_spec = pl.BlockSpec((tm, tk), lambda i, j, kk: (i, kk))
        b_spec = pl.BlockSpec((tn, tk), lambda i, j, kk: (j, kk))
    else:
        a_spec = pl.BlockSpec((tk, tm), lambda i, j, kk: (kk, i))
        b_spec = pl.BlockSpec((tk, tn), lambda i, j, kk: (kk, j))
    return pl.pallas_call(
        body, name=name, grid=(m // tm, n // tn, nk), in_specs=[a_spec, b_spec],
        out_specs=pl.BlockSpec((tm, tn), lambda i, j, kk: (i, j)),
        out_shape=jax.ShapeDtypeStruct((m, n), F32), scratch_shapes=[pltpu.VMEM((tm, tn), F32)],
        compiler_params=_cparams(("parallel", "parallel", "arbitrary")))(a, b)


def _mla_prep_fn(qraw, kvraw, kr, t1, t2):
    kr_rot = kr * t1 + roll_half(kr) * t2
    qs, ks = [], []
    for h in range(MLA_HEADS):
        q_r = qraw[:, h * 256 + 128:(h + 1) * 256]
        qs += [qraw[:, h * 256:h * 256 + 128], q_r * t1 + roll_half(q_r) * t2]
        ks += [kvraw[:, h * 128:(h + 1) * 128], kr_rot]
    return jnp.concatenate(qs, axis=1), jnp.concatenate(ks, axis=1), kvraw[:, 512:]


def _flash_tile(s):
    return 512 if s >= 2048 else 128


FLASH_SCALE = (MLA_NOPE + MLA_ROPE) ** -0.5
LOG2_E = 1.4426950408889634
FLASH_EXP2 = FLASH_SCALE * LOG2_E
STAT_ROWS = 8
FLASH_PAIR = 4
FLASH_STRIP = 32


def _as_rows(col_b):
    ones = jnp.full((STAT_ROWS, LANES), 1.0 / LANES, F32)
    return _dot_raw(ones, col_b, "nt", True)


def _flash_fwd(qf, kf, vf):
    s = qf.shape[0]
    t = _flash_tile(s)
    nb = s // t
    pair = range(FLASH_PAIR)

    ck = min(FLASH_STRIP, t)

    def body(q_ref, k_ref, v_ref, o_ref, lse_ref, m_sc, l_sc, acc_sc, st_sc, pt_sc):
        i = pl.program_id(1)
        m_sc[...] = jnp.full_like(m_sc, -1e30)
        l_sc[...] = jnp.zeros_like(l_sc)
        acc_sc[...] = jnp.zeros_like(acc_sc)
        qs = [q_ref[:, hh * 256:(hh + 1) * 256] for hh in pair]

        def step(j, on_diagonal):
            rows = pl.ds(pl.multiple_of(j * t, t), t)
            for hh in pair:
                st = _dot_raw(k_ref[rows, hh * 256:(hh + 1) * 256], qs[hh], "nt", False)
                if on_diagonal:
                    keep = lax.broadcasted_iota(jnp.int32, (t, t), 0) <= lax.broadcasted_iota(jnp.int32, (t, t), 1)
                    st = jnp.where(keep, st, -1e30)
                st_sc[hh] = st
            m_olds = [m_sc[hh] for hh in pair]
            m_news = [jnp.maximum(m_olds[hh], jnp.max(st_sc[hh], axis=0, keepdims=True)) for hh in pair]
            alphas = [jnp.exp2((m_olds[hh] - m_news[hh]) * FLASH_EXP2) for hh in pair]
            shifts = [m_news[hh] * FLASH_EXP2 for hh in pair]
            sums = [jnp.zeros((8, t), F32) for _ in pair]
            for hh in pair:
                for r in range(0, t, ck):
                    p = jnp.exp2(st_sc[hh, r:r + ck, :] * FLASH_EXP2 - shifts[hh])
                    pt_sc[hh, r:r + ck, :] = p.astype(BF16)
                    sums[hh] = sums[hh] + functools.reduce(lambda a, b: a + b, [p[u:u + 8] for u in range(0, ck, 8)])
            pvs = [_dot_raw(v_ref[rows, hh * 128:(hh + 1) * 128], pt_sc[hh], "tn", False) for hh in pair]
            for hh in pair:
                l_sc[hh] = alphas[hh] * l_sc[hh] + jnp.sum(sums[hh], axis=0, keepdims=True)
                acc_sc[hh] = alphas[hh] * acc_sc[hh] + pvs[hh]
                m_sc[hh] = m_news[hh]

        def two_steps(p, carry):
            step(2 * p, False)
            step(2 * p + 1, False)
            return carry

        lax.fori_loop(0, i // 2, two_steps, 0)

        @pl.when(i % 2 == 1)
        def _():
            step(i - 1, False)

        step(i, True)
        for hh in pair:
            o_ref[:, hh * 128:(hh + 1) * 128] = (acc_sc[hh] / l_sc[hh]).T
            lse2 = m_sc[hh] * FLASH_EXP2 + jnp.log(l_sc[hh]) * LOG2_E
            lse_ref[hh] = jnp.broadcast_to(lse2, (STAT_ROWS, t))

    p = FLASH_PAIR
    return pl.pallas_call(
        body, name="flash_fwd", grid=(MLA_HEADS // p, nb),
        in_specs=[pl.BlockSpec((t, p * 256), lambda h, i: (i, h)),
                  pl.BlockSpec((s, p * 256), lambda h, i: (0, h), pipeline_mode=pl.Buffered(1)),
                  pl.BlockSpec((s, p * 128), lambda h, i: (0, h), pipeline_mode=pl.Buffered(1))],
        out_specs=[pl.BlockSpec((t, p * 128), lambda h, i: (i, h)),
                   pl.BlockSpec((p, STAT_ROWS, t), lambda h, i: (h, 0, i))],
        out_shape=[jax.ShapeDtypeStruct((s, MLA_WIDTH), F32),
                   jax.ShapeDtypeStruct((MLA_HEADS, STAT_ROWS, s), F32)],
        scratch_shapes=[pltpu.VMEM((p, 1, t), F32), pltpu.VMEM((p, 1, t), F32), pltpu.VMEM((p, 128, t), F32),
                        pltpu.VMEM((p, t, t), F32), pltpu.VMEM((p, t, t), BF16)],
        compiler_params=_cparams(("parallel", "arbitrary")))(qf, kf, vf)


def _flash_bwd(qf, kf, vf, lse, o, do):
    s = qf.shape[0]
    t = _flash_tile(s)
    nb = s // t

    def body(q_ref, k_ref, v_ref, lse_ref, o_ref, do_ref, dq_ref, dk_ref, dv_ref, dl_sc, do16_sc):
        j = pl.program_id(1)

        @pl.when(j == 0)
        def _():
            dq_ref[...] = jnp.zeros_like(dq_ref)

            def stage(b, carry):
                rows = pl.ds(pl.multiple_of(b * t, t), t)
                do_t = do_ref[rows, :]
                delta = jnp.sum(do_t * o_ref[rows, :], axis=1, keepdims=True)
                dl_sc[:, rows] = _as_rows(jnp.broadcast_to(delta, (t, LANES)))
                do16_sc[rows, :] = do_t.astype(BF16)
                return carry

            lax.fori_loop(0, nb, stage, 0, unroll=min(4, nb))

        dk_ref[...] = jnp.zeros_like(dk_ref)
        dv_ref[...] = jnp.zeros_like(dv_ref)
        k = k_ref[...]
        v = v_ref[...]

        def step(i, on_diagonal):
            rows = pl.ds(pl.multiple_of(i * t, t), t)
            q = q_ref[rows, :]
            do_t = do16_sc[rows, :]
            st = _dot_raw(k, q, "nt", False) * FLASH_EXP2 - lse_ref[0, 0:1, rows]
            if on_diagonal:
                keep = lax.broadcasted_iota(jnp.int32, (t, t), 0) <= lax.broadcasted_iota(jnp.int32, (t, t), 1)
                st = jnp.where(keep, st, -1e30)
            pt = jnp.exp2(st)
            dst = pt * (_dot_raw(v, do_t, "nt", False) - dl_sc[0:1, rows])
            dv_ref[...] += _dot_raw(pt, do_t, "nn", False)
            dk_ref[...] += _dot_raw(dst, q, "nn", False)
            dq_ref[rows, :] += _dot_raw(dst, k, "tn", False)

        def four_steps(p, carry):
            for u in range(4):
                step(j + 1 + 4 * p + u, False)
            return carry

        def one_step(r, carry):
            step(nb - 1 - r, False)
            return carry

        step(j, True)
        below = nb - 1 - j
        lax.fori_loop(0, below // 4, four_steps, 0)
        lax.fori_loop(0, below % 4, one_step, 0)

        dk_ref[...] *= FLASH_SCALE

        @pl.when(j == nb - 1)
        def _():
            dq_ref[...] *= FLASH_SCALE

    per_head = lambda d: pl.BlockSpec((s, d), lambda h, j: (0, h), pipeline_mode=pl.Buffered(1))
    return pl.pallas_call(
        body, name="flash_bwd", grid=(MLA_HEADS, nb),
        in_specs=[per_head(256), pl.BlockSpec((t, 256), lambda h, j: (j, h)),
                  pl.BlockSpec((t, 128), lambda h, j: (j, h)),
                  pl.BlockSpec((1, STAT_ROWS, s), lambda h, j: (h, 0, 0)), per_head(128), per_head(128)],
        out_specs=[pl.BlockSpec((s, 256), lambda h, j: (0, h)), pl.BlockSpec((t, 256), lambda h, j: (j, h)),
                   pl.BlockSpec((t, 128), lambda h, j: (j, h))],
        out_shape=[jax.ShapeDtypeStruct((s, 1024), F32), jax.ShapeDtypeStruct((s, 1024), F32),
                   jax.ShapeDtypeStruct((s, MLA_WIDTH), F32)],
        scratch_shapes=[pltpu.VMEM((STAT_ROWS, s), F32), pltpu.VMEM((s, 128), BF16)],
        compiler_params=_cparams(("parallel", "arbitrary")))(qf, kf, vf, lse, o, do)


MLA_FRONT_TILE = 512


def _mla_front_fn(q_lat, kv_lat, kr, t1, t2, q_norm_w, q_up, kv_norm_w, kv_up):
    qraw = bdot(_rms(q_lat, q_norm_w), q_up)
    kvraw = bdot(_rms(kv_lat, kv_norm_w), kv_up)
    return _mla_prep_fn(qraw, kvraw, kr, t1, t2)


def _mla_front(rows, params):
    return _rw_fwd(_mla_front_fn, "mla_front", rows, params, (1024, 1024, 512), (BF16, BF16, BF16), (),
                   MLA_FRONT_TILE)


@jax.custom_vjp
def mla_attention(rows, params):
    return _flash_fwd(*_mla_front(rows, params))[0]


def _mla_attention_fwd(rows, params):
    qf, kf, vf = _mla_front(rows, params)
    o, lse = _flash_fwd(qf, kf, vf)
    return o, (rows, params, qf, kf, vf, o, lse)


def _mla_attention_bwd(res, do):
    rows, params, qf, kf, vf, o, lse = res
    d_rows, d_params = _rw_bwd(_mla_front_fn, "mla_front_bwd", rows, params, _flash_bwd(qf, kf, vf, lse, o, do), (),
                               3, MLA_FRONT_TILE)
    return d_rows + (jnp.zeros_like(rows[3]), jnp.zeros_like(rows[4])), d_params


mla_attention.defvjp(_mla_attention_fwd, _mla_attention_bwd)


def _lane_pick(x, lane):
    ids = lax.broadcasted_iota(jnp.int32, x.shape, 1)
    col = jnp.sum(jnp.where(ids == lane, x, 0.0), axis=1, keepdims=True)
    return jnp.broadcast_to(col, x.shape)


GDN_HALO = 8


@functools.partial(jax.custom_vjp, nondiff_argnums=(1,))
def _roll_rows(x, d):
    return pltpu.roll(x, d, 0)


_roll_rows.defvjp(lambda x, d: (pltpu.roll(x, d, 0), None), lambda d, _, g: (pltpu.roll(g, g.shape[0] - d, 0),))


def _gdn_prep_fn(prev, cur, ab, w0, w1, w2, w3, a_log, dt_bias):
    xcat = jnp.concatenate([prev, cur], axis=0)
    x0, x1, x2 = [_roll_rows(xcat, GDN_CONV - 1 - j)[GDN_HALO:] for j in range(GDN_CONV - 1)]
    qkv = _silu(x0 * w0 + x1 * w1 + x2 * w2 + cur * w3)
    g_all = -jnp.exp(a_log) * _softplus(ab + dt_bias)
    beta_all = _sigmoid(ab)
    qs, ks, gs, bs = [], [], [], []
    for h in range(GDN_HEADS):
        q = qkv[:, h * 128:(h + 1) * 128]
        k = qkv[:, 512 + h * 128:512 + (h + 1) * 128]
        qs.append(q * lax.rsqrt(jnp.sum(q * q, axis=-1, keepdims=True) + NORM_EPS) * (GDN_DK ** -0.5))
        ks.append(k * lax.rsqrt(jnp.sum(k * k, axis=-1, keepdims=True) + NORM_EPS))
        gs.append(_lane_pick(g_all, h))
        bs.append(_lane_pick(beta_all, GDN_HEADS + h))
    cat = lambda xs: jnp.concatenate(xs, axis=1)
    return cat(qs), cat(ks), qkv[:, 1024:], cat(gs), cat(bs)


GDN_PREP_TILE = 256


def _gdn_prep_specs(s, params, reverse=False):
    t = min(GDN_PREP_TILE, s)
    n = s // t
    blk = (lambda i: n - 1 - i) if reverse else (lambda i: i)
    prev = pl.BlockSpec((GDN_HALO, GDN_QKV), lambda i: (jnp.maximum(blk(i) * (t // GDN_HALO) - 1, 0), 0))
    rows = lambda d: pl.BlockSpec((t, d), lambda i: (blk(i), 0))
    return t, rows, [prev, rows(GDN_QKV), rows(LANES)] + [pl.BlockSpec(p.shape, lambda i: (0, 0)) for p in params]


def _gdn_prep_masked(has_rows_before):
    return lambda prev, *rest: _gdn_prep_fn(prev * has_rows_before, *rest)


def _gdn_prep_fwd(qkv, ab, params):
    s = qkv.shape[0]
    t, rows, in_specs = _gdn_prep_specs(s, params)

    def body(prev_ref, cur_ref, ab_ref, *refs):
        p_refs, o_refs = refs[:len(params)], refs[len(params):]
        has_rows_before = (pl.program_id(0) > 0).astype(F32)
        outs = _gdn_prep_masked(has_rows_before)(prev_ref[...], cur_ref[...], ab_ref[...], *[p[...] for p in p_refs])
        for ref, val in zip(o_refs, outs):
            ref[...] = val

    return pl.pallas_call(
        body, name="gdn_prep", grid=(s // t,), in_specs=in_specs,
        out_specs=[rows(GDN_WIDTH)] * 5, out_shape=[jax.ShapeDtypeStruct((s, GDN_WIDTH), F32)] * 5,
        compiler_params=_cparams(("parallel",)))(qkv, qkv, ab, *params)


def _gdn_prep_bwd(qkv, ab, params, cts):
    s = qkv.shape[0]
    t, rows, in_specs = _gdn_prep_specs(s, params, reverse=True)
    n = s // t
    npar = len(params)

    def body(prev_ref, cur_ref, ab_ref, *refs):
        p_refs, g_refs = refs[:npar], refs[npar:npar + 5]
        dcur_ref, dab_ref = refs[npar + 5:npar + 7]
        dp_refs, carry_sc = refs[npar + 7:-1], refs[-1]

        @pl.when(pl.program_id(0) == 0)
        def _():
            carry_sc[...] = jnp.zeros_like(carry_sc)
            for ref in dp_refs:
                ref[...] = jnp.zeros_like(ref)

        has_rows_before = (pl.program_id(0) < n - 1).astype(F32)
        _, vjp = jax.vjp(_gdn_prep_masked(has_rows_before), prev_ref[...], cur_ref[...], ab_ref[...],
                         *[p[...] for p in p_refs])
        d_prev, d_cur, d_ab, *d_params = vjp(tuple(g[...] for g in g_refs))
        dcur_ref[...] = d_cur
        dcur_ref[t - GDN_HALO:, :] += carry_sc[...]
        carry_sc[...] = d_prev
        dab_ref[...] = d_ab
        for ref, val in zip(dp_refs, d_params):
            ref[...] += val

    res = pl.pallas_call(
        body, name="gdn_prep_bwd", grid=(n,), in_specs=in_specs + [rows(GDN_WIDTH)] * 5,
        out_specs=[rows(GDN_QKV), rows(LANES)] + [pl.BlockSpec(p.shape, lambda i: (0, 0)) for p in params],
        out_shape=[jax.ShapeDtypeStruct((s, GDN_QKV), F32), jax.ShapeDtypeStruct((s, LANES), F32)]
        + [jax.ShapeDtypeStruct(p.shape, F32) for p in params],
        scratch_shapes=[pltpu.VMEM((GDN_HALO, GDN_QKV), F32)],
        compiler_params=_cparams(("arbitrary",)))(qkv, qkv, ab, *params, *cts)
    return res[0], res[1], tuple(res[2:])


@jax.custom_vjp
def gdn_prep(qkv, ab, params):
    return tuple(_gdn_prep_fwd(qkv, ab, params))


def _gdn_prep_vjp_fwd(qkv, ab, params):
    return tuple(_gdn_prep_fwd(qkv, ab, params)), (qkv, ab, params)


def _gdn_prep_vjp_bwd(res, cts):
    qkv, ab, params = res
    return _gdn_prep_bwd(qkv, ab, params, cts)


gdn_prep.defvjp(_gdn_prep_vjp_fwd, _gdn_prep_vjp_bwd)


def _tri_dot(x, mode):
    c = x.shape[0]
    tri = (lax.broadcasted_iota(jnp.int32, (c, c), 0) >= lax.broadcasted_iota(jnp.int32, (c, c), 1)).astype(BF16)
    hi = x.astype(BF16)
    rest = x - hi.astype(F32)
    mid = rest.astype(BF16)
    lo = (rest - mid.astype(F32)).astype(BF16)
    dot = lambda part: lax.dot_general(tri, part, _DN[mode], preferred_element_type=F32)
    return dot(hi) + (dot(mid) + dot(lo))


@jax.custom_vjp
def _chunk_cumsum(x):
    return _tri_dot(x, "nn")


_chunk_cumsum.defvjp(lambda x: (_tri_dot(x, "nn"), None), lambda _, g: (_tri_dot(g, "tn"),))


@jax.custom_vjp
def _unit_lower_inverse(lms):
    c = lms[0].shape[0]
    row = lax.broadcasted_iota(jnp.int32, (c, c), 0)
    col = lax.broadcasted_iota(jnp.int32, (c, c), 1)
    ts = [(row == col).astype(F32) - jnp.where((row >> 1) == (col >> 1), lm, 0.0) for lm in lms]
    for level in range(1, int(math.log2(c))):
        below = ((row >> (level + 1)) == (col >> (level + 1))) & ((row >> level) != (col >> level))
        mids = [_dot_split(t, jnp.where(below, lm, 0.0)) for t, lm in zip(ts, lms)]
        ts = [t - _dot_split(mid, t) for t, mid in zip(ts, mids)]
    return tuple(ts)


def _uli_fwd(lms):
    ts = _unit_lower_inverse(lms)
    return ts, ts


def _uli_bwd(ts, gs):
    mids = [bdot(t, g, "tn") for t, g in zip(ts, gs)]
    return (tuple(-bdot(mid, t, "nt") for t, mid in zip(ts, mids)),)


_unit_lower_inverse.defvjp(_uli_fwd, _uli_bwd)


@jax.custom_vjp
def _known_inverse(lms, ts):
    return ts


_known_inverse.defvjp(lambda lms, ts: (ts, ts),
                      lambda ts, gs: (_uli_bwd(ts, gs)[0], tuple(jnp.zeros_like(t) for t in ts)))


GDN_STEP_CHUNKS = 4


def _gdn_chunk_fn(qs, ks, vs, gbs, bbs, s0s, known_ts=None):
    chains, n_heads = range(len(qs)), len(s0s)
    c = qs[0].shape[0]
    row = lax.broadcasted_iota(jnp.int32, (c, c), 0)
    col = lax.broadcasted_iota(jnp.int32, (c, c), 1)
    incl, strict = row >= col, row > col
    gc = [_chunk_cumsum(gbs[i]) for i in chains]
    decay = [jnp.exp(jnp.where(incl, gc[i] - gc[i].T, -1e30)) for i in chains]
    g_last = [jnp.sum(gbs[i], axis=0, keepdims=True) for i in chains]
    eg = [jnp.exp(gc[i]) for i in chains]
    kb = [ks[i] * bbs[i] for i in chains]
    lms = tuple(jnp.where(strict, bdot(kb[i], ks[i], "nt") * decay[i], 0.0) for i in chains)
    ts = _unit_lower_inverse(lms) if known_ts is None else _known_inverse(lms, known_ts)
    u = [bdot(ts[i], vs[i] * bbs[i]) for i in chains]
    w = [bdot(ts[i], kb[i] * eg[i]) for i in chains]
    qk = [bdot(qs[i], ks[i], "nt") * decay[i] for i in chains]
    q_dec = [qs[i] * eg[i] for i in chains]
    k_dec = [ks[i] * jnp.exp(g_last[i] - gc[i]) for i in chains]
    states, outs = list(s0s), []
    for first in range(0, len(qs), n_heads):
        here = range(first, first + n_heads)
        v_new = [u[i] - bdot(w[i], states[i - first]) for i in here]
        outs += [bdot(q_dec[i], states[i - first]) + bdot(qk[i], v_new[i - first]) for i in here]
        states = [states[i - first] * jnp.exp(g_last[i]) + bdot(k_dec[i], v_new[i - first], "tn") for i in here]
    return (tuple(outs), tuple(states)), ts


def _chain_tiles(ref, c):
    return tuple(ref[p * c:(p + 1) * c, h * 128:(h + 1) * 128]
                 for p in range(ref.shape[0] // c) for h in range(GDN_HEADS))


def _gdn_step(s):
    c = min(GDN_CHUNK, s)
    p = min(GDN_STEP_CHUNKS, s // c)
    return c, p, s // (c * p)


def _gdn_fwd(q, k, v, gb, bb):
    s = q.shape[0]
    c, p, steps = _gdn_step(s)

    def body(q_ref, k_ref, v_ref, g_ref, b_ref, o_ref, st_ref, inv_ref, s_sc):
        @pl.when(pl.program_id(0) == 0)
        def _():
            s_sc[...] = jnp.zeros_like(s_sc)

        s0s = tuple(s_sc[h] for h in range(GDN_HEADS))
        for h in range(GDN_HEADS):
            st_ref[h, 0] = s0s[h]
        (os, s1s), ts = _gdn_chunk_fn(*[_chain_tiles(ref, c) for ref in (q_ref, k_ref, v_ref, g_ref, b_ref)], s0s)
        for i in range(p * GDN_HEADS):
            cc, h = divmod(i, GDN_HEADS)
            o_ref[cc * c:(cc + 1) * c, h * 128:(h + 1) * 128] = os[i]
            inv_ref[h, cc] = ts[i]
        for h in range(GDN_HEADS):
            s_sc[h] = s1s[h]

    blk = pl.BlockSpec((p * c, GDN_WIDTH), lambda n: (n, 0))
    return pl.pallas_call(
        body, name="gdn_fwd", grid=(steps,), in_specs=[blk] * 5,
        out_specs=[blk, pl.BlockSpec((GDN_HEADS, 1, 128, 128), lambda n: (0, n, 0, 0)),
                   pl.BlockSpec((GDN_HEADS, p, c, c), lambda n: (0, n, 0, 0))],
        out_shape=[jax.ShapeDtypeStruct((s, GDN_WIDTH), F32), jax.ShapeDtypeStruct((GDN_HEADS, steps, 128, 128), F32),
                   jax.ShapeDtypeStruct((GDN_HEADS, steps * p, c, c), F32)],
        scratch_shapes=[pltpu.VMEM((GDN_HEADS, 128, 128), F32)],
        compiler_params=_cparams(("arbitrary",)))(q, k, v, gb, bb)


def _gdn_bwd(q, k, v, gb, bb, states, inverses, do):
    s = q.shape[0]
    c, p, steps = _gdn_step(s)

    def body(q_ref, k_ref, v_ref, g_ref, b_ref, st_ref, inv_ref, do_ref, dq_ref, dk_ref, dv_ref, dg_ref, db_ref, ds_sc):
        @pl.when(pl.program_id(0) == 0)
        def _():
            ds_sc[...] = jnp.zeros_like(ds_sc)

        s0s = tuple(st_ref[h, 0] for h in range(GDN_HEADS))
        ts = tuple(inv_ref[h, cc] for cc in range(p) for h in range(GDN_HEADS))
        chunks = lambda *args: _gdn_chunk_fn(*args, known_ts=ts)[0]
        _, vjp = jax.vjp(chunks, *[_chain_tiles(ref, c) for ref in (q_ref, k_ref, v_ref, g_ref, b_ref)], s0s)
        *d_tiles, ds0s = vjp((_chain_tiles(do_ref, c), tuple(ds_sc[h] for h in range(GDN_HEADS))))
        for i in range(p * GDN_HEADS):
            cc, h = divmod(i, GDN_HEADS)
            for ref, d in zip((dq_ref, dk_ref, dv_ref, dg_ref, db_ref), d_tiles):
                ref[cc * c:(cc + 1) * c, h * 128:(h + 1) * 128] = d[i]
        for h in range(GDN_HEADS):
            ds_sc[h] = ds0s[h]

    blk = pl.BlockSpec((p * c, GDN_WIDTH), lambda n: (steps - 1 - n, 0))
    return pl.pallas_call(
        body, name="gdn_bwd", grid=(steps,),
        in_specs=[blk] * 5 + [pl.BlockSpec((GDN_HEADS, 1, 128, 128), lambda n: (0, steps - 1 - n, 0, 0)),
                              pl.BlockSpec((GDN_HEADS, p, c, c), lambda n: (0, steps - 1 - n, 0, 0)), blk],
        out_specs=[blk] * 5, out_shape=[jax.ShapeDtypeStruct((s, GDN_WIDTH), F32)] * 5,
        scratch_shapes=[pltpu.VMEM((GDN_HEADS, 128, 128), F32)],
        compiler_params=_cparams(("arbitrary",)))(q, k, v, gb, bb, states, inverses, do)


@jax.custom_vjp
def gdn_core(q, k, v, gb, bb):
    return _gdn_fwd(q, k, v, gb, bb)[0]


def _gdn_core_fwd(q, k, v, gb, bb):
    o, states, inverses = _gdn_fwd(q, k, v, gb, bb)
    return o, (q, k, v, gb, bb, states, inverses)


def _gdn_core_bwd(res, do):
    return tuple(_gdn_bwd(*res, do))


gdn_core.defvjp(_gdn_core_fwd, _gdn_core_bwd)


def _cact_fn(c):
    return (_silu(c),)


def _prenorm_fn(x, w, scale_raw, scale_b, shift_raw, shift_b):
    return (_rms(x, w) * (1.0 + scale_raw + scale_b) + shift_raw + shift_b,)


def _tail_fn(x, o_mla, z_mla, o_gdn, z_gdn, o_norm_w, w_out, post_w, gate_raw, gate_b):
    y_mla = o_mla * _silu(z_mla)
    parts = [_rms(o_gdn[:, h * 128:(h + 1) * 128], o_norm_w) for h in range(GDN_HEADS)]
    y_gdn = jnp.concatenate(parts, axis=1) * _silu(z_gdn)
    y = bdot(y_mla, w_out[:MLA_WIDTH]) + bdot(y_gdn, w_out[MLA_WIDTH:])
    return (x + (gate_raw + gate_b) * _rms(y, post_w),)


def _loss_fn(y, tgt):
    err = y - tgt
    part = jnp.sum(0.5 * jnp.mean(err * err, axis=-1, keepdims=True), axis=0, keepdims=True)
    return (jnp.broadcast_to(part, (1, LANES)),)


def _front_fwd(x, norm_params, ws16):
    s = x.shape[0]
    t = min(512, s)

    def body(x_ref, *refs):
        p_refs, w_refs, o_refs = refs[:len(norm_params)], refs[len(norm_params):-len(ws16)], refs[-len(ws16):]
        (h,) = _prenorm_fn(x_ref[...], *[p[...] for p in p_refs])
        h16 = h.astype(BF16)
        for w_ref, o_ref in zip(w_refs, o_refs):
            o_ref[...] = _dot_raw(h16, w_ref[...], "nn", False)

    rows = lambda d: pl.BlockSpec((t, d), lambda i: (i, 0))
    res = pl.pallas_call(
        body, name="front", grid=(s // t,),
        in_specs=[rows(D_MODEL)] + [_resident(p.shape) for p in norm_params] + [_resident(w.shape) for w in ws16],
        out_specs=[rows(w.shape[1]) for w in ws16],
        out_shape=[jax.ShapeDtypeStruct((s, w.shape[1]), F32) for w in ws16],
        compiler_params=_cparams(("parallel",)))(x, *norm_params, *ws16)
    return tuple(res)


def _front_bwd(x, norm_params, ws16, dys, dx_skip):
    s = x.shape[0]
    t = min(512, s)
    npar, ng = len(norm_params), len(ws16)

    def body(x_ref, *refs):
        p_refs, w_refs, g_refs = refs[:npar], refs[npar:npar + ng], refs[npar + ng:npar + 2 * ng]
        skip_ref, dx_ref = refs[npar + 2 * ng], refs[npar + 2 * ng + 1]
        dp_refs, dw_refs = refs[npar + 2 * ng + 2:2 * npar + 2 * ng + 2], refs[2 * npar + 2 * ng + 2:]

        @pl.when(pl.program_id(0) == 0)
        def _():
            for ref in dp_refs + dw_refs:
                ref[...] = jnp.zeros_like(ref)

        (h,), vjp = jax.vjp(_prenorm_fn, x_ref[...], *[p[...] for p in p_refs])
        h16 = h.astype(BF16)
        dys16 = [g[...].astype(BF16) for g in g_refs]
        dh = functools.reduce(lambda a, b: a + b,
                              [_dot_raw(dy, w_ref[...], "nt", False) for dy, w_ref in zip(dys16, w_refs)])
        for dy, dw_ref in zip(dys16, dw_refs):
            dw_ref[...] += _dot_raw(h16, dy, "tn", False)
        dx, *d_params = vjp((dh,))
        dx_ref[...] = dx + skip_ref[...]
        for ref, val in zip(dp_refs, d_params):
            ref[...] += val

    rows = lambda d: pl.BlockSpec((t, d), lambda i: (i, 0))
    res = pl.pallas_call(
        body, name="front_bwd", grid=(s // t,),
        in_specs=[rows(D_MODEL)] + [_resident(p.shape) for p in norm_params] + [_resident(w.shape) for w in ws16]
        + [rows(w.shape[1]) for w in ws16] + [rows(D_MODEL)],
        out_specs=[rows(D_MODEL)] + [_resident(p.shape) for p in norm_params] + [_resident(w.shape) for w in ws16],
        out_shape=[jax.ShapeDtypeStruct(x.shape, F32)] + [jax.ShapeDtypeStruct(p.shape, F32) for p in norm_params]
        + [jax.ShapeDtypeStruct(w.shape, F32) for w in ws16],
        compiler_params=_cparams(("arbitrary",)))(x, *norm_params, *ws16, *dys, dx_skip)
    return res[0], tuple(res[1:1 + npar]), tuple(res[1 + npar:])


@jax.custom_vjp
def front(x, norm_params, ws):
    return _front_fwd(x, norm_params, tuple(w.astype(BF16) for w in ws)) + (x,)


def _front_vjp_fwd(x, norm_params, ws):
    ws16 = tuple(w.astype(BF16) for w in ws)
    return _front_fwd(x, norm_params, ws16) + (x,), (x, norm_params, ws16)


def _front_vjp_bwd(res, cts):
    return _front_bwd(*res, cts[:-1], cts[-1])


front.defvjp(_front_vjp_fwd, _front_vjp_bwd)


_OPS = dict(
    cact=make_rowwise(_cact_fn, "c_act", (D_MODEL,), tile=16),
    tail=make_rowwise(_tail_fn, "tail", (D_MODEL,), tile=512),
    loss=make_rowwise(_loss_fn, "loss", (), acc_dims=(LANES,), n_nondiff=1, tile=512),
)


def _swap_halves(w):
    half = w.shape[-1] // 2
    return jnp.concatenate([w[..., half:], w[..., :half]], axis=-1)


def _w_in_groups(w):
    k_pe = w[:, 640:704]
    ab = jnp.concatenate([w[:, 2752:2760], jnp.zeros((w.shape[0], LANES - 8), w.dtype)], axis=1)
    return (w[:, :384], w[:, 384:640], jnp.concatenate([k_pe, _swap_halves(k_pe)], axis=1), w[:, 704:1216],
            w[:, 1216:2752], ab, w[:, 2760:])


def _q_up_ext(w):
    parts = []
    for h in range(MLA_HEADS):
        rope = w[:, h * 192 + 128:(h + 1) * 192]
        parts += [w[:, h * 192:h * 192 + 128], rope, _swap_halves(rope)]
    return jnp.concatenate(parts, axis=1)


def _kv_up_perm(w):
    ks = [w[:, h * 256:h * 256 + 128] for h in range(MLA_HEADS)]
    vs = [w[:, h * 256 + 128:(h + 1) * 256] for h in range(MLA_HEADS)]
    return jnp.concatenate(ks + vs, axis=1)


def _pad_lanes(v):
    return jnp.pad(v, (0, LANES - v.shape[0]))[None, :]


def _local_loss(weights, mods, x, positions, target):
    s = x.shape[0]
    half = MLA_ROPE // 2
    inv_freq = jnp.power(ROPE_THETA, -jnp.arange(half, dtype=F32) * 2.0 / MLA_ROPE)
    ang = positions.astype(F32)[:, None] * inv_freq
    cos, sin, zero = jnp.cos(ang), jnp.sin(ang), jnp.zeros((s, 2 * half), F32)
    t1 = jnp.concatenate([cos, cos, zero], axis=1)
    t2 = jnp.concatenate([-sin, sin, zero], axis=1)

    for l in range(DEPTH):
        mod = mods[l]
        b = weights["b_mod"][l][None, :]
        shift_raw, scale_raw, gate_raw = mod[:, :1024], mod[:, 1024:2048], mod[:, 2048:]
        shift_b, scale_b, gate_b = b[:, :1024], b[:, 1024:2048], b[:, 2048:]
        q_lat, kv_lat, kr, z_mla, qkv, ab, z_gdn, x = front(
            x, (weights["pre_norm_w"][l][None], scale_raw, scale_b, shift_raw, shift_b),
            _w_in_groups(weights["w_in"][l]))
        o_mla = mla_attention((q_lat, kv_lat, kr, t1, t2),
                              (weights["mla_q_norm_w"][l][None], _q_up_ext(weights["mla_q_up"][l]),
                               weights["mla_kv_norm_w"][l][None], _kv_up_perm(weights["mla_kv_up"][l])))
        cw = weights["gdn_conv_w"][l]
        params = tuple(cw[j][None] for j in range(GDN_CONV))
        params += (_pad_lanes(weights["gdn_a_log"][l]), _pad_lanes(weights["gdn_dt_bias"][l]))
        o_gdn = gdn_core(*gdn_prep(qkv, ab, params))
        (x,) = _OPS["tail"]((x, o_mla, z_mla, o_gdn, z_gdn),
                            (weights["gdn_o_norm_w"][l][None], weights["w_out"][l], weights["post_norm_w"][l][None],
                             gate_raw, gate_b))
    (acc,) = _OPS["loss"]((x, target), ())
    return acc[0, 0]


def _chip_index():
    return 2 * lax.axis_index("x") + lax.axis_index("y")


def _other_chips(x, y):
    return [(1 - x, y), (x, 1 - y), (1 - x, 1 - y)]


def _any_spec():
    return pl.BlockSpec(memory_space=pl.ANY)


def _half(ref, hc):
    n = ref.shape[0] // 2
    return ref.at[pl.ds(hc * n, n)]


def ag_weights(shards):
    n = len(shards)

    def body(*refs):
        ins, outs = refs[:n], refs[n:2 * n]
        send_sems, recv_sems = refs[2 * n:]
        x, y, c = lax.axis_index("x"), lax.axis_index("y"), lax.axis_index("c")
        sibling = (x, y, 1 - c)
        chips = _other_chips(x, y)

        def copy(t, k, src, chip_xy, hc, to):
            return pltpu.make_async_remote_copy(
                src_ref=src, dst_ref=_half(outs[t].at[2 * chip_xy[0] + chip_xy[1]], hc),
                send_sem=send_sems.at[6 * t + k], recv_sem=recv_sems.at[6 * t + k], device_id=to, device_id_type=MESH)

        first = [copy(t, k, _half(ins[t], c), (x, y), c, (*chip, c)) for k, chip in enumerate(chips) for t in range(n)]
        for cp in first:
            cp.start()
        passed = []
        for k, chip in enumerate(chips):
            for t in range(n):
                landed = _half(outs[t].at[2 * chip[0] + chip[1]], c)
                copy(t, k, landed, chip, c, (x, y, c)).wait_recv()
                passed.append(copy(t, 3 + k, landed, chip, c, sibling))
                passed[-1].start()
        for k, chip in enumerate(chips):
            for t in range(n):
                copy(t, 3 + k, _half(ins[t], c), chip, 1 - c, (x, y, c)).wait_recv()
        for cp in first + passed:
            cp.wait_send()

    return pl.pallas_call(
        body, name="ag_weights", in_specs=[_any_spec()] * n, out_specs=[_any_spec()] * n,
        out_shape=[jax.ShapeDtypeStruct((N_CHIPS,) + a.shape, a.dtype) for a in shards],
        scratch_shapes=[pltpu.SemaphoreType.DMA((6 * n,)), pltpu.SemaphoreType.DMA((6 * n,))],
        compiler_params=pltpu.CompilerParams(has_side_effects=True))(*shards)


def all_gather_rows(x):
    r, cols = x.shape
    flips = [(k >> 2 & 1, k >> 1 & 1, k & 1) for k in range(1, 8)]

    def body(x_ref, out_ref, send_sems, recv_sems):
        here = (lax.axis_index("x"), lax.axis_index("y"), lax.axis_index("c"))
        peers = [tuple(1 - p if f else p for p, f in zip(here, flip)) for flip in flips]
        slot = lambda dev: 4 * dev[0] + 2 * dev[1] + dev[2]
        out_ref[slot(here)] = x_ref[...]
        copies = [pltpu.make_async_remote_copy(
            src_ref=x_ref, dst_ref=out_ref.at[slot(here)], send_sem=send_sems.at[k], recv_sem=recv_sems.at[k],
            device_id=peer, device_id_type=MESH) for k, peer in enumerate(peers)]
        for cp in copies:
            cp.start()
        for k, peer in enumerate(peers):
            pltpu.make_async_remote_copy(
                src_ref=x_ref, dst_ref=out_ref.at[slot(peer)], send_sem=send_sems.at[k], recv_sem=recv_sems.at[k],
                device_id=peer, device_id_type=MESH).wait_recv()
        for cp in copies:
            cp.wait_send()

    return pl.pallas_call(
        body, name="all_gather_rows", in_specs=[pl.BlockSpec(memory_space=pltpu.VMEM)],
        out_specs=pl.BlockSpec(memory_space=pltpu.VMEM), out_shape=jax.ShapeDtypeStruct((8, r, cols), F32),
        scratch_shapes=[pltpu.SemaphoreType.DMA((7,)), pltpu.SemaphoreType.DMA((7,))],
        compiler_params=pltpu.CompilerParams(has_side_effects=True, vmem_limit_bytes=VMEM_LIMIT))(x)


def rs_pair(gs):
    n = len(gs)

    def body(*refs):
        ins, outs = refs[:n], refs[n:2 * n]
        send_sems, recv_sems = refs[2 * n:]
        x, y, c = lax.axis_index("x"), lax.axis_index("y"), lax.axis_index("c")
        lh = [g.shape[1] // 2 for g in gs]
        copies = [pltpu.make_async_remote_copy(
            src_ref=ins[t].at[:, pl.ds((1 - c) * lh[t], lh[t])], dst_ref=outs[t], send_sem=send_sems.at[t],
            recv_sem=recv_sems.at[t], device_id=(x, y, 1 - c), device_id_type=MESH) for t in range(n)]
        for cp in copies:
            cp.start()
        for cp in copies:
            cp.wait()

    return pl.pallas_call(
        body, name="rs_pair", in_specs=[_any_spec()] * n, out_specs=[_any_spec()] * n,
        out_shape=[jax.ShapeDtypeStruct((N_CHIPS, g.shape[1] // 2) + g.shape[2:], F32) for g in gs],
        scratch_shapes=[pltpu.SemaphoreType.DMA((n,)), pltpu.SemaphoreType.DMA((n,))],
        compiler_params=pltpu.CompilerParams(has_side_effects=True))(*gs)


def rs_cross(pairs):
    n = len(pairs)

    def body(*refs):
        ins, outs = refs[:n], refs[n:2 * n]
        send_sems, recv_sems = refs[2 * n:]
        x, y, c = lax.axis_index("x"), lax.axis_index("y"), lax.axis_index("c")
        copies = []
        for k, chip in enumerate(_other_chips(x, y)):
            for t in range(n):
                copies.append(pltpu.make_async_remote_copy(
                    src_ref=ins[t].at[2 * chip[0] + chip[1]], dst_ref=outs[t].at[k], send_sem=send_sems.at[3 * t + k],
                    recv_sem=recv_sems.at[3 * t + k], device_id=(*chip, c), device_id_type=MESH))
        for cp in copies:
            cp.start()
        for cp in copies:
            cp.wait()

    return pl.pallas_call(
        body, name="rs_cross", in_specs=[_any_spec()] * n, out_specs=[_any_spec()] * n,
        out_shape=[jax.ShapeDtypeStruct((3,) + p.shape[1:], p.dtype) for p in pairs],
        scratch_shapes=[pltpu.SemaphoreType.DMA((3 * n,)), pltpu.SemaphoreType.DMA((3 * n,))],
        compiler_params=pltpu.CompilerParams(has_side_effects=True))(*pairs)


def rs_share(blocks):
    n = len(blocks)

    def body(*refs):
        ins, outs = refs[:n], refs[n:2 * n]
        send_sems, recv_sems = refs[2 * n:]
        x, y, c = lax.axis_index("x"), lax.axis_index("y"), lax.axis_index("c")
        sends = [pltpu.make_async_remote_copy(
            src_ref=_half(ins[t], c), dst_ref=_half(outs[t], c), send_sem=send_sems.at[t], recv_sem=recv_sems.at[t],
            device_id=(x, y, 1 - c), device_id_type=MESH) for t in range(n)]
        for cp in sends:
            cp.start()
        for t in range(n):
            pltpu.make_async_remote_copy(
                src_ref=_half(ins[t], c), dst_ref=_half(outs[t], 1 - c), send_sem=send_sems.at[t],
                recv_sem=recv_sems.at[t], device_id=(x, y, 1 - c), device_id_type=MESH).wait_recv()
        for cp in sends:
            cp.wait_send()

    return pl.pallas_call(
        body, name="rs_share", in_specs=[_any_spec()] * n, out_specs=[_any_spec()] * n,
        out_shape=[jax.ShapeDtypeStruct(b.shape, F32) for b in blocks],
        input_output_aliases={t: t for t in range(n)},
        scratch_shapes=[pltpu.SemaphoreType.DMA((n,)), pltpu.SemaphoreType.DMA((n,))],
        compiler_params=pltpu.CompilerParams(has_side_effects=True))(*blocks)


def _row_tile(rows, target):
    best = None
    for t in range(8, min(rows, target) + 1, 8):
        if rows % t == 0:
            best = t
    return rows if best is None else best


TILE_BYTES = 2 * 1024 * 1024


def _flat_rows(shape):
    rows = int(np.prod(shape[1:-1]))
    cols_padded = -(-shape[-1] // LANES) * LANES
    return rows, _row_tile(rows, max(8, TILE_BYTES // (4 * cols_padded)))


def pair_add(g, from_sibling, out_dtype):
    cols = g.shape[-1]
    rph, t = _flat_rows(from_sibling.shape)
    nt = rph // t
    c_arr = lax.axis_index("c").astype(jnp.int32).reshape(1)

    def body(c_ref, a_ref, b_ref, o_ref):
        o_ref[...] = (a_ref[...] + b_ref[...]).astype(o_ref.dtype)

    out = pl.pallas_call(
        body, name="pair_add",
        grid_spec=pltpu.PrefetchScalarGridSpec(
            num_scalar_prefetch=1, grid=(N_CHIPS, nt),
            in_specs=[pl.BlockSpec((t, cols), lambda j, i, c_ref: (j * 2 * nt + c_ref[0] * nt + i, 0)),
                      pl.BlockSpec((t, cols), lambda j, i, c_ref: (j * nt + i, 0))],
            out_specs=pl.BlockSpec((t, cols), lambda j, i, c_ref: (j * nt + i, 0))),
        out_shape=jax.ShapeDtypeStruct((N_CHIPS * rph, cols), out_dtype),
        compiler_params=_cparams(("parallel", "parallel")))(c_arr, g.reshape(-1, cols), from_sibling.reshape(-1, cols))
    return out.reshape(from_sibling.shape)


def chip_add(pairs, received):
    cols = pairs.shape[-1]
    rph, t = _flat_rows(pairs.shape)
    nt = rph // t
    j_arr = _chip_index().astype(jnp.int32).reshape(1)
    c_arr = lax.axis_index("c").astype(jnp.int32).reshape(1)
    r2 = received.reshape(-1, cols)

    def body(j_ref, c_ref, a_ref, r0_ref, r1_ref, r2_ref, o_ref):
        a, r0, r1, r2 = [ref[...].astype(F32) for ref in (a_ref, r0_ref, r1_ref, r2_ref)]
        o_ref[...] = (a + r0) + (r1 + r2)

    out = pl.pallas_call(
        body, name="chip_add",
        grid_spec=pltpu.PrefetchScalarGridSpec(
            num_scalar_prefetch=2, grid=(nt,),
            in_specs=[pl.BlockSpec((t, cols), lambda i, j_ref, c_ref: (j_ref[0] * nt + i, 0)),
                      pl.BlockSpec((t, cols), lambda i, j_ref, c_ref: (i, 0)),
                      pl.BlockSpec((t, cols), lambda i, j_ref, c_ref: (nt + i, 0)),
                      pl.BlockSpec((t, cols), lambda i, j_ref, c_ref: (2 * nt + i, 0))],
            out_specs=pl.BlockSpec((t, cols), lambda i, j_ref, c_ref: (c_ref[0] * nt + i, 0))),
        out_shape=jax.ShapeDtypeStruct((2 * rph, cols), F32),
        compiler_params=_cparams(("parallel",)))(j_arr, c_arr, pairs.reshape(-1, cols), r2, r2, r2)
    return out.reshape((2 * pairs.shape[1],) + pairs.shape[2:])


def reduce_grads(gs, cross_dtypes):
    pairs = [pair_add(g, r, dt) for g, r, dt in zip(gs, rs_pair(gs), cross_dtypes)]
    return rs_share([chip_add(p, r) for p, r in zip(pairs, rs_cross(pairs))])


def adamw(w, g, m, v):
    shape = w.shape
    cols = shape[-1]
    rows = int(np.prod(shape[:-1]))
    flat = lambda a: a.reshape(rows, cols)
    t = _row_tile(rows, 512)

    def body(w_ref, g_ref, m_ref, v_ref, d_ref, mo_ref, vo_ref):
        gv = g_ref[...]
        m_new = ADAM_B1 * m_ref[...] + (1.0 - ADAM_B1) * gv
        v_new = ADAM_B2 * v_ref[...] + (1.0 - ADAM_B2) * (gv * gv)
        m_hat = m_new / (1.0 - ADAM_B1 ** ADAM_STEP)
        v_hat = v_new / (1.0 - ADAM_B2 ** ADAM_STEP)
        d_ref[...] = -ADAM_LR * (m_hat / (jnp.sqrt(v_hat) + ADAM_EPS) + ADAM_WD * w_ref[...])
        mo_ref[...] = m_new
        vo_ref[...] = v_new

    spec = pl.BlockSpec((t, cols), lambda i: (i, 0))
    outs = pl.pallas_call(
        body, name="adamw", grid=(rows // t,), in_specs=[spec] * 4, out_specs=[spec] * 3,
        out_shape=[jax.ShapeDtypeStruct((rows, cols), F32)] * 3,
        compiler_params=_cparams(("parallel",)))(flat(w), flat(g), flat(m), flat(v))
    return tuple(o.reshape(shape) for o in outs)


SHARDED = (("w_in", 2), ("mla_q_up", 2), ("mla_kv_up", 2), ("gdn_conv_w", 2), ("w_out", 1))
REPLICATED = ("b_mod", "pre_norm_w", "post_norm_w", "mla_q_norm_w", "mla_kv_norm_w", "gdn_a_log", "gdn_dt_bias",
              "gdn_o_norm_w")
WEIGHT_ORDER = ("w_mod", "b_mod", "pre_norm_w", "post_norm_w", "w_in", "mla_q_norm_w", "mla_q_up", "mla_kv_norm_w",
                "mla_kv_up", "gdn_conv_w", "gdn_a_log", "gdn_dt_bias", "gdn_o_norm_w", "w_out")
EXACT_F32 = ("gdn_conv_w",)
SMALL_ROWS = 48


def _gather_weights(shards):
    names = [name for name, _ in SHARDED]
    own = [shards[n] if n in EXACT_F32 else shards[n].astype(BF16) for n in names]
    gathered = ag_weights(own)
    full = {}
    for (name, axis), blk, mine in zip(SHARDED, gathered, own):
        shp = shards[name].shape
        blk = lax.dynamic_update_index_in_dim(blk, mine, _chip_index(), 0)
        blk = jnp.moveaxis(blk.astype(F32), 0, axis)
        full[name] = blk.reshape(shp[:axis] + (N_CHIPS * shp[axis],) + shp[axis + 1:])
    return full


def _split_grads(grads):
    pieces = []
    for name, axis in SHARDED:
        g = grads[name]
        shp = g.shape
        g = g.reshape(shp[:axis] + (N_CHIPS, shp[axis] // N_CHIPS) + shp[axis + 1:])
        pieces.append(jnp.moveaxis(g, axis, 0))
    small = jnp.concatenate([grads[name] for name in REPLICATED], axis=1)
    small = jnp.pad(small, ((0, 0), (0, SMALL_ROWS * LANES - small.shape[1]))).reshape(DEPTH, SMALL_ROWS, LANES)
    pieces.append(jnp.broadcast_to(small[None], (N_CHIPS,) + small.shape))
    return pieces


def _unsplit_small(small, rep_shapes):
    flat = small.reshape(DEPTH, SMALL_ROWS * LANES)
    out, off = {}, 0
    for name in REPLICATED:
        size = rep_shapes[name][1]
        out[name] = flat[:, off:off + size]
        off += size
    return out


MOD_ROWS = 16


def _device_slot():
    return 4 * lax.axis_index("x") + 2 * lax.axis_index("y") + lax.axis_index("c")


def _adaln_projection(c, w_mod_shard):
    (c_act,) = _OPS["cact"]((jnp.pad(c, ((0, 7), (0, 0))),), ())
    c_acts = jnp.pad(all_gather_rows(c_act)[:, 0, :], ((0, MOD_ROWS - 8), (0, 0)))
    part = jnp.concatenate([_mm(c_acts, w_mod_shard[l].astype(BF16), "nn", "mod_proj") for l in range(DEPTH)], axis=0)
    parts = all_gather_rows(part)[::2].reshape(N_CHIPS, DEPTH, MOD_ROWS, -1)
    mods = jnp.moveaxis(parts, 0, 2).reshape(DEPTH, MOD_ROWS, -1)
    return c_acts, lax.dynamic_slice_in_dim(mods, _device_slot(), 1, axis=1)


def _adaln_weight_grad(c_acts, d_mods):
    cols = d_mods.shape[-1] // N_CHIPS
    rows = jnp.pad(d_mods[:, 0, :], ((0, 8 - DEPTH), (0, 0)))
    all_rows = all_gather_rows(rows)[:, :DEPTH, :]
    mine = lax.dynamic_slice_in_dim(all_rows, _chip_index() * cols, cols, axis=2)
    mine = jnp.pad(mine, ((0, MOD_ROWS - 8), (0, 0), (0, 0)))
    return jnp.stack([_mm(c_acts, mine[:, l, :], "tn", "mod_dw") for l in range(DEPTH)])


def kernel(x, c, positions, w_mod, b_mod, pre_norm_w, post_norm_w, w_in, mla_q_norm_w, mla_q_up, mla_kv_norm_w, mla_kv_up, gdn_conv_w, gdn_a_log, gdn_dt_bias, gdn_o_norm_w, w_out, loss_target, m_w_mod, m_b_mod, m_pre_norm_w, m_post_norm_w, m_w_in, m_mla_q_norm_w, m_mla_q_up, m_mla_kv_norm_w, m_mla_kv_up, m_gdn_conv_w, m_gdn_a_log, m_gdn_dt_bias, m_gdn_o_norm_w, m_w_out, v_w_mod, v_b_mod, v_pre_norm_w, v_post_norm_w, v_w_in, v_mla_q_norm_w, v_mla_q_up, v_mla_kv_norm_w, v_mla_kv_up, v_gdn_conv_w, v_gdn_a_log, v_gdn_dt_bias, v_gdn_o_norm_w, v_w_out):
    given = dict(w_mod=w_mod, b_mod=b_mod, pre_norm_w=pre_norm_w, post_norm_w=post_norm_w, w_in=w_in,
                 mla_q_norm_w=mla_q_norm_w, mla_q_up=mla_q_up, mla_kv_norm_w=mla_kv_norm_w, mla_kv_up=mla_kv_up,
                 gdn_conv_w=gdn_conv_w, gdn_a_log=gdn_a_log, gdn_dt_bias=gdn_dt_bias, gdn_o_norm_w=gdn_o_norm_w,
                 w_out=w_out)
    moments_m = dict(w_mod=m_w_mod, b_mod=m_b_mod, pre_norm_w=m_pre_norm_w, post_norm_w=m_post_norm_w, w_in=m_w_in,
                     mla_q_norm_w=m_mla_q_norm_w, mla_q_up=m_mla_q_up, mla_kv_norm_w=m_mla_kv_norm_w,
                     mla_kv_up=m_mla_kv_up, gdn_conv_w=m_gdn_conv_w, gdn_a_log=m_gdn_a_log,
                     gdn_dt_bias=m_gdn_dt_bias, gdn_o_norm_w=m_gdn_o_norm_w, w_out=m_w_out)
    moments_v = dict(w_mod=v_w_mod, b_mod=v_b_mod, pre_norm_w=v_pre_norm_w, post_norm_w=v_post_norm_w, w_in=v_w_in,
                     mla_q_norm_w=v_mla_q_norm_w, mla_q_up=v_mla_q_up, mla_kv_norm_w=v_mla_kv_norm_w,
                     mla_kv_up=v_mla_kv_up, gdn_conv_w=v_gdn_conv_w, gdn_a_log=v_gdn_a_log,
                     gdn_dt_bias=v_gdn_dt_bias, gdn_o_norm_w=v_gdn_o_norm_w, w_out=v_w_out)

    full = _gather_weights({name: given[name] for name, _ in SHARDED})
    for name in REPLICATED:
        full[name] = given[name]
    c_acts, mods = _adaln_projection(c, w_mod)
    loss_local, (grads, d_mods, grad_x) = jax.value_and_grad(_local_loss, argnums=(0, 1, 2))(
        full, mods, x[0], positions[0], loss_target[0])
    loss = lax.psum(loss_local, AXES)

    cross_dtypes = [F32 if name in EXACT_F32 else BF16 for name, _ in SHARDED] + [F32]
    reduced = reduce_grads(_split_grads(grads), cross_dtypes)
    grad_w = {name: g for (name, _), g in zip(SHARDED, reduced)}
    grad_w.update(_unsplit_small(reduced[-1], {name: given[name].shape for name in REPLICATED}))
    grad_w["w_mod"] = _adaln_weight_grad(c_acts, d_mods)
    delta, new_m, new_v = {}, {}, {}
    for name in WEIGHT_ORDER:
        delta[name], new_m[name], new_v[name] = adamw(given[name], grad_w[name], moments_m[name], moments_v[name])
    return (loss, grad_x[None], *[grad_w[n] for n in WEIGHT_ORDER], *[delta[n] for n in WEIGHT_ORDER],
            *[new_m[n] for n in WEIGHT_ORDER], *[new_v[n] for n in WEIGHT_ORDER])
```

```python
import functools
import math

import numpy as np
import jax
import jax.numpy as jnp
from jax import lax
from jax.experimental import pallas as pl
from jax.experimental.pallas import tpu as pltpu

F32 = jnp.float32
BF16 = jnp.bfloat16
MESH = pl.DeviceIdType.MESH
AXES = ("x", "y", "c")

D_MODEL = 1024
DEPTH = 4
MLA_HEADS = 4
MLA_NOPE = 128
MLA_ROPE = 64
MLA_WIDTH = 512
GDN_HEADS = 4
GDN_DK = 128
GDN_WIDTH = 512
GDN_QKV = 1536
GDN_CONV = 4
ROPE_THETA = 10000.0
NORM_EPS = 1e-6
ADAM_LR, ADAM_B1, ADAM_B2, ADAM_EPS, ADAM_WD, ADAM_STEP = 0.001, 0.9, 0.999, 1e-08, 0.01, 10

LANES = 128
N_CHIPS = 4
GDN_CHUNK = 128
VMEM_LIMIT = 56 * 1024 * 1024


def _cparams(sem=None):
    if sem is None:
        return pltpu.CompilerParams(vmem_limit_bytes=VMEM_LIMIT)
    return pltpu.CompilerParams(dimension_semantics=sem, vmem_limit_bytes=VMEM_LIMIT)


def _pick(dim, target):
    if dim <= target:
        return dim
    best = None
    for t in range(LANES, target + 1, LANES):
        if dim % t == 0:
            best = t
    assert best is not None, (dim, target)
    return best


def _resident(shape):
    return pl.BlockSpec(shape, lambda i: (0,) * len(shape), pipeline_mode=pl.Buffered(1))


_DN = {"nn": (((1,), (0,)), ((), ())), "nt": (((1,), (1,)), ((), ())), "tn": (((0,), (0,)), ((), ()))}


def _dot_raw(a, b, mode, exact):
    if exact:
        return lax.dot_general(a, b, _DN[mode], precision=lax.Precision.HIGHEST, preferred_element_type=F32)
    return lax.dot_general(a.astype(BF16), b.astype(BF16), _DN[mode], preferred_element_type=F32)


def _dot_split(a, b):
    a_hi, b_hi = a.astype(BF16), b.astype(BF16)
    a_lo, b_lo = (a - a_hi.astype(F32)).astype(BF16), (b - b_hi.astype(F32)).astype(BF16)
    dot = lambda u, w: lax.dot_general(u, w, _DN["nn"], preferred_element_type=F32)
    return dot(a_hi, b_hi) + (dot(a_hi, b_lo) + dot(a_lo, b_hi))


@functools.partial(jax.custom_vjp, nondiff_argnums=(2, 3))
def bdot(a, b, mode="nn", exact=False):
    return _dot_raw(a, b, mode, exact)


def _bdot_fwd(a, b, mode, exact):
    return _dot_raw(a, b, mode, exact), (a, b)


def _bdot_bwd(mode, exact, res, g):
    a, b = res
    if mode == "nn":
        return bdot(g, b, "nt", exact), bdot(a, g, "tn", exact)
    if mode == "nt":
        return bdot(g, b, "nn", exact), bdot(g, a, "tn", exact)
    return bdot(b, g, "nt", exact), bdot(a, g, "nn", exact)


bdot.defvjp(_bdot_fwd, _bdot_bwd)


@jax.custom_vjp
def roll_half(x):
    return pltpu.roll(x, 64, 1)


roll_half.defvjp(lambda x: (pltpu.roll(x, 64, 1), None), lambda _, g: (pltpu.roll(g, 64, 1),))


def _sigmoid(x):
    return 1.0 / (1.0 + jnp.exp(-x))


def _silu(x):
    return x * _sigmoid(x)


def _softplus(x):
    return jnp.maximum(x, 0.0) + jnp.log(1.0 + jnp.exp(-jnp.abs(x)))


def _rms(x, w):
    return x * lax.rsqrt(jnp.mean(x * x, axis=-1, keepdims=True) + NORM_EPS) * w


def _rw_fwd(fn, name, rows, params, out_dims, out_dtypes, acc_dims, tile):
    s = rows[0].shape[0]
    t = min(tile, s)
    n = s // t
    nr, npar, no, na = len(rows), len(params), len(out_dims), len(acc_dims)

    def body(*refs):
        r, p = refs[:nr], refs[nr:nr + npar]
        o, a = refs[nr + npar:nr + npar + no], refs[nr + npar + no:]
        outs = fn(*[x[...] for x in r], *[x[...] for x in p])
        for ref, val in zip(o, outs[:no]):
            ref[...] = val.astype(ref.dtype)
        if na:
            @pl.when(pl.program_id(0) == 0)
            def _():
                for ref in a:
                    ref[...] = jnp.zeros_like(ref)
            for ref, val in zip(a, outs[no:]):
                ref[...] += val

    in_specs = [pl.BlockSpec((t, x.shape[1]), lambda i: (i, 0)) for x in rows]
    in_specs += [_resident(x.shape) for x in params]
    out_specs = [pl.BlockSpec((t, d), lambda i: (i, 0)) for d in out_dims]
    out_specs += [_resident((1, d)) for d in acc_dims]
    out_shape = [jax.ShapeDtypeStruct((s, d), dt) for d, dt in zip(out_dims, out_dtypes)]
    out_shape += [jax.ShapeDtypeStruct((1, d), F32) for d in acc_dims]
    res = pl.pallas_call(body, name=name, grid=(n,), in_specs=in_specs, out_specs=out_specs, out_shape=out_shape,
                         compiler_params=_cparams(("arbitrary",)))(*rows, *params)
    return tuple(res)


def _rw_bwd(fn, name, rows, params, row_cts, acc_cts, n_diff, tile):
    s = rows[0].shape[0]
    t = min(tile, s)
    n = s // t
    nr, npar, no, na = len(rows), len(params), len(row_cts), len(acc_cts)

    def body(*refs):
        r, p = refs[:nr], refs[nr:nr + npar]
        g, ga = refs[nr + npar:nr + npar + no], refs[nr + npar + no:nr + npar + no + na]
        dr, dp = refs[nr + npar + no + na:nr + npar + no + na + n_diff], refs[nr + npar + no + na + n_diff:]
        _, vjp = jax.vjp(fn, *[x[...] for x in r], *[x[...] for x in p])
        cts = vjp(tuple([x[...] for x in g] + [x[...] for x in ga]))
        for ref, val in zip(dr, cts[:n_diff]):
            ref[...] = val
        if npar:
            @pl.when(pl.program_id(0) == 0)
            def _():
                for ref in dp:
                    ref[...] = jnp.zeros_like(ref)
            for ref, val in zip(dp, cts[nr:]):
                ref[...] += val

    in_specs = [pl.BlockSpec((t, x.shape[1]), lambda i: (i, 0)) for x in rows]
    in_specs += [_resident(x.shape) for x in params]
    in_specs += [pl.BlockSpec((t, x.shape[1]), lambda i: (i, 0)) for x in row_cts]
    in_specs += [_resident(x.shape) for x in acc_cts]
    out_specs = [pl.BlockSpec((t, x.shape[1]), lambda i: (i, 0)) for x in rows[:n_diff]]
    out_specs += [_resident(x.shape) for x in params]
    out_shape = [jax.ShapeDtypeStruct(x.shape, F32) for x in rows[:n_diff]]
    out_shape += [jax.ShapeDtypeStruct(x.shape, F32) for x in params]
    res = pl.pallas_call(body, name=name, grid=(n,), in_specs=in_specs, out_specs=out_specs, out_shape=out_shape,
                         compiler_params=_cparams(("arbitrary",)))(*rows, *params, *row_cts, *acc_cts)
    return tuple(res[:n_diff]), tuple(res[n_diff:])


def make_rowwise(fn, name, out_dims, acc_dims=(), n_nondiff=0, tile=256):
    out_dtypes = (F32,) * len(out_dims)

    @jax.custom_vjp
    def op(rows, params):
        return _rw_fwd(fn, name, rows, params, out_dims, out_dtypes, acc_dims, tile)

    def fwd(rows, params):
        return op(rows, params), (rows, params)

    def bwd(res, cts):
        rows, params = res
        n_diff = len(rows) - n_nondiff
        d_rows, d_params = _rw_bwd(fn, name + "_bwd", rows, params, cts[:len(out_dims)], cts[len(out_dims):],
                                   n_diff, tile)
        d_rows = d_rows + tuple(jnp.zeros_like(x) for x in rows[n_diff:])
        return d_rows, d_params

    op.defvjp(fwd, bwd)
    return op


def _mm(a, b, mode, name):
    if mode == "nn":
        (m, k), (_, n) = a.shape, b.shape
    elif mode == "nt":
        (m, k), (n, _) = a.shape, b.shape
    else:
        (k, m), (_, n) = a.shape, b.shape
    tm = _pick(m, 512)
    tn = _pick(n, 1152)
    tk = _pick(k, 1152) if mode != "tn" else _pick(k, 512)
    nk = k // tk

    def body(a_ref, b_ref, o_ref, acc_ref):
        kk = pl.program_id(2)

        @pl.when(kk == 0)
        def _():
            acc_ref[...] = jnp.zeros_like(acc_ref)

        acc_ref[...] += _dot_raw(a_ref[...], b_ref[...], mode, False)

        @pl.when(kk == nk - 1)
        def _():
            o_ref[...] = acc_ref[...]

    if mode == "nn":
        a_spec = pl.BlockSpec((tm, tk), lambda i, j, kk: (i, kk))
        b_spec = pl.BlockSpec((tk, tn), lambda i, j, kk: (kk, j))
    elif mode == "nt":
        a_spec = pl.BlockSpec((tm, tk), lambda i, j, kk: (i, kk))
        b_spec = pl.BlockSpec((tn, tk), lambda i, j, kk: (j, kk))
    else:
        a_spec = pl.BlockSpec((tk, tm), lambda i, j, kk: (kk, i))
        b_spec = pl.BlockSpec((tk, tn), lambda i, j, kk: (kk, j))
    return pl.pallas_call(
        body, name=name, grid=(m // tm, n // tn, nk), in_specs=[a_spec, b_spec],
        out_specs=pl.BlockSpec((tm, tn), lambda i, j, kk: (i, j)),
        out_shape=jax.ShapeDtypeStruct((m, n), F32), scratch_shapes=[pltpu.VMEM((tm, tn), F32)],
        compiler_params=_cparams(("parallel", "parallel", "arbitrary")))(a, b)


def _mla_prep_fn(qraw, kvraw, kr, t1, t2):
    kr_rot = kr * t1 + roll_half(kr) * t2
    qs, ks = [], []
    for h in range(MLA_HEADS):
        q_r = qraw[:, h * 256 + 128:(h + 1) * 256]
        qs += [qraw[:, h * 256:h * 256 + 128], q_r * t1 + roll_half(q_r) * t2]
        ks += [kvraw[:, h * 128:(h + 1) * 128], kr_rot]
    return jnp.concatenate(qs, axis=1), jnp.concatenate(ks, axis=1), kvraw[:, 512:]


def _flash_tile(s):
    return 512 if s >= 2048 else 128


FLASH_SCALE = (MLA_NOPE + MLA_ROPE) ** -0.5
LOG2_E = 1.4426950408889634
FLASH_EXP2 = FLASH_SCALE * LOG2_E
STAT_ROWS = 8
FLASH_PAIR = 4
FLASH_STRIP = 32


def _as_rows(col_b):
    ones = jnp.full((STAT_ROWS, LANES), 1.0 / LANES, F32)
    return _dot_raw(ones, col_b, "nt", True)


def _flash_fwd(qf, kf, vf):
    s = qf.shape[0]
    t = _flash_tile(s)
    nb = s // t
    pair = range(FLASH_PAIR)

    ck = min(FLASH_STRIP, t)

    def body(q_ref, k_ref, v_ref, o_ref, lse_ref, m_sc, l_sc, acc_sc, st_sc, pt_sc):
        i = pl.program_id(1)
        m_sc[...] = jnp.full_like(m_sc, -1e30)
        l_sc[...] = jnp.zeros_like(l_sc)
        acc_sc[...] = jnp.zeros_like(acc_sc)
        qs = [q_ref[:, hh * 256:(hh + 1) * 256] for hh in pair]

        def step(j, on_diagonal):
            rows = pl.ds(pl.multiple_of(j * t, t), t)
            for hh in pair:
                st = _dot_raw(k_ref[rows, hh * 256:(hh + 1) * 256], qs[hh], "nt", False)
                if on_diagonal:
                    keep = lax.broadcasted_iota(jnp.int32, (t, t), 0) <= lax.broadcasted_iota(jnp.int32, (t, t), 1)
                    st = jnp.where(keep, st, -1e30)
                st_sc[hh] = st
            m_olds = [m_sc[hh] for hh in pair]
            m_news = [jnp.maximum(m_olds[hh], jnp.max(st_sc[hh], axis=0, keepdims=True)) for hh in pair]
            alphas = [jnp.exp2((m_olds[hh] - m_news[hh]) * FLASH_EXP2) for hh in pair]
            shifts = [m_news[hh] * FLASH_EXP2 for hh in pair]
            sums = [jnp.zeros((8, t), F32) for _ in pair]
            for hh in pair:
                for r in range(0, t, ck):
                    p = jnp.exp2(st_sc[hh, r:r + ck, :] * FLASH_EXP2 - shifts[hh])
                    pt_sc[hh, r:r + ck, :] = p.astype(BF16)
                    sums[hh] = sums[hh] + functools.reduce(lambda a, b: a + b, [p[u:u + 8] for u in range(0, ck, 8)])
            pvs = [_dot_raw(v_ref[rows, hh * 128:(hh + 1) * 128], pt_sc[hh], "tn", False) for hh in pair]
            for hh in pair:
                l_sc[hh] = alphas[hh] * l_sc[hh] + jnp.sum(sums[hh], axis=0, keepdims=True)
                acc_sc[hh] = alphas[hh] * acc_sc[hh] + pvs[hh]
                m_sc[hh] = m_news[hh]

        def four_steps(p, carry):
            for u in range(4):
                step(4 * p + u, False)
            return carry

        def one_step(r, carry):
            step(i - 1 - r, False)
            return carry

        lax.fori_loop(0, i // 4, four_steps, 0)
        lax.fori_loop(0, i % 4, one_step, 0)
        step(i, True)
        for hh in pair:
            o_ref[:, hh * 128:(hh + 1) * 128] = (acc_sc[hh] / l_sc[hh]).T
            lse2 = m_sc[hh] * FLASH_EXP2 + jnp.log(l_sc[hh]) * LOG2_E
            lse_ref[hh] = jnp.broadcast_to(lse2, (STAT_ROWS, t))

    p = FLASH_PAIR
    return pl.pallas_call(
        body, name="flash_fwd", grid=(MLA_HEADS // p, nb),
        in_specs=[pl.BlockSpec((t, p * 256), lambda h, i: (i, h)),
                  pl.BlockSpec((s, p * 256), lambda h, i: (0, h), pipeline_mode=pl.Buffered(1)),
                  pl.BlockSpec((s, p * 128), lambda h, i: (0, h), pipeline_mode=pl.Buffered(1))],
        out_specs=[pl.BlockSpec((t, p * 128), lambda h, i: (i, h)),
                   pl.BlockSpec((p, STAT_ROWS, t), lambda h, i: (h, 0, i))],
        out_shape=[jax.ShapeDtypeStruct((s, MLA_WIDTH), F32),
                   jax.ShapeDtypeStruct((MLA_HEADS, STAT_ROWS, s), F32)],
        scratch_shapes=[pltpu.VMEM((p, 1, t), F32), pltpu.VMEM((p, 1, t), F32), pltpu.VMEM((p, 128, t), F32),
                        pltpu.VMEM((p, t, t), F32), pltpu.VMEM((p, t, t), BF16)],
        compiler_params=_cparams(("parallel", "arbitrary")))(qf, kf, vf)


def _flash_bwd(qf, kf, vf, lse, o, do):
    s = qf.shape[0]
    t = _flash_tile(s)
    nb = s // t

    def body(q_ref, k_ref, v_ref, lse_ref, o_ref, do_ref, dq_ref, dk_ref, dv_ref, dl_sc, do16_sc):
        j = pl.program_id(1)

        @pl.when(j == 0)
        def _():
            dq_ref[...] = jnp.zeros_like(dq_ref)

            def stage(b, carry):
                rows = pl.ds(pl.multiple_of(b * t, t), t)
                do_t = do_ref[rows, :]
                delta = jnp.sum(do_t * o_ref[rows, :], axis=1, keepdims=True)
                dl_sc[:, rows] = _as_rows(jnp.broadcast_to(delta, (t, LANES)))
                do16_sc[rows, :] = do_t.astype(BF16)
                return carry

            lax.fori_loop(0, nb, stage, 0, unroll=min(4, nb))

        dk_ref[...] = jnp.zeros_like(dk_ref)
        dv_ref[...] = jnp.zeros_like(dv_ref)
        k = k_ref[...]
        v = v_ref[...]

        def step(i, on_diagonal):
            rows = pl.ds(pl.multiple_of(i * t, t), t)
            q = q_ref[rows, :]
            do_t = do16_sc[rows, :]
            st = _dot_raw(k, q, "nt", False) * FLASH_EXP2 - lse_ref[0, 0:1, rows]
            if on_diagonal:
                keep = lax.broadcasted_iota(jnp.int32, (t, t), 0) <= lax.broadcasted_iota(jnp.int32, (t, t), 1)
                st = jnp.where(keep, st, -1e30)
            pt = jnp.exp2(st)
            dst = pt * (_dot_raw(v, do_t, "nt", False) - dl_sc[0:1, rows])
            dv_ref[...] += _dot_raw(pt, do_t, "nn", False)
            dk_ref[...] += _dot_raw(dst, q, "nn", False)
            dq_ref[rows, :] += _dot_raw(dst, k, "tn", False)

        def four_steps(p, carry):
            for u in range(4):
                step(j + 1 + 4 * p + u, False)
            return carry

        def one_step(r, carry):
            step(nb - 1 - r, False)
            return carry

        step(j, True)
        below = nb - 1 - j
        lax.fori_loop(0, below // 4, four_steps, 0)
        lax.fori_loop(0, below % 4, one_step, 0)

        dk_ref[...] *= FLASH_SCALE

        @pl.when(j == nb - 1)
        def _():
            dq_ref[...] *= FLASH_SCALE

    per_head = lambda d: pl.BlockSpec((s, d), lambda h, j: (0, h), pipeline_mode=pl.Buffered(1))
    return pl.pallas_call(
        body, name="flash_bwd", grid=(MLA_HEADS, nb),
        in_specs=[per_head(256), pl.BlockSpec((t, 256), lambda h, j: (j, h)),
                  pl.BlockSpec((t, 128), lambda h, j: (j, h)),
                  pl.BlockSpec((1, STAT_ROWS, s), lambda h, j: (h, 0, 0)), per_head(128), per_head(128)],
        out_specs=[pl.BlockSpec((s, 256), lambda h, j: (0, h)), pl.BlockSpec((t, 256), lambda h, j: (j, h)),
                   pl.BlockSpec((t, 128), lambda h, j: (j, h))],
        out_shape=[jax.ShapeDtypeStruct((s, 1024), F32), jax.ShapeDtypeStruct((s, 1024), F32),
                   jax.ShapeDtypeStruct((s, MLA_WIDTH), F32)],
        scratch_shapes=[pltpu.VMEM((STAT_ROWS, s), F32), pltpu.VMEM((s, 128), BF16)],
        compiler_params=_cparams(("parallel", "arbitrary")))(qf, kf, vf, lse, o, do)


MLA_FRONT_TILE = 512


def _mla_front_fn(q_lat, kv_lat, kr, t1, t2, q_norm_w, q_up, kv_norm_w, kv_up):
    qraw = bdot(_rms(q_lat, q_norm_w), q_up)
    kvraw = bdot(_rms(kv_lat, kv_norm_w), kv_up)
    return _mla_prep_fn(qraw, kvraw, kr, t1, t2)


def _mla_front(rows, params):
    return _rw_fwd(_mla_front_fn, "mla_front", rows, params, (1024, 1024, 512), (BF16, BF16, BF16), (),
                   MLA_FRONT_TILE)


@jax.custom_vjp
def mla_attention(rows, params):
    return _flash_fwd(*_mla_front(rows, params))[0]


def _mla_attention_fwd(rows, params):
    qf, kf, vf = _mla_front(rows, params)
    o, lse = _flash_fwd(qf, kf, vf)
    return o, (rows, params, qf, kf, vf, o, lse)


def _mla_attention_bwd(res, do):
    rows, params, qf, kf, vf, o, lse = res
    d_rows, d_params = _rw_bwd(_mla_front_fn, "mla_front_bwd", rows, params, _flash_bwd(qf, kf, vf, lse, o, do), (),
                               3, MLA_FRONT_TILE)
    return d_rows + (jnp.zeros_like(rows[3]), jnp.zeros_like(rows[4])), d_params


mla_attention.defvjp(_mla_attention_fwd, _mla_attention_bwd)


def _lane_pick(x, lane):
    ids = lax.broadcasted_iota(jnp.int32, x.shape, 1)
    col = jnp.sum(jnp.where(ids == lane, x, 0.0), axis=1, keepdims=True)
    return jnp.broadcast_to(col, x.shape)


GDN_HALO = 8


@functools.partial(jax.custom_vjp, nondiff_argnums=(1,))
def _roll_rows(x, d):
    return pltpu.roll(x, d, 0)


_roll_rows.defvjp(lambda x, d: (pltpu.roll(x, d, 0), None), lambda d, _, g: (pltpu.roll(g, g.shape[0] - d, 0),))


def _gdn_prep_fn(prev, cur, ab, w0, w1, w2, w3, a_log, dt_bias):
    xcat = jnp.concatenate([prev, cur], axis=0)
    x0, x1, x2 = [_roll_rows(xcat, GDN_CONV - 1 - j)[GDN_HALO:] for j in range(GDN_CONV - 1)]
    qkv = _silu(x0 * w0 + x1 * w1 + x2 * w2 + cur * w3)
    g_all = -jnp.exp(a_log) * _softplus(ab + dt_bias)
    beta_all = _sigmoid(ab)
    qs, ks, gs, bs = [], [], [], []
    for h in range(GDN_HEADS):
        q = qkv[:, h * 128:(h + 1) * 128]
        k = qkv[:, 512 + h * 128:512 + (h + 1) * 128]
        qs.append(q * lax.rsqrt(jnp.sum(q * q, axis=-1, keepdims=True) + NORM_EPS) * (GDN_DK ** -0.5))
        ks.append(k * lax.rsqrt(jnp.sum(k * k, axis=-1, keepdims=True) + NORM_EPS))
        gs.append(_lane_pick(g_all, h))
        bs.append(_lane_pick(beta_all, GDN_HEADS + h))
    cat = lambda xs: jnp.concatenate(xs, axis=1)
    return cat(qs), cat(ks), qkv[:, 1024:], cat(gs), cat(bs)


GDN_PREP_TILE = 256


def _gdn_prep_specs(s, params, reverse=False):
    t = min(GDN_PREP_TILE, s)
    n = s // t
    blk = (lambda i: n - 1 - i) if reverse else (lambda i: i)
    prev = pl.BlockSpec((GDN_HALO, GDN_QKV), lambda i: (jnp.maximum(blk(i) * (t // GDN_HALO) - 1, 0), 0))
    rows = lambda d: pl.BlockSpec((t, d), lambda i: (blk(i), 0))
    return t, rows, [prev, rows(GDN_QKV), rows(LANES)] + [pl.BlockSpec(p.shape, lambda i: (0, 0)) for p in params]


def _gdn_prep_masked(has_rows_before):
    return lambda prev, *rest: _gdn_prep_fn(prev * has_rows_before, *rest)


def _gdn_prep_fwd(qkv, ab, params):
    s = qkv.shape[0]
    t, rows, in_specs = _gdn_prep_specs(s, params)

    def body(prev_ref, cur_ref, ab_ref, *refs):
        p_refs, o_refs = refs[:len(params)], refs[len(params):]
        has_rows_before = (pl.program_id(0) > 0).astype(F32)
        outs = _gdn_prep_masked(has_rows_before)(prev_ref[...], cur_ref[...], ab_ref[...], *[p[...] for p in p_refs])
        for ref, val in zip(o_refs, outs):
            ref[...] = val

    return pl.pallas_call(
        body, name="gdn_prep", grid=(s // t,), in_specs=in_specs,
        out_specs=[rows(GDN_WIDTH)] * 5, out_shape=[jax.ShapeDtypeStruct((s, GDN_WIDTH), F32)] * 5,
        compiler_params=_cparams(("parallel",)))(qkv, qkv, ab, *params)


def _gdn_prep_bwd(qkv, ab, params, cts):
    s = qkv.shape[0]
    t, rows, in_specs = _gdn_prep_specs(s, params, reverse=True)
    n = s // t
    npar = len(params)

    def body(prev_ref, cur_ref, ab_ref, *refs):
        p_refs, g_refs = refs[:npar], refs[npar:npar + 5]
        dcur_ref, dab_ref = refs[npar + 5:npar + 7]
        dp_refs, carry_sc = refs[npar + 7:-1], refs[-1]

        @pl.when(pl.program_id(0) == 0)
        def _():
            carry_sc[...] = jnp.zeros_like(carry_sc)
            for ref in dp_refs:
                ref[...] = jnp.zeros_like(ref)

        has_rows_before = (pl.program_id(0) < n - 1).astype(F32)
        _, vjp = jax.vjp(_gdn_prep_masked(has_rows_before), prev_ref[...], cur_ref[...], ab_ref[...],
                         *[p[...] for p in p_refs])
        d_prev, d_cur, d_ab, *d_params = vjp(tuple(g[...] for g in g_refs))
        dcur_ref[...] = d_cur
        dcur_ref[t - GDN_HALO:, :] += carry_sc[...]
        carry_sc[...] = d_prev
        dab_ref[...] = d_ab
        for ref, val in zip(dp_refs, d_params):
            ref[...] += val

    res = pl.pallas_call(
        body, name="gdn_prep_bwd", grid=(n,), in_specs=in_specs + [rows(GDN_WIDTH)] * 5,
        out_specs=[rows(GDN_QKV), rows(LANES)] + [pl.BlockSpec(p.shape, lambda i: (0, 0)) for p in params],
        out_shape=[jax.ShapeDtypeStruct((s, GDN_QKV), F32), jax.ShapeDtypeStruct((s, LANES), F32)]
        + [jax.ShapeDtypeStruct(p.shape, F32) for p in params],
        scratch_shapes=[pltpu.VMEM((GDN_HALO, GDN_QKV), F32)],
        compiler_params=_cparams(("arbitrary",)))(qkv, qkv, ab, *params, *cts)
    return res[0], res[1], tuple(res[2:])


@jax.custom_vjp
def gdn_prep(qkv, ab, params):
    return tuple(_gdn_prep_fwd(qkv, ab, params))


def _gdn_prep_vjp_fwd(qkv, ab, params):
    return tuple(_gdn_prep_fwd(qkv, ab, params)), (qkv, ab, params)


def _gdn_prep_vjp_bwd(res, cts):
    qkv, ab, params = res
    return _gdn_prep_bwd(qkv, ab, params, cts)


gdn_prep.defvjp(_gdn_prep_vjp_fwd, _gdn_prep_vjp_bwd)


def _tri_dot(x, mode):
    c = x.shape[0]
    tri = (lax.broadcasted_iota(jnp.int32, (c, c), 0) >= lax.broadcasted_iota(jnp.int32, (c, c), 1)).astype(BF16)
    hi = x.astype(BF16)
    rest = x - hi.astype(F32)
    mid = rest.astype(BF16)
    lo = (rest - mid.astype(F32)).astype(BF16)
    dot = lambda part: lax.dot_general(tri, part, _DN[mode], preferred_element_type=F32)
    return dot(hi) + (dot(mid) + dot(lo))


@jax.custom_vjp
def _chunk_cumsum(x):
    return _tri_dot(x, "nn")


_chunk_cumsum.defvjp(lambda x: (_tri_dot(x, "nn"), None), lambda _, g: (_tri_dot(g, "tn"),))


@jax.custom_vjp
def _unit_lower_inverse(lms):
    c = lms[0].shape[0]
    row = lax.broadcasted_iota(jnp.int32, (c, c), 0)
    col = lax.broadcasted_iota(jnp.int32, (c, c), 1)
    ts = [(row == col).astype(F32) - jnp.where((row >> 1) == (col >> 1), lm, 0.0) for lm in lms]
    for level in range(1, int(math.log2(c))):
        below = ((row >> (level + 1)) == (col >> (level + 1))) & ((row >> level) != (col >> level))
        mids = [_dot_split(t, jnp.where(below, lm, 0.0)) for t, lm in zip(ts, lms)]
        ts = [t - _dot_split(mid, t) for t, mid in zip(ts, mids)]
    return tuple(ts)


def _uli_fwd(lms):
    ts = _unit_lower_inverse(lms)
    return ts, ts


def _uli_bwd(ts, gs):
    mids = [bdot(t, g, "tn") for t, g in zip(ts, gs)]
    return (tuple(-bdot(mid, t, "nt") for t, mid in zip(ts, mids)),)


_unit_lower_inverse.defvjp(_uli_fwd, _uli_bwd)


@jax.custom_vjp
def _known_inverse(lms, ts):
    return ts


_known_inverse.defvjp(lambda lms, ts: (ts, ts),
                      lambda ts, gs: (_uli_bwd(ts, gs)[0], tuple(jnp.zeros_like(t) for t in ts)))


GDN_STEP_CHUNKS = 4


def _gdn_chunk_fn(qs, ks, vs, gbs, bbs, s0s, known_ts=None):
    chains, n_heads = range(len(qs)), len(s0s)
    c = qs[0].shape[0]
    row = lax.broadcasted_iota(jnp.int32, (c, c), 0)
    col = lax.broadcasted_iota(jnp.int32, (c, c), 1)
    incl, strict = row >= col, row > col
    gc = [_chunk_cumsum(gbs[i]) for i in chains]
    decay = [jnp.exp(jnp.where(incl, gc[i] - gc[i].T, -1e30)) for i in chains]
    g_last = [jnp.sum(gbs[i], axis=0, keepdims=True) for i in chains]
    eg = [jnp.exp(gc[i]) for i in chains]
    kb = [ks[i] * bbs[i] for i in chains]
    kq = [bdot(jnp.concatenate([kb[i], qs[i]], axis=0), ks[i], "nt") for i in chains]
    lms = tuple(jnp.where(strict, kq[i][:c] * decay[i], 0.0) for i in chains)
    qk = [kq[i][c:] * decay[i] for i in chains]
    ts = _unit_lower_inverse(lms) if known_ts is None else _known_inverse(lms, known_ts)
    uw = [bdot(ts[i], jnp.concatenate([vs[i] * bbs[i], kb[i] * eg[i]], axis=1)) for i in chains]
    u = [uw[i][:, :128] for i in chains]
    w = [uw[i][:, 128:] for i in chains]
    q_dec = [qs[i] * eg[i] for i in chains]
    k_dec = [ks[i] * jnp.exp(g_last[i] - gc[i]) for i in chains]
    states, outs = list(s0s), []
    for first in range(0, len(qs), n_heads):
        here = range(first, first + n_heads)
        v_new = [u[i] - bdot(w[i], states[i - first]) for i in here]
        outs += [bdot(q_dec[i], states[i - first]) + bdot(qk[i], v_new[i - first]) for i in here]
        states = [states[i - first] * jnp.exp(g_last[i]) + bdot(k_dec[i], v_new[i - first], "tn") for i in here]
    return (tuple(outs), tuple(states)), ts


def _chain_tiles(ref, c):
    return tuple(ref[p * c:(p + 1) * c, h * 128:(h + 1) * 128]
                 for p in range(ref.shape[0] // c) for h in range(GDN_HEADS))


def _gdn_step(s):
    c = min(GDN_CHUNK, s)
    p = min(GDN_STEP_CHUNKS, s // c)
    return c, p, s // (c * p)


def _gdn_fwd(q, k, v, gb, bb):
    s = q.shape[0]
    c, p, steps = _gdn_step(s)

    def body(q_ref, k_ref, v_ref, g_ref, b_ref, o_ref, st_ref, inv_ref, s_sc):
        @pl.when(pl.program_id(0) == 0)
        def _():
            s_sc[...] = jnp.zeros_like(s_sc)

        s0s = tuple(s_sc[h] for h in range(GDN_HEADS))
        for h in range(GDN_HEADS):
            st_ref[h, 0] = s0s[h]
        (os, s1s), ts = _gdn_chunk_fn(*[_chain_tiles(ref, c) for ref in (q_ref, k_ref, v_ref, g_ref, b_ref)], s0s)
        for i in range(p * GDN_HEADS):
            cc, h = divmod(i, GDN_HEADS)
            o_ref[cc * c:(cc + 1) * c, h * 128:(h + 1) * 128] = os[i]
            inv_ref[h, cc] = ts[i]
        for h in range(GDN_HEADS):
            s_sc[h] = s1s[h]

    blk = pl.BlockSpec((p * c, GDN_WIDTH), lambda n: (n, 0))
    return pl.pallas_call(
        body, name="gdn_fwd", grid=(steps,), in_specs=[blk] * 5,
        out_specs=[blk, pl.BlockSpec((GDN_HEADS, 1, 128, 128), lambda n: (0, n, 0, 0)),
                   pl.BlockSpec((GDN_HEADS, p, c, c), lambda n: (0, n, 0, 0))],
        out_shape=[jax.ShapeDtypeStruct((s, GDN_WIDTH), F32), jax.ShapeDtypeStruct((GDN_HEADS, steps, 128, 128), F32),
                   jax.ShapeDtypeStruct((GDN_HEADS, steps * p, c, c), F32)],
        scratch_shapes=[pltpu.VMEM((GDN_HEADS, 128, 128), F32)],
        compiler_params=_cparams(("arbitrary",)))(q, k, v, gb, bb)


def _gdn_bwd(q, k, v, gb, bb, states, inverses, do):
    s = q.shape[0]
    c, p, steps = _gdn_step(s)

    def body(q_ref, k_ref, v_ref, g_ref, b_ref, st_ref, inv_ref, do_ref, dq_ref, dk_ref, dv_ref, dg_ref, db_ref, ds_sc):
        @pl.when(pl.program_id(0) == 0)
        def _():
            ds_sc[...] = jnp.zeros_like(ds_sc)

        s0s = tuple(st_ref[h, 0] for h in range(GDN_HEADS))
        ts = tuple(inv_ref[h, cc] for cc in range(p) for h in range(GDN_HEADS))
        chunks = lambda *args: _gdn_chunk_fn(*args, known_ts=ts)[0]
        _, vjp = jax.vjp(chunks, *[_chain_tiles(ref, c) for ref in (q_ref, k_ref, v_ref, g_ref, b_ref)], s0s)
        *d_tiles, ds0s = vjp((_chain_tiles(do_ref, c), tuple(ds_sc[h] for h in range(GDN_HEADS))))
        for i in range(p * GDN_HEADS):
            cc, h = divmod(i, GDN_HEADS)
            for ref, d in zip((dq_ref, dk_ref, dv_ref, dg_ref, db_ref), d_tiles):
                ref[cc * c:(cc + 1) * c, h * 128:(h + 1) * 128] = d[i]
        for h in range(GDN_HEADS):
            ds_sc[h] = ds0s[h]

    blk = pl.BlockSpec((p * c, GDN_WIDTH), lambda n: (steps - 1 - n, 0))
    return pl.pallas_call(
        body, name="gdn_bwd", grid=(steps,),
        in_specs=[blk] * 5 + [pl.BlockSpec((GDN_HEADS, 1, 128, 128), lambda n: (0, steps - 1 - n, 0, 0)),
                              pl.BlockSpec((GDN_HEADS, p, c, c), lambda n: (0, steps - 1 - n, 0, 0)), blk],
        out_specs=[blk] * 5, out_shape=[jax.ShapeDtypeStruct((s, GDN_WIDTH), F32)] * 5,
        scratch_shapes=[pltpu.VMEM((GDN_HEADS, 128, 128), F32)],
        compiler_params=_cparams(("arbitrary",)))(q, k, v, gb, bb, states, inverses, do)


@jax.custom_vjp
def gdn_core(q, k, v, gb, bb):
    return _gdn_fwd(q, k, v, gb, bb)[0]


def _gdn_core_fwd(q, k, v, gb, bb):
    o, states, inverses = _gdn_fwd(q, k, v, gb, bb)
    return o, (q, k, v, gb, bb, states, inverses)


def _gdn_core_bwd(res, do):
    return tuple(_gdn_bwd(*res, do))


gdn_core.defvjp(_gdn_core_fwd, _gdn_core_bwd)


def _cact_fn(c):
    return (_silu(c),)


def _prenorm_fn(x, w, scale_raw, scale_b, shift_raw, shift_b):
    return (_rms(x, w) * (1.0 + scale_raw + scale_b) + shift_raw + shift_b,)


def _tail_fn(x, o_mla, z_mla, o_gdn, z_gdn, o_norm_w, w_out, post_w, gate_raw, gate_b):
    y_mla = o_mla * _silu(z_mla)
    parts = [_rms(o_gdn[:, h * 128:(h + 1) * 128], o_norm_w) for h in range(GDN_HEADS)]
    y_gdn = jnp.concatenate(parts, axis=1) * _silu(z_gdn)
    y = bdot(y_mla, w_out[:MLA_WIDTH]) + bdot(y_gdn, w_out[MLA_WIDTH:])
    return (x + (gate_raw + gate_b) * _rms(y, post_w),)


def _loss_fn(y, tgt):
    err = y - tgt
    part = jnp.sum(0.5 * jnp.mean(err * err, axis=-1, keepdims=True), axis=0, keepdims=True)
    return (jnp.broadcast_to(part, (1, LANES)),)


def _front_fwd(x, norm_params, ws16):
    s = x.shape[0]
    t = min(512, s)

    def body(x_ref, *refs):
        p_refs, w_refs, o_refs = refs[:len(norm_params)], refs[len(norm_params):-len(ws16)], refs[-len(ws16):]
        (h,) = _prenorm_fn(x_ref[...], *[p[...] for p in p_refs])
        h16 = h.astype(BF16)
        for w_ref, o_ref in zip(w_refs, o_refs):
            o_ref[...] = _dot_raw(h16, w_ref[...], "nn", False)

    rows = lambda d: pl.BlockSpec((t, d), lambda i: (i, 0))
    res = pl.pallas_call(
        body, name="front", grid=(s // t,),
        in_specs=[rows(D_MODEL)] + [_resident(p.shape) for p in norm_params] + [_resident(w.shape) for w in ws16],
        out_specs=[rows(w.shape[1]) for w in ws16],
        out_shape=[jax.ShapeDtypeStruct((s, w.shape[1]), F32) for w in ws16],
        compiler_params=_cparams(("parallel",)))(x, *norm_params, *ws16)
    return tuple(res)


def _front_bwd(x, norm_params, ws16, dys, dx_skip):
    s = x.shape[0]
    t = min(512, s)
    npar, ng = len(norm_params), len(ws16)

    def body(x_ref, *refs):
        p_refs, w_refs, g_refs = refs[:npar], refs[npar:npar + ng], refs[npar + ng:npar + 2 * ng]
        skip_ref, dx_ref = refs[npar + 2 * ng], refs[npar + 2 * ng + 1]
        dp_refs, dw_refs = refs[npar + 2 * ng + 2:2 * npar + 2 * ng + 2], refs[2 * npar + 2 * ng + 2:]

        @pl.when(pl.program_id(0) == 0)
        def _():
            for ref in dp_refs + dw_refs:
                ref[...] = jnp.zeros_like(ref)

        (h,), vjp = jax.vjp(_prenorm_fn, x_ref[...], *[p[...] for p in p_refs])
        h16 = h.astype(BF16)
        dys16 = [g[...].astype(BF16) for g in g_refs]
        dh = functools.reduce(lambda a, b: a + b,
                              [_dot_raw(dy, w_ref[...], "nt", False) for dy, w_ref in zip(dys16, w_refs)])
        for dy, dw_ref in zip(dys16, dw_refs):
            dw_ref[...] += _dot_raw(h16, dy, "tn", False)
        dx, *d_params = vjp((dh,))
        dx_ref[...] = dx + skip_ref[...]
        for ref, val in zip(dp_refs, d_params):
            ref[...] += val

    rows = lambda d: pl.BlockSpec((t, d), lambda i: (i, 0))
    res = pl.pallas_call(
        body, name="front_bwd", grid=(s // t,),
        in_specs=[rows(D_MODEL)] + [_resident(p.shape) for p in norm_params] + [_resident(w.shape) for w in ws16]
        + [rows(w.shape[1]) for w in ws16] + [rows(D_MODEL)],
        out_specs=[rows(D_MODEL)] + [_resident(p.shape) for p in norm_params] + [_resident(w.shape) for w in ws16],
        out_shape=[jax.ShapeDtypeStruct(x.shape, F32)] + [jax.ShapeDtypeStruct(p.shape, F32) for p in norm_params]
        + [jax.ShapeDtypeStruct(w.shape, F32) for w in ws16],
        compiler_params=_cparams(("arbitrary",)))(x, *norm_params, *ws16, *dys, dx_skip)
    return res[0], tuple(res[1:1 + npar]), tuple(res[1 + npar:])


@jax.custom_vjp
def front(x, norm_params, ws):
    return _front_fwd(x, norm_params, tuple(w.astype(BF16) for w in ws)) + (x,)


def _front_vjp_fwd(x, norm_params, ws):
    ws16 = tuple(w.astype(BF16) for w in ws)
    return _front_fwd(x, norm_params, ws16) + (x,), (x, norm_params, ws16)


def _front_vjp_bwd(res, cts):
    return _front_bwd(*res, cts[:-1], cts[-1])


front.defvjp(_front_vjp_fwd, _front_vjp_bwd)


_OPS = dict(
    cact=make_rowwise(_cact_fn, "c_act", (D_MODEL,), tile=16),
    tail=make_rowwise(_tail_fn, "tail", (D_MODEL,), tile=512),
    loss=make_rowwise(_loss_fn, "loss", (), acc_dims=(LANES,), n_nondiff=1, tile=512),
)


def _swap_halves(w):
    half = w.shape[-1] // 2
    return jnp.concatenate([w[..., half:], w[..., :half]], axis=-1)


def _w_in_groups(w):
    k_pe = w[:, 640:704]
    ab = jnp.concatenate([w[:, 2752:2760], jnp.zeros((w.shape[0], LANES - 8), w.dtype)], axis=1)
    return (w[:, :384], w[:, 384:640], jnp.concatenate([k_pe, _swap_halves(k_pe)], axis=1), w[:, 704:1216],
            w[:, 1216:2752], ab, w[:, 2760:])


def _q_up_ext(w):
    parts = []
    for h in range(MLA_HEADS):
        rope = w[:, h * 192 + 128:(h + 1) * 192]
        parts += [w[:, h * 192:h * 192 + 128], rope, _swap_halves(rope)]
    return jnp.concatenate(parts, axis=1)


def _kv_up_perm(w):
    ks = [w[:, h * 256:h * 256 + 128] for h in range(MLA_HEADS)]
    vs = [w[:, h * 256 + 128:(h + 1) * 256] for h in range(MLA_HEADS)]
    return jnp.concatenate(ks + vs, axis=1)


def _pad_lanes(v):
    return jnp.pad(v, (0, LANES - v.shape[0]))[None, :]


def _local_loss(weights, mods, x, positions, target):
    s = x.shape[0]
    half = MLA_ROPE // 2
    inv_freq = jnp.power(ROPE_THETA, -jnp.arange(half, dtype=F32) * 2.0 / MLA_ROPE)
    ang = positions.astype(F32)[:, None] * inv_freq
    cos, sin, zero = jnp.cos(ang), jnp.sin(ang), jnp.zeros((s, 2 * half), F32)
    t1 = jnp.concatenate([cos, cos, zero], axis=1)
    t2 = jnp.concatenate([-sin, sin, zero], axis=1)

    for l in range(DEPTH):
        mod = mods[l]
        b = weights["b_mod"][l][None, :]
        shift_raw, scale_raw, gate_raw = mod[:, :1024], mod[:, 1024:2048], mod[:, 2048:]
        shift_b, scale_b, gate_b = b[:, :1024], b[:, 1024:2048], b[:, 2048:]
        q_lat, kv_lat, kr, z_mla, qkv, ab, z_gdn, x = front(
            x, (weights["pre_norm_w"][l][None], scale_raw, scale_b, shift_raw, shift_b),
            _w_in_groups(weights["w_in"][l]))
        o_mla = mla_attention((q_lat, kv_lat, kr, t1, t2),
                              (weights["mla_q_norm_w"][l][None], _q_up_ext(weights["mla_q_up"][l]),
                               weights["mla_kv_norm_w"][l][None], _kv_up_perm(weights["mla_kv_up"][l])))
        cw = weights["gdn_conv_w"][l]
        params = tuple(cw[j][None] for j in range(GDN_CONV))
        params += (_pad_lanes(weights["gdn_a_log"][l]), _pad_lanes(weights["gdn_dt_bias"][l]))
        o_gdn = gdn_core(*gdn_prep(qkv, ab, params))
        (x,) = _OPS["tail"]((x, o_mla, z_mla, o_gdn, z_gdn),
                            (weights["gdn_o_norm_w"][l][None], weights["w_out"][l], weights["post_norm_w"][l][None],
                             gate_raw, gate_b))
    (acc,) = _OPS["loss"]((x, target), ())
    return acc[0, 0]


def _chip_index():
    return 2 * lax.axis_index("x") + lax.axis_index("y")


def _other_chips(x, y):
    return [(1 - x, y), (x, 1 - y), (1 - x, 1 - y)]


def _any_spec():
    return pl.BlockSpec(memory_space=pl.ANY)


def _half(ref, hc):
    n = ref.shape[0] // 2
    return ref.at[pl.ds(hc * n, n)]


def ag_weights(shards):
    n = len(shards)

    def body(*refs):
        ins, outs = refs[:n], refs[n:2 * n]
        send_sems, recv_sems = refs[2 * n:]
        x, y, c = lax.axis_index("x"), lax.axis_index("y"), lax.axis_index("c")
        sibling = (x, y, 1 - c)
        chips = _other_chips(x, y)

        def copy(t, k, src, chip_xy, hc, to):
            return pltpu.make_async_remote_copy(
                src_ref=src, dst_ref=_half(outs[t].at[2 * chip_xy[0] + chip_xy[1]], hc),
                send_sem=send_sems.at[6 * t + k], recv_sem=recv_sems.at[6 * t + k], device_id=to, device_id_type=MESH)

        first = [copy(t, k, _half(ins[t], c), (x, y), c, (*chip, c)) for k, chip in enumerate(chips) for t in range(n)]
        for cp in first:
            cp.start()
        passed = []
        for k, chip in enumerate(chips):
            for t in range(n):
                landed = _half(outs[t].at[2 * chip[0] + chip[1]], c)
                copy(t, k, landed, chip, c, (x, y, c)).wait_recv()
                passed.append(copy(t, 3 + k, landed, chip, c, sibling))
                passed[-1].start()
        for k, chip in enumerate(chips):
            for t in range(n):
                copy(t, 3 + k, _half(ins[t], c), chip, 1 - c, (x, y, c)).wait_recv()
        for cp in first + passed:
            cp.wait_send()

    return pl.pallas_call(
        body, name="ag_weights", in_specs=[_any_spec()] * n, out_specs=[_any_spec()] * n,
        out_shape=[jax.ShapeDtypeStruct((N_CHIPS,) + a.shape, a.dtype) for a in shards],
        scratch_shapes=[pltpu.SemaphoreType.DMA((6 * n,)), pltpu.SemaphoreType.DMA((6 * n,))],
        compiler_params=pltpu.CompilerParams(has_side_effects=True))(*shards)


def all_gather_rows(x):
    r, cols = x.shape
    flips = [(k >> 2 & 1, k >> 1 & 1, k & 1) for k in range(1, 8)]

    def body(x_ref, out_ref, send_sems, recv_sems):
        here = (lax.axis_index("x"), lax.axis_index("y"), lax.axis_index("c"))
        peers = [tuple(1 - p if f else p for p, f in zip(here, flip)) for flip in flips]
        slot = lambda dev: 4 * dev[0] + 2 * dev[1] + dev[2]
        out_ref[slot(here)] = x_ref[...]
        copies = [pltpu.make_async_remote_copy(
            src_ref=x_ref, dst_ref=out_ref.at[slot(here)], send_sem=send_sems.at[k], recv_sem=recv_sems.at[k],
            device_id=peer, device_id_type=MESH) for k, peer in enumerate(peers)]
        for cp in copies:
            cp.start()
        for k, peer in enumerate(peers):
            pltpu.make_async_remote_copy(
                src_ref=x_ref, dst_ref=out_ref.at[slot(peer)], send_sem=send_sems.at[k], recv_sem=recv_sems.at[k],
                device_id=peer, device_id_type=MESH).wait_recv()
        for cp in copies:
            cp.wait_send()

    return pl.pallas_call(
        body, name="all_gather_rows", in_specs=[pl.BlockSpec(memory_space=pltpu.VMEM)],
        out_specs=pl.BlockSpec(memory_space=pltpu.VMEM), out_shape=jax.ShapeDtypeStruct((8, r, cols), F32),
        scratch_shapes=[pltpu.SemaphoreType.DMA((7,)), pltpu.SemaphoreType.DMA((7,))],
        compiler_params=pltpu.CompilerParams(has_side_effects=True, vmem_limit_bytes=VMEM_LIMIT))(x)


def rs_pair(gs):
    n = len(gs)

    def body(*refs):
        ins, outs = refs[:n], refs[n:2 * n]
        send_sems, recv_sems = refs[2 * n:]
        x, y, c = lax.axis_index("x"), lax.axis_index("y"), lax.axis_index("c")
        lh = [g.shape[1] // 2 for g in gs]
        copies = [pltpu.make_async_remote_copy(
            src_ref=ins[t].at[:, pl.ds((1 - c) * lh[t], lh[t])], dst_ref=outs[t], send_sem=send_sems.at[t],
            recv_sem=recv_sems.at[t], device_id=(x, y, 1 - c), device_id_type=MESH) for t in range(n)]
        for cp in copies:
            cp.start()
        for cp in copies:
            cp.wait()

    return pl.pallas_call(
        body, name="rs_pair", in_specs=[_any_spec()] * n, out_specs=[_any_spec()] * n,
        out_shape=[jax.ShapeDtypeStruct((N_CHIPS, g.shape[1] // 2) + g.shape[2:], F32) for g in gs],
        scratch_shapes=[pltpu.SemaphoreType.DMA((n,)), pltpu.SemaphoreType.DMA((n,))],
        compiler_params=pltpu.CompilerParams(has_side_effects=True))(*gs)


def rs_cross(pairs):
    n = len(pairs)

    def body(*refs):
        ins, outs = refs[:n], refs[n:2 * n]
        send_sems, recv_sems = refs[2 * n:]
        x, y, c = lax.axis_index("x"), lax.axis_index("y"), lax.axis_index("c")
        copies = []
        for k, chip in enumerate(_other_chips(x, y)):
            for t in range(n):
                copies.append(pltpu.make_async_remote_copy(
                    src_ref=ins[t].at[2 * chip[0] + chip[1]], dst_ref=outs[t].at[k], send_sem=send_sems.at[3 * t + k],
                    recv_sem=recv_sems.at[3 * t + k], device_id=(*chip, c), device_id_type=MESH))
        for cp in copies:
            cp.start()
        for cp in copies:
            cp.wait()

    return pl.pallas_call(
        body, name="rs_cross", in_specs=[_any_spec()] * n, out_specs=[_any_spec()] * n,
        out_shape=[jax.ShapeDtypeStruct((3,) + p.shape[1:], p.dtype) for p in pairs],
        scratch_shapes=[pltpu.SemaphoreType.DMA((3 * n,)), pltpu.SemaphoreType.DMA((3 * n,))],
        compiler_params=pltpu.CompilerParams(has_side_effects=True))(*pairs)


def rs_share(blocks):
    n = len(blocks)

    def body(*refs):
        ins, outs = refs[:n], refs[n:2 * n]
        send_sems, recv_sems = refs[2 * n:]
        x, y, c = lax.axis_index("x"), lax.axis_index("y"), lax.axis_index("c")
        sends = [pltpu.make_async_remote_copy(
            src_ref=_half(ins[t], c), dst_ref=_half(outs[t], c), send_sem=send_sems.at[t], recv_sem=recv_sems.at[t],
            device_id=(x, y, 1 - c), device_id_type=MESH) for t in range(n)]
        for cp in sends:
            cp.start()
        for t in range(n):
            pltpu.make_async_remote_copy(
                src_ref=_half(ins[t], c), dst_ref=_half(outs[t], 1 - c), send_sem=send_sems.at[t],
                recv_sem=recv_sems.at[t], device_id=(x, y, 1 - c), device_id_type=MESH).wait_recv()
        for cp in sends:
            cp.wait_send()

    return pl.pallas_call(
        body, name="rs_share", in_specs=[_any_spec()] * n, out_specs=[_any_spec()] * n,
        out_shape=[jax.ShapeDtypeStruct(b.shape, F32) for b in blocks],
        input_output_aliases={t: t for t in range(n)},
        scratch_shapes=[pltpu.SemaphoreType.DMA((n,)), pltpu.SemaphoreType.DMA((n,))],
        compiler_params=pltpu.CompilerParams(has_side_effects=True))(*blocks)


def _row_tile(rows, target):
    best = None
    for t in range(8, min(rows, target) + 1, 8):
        if rows % t == 0:
            best = t
    return rows if best is None else best


TILE_BYTES = 2 * 1024 * 1024


def _flat_rows(shape):
    rows = int(np.prod(shape[1:-1]))
    cols_padded = -(-shape[-1] // LANES) * LANES
    return rows, _row_tile(rows, max(8, TILE_BYTES // (4 * cols_padded)))


def pair_add(g, from_sibling, out_dtype):
    cols = g.shape[-1]
    rph, t = _flat_rows(from_sibling.shape)
    nt = rph // t
    c_arr = lax.axis_index("c").astype(jnp.int32).reshape(1)

    def body(c_ref, a_ref, b_ref, o_ref):
        o_ref[...] = (a_ref[...] + b_ref[...]).astype(o_ref.dtype)

    out = pl.pallas_call(
        body, name="pair_add",
        grid_spec=pltpu.PrefetchScalarGridSpec(
            num_scalar_prefetch=1, grid=(N_CHIPS, nt),
            in_specs=[pl.BlockSpec((t, cols), lambda j, i, c_ref: (j * 2 * nt + c_ref[0] * nt + i, 0)),
                      pl.BlockSpec((t, cols), lambda j, i, c_ref: (j * nt + i, 0))],
            out_specs=pl.BlockSpec((t, cols), lambda j, i, c_ref: (j * nt + i, 0))),
        out_shape=jax.ShapeDtypeStruct((N_CHIPS * rph, cols), out_dtype),
        compiler_params=_cparams(("parallel", "parallel")))(c_arr, g.reshape(-1, cols), from_sibling.reshape(-1, cols))
    return out.reshape(from_sibling.shape)


def chip_add(pairs, received):
    cols = pairs.shape[-1]
    rph, t = _flat_rows(pairs.shape)
    nt = rph // t
    j_arr = _chip_index().astype(jnp.int32).reshape(1)
    c_arr = lax.axis_index("c").astype(jnp.int32).reshape(1)
    r2 = received.reshape(-1, cols)

    def body(j_ref, c_ref, a_ref, r0_ref, r1_ref, r2_ref, o_ref):
        a, r0, r1, r2 = [ref[...].astype(F32) for ref in (a_ref, r0_ref, r1_ref, r2_ref)]
        o_ref[...] = (a + r0) + (r1 + r2)

    out = pl.pallas_call(
        body, name="chip_add",
        grid_spec=pltpu.PrefetchScalarGridSpec(
            num_scalar_prefetch=2, grid=(nt,),
            in_specs=[pl.BlockSpec((t, cols), lambda i, j_ref, c_ref: (j_ref[0] * nt + i, 0)),
                      pl.BlockSpec((t, cols), lambda i, j_ref, c_ref: (i, 0)),
                      pl.BlockSpec((t, cols), lambda i, j_ref, c_ref: (nt + i, 0)),
                      pl.BlockSpec((t, cols), lambda i, j_ref, c_ref: (2 * nt + i, 0))],
            out_specs=pl.BlockSpec((t, cols), lambda i, j_ref, c_ref: (c_ref[0] * nt + i, 0))),
        out_shape=jax.ShapeDtypeStruct((2 * rph, cols), F32),
        compiler_params=_cparams(("parallel",)))(j_arr, c_arr, pairs.reshape(-1, cols), r2, r2, r2)
    return out.reshape((2 * pairs.shape[1],) + pairs.shape[2:])


def reduce_grads(gs, cross_dtypes):
    pairs = [pair_add(g, r, dt) for g, r, dt in zip(gs, rs_pair(gs), cross_dtypes)]
    return rs_share([chip_add(p, r) for p, r in zip(pairs, rs_cross(pairs))])


def adamw(w, g, m, v):
    shape = w.shape
    cols = shape[-1]
    rows = int(np.prod(shape[:-1]))
    flat = lambda a: a.reshape(rows, cols)
    t = _row_tile(rows, 512)

    def body(w_ref, g_ref, m_ref, v_ref, d_ref, mo_ref, vo_ref):
        gv = g_ref[...]
        m_new = ADAM_B1 * m_ref[...] + (1.0 - ADAM_B1) * gv
        v_new = ADAM_B2 * v_ref[...] + (1.0 - ADAM_B2) * (gv * gv)
        m_hat = m_new / (1.0 - ADAM_B1 ** ADAM_STEP)
        v_hat = v_new / (1.0 - ADAM_B2 ** ADAM_STEP)
        d_ref[...] = -ADAM_LR * (m_hat / (jnp.sqrt(v_hat) + ADAM_EPS) + ADAM_WD * w_ref[...])
        mo_ref[...] = m_new
        vo_ref[...] = v_new

    spec = pl.BlockSpec((t, cols), lambda i: (i, 0))
    outs = pl.pallas_call(
        body, name="adamw", grid=(rows // t,), in_specs=[spec] * 4, out_specs=[spec] * 3,
        out_shape=[jax.ShapeDtypeStruct((rows, cols), F32)] * 3,
        compiler_params=_cparams(("parallel",)))(flat(w), flat(g), flat(m), flat(v))
    return tuple(o.reshape(shape) for o in outs)


SHARDED = (("w_in", 2), ("mla_q_up", 2), ("mla_kv_up", 2), ("gdn_conv_w", 2), ("w_out", 1))
REPLICATED = ("b_mod", "pre_norm_w", "post_norm_w", "mla_q_norm_w", "mla_kv_norm_w", "gdn_a_log", "gdn_dt_bias",
              "gdn_o_norm_w")
WEIGHT_ORDER = ("w_mod", "b_mod", "pre_norm_w", "post_norm_w", "w_in", "mla_q_norm_w", "mla_q_up", "mla_kv_norm_w",
                "mla_kv_up", "gdn_conv_w", "gdn_a_log", "gdn_dt_bias", "gdn_o_norm_w", "w_out")
EXACT_F32 = ("gdn_conv_w",)
SMALL_ROWS = 48


def _gather_weights(shards):
    names = [name for name, _ in SHARDED]
    own = [shards[n] if n in EXACT_F32 else shards[n].astype(BF16) for n in names]
    gathered = ag_weights(own)
    full = {}
    for (name, axis), blk, mine in zip(SHARDED, gathered, own):
        shp = shards[name].shape
        blk = lax.dynamic_update_index_in_dim(blk, mine, _chip_index(), 0)
        blk = jnp.moveaxis(blk.astype(F32), 0, axis)
        full[name] = blk.reshape(shp[:axis] + (N_CHIPS * shp[axis],) + shp[axis + 1:])
    return full


def _split_grads(grads):
    pieces = []
    for name, axis in SHARDED:
        g = grads[name]
        shp = g.shape
        g = g.reshape(shp[:axis] + (N_CHIPS, shp[axis] // N_CHIPS) + shp[axis + 1:])
        pieces.append(jnp.moveaxis(g, axis, 0))
    small = jnp.concatenate([grads[name] for name in REPLICATED], axis=1)
    small = jnp.pad(small, ((0, 0), (0, SMALL_ROWS * LANES - small.shape[1]))).reshape(DEPTH, SMALL_ROWS, LANES)
    pieces.append(jnp.broadcast_to(small[None], (N_CHIPS,) + small.shape))
    return pieces


def _unsplit_small(small, rep_shapes):
    flat = small.reshape(DEPTH, SMALL_ROWS * LANES)
    out, off = {}, 0
    for name in REPLICATED:
        size = rep_shapes[name][1]
        out[name] = flat[:, off:off + size]
        off += size
    return out


MOD_ROWS = 16


def _device_slot():
    return 4 * lax.axis_index("x") + 2 * lax.axis_index("y") + lax.axis_index("c")


def _adaln_projection(c, w_mod_shard):
    (c_act,) = _OPS["cact"]((jnp.pad(c, ((0, 7), (0, 0))),), ())
    c_acts = jnp.pad(all_gather_rows(c_act)[:, 0, :], ((0, MOD_ROWS - 8), (0, 0)))
    part = jnp.concatenate([_mm(c_acts, w_mod_shard[l].astype(BF16), "nn", "mod_proj") for l in range(DEPTH)], axis=0)
    parts = all_gather_rows(part)[::2].reshape(N_CHIPS, DEPTH, MOD_ROWS, -1)
    mods = jnp.moveaxis(parts, 0, 2).reshape(DEPTH, MOD_ROWS, -1)
    return c_acts, lax.dynamic_slice_in_dim(mods, _device_slot(), 1, axis=1)


def _adaln_weight_grad(c_acts, d_mods):
    cols = d_mods.shape[-1] // N_CHIPS
    rows = jnp.pad(d_mods[:, 0, :], ((0, 8 - DEPTH), (0, 0)))
    all_rows = all_gather_rows(rows)[:, :DEPTH, :]
    mine = lax.dynamic_slice_in_dim(all_rows, _chip_index() * cols, cols, axis=2)
    mine = jnp.pad(mine, ((0, MOD_ROWS - 8), (0, 0), (0, 0)))
    return jnp.stack([_mm(c_acts, mine[:, l, :], "tn", "mod_dw") for l in range(DEPTH)])


def kernel(x, c, positions, w_mod, b_mod, pre_norm_w, post_norm_w, w_in, mla_q_norm_w, mla_q_up, mla_kv_norm_w, mla_kv_up, gdn_conv_w, gdn_a_log, gdn_dt_bias, gdn_o_norm_w, w_out, loss_target, m_w_mod, m_b_mod, m_pre_norm_w, m_post_norm_w, m_w_in, m_mla_q_norm_w, m_mla_q_up, m_mla_kv_norm_w, m_mla_kv_up, m_gdn_conv_w, m_gdn_a_log, m_gdn_dt_bias, m_gdn_o_norm_w, m_w_out, v_w_mod, v_b_mod, v_pre_norm_w, v_post_norm_w, v_w_in, v_mla_q_norm_w, v_mla_q_up, v_mla_kv_norm_w, v_mla_kv_up, v_gdn_conv_w, v_gdn_a_log, v_gdn_dt_bias, v_gdn_o_norm_w, v_w_out):
    given = dict(w_mod=w_mod, b_mod=b_mod, pre_norm_w=pre_norm_w, post_norm_w=post_norm_w, w_in=w_in,
                 mla_q_norm_w=mla_q_norm_w, mla_q_up=mla_q_up, mla_kv_norm_w=mla_kv_norm_w, mla_kv_up=mla_kv_up,
                 gdn_conv_w=gdn_conv_w, gdn_a_log=gdn_a_log, gdn_dt_bias=gdn_dt_bias, gdn_o_norm_w=gdn_o_norm_w,
                 w_out=w_out)
    moments_m = dict(w_mod=m_w_mod, b_mod=m_b_mod, pre_norm_w=m_pre_norm_w, post_norm_w=m_post_norm_w, w_in=m_w_in,
                     mla_q_norm_w=m_mla_q_norm_w, mla_q_up=m_mla_q_up, mla_kv_norm_w=m_mla_kv_norm_w,
                     mla_kv_up=m_mla_kv_up, gdn_conv_w=m_gdn_conv_w, gdn_a_log=m_gdn_a_log,
                     gdn_dt_bias=m_gdn_dt_bias, gdn_o_norm_w=m_gdn_o_norm_w, w_out=m_w_out)
    moments_v = dict(w_mod=v_w_mod, b_mod=v_b_mod, pre_norm_w=v_pre_norm_w, post_norm_w=v_post_norm_w, w_in=v_w_in,
                     mla_q_norm_w=v_mla_q_norm_w, mla_q_up=v_mla_q_up, mla_kv_norm_w=v_mla_kv_norm_w,
                     mla_kv_up=v_mla_kv_up, gdn_conv_w=v_gdn_conv_w, gdn_a_log=v_gdn_a_log,
                     gdn_dt_bias=v_gdn_dt_bias, gdn_o_norm_w=v_gdn_o_norm_w, w_out=v_w_out)

    full = _gather_weights({name: given[name] for name, _ in SHARDED})
    for name in REPLICATED:
        full[name] = given[name]
    c_acts, mods = _adaln_projection(c, w_mod)
    loss_local, (grads, d_mods, grad_x) = jax.value_and_grad(_local_loss, argnums=(0, 1, 2))(
        full, mods, x[0], positions[0], loss_target[0])
    loss = lax.psum(loss_local, AXES)

    cross_dtypes = [F32 if name in EXACT_F32 else BF16 for name, _ in SHARDED] + [F32]
    reduced = reduce_grads(_split_grads(grads), cross_dtypes)
    grad_w = {name: g for (name, _), g in zip(SHARDED, reduced)}
    grad_w.update(_unsplit_small(reduced[-1], {name: given[name].shape for name in REPLICATED}))
    grad_w["w_mod"] = _adaln_weight_grad(c_acts, d_mods)
    delta, new_m, new_v = {}, {}, {}
    for name in WEIGHT_ORDER:
        delta[name], new_m[name], new_v[name] = adamw(given[name], grad_w[name], moments_m[name], moments_v[name])
    return (loss, grad_x[None], *[grad_w[n] for n in WEIGHT_ORDER], *[delta[n] for n in WEIGHT_ORDER],
            *[new_m[n] for n in WEIGHT_ORDER], *[new_v[n] for n in WEIGHT_ORDER])
```

```python
import functools
import math

import numpy as np
import jax
import jax.numpy as jnp
from jax import lax
from jax.experimental import pallas as pl
from jax.experimental.pallas import tpu as pltpu

F32 = jnp.float32
BF16 = jnp.bfloat16
MESH = pl.DeviceIdType.MESH
AXES = ("x", "y", "c")

D_MODEL = 1024
DEPTH = 4
MLA_HEADS = 4
MLA_NOPE = 128
MLA_ROPE = 64
MLA_WIDTH = 512
GDN_HEADS = 4
GDN_DK = 128
GDN_WIDTH = 512
GDN_QKV = 1536
GDN_CONV = 4
ROPE_THETA = 10000.0
NORM_EPS = 1e-6
ADAM_LR, ADAM_B1, ADAM_B2, ADAM_EPS, ADAM_WD, ADAM_STEP = 0.001, 0.9, 0.999, 1e-08, 0.01, 10

LANES = 128
N_CHIPS = 4
GDN_CHUNK = 128
VMEM_LIMIT = 56 * 1024 * 1024


def _cparams(sem=None):
    if sem is None:
        return pltpu.CompilerParams(vmem_limit_bytes=VMEM_LIMIT)
    return pltpu.CompilerParams(dimension_semantics=sem, vmem_limit_bytes=VMEM_LIMIT)


def _pick(dim, target):
    if dim <= target:
        return dim
    best = None
    for t in range(LANES, target + 1, LANES):
        if dim % t == 0:
            best = t
    assert best is not None, (dim, target)
    return best


def _resident(shape):
    return pl.BlockSpec(shape, lambda i: (0,) * len(shape), pipeline_mode=pl.Buffered(1))


_DN = {"nn": (((1,), (0,)), ((), ())), "nt": (((1,), (1,)), ((), ())), "tn": (((0,), (0,)), ((), ()))}


def _dot_raw(a, b, mode, exact):
    if exact:
        return lax.dot_general(a, b, _DN[mode], precision=lax.Precision.HIGHEST, preferred_element_type=F32)
    return lax.dot_general(a.astype(BF16), b.astype(BF16), _DN[mode], preferred_element_type=F32)


def _dot_split(a, b):
    a_hi, b_hi = a.astype(BF16), b.astype(BF16)
    a_lo, b_lo = (a - a_hi.astype(F32)).astype(BF16), (b - b_hi.astype(F32)).astype(BF16)
    dot = lambda u, w: lax.dot_general(u, w, _DN["nn"], preferred_element_type=F32)
    return dot(a_hi, b_hi) + (dot(a_hi, b_lo) + dot(a_lo, b_hi))


@functools.partial(jax.custom_vjp, nondiff_argnums=(2, 3))
def bdot(a, b, mode="nn", exact=False):
    return _dot_raw(a, b, mode, exact)


def _bdot_fwd(a, b, mode, exact):
    return _dot_raw(a, b, mode, exact), (a, b)


def _bdot_bwd(mode, exact, res, g):
    a, b = res
    if mode == "nn":
        return bdot(g, b, "nt", exact), bdot(a, g, "tn", exact)
    if mode == "nt":
        return bdot(g, b, "nn", exact), bdot(g, a, "tn", exact)
    return bdot(b, g, "nt", exact), bdot(a, g, "nn", exact)


bdot.defvjp(_bdot_fwd, _bdot_bwd)


@jax.custom_vjp
def roll_half(x):
    return pltpu.roll(x, 64, 1)


roll_half.defvjp(lambda x: (pltpu.roll(x, 64, 1), None), lambda _, g: (pltpu.roll(g, 64, 1),))


def _sigmoid(x):
    return 1.0 / (1.0 + jnp.exp(-x))


def _silu(x):
    return x * _sigmoid(x)


def _softplus(x):
    return jnp.maximum(x, 0.0) + jnp.log(1.0 + jnp.exp(-jnp.abs(x)))


def _rms(x, w):
    return x * lax.rsqrt(jnp.mean(x * x, axis=-1, keepdims=True) + NORM_EPS) * w


def _rw_fwd(fn, name, rows, params, out_dims, out_dtypes, acc_dims, tile):
    s = rows[0].shape[0]
    t = min(tile, s)
    n = s // t
    nr, npar, no, na = len(rows), len(params), len(out_dims), len(acc_dims)

    def body(*refs):
        r, p = refs[:nr], refs[nr:nr + npar]
        o, a = refs[nr + npar:nr + npar + no], refs[nr + npar + no:]
        outs = fn(*[x[...] for x in r], *[x[...] for x in p])
        for ref, val in zip(o, outs[:no]):
            ref[...] = val.astype(ref.dtype)
        if na:
            @pl.when(pl.program_id(0) == 0)
            def _():
                for ref in a:
                    ref[...] = jnp.zeros_like(ref)
            for ref, val in zip(a, outs[no:]):
                ref[...] += val

    in_specs = [pl.BlockSpec((t, x.shape[1]), lambda i: (i, 0)) for x in rows]
    in_specs += [_resident(x.shape) for x in params]
    out_specs = [pl.BlockSpec((t, d), lambda i: (i, 0)) for d in out_dims]
    out_specs += [_resident((1, d)) for d in acc_dims]
    out_shape = [jax.ShapeDtypeStruct((s, d), dt) for d, dt in zip(out_dims, out_dtypes)]
    out_shape += [jax.ShapeDtypeStruct((1, d), F32) for d in acc_dims]
    res = pl.pallas_call(body, name=name, grid=(n,), in_specs=in_specs, out_specs=out_specs, out_shape=out_shape,
                         compiler_params=_cparams(("arbitrary",)))(*rows, *params)
    return tuple(res)


def _rw_bwd(fn, name, rows, params, row_cts, acc_cts, n_diff, tile, split=1):
    s = rows[0].shape[0]
    t = min(tile, s)
    n = s // t
    nr, npar, no, na = len(rows), len(params), len(row_cts), len(acc_cts)

    def body(*refs):
        r, p = refs[:nr], refs[nr:nr + npar]
        g, ga = refs[nr + npar:nr + npar + no], refs[nr + npar + no:nr + npar + no + na]
        dr, dp = refs[nr + npar + no + na:nr + npar + no + na + n_diff], refs[nr + npar + no + na + n_diff:]
        if npar:
            @pl.when(pl.program_id(0) == 0)
            def _():
                for ref in dp:
                    ref[...] = jnp.zeros_like(ref)
        for piece in range(split):
            rows_here = slice(piece * (t // split), (piece + 1) * (t // split))
            _, vjp = jax.vjp(fn, *[x[rows_here, :] for x in r], *[x[...] for x in p])
            cts = vjp(tuple([x[rows_here, :] for x in g] + [x[...] for x in ga]))
            for ref, val in zip(dr, cts[:n_diff]):
                ref[rows_here, :] = val
            for ref, val in zip(dp, cts[nr:]):
                ref[...] += val

    in_specs = [pl.BlockSpec((t, x.shape[1]), lambda i: (i, 0)) for x in rows]
    in_specs += [_resident(x.shape) for x in params]
    in_specs += [pl.BlockSpec((t, x.shape[1]), lambda i: (i, 0)) for x in row_cts]
    in_specs += [_resident(x.shape) for x in acc_cts]
    out_specs = [pl.BlockSpec((t, x.shape[1]), lambda i: (i, 0)) for x in rows[:n_diff]]
    out_specs += [_resident(x.shape) for x in params]
    out_shape = [jax.ShapeDtypeStruct(x.shape, F32) for x in rows[:n_diff]]
    out_shape += [jax.ShapeDtypeStruct(x.shape, F32) for x in params]
    res = pl.pallas_call(body, name=name, grid=(n,), in_specs=in_specs, out_specs=out_specs, out_shape=out_shape,
                         compiler_params=_cparams(("arbitrary",)))(*rows, *params, *row_cts, *acc_cts)
    return tuple(res[:n_diff]), tuple(res[n_diff:])


def make_rowwise(fn, name, out_dims, acc_dims=(), n_nondiff=0, tile=256, bwd_split=1):
    out_dtypes = (F32,) * len(out_dims)

    @jax.custom_vjp
    def op(rows, params):
        return _rw_fwd(fn, name, rows, params, out_dims, out_dtypes, acc_dims, tile)

    def fwd(rows, params):
        return op(rows, params), (rows, params)

    def bwd(res, cts):
        rows, params = res
        n_diff = len(rows) - n_nondiff
        d_rows, d_params = _rw_bwd(fn, name + "_bwd", rows, params, cts[:len(out_dims)], cts[len(out_dims):],
                                   n_diff, tile, bwd_split)
        d_rows = d_rows + tuple(jnp.zeros_like(x) for x in rows[n_diff:])
        return d_rows, d_params

    op.defvjp(fwd, bwd)
    return op


def _mm(a, b, mode, name):
    if mode == "nn":
        (m, k), (_, n) = a.shape, b.shape
    elif mode == "nt":
        (m, k), (n, _) = a.shape, b.shape
    else:
        (k, m), (_, n) = a.shape, b.shape
    tm = _pick(m, 512)
    tn = _pick(n, 1152)
    tk = _pick(k, 1152) if mode != "tn" else _pick(k, 512)
    nk = k // tk

    def body(a_ref, b_ref, o_ref, acc_ref):
        kk = pl.program_id(2)

        @pl.when(kk == 0)
        def _():
            acc_ref[...] = jnp.zeros_like(acc_ref)

        acc_ref[...] += _dot_raw(a_ref[...], b_ref[...], mode, False)

        @pl.when(kk == nk - 1)
        def _():
            o_ref[...] = acc_ref[...]

    if mode == "nn":
        a_spec = pl.BlockSpec((tm, tk), lambda i, j, kk: (i, kk))
        b_spec = pl.BlockSpec((tk, tn), lambda i, j, kk: (kk, j))
    elif mode == "nt":
        a_spec = pl.BlockSpec((tm, tk), lambda i, j, kk: (i, kk))
        b_spec = pl.BlockSpec((tn, tk), lambda i, j, kk: (j, kk))
    else:
        a_spec = pl.BlockSpec((tk, tm), lambda i, j, kk: (kk, i))
        b_spec = pl.BlockSpec((tk, tn), lambda i, j, kk: (kk, j))
    return pl.pallas_call(
        body, name=name, grid=(m // tm, n // tn, nk), in_specs=[a_spec, b_spec],
        out_specs=pl.BlockSpec((tm, tn), lambda i, j, kk: (i, j)),
        out_shape=jax.ShapeDtypeStruct((m, n), F32), scratch_shapes=[pltpu.VMEM((tm, tn), F32)],
        compiler_params=_cparams(("parallel", "parallel", "arbitrary")))(a, b)


def _mla_prep_fn(qraw, kvraw, kr, t1, t2):
    kr_rot = kr * t1 + roll_half(kr) * t2
    qs, ks = [], []
    for h in range(MLA_HEADS):
        q_r = qraw[:, h * 256 + 128:(h + 1) * 256]
        qs += [qraw[:, h * 256:h * 256 + 128], q_r * t1 + roll_half(q_r) * t2]
        ks += [kvraw[:, h * 128:(h + 1) * 128], kr_rot]
    return jnp.concatenate(qs, axis=1), jnp.concatenate(ks, axis=1), kvraw[:, 512:]


def _flash_tile(s):
    return 512 if s >= 2048 else 128


FLASH_SCALE = (MLA_NOPE + MLA_ROPE) ** -0.5
LOG2_E = 1.4426950408889634
FLASH_EXP2 = FLASH_SCALE * LOG2_E
STAT_ROWS = 8
FLASH_PAIR = 4
FLASH_STRIP = 32


def _as_rows(col_b):
    ones = jnp.full((STAT_ROWS, LANES), 1.0 / LANES, F32)
    return _dot_raw(ones, col_b, "nt", True)


def _flash_fwd(qf, kf, vf):
    s = qf.shape[0]
    t = _flash_tile(s)
    nb = s // t
    pair = range(FLASH_PAIR)

    ck = min(FLASH_STRIP, t)

    def body(q_ref, k_ref, v_ref, o_ref, lse_ref, m_sc, l_sc, acc_sc, st_sc, pt_sc):
        i = pl.program_id(1)
        m_sc[...] = jnp.full_like(m_sc, -1e30)
        l_sc[...] = jnp.zeros_like(l_sc)
        acc_sc[...] = jnp.zeros_like(acc_sc)
        qs = [q_ref[:, hh * 256:(hh + 1) * 256] for hh in pair]

        def step(j, on_diagonal):
            rows = pl.ds(pl.multiple_of(j * t, t), t)
            for hh in pair:
                st = _dot_raw(k_ref[rows, hh * 256:(hh + 1) * 256], qs[hh], "nt", False)
                if on_diagonal:
                    keep = lax.broadcasted_iota(jnp.int32, (t, t), 0) <= lax.broadcasted_iota(jnp.int32, (t, t), 1)
                    st = jnp.where(keep, st, -1e30)
                st_sc[hh] = st
            m_olds = [m_sc[hh] for hh in pair]
            m_news = [jnp.maximum(m_olds[hh], jnp.max(st_sc[hh], axis=0, keepdims=True)) for hh in pair]
            alphas = [jnp.exp2((m_olds[hh] - m_news[hh]) * FLASH_EXP2) for hh in pair]
            shifts = [m_news[hh] * FLASH_EXP2 for hh in pair]
            sums = [jnp.zeros((8, t), F32) for _ in pair]
            for hh in pair:
                for r in range(0, t, ck):
                    p = jnp.exp2(st_sc[hh, r:r + ck, :] * FLASH_EXP2 - shifts[hh])
                    pt_sc[hh, r:r + ck, :] = p.astype(BF16)
                    sums[hh] = sums[hh] + functools.reduce(lambda a, b: a + b, [p[u:u + 8] for u in range(0, ck, 8)])
            pvs = [_dot_raw(v_ref[rows, hh * 128:(hh + 1) * 128], pt_sc[hh], "tn", False) for hh in pair]
            for hh in pair:
                l_sc[hh] = alphas[hh] * l_sc[hh] + jnp.sum(sums[hh], axis=0, keepdims=True)
                acc_sc[hh] = alphas[hh] * acc_sc[hh] + pvs[hh]
                m_sc[hh] = m_news[hh]

        def four_steps(p, carry):
            for u in range(4):
                step(4 * p + u, False)
            return carry

        def one_step(r, carry):
            step(i - 1 - r, False)
            return carry

        lax.fori_loop(0, i // 4, four_steps, 0)
        lax.fori_loop(0, i % 4, one_step, 0)
        step(i, True)
        for hh in pair:
            o_ref[:, hh * 128:(hh + 1) * 128] = (acc_sc[hh] / l_sc[hh]).T
            lse2 = m_sc[hh] * FLASH_EXP2 + jnp.log(l_sc[hh]) * LOG2_E
            lse_ref[hh] = jnp.broadcast_to(lse2, (STAT_ROWS, t))

    p = FLASH_PAIR
    return pl.pallas_call(
        body, name="flash_fwd", grid=(MLA_HEADS // p, nb),
        in_specs=[pl.BlockSpec((t, p * 256), lambda h, i: (i, h)),
                  pl.BlockSpec((s, p * 256), lambda h, i: (0, h), pipeline_mode=pl.Buffered(1)),
                  pl.BlockSpec((s, p * 128), lambda h, i: (0, h), pipeline_mode=pl.Buffered(1))],
        out_specs=[pl.BlockSpec((t, p * 128), lambda h, i: (i, h)),
                   pl.BlockSpec((p, STAT_ROWS, t), lambda h, i: (h, 0, i))],
        out_shape=[jax.ShapeDtypeStruct((s, MLA_WIDTH), F32),
                   jax.ShapeDtypeStruct((MLA_HEADS, STAT_ROWS, s), F32)],
        scratch_shapes=[pltpu.VMEM((p, 1, t), F32), pltpu.VMEM((p, 1, t), F32), pltpu.VMEM((p, 128, t), F32),
                        pltpu.VMEM((p, t, t), F32), pltpu.VMEM((p, t, t), BF16)],
        compiler_params=_cparams(("parallel", "arbitrary")))(qf, kf, vf)


def _flash_bwd(qf, kf, vf, lse, o, do):
    s = qf.shape[0]
    t = _flash_tile(s)
    nb = s // t

    def body(q_ref, k_ref, v_ref, lse_ref, o_ref, do_ref, dq_ref, dk_ref, dv_ref, dl_sc, do16_sc):
        j = pl.program_id(1)

        @pl.when(j == 0)
        def _():
            dq_ref[...] = jnp.zeros_like(dq_ref)

            def stage(b, carry):
                rows = pl.ds(pl.multiple_of(b * t, t), t)
                do_t = do_ref[rows, :]
                delta = jnp.sum(do_t * o_ref[rows, :], axis=1, keepdims=True)
                dl_sc[:, rows] = _as_rows(jnp.broadcast_to(delta, (t, LANES)))
                do16_sc[rows, :] = do_t.astype(BF16)
                return carry

            lax.fori_loop(0, nb, stage, 0, unroll=min(4, nb))

        dk_ref[...] = jnp.zeros_like(dk_ref)
        dv_ref[...] = jnp.zeros_like(dv_ref)
        k = k_ref[...]
        v = v_ref[...]

        def step(i, on_diagonal):
            rows = pl.ds(pl.multiple_of(i * t, t), t)
            q = q_ref[rows, :]
            do_t = do16_sc[rows, :]
            st = _dot_raw(k, q, "nt", False) * FLASH_EXP2 - lse_ref[0, 0:1, rows]
            if on_diagonal:
                keep = lax.broadcasted_iota(jnp.int32, (t, t), 0) <= lax.broadcasted_iota(jnp.int32, (t, t), 1)
                st = jnp.where(keep, st, -1e30)
            pt = jnp.exp2(st)
            dst = pt * (_dot_raw(v, do_t, "nt", False) - dl_sc[0:1, rows])
            dv_ref[...] += _dot_raw(pt, do_t, "nn", False)
            dk_ref[...] += _dot_raw(dst, q, "nn", False)
            dq_ref[rows, :] += _dot_raw(dst, k, "tn", False)

        def four_steps(p, carry):
            for u in range(4):
                step(j + 1 + 4 * p + u, False)
            return carry

        def one_step(r, carry):
            step(nb - 1 - r, False)
            return carry

        step(j, True)
        below = nb - 1 - j
        lax.fori_loop(0, below // 4, four_steps, 0)
        lax.fori_loop(0, below % 4, one_step, 0)

        dk_ref[...] *= FLASH_SCALE

        @pl.when(j == nb - 1)
        def _():
            dq_ref[...] *= FLASH_SCALE

    per_head = lambda d: pl.BlockSpec((s, d), lambda h, j: (0, h), pipeline_mode=pl.Buffered(1))
    return pl.pallas_call(
        body, name="flash_bwd", grid=(MLA_HEADS, nb),
        in_specs=[per_head(256), pl.BlockSpec((t, 256), lambda h, j: (j, h)),
                  pl.BlockSpec((t, 128), lambda h, j: (j, h)),
                  pl.BlockSpec((1, STAT_ROWS, s), lambda h, j: (h, 0, 0)), per_head(128), per_head(128)],
        out_specs=[pl.BlockSpec((s, 256), lambda h, j: (0, h)), pl.BlockSpec((t, 256), lambda h, j: (j, h)),
                   pl.BlockSpec((t, 128), lambda h, j: (j, h))],
        out_shape=[jax.ShapeDtypeStruct((s, 1024), F32), jax.ShapeDtypeStruct((s, 1024), F32),
                   jax.ShapeDtypeStruct((s, MLA_WIDTH), F32)],
        scratch_shapes=[pltpu.VMEM((STAT_ROWS, s), F32), pltpu.VMEM((s, 128), BF16)],
        compiler_params=_cparams(("parallel", "arbitrary")))(qf, kf, vf, lse, o, do)


MLA_FRONT_TILE = 512


def _mla_front_fn(q_lat, kv_lat, kr, t1, t2, q_norm_w, q_up, kv_norm_w, kv_up):
    qraw = bdot(_rms(q_lat, q_norm_w), q_up)
    kvraw = bdot(_rms(kv_lat, kv_norm_w), kv_up)
    return _mla_prep_fn(qraw, kvraw, kr, t1, t2)


def _mla_front(rows, params):
    return _rw_fwd(_mla_front_fn, "mla_front", rows, params, (1024, 1024, 512), (BF16, BF16, BF16), (),
                   MLA_FRONT_TILE)


@jax.custom_vjp
def mla_attention(rows, params):
    return _flash_fwd(*_mla_front(rows, params))[0]


def _mla_attention_fwd(rows, params):
    qf, kf, vf = _mla_front(rows, params)
    o, lse = _flash_fwd(qf, kf, vf)
    return o, (rows, params, qf, kf, vf, o, lse)


def _mla_attention_bwd(res, do):
    rows, params, qf, kf, vf, o, lse = res
    d_rows, d_params = _rw_bwd(_mla_front_fn, "mla_front_bwd", rows, params, _flash_bwd(qf, kf, vf, lse, o, do), (),
                               3, MLA_FRONT_TILE)
    return d_rows + (jnp.zeros_like(rows[3]), jnp.zeros_like(rows[4])), d_params


mla_attention.defvjp(_mla_attention_fwd, _mla_attention_bwd)


def _lane_pick(x, lane):
    ids = lax.broadcasted_iota(jnp.int32, x.shape, 1)
    col = jnp.sum(jnp.where(ids == lane, x, 0.0), axis=1, keepdims=True)
    return jnp.broadcast_to(col, x.shape)


GDN_HALO = 8


@functools.partial(jax.custom_vjp, nondiff_argnums=(1,))
def _roll_rows(x, d):
    return pltpu.roll(x, d, 0)


_roll_rows.defvjp(lambda x, d: (pltpu.roll(x, d, 0), None), lambda d, _, g: (pltpu.roll(g, g.shape[0] - d, 0),))


def _gdn_prep_fn(prev, cur, ab, w0, w1, w2, w3, a_log, dt_bias):
    xcat = jnp.concatenate([prev, cur], axis=0)
    x0, x1, x2 = [_roll_rows(xcat, GDN_CONV - 1 - j)[GDN_HALO:] for j in range(GDN_CONV - 1)]
    qkv = _silu(x0 * w0 + x1 * w1 + x2 * w2 + cur * w3)
    g_all = -jnp.exp(a_log) * _softplus(ab + dt_bias)
    beta_all = _sigmoid(ab)
    qs, ks, gs, bs = [], [], [], []
    for h in range(GDN_HEADS):
        q = qkv[:, h * 128:(h + 1) * 128]
        k = qkv[:, 512 + h * 128:512 + (h + 1) * 128]
        qs.append(q * lax.rsqrt(jnp.sum(q * q, axis=-1, keepdims=True) + NORM_EPS) * (GDN_DK ** -0.5))
        ks.append(k * lax.rsqrt(jnp.sum(k * k, axis=-1, keepdims=True) + NORM_EPS))
        gs.append(_lane_pick(g_all, h))
        bs.append(_lane_pick(beta_all, GDN_HEADS + h))
    cat = lambda xs: jnp.concatenate(xs, axis=1)
    return cat(qs), cat(ks), qkv[:, 1024:], cat(gs), cat(bs)


GDN_PREP_TILE = 256


def _gdn_prep_specs(s, params, reverse=False):
    t = min(GDN_PREP_TILE, s)
    n = s // t
    blk = (lambda i: n - 1 - i) if reverse else (lambda i: i)
    prev = pl.BlockSpec((GDN_HALO, GDN_QKV), lambda i: (jnp.maximum(blk(i) * (t // GDN_HALO) - 1, 0), 0))
    rows = lambda d: pl.BlockSpec((t, d), lambda i: (blk(i), 0))
    return t, rows, [prev, rows(GDN_QKV), rows(LANES)] + [pl.BlockSpec(p.shape, lambda i: (0, 0)) for p in params]


def _gdn_prep_masked(has_rows_before):
    return lambda prev, *rest: _gdn_prep_fn(prev * has_rows_before, *rest)


def _gdn_prep_fwd(qkv, ab, params):
    s = qkv.shape[0]
    t, rows, in_specs = _gdn_prep_specs(s, params)

    def body(prev_ref, cur_ref, ab_ref, *refs):
        p_refs, o_refs = refs[:len(params)], refs[len(params):]
        has_rows_before = (pl.program_id(0) > 0).astype(F32)
        outs = _gdn_prep_masked(has_rows_before)(prev_ref[...], cur_ref[...], ab_ref[...], *[p[...] for p in p_refs])
        for ref, val in zip(o_refs, outs):
            ref[...] = val

    return pl.pallas_call(
        body, name="gdn_prep", grid=(s // t,), in_specs=in_specs,
        out_specs=[rows(GDN_WIDTH)] * 5, out_shape=[jax.ShapeDtypeStruct((s, GDN_WIDTH), F32)] * 5,
        compiler_params=_cparams(("parallel",)))(qkv, qkv, ab, *params)


def _gdn_prep_bwd(qkv, ab, params, cts):
    s = qkv.shape[0]
    t, rows, in_specs = _gdn_prep_specs(s, params, reverse=True)
    n = s // t
    npar = len(params)

    def body(prev_ref, cur_ref, ab_ref, *refs):
        p_refs, g_refs = refs[:npar], refs[npar:npar + 5]
        dcur_ref, dab_ref = refs[npar + 5:npar + 7]
        dp_refs, carry_sc = refs[npar + 7:-1], refs[-1]

        @pl.when(pl.program_id(0) == 0)
        def _():
            carry_sc[...] = jnp.zeros_like(carry_sc)
            for ref in dp_refs:
                ref[...] = jnp.zeros_like(ref)

        has_rows_before = (pl.program_id(0) < n - 1).astype(F32)
        _, vjp = jax.vjp(_gdn_prep_masked(has_rows_before), prev_ref[...], cur_ref[...], ab_ref[...],
                         *[p[...] for p in p_refs])
        d_prev, d_cur, d_ab, *d_params = vjp(tuple(g[...] for g in g_refs))
        dcur_ref[...] = d_cur
        dcur_ref[t - GDN_HALO:, :] += carry_sc[...]
        carry_sc[...] = d_prev
        dab_ref[...] = d_ab
        for ref, val in zip(dp_refs, d_params):
            ref[...] += val

    res = pl.pallas_call(
        body, name="gdn_prep_bwd", grid=(n,), in_specs=in_specs + [rows(GDN_WIDTH)] * 5,
        out_specs=[rows(GDN_QKV), rows(LANES)] + [pl.BlockSpec(p.shape, lambda i: (0, 0)) for p in params],
        out_shape=[jax.ShapeDtypeStruct((s, GDN_QKV), F32), jax.ShapeDtypeStruct((s, LANES), F32)]
        + [jax.ShapeDtypeStruct(p.shape, F32) for p in params],
        scratch_shapes=[pltpu.VMEM((GDN_HALO, GDN_QKV), F32)],
        compiler_params=_cparams(("arbitrary",)))(qkv, qkv, ab, *params, *cts)
    return res[0], res[1], tuple(res[2:])


@jax.custom_vjp
def gdn_prep(qkv, ab, params):
    return tuple(_gdn_prep_fwd(qkv, ab, params))


def _gdn_prep_vjp_fwd(qkv, ab, params):
    return tuple(_gdn_prep_fwd(qkv, ab, params)), (qkv, ab, params)


def _gdn_prep_vjp_bwd(res, cts):
    qkv, ab, params = res
    return _gdn_prep_bwd(qkv, ab, params, cts)


gdn_prep.defvjp(_gdn_prep_vjp_fwd, _gdn_prep_vjp_bwd)


def _tri_dot(x, mode):
    c = x.shape[0]
    tri = (lax.broadcasted_iota(jnp.int32, (c, c), 0) >= lax.broadcasted_iota(jnp.int32, (c, c), 1)).astype(BF16)
    hi = x.astype(BF16)
    rest = x - hi.astype(F32)
    mid = rest.astype(BF16)
    lo = (rest - mid.astype(F32)).astype(BF16)
    dot = lambda part: lax.dot_general(tri, part, _DN[mode], preferred_element_type=F32)
    return dot(hi) + (dot(mid) + dot(lo))


@jax.custom_vjp
def _chunk_cumsum(x):
    return _tri_dot(x, "nn")


_chunk_cumsum.defvjp(lambda x: (_tri_dot(x, "nn"), None), lambda _, g: (_tri_dot(g, "tn"),))


@jax.custom_vjp
def _unit_lower_inverse(lms):
    c = lms[0].shape[0]
    row = lax.broadcasted_iota(jnp.int32, (c, c), 0)
    col = lax.broadcasted_iota(jnp.int32, (c, c), 1)
    ts = [(row == col).astype(F32) - jnp.where((row >> 1) == (col >> 1), lm, 0.0) for lm in lms]
    for level in range(1, int(math.log2(c))):
        below = ((row >> (level + 1)) == (col >> (level + 1))) & ((row >> level) != (col >> level))
        mids = [_dot_split(t, jnp.where(below, lm, 0.0)) for t, lm in zip(ts, lms)]
        ts = [t - _dot_split(mid, t) for t, mid in zip(ts, mids)]
    return tuple(ts)


def _uli_fwd(lms):
    ts = _unit_lower_inverse(lms)
    return ts, ts


def _uli_bwd(ts, gs):
    mids = [bdot(t, g, "tn") for t, g in zip(ts, gs)]
    return (tuple(-bdot(mid, t, "nt") for t, mid in zip(ts, mids)),)


_unit_lower_inverse.defvjp(_uli_fwd, _uli_bwd)


@jax.custom_vjp
def _known_inverse(lms, ts):
    return ts


_known_inverse.defvjp(lambda lms, ts: (ts, ts),
                      lambda ts, gs: (_uli_bwd(ts, gs)[0], tuple(jnp.zeros_like(t) for t in ts)))


GDN_STEP_CHUNKS = 4


def _gdn_chunk_fn(qs, ks, vs, gbs, bbs, s0s, known_ts=None):
    chains, n_heads = range(len(qs)), len(s0s)
    c = qs[0].shape[0]
    row = lax.broadcasted_iota(jnp.int32, (c, c), 0)
    col = lax.broadcasted_iota(jnp.int32, (c, c), 1)
    incl, strict = row >= col, row > col
    gc = [_chunk_cumsum(gbs[i]) for i in chains]
    decay = [jnp.exp(jnp.where(incl, gc[i] - gc[i].T, -1e30)) for i in chains]
    g_last = [jnp.sum(gbs[i], axis=0, keepdims=True) for i in chains]
    eg = [jnp.exp(gc[i]) for i in chains]
    kb = [ks[i] * bbs[i] for i in chains]
    kq = [bdot(jnp.concatenate([kb[i], qs[i]], axis=0), ks[i], "nt") for i in chains]
    lms = tuple(jnp.where(strict, kq[i][:c] * decay[i], 0.0) for i in chains)
    qk = [kq[i][c:] * decay[i] for i in chains]
    ts = _unit_lower_inverse(lms) if known_ts is None else _known_inverse(lms, known_ts)
    uw = [bdot(ts[i], jnp.concatenate([vs[i] * bbs[i], kb[i] * eg[i]], axis=1)) for i in chains]
    u = [uw[i][:, :128] for i in chains]
    w = [uw[i][:, 128:] for i in chains]
    q_dec = [qs[i] * eg[i] for i in chains]
    k_dec = [ks[i] * jnp.exp(g_last[i] - gc[i]) for i in chains]
    states, outs = list(s0s), []
    for first in range(0, len(qs), n_heads):
        here = range(first, first + n_heads)
        v_new = [u[i] - bdot(w[i], states[i - first]) for i in here]
        outs += [bdot(q_dec[i], states[i - first]) + bdot(qk[i], v_new[i - first]) for i in here]
        states = [states[i - first] * jnp.exp(g_last[i]) + bdot(k_dec[i], v_new[i - first], "tn") for i in here]
    return (tuple(outs), tuple(states)), ts


def _chain_tiles(ref, c):
    return tuple(ref[p * c:(p + 1) * c, h * 128:(h + 1) * 128]
                 for p in range(ref.shape[0] // c) for h in range(GDN_HEADS))


def _gdn_step(s):
    c = min(GDN_CHUNK, s)
    p = min(GDN_STEP_CHUNKS, s // c)
    return c, p, s // (c * p)


def _gdn_fwd(q, k, v, gb, bb):
    s = q.shape[0]
    c, p, steps = _gdn_step(s)

    def body(q_ref, k_ref, v_ref, g_ref, b_ref, o_ref, st_ref, inv_ref, s_sc):
        @pl.when(pl.program_id(0) == 0)
        def _():
            s_sc[...] = jnp.zeros_like(s_sc)

        s0s = tuple(s_sc[h] for h in range(GDN_HEADS))
        for h in range(GDN_HEADS):
            st_ref[h, 0] = s0s[h]
        (os, s1s), ts = _gdn_chunk_fn(*[_chain_tiles(ref, c) for ref in (q_ref, k_ref, v_ref, g_ref, b_ref)], s0s)
        for i in range(p * GDN_HEADS):
            cc, h = divmod(i, GDN_HEADS)
            o_ref[cc * c:(cc + 1) * c, h * 128:(h + 1) * 128] = os[i]
            inv_ref[h, cc] = ts[i]
        for h in range(GDN_HEADS):
            s_sc[h] = s1s[h]

    blk = pl.BlockSpec((p * c, GDN_WIDTH), lambda n: (n, 0))
    return pl.pallas_call(
        body, name="gdn_fwd", grid=(steps,), in_specs=[blk] * 5,
        out_specs=[blk, pl.BlockSpec((GDN_HEADS, 1, 128, 128), lambda n: (0, n, 0, 0)),
                   pl.BlockSpec((GDN_HEADS, p, c, c), lambda n: (0, n, 0, 0))],
        out_shape=[jax.ShapeDtypeStruct((s, GDN_WIDTH), F32), jax.ShapeDtypeStruct((GDN_HEADS, steps, 128, 128), F32),
                   jax.ShapeDtypeStruct((GDN_HEADS, steps * p, c, c), F32)],
        scratch_shapes=[pltpu.VMEM((GDN_HEADS, 128, 128), F32)],
        compiler_params=_cparams(("arbitrary",)))(q, k, v, gb, bb)


def _gdn_bwd(q, k, v, gb, bb, states, inverses, do):
    s = q.shape[0]
    c, p, steps = _gdn_step(s)

    def body(q_ref, k_ref, v_ref, g_ref, b_ref, st_ref, inv_ref, do_ref, dq_ref, dk_ref, dv_ref, dg_ref, db_ref, ds_sc):
        @pl.when(pl.program_id(0) == 0)
        def _():
            ds_sc[...] = jnp.zeros_like(ds_sc)

        s0s = tuple(st_ref[h, 0] for h in range(GDN_HEADS))
        ts = tuple(inv_ref[h, cc] for cc in range(p) for h in range(GDN_HEADS))
        chunks = lambda *args: _gdn_chunk_fn(*args, known_ts=ts)[0]
        _, vjp = jax.vjp(chunks, *[_chain_tiles(ref, c) for ref in (q_ref, k_ref, v_ref, g_ref, b_ref)], s0s)
        *d_tiles, ds0s = vjp((_chain_tiles(do_ref, c), tuple(ds_sc[h] for h in range(GDN_HEADS))))
        for i in range(p * GDN_HEADS):
            cc, h = divmod(i, GDN_HEADS)
            for ref, d in zip((dq_ref, dk_ref, dv_ref, dg_ref, db_ref), d_tiles):
                ref[cc * c:(cc + 1) * c, h * 128:(h + 1) * 128] = d[i]
        for h in range(GDN_HEADS):
            ds_sc[h] = ds0s[h]

    blk = pl.BlockSpec((p * c, GDN_WIDTH), lambda n: (steps - 1 - n, 0))
    return pl.pallas_call(
        body, name="gdn_bwd", grid=(steps,),
        in_specs=[blk] * 5 + [pl.BlockSpec((GDN_HEADS, 1, 128, 128), lambda n: (0, steps - 1 - n, 0, 0)),
                              pl.BlockSpec((GDN_HEADS, p, c, c), lambda n: (0, steps - 1 - n, 0, 0)), blk],
        out_specs=[blk] * 5, out_shape=[jax.ShapeDtypeStruct((s, GDN_WIDTH), F32)] * 5,
        scratch_shapes=[pltpu.VMEM((GDN_HEADS, 128, 128), F32)],
        compiler_params=_cparams(("arbitrary",)))(q, k, v, gb, bb, states, inverses, do)


@jax.custom_vjp
def gdn_core(q, k, v, gb, bb):
    return _gdn_fwd(q, k, v, gb, bb)[0]


def _gdn_core_fwd(q, k, v, gb, bb):
    o, states, inverses = _gdn_fwd(q, k, v, gb, bb)
    return o, (q, k, v, gb, bb, states, inverses)


def _gdn_core_bwd(res, do):
    return tuple(_gdn_bwd(*res, do))


gdn_core.defvjp(_gdn_core_fwd, _gdn_core_bwd)


def _cact_fn(c):
    return (_silu(c),)


def _prenorm_fn(x, w, scale_raw, scale_b, shift_raw, shift_b):
    return (_rms(x, w) * (1.0 + scale_raw + scale_b) + shift_raw + shift_b,)


def _tail_fn(x, o_mla, z_mla, o_gdn, z_gdn, o_norm_w, w_out, post_w, gate_raw, gate_b):
    y_mla = o_mla * _silu(z_mla)
    parts = [_rms(o_gdn[:, h * 128:(h + 1) * 128], o_norm_w) for h in range(GDN_HEADS)]
    y_gdn = jnp.concatenate(parts, axis=1) * _silu(z_gdn)
    y = bdot(y_mla, w_out[:MLA_WIDTH]) + bdot(y_gdn, w_out[MLA_WIDTH:])
    return (x + (gate_raw + gate_b) * _rms(y, post_w),)


def _loss_fn(y, tgt):
    err = y - tgt
    part = jnp.sum(0.5 * jnp.mean(err * err, axis=-1, keepdims=True), axis=0, keepdims=True)
    return (jnp.broadcast_to(part, (1, LANES)),)


def _front_fwd(x, norm_params, ws16):
    s = x.shape[0]
    t = min(512, s)

    def body(x_ref, *refs):
        p_refs, w_refs, o_refs = refs[:len(norm_params)], refs[len(norm_params):-len(ws16)], refs[-len(ws16):]
        (h,) = _prenorm_fn(x_ref[...], *[p[...] for p in p_refs])
        h16 = h.astype(BF16)
        for w_ref, o_ref in zip(w_refs, o_refs):
            o_ref[...] = _dot_raw(h16, w_ref[...], "nn", False)

    rows = lambda d: pl.BlockSpec((t, d), lambda i: (i, 0))
    res = pl.pallas_call(
        body, name="front", grid=(s // t,),
        in_specs=[rows(D_MODEL)] + [_resident(p.shape) for p in norm_params] + [_resident(w.shape) for w in ws16],
        out_specs=[rows(w.shape[1]) for w in ws16],
        out_shape=[jax.ShapeDtypeStruct((s, w.shape[1]), F32) for w in ws16],
        compiler_params=_cparams(("parallel",)))(x, *norm_params, *ws16)
    return tuple(res)


def _front_bwd(x, norm_params, ws16, dys, dx_skip):
    s = x.shape[0]
    t = min(512, s)
    npar, ng = len(norm_params), len(ws16)

    def body(x_ref, *refs):
        p_refs, w_refs, g_refs = refs[:npar], refs[npar:npar + ng], refs[npar + ng:npar + 2 * ng]
        skip_ref, dx_ref = refs[npar + 2 * ng], refs[npar + 2 * ng + 1]
        dp_refs, dw_refs = refs[npar + 2 * ng + 2:2 * npar + 2 * ng + 2], refs[2 * npar + 2 * ng + 2:]

        @pl.when(pl.program_id(0) == 0)
        def _():
            for ref in dp_refs + dw_refs:
                ref[...] = jnp.zeros_like(ref)

        (h,), vjp = jax.vjp(_prenorm_fn, x_ref[...], *[p[...] for p in p_refs])
        h16 = h.astype(BF16)
        dys16 = [g[...].astype(BF16) for g in g_refs]
        dh = functools.reduce(lambda a, b: a + b,
                              [_dot_raw(dy, w_ref[...], "nt", False) for dy, w_ref in zip(dys16, w_refs)])
        for dy, dw_ref in zip(dys16, dw_refs):
            dw_ref[...] += _dot_raw(h16, dy, "tn", False)
        dx, *d_params = vjp((dh,))
        dx_ref[...] = dx + skip_ref[...]
        for ref, val in zip(dp_refs, d_params):
            ref[...] += val

    rows = lambda d: pl.BlockSpec((t, d), lambda i: (i, 0))
    res = pl.pallas_call(
        body, name="front_bwd", grid=(s // t,),
        in_specs=[rows(D_MODEL)] + [_resident(p.shape) for p in norm_params] + [_resident(w.shape) for w in ws16]
        + [rows(w.shape[1]) for w in ws16] + [rows(D_MODEL)],
        out_specs=[rows(D_MODEL)] + [_resident(p.shape) for p in norm_params] + [_resident(w.shape) for w in ws16],
        out_shape=[jax.ShapeDtypeStruct(x.shape, F32)] + [jax.ShapeDtypeStruct(p.shape, F32) for p in norm_params]
        + [jax.ShapeDtypeStruct(w.shape, F32) for w in ws16],
        compiler_params=_cparams(("arbitrary",)))(x, *norm_params, *ws16, *dys, dx_skip)
    return res[0], tuple(res[1:1 + npar]), tuple(res[1 + npar:])


@jax.custom_vjp
def front(x, norm_params, ws):
    return _front_fwd(x, norm_params, tuple(w.astype(BF16) for w in ws)) + (x,)


def _front_vjp_fwd(x, norm_params, ws):
    ws16 = tuple(w.astype(BF16) for w in ws)
    return _front_fwd(x, norm_params, ws16) + (x,), (x, norm_params, ws16)


def _front_vjp_bwd(res, cts):
    return _front_bwd(*res, cts[:-1], cts[-1])


front.defvjp(_front_vjp_fwd, _front_vjp_bwd)


_OPS = dict(
    cact=make_rowwise(_cact_fn, "c_act", (D_MODEL,), tile=16),
    tail=make_rowwise(_tail_fn, "tail", (D_MODEL,), tile=512, bwd_split=2),
    loss=make_rowwise(_loss_fn, "loss", (), acc_dims=(LANES,), n_nondiff=1, tile=512),
)


def _swap_halves(w):
    half = w.shape[-1] // 2
    return jnp.concatenate([w[..., half:], w[..., :half]], axis=-1)


def _w_in_groups(w):
    k_pe = w[:, 640:704]
    ab = jnp.concatenate([w[:, 2752:2760], jnp.zeros((w.shape[0], LANES - 8), w.dtype)], axis=1)
    return (w[:, :384], w[:, 384:640], jnp.concatenate([k_pe, _swap_halves(k_pe)], axis=1), w[:, 704:1216],
            w[:, 1216:2752], ab, w[:, 2760:])


def _q_up_ext(w):
    parts = []
    for h in range(MLA_HEADS):
        rope = w[:, h * 192 + 128:(h + 1) * 192]
        parts += [w[:, h * 192:h * 192 + 128], rope, _swap_halves(rope)]
    return jnp.concatenate(parts, axis=1)


def _kv_up_perm(w):
    ks = [w[:, h * 256:h * 256 + 128] for h in range(MLA_HEADS)]
    vs = [w[:, h * 256 + 128:(h + 1) * 256] for h in range(MLA_HEADS)]
    return jnp.concatenate(ks + vs, axis=1)


def _pad_lanes(v):
    return jnp.pad(v, (0, LANES - v.shape[0]))[None, :]


def _local_loss(weights, mods, x, positions, target):
    s = x.shape[0]
    half = MLA_ROPE // 2
    inv_freq = jnp.power(ROPE_THETA, -jnp.arange(half, dtype=F32) * 2.0 / MLA_ROPE)
    ang = positions.astype(F32)[:, None] * inv_freq
    cos, sin, zero = jnp.cos(ang), jnp.sin(ang), jnp.zeros((s, 2 * half), F32)
    t1 = jnp.concatenate([cos, cos, zero], axis=1)
    t2 = jnp.concatenate([-sin, sin, zero], axis=1)

    for l in range(DEPTH):
        mod = mods[l]
        b = weights["b_mod"][l][None, :]
        shift_raw, scale_raw, gate_raw = mod[:, :1024], mod[:, 1024:2048], mod[:, 2048:]
        shift_b, scale_b, gate_b = b[:, :1024], b[:, 1024:2048], b[:, 2048:]
        q_lat, kv_lat, kr, z_mla, qkv, ab, z_gdn, x = front(
            x, (weights["pre_norm_w"][l][None], scale_raw, scale_b, shift_raw, shift_b),
            _w_in_groups(weights["w_in"][l]))
        o_mla = mla_attention((q_lat, kv_lat, kr, t1, t2),
                              (weights["mla_q_norm_w"][l][None], _q_up_ext(weights["mla_q_up"][l]),
                               weights["mla_kv_norm_w"][l][None], _kv_up_perm(weights["mla_kv_up"][l])))
        cw = weights["gdn_conv_w"][l]
        params = tuple(cw[j][None] for j in range(GDN_CONV))
        params += (_pad_lanes(weights["gdn_a_log"][l]), _pad_lanes(weights["gdn_dt_bias"][l]))
        o_gdn = gdn_core(*gdn_prep(qkv, ab, params))
        (x,) = _OPS["tail"]((x, o_mla, z_mla, o_gdn, z_gdn),
                            (weights["gdn_o_norm_w"][l][None], weights["w_out"][l], weights["post_norm_w"][l][None],
                             gate_raw, gate_b))
    (acc,) = _OPS["loss"]((x, target), ())
    return acc[0, 0]


def _chip_index():
    return 2 * lax.axis_index("x") + lax.axis_index("y")


def _other_chips(x, y):
    return [(1 - x, y), (x, 1 - y), (1 - x, 1 - y)]


def _any_spec():
    return pl.BlockSpec(memory_space=pl.ANY)


def _half(ref, hc):
    n = ref.shape[0] // 2
    return ref.at[pl.ds(hc * n, n)]


def ag_weights(shards):
    n = len(shards)

    def body(*refs):
        ins, outs = refs[:n], refs[n:2 * n]
        send_sems, recv_sems = refs[2 * n:]
        x, y, c = lax.axis_index("x"), lax.axis_index("y"), lax.axis_index("c")
        sibling = (x, y, 1 - c)
        chips = _other_chips(x, y)

        def copy(t, k, src, chip_xy, hc, to):
            return pltpu.make_async_remote_copy(
                src_ref=src, dst_ref=_half(outs[t].at[2 * chip_xy[0] + chip_xy[1]], hc),
                send_sem=send_sems.at[6 * t + k], recv_sem=recv_sems.at[6 * t + k], device_id=to, device_id_type=MESH)

        first = [copy(t, k, _half(ins[t], c), (x, y), c, (*chip, c)) for k, chip in enumerate(chips) for t in range(n)]
        for cp in first:
            cp.start()
        passed = []
        for k, chip in enumerate(chips):
            for t in range(n):
                landed = _half(outs[t].at[2 * chip[0] + chip[1]], c)
                copy(t, k, landed, chip, c, (x, y, c)).wait_recv()
                passed.append(copy(t, 3 + k, landed, chip, c, sibling))
                passed[-1].start()
        for k, chip in enumerate(chips):
            for t in range(n):
                copy(t, 3 + k, _half(ins[t], c), chip, 1 - c, (x, y, c)).wait_recv()
        for cp in first + passed:
            cp.wait_send()

    return pl.pallas_call(
        body, name="ag_weights", in_specs=[_any_spec()] * n, out_specs=[_any_spec()] * n,
        out_shape=[jax.ShapeDtypeStruct((N_CHIPS,) + a.shape, a.dtype) for a in shards],
        scratch_shapes=[pltpu.SemaphoreType.DMA((6 * n,)), pltpu.SemaphoreType.DMA((6 * n,))],
        compiler_params=pltpu.CompilerParams(has_side_effects=True))(*shards)


def all_gather_rows(x):
    r, cols = x.shape
    flips = [(k >> 2 & 1, k >> 1 & 1, k & 1) for k in range(1, 8)]

    def body(x_ref, out_ref, send_sems, recv_sems):
        here = (lax.axis_index("x"), lax.axis_index("y"), lax.axis_index("c"))
        peers = [tuple(1 - p if f else p for p, f in zip(here, flip)) for flip in flips]
        slot = lambda dev: 4 * dev[0] + 2 * dev[1] + dev[2]
        out_ref[slot(here)] = x_ref[...]
        copies = [pltpu.make_async_remote_copy(
            src_ref=x_ref, dst_ref=out_ref.at[slot(here)], send_sem=send_sems.at[k], recv_sem=recv_sems.at[k],
            device_id=peer, device_id_type=MESH) for k, peer in enumerate(peers)]
        for cp in copies:
            cp.start()
        for k, peer in enumerate(peers):
            pltpu.make_async_remote_copy(
                src_ref=x_ref, dst_ref=out_ref.at[slot(peer)], send_sem=send_sems.at[k], recv_sem=recv_sems.at[k],
                device_id=peer, device_id_type=MESH).wait_recv()
        for cp in copies:
            cp.wait_send()

    return pl.pallas_call(
        body, name="all_gather_rows", in_specs=[pl.BlockSpec(memory_space=pltpu.VMEM)],
        out_specs=pl.BlockSpec(memory_space=pltpu.VMEM), out_shape=jax.ShapeDtypeStruct((8, r, cols), F32),
        scratch_shapes=[pltpu.SemaphoreType.DMA((7,)), pltpu.SemaphoreType.DMA((7,))],
        compiler_params=pltpu.CompilerParams(has_side_effects=True, vmem_limit_bytes=VMEM_LIMIT))(x)


def rs_pair(gs):
    n = len(gs)

    def body(*refs):
        ins, outs = refs[:n], refs[n:2 * n]
        send_sems, recv_sems = refs[2 * n:]
        x, y, c = lax.axis_index("x"), lax.axis_index("y"), lax.axis_index("c")
        lh = [g.shape[1] // 2 for g in gs]
        copies = [pltpu.make_async_remote_copy(
            src_ref=ins[t].at[:, pl.ds((1 - c) * lh[t], lh[t])], dst_ref=outs[t], send_sem=send_sems.at[t],
            recv_sem=recv_sems.at[t], device_id=(x, y, 1 - c), device_id_type=MESH) for t in range(n)]
        for cp in copies:
            cp.start()
        for cp in copies:
            cp.wait()

    return pl.pallas_call(
        body, name="rs_pair", in_specs=[_any_spec()] * n, out_specs=[_any_spec()] * n,
        out_shape=[jax.ShapeDtypeStruct((N_CHIPS, g.shape[1] // 2) + g.shape[2:], F32) for g in gs],
        scratch_shapes=[pltpu.SemaphoreType.DMA((n,)), pltpu.SemaphoreType.DMA((n,))],
        compiler_params=pltpu.CompilerParams(has_side_effects=True))(*gs)


def rs_cross(pairs):
    n = len(pairs)

    def body(*refs):
        ins, outs = refs[:n], refs[n:2 * n]
        send_sems, recv_sems = refs[2 * n:]
        x, y, c = lax.axis_index("x"), lax.axis_index("y"), lax.axis_index("c")
        copies = []
        for k, chip in enumerate(_other_chips(x, y)):
            for t in range(n):
                copies.append(pltpu.make_async_remote_copy(
                    src_ref=ins[t].at[2 * chip[0] + chip[1]], dst_ref=outs[t].at[k], send_sem=send_sems.at[3 * t + k],
                    recv_sem=recv_sems.at[3 * t + k], device_id=(*chip, c), device_id_type=MESH))
        for cp in copies:
            cp.start()
        for cp in copies:
            cp.wait()

    return pl.pallas_call(
        body, name="rs_cross", in_specs=[_any_spec()] * n, out_specs=[_any_spec()] * n,
        out_shape=[jax.ShapeDtypeStruct((3,) + p.shape[1:], p.dtype) for p in pairs],
        scratch_shapes=[pltpu.SemaphoreType.DMA((3 * n,)), pltpu.SemaphoreType.DMA((3 * n,))],
        compiler_params=pltpu.CompilerParams(has_side_effects=True))(*pairs)


def rs_share(blocks):
    n = len(blocks)

    def body(*refs):
        ins, outs = refs[:n], refs[n:2 * n]
        send_sems, recv_sems = refs[2 * n:]
        x, y, c = lax.axis_index("x"), lax.axis_index("y"), lax.axis_index("c")
        sends = [pltpu.make_async_remote_copy(
            src_ref=_half(ins[t], c), dst_ref=_half(outs[t], c), send_sem=send_sems.at[t], recv_sem=recv_sems.at[t],
            device_id=(x, y, 1 - c), device_id_type=MESH) for t in range(n)]
        for cp in sends:
            cp.start()
        for t in range(n):
            pltpu.make_async_remote_copy(
                src_ref=_half(ins[t], c), dst_ref=_half(outs[t], 1 - c), send_sem=send_sems.at[t],
                recv_sem=recv_sems.at[t], device_id=(x, y, 1 - c), device_id_type=MESH).wait_recv()
        for cp in sends:
            cp.wait_send()

    return pl.pallas_call(
        body, name="rs_share", in_specs=[_any_spec()] * n, out_specs=[_any_spec()] * n,
        out_shape=[jax.ShapeDtypeStruct(b.shape, F32) for b in blocks],
        input_output_aliases={t: t for t in range(n)},
        scratch_shapes=[pltpu.SemaphoreType.DMA((n,)), pltpu.SemaphoreType.DMA((n,))],
        compiler_params=pltpu.CompilerParams(has_side_effects=True))(*blocks)


def _row_tile(rows, target):
    best = None
    for t in range(8, min(rows, target) + 1, 8):
        if rows % t == 0:
            best = t
    return rows if best is None else best


TILE_BYTES = 2 * 1024 * 1024


def _flat_rows(shape):
    rows = int(np.prod(shape[1:-1]))
    cols_padded = -(-shape[-1] // LANES) * LANES
    return rows, _row_tile(rows, max(8, TILE_BYTES // (4 * cols_padded)))


def pair_add(g, from_sibling, out_dtype):
    cols = g.shape[-1]
    rph, t = _flat_rows(from_sibling.shape)
    nt = rph // t
    c_arr = lax.axis_index("c").astype(jnp.int32).reshape(1)

    def body(c_ref, a_ref, b_ref, o_ref):
        o_ref[...] = (a_ref[...] + b_ref[...]).astype(o_ref.dtype)

    out = pl.pallas_call(
        body, name="pair_add",
        grid_spec=pltpu.PrefetchScalarGridSpec(
            num_scalar_prefetch=1, grid=(N_CHIPS, nt),
            in_specs=[pl.BlockSpec((t, cols), lambda j, i, c_ref: (j * 2 * nt + c_ref[0] * nt + i, 0)),
                      pl.BlockSpec((t, cols), lambda j, i, c_ref: (j * nt + i, 0))],
            out_specs=pl.BlockSpec((t, cols), lambda j, i, c_ref: (j * nt + i, 0))),
        out_shape=jax.ShapeDtypeStruct((N_CHIPS * rph, cols), out_dtype),
        compiler_params=_cparams(("parallel", "parallel")))(c_arr, g.reshape(-1, cols), from_sibling.reshape(-1, cols))
    return out.reshape(from_sibling.shape)


def chip_add(pairs, received):
    cols = pairs.shape[-1]
    rph, t = _flat_rows(pairs.shape)
    nt = rph // t
    j_arr = _chip_index().astype(jnp.int32).reshape(1)
    c_arr = lax.axis_index("c").astype(jnp.int32).reshape(1)
    r2 = received.reshape(-1, cols)

    def body(j_ref, c_ref, a_ref, r0_ref, r1_ref, r2_ref, o_ref):
        a, r0, r1, r2 = [ref[...].astype(F32) for ref in (a_ref, r0_ref, r1_ref, r2_ref)]
        o_ref[...] = (a + r0) + (r1 + r2)

    out = pl.pallas_call(
        body, name="chip_add",
        grid_spec=pltpu.PrefetchScalarGridSpec(
            num_scalar_prefetch=2, grid=(nt,),
            in_specs=[pl.BlockSpec((t, cols), lambda i, j_ref, c_ref: (j_ref[0] * nt + i, 0)),
                      pl.BlockSpec((t, cols), lambda i, j_ref, c_ref: (i, 0)),
                      pl.BlockSpec((t, cols), lambda i, j_ref, c_ref: (nt + i, 0)),
                      pl.BlockSpec((t, cols), lambda i, j_ref, c_ref: (2 * nt + i, 0))],
            out_specs=pl.BlockSpec((t, cols), lambda i, j_ref, c_ref: (c_ref[0] * nt + i, 0))),
        out_shape=jax.ShapeDtypeStruct((2 * rph, cols), F32),
        compiler_params=_cparams(("parallel",)))(j_arr, c_arr, pairs.reshape(-1, cols), r2, r2, r2)
    return out.reshape((2 * pairs.shape[1],) + pairs.shape[2:])


def reduce_grads(gs, cross_dtypes):
    pairs = [pair_add(g, r, dt) for g, r, dt in zip(gs, rs_pair(gs), cross_dtypes)]
    return rs_share([chip_add(p, r) for p, r in zip(pairs, rs_cross(pairs))])


def adamw(w, g, m, v):
    shape = w.shape
    cols = shape[-1]
    rows = int(np.prod(shape[:-1]))
    flat = lambda a: a.reshape(rows, cols)
    t = _row_tile(rows, 512)

    def body(w_ref, g_ref, m_ref, v_ref, d_ref, mo_ref, vo_ref):
        gv = g_ref[...]
        m_new = ADAM_B1 * m_ref[...] + (1.0 - ADAM_B1) * gv
        v_new = ADAM_B2 * v_ref[...] + (1.0 - ADAM_B2) * (gv * gv)
        m_hat = m_new / (1.0 - ADAM_B1 ** ADAM_STEP)
        v_hat = v_new / (1.0 - ADAM_B2 ** ADAM_STEP)
        d_ref[...] = -ADAM_LR * (m_hat / (jnp.sqrt(v_hat) + ADAM_EPS) + ADAM_WD * w_ref[...])
        mo_ref[...] = m_new
        vo_ref[...] = v_new

    spec = pl.BlockSpec((t, cols), lambda i: (i, 0))
    outs = pl.pallas_call(
        body, name="adamw", grid=(rows // t,), in_specs=[spec] * 4, out_specs=[spec] * 3,
        out_shape=[jax.ShapeDtypeStruct((rows, cols), F32)] * 3,
        compiler_params=_cparams(("parallel",)))(flat(w), flat(g), flat(m), flat(v))
    return tuple(o.reshape(shape) for o in outs)


SHARDED = (("w_in", 2), ("mla_q_up", 2), ("mla_kv_up", 2), ("gdn_conv_w", 2), ("w_out", 1))
REPLICATED = ("b_mod", "pre_norm_w", "post_norm_w", "mla_q_norm_w", "mla_kv_norm_w", "gdn_a_log", "gdn_dt_bias",
              "gdn_o_norm_w")
WEIGHT_ORDER = ("w_mod", "b_mod", "pre_norm_w", "post_norm_w", "w_in", "mla_q_norm_w", "mla_q_up", "mla_kv_norm_w",
                "mla_kv_up", "gdn_conv_w", "gdn_a_log", "gdn_dt_bias", "gdn_o_norm_w", "w_out")
EXACT_F32 = ("gdn_conv_w",)
SMALL_ROWS = 48


def _gather_weights(shards):
    names = [name for name, _ in SHARDED]
    own = [shards[n] if n in EXACT_F32 else shards[n].astype(BF16) for n in names]
    gathered = ag_weights(own)
    full = {}
    for (name, axis), blk, mine in zip(SHARDED, gathered, own):
        shp = shards[name].shape
        blk = lax.dynamic_update_index_in_dim(blk, mine, _chip_index(), 0)
        blk = jnp.moveaxis(blk.astype(F32), 0, axis)
        full[name] = blk.reshape(shp[:axis] + (N_CHIPS * shp[axis],) + shp[axis + 1:])
    return full


def _split_grads(grads):
    pieces = []
    for name, axis in SHARDED:
        g = grads[name]
        shp = g.shape
        g = g.reshape(shp[:axis] + (N_CHIPS, shp[axis] // N_CHIPS) + shp[axis + 1:])
        pieces.append(jnp.moveaxis(g, axis, 0))
    small = jnp.concatenate([grads[name] for name in REPLICATED], axis=1)
    small = jnp.pad(small, ((0, 0), (0, SMALL_ROWS * LANES - small.shape[1]))).reshape(DEPTH, SMALL_ROWS, LANES)
    pieces.append(jnp.broadcast_to(small[None], (N_CHIPS,) + small.shape))
    return pieces


def _unsplit_small(small, rep_shapes):
    flat = small.reshape(DEPTH, SMALL_ROWS * LANES)
    out, off = {}, 0
    for name in REPLICATED:
        size = rep_shapes[name][1]
        out[name] = flat[:, off:off + size]
        off += size
    return out


MOD_ROWS = 16


def _device_slot():
    return 4 * lax.axis_index("x") + 2 * lax.axis_index("y") + lax.axis_index("c")


def _adaln_projection(c, w_mod_shard):
    (c_act,) = _OPS["cact"]((jnp.pad(c, ((0, 7), (0, 0))),), ())
    c_acts = jnp.pad(all_gather_rows(c_act)[:, 0, :], ((0, MOD_ROWS - 8), (0, 0)))
    part = jnp.concatenate([_mm(c_acts, w_mod_shard[l].astype(BF16), "nn", "mod_proj") for l in range(DEPTH)], axis=0)
    parts = all_gather_rows(part)[::2].reshape(N_CHIPS, DEPTH, MOD_ROWS, -1)
    mods = jnp.moveaxis(parts, 0, 2).reshape(DEPTH, MOD_ROWS, -1)
    return c_acts, lax.dynamic_slice_in_dim(mods, _device_slot(), 1, axis=1)


def _adaln_weight_grad(c_acts, d_mods):
    cols = d_mods.shape[-1] // N_CHIPS
    rows = jnp.pad(d_mods[:, 0, :], ((0, 8 - DEPTH), (0, 0)))
    all_rows = all_gather_rows(rows)[:, :DEPTH, :]
    mine = lax.dynamic_slice_in_dim(all_rows, _chip_index() * cols, cols, axis=2)
    mine = jnp.pad(mine, ((0, MOD_ROWS - 8), (0, 0), (0, 0)))
    return jnp.stack([_mm(c_acts, mine[:, l, :], "tn", "mod_dw") for l in range(DEPTH)])


def kernel(x, c, positions, w_mod, b_mod, pre_norm_w, post_norm_w, w_in, mla_q_norm_w, mla_q_up, mla_kv_norm_w, mla_kv_up, gdn_conv_w, gdn_a_log, gdn_dt_bias, gdn_o_norm_w, w_out, loss_target, m_w_mod, m_b_mod, m_pre_norm_w, m_post_norm_w, m_w_in, m_mla_q_norm_w, m_mla_q_up, m_mla_kv_norm_w, m_mla_kv_up, m_gdn_conv_w, m_gdn_a_log, m_gdn_dt_bias, m_gdn_o_norm_w, m_w_out, v_w_mod, v_b_mod, v_pre_norm_w, v_post_norm_w, v_w_in, v_mla_q_norm_w, v_mla_q_up, v_mla_kv_norm_w, v_mla_kv_up, v_gdn_conv_w, v_gdn_a_log, v_gdn_dt_bias, v_gdn_o_norm_w, v_w_out):
    given = dict(w_mod=w_mod, b_mod=b_mod, pre_norm_w=pre_norm_w, post_norm_w=post_norm_w, w_in=w_in,
                 mla_q_norm_w=mla_q_norm_w, mla_q_up=mla_q_up, mla_kv_norm_w=mla_kv_norm_w, mla_kv_up=mla_kv_up,
                 gdn_conv_w=gdn_conv_w, gdn_a_log=gdn_a_log, gdn_dt_bias=gdn_dt_bias, gdn_o_norm_w=gdn_o_norm_w,
                 w_out=w_out)
    moments_m = dict(w_mod=m_w_mod, b_mod=m_b_mod, pre_norm_w=m_pre_norm_w, post_norm_w=m_post_norm_w, w_in=m_w_in,
                     mla_q_norm_w=m_mla_q_norm_w, mla_q_up=m_mla_q_up, mla_kv_norm_w=m_mla_kv_norm_w,
                     mla_kv_up=m_mla_kv_up, gdn_conv_w=m_gdn_conv_w, gdn_a_log=m_gdn_a_log,
                     gdn_dt_bias=m_gdn_dt_bias, gdn_o_norm_w=m_gdn_o_norm_w, w_out=m_w_out)
    moments_v = dict(w_mod=v_w_mod, b_mod=v_b_mod, pre_norm_w=v_pre_norm_w, post_norm_w=v_post_norm_w, w_in=v_w_in,
                     mla_q_norm_w=v_mla_q_norm_w, mla_q_up=v_mla_q_up, mla_kv_norm_w=v_mla_kv_norm_w,
                     mla_kv_up=v_mla_kv_up, gdn_conv_w=v_gdn_conv_w, gdn_a_log=v_gdn_a_log,
                     gdn_dt_bias=v_gdn_dt_bias, gdn_o_norm_w=v_gdn_o_norm_w, w_out=v_w_out)

    full = _gather_weights({name: given[name] for name, _ in SHARDED})
    for name in REPLICATED:
        full[name] = given[name]
    c_acts, mods = _adaln_projection(c, w_mod)
    loss_local, (grads, d_mods, grad_x) = jax.value_and_grad(_local_loss, argnums=(0, 1, 2))(
        full, mods, x[0], positions[0], loss_target[0])
    loss = lax.psum(loss_local, AXES)

    cross_dtypes = [F32 if name in EXACT_F32 else BF16 for name, _ in SHARDED] + [F32]
    reduced = reduce_grads(_split_grads(grads), cross_dtypes)
    grad_w = {name: g for (name, _), g in zip(SHARDED, reduced)}
    grad_w.update(_unsplit_small(reduced[-1], {name: given[name].shape for name in REPLICATED}))
    grad_w["w_mod"] = _adaln_weight_grad(c_acts, d_mods)
    delta, new_m, new_v = {}, {}, {}
    for name in WEIGHT_ORDER:
        delta[name], new_m[name], new_v[name] = adamw(given[name], grad_w[name], moments_m[name], moments_v[name])
    return (loss, grad_x[None], *[grad_w[n] for n in WEIGHT_ORDER], *[delta[n] for n in WEIGHT_ORDER],
            *[new_m[n] for n in WEIGHT_ORDER], *[new_v[n] for n in WEIGHT_ORDER])
```
